```python
import math
import jax, jax.numpy as jnp
from jax import lax
import numpy as np

D_MODEL = 1024
BATCH = 8
SEQ = 4096
DEPTH = 1

CHUNK = 64
Q_BLOCK = 128
D_MIX = D_MODEL
MLA_HEADS = 8
NOPE_DIM = 64
ROPE_DIM = 32
V_DIM = 64
Q_LORA = 256
KV_LORA = 128
ROPE_THETA = 10000.0
MLA_WIDTH = MLA_HEADS * V_DIM
ATTN_SCALE = (NOPE_DIM + ROPE_DIM) ** -0.5
GM_HEADS = 8
GM_DIM = 64
GM_CHUNK = 128
GM_WIDTH = GM_HEADS * GM_DIM
IN_COLS = Q_LORA + KV_LORA + ROPE_DIM + 2 * GM_WIDTH
D_FF = 2816
CONV_W = 3
EPS = 1e-6

kernel_name = "hybrid_mla_gmlp_convffn_block"


def rms_norm(x, g):
    x32 = x.astype(jnp.float32)
    y = x32 * lax.rsqrt(jnp.mean(x32 * x32, axis=-1, keepdims=True) + EPS)
    return (y * g.astype(jnp.float32)).astype(x.dtype)


def layer_norm(x, g, b):
    x32 = x.astype(jnp.float32)
    mu = jnp.mean(x32, axis=-1, keepdims=True)
    var = jnp.mean(jnp.square(x32 - mu), axis=-1, keepdims=True)
    y = (x32 - mu) * lax.rsqrt(var + EPS)
    return (y * g.astype(jnp.float32) + b.astype(jnp.float32)).astype(x.dtype)


def modulate(h, shift, scale):
    return h * (1.0 + scale[:, None, :]) + shift[:, None, :]


def rope_tables(seq):
    pos = jnp.arange(seq, dtype=jnp.float32)
    inv = ROPE_THETA ** (-jnp.arange(0, ROPE_DIM, 2, dtype=jnp.float32) / ROPE_DIM)
    ang = pos[:, None] * inv[None, :]
    return jnp.cos(ang), jnp.sin(ang)


def apply_rope(t, cos, sin):
    t32 = t.astype(jnp.float32)
    t1, t2 = jnp.split(t32, 2, axis=-1)
    out = jnp.concatenate([t1 * cos - t2 * sin, t2 * cos + t1 * sin], axis=-1)
    return out.astype(t.dtype)


def mla_attention(q_nope, q_rope, k_nope, k_rope, v):
    B, S, H, _ = q_nope.shape
    nb = S // Q_BLOCK
    k_chunk = jnp.arange(S) // CHUNK

    def to_blocks(t):
        return t.reshape((B, nb, Q_BLOCK) + t.shape[2:]).swapaxes(0, 1)

    def one_block(args):
        qn, qr, bi = args
        s = (jnp.einsum('bqhd,bkhd->bhqk', qn, k_nope)
             + jnp.einsum('bqhr,bkr->bhqk', qr, k_rope))
        s = s.astype(jnp.float32) * ATTN_SCALE
        q_chunk = (bi * Q_BLOCK + jnp.arange(Q_BLOCK)) // CHUNK
        mask = k_chunk[None, :] <= q_chunk[:, None]
        s = jnp.where(mask[None, None], s, -jnp.inf)
        p = jax.nn.softmax(s, axis=-1).astype(v.dtype)
        return jnp.einsum('bhqk,bkhd->bqhd', p, v)

    o = lax.map(one_block, (to_blocks(q_nope), to_blocks(q_rope), jnp.arange(nb)))
    return o.swapaxes(0, 1).reshape(B, S, H * V_DIM)


def gmlp_spatial_gate(u, v, ln_g, ln_b, w_s, b_s):
    B, S, H, Dh = v.shape
    v = layer_norm(v, ln_g, ln_b)
    idx = jnp.arange(GM_CHUNK) // CHUNK
    mask = (idx[None, :] <= idx[:, None]).astype(w_s.dtype)
    w_m = w_s * mask[None]
    vb = v.reshape(B, S // GM_CHUNK, GM_CHUNK, H, Dh)
    mixed = jnp.einsum('hij,bnjhd->bnihd', w_m, vb) + b_s.T[None, None, :, :, None]
    return u * mixed.reshape(B, S, H, Dh)


def token_mixer(h, w_in, g_q, w_uq, g_kv, w_ukv, gm_ln_g, gm_ln_b, w_spatial, b_spatial, w_out):
    B, S, _ = h.shape
    z = h @ w_in
    o1 = Q_LORA
    o2 = o1 + KV_LORA
    o3 = o2 + ROPE_DIM
    c_q, c_kv, k_r, g_uv = z[..., :o1], z[..., o1:o2], z[..., o2:o3], z[..., o3:]
    q = (rms_norm(c_q, g_q) @ w_uq).reshape(B, S, MLA_HEADS, NOPE_DIM + ROPE_DIM)
    kv = (rms_norm(c_kv, g_kv) @ w_ukv).reshape(B, S, MLA_HEADS, NOPE_DIM + V_DIM)
    q_nope, q_rope = q[..., :NOPE_DIM], q[..., NOPE_DIM:]
    k_nope, v = kv[..., :NOPE_DIM], kv[..., NOPE_DIM:]
    cos, sin = rope_tables(S)
    q_rope = apply_rope(q_rope, cos[None, :, None, :], sin[None, :, None, :])
    k_rope = apply_rope(k_r, cos[None], sin[None])
    attn = mla_attention(q_nope, q_rope, k_nope, k_rope, v)
    g = jax.nn.gelu(g_uv)
    u = g[..., :GM_WIDTH].reshape(B, S, GM_HEADS, GM_DIM)
    vv = g[..., GM_WIDTH:].reshape(B, S, GM_HEADS, GM_DIM)
    sgu = gmlp_spatial_gate(u, vv, gm_ln_g, gm_ln_b, w_spatial, b_spatial).reshape(B, S, GM_WIDTH)
    return jnp.concatenate([attn, sgu], axis=-1) @ w_out


def conv_ffn(h, w_up, conv_w, conv_b, w_down):
    S = h.shape[1]
    up = h @ w_up
    upp = jnp.pad(up, ((0, 0), (CONV_W - 1, 0), (0, 0)))
    y = conv_b + sum(upp[:, k:k + S, :] * conv_w[k] for k in range(CONV_W))
    a, b = y[..., :D_FF], y[..., D_FF:]
    return (jax.nn.silu(a) * b) @ w_down


def _fwd_setup_inputs(seed: int = 0) -> dict:
    key = jax.random.key(seed)
    ks = jax.random.split(key, 24)
    f32 = jnp.float32
    n = lambda k, shape, s: jax.random.normal(k, shape, f32) * s
    gain = lambda k, shape: 1.0 + 0.05 * jax.random.normal(k, shape, f32)
    L = DEPTH
    return {
        "x": jax.random.normal(ks[0], (BATCH, SEQ, D_MODEL), f32),
        "c": jax.random.normal(ks[1], (BATCH, D_MODEL), f32),
        "w_ada": n(ks[2], (L, D_MODEL, 6 * D_MODEL), 0.5 * D_MODEL ** -0.5),
        "b_ada": n(ks[3], (L, 6 * D_MODEL), 0.02),
        "g_pre_mix": gain(ks[4], (L, D_MODEL)),
        "g_post_mix": gain(ks[5], (L, D_MODEL)),
        "w_in": n(ks[6], (L, D_MODEL, IN_COLS), D_MODEL ** -0.5),
        "g_q": gain(ks[7], (L, Q_LORA)),
        "w_uq": n(ks[8], (L, Q_LORA, MLA_HEADS * (NOPE_DIM + ROPE_DIM)), Q_LORA ** -0.5),
        "g_kv": gain(ks[9], (L, KV_LORA)),
        "w_ukv": n(ks[10], (L, KV_LORA, MLA_HEADS * (NOPE_DIM + V_DIM)), KV_LORA ** -0.5),
        "gm_ln_g": gain(ks[11], (L, GM_HEADS, GM_DIM)),
        "gm_ln_b": n(ks[12], (L, GM_HEADS, GM_DIM), 0.02),
        "w_spatial": n(ks[13], (L, GM_HEADS, GM_CHUNK, GM_CHUNK), GM_CHUNK ** -0.5),
        "b_spatial": 1.0 + n(ks[14], (L, GM_HEADS, GM_CHUNK), 0.05),
        "w_out": n(ks[15], (L, D_MIX, D_MODEL), D_MIX ** -0.5),
        "g_pre_ffn": gain(ks[16], (L, D_MODEL)),
        "g_post_ffn": gain(ks[17], (L, D_MODEL)),
        "w_up": n(ks[18], (L, D_MODEL, 2 * D_FF), D_MODEL ** -0.5),
        "conv_w": n(ks[19], (L, CONV_W, 2 * D_FF), CONV_W ** -0.5),
        "conv_b": n(ks[20], (L, 2 * D_FF), 0.02),
        "w_down": n(ks[21], (L, D_FF, D_MODEL), D_FF ** -0.5),
    }


def _fwd_reference(x, c, w_ada, b_ada, g_pre_mix, g_post_mix, w_in, g_q, w_uq, g_kv, w_ukv,
              gm_ln_g, gm_ln_b, w_spatial, b_spatial, w_out, g_pre_ffn, g_post_ffn,
              w_up, conv_w, conv_b, w_down):
    c_act = jax.nn.silu(c)
    for l in range(DEPTH):
        ada = c_act @ w_ada[l] + b_ada[l]
        sh1, sc1, gt1, sh2, sc2, gt2 = jnp.split(ada, 6, axis=-1)
        h = modulate(rms_norm(x, g_pre_mix[l]), sh1, sc1)
        m = token_mixer(h, w_in[l], g_q[l], w_uq[l], g_kv[l], w_ukv[l], gm_ln_g[l], gm_ln_b[l],
                        w_spatial[l], b_spatial[l], w_out[l])
        x = x + gt1[:, None, :] * rms_norm(m, g_post_mix[l])
        h = modulate(rms_norm(x, g_pre_ffn[l]), sh2, sc2)
        f = conv_ffn(h, w_up[l], conv_w[l], conv_b[l], w_down[l])
        x = x + gt2[:, None, :] * rms_norm(f, g_post_ffn[l])
    return x


import jax as _jax
import jax.numpy as _jnp

TWIN_FORMAT = 'train_step'
FWD_PARAMS = ['x', 'c', 'w_ada', 'b_ada', 'g_pre_mix', 'g_post_mix', 'w_in', 'g_q', 'w_uq', 'g_kv', 'w_ukv', 'gm_ln_g', 'gm_ln_b', 'w_spatial', 'b_spatial', 'w_out', 'g_pre_ffn', 'g_post_ffn', 'w_up', 'conv_w', 'conv_b', 'w_down']
TWIN_WEIGHTS = ['w_ada', 'b_ada', 'g_pre_mix', 'g_post_mix', 'w_in', 'g_q', 'w_uq', 'g_kv', 'w_ukv', 'gm_ln_g', 'gm_ln_b', 'w_spatial', 'b_spatial', 'w_out', 'g_pre_ffn', 'g_post_ffn', 'w_up', 'conv_w', 'conv_b', 'w_down']
TWIN_DIFF_INPUT = 'x'
TWIN_INPUTS = ['x', 'c', 'w_ada', 'b_ada', 'g_pre_mix', 'g_post_mix', 'w_in', 'g_q', 'w_uq', 'g_kv', 'w_ukv', 'gm_ln_g', 'gm_ln_b', 'w_spatial', 'b_spatial', 'w_out', 'g_pre_ffn', 'g_post_ffn', 'w_up', 'conv_w', 'conv_b', 'w_down', 'loss_target', 'm_w_ada', 'm_b_ada', 'm_g_pre_mix', 'm_g_post_mix', 'm_w_in', 'm_g_q', 'm_w_uq', 'm_g_kv', 'm_w_ukv', 'm_gm_ln_g', 'm_gm_ln_b', 'm_w_spatial', 'm_b_spatial', 'm_w_out', 'm_g_pre_ffn', 'm_g_post_ffn', 'm_w_up', 'm_conv_w', 'm_conv_b', 'm_w_down', 'v_w_ada', 'v_b_ada', 'v_g_pre_mix', 'v_g_post_mix', 'v_w_in', 'v_g_q', 'v_w_uq', 'v_g_kv', 'v_w_ukv', 'v_gm_ln_g', 'v_gm_ln_b', 'v_w_spatial', 'v_b_spatial', 'v_w_out', 'v_g_pre_ffn', 'v_g_post_ffn', 'v_w_up', 'v_conv_w', 'v_conv_b', 'v_w_down']
TWIN_OUTPUTS = ['loss', 'grad_x', 'grad_w_ada', 'grad_b_ada', 'grad_g_pre_mix', 'grad_g_post_mix', 'grad_w_in', 'grad_g_q', 'grad_w_uq', 'grad_g_kv', 'grad_w_ukv', 'grad_gm_ln_g', 'grad_gm_ln_b', 'grad_w_spatial', 'grad_b_spatial', 'grad_w_out', 'grad_g_pre_ffn', 'grad_g_post_ffn', 'grad_w_up', 'grad_conv_w', 'grad_conv_b', 'grad_w_down', 'delta_w_ada', 'delta_b_ada', 'delta_g_pre_mix', 'delta_g_post_mix', 'delta_w_in', 'delta_g_q', 'delta_w_uq', 'delta_g_kv', 'delta_w_ukv', 'delta_gm_ln_g', 'delta_gm_ln_b', 'delta_w_spatial', 'delta_b_spatial', 'delta_w_out', 'delta_g_pre_ffn', 'delta_g_post_ffn', 'delta_w_up', 'delta_conv_w', 'delta_conv_b', 'delta_w_down', 'new_m_w_ada', 'new_m_b_ada', 'new_m_g_pre_mix', 'new_m_g_post_mix', 'new_m_w_in', 'new_m_g_q', 'new_m_w_uq', 'new_m_g_kv', 'new_m_w_ukv', 'new_m_gm_ln_g', 'new_m_gm_ln_b', 'new_m_w_spatial', 'new_m_b_spatial', 'new_m_w_out', 'new_m_g_pre_ffn', 'new_m_g_post_ffn', 'new_m_w_up', 'new_m_conv_w', 'new_m_conv_b', 'new_m_w_down', 'new_v_w_ada', 'new_v_b_ada', 'new_v_g_pre_mix', 'new_v_g_post_mix', 'new_v_w_in', 'new_v_g_q', 'new_v_w_uq', 'new_v_g_kv', 'new_v_w_ukv', 'new_v_gm_ln_g', 'new_v_gm_ln_b', 'new_v_w_spatial', 'new_v_b_spatial', 'new_v_w_out', 'new_v_g_pre_ffn', 'new_v_g_post_ffn', 'new_v_w_up', 'new_v_conv_w', 'new_v_conv_b', 'new_v_w_down']
TWIN_LEAF_KINDS = {'loss': 'loss', 'grad_x': 'grad_x', 'grad_w_ada': 'grad_w', 'grad_b_ada': 'grad_w', 'grad_g_pre_mix': 'grad_w', 'grad_g_post_mix': 'grad_w', 'grad_w_in': 'grad_w', 'grad_g_q': 'grad_w', 'grad_w_uq': 'grad_w', 'grad_g_kv': 'grad_w', 'grad_w_ukv': 'grad_w', 'grad_gm_ln_g': 'grad_w', 'grad_gm_ln_b': 'grad_w', 'grad_w_spatial': 'grad_w', 'grad_b_spatial': 'grad_w', 'grad_w_out': 'grad_w', 'grad_g_pre_ffn': 'grad_w', 'grad_g_post_ffn': 'grad_w', 'grad_w_up': 'grad_w', 'grad_conv_w': 'grad_w', 'grad_conv_b': 'grad_w', 'grad_w_down': 'grad_w', 'delta_w_ada': 'delta_w', 'delta_b_ada': 'delta_w', 'delta_g_pre_mix': 'delta_w', 'delta_g_post_mix': 'delta_w', 'delta_w_in': 'delta_w', 'delta_g_q': 'delta_w', 'delta_w_uq': 'delta_w', 'delta_g_kv': 'delta_w', 'delta_w_ukv': 'delta_w', 'delta_gm_ln_g': 'delta_w', 'delta_gm_ln_b': 'delta_w', 'delta_w_spatial': 'delta_w', 'delta_b_spatial': 'delta_w', 'delta_w_out': 'delta_w', 'delta_g_pre_ffn': 'delta_w', 'delta_g_post_ffn': 'delta_w', 'delta_w_up': 'delta_w', 'delta_conv_w': 'delta_w', 'delta_conv_b': 'delta_w', 'delta_w_down': 'delta_w', 'new_m_w_ada': 'new_m', 'new_m_b_ada': 'new_m', 'new_m_g_pre_mix': 'new_m', 'new_m_g_post_mix': 'new_m', 'new_m_w_in': 'new_m', 'new_m_g_q': 'new_m', 'new_m_w_uq': 'new_m', 'new_m_g_kv': 'new_m', 'new_m_w_ukv': 'new_m', 'new_m_gm_ln_g': 'new_m', 'new_m_gm_ln_b': 'new_m', 'new_m_w_spatial': 'new_m', 'new_m_b_spatial': 'new_m', 'new_m_w_out': 'new_m', 'new_m_g_pre_ffn': 'new_m', 'new_m_g_post_ffn': 'new_m', 'new_m_w_up': 'new_m', 'new_m_conv_w': 'new_m', 'new_m_conv_b': 'new_m', 'new_m_w_down': 'new_m', 'new_v_w_ada': 'new_v', 'new_v_b_ada': 'new_v', 'new_v_g_pre_mix': 'new_v', 'new_v_g_post_mix': 'new_v', 'new_v_w_in': 'new_v', 'new_v_g_q': 'new_v', 'new_v_w_uq': 'new_v', 'new_v_g_kv': 'new_v', 'new_v_w_ukv': 'new_v', 'new_v_gm_ln_g': 'new_v', 'new_v_gm_ln_b': 'new_v', 'new_v_w_spatial': 'new_v', 'new_v_b_spatial': 'new_v', 'new_v_w_out': 'new_v', 'new_v_g_pre_ffn': 'new_v', 'new_v_g_post_ffn': 'new_v', 'new_v_w_up': 'new_v', 'new_v_conv_w': 'new_v', 'new_v_conv_b': 'new_v', 'new_v_w_down': 'new_v'}


def _forward(args):
    return _fwd_reference(*[args[k] for k in FWD_PARAMS])


def _output_shape():
    out = _jax.eval_shape(lambda: _forward(_fwd_setup_inputs(0)))
    return out.shape, out.dtype

N_MICROBATCH = 1
ADAM_LR = 0.001
ADAM_B1 = 0.9
ADAM_B2 = 0.999
ADAM_EPS = 1e-08
ADAM_WD = 0.01
ADAM_STEP = 10
PER_EXAMPLE_BATCH_AXIS = {'x': 0, 'c': 0, 'loss_target': 0}
SHARED_INPUTS = []
_WEIGHT_DTYPES = {'w_ada': _jnp.float32, 'b_ada': _jnp.float32, 'g_pre_mix': _jnp.float32, 'g_post_mix': _jnp.float32, 'w_in': _jnp.float32, 'g_q': _jnp.float32, 'w_uq': _jnp.float32, 'g_kv': _jnp.float32, 'w_ukv': _jnp.float32, 'gm_ln_g': _jnp.float32, 'gm_ln_b': _jnp.float32, 'w_spatial': _jnp.float32, 'b_spatial': _jnp.float32, 'w_out': _jnp.float32, 'g_pre_ffn': _jnp.float32, 'g_post_ffn': _jnp.float32, 'w_up': _jnp.float32, 'conv_w': _jnp.float32, 'conv_b': _jnp.float32, 'w_down': _jnp.float32}
MOMENT_SCALE = {'w_ada': 1.793982e+00, 'b_ada': 3.391077e+00, 'g_pre_mix': 1.163368e-01, 'g_post_mix': 3.729448e+00, 'w_in': 1.938294e-01, 'g_q': 3.193418e-02, 'w_uq': 1.706119e-02, 'g_kv': 5.991636e-01, 'w_ukv': 2.089527e-01, 'gm_ln_g': 9.359997e-02, 'gm_ln_b': 7.788040e-02, 'w_spatial': 5.525440e-02, 'b_spatial': 7.010192e-02, 'w_out': 3.488549e-01, 'g_pre_ffn': 1.271513e-01, 'g_post_ffn': 3.851990e+00, 'w_up': 6.112379e-02, 'conv_w': 6.636353e-02, 'conv_b': 1.181456e-01, 'w_down': 1.198699e-01}


def _to_microbatches(a, axis):
    t = _jnp.moveaxis(a, axis, 0)
    t = t.reshape((N_MICROBATCH, t.shape[0] // N_MICROBATCH) + t.shape[1:])
    return _jnp.moveaxis(t, 1, axis + 1)


def setup_inputs(seed: int = 0) -> dict:
    inp = _fwd_setup_inputs(seed)
    key = _jax.random.fold_in(_jax.random.key(seed), 7919)
    shape, _ = _output_shape()
    out = dict(inp)
    out["loss_target"] = _jax.random.normal(_jax.random.fold_in(key, 0), shape, _jnp.float32)
    for i, name in enumerate(TWIN_WEIGHTS):
        w = inp[name].astype(_jnp.float32)
        if MOMENT_SCALE is None:
            s = _jnp.sqrt(_jnp.mean(_jnp.square(w)) + 1e-30)
        else:
            s = MOMENT_SCALE[name]
        km, kv = _jax.random.split(_jax.random.fold_in(key, i + 1))
        out[name] = w
        out["m_" + name] = s * _jax.random.normal(km, w.shape, _jnp.float32)
        out["v_" + name] = (s * s) * _jax.random.uniform(kv, w.shape, _jnp.float32, 0.5, 1.5)
    if N_MICROBATCH > 1:
        for name, axis in PER_EXAMPLE_BATCH_AXIS.items():
            out[name] = _to_microbatches(out[name], axis)
    return {'x': out['x'], 'c': out['c'], 'w_ada': out['w_ada'], 'b_ada': out['b_ada'], 'g_pre_mix': out['g_pre_mix'], 'g_post_mix': out['g_post_mix'], 'w_in': out['w_in'], 'g_q': out['g_q'], 'w_uq': out['w_uq'], 'g_kv': out['g_kv'], 'w_ukv': out['w_ukv'], 'gm_ln_g': out['gm_ln_g'], 'gm_ln_b': out['gm_ln_b'], 'w_spatial': out['w_spatial'], 'b_spatial': out['b_spatial'], 'w_out': out['w_out'], 'g_pre_ffn': out['g_pre_ffn'], 'g_post_ffn': out['g_post_ffn'], 'w_up': out['w_up'], 'conv_w': out['conv_w'], 'conv_b': out['conv_b'], 'w_down': out['w_down'], 'loss_target': out['loss_target'], 'm_w_ada': out['m_w_ada'], 'm_b_ada': out['m_b_ada'], 'm_g_pre_mix': out['m_g_pre_mix'], 'm_g_post_mix': out['m_g_post_mix'], 'm_w_in': out['m_w_in'], 'm_g_q': out['m_g_q'], 'm_w_uq': out['m_w_uq'], 'm_g_kv': out['m_g_kv'], 'm_w_ukv': out['m_w_ukv'], 'm_gm_ln_g': out['m_gm_ln_g'], 'm_gm_ln_b': out['m_gm_ln_b'], 'm_w_spatial': out['m_w_spatial'], 'm_b_spatial': out['m_b_spatial'], 'm_w_out': out['m_w_out'], 'm_g_pre_ffn': out['m_g_pre_ffn'], 'm_g_post_ffn': out['m_g_post_ffn'], 'm_w_up': out['m_w_up'], 'm_conv_w': out['m_conv_w'], 'm_conv_b': out['m_conv_b'], 'm_w_down': out['m_w_down'], 'v_w_ada': out['v_w_ada'], 'v_b_ada': out['v_b_ada'], 'v_g_pre_mix': out['v_g_pre_mix'], 'v_g_post_mix': out['v_g_post_mix'], 'v_w_in': out['v_w_in'], 'v_g_q': out['v_g_q'], 'v_w_uq': out['v_w_uq'], 'v_g_kv': out['v_g_kv'], 'v_w_ukv': out['v_w_ukv'], 'v_gm_ln_g': out['v_gm_ln_g'], 'v_gm_ln_b': out['v_gm_ln_b'], 'v_w_spatial': out['v_w_spatial'], 'v_b_spatial': out['v_b_spatial'], 'v_w_out': out['v_w_out'], 'v_g_pre_ffn': out['v_g_pre_ffn'], 'v_g_post_ffn': out['v_g_post_ffn'], 'v_w_up': out['v_w_up'], 'v_conv_w': out['v_conv_w'], 'v_conv_b': out['v_conv_b'], 'v_w_down': out['v_w_down']}


def _loss(weights, diff, rest, loss_target):
    with _jax.named_scope("forward"):
        args = {**rest, TWIN_DIFF_INPUT: diff, **{k: w.astype(_WEIGHT_DTYPES[k]) for k, w in weights.items()}}
        y = _forward(args)
    with _jax.named_scope("loss_head"):
        err = _jnp.square(y.astype(_jnp.float32) - loss_target)
        return 0.5 * _jnp.sum(_jnp.mean(err, axis=-1)) if err.ndim else 0.5 * err


def _adamw(w, g, m, v):
    m = ADAM_B1 * m + (1.0 - ADAM_B1) * g
    v = ADAM_B2 * v + (1.0 - ADAM_B2) * _jnp.square(g)
    m_hat = m / (1.0 - ADAM_B1 ** ADAM_STEP)
    v_hat = v / (1.0 - ADAM_B2 ** ADAM_STEP)
    delta = -ADAM_LR * (m_hat / (_jnp.sqrt(v_hat) + ADAM_EPS) + ADAM_WD * w)
    return delta, m, v


def reference(x, c, w_ada, b_ada, g_pre_mix, g_post_mix, w_in, g_q, w_uq, g_kv, w_ukv, gm_ln_g, gm_ln_b, w_spatial, b_spatial, w_out, g_pre_ffn, g_post_ffn, w_up, conv_w, conv_b, w_down, loss_target, m_w_ada, m_b_ada, m_g_pre_mix, m_g_post_mix, m_w_in, m_g_q, m_w_uq, m_g_kv, m_w_ukv, m_gm_ln_g, m_gm_ln_b, m_w_spatial, m_b_spatial, m_w_out, m_g_pre_ffn, m_g_post_ffn, m_w_up, m_conv_w, m_conv_b, m_w_down, v_w_ada, v_b_ada, v_g_pre_mix, v_g_post_mix, v_w_in, v_g_q, v_w_uq, v_g_kv, v_w_ukv, v_gm_ln_g, v_gm_ln_b, v_w_spatial, v_b_spatial, v_w_out, v_g_pre_ffn, v_g_post_ffn, v_w_up, v_conv_w, v_conv_b, v_w_down):
    given = dict(x=x, c=c, w_ada=w_ada, b_ada=b_ada, g_pre_mix=g_pre_mix, g_post_mix=g_post_mix, w_in=w_in, g_q=g_q, w_uq=w_uq, g_kv=g_kv, w_ukv=w_ukv, gm_ln_g=gm_ln_g, gm_ln_b=gm_ln_b, w_spatial=w_spatial, b_spatial=b_spatial, w_out=w_out, g_pre_ffn=g_pre_ffn, g_post_ffn=g_post_ffn, w_up=w_up, conv_w=conv_w, conv_b=conv_b, w_down=w_down, loss_target=loss_target, m_w_ada=m_w_ada, m_b_ada=m_b_ada, m_g_pre_mix=m_g_pre_mix, m_g_post_mix=m_g_post_mix, m_w_in=m_w_in, m_g_q=m_g_q, m_w_uq=m_w_uq, m_g_kv=m_g_kv, m_w_ukv=m_w_ukv, m_gm_ln_g=m_gm_ln_g, m_gm_ln_b=m_gm_ln_b, m_w_spatial=m_w_spatial, m_b_spatial=m_b_spatial, m_w_out=m_w_out, m_g_pre_ffn=m_g_pre_ffn, m_g_post_ffn=m_g_post_ffn, m_w_up=m_w_up, m_conv_w=m_conv_w, m_conv_b=m_conv_b, m_w_down=m_w_down, v_w_ada=v_w_ada, v_b_ada=v_b_ada, v_g_pre_mix=v_g_pre_mix, v_g_post_mix=v_g_post_mix, v_w_in=v_w_in, v_g_q=v_g_q, v_w_uq=v_w_uq, v_g_kv=v_g_kv, v_w_ukv=v_w_ukv, v_gm_ln_g=v_gm_ln_g, v_gm_ln_b=v_gm_ln_b, v_w_spatial=v_w_spatial, v_b_spatial=v_b_spatial, v_w_out=v_w_out, v_g_pre_ffn=v_g_pre_ffn, v_g_post_ffn=v_g_post_ffn, v_w_up=v_w_up, v_conv_w=v_conv_w, v_conv_b=v_conv_b, v_w_down=v_w_down)
    weights = {n: given[n] for n in TWIN_WEIGHTS}
    shared = {n: given[n] for n in SHARED_INPUTS}
    per_example = {n: given[n] for n in ['x', 'c']}
    grad_fn = _jax.value_and_grad(_loss, argnums=(0, 1))

    def one_microbatch(ex, loss_target):
        ex = dict(ex)
        diff = ex.pop(TWIN_DIFF_INPUT)
        return grad_fn(weights, diff, {**shared, **ex}, loss_target)

    if N_MICROBATCH == 1:
        loss, (grad_w, grad_x) = one_microbatch(per_example, given["loss_target"])
    else:
        def body(carry, xs):
            loss_sum, grad_sum = carry
            l_k, (gw_k, gx_k) = one_microbatch(xs[0], xs[1])
            with _jax.named_scope("update"):
                return (loss_sum + l_k, _jax.tree.map(_jnp.add, grad_sum, gw_k)), gx_k

        init = (_jnp.zeros((), _jnp.float32), _jax.tree.map(_jnp.zeros_like, weights))
        (loss, grad_w), grad_x = _jax.lax.scan(body, init, (per_example, given["loss_target"]))
    with _jax.named_scope("update"):
        delta_w, new_m, new_v = {}, {}, {}
        for n in TWIN_WEIGHTS:
            delta_w[n], new_m[n], new_v[n] = _adamw(weights[n], grad_w[n], given["m_" + n], given["v_" + n])
    return (loss, grad_x, *[grad_w[n] for n in TWIN_WEIGHTS], *[delta_w[n] for n in TWIN_WEIGHTS],
            *[new_m[n] for n in TWIN_WEIGHTS], *[new_v[n] for n in TWIN_WEIGHTS])
```

```python
import functools

import jax
import jax.numpy as jnp
from jax import lax
from jax.experimental import pallas as pl
from jax.experimental.pallas import tpu as pltpu

F32 = jnp.float32
MXU = jnp.bfloat16

N_DEV = 8
D_MODEL = 1024
HEADS = 8
HEAD_PAD = 128
NOPE = 64
ROPE = 32
Q_LORA = 256
KV_LORA = 128
GM_WIDTH = 512
GM_DIM = 64
GM_CHUNK = 128
CHUNK_SHIFT = 6
ROPE_THETA = 10000.0
ATTN_SCALE = (NOPE + ROPE) ** -0.5
Z_COLS = 1536
FF_BLK = 704
EPS = 1e-6
ADAM_LR = 0.001
ADAM_B1 = 0.9
ADAM_B2 = 0.999
ADAM_EPS = 1e-08
ADAM_WD = 0.01
ADAM_STEP = 10
VMEM_LIMIT = 56 * 1024 * 1024
MESH = pl.DeviceIdType.MESH


def _dot(a, b):
    return jnp.dot(a, b, preferred_element_type=F32)


def _dot_nt(a, b):
    return lax.dot_general(a, b, (((1,), (1,)), ((), ())), preferred_element_type=F32)


def _dot_tn(a, b):
    return lax.dot_general(a, b, (((0,), (0,)), ((), ())), preferred_element_type=F32)


def _call(body, *, name, grid, in_specs, out_specs, out_shape, scratch=(), sem=None):
    params = pltpu.CompilerParams(dimension_semantics=sem, vmem_limit_bytes=VMEM_LIMIT)
    return pl.pallas_call(body, name=name, grid=grid, in_specs=in_specs, out_specs=out_specs,
                          out_shape=out_shape, scratch_shapes=list(scratch), compiler_params=params)


def _full(shape):
    n = len(shape)
    return pl.BlockSpec(shape, lambda *_: (0,) * n)


def _rows(tm, cols, col_block=0):
    return pl.BlockSpec((tm, cols), lambda i: (i, col_block))


def _sds(shape, dtype):
    return jax.ShapeDtypeStruct(shape, dtype)


def _row(ref, k):
    return ref[pl.ds(k, 1), :]


def _rms(x):
    r = lax.rsqrt(jnp.mean(x * x, axis=-1, keepdims=True) + EPS)
    return x * r, r


def _rms_bwd(d_hat, hat, r):
    return r * (d_hat - hat * jnp.mean(d_hat * hat, axis=-1, keepdims=True))


def _rope_partner(t):
    lane = lax.broadcasted_iota(jnp.int32, t.shape, 1)
    swapped = jnp.where(lane < NOPE + ROPE // 2, -pltpu.roll(t, HEAD_PAD - ROPE // 2, 1), pltpu.roll(t, ROPE // 2, 1))
    return jnp.where((lane >= NOPE) & (lane < NOPE + ROPE), swapped, 0.0)


def _rope(t, cos, sin):
    return t * cos + _rope_partner(t) * sin


def _rope_transposed(g, cos, sin):
    return g * cos - _rope_partner(g * sin)


def _gelu(x):
    return x * (0.5 * (1.0 + jnp.tanh(0.7978845608028654 * (x + 0.044715 * (x * x * x)))))


def _gelu_grad(x):
    t = jnp.tanh(0.7978845608028654 * (x + 0.044715 * (x * x * x)))
    return 0.5 * (1.0 + t) + 0.5 * x * (1.0 - t * t) * (0.7978845608028654 * (1.0 + 3.0 * 0.044715 * (x * x)))


def _split_dot(x, mat):
    hi = x.astype(MXU)
    lo = (x - hi.astype(F32)).astype(MXU)
    return _dot(hi, mat) + _dot(lo, mat)


def _seg_matrix():
    r = lax.broadcasted_iota(jnp.int32, (GM_WIDTH, GM_WIDTH), 0) >> 6
    c = lax.broadcasted_iota(jnp.int32, (GM_WIDTH, GM_WIDTH), 1) >> 6
    return jnp.where(r == c, 1.0 / GM_DIM, 0.0).astype(MXU)


def _spatial_mask():
    i = lax.broadcasted_iota(jnp.int32, (GM_CHUNK, GM_CHUNK), 0) >> CHUNK_SHIFT
    j = lax.broadcasted_iota(jnp.int32, (GM_CHUNK, GM_CHUNK), 1) >> CHUNK_SHIFT
    return (j <= i).astype(F32)


def _head_lane_mask(h, rows):
    lane = lax.broadcasted_iota(jnp.int32, (rows, GM_WIDTH), 1) >> 6
    return lane == h


def _my_place():
    return lax.axis_index("x"), lax.axis_index("y"), lax.axis_index("c")


def _flat(p):
    return 4 * p[0] + 2 * p[1] + p[2]


def _all_gather(arrays, name):
    n = len(arrays)

    def body(*refs):
        ins, outs = refs[:n], refs[n:2 * n]
        send_sems, recv_sems, local_sems = refs[2 * n:]
        x, y, c = _my_place()
        me, sibling = (x, y, c), (x, y, 1 - c)
        chips = [(1 - x, y), (x, 1 - y), (1 - x, 1 - y)]

        def copy(a, k, block, to, src=None):
            slot = outs[a].at[_flat(block)]
            return pltpu.make_async_remote_copy(
                src_ref=slot if src is None else src, dst_ref=slot,
                send_sem=send_sems.at[7 * a + k], recv_sem=recv_sems.at[7 * a + k],
                device_id=to, device_id_type=MESH)

        mine = [pltpu.make_async_copy(ins[a], outs[a].at[_flat(me)], local_sems.at[a]) for a in range(n)]
        for cp in mine:
            cp.start()
        first = []
        for a in range(n):
            first.append(copy(a, 0, me, sibling, src=ins[a]))
            first += [copy(a, 1 + j, me, (*chip, c), src=ins[a]) for j, chip in enumerate(chips)]
        for cp in first:
            cp.start()
        passed = []
        for a in range(n):
            for j, chip in enumerate(chips):
                copy(a, 1 + j, (*chip, c), me).wait_recv()
                fwd = copy(a, 4 + j, (*chip, c), sibling)
                fwd.start()
                passed.append(fwd)
        for a in range(n):
            copy(a, 0, sibling, me).wait_recv()
            for j, chip in enumerate(chips):
                copy(a, 4 + j, (*chip, 1 - c), me).wait_recv()
        for cp in first + passed:
            cp.wait_send()
        for cp in mine:
            cp.wait()

    any_spec = pl.BlockSpec(memory_space=pl.ANY)
    return pl.pallas_call(
        body, name=name,
        in_specs=[any_spec] * n, out_specs=[any_spec] * n,
        out_shape=[_sds((N_DEV,) + a.shape, a.dtype) for a in arrays],
        scratch_shapes=[pltpu.SemaphoreType.DMA((7 * n,)), pltpu.SemaphoreType.DMA((7 * n,)),
                        pltpu.SemaphoreType.DMA((n,))],
    )(*arrays)


def _all_to_all(arrays, name):
    n = len(arrays)
    flips = [(fx, fy, fc) for fx in (0, 1) for fy in (0, 1) for fc in (0, 1)][1:]

    def body(*refs):
        ins, outs = refs[:n], refs[n:2 * n]
        send_sems, recv_sems, local_sems = refs[2 * n:]
        x, y, c = _my_place()
        me = (x, y, c)

        def peer(f):
            return tuple(1 - v if b else v for v, b in zip(me, f))

        def copy(a, k):
            p = peer(flips[k])
            return pltpu.make_async_remote_copy(
                src_ref=ins[a].at[_flat(p)], dst_ref=outs[a].at[_flat(me)],
                send_sem=send_sems.at[7 * a + k], recv_sem=recv_sems.at[7 * a + k],
                device_id=p, device_id_type=MESH)

        def arrival(a, k):
            p = peer(flips[k])
            return pltpu.make_async_remote_copy(
                src_ref=ins[a].at[_flat(p)], dst_ref=outs[a].at[_flat(p)],
                send_sem=send_sems.at[7 * a + k], recv_sem=recv_sems.at[7 * a + k],
                device_id=p, device_id_type=MESH)

        mine = [pltpu.make_async_copy(ins[a].at[_flat(me)], outs[a].at[_flat(me)], local_sems.at[a]) for a in range(n)]
        for cp in mine:
            cp.start()
        sends = [copy(a, k) for a in range(n) for k in range(7)]
        for cp in sends:
            cp.start()
        for a in range(n):
            for k in range(7):
                arrival(a, k).wait_recv()
        for cp in sends:
            cp.wait_send()
        for cp in mine:
            cp.wait()

    any_spec = pl.BlockSpec(memory_space=pl.ANY)
    return pl.pallas_call(
        body, name=name,
        in_specs=[any_spec] * n, out_specs=[any_spec] * n,
        out_shape=[_sds(a.shape, a.dtype) for a in arrays],
        scratch_shapes=[pltpu.SemaphoreType.DMA((7 * n,)), pltpu.SemaphoreType.DMA((7 * n,)),
                        pltpu.SemaphoreType.DMA((n,))],
    )(*arrays)


def _ada_fwd(c_all, w_ada):
    def body(c_ref, w_ref, part_ref, act_ref):
        cv = c_ref[...]
        act = cv * jax.nn.sigmoid(cv)
        act_ref[...] = act
        part_ref[...] = _dot(act.astype(MXU), w_ref[...].astype(MXU))

    cols = w_ada.shape[1]
    return _call(body, name="ada_fwd", grid=(1,),
                 in_specs=[_full(c_all.shape), _full(w_ada.shape)],
                 out_specs=[_full((N_DEV, cols)), _full(c_all.shape)],
                 out_shape=[_sds((N_DEV, cols), F32), _sds(c_all.shape, F32)])(c_all, w_ada)


def _mix_in_fwd(x, ada_raw, ada_b, g_pre, w1, g_q, g_kv, wq, wkv, cos_t, sin_t, tm):
    s = x.shape[0]

    def body(x_ref, ar_ref, ab_ref, g_ref, w1_ref, gq_ref, gkv_ref, wq_ref, wkv_ref, cos_ref, sin_ref,
             h1_ref, z_ref, qp_ref, kp_ref, vp_ref, cqn_ref, ckvn_ref):
        sh = _row(ar_ref, 0) + _row(ab_ref, 0)
        sc = _row(ar_ref, 1) + _row(ab_ref, 1)
        xn, _ = _rms(x_ref[...])
        hb = ((xn * g_ref[...]) * (1.0 + sc) + sh).astype(MXU)
        h1_ref[...] = hb
        z = _dot(hb, w1_ref[...])
        z_ref[...] = z
        cos, sin = cos_ref[...], sin_ref[...]
        cqn = (_rms(z[:, :Q_LORA])[0] * gq_ref[...]).astype(MXU)
        ckvn = (_rms(z[:, Q_LORA:Q_LORA + KV_LORA])[0] * gkv_ref[...]).astype(MXU)
        cqn_ref[...] = cqn
        ckvn_ref[...] = ckvn
        q = _dot(cqn, wq_ref[...])
        kv = _dot(ckvn, wkv_ref[...])
        k_rope = _rope(z[:, Q_LORA + KV_LORA:Q_LORA + KV_LORA + HEAD_PAD], cos, sin)
        for h in range(HEADS):
            blk = slice(h * HEAD_PAD, (h + 1) * HEAD_PAD)
            qp_ref[:, blk] = _rope(q[:, blk], cos, sin).astype(MXU)
            kp_ref[:, blk] = (kv[:, blk] + k_rope).astype(MXU)
        vp_ref[...] = kv[:, HEADS * HEAD_PAD:].astype(MXU)

    hp = HEADS * HEAD_PAD
    return _call(
        body, name="mix_in_fwd", grid=(s // tm,), sem=("parallel",),
        in_specs=[_rows(tm, D_MODEL), _full(ada_raw.shape), _full(ada_b.shape), _full(g_pre.shape), _full(w1.shape),
                  _full(g_q.shape), _full(g_kv.shape), _full(wq.shape), _full(wkv.shape),
                  _rows(tm, HEAD_PAD), _rows(tm, HEAD_PAD)],
        out_specs=[_rows(tm, D_MODEL), _rows(tm, Z_COLS), _rows(tm, hp), _rows(tm, hp), _rows(tm, hp),
                   _rows(tm, Q_LORA), _rows(tm, KV_LORA)],
        out_shape=[_sds((s, D_MODEL), MXU), _sds((s, Z_COLS), F32), _sds((s, hp), MXU), _sds((s, hp), MXU),
                   _sds((s, hp), MXU), _sds((s, Q_LORA), MXU), _sds((s, KV_LORA), MXU)],
    )(x, ada_raw, ada_b, g_pre, w1, g_q, g_kv, wq, wkv, cos_t, sin_t)


def _gm_norm(zv, seg):
    gv = _gelu(zv)
    cen = gv - _split_dot(gv, seg)
    rstd = lax.rsqrt(_split_dot(cen * cen, seg) + EPS)
    return gv, cen * rstd, rstd


def _gm_mix(wm, vb, rows):
    out = jnp.zeros((rows, GM_WIDTH), F32)
    for h in range(HEADS):
        out = out + jnp.where(_head_lane_mask(h, rows), _dot(wm[h], vb), 0.0)
    return out


def _gmlp_fwd(z, ln_g, ln_b, w_sp, bias_exp, tm):
    s = z.shape[0]
    nblk = tm // GM_CHUNK

    def body(zu_ref, zv_ref, lg_ref, lb_ref, w_ref, be_ref, sgu_ref):
        seg = _seg_matrix()
        mask = _spatial_mask()
        wm = [(w_ref[h] * mask).astype(MXU) for h in range(HEADS)]
        gu = _gelu(zu_ref[...])
        _, vhat, _ = _gm_norm(zv_ref[...], seg)
        vln = (vhat * lg_ref[...] + lb_ref[...]).astype(MXU)
        for n in range(nblk):
            rows = slice(n * GM_CHUNK, (n + 1) * GM_CHUNK)
            mixed = _gm_mix(wm, vln[rows], GM_CHUNK) + be_ref[...]
            sgu_ref[rows, :] = (gu[rows] * mixed).astype(MXU)

    return _call(
        body, name="gmlp_fwd", grid=(s // tm,), sem=("parallel",),
        in_specs=[_rows(tm, GM_WIDTH, 1), _rows(tm, GM_WIDTH, 2), _full(ln_g.shape), _full(ln_b.shape),
                  _full(w_sp.shape), _full(bias_exp.shape)],
        out_specs=_rows(tm, GM_WIDTH), out_shape=_sds((s, GM_WIDTH), MXU),
    )(z, z, ln_g, ln_b, w_sp, bias_exp)


def _chunk_mask(i, j, tq, tk):
    qc = (i * tq + lax.broadcasted_iota(jnp.int32, (tq, tk), 0)) >> CHUNK_SHIFT
    kc = (j * tk + lax.broadcasted_iota(jnp.int32, (tq, tk), 1)) >> CHUNK_SHIFT
    return kc <= qc


NEG_BIG = -1e30


def _attn_fwd(qp, kp, vp, tq):
    s = qp.shape[0]
    nq = s // tq

    def body(q_ref, k_ref, v_ref, o_ref, lse_ref, m_sc, l_sc, acc_sc):
        i, j = pl.program_id(1), pl.program_id(2)

        @pl.when(j == 0)
        def _():
            m_sc[...] = jnp.full(m_sc.shape, NEG_BIG, F32)
            l_sc[...] = jnp.zeros(l_sc.shape, F32)
            acc_sc[...] = jnp.zeros(acc_sc.shape, F32)

        @pl.when(j <= i)
        def _():
            sc = _dot_nt(q_ref[...], k_ref[...]) * ATTN_SCALE
            sc = jnp.where(_chunk_mask(i, j, tq, tq), sc, NEG_BIG)
            m_prev = m_sc[...]
            m_new = jnp.maximum(m_prev, jnp.max(sc, axis=-1, keepdims=True))
            alpha = jnp.exp(m_prev - m_new)
            p = jnp.exp(sc - m_new)
            l_sc[...] = alpha * l_sc[...] + jnp.sum(p, axis=-1, keepdims=True)
            acc_sc[...] = alpha * acc_sc[...] + _dot(p.astype(MXU), v_ref[...])
            m_sc[...] = m_new

        @pl.when(j == i)
        def _():
            o_ref[...] = (acc_sc[...] / l_sc[...]).astype(MXU)
            lse_ref[0] = m_sc[...] + jnp.log(l_sc[...])

    q_spec = pl.BlockSpec((tq, HEAD_PAD), lambda h, i, j: (i, h))
    kv_spec = pl.BlockSpec((tq, HEAD_PAD), lambda h, i, j: (jnp.minimum(i, j), h))
    return _call(
        body, name="attn_fwd", grid=(HEADS, nq, nq), sem=("parallel", "parallel", "arbitrary"),
        in_specs=[q_spec, kv_spec, kv_spec],
        out_specs=[q_spec, pl.BlockSpec((1, tq, 1), lambda h, i, j: (h, i, 0))],
        out_shape=[_sds(qp.shape, MXU), _sds((HEADS, s, 1), F32)],
        scratch=[pltpu.VMEM((tq, 1), F32), pltpu.VMEM((tq, 1), F32), pltpu.VMEM((tq, HEAD_PAD), F32)],
    )(qp, kp, vp)


def _out_proj_fwd(o_pad, sgu, wo, x, ada_raw, ada_b, g_post_mix, g_pre_ffn, tm):
    s = x.shape[0]
    hp = HEADS * HEAD_PAD

    def body(o_ref, sgu_ref, wo_ref, x_ref, ar_ref, ab_ref, gpm_ref, gpf_ref, m_ref, x2_ref, h2_ref):
        gt1 = _row(ar_ref, 2) + _row(ab_ref, 2)
        sh2 = _row(ar_ref, 3) + _row(ab_ref, 3)
        sc2 = _row(ar_ref, 4) + _row(ab_ref, 4)
        m = _dot(o_ref[...], wo_ref[pl.ds(0, hp), :]) + _dot(sgu_ref[...], wo_ref[pl.ds(hp, GM_WIDTH), :])
        m_ref[...] = m
        x2 = x_ref[...] + gt1 * (_rms(m)[0] * gpm_ref[...])
        x2_ref[...] = x2
        h2_ref[...] = ((_rms(x2)[0] * gpf_ref[...]) * (1.0 + sc2) + sh2).astype(MXU)

    return _call(
        body, name="out_proj_fwd", grid=(s // tm,), sem=("parallel",),
        in_specs=[_rows(tm, hp), _rows(tm, GM_WIDTH), _full(wo.shape), _rows(tm, D_MODEL), _full(ada_raw.shape),
                  _full(ada_b.shape), _full(g_post_mix.shape), _full(g_pre_ffn.shape)],
        out_specs=[_rows(tm, D_MODEL)] * 3,
        out_shape=[_sds((s, D_MODEL), F32), _sds((s, D_MODEL), F32), _sds((s, D_MODEL), MXU)],
    )(o_pad, sgu, wo, x, ada_raw, ada_b, g_post_mix, g_pre_ffn)


def _conv_taps(u, halo):
    ext = jnp.concatenate([halo, u], axis=0)
    return pltpu.roll(ext, 1, 0)[8:], pltpu.roll(ext, 2, 0)[8:]


def _conv(u, halo, cw_ref, cb_ref):
    m1, m2 = _conv_taps(u, halo)
    y = cb_ref[0] + ((m2 * cw_ref[0, pl.ds(0, 1), :] + m1 * cw_ref[0, pl.ds(1, 1), :]) + u * cw_ref[0, pl.ds(2, 1), :])
    return y, m1, m2


def _ffn_up_fwd(h2, w_up, conv_w, conv_b, tm):
    s = h2.shape[0]
    half = N_DEV // 2

    def body(h_ref, wa_ref, wb_ref, cwa_ref, cwb_ref, cba_ref, cbb_ref, ua_ref, ub_ref, act_ref, halo_a, halo_b):
        i = pl.program_id(1)

        @pl.when(i == 0)
        def _():
            halo_a[...] = jnp.zeros(halo_a.shape, F32)
            halo_b[...] = jnp.zeros(halo_b.shape, F32)

        hb = h_ref[...]
        ua = _dot(hb, wa_ref[0])
        ub = _dot(hb, wb_ref[0])
        ua_ref[0] = ua
        ub_ref[0] = ub
        ya, _, _ = _conv(ua, halo_a[...], cwa_ref, cba_ref)
        yb, _, _ = _conv(ub, halo_b[...], cwb_ref, cbb_ref)
        halo_a[...] = ua[tm - 8:]
        halo_b[...] = ub[tm - 8:]
        act_ref[0] = ((ya * jax.nn.sigmoid(ya)) * yb).astype(MXU)

    def blk(shape, off):
        return pl.BlockSpec(shape, lambda j, i: (j + off, 0, 0))

    def tok(off=0):
        return pl.BlockSpec((1, tm, FF_BLK), lambda j, i: (j + off, i, 0))

    return _call(
        body, name="ffn_up_fwd", grid=(half, s // tm), sem=("parallel", "arbitrary"),
        in_specs=[pl.BlockSpec((tm, D_MODEL), lambda j, i: (i, 0)),
                  blk((1, D_MODEL, FF_BLK), 0), blk((1, D_MODEL, FF_BLK), half),
                  blk((1, 3, FF_BLK), 0), blk((1, 3, FF_BLK), half), blk((1, 1, FF_BLK), 0), blk((1, 1, FF_BLK), half)],
        out_specs=[tok(), tok(), tok()],
        out_shape=[_sds((half, s, FF_BLK), F32), _sds((half, s, FF_BLK), F32), _sds((half, s, FF_BLK), MXU)],
        scratch=[pltpu.VMEM((8, FF_BLK), F32), pltpu.VMEM((8, FF_BLK), F32)],
    )(h2, w_up, w_up, conv_w, conv_w, conv_b, conv_b)


def _ffn_down_fwd(act, wd, x2, target, ada_raw, ada_b, g_post_ffn, tm):
    s = x2.shape[0]
    half = N_DEV // 2

    def body(act_ref, wd_ref, x2_ref, t_ref, ar_ref, ab_ref, g_ref, dout_ref, df_ref, loss_ref, dgt_ref, dg_ref):
        i = pl.program_id(0)

        @pl.when(i == 0)
        def _():
            loss_ref[...] = jnp.zeros(loss_ref.shape, F32)
            dgt_ref[...] = jnp.zeros(dgt_ref.shape, F32)
            dg_ref[...] = jnp.zeros(dg_ref.shape, F32)

        gt2 = _row(ar_ref, 5) + _row(ab_ref, 5)
        g = g_ref[...]
        f = _dot(act_ref[0], wd_ref[0])
        for j in range(1, half):
            f = f + _dot(act_ref[j], wd_ref[j])
        fhat, rf = _rms(f)
        fn = fhat * g
        err = (x2_ref[...] + gt2 * fn) - t_ref[...]
        loss_ref[...] += 0.5 * jnp.sum(jnp.mean(err * err, axis=-1, keepdims=True))
        d_out = err * (1.0 / D_MODEL)
        dout_ref[...] = d_out
        dgt_ref[...] += jnp.sum(d_out * fn, axis=0, keepdims=True)
        d_fn = d_out * gt2
        dg_ref[...] += jnp.sum(d_fn * fhat, axis=0, keepdims=True)
        df_ref[...] = _rms_bwd(d_fn * g, fhat, rf).astype(MXU)

    vec = pl.BlockSpec((1, D_MODEL), lambda i: (0, 0))
    return _call(
        body, name="ffn_down_fwd", grid=(s // tm,), sem=("arbitrary",),
        in_specs=[pl.BlockSpec((half, tm, FF_BLK), lambda i: (0, i, 0)), _full(wd.shape), _rows(tm, D_MODEL),
                  _rows(tm, D_MODEL), _full(ada_raw.shape), _full(ada_b.shape), _full(g_post_ffn.shape)],
        out_specs=[_rows(tm, D_MODEL), _rows(tm, D_MODEL), pl.BlockSpec((1, 128), lambda i: (0, 0)), vec, vec],
        out_shape=[_sds((s, D_MODEL), F32), _sds((s, D_MODEL), MXU), _sds((1, 128), F32),
                   _sds((1, D_MODEL), F32), _sds((1, D_MODEL), F32)],
    )(act, wd, x2, target, ada_raw, ada_b, g_post_ffn)


def _ffn_down_bwd(d_f, wd, up_a, up_b, conv_w, conv_b, tm):
    s = d_f.shape[0]
    half = N_DEV // 2
    nt = s // tm
    hb = tm // 8

    def body(df_ref, wd_ref, ua_ref, ub_ref, pa_ref, pb_ref, cwa_ref, cwb_ref, cba_ref, cbb_ref,
             dua_ref, dub_ref, dcwa_ref, dcwb_ref, dcba_ref, dcbb_ref, next_a, next_b):
        i = pl.program_id(1)
        first_tile = i == nt - 1

        @pl.when(i == 0)
        def _():
            next_a[...] = jnp.zeros(next_a.shape, F32)
            next_b[...] = jnp.zeros(next_b.shape, F32)
            for r in (dcwa_ref, dcwb_ref, dcba_ref, dcbb_ref):
                r[...] = jnp.zeros(r.shape, F32)

        d_act = _dot_nt(df_ref[...], wd_ref[0])
        keep = jnp.where(first_tile, 0.0, 1.0)
        ua, ub = ua_ref[0], ub_ref[0]
        ya, ma1, ma2 = _conv(ua, pa_ref[0] * keep, cwa_ref, cba_ref)
        yb, mb1, mb2 = _conv(ub, pb_ref[0] * keep, cwb_ref, cbb_ref)
        sig = jax.nn.sigmoid(ya)
        d_ya = d_act * yb * (sig * (1.0 + ya * (1.0 - sig)))
        d_yb = d_act * (ya * sig)

        def conv_bwd(d_y, u, m1, m2, nxt, cw_ref, du_ref, dcw_ref, dcb_ref):
            ext = jnp.concatenate([d_y, nxt[...]], axis=0)
            p1 = pltpu.roll(ext, tm + 7, 0)[:tm]
            p2 = pltpu.roll(ext, tm + 6, 0)[:tm]
            d_u = (d_y * cw_ref[0, pl.ds(2, 1), :] + p1 * cw_ref[0, pl.ds(1, 1), :]) + p2 * cw_ref[0, pl.ds(0, 1), :]
            du_ref[0] = d_u.astype(MXU)
            dcb_ref[0] += jnp.sum(d_y, axis=0, keepdims=True)
            dcw_ref[0, pl.ds(0, 1), :] += jnp.sum(d_y * m2, axis=0, keepdims=True)
            dcw_ref[0, pl.ds(1, 1), :] += jnp.sum(d_y * m1, axis=0, keepdims=True)
            dcw_ref[0, pl.ds(2, 1), :] += jnp.sum(d_y * u, axis=0, keepdims=True)
            nxt[...] = d_y[:8]

        conv_bwd(d_ya, ua, ma1, ma2, next_a, cwa_ref, dua_ref, dcwa_ref, dcba_ref)
        conv_bwd(d_yb, ub, mb1, mb2, next_b, cwb_ref, dub_ref, dcwb_ref, dcbb_ref)

    def rev(i):
        return nt - 1 - i

    def blk(shape, off):
        return pl.BlockSpec(shape, lambda j, i: (j + off, 0, 0))

    tok = pl.BlockSpec((1, tm, FF_BLK), lambda j, i: (j, rev(i), 0))
    prev = pl.BlockSpec((1, 8, FF_BLK), lambda j, i: (j, jnp.maximum(rev(i) * hb - 1, 0), 0))
    acc3 = pl.BlockSpec((1, 3, FF_BLK), lambda j, i: (j, 0, 0))
    acc1 = pl.BlockSpec((1, 1, FF_BLK), lambda j, i: (j, 0, 0))
    return _call(
        body, name="ffn_down_bwd", grid=(half, nt), sem=("parallel", "arbitrary"),
        in_specs=[pl.BlockSpec((tm, D_MODEL), lambda j, i: (rev(i), 0)), blk((1, FF_BLK, D_MODEL), 0),
                  tok, tok, prev, prev,
                  blk((1, 3, FF_BLK), 0), blk((1, 3, FF_BLK), half), blk((1, 1, FF_BLK), 0), blk((1, 1, FF_BLK), half)],
        out_specs=[tok, tok, acc3, acc3, acc1, acc1],
        out_shape=[_sds((half, s, FF_BLK), MXU), _sds((half, s, FF_BLK), MXU),
                   _sds((half, 3, FF_BLK), F32), _sds((half, 3, FF_BLK), F32),
                   _sds((half, 1, FF_BLK), F32), _sds((half, 1, FF_BLK), F32)],
        scratch=[pltpu.VMEM((8, FF_BLK), F32), pltpu.VMEM((8, FF_BLK), F32)],
    )(d_f, wd, up_a, up_b, up_a, up_b, conv_w, conv_w, conv_b, conv_b)


def _ffn_up_bwd(d_up_a, d_up_b, w_up, x2, m, d_out, ada_raw, ada_b, g_pre_ffn, g_post_mix, tm):
    s = x2.shape[0]
    half = N_DEV // 2

    def body(da_ref, db_ref, w_ref, x2_ref, m_ref, dout_ref, ar_ref, ab_ref, gpf_ref, gpm_ref,
             dx_ref, dm_ref, dsh_ref, dsc_ref, dgpf_ref, dgt1_ref, dgpm_ref):
        i = pl.program_id(0)

        @pl.when(i == 0)
        def _():
            for r in (dsh_ref, dsc_ref, dgpf_ref, dgt1_ref, dgpm_ref):
                r[...] = jnp.zeros(r.shape, F32)

        gt1 = _row(ar_ref, 2) + _row(ab_ref, 2)
        sc2 = _row(ar_ref, 4) + _row(ab_ref, 4)
        gpf, gpm = gpf_ref[...], gpm_ref[...]
        d_h2 = _dot_nt(da_ref[0], w_ref[0])
        for j in range(1, half):
            d_h2 = d_h2 + _dot_nt(da_ref[j], w_ref[j])
        for j in range(half):
            d_h2 = d_h2 + _dot_nt(db_ref[j], w_ref[half + j])
        x2n, r2 = _rms(x2_ref[...])
        dsh_ref[...] += jnp.sum(d_h2, axis=0, keepdims=True)
        dsc_ref[...] += jnp.sum(d_h2 * (x2n * gpf), axis=0, keepdims=True)
        d_mod = d_h2 * (1.0 + sc2)
        dgpf_ref[...] += jnp.sum(d_mod * x2n, axis=0, keepdims=True)
        d_x2 = dout_ref[...] + _rms_bwd(d_mod * gpf, x2n, r2)
        dx_ref[...] = d_x2
        mhat, rm = _rms(m_ref[...])
        dgt1_ref[...] += jnp.sum(d_x2 * (mhat * gpm), axis=0, keepdims=True)
        d_mn = d_x2 * gt1
        dgpm_ref[...] += jnp.sum(d_mn * mhat, axis=0, keepdims=True)
        dm_ref[...] = _rms_bwd(d_mn * gpm, mhat, rm).astype(MXU)

    vec = pl.BlockSpec((1, D_MODEL), lambda i: (0, 0))
    tok = pl.BlockSpec((half, tm, FF_BLK), lambda i: (0, i, 0))
    return _call(
        body, name="ffn_up_bwd", grid=(s // tm,), sem=("arbitrary",),
        in_specs=[tok, tok, _full(w_up.shape), _rows(tm, D_MODEL), _rows(tm, D_MODEL), _rows(tm, D_MODEL),
                  _full(ada_raw.shape), _full(ada_b.shape), _full(g_pre_ffn.shape), _full(g_post_mix.shape)],
        out_specs=[_rows(tm, D_MODEL), _rows(tm, D_MODEL), vec, vec, vec, vec, vec],
        out_shape=[_sds((s, D_MODEL), F32), _sds((s, D_MODEL), MXU)] + [_sds((1, D_MODEL), F32)] * 5,
    )(d_up_a, d_up_b, w_up, x2, m, d_out, ada_raw, ada_b, g_pre_ffn, g_post_mix)


def _out_proj_bwd(d_m, wo, o_pad, tm):
    s = d_m.shape[0]
    hp = HEADS * HEAD_PAD

    def body(dm_ref, wo_ref, o_ref, do_ref, dsgu_ref, delta_ref):
        d_cat = _dot_nt(dm_ref[...], wo_ref[...])
        d_o = d_cat[:, :hp]
        do_ref[...] = d_o.astype(MXU)
        dsgu_ref[...] = d_cat[:, hp:]
        prod = d_o * o_ref[...].astype(F32)
        for h in range(HEADS):
            delta_ref[h] = jnp.sum(prod[:, h * HEAD_PAD:(h + 1) * HEAD_PAD], axis=-1, keepdims=True)

    return _call(
        body, name="out_proj_bwd", grid=(s // tm,), sem=("parallel",),
        in_specs=[_rows(tm, D_MODEL), _full(wo.shape), _rows(tm, hp)],
        out_specs=[_rows(tm, hp), _rows(tm, GM_WIDTH), pl.BlockSpec((HEADS, tm, 1), lambda i: (0, i, 0))],
        out_shape=[_sds((s, hp), MXU), _sds((s, GM_WIDTH), F32), _sds((HEADS, s, 1), F32)],
    )(d_m, wo, o_pad)


def _attn_bwd(qp, kp, vp, d_o, lse, delta, tq):
    s = qp.shape[0]
    nq = s // tq

    def body(q_ref, k_ref, v_ref, do_ref, lse_ref, dl_ref, dq_ref, dk_ref, dv_ref, dk_sc, dv_sc):
        j, i = pl.program_id(1), pl.program_id(2)

        @pl.when((j == 0) & (i == 0))
        def _():
            dq_ref[...] = jnp.zeros(dq_ref.shape, F32)

        @pl.when(i == j)
        def _():
            dk_sc[...] = jnp.zeros(dk_sc.shape, F32)
            dv_sc[...] = jnp.zeros(dv_sc.shape, F32)

        @pl.when(i >= j)
        def _():
            q, k, do = q_ref[...], k_ref[...], do_ref[...]
            sc = _dot_nt(q, k) * ATTN_SCALE
            sc = jnp.where(_chunk_mask(i, j, tq, tq), sc, NEG_BIG)
            p = jnp.exp(sc - lse_ref[0])
            dv_sc[...] += _dot_tn(p.astype(MXU), do)
            dp = _dot_nt(do, v_ref[...])
            ds = ((p * (dp - dl_ref[0])) * ATTN_SCALE).astype(MXU)
            dk_sc[...] += _dot_tn(ds, q)
            rows = pl.ds(pl.multiple_of(i * tq, tq), tq)
            dq_ref[rows, :] += _dot(ds, k)

        @pl.when(i == nq - 1)
        def _():
            dk_ref[...] = dk_sc[...]
            dv_ref[...] = dv_sc[...]

    q_spec = pl.BlockSpec((tq, HEAD_PAD), lambda h, j, i: (jnp.maximum(i, j), h))
    kv_spec = pl.BlockSpec((tq, HEAD_PAD), lambda h, j, i: (j, h))
    col_spec = pl.BlockSpec((1, tq, 1), lambda h, j, i: (h, jnp.maximum(i, j), 0))
    return _call(
        body, name="attn_bwd", grid=(HEADS, nq, nq), sem=("parallel", "arbitrary", "arbitrary"),
        in_specs=[q_spec, kv_spec, kv_spec, q_spec, col_spec, col_spec],
        out_specs=[pl.BlockSpec((s, HEAD_PAD), lambda h, j, i: (0, h)), kv_spec, kv_spec],
        out_shape=[_sds(qp.shape, F32), _sds(qp.shape, F32), _sds(qp.shape, F32)],
        scratch=[pltpu.VMEM((tq, HEAD_PAD), F32), pltpu.VMEM((tq, HEAD_PAD), F32)],
    )(qp, kp, vp, d_o, lse, delta)


def _gmlp_bwd(z, d_sgu, ln_g, ln_b, w_sp, bias_exp, tm):
    s = z.shape[0]
    nblk = tm // GM_CHUNK

    def body(zu_ref, zv_ref, dsgu_ref, lg_ref, lb_ref, w_ref, be_ref,
             dguv_ref, dws_ref, dbs_ref, dlg_ref, dlb_ref, dbe_sc, dvln_sc):
        i = pl.program_id(0)

        @pl.when(i == 0)
        def _():
            for r in (dws_ref, dlg_ref, dlb_ref, dbe_sc):
                r[...] = jnp.zeros(r.shape, F32)

        seg = _seg_matrix()
        mask = _spatial_mask()
        wm = [(w_ref[h] * mask).astype(MXU) for h in range(HEADS)]
        zu, zv = zu_ref[...], zv_ref[...]
        gu = _gelu(zu)
        _, vhat, rstd = _gm_norm(zv, seg)
        lg = lg_ref[...]
        vln = (vhat * lg + lb_ref[...]).astype(MXU)
        d_sgu = dsgu_ref[...]
        for n in range(nblk):
            rows = slice(n * GM_CHUNK, (n + 1) * GM_CHUNK)
            vb = vln[rows]
            mixed = _gm_mix(wm, vb, GM_CHUNK) + be_ref[...]
            d_mixed = d_sgu[rows] * gu[rows]
            dguv_ref[rows, pl.ds(0, GM_WIDTH)] = ((d_sgu[rows] * mixed) * _gelu_grad(zu[rows])).astype(MXU)
            dbe_sc[...] += d_mixed
            dmb = d_mixed.astype(MXU)
            d_vln = jnp.zeros((GM_CHUNK, GM_WIDTH), F32)
            for h in range(HEADS):
                hm = _head_lane_mask(h, GM_CHUNK)
                dws_ref[h] += _dot_nt(jnp.where(hm, dmb, jnp.zeros_like(dmb)), vb)
                d_vln = d_vln + jnp.where(hm, _dot_tn(wm[h], dmb), 0.0)
            dvln_sc[rows, :] = d_vln
        d_vln = dvln_sc[...]
        dlg_ref[...] += jnp.sum(d_vln * vhat, axis=0, keepdims=True)
        dlb_ref[...] += jnp.sum(d_vln, axis=0, keepdims=True)
        d_vhat = d_vln * lg
        d_gv = rstd * ((d_vhat - _split_dot(d_vhat, seg)) - vhat * _split_dot(d_vhat * vhat, seg))
        dguv_ref[:, pl.ds(GM_WIDTH, GM_WIDTH)] = (d_gv * _gelu_grad(zv)).astype(MXU)

        @pl.when(i == pl.num_programs(0) - 1)
        def _():
            for h in range(HEADS):
                dws_ref[h] = dws_ref[h] * mask
            hrow = lax.broadcasted_iota(jnp.int32, (HEADS, GM_WIDTH), 0)
            hlane = lax.broadcasted_iota(jnp.int32, (HEADS, GM_WIDTH), 1) >> 6
            ind = jnp.where(hrow == hlane, 1.0, 0.0).astype(MXU)
            acc = dbe_sc[...]
            hi = acc.astype(MXU)
            lo = (acc - hi.astype(F32)).astype(MXU)
            dbs_ref[...] = _dot_nt(ind, hi) + _dot_nt(ind, lo)

    return _call(
        body, name="gmlp_bwd", grid=(s // tm,), sem=("arbitrary",),
        in_specs=[_rows(tm, GM_WIDTH, 1), _rows(tm, GM_WIDTH, 2), _rows(tm, GM_WIDTH), _full(ln_g.shape),
                  _full(ln_b.shape), _full(w_sp.shape), _full(bias_exp.shape)],
        out_specs=[_rows(tm, 2 * GM_WIDTH), _full(w_sp.shape), _full((HEADS, GM_CHUNK)), _full(ln_g.shape),
                   _full(ln_b.shape)],
        out_shape=[_sds((s, 2 * GM_WIDTH), MXU), _sds(w_sp.shape, F32), _sds((HEADS, GM_CHUNK), F32),
                   _sds(ln_g.shape, F32), _sds(ln_b.shape, F32)],
        scratch=[pltpu.VMEM((GM_CHUNK, GM_WIDTH), F32), pltpu.VMEM((tm, GM_WIDTH), F32)],
    )(z, z, d_sgu, ln_g, ln_b, w_sp, bias_exp)


def _mix_in_bwd(dq, dk, dv, z, d_guv, x, d_x_part, ada_raw, ada_b, g_pre, g_q, g_kv, w1a, w1b, wq, wkv,
                cos_t, sin_t, tm):
    s = x.shape[0]
    hp = HEADS * HEAD_PAD
    za = Q_LORA + KV_LORA + HEAD_PAD

    def body(dq_ref, dk_ref, dv_ref, z_ref, dguv_ref, x_ref, dxp_ref, ar_ref, ab_ref, g_ref, gq_ref, gkv_ref,
             w1a_ref, w1b_ref, wq_ref, wkv_ref, cos_ref, sin_ref,
             gx_ref, dza_ref, dqp_ref, dkvp_ref, dsh_ref, dsc_ref, dg_ref, dgq_ref, dgkv_ref):
        i = pl.program_id(0)

        @pl.when(i == 0)
        def _():
            for r in (dsh_ref, dsc_ref, dg_ref, dgq_ref, dgkv_ref):
                r[...] = jnp.zeros(r.shape, F32)

        cos, sin = cos_ref[...], sin_ref[...]
        d_krot = jnp.zeros((tm, HEAD_PAD), F32)
        for h in range(HEADS):
            blk = slice(h * HEAD_PAD, (h + 1) * HEAD_PAD)
            dqp_ref[:, blk] = _rope_transposed(dq_ref[:, blk], cos, sin).astype(MXU)
            dk_h = dk_ref[:, blk]
            d_krot = d_krot + dk_h
            dkvp_ref[:, blk] = dk_h.astype(MXU)
        dkvp_ref[:, pl.ds(hp, hp)] = dv_ref[...].astype(MXU)
        lane = lax.broadcasted_iota(jnp.int32, (tm, HEAD_PAD), 1)
        d_kr = jnp.where((lane >= NOPE) & (lane < NOPE + ROPE), _rope_transposed(d_krot, cos, sin), 0.0)
        d_cqn = _dot_nt(dqp_ref[...], wq_ref[...])
        d_ckvn = _dot_nt(dkvp_ref[...], wkv_ref[...])
        zt = z_ref[...]
        gq, gkv = gq_ref[...], gkv_ref[...]
        cq_hat, rq = _rms(zt[:, :Q_LORA])
        ckv_hat, rkv = _rms(zt[:, Q_LORA:Q_LORA + KV_LORA])
        dgq_ref[...] += jnp.sum(d_cqn * cq_hat, axis=0, keepdims=True)
        dgkv_ref[...] += jnp.sum(d_ckvn * ckv_hat, axis=0, keepdims=True)
        d_cq = _rms_bwd(d_cqn * gq, cq_hat, rq)
        d_ckv = _rms_bwd(d_ckvn * gkv, ckv_hat, rkv)
        d_za = jnp.concatenate([d_cq, d_ckv, d_kr], axis=1).astype(MXU)
        dza_ref[...] = d_za
        d_h1 = _dot_nt(d_za, w1a_ref[...]) + _dot_nt(dguv_ref[...], w1b_ref[...])
        sc1 = _row(ar_ref, 1) + _row(ab_ref, 1)
        g = g_ref[...]
        xn, r1 = _rms(x_ref[...])
        dsh_ref[...] += jnp.sum(d_h1, axis=0, keepdims=True)
        dsc_ref[...] += jnp.sum(d_h1 * (xn * g), axis=0, keepdims=True)
        d_mod = d_h1 * (1.0 + sc1)
        dg_ref[...] += jnp.sum(d_mod * xn, axis=0, keepdims=True)
        gx_ref[...] = dxp_ref[...] + _rms_bwd(d_mod * g, xn, r1)

    vec = pl.BlockSpec((1, D_MODEL), lambda i: (0, 0))
    return _call(
        body, name="mix_in_bwd", grid=(s // tm,), sem=("arbitrary",),
        in_specs=[_rows(tm, hp), _rows(tm, hp), _rows(tm, hp), _rows(tm, za), _rows(tm, 2 * GM_WIDTH),
                  _rows(tm, D_MODEL), _rows(tm, D_MODEL), _full(ada_raw.shape), _full(ada_b.shape), _full(g_pre.shape),
                  _full(g_q.shape), _full(g_kv.shape), _full(w1a.shape), _full(w1b.shape), _full(wq.shape),
                  _full(wkv.shape), _rows(tm, HEAD_PAD), _rows(tm, HEAD_PAD)],
        out_specs=[_rows(tm, D_MODEL), _rows(tm, za), _rows(tm, hp), _rows(tm, 2 * hp), vec, vec, vec,
                   _full(g_q.shape), _full(g_kv.shape)],
        out_shape=[_sds((s, D_MODEL), F32), _sds((s, za), MXU), _sds((s, hp), MXU), _sds((s, 2 * hp), MXU),
                   _sds((1, D_MODEL), F32), _sds((1, D_MODEL), F32), _sds((1, D_MODEL), F32),
                   _sds(g_q.shape, F32), _sds(g_kv.shape, F32)],
    )(dq, dk, dv, z, d_guv, x, d_x_part, ada_raw, ada_b, g_pre, g_q, g_kv, w1a, w1b, wq, wkv, cos_t, sin_t)


def _tn_matmul(a, b, name, ts):
    ga, s, m = a.shape
    gb, _, n = b.shape
    g = max(ga, gb)
    tn = n if n <= 1024 else 1024
    steps = s // ts

    def body(a_ref, b_ref, o_ref, acc):
        k = pl.program_id(2)

        @pl.when(k == 0)
        def _():
            acc[...] = jnp.zeros(acc.shape, F32)

        acc[...] += _dot_tn(a_ref[0], b_ref[0])

        @pl.when(k == steps - 1)
        def _():
            o_ref[0] = acc[...].astype(MXU)

    return _call(
        body, name=name, grid=(g, n // tn, steps), sem=("parallel", "parallel", "arbitrary"),
        in_specs=[pl.BlockSpec((1, ts, m), lambda gi, ni, k: (gi if ga > 1 else 0, k, 0)),
                  pl.BlockSpec((1, ts, tn), lambda gi, ni, k: (gi if gb > 1 else 0, k, ni))],
        out_specs=pl.BlockSpec((1, m, tn), lambda gi, ni, k: (gi, 0, ni)),
        out_shape=_sds((g, m, n), MXU),
        scratch=[pltpu.VMEM((m, tn), F32)],
    )(a, b)


def _adamw(w, g, m, v):
    m2 = ADAM_B1 * m + (1.0 - ADAM_B1) * g
    v2 = ADAM_B2 * v + (1.0 - ADAM_B2) * (g * g)
    m_hat = m2 / (1.0 - ADAM_B1 ** ADAM_STEP)
    v_hat = v2 / (1.0 - ADAM_B2 ** ADAM_STEP)
    delta = -ADAM_LR * (m_hat / (jnp.sqrt(v_hat) + ADAM_EPS) + ADAM_WD * w)
    return delta, m2, v2


def _adam_reduce(recv, w, m, v, name):
    r, c = w.shape
    tr = r if r <= 512 else 256

    def body(p_ref, w_ref, m_ref, v_ref, g_ref, d_ref, mo_ref, vo_ref):
        g = p_ref[0].astype(F32)
        for j in range(1, N_DEV):
            g = g + p_ref[j].astype(F32)
        g_ref[...] = g
        d_ref[...], mo_ref[...], vo_ref[...] = _adamw(w_ref[...], g, m_ref[...], v_ref[...])

    blk = pl.BlockSpec((tr, c), lambda i: (i, 0))
    return _call(
        body, name=name, grid=(r // tr,), sem=("parallel",),
        in_specs=[pl.BlockSpec((N_DEV, tr, c), lambda i: (0, i, 0)), blk, blk, blk],
        out_specs=[blk] * 4, out_shape=[_sds((r, c), F32)] * 4,
    )(recv, w, m, v)


def _adam_direct(g, w, m, v, name):
    def body(g_ref, w_ref, m_ref, v_ref, d_ref, mo_ref, vo_ref):
        d_ref[...], mo_ref[...], vo_ref[...] = _adamw(w_ref[...], g_ref[...], m_ref[...], v_ref[...])

    return _call(body, name=name, grid=(1,), in_specs=[_full(w.shape)] * 4, out_specs=[_full(w.shape)] * 3,
                 out_shape=[_sds(w.shape, F32)] * 3)(g, w, m, v)


def _adam_w_ada(c_act_t, d_ada_cols, w, m, v):
    r, c = w.shape
    tr = 256

    def body(ct_ref, da_ref, w_ref, m_ref, v_ref, g_ref, d_ref, mo_ref, vo_ref):
        g = ct_ref[:, pl.ds(0, 1)] * da_ref[pl.ds(0, 1), :]
        for b in range(1, N_DEV):
            g = g + ct_ref[:, pl.ds(b, 1)] * da_ref[pl.ds(b, 1), :]
        g_ref[...] = g
        d_ref[...], mo_ref[...], vo_ref[...] = _adamw(w_ref[...], g, m_ref[...], v_ref[...])

    blk = pl.BlockSpec((tr, c), lambda i: (i, 0))
    return _call(
        body, name="adam_w_ada", grid=(r // tr,), sem=("parallel",),
        in_specs=[pl.BlockSpec((tr, N_DEV), lambda i: (i, 0)), _full(d_ada_cols.shape), blk, blk, blk],
        out_specs=[blk] * 4, out_shape=[_sds((r, c), F32)] * 4,
    )(c_act_t, d_ada_cols, w, m, v)


def _adam_small(gathered, w, m, v):
    rows, lanes = w.shape

    def body(p_ref, w_ref, m_ref, v_ref, g_ref, d_ref, mo_ref, vo_ref):
        g = p_ref[0]
        for j in range(1, N_DEV):
            g = g + p_ref[j]
        g_ref[...] = g
        d_ref[...], mo_ref[...], vo_ref[...] = _adamw(w_ref[...], g, m_ref[...], v_ref[...])

    return _call(body, name="adam_small", grid=(1,),
                 in_specs=[_full(gathered.shape)] + [_full(w.shape)] * 3, out_specs=[_full(w.shape)] * 4,
                 out_shape=[_sds(w.shape, F32)] * 4)(gathered, w, m, v)


def _rope_tables(s):
    pos = jnp.arange(s, dtype=F32)
    inv = ROPE_THETA ** (-jnp.arange(0, ROPE, 2, dtype=F32) / ROPE)
    ang = pos[:, None] * inv[None, :]
    cos, sin = jnp.cos(ang), jnp.sin(ang)
    ones, zeros = jnp.ones((s, NOPE), F32), jnp.zeros((s, NOPE), F32)
    cos_t = jnp.concatenate([ones, cos, cos, ones[:, :HEAD_PAD - NOPE - ROPE]], axis=1)
    sin_t = jnp.concatenate([zeros, sin, sin, zeros[:, :HEAD_PAD - NOPE - ROPE]], axis=1)
    return cos_t, sin_t


def _pack(parts, rows):
    flat = jnp.concatenate([p.reshape(-1) for p in parts])
    return jnp.pad(flat, (0, rows * 128 - flat.shape[0])).reshape(rows, 128)


def kernel(x, c, w_ada, b_ada, g_pre_mix, g_post_mix, w_in, g_q, w_uq, g_kv, w_ukv, gm_ln_g, gm_ln_b, w_spatial, b_spatial, w_out, g_pre_ffn, g_post_ffn, w_up, conv_w, conv_b, w_down, loss_target, m_w_ada, m_b_ada, m_g_pre_mix, m_g_post_mix, m_w_in, m_g_q, m_w_uq, m_g_kv, m_w_ukv, m_gm_ln_g, m_gm_ln_b, m_w_spatial, m_b_spatial, m_w_out, m_g_pre_ffn, m_g_post_ffn, m_w_up, m_conv_w, m_conv_b, m_w_down, v_w_ada, v_b_ada, v_g_pre_mix, v_g_post_mix, v_w_in, v_g_q, v_w_uq, v_g_kv, v_w_ukv, v_gm_ln_g, v_gm_ln_b, v_w_spatial, v_b_spatial, v_w_out, v_g_pre_ffn, v_g_post_ffn, v_w_up, v_conv_w, v_conv_b, v_w_down):
    s = x.shape[1]
    tm = min(256, s)
    tq = min(256, s)
    ts = min(512, s)
    hp = HEADS * HEAD_PAD
    half = N_DEV // 2
    my_slot = 4 * lax.axis_index("x") + 2 * lax.axis_index("y") + lax.axis_index("c")
    x2d, target = x[0], loss_target[0]

    g_c, g_in, g_uq, g_ukv, g_out, g_up, g_down, g_cw = _all_gather(
        [c, w_in[0].astype(MXU), w_uq[0].astype(MXU), w_ukv[0].astype(MXU), w_out[0].astype(MXU),
         w_up[0].astype(MXU), w_down[0].astype(MXU), conv_w[0]], "gather_weights")

    w_in_f = jnp.transpose(g_in, (1, 0, 2)).reshape(D_MODEL, -1)
    o1, o2, o3 = Q_LORA, Q_LORA + KV_LORA, Q_LORA + KV_LORA + ROPE
    w1 = jnp.concatenate([w_in_f[:, :o2], jnp.zeros((D_MODEL, NOPE), MXU), w_in_f[:, o2:o3],
                          jnp.zeros((D_MODEL, HEAD_PAD - NOPE - ROPE), MXU), w_in_f[:, o3:]], axis=1)
    w_uq_f = jnp.transpose(g_uq, (1, 0, 2)).reshape(Q_LORA, HEADS, NOPE + ROPE)
    wq = jnp.pad(w_uq_f, ((0, 0), (0, 0), (0, HEAD_PAD - NOPE - ROPE))).reshape(Q_LORA, hp)
    w_ukv_f = jnp.transpose(g_ukv, (1, 0, 2)).reshape(KV_LORA, HEADS, 2 * NOPE)
    pad_head = ((0, 0), (0, 0), (0, HEAD_PAD - NOPE))
    wkv = jnp.concatenate([jnp.pad(w_ukv_f[:, :, :NOPE], pad_head).reshape(KV_LORA, hp),
                           jnp.pad(w_ukv_f[:, :, NOPE:], pad_head).reshape(KV_LORA, hp)], axis=1)
    w_out_f = g_out.reshape(2 * GM_WIDTH, D_MODEL)
    wo_attn = jnp.pad(w_out_f[:GM_WIDTH].reshape(HEADS, NOPE, D_MODEL), ((0, 0), (0, HEAD_PAD - NOPE), (0, 0)))
    wo = jnp.concatenate([wo_attn.reshape(hp, D_MODEL), w_out_f[GM_WIDTH:]], axis=0)
    wd = g_down.reshape(half, FF_BLK, D_MODEL)
    cb8 = conv_b.reshape(N_DEV, 1, FF_BLK)
    bias_exp = jnp.repeat(b_spatial[0].T, GM_DIM, axis=1)
    ln_g, ln_b = gm_ln_g.reshape(1, GM_WIDTH), gm_ln_b.reshape(1, GM_WIDTH)
    w_sp = w_spatial[0]
    cos_t, sin_t = _rope_tables(s)

    ada_part, c_act = _ada_fwd(g_c.reshape(N_DEV, D_MODEL), w_ada[0])
    ada_recv, = _all_to_all([ada_part.reshape(N_DEV, 1, -1)], "ada_rows")
    ada_raw = ada_recv.reshape(6, D_MODEL)
    ada_b = b_ada.reshape(6, D_MODEL)

    h1, z, qp, kp, vp, cqn, ckvn = _mix_in_fwd(x2d, ada_raw, ada_b, g_pre_mix, w1, g_q, g_kv, wq, wkv, cos_t, sin_t, tm)
    sgu = _gmlp_fwd(z, ln_g, ln_b, w_sp, bias_exp, tm)
    o_pad, lse = _attn_fwd(qp, kp, vp, tq)
    m_mix, x2, h2 = _out_proj_fwd(o_pad, sgu, wo, x2d, ada_raw, ada_b, g_post_mix, g_pre_ffn, tm)
    up_a, up_b, act = _ffn_up_fwd(h2, g_up, g_cw, cb8, tm)
    d_out, d_f, loss_part, d_gt2, d_g_post_ffn = _ffn_down_fwd(act, wd, x2, target, ada_raw, ada_b, g_post_ffn, tm)
    loss = lax.psum(loss_part[0, 0], ("x", "y", "c"))

    d_up_a, d_up_b, dcw_a, dcw_b, dcb_a, dcb_b = _ffn_down_bwd(d_f, wd, up_a, up_b, g_cw, cb8, tm)
    d_x2, d_m, d_sh2, d_sc2, d_g_pre_ffn, d_gt1, d_g_post_mix = _ffn_up_bwd(
        d_up_a, d_up_b, g_up, x2, m_mix, d_out, ada_raw, ada_b, g_pre_ffn, g_post_mix, tm)
    d_o, d_sgu, delta = _out_proj_bwd(d_m, wo, o_pad, tm)
    dq, dk, dv = _attn_bwd(qp, kp, vp, d_o, lse, delta, tq)
    d_guv, d_ws, d_bs, d_ln_g, d_ln_b = _gmlp_bwd(z, d_sgu, ln_g, ln_b, w_sp, bias_exp, tm)
    za = Q_LORA + KV_LORA + HEAD_PAD
    grad_x, d_za, d_qp, d_kvp, d_sh1, d_sc1, d_g_pre_mix, d_g_q, d_g_kv = _mix_in_bwd(
        dq, dk, dv, z, d_guv, x2d, d_x2, ada_raw, ada_b, g_pre_mix, g_q, g_kv, w1[:, :za], w1[:, za:], wq, wkv,
        cos_t, sin_t, tm)

    p_down = _tn_matmul(act, d_f[None], "dw_down", ts).reshape(N_DEV, -1, D_MODEL)
    h2_3 = h2[None]
    p_up = jnp.concatenate([_tn_matmul(h2_3, d_up_a, "dw_up_a", ts), _tn_matmul(h2_3, d_up_b, "dw_up_b", ts)], axis=0)
    d_m3 = d_m[None]
    dwo_attn = _tn_matmul(o_pad[None], d_m3, "dw_out_attn", ts)[0].reshape(HEADS, HEAD_PAD, D_MODEL)[:, :NOPE]
    dwo_sgu = _tn_matmul(sgu[None], d_m3, "dw_out_sgu", ts)[0]
    p_out = jnp.concatenate([dwo_attn.reshape(GM_WIDTH, D_MODEL), dwo_sgu], axis=0).reshape(N_DEV, -1, D_MODEL)
    h1_3 = h1[None]
    dw1a = _tn_matmul(h1_3, d_za[None], "dw_in_a", ts)[0]
    dw1b = _tn_matmul(h1_3, d_guv[None], "dw_in_b", ts)[0]
    d_w_in = jnp.concatenate([dw1a[:, :o2], dw1a[:, o2 + NOPE:o2 + NOPE + ROPE], dw1b], axis=1)
    p_in = jnp.transpose(d_w_in.reshape(D_MODEL, N_DEV, -1), (1, 0, 2))
    dwq = _tn_matmul(cqn[None], d_qp[None], "dw_uq", ts)[0].reshape(Q_LORA, HEADS, HEAD_PAD)[:, :, :NOPE + ROPE]
    p_uq = jnp.transpose(dwq.reshape(Q_LORA, N_DEV, -1), (1, 0, 2))
    dwkv = _tn_matmul(ckvn[None], d_kvp[None], "dw_ukv", ts)[0]
    dwk = dwkv[:, :hp].reshape(KV_LORA, HEADS, HEAD_PAD)[:, :, :NOPE]
    dwv = dwkv[:, hp:].reshape(KV_LORA, HEADS, HEAD_PAD)[:, :, :NOPE]
    p_ukv = jnp.transpose(jnp.concatenate([dwk, dwv], axis=2), (1, 0, 2))

    r_in, r_uq, r_ukv, r_out, r_up, r_down = _all_to_all([p_in, p_uq, p_ukv, p_out, p_up, p_down], "scatter_grads")

    d_ada = jnp.concatenate([d_sh1, d_sc1, d_gt1, d_sh2, d_sc2, d_gt2], axis=1)
    d_cw = jnp.concatenate([dcw_a, dcw_b], axis=0)
    d_cb = jnp.concatenate([dcb_a, dcb_b], axis=0)
    small_g = [d_ada, d_g_pre_mix, d_g_post_mix, d_g_pre_ffn, d_g_post_ffn, d_g_q, d_g_kv, d_ln_g, d_ln_b, d_ws, d_bs,
               d_cb, d_cw]
    sizes = [int(p.size) for p in small_g]
    rows = -(-sum(sizes) // (8 * 128)) * 8
    zero_cw = jnp.zeros_like(d_cw)
    small_w = [b_ada, g_pre_mix, g_post_mix, g_pre_ffn, g_post_ffn, g_q, g_kv, gm_ln_g, gm_ln_b, w_spatial, b_spatial,
               conv_b, zero_cw]
    small_m = [m_b_ada, m_g_pre_mix, m_g_post_mix, m_g_pre_ffn, m_g_post_ffn, m_g_q, m_g_kv, m_gm_ln_g, m_gm_ln_b,
               m_w_spatial, m_b_spatial, m_conv_b, zero_cw]
    small_v = [v_b_ada, v_g_pre_mix, v_g_post_mix, v_g_pre_ffn, v_g_post_ffn, v_g_q, v_g_kv, v_gm_ln_g, v_gm_ln_b,
               v_w_spatial, v_b_spatial, v_conv_b, zero_cw]
    g_small, = _all_gather([_pack(small_g, rows)], "gather_small_grads")
    sm_g, sm_d, sm_m, sm_v = _adam_small(g_small, _pack(small_w, rows), _pack(small_m, rows), _pack(small_v, rows))

    def unpack(packed, shapes):
        flat = packed.reshape(-1)
        out, off = [], 0
        for size, shape in zip(sizes, shapes):
            out.append(flat[off:off + size].reshape(shape))
            off += size
        return out

    small_shapes = [p.shape for p in small_w]
    sg, sd, smm, svv = (unpack(p, small_shapes) for p in (sm_g, sm_d, sm_m, sm_v))
    (g_b_ada, g_g_pre_mix, g_g_post_mix, g_g_pre_ffn, g_g_post_ffn, g_g_q, g_g_kv, g_ln_g, g_ln_b, g_w_sp, g_b_sp,
     g_conv_b, g_cw_all) = sg

    def big(recv, w, m, v, name):
        g, d, m2, v2 = _adam_reduce(recv, w[0], m[0], v[0], name)
        return g[None], d[None], m2[None], v2[None]

    a_in = big(r_in, w_in, m_w_in, v_w_in, "adam_w_in")
    a_uq = big(r_uq, w_uq, m_w_uq, v_w_uq, "adam_w_uq")
    a_ukv = big(r_ukv, w_ukv, m_w_ukv, v_w_ukv, "adam_w_ukv")
    a_out = big(r_out, w_out, m_w_out, v_w_out, "adam_w_out")
    a_up = big(r_up, w_up, m_w_up, v_w_up, "adam_w_up")
    a_down = big(r_down, w_down, m_w_down, v_w_down, "adam_w_down")
    ada_cols = w_ada.shape[2]
    d_ada_all = g_small.reshape(N_DEV, -1)[:, :6 * D_MODEL]
    d_ada_cols = lax.dynamic_slice(d_ada_all, (0, my_slot * ada_cols), (N_DEV, ada_cols))
    a_ada = tuple(t[None] for t in _adam_w_ada(c_act.T, d_ada_cols, w_ada[0], m_w_ada[0], v_w_ada[0]))
    g_cw_mine = lax.dynamic_slice(g_cw_all, (my_slot, 0, 0), (1, 3, FF_BLK))
    a_cw = (g_cw_mine,) + tuple(_adam_direct(g_cw_mine, conv_w, m_conv_w, v_conv_w, "adam_conv_w"))

    def small(k):
        return sg[k], sd[k], smm[k], svv[k]

    per_weight = [a_ada, small(0), small(1), small(2), a_in, small(5), a_uq, small(6), a_ukv, small(7), small(8),
                  small(9), small(10), a_out, small(3), small(4), a_up, a_cw, small(11), a_down]
    outs = [loss, grad_x[None]]
    for k in range(4):
        outs += [t[k] for t in per_weight]
    return tuple(outs)
```

```python
import functools

import jax
import jax.numpy as jnp
from jax import lax
from jax.experimental import pallas as pl
from jax.experimental.pallas import tpu as pltpu

F32 = jnp.float32
MXU = jnp.bfloat16

N_DEV = 8
D_MODEL = 1024
HEADS = 8
HEAD_PAD = 128
NOPE = 64
ROPE = 32
Q_LORA = 256
KV_LORA = 128
GM_WIDTH = 512
GM_DIM = 64
GM_CHUNK = 128
CHUNK_SHIFT = 6
ROPE_THETA = 10000.0
ATTN_SCALE = (NOPE + ROPE) ** -0.5
Z_COLS = 1536
FF_BLK = 704
EPS = 1e-6
ADAM_LR = 0.001
ADAM_B1 = 0.9
ADAM_B2 = 0.999
ADAM_EPS = 1e-08
ADAM_WD = 0.01
ADAM_STEP = 10
VMEM_LIMIT = 56 * 1024 * 1024
MESH = pl.DeviceIdType.MESH


def _dot(a, b):
    return jnp.dot(a, b, preferred_element_type=F32)


def _dot_nt(a, b):
    return lax.dot_general(a, b, (((1,), (1,)), ((), ())), preferred_element_type=F32)


def _dot_tn(a, b):
    return lax.dot_general(a, b, (((0,), (0,)), ((), ())), preferred_element_type=F32)


def _call(body, *, name, grid, in_specs, out_specs, out_shape, scratch=(), sem=None):
    params = pltpu.CompilerParams(dimension_semantics=sem, vmem_limit_bytes=VMEM_LIMIT)
    return pl.pallas_call(body, name=name, grid=grid, in_specs=in_specs, out_specs=out_specs,
                          out_shape=out_shape, scratch_shapes=list(scratch), compiler_params=params)


def _full(shape):
    n = len(shape)
    return pl.BlockSpec(shape, lambda *_: (0,) * n)


def _rows(tm, cols, col_block=0):
    return pl.BlockSpec((tm, cols), lambda i: (i, col_block))


def _sds(shape, dtype):
    return jax.ShapeDtypeStruct(shape, dtype)


def _row(ref, k):
    return ref[pl.ds(k, 1), :]


def _rms(x):
    r = lax.rsqrt(jnp.mean(x * x, axis=-1, keepdims=True) + EPS)
    return x * r, r


def _rms_bwd(d_hat, hat, r):
    return r * (d_hat - hat * jnp.mean(d_hat * hat, axis=-1, keepdims=True))


def _rope_partner(t):
    lane = lax.broadcasted_iota(jnp.int32, t.shape, 1)
    swapped = jnp.where(lane < NOPE + ROPE // 2, -pltpu.roll(t, HEAD_PAD - ROPE // 2, 1), pltpu.roll(t, ROPE // 2, 1))
    return jnp.where((lane >= NOPE) & (lane < NOPE + ROPE), swapped, 0.0)


def _rope(t, cos, sin):
    return t * cos + _rope_partner(t) * sin


def _rope_transposed(g, cos, sin):
    return g * cos - _rope_partner(g * sin)


def _gelu(x):
    return x * (0.5 * (1.0 + jnp.tanh(0.7978845608028654 * (x + 0.044715 * (x * x * x)))))


def _gelu_grad(x):
    t = jnp.tanh(0.7978845608028654 * (x + 0.044715 * (x * x * x)))
    return 0.5 * (1.0 + t) + 0.5 * x * (1.0 - t * t) * (0.7978845608028654 * (1.0 + 3.0 * 0.044715 * (x * x)))


def _split_dot(x, mat):
    hi = x.astype(MXU)
    lo = (x - hi.astype(F32)).astype(MXU)
    return _dot(hi, mat) + _dot(lo, mat)


def _seg_matrix():
    r = lax.broadcasted_iota(jnp.int32, (GM_WIDTH, GM_WIDTH), 0) >> 6
    c = lax.broadcasted_iota(jnp.int32, (GM_WIDTH, GM_WIDTH), 1) >> 6
    return jnp.where(r == c, 1.0 / GM_DIM, 0.0).astype(MXU)


def _spatial_mask():
    i = lax.broadcasted_iota(jnp.int32, (GM_CHUNK, GM_CHUNK), 0) >> CHUNK_SHIFT
    j = lax.broadcasted_iota(jnp.int32, (GM_CHUNK, GM_CHUNK), 1) >> CHUNK_SHIFT
    return (j <= i).astype(F32)


def _head_lane_mask(h, rows):
    lane = lax.broadcasted_iota(jnp.int32, (rows, GM_WIDTH), 1) >> 6
    return lane == h


def _my_place():
    return lax.axis_index("x"), lax.axis_index("y"), lax.axis_index("c")


def _flat(p):
    return 4 * p[0] + 2 * p[1] + p[2]


def _all_gather(arrays, name):
    n = len(arrays)

    def body(*refs):
        ins, outs = refs[:n], refs[n:2 * n]
        send_sems, recv_sems, local_sems = refs[2 * n:]
        x, y, c = _my_place()
        me, sibling = (x, y, c), (x, y, 1 - c)
        chips = [(1 - x, y), (x, 1 - y), (1 - x, 1 - y)]

        def copy(a, k, block, to, src=None):
            slot = outs[a].at[_flat(block)]
            return pltpu.make_async_remote_copy(
                src_ref=slot if src is None else src, dst_ref=slot,
                send_sem=send_sems.at[7 * a + k], recv_sem=recv_sems.at[7 * a + k],
                device_id=to, device_id_type=MESH)

        mine = [pltpu.make_async_copy(ins[a], outs[a].at[_flat(me)], local_sems.at[a]) for a in range(n)]
        for cp in mine:
            cp.start()
        first = []
        for a in range(n):
            first.append(copy(a, 0, me, sibling, src=ins[a]))
            first += [copy(a, 1 + j, me, (*chip, c), src=ins[a]) for j, chip in enumerate(chips)]
        for cp in first:
            cp.start()
        passed = []
        for a in range(n):
            for j, chip in enumerate(chips):
                copy(a, 1 + j, (*chip, c), me).wait_recv()
                fwd = copy(a, 4 + j, (*chip, c), sibling)
                fwd.start()
                passed.append(fwd)
        for a in range(n):
            copy(a, 0, sibling, me).wait_recv()
            for j, chip in enumerate(chips):
                copy(a, 4 + j, (*chip, 1 - c), me).wait_recv()
        for cp in first + passed:
            cp.wait_send()
        for cp in mine:
            cp.wait()

    any_spec = pl.BlockSpec(memory_space=pl.ANY)
    return pl.pallas_call(
        body, name=name,
        in_specs=[any_spec] * n, out_specs=[any_spec] * n,
        out_shape=[_sds((N_DEV,) + a.shape, a.dtype) for a in arrays],
        scratch_shapes=[pltpu.SemaphoreType.DMA((7 * n,)), pltpu.SemaphoreType.DMA((7 * n,)),
                        pltpu.SemaphoreType.DMA((n,))],
    )(*arrays)


def _all_to_all(arrays, name):
    n = len(arrays)
    flips = [(fx, fy, fc) for fx in (0, 1) for fy in (0, 1) for fc in (0, 1)][1:]

    def body(*refs):
        ins, outs = refs[:n], refs[n:2 * n]
        send_sems, recv_sems, local_sems = refs[2 * n:]
        x, y, c = _my_place()
        me = (x, y, c)

        def peer(f):
            return tuple(1 - v if b else v for v, b in zip(me, f))

        def copy(a, k):
            p = peer(flips[k])
            return pltpu.make_async_remote_copy(
                src_ref=ins[a].at[_flat(p)], dst_ref=outs[a].at[_flat(me)],
                send_sem=send_sems.at[7 * a + k], recv_sem=recv_sems.at[7 * a + k],
                device_id=p, device_id_type=MESH)

        def arrival(a, k):
            p = peer(flips[k])
            return pltpu.make_async_remote_copy(
                src_ref=ins[a].at[_flat(p)], dst_ref=outs[a].at[_flat(p)],
                send_sem=send_sems.at[7 * a + k], recv_sem=recv_sems.at[7 * a + k],
                device_id=p, device_id_type=MESH)

        mine = [pltpu.make_async_copy(ins[a].at[_flat(me)], outs[a].at[_flat(me)], local_sems.at[a]) for a in range(n)]
        for cp in mine:
            cp.start()
        sends = [copy(a, k) for a in range(n) for k in range(7)]
        for cp in sends:
            cp.start()
        for a in range(n):
            for k in range(7):
                arrival(a, k).wait_recv()
        for cp in sends:
            cp.wait_send()
        for cp in mine:
            cp.wait()

    any_spec = pl.BlockSpec(memory_space=pl.ANY)
    return pl.pallas_call(
        body, name=name,
        in_specs=[any_spec] * n, out_specs=[any_spec] * n,
        out_shape=[_sds(a.shape, a.dtype) for a in arrays],
        scratch_shapes=[pltpu.SemaphoreType.DMA((7 * n,)), pltpu.SemaphoreType.DMA((7 * n,)),
                        pltpu.SemaphoreType.DMA((n,))],
    )(*arrays)


def _ada_fwd(c_all, w_ada):
    def body(c_ref, w_ref, part_ref, act_ref):
        cv = c_ref[...]
        act = cv * jax.nn.sigmoid(cv)
        act_ref[...] = act
        part_ref[...] = _dot(act.astype(MXU), w_ref[...].astype(MXU))

    cols = w_ada.shape[1]
    return _call(body, name="ada_fwd", grid=(1,),
                 in_specs=[_full(c_all.shape), _full(w_ada.shape)],
                 out_specs=[_full((N_DEV, cols)), _full(c_all.shape)],
                 out_shape=[_sds((N_DEV, cols), F32), _sds(c_all.shape, F32)])(c_all, w_ada)


def _mix_in_fwd(x, ada_raw, ada_b, g_pre, w1, g_q, g_kv, wq, wkv, cos_t, sin_t, tm):
    s = x.shape[0]

    def body(x_ref, ar_ref, ab_ref, g_ref, w1_ref, gq_ref, gkv_ref, wq_ref, wkv_ref, cos_ref, sin_ref,
             h1_ref, z_ref, qp_ref, kp_ref, vp_ref, cqn_ref, ckvn_ref):
        sh = _row(ar_ref, 0) + _row(ab_ref, 0)
        sc = _row(ar_ref, 1) + _row(ab_ref, 1)
        xn, _ = _rms(x_ref[...])
        hb = ((xn * g_ref[...]) * (1.0 + sc) + sh).astype(MXU)
        h1_ref[...] = hb
        z = _dot(hb, w1_ref[...])
        z_ref[...] = z
        cos, sin = cos_ref[...], sin_ref[...]
        cqn = (_rms(z[:, :Q_LORA])[0] * gq_ref[...]).astype(MXU)
        ckvn = (_rms(z[:, Q_LORA:Q_LORA + KV_LORA])[0] * gkv_ref[...]).astype(MXU)
        cqn_ref[...] = cqn
        ckvn_ref[...] = ckvn
        q = _dot(cqn, wq_ref[...])
        kv = _dot(ckvn, wkv_ref[...])
        k_rope = _rope(z[:, Q_LORA + KV_LORA:Q_LORA + KV_LORA + HEAD_PAD], cos, sin)
        for h in range(HEADS):
            blk = slice(h * HEAD_PAD, (h + 1) * HEAD_PAD)
            qp_ref[:, blk] = _rope(q[:, blk], cos, sin).astype(MXU)
            kp_ref[:, blk] = (kv[:, blk] + k_rope).astype(MXU)
        v_lane = lax.broadcasted_iota(jnp.int32, (tm, HEADS * HEAD_PAD), 1) & (HEAD_PAD - 1)
        vp_ref[...] = jnp.where(v_lane == NOPE, 1.0, kv[:, HEADS * HEAD_PAD:]).astype(MXU)

    hp = HEADS * HEAD_PAD
    return _call(
        body, name="mix_in_fwd", grid=(s // tm,), sem=("parallel",),
        in_specs=[_rows(tm, D_MODEL), _full(ada_raw.shape), _full(ada_b.shape), _full(g_pre.shape), _full(w1.shape),
                  _full(g_q.shape), _full(g_kv.shape), _full(wq.shape), _full(wkv.shape),
                  _rows(tm, HEAD_PAD), _rows(tm, HEAD_PAD)],
        out_specs=[_rows(tm, D_MODEL), _rows(tm, Z_COLS), _rows(tm, hp), _rows(tm, hp), _rows(tm, hp),
                   _rows(tm, Q_LORA), _rows(tm, KV_LORA)],
        out_shape=[_sds((s, D_MODEL), MXU), _sds((s, Z_COLS), F32), _sds((s, hp), MXU), _sds((s, hp), MXU),
                   _sds((s, hp), MXU), _sds((s, Q_LORA), MXU), _sds((s, KV_LORA), MXU)],
    )(x, ada_raw, ada_b, g_pre, w1, g_q, g_kv, wq, wkv, cos_t, sin_t)


def _gm_norm(zv, seg):
    gv = _gelu(zv)
    cen = gv - _split_dot(gv, seg)
    rstd = lax.rsqrt(_split_dot(cen * cen, seg) + EPS)
    return gv, cen * rstd, rstd


def _gm_mix(wm, vb, rows):
    out = jnp.zeros((rows, GM_WIDTH), F32)
    for h in range(HEADS):
        out = out + jnp.where(_head_lane_mask(h, rows), _dot(wm[h], vb), 0.0)
    return out


def _gmlp_fwd(z, ln_g, ln_b, w_sp, bias_exp, tm):
    s = z.shape[0]
    nblk = tm // GM_CHUNK

    def body(zu_ref, zv_ref, lg_ref, lb_ref, w_ref, be_ref, sgu_ref):
        seg = _seg_matrix()
        mask = _spatial_mask()
        wm = [(w_ref[h] * mask).astype(MXU) for h in range(HEADS)]
        gu = _gelu(zu_ref[...])
        _, vhat, _ = _gm_norm(zv_ref[...], seg)
        vln = (vhat * lg_ref[...] + lb_ref[...]).astype(MXU)
        for n in range(nblk):
            rows = slice(n * GM_CHUNK, (n + 1) * GM_CHUNK)
            mixed = _gm_mix(wm, vln[rows], GM_CHUNK) + be_ref[...]
            sgu_ref[rows, :] = (gu[rows] * mixed).astype(MXU)

    return _call(
        body, name="gmlp_fwd", grid=(s // tm,), sem=("parallel",),
        in_specs=[_rows(tm, GM_WIDTH, 1), _rows(tm, GM_WIDTH, 2), _full(ln_g.shape), _full(ln_b.shape),
                  _full(w_sp.shape), _full(bias_exp.shape)],
        out_specs=_rows(tm, GM_WIDTH), out_shape=_sds((s, GM_WIDTH), MXU),
    )(z, z, ln_g, ln_b, w_sp, bias_exp)


def _diag_mask(t):
    qc = lax.broadcasted_iota(jnp.int32, (t, t), 0) >> CHUNK_SHIFT
    kc = lax.broadcasted_iota(jnp.int32, (t, t), 1) >> CHUNK_SHIFT
    return kc <= qc


NEG_BIG = -1e30
ATTN_HEADS_PER_STEP = 2


def _attn_fwd(qp, kp, vp, tq):
    s = qp.shape[0]
    nq = s // tq
    hb = ATTN_HEADS_PER_STEP
    width = hb * HEAD_PAD

    def body(q_ref, k_ref, v_ref, o_ref, lse_ref, m_sc, acc_sc):
        i = pl.program_id(1)
        m_sc[...] = jnp.full(m_sc.shape, NEG_BIG, F32)
        acc_sc[...] = jnp.zeros(acc_sc.shape, F32)

        def tile(j, masked):
            rows = pl.ds(pl.multiple_of(j * tq, tq), tq)
            for hh in range(hb):
                lanes = slice(hh * HEAD_PAD, (hh + 1) * HEAD_PAD)
                sc = _dot_nt(q_ref[:, lanes], k_ref[rows, lanes]) * ATTN_SCALE
                if masked:
                    sc = jnp.where(_diag_mask(tq), sc, NEG_BIG)
                blocks = [sc[:, b * 128:(b + 1) * 128] for b in range(tq // 128)]
                m_prev = m_sc[hh]
                m_tile = jnp.max(functools.reduce(jnp.maximum, blocks), axis=-1, keepdims=True)
                m_new = jnp.maximum(m_prev, m_tile)
                alpha = jnp.exp(m_prev - m_new)
                p = jnp.concatenate([jnp.exp(b - m_new) for b in blocks], axis=1).astype(MXU)
                acc_sc[hh] = alpha * acc_sc[hh] + _dot(p, v_ref[rows, lanes])
                m_sc[hh] = m_new

        def off_diagonal(j, carry):
            tile(j, False)
            return carry

        lax.fori_loop(0, i, off_diagonal, 0)
        tile(i, True)
        for hh in range(hb):
            lanes = slice(hh * HEAD_PAD, (hh + 1) * HEAD_PAD)
            acc = acc_sc[hh]
            denom = acc[:, NOPE:NOPE + 1]
            o_ref[:, lanes] = (acc / denom).astype(MXU)
            lse_ref[hh] = m_sc[hh][:, :1] + jnp.log(denom)

    q_spec = pl.BlockSpec((tq, width), lambda g, i: (i, g))
    kv_spec = pl.BlockSpec((s, width), lambda g, i: (0, g))
    return _call(
        body, name="attn_fwd", grid=(HEADS // hb, nq), sem=("parallel", "parallel"),
        in_specs=[q_spec, kv_spec, kv_spec],
        out_specs=[q_spec, pl.BlockSpec((hb, tq, 1), lambda g, i: (g, i, 0))],
        out_shape=[_sds(qp.shape, MXU), _sds((HEADS, s, 1), F32)],
        scratch=[pltpu.VMEM((hb, tq, HEAD_PAD), F32), pltpu.VMEM((hb, tq, HEAD_PAD), F32)],
    )(qp, kp, vp)


def _out_proj_fwd(o_pad, sgu, wo, x, ada_raw, ada_b, g_post_mix, g_pre_ffn, tm):
    s = x.shape[0]
    hp = HEADS * HEAD_PAD

    def body(o_ref, sgu_ref, wo_ref, x_ref, ar_ref, ab_ref, gpm_ref, gpf_ref, m_ref, x2_ref, h2_ref):
        gt1 = _row(ar_ref, 2) + _row(ab_ref, 2)
        sh2 = _row(ar_ref, 3) + _row(ab_ref, 3)
        sc2 = _row(ar_ref, 4) + _row(ab_ref, 4)
        m = _dot(o_ref[...], wo_ref[pl.ds(0, hp), :]) + _dot(sgu_ref[...], wo_ref[pl.ds(hp, GM_WIDTH), :])
        m_ref[...] = m
        x2 = x_ref[...] + gt1 * (_rms(m)[0] * gpm_ref[...])
        x2_ref[...] = x2
        h2_ref[...] = ((_rms(x2)[0] * gpf_ref[...]) * (1.0 + sc2) + sh2).astype(MXU)

    return _call(
        body, name="out_proj_fwd", grid=(s // tm,), sem=("parallel",),
        in_specs=[_rows(tm, hp), _rows(tm, GM_WIDTH), _full(wo.shape), _rows(tm, D_MODEL), _full(ada_raw.shape),
                  _full(ada_b.shape), _full(g_post_mix.shape), _full(g_pre_ffn.shape)],
        out_specs=[_rows(tm, D_MODEL)] * 3,
        out_shape=[_sds((s, D_MODEL), F32), _sds((s, D_MODEL), F32), _sds((s, D_MODEL), MXU)],
    )(o_pad, sgu, wo, x, ada_raw, ada_b, g_post_mix, g_pre_ffn)


def _conv_taps(u, halo):
    ext = jnp.concatenate([halo, u], axis=0)
    return pltpu.roll(ext, 1, 0)[8:], pltpu.roll(ext, 2, 0)[8:]


def _conv(u, halo, cw_ref, cb_ref):
    m1, m2 = _conv_taps(u, halo)
    y = cb_ref[0] + ((m2 * cw_ref[0, pl.ds(0, 1), :] + m1 * cw_ref[0, pl.ds(1, 1), :]) + u * cw_ref[0, pl.ds(2, 1), :])
    return y, m1, m2


def _ffn_up_fwd(h2, w_up, conv_w, conv_b, tm):
    s = h2.shape[0]
    half = N_DEV // 2

    def body(h_ref, wa_ref, wb_ref, cwa_ref, cwb_ref, cba_ref, cbb_ref, ua_ref, ub_ref, act_ref, halo_a, halo_b):
        i = pl.program_id(1)

        @pl.when(i == 0)
        def _():
            halo_a[...] = jnp.zeros(halo_a.shape, F32)
            halo_b[...] = jnp.zeros(halo_b.shape, F32)

        hb = h_ref[...]
        ua = _dot(hb, wa_ref[0])
        ub = _dot(hb, wb_ref[0])
        ua_ref[0] = ua
        ub_ref[0] = ub
        ya, _, _ = _conv(ua, halo_a[...], cwa_ref, cba_ref)
        yb, _, _ = _conv(ub, halo_b[...], cwb_ref, cbb_ref)
        halo_a[...] = ua[tm - 8:]
        halo_b[...] = ub[tm - 8:]
        act_ref[0] = ((ya * jax.nn.sigmoid(ya)) * yb).astype(MXU)

    def blk(shape, off):
        return pl.BlockSpec(shape, lambda j, i: (j + off, 0, 0))

    def tok(off=0):
        return pl.BlockSpec((1, tm, FF_BLK), lambda j, i: (j + off, i, 0))

    return _call(
        body, name="ffn_up_fwd", grid=(half, s // tm), sem=("parallel", "arbitrary"),
        in_specs=[pl.BlockSpec((tm, D_MODEL), lambda j, i: (i, 0)),
                  blk((1, D_MODEL, FF_BLK), 0), blk((1, D_MODEL, FF_BLK), half),
                  blk((1, 3, FF_BLK), 0), blk((1, 3, FF_BLK), half), blk((1, 1, FF_BLK), 0), blk((1, 1, FF_BLK), half)],
        out_specs=[tok(), tok(), tok()],
        out_shape=[_sds((half, s, FF_BLK), F32), _sds((half, s, FF_BLK), F32), _sds((half, s, FF_BLK), MXU)],
        scratch=[pltpu.VMEM((8, FF_BLK), F32), pltpu.VMEM((8, FF_BLK), F32)],
    )(h2, w_up, w_up, conv_w, conv_w, conv_b, conv_b)


def _ffn_down_fwd(act, wd, x2, target, ada_raw, ada_b, g_post_ffn, tm):
    s = x2.shape[0]
    half = N_DEV // 2

    def body(act_ref, wd_ref, x2_ref, t_ref, ar_ref, ab_ref, g_ref, dout_ref, df_ref, loss_ref, dgt_ref, dg_ref):
        i = pl.program_id(0)

        @pl.when(i == 0)
        def _():
            loss_ref[...] = jnp.zeros(loss_ref.shape, F32)
            dgt_ref[...] = jnp.zeros(dgt_ref.shape, F32)
            dg_ref[...] = jnp.zeros(dg_ref.shape, F32)

        gt2 = _row(ar_ref, 5) + _row(ab_ref, 5)
        g = g_ref[...]
        f = _dot(act_ref[0], wd_ref[0])
        for j in range(1, half):
            f = f + _dot(act_ref[j], wd_ref[j])
        fhat, rf = _rms(f)
        fn = fhat * g
        err = (x2_ref[...] + gt2 * fn) - t_ref[...]
        loss_ref[...] += 0.5 * jnp.sum(jnp.mean(err * err, axis=-1, keepdims=True))
        d_out = err * (1.0 / D_MODEL)
        dout_ref[...] = d_out
        dgt_ref[...] += jnp.sum(d_out * fn, axis=0, keepdims=True)
        d_fn = d_out * gt2
        dg_ref[...] += jnp.sum(d_fn * fhat, axis=0, keepdims=True)
        df_ref[...] = _rms_bwd(d_fn * g, fhat, rf).astype(MXU)

    vec = pl.BlockSpec((1, D_MODEL), lambda i: (0, 0))
    return _call(
        body, name="ffn_down_fwd", grid=(s // tm,), sem=("arbitrary",),
        in_specs=[pl.BlockSpec((half, tm, FF_BLK), lambda i: (0, i, 0)), _full(wd.shape), _rows(tm, D_MODEL),
                  _rows(tm, D_MODEL), _full(ada_raw.shape), _full(ada_b.shape), _full(g_post_ffn.shape)],
        out_specs=[_rows(tm, D_MODEL), _rows(tm, D_MODEL), pl.BlockSpec((1, 128), lambda i: (0, 0)), vec, vec],
        out_shape=[_sds((s, D_MODEL), F32), _sds((s, D_MODEL), MXU), _sds((1, 128), F32),
                   _sds((1, D_MODEL), F32), _sds((1, D_MODEL), F32)],
    )(act, wd, x2, target, ada_raw, ada_b, g_post_ffn)


def _ffn_down_bwd(d_f, wd, up_a, up_b, conv_w, conv_b, tm):
    s = d_f.shape[0]
    half = N_DEV // 2
    nt = s // tm
    hb = tm // 8

    def body(df_ref, wd_ref, ua_ref, ub_ref, pa_ref, pb_ref, cwa_ref, cwb_ref, cba_ref, cbb_ref,
             dua_ref, dub_ref, dcwa_ref, dcwb_ref, dcba_ref, dcbb_ref, next_a, next_b):
        i = pl.program_id(1)
        first_tile = i == nt - 1

        @pl.when(i == 0)
        def _():
            next_a[...] = jnp.zeros(next_a.shape, F32)
            next_b[...] = jnp.zeros(next_b.shape, F32)
            for r in (dcwa_ref, dcwb_ref, dcba_ref, dcbb_ref):
                r[...] = jnp.zeros(r.shape, F32)

        d_act = _dot_nt(df_ref[...], wd_ref[0])
        keep = jnp.where(first_tile, 0.0, 1.0)
        ua, ub = ua_ref[0], ub_ref[0]
        ya, ma1, ma2 = _conv(ua, pa_ref[0] * keep, cwa_ref, cba_ref)
        yb, mb1, mb2 = _conv(ub, pb_ref[0] * keep, cwb_ref, cbb_ref)
        sig = jax.nn.sigmoid(ya)
        d_ya = d_act * yb * (sig * (1.0 + ya * (1.0 - sig)))
        d_yb = d_act * (ya * sig)

        def conv_bwd(d_y, u, m1, m2, nxt, cw_ref, du_ref, dcw_ref, dcb_ref):
            ext = jnp.concatenate([d_y, nxt[...]], axis=0)
            p1 = pltpu.roll(ext, tm + 7, 0)[:tm]
            p2 = pltpu.roll(ext, tm + 6, 0)[:tm]
            d_u = (d_y * cw_ref[0, pl.ds(2, 1), :] + p1 * cw_ref[0, pl.ds(1, 1), :]) + p2 * cw_ref[0, pl.ds(0, 1), :]
            du_ref[0] = d_u.astype(MXU)
            dcb_ref[0] += jnp.sum(d_y, axis=0, keepdims=True)
            dcw_ref[0, pl.ds(0, 1), :] += jnp.sum(d_y * m2, axis=0, keepdims=True)
            dcw_ref[0, pl.ds(1, 1), :] += jnp.sum(d_y * m1, axis=0, keepdims=True)
            dcw_ref[0, pl.ds(2, 1), :] += jnp.sum(d_y * u, axis=0, keepdims=True)
            nxt[...] = d_y[:8]

        conv_bwd(d_ya, ua, ma1, ma2, next_a, cwa_ref, dua_ref, dcwa_ref, dcba_ref)
        conv_bwd(d_yb, ub, mb1, mb2, next_b, cwb_ref, dub_ref, dcwb_ref, dcbb_ref)

    def rev(i):
        return nt - 1 - i

    def blk(shape, off):
        return pl.BlockSpec(shape, lambda j, i: (j + off, 0, 0))

    tok = pl.BlockSpec((1, tm, FF_BLK), lambda j, i: (j, rev(i), 0))
    prev = pl.BlockSpec((1, 8, FF_BLK), lambda j, i: (j, jnp.maximum(rev(i) * hb - 1, 0), 0))
    acc3 = pl.BlockSpec((1, 3, FF_BLK), lambda j, i: (j, 0, 0))
    acc1 = pl.BlockSpec((1, 1, FF_BLK), lambda j, i: (j, 0, 0))
    return _call(
        body, name="ffn_down_bwd", grid=(half, nt), sem=("parallel", "arbitrary"),
        in_specs=[pl.BlockSpec((tm, D_MODEL), lambda j, i: (rev(i), 0)), blk((1, FF_BLK, D_MODEL), 0),
                  tok, tok, prev, prev,
                  blk((1, 3, FF_BLK), 0), blk((1, 3, FF_BLK), half), blk((1, 1, FF_BLK), 0), blk((1, 1, FF_BLK), half)],
        out_specs=[tok, tok, acc3, acc3, acc1, acc1],
        out_shape=[_sds((half, s, FF_BLK), MXU), _sds((half, s, FF_BLK), MXU),
                   _sds((half, 3, FF_BLK), F32), _sds((half, 3, FF_BLK), F32),
                   _sds((half, 1, FF_BLK), F32), _sds((half, 1, FF_BLK), F32)],
        scratch=[pltpu.VMEM((8, FF_BLK), F32), pltpu.VMEM((8, FF_BLK), F32)],
    )(d_f, wd, up_a, up_b, up_a, up_b, conv_w, conv_w, conv_b, conv_b)


def _ffn_up_bwd(d_up_a, d_up_b, w_up, x2, m, d_out, ada_raw, ada_b, g_pre_ffn, g_post_mix, tm):
    s = x2.shape[0]
    half = N_DEV // 2

    def body(da_ref, db_ref, w_ref, x2_ref, m_ref, dout_ref, ar_ref, ab_ref, gpf_ref, gpm_ref,
             dx_ref, dm_ref, dsh_ref, dsc_ref, dgpf_ref, dgt1_ref, dgpm_ref):
        i = pl.program_id(0)

        @pl.when(i == 0)
        def _():
            for r in (dsh_ref, dsc_ref, dgpf_ref, dgt1_ref, dgpm_ref):
                r[...] = jnp.zeros(r.shape, F32)

        gt1 = _row(ar_ref, 2) + _row(ab_ref, 2)
        sc2 = _row(ar_ref, 4) + _row(ab_ref, 4)
        gpf, gpm = gpf_ref[...], gpm_ref[...]
        d_h2 = _dot_nt(da_ref[0], w_ref[0])
        for j in range(1, half):
            d_h2 = d_h2 + _dot_nt(da_ref[j], w_ref[j])
        for j in range(half):
            d_h2 = d_h2 + _dot_nt(db_ref[j], w_ref[half + j])
        x2n, r2 = _rms(x2_ref[...])
        dsh_ref[...] += jnp.sum(d_h2, axis=0, keepdims=True)
        dsc_ref[...] += jnp.sum(d_h2 * (x2n * gpf), axis=0, keepdims=True)
        d_mod = d_h2 * (1.0 + sc2)
        dgpf_ref[...] += jnp.sum(d_mod * x2n, axis=0, keepdims=True)
        d_x2 = dout_ref[...] + _rms_bwd(d_mod * gpf, x2n, r2)
        dx_ref[...] = d_x2
        mhat, rm = _rms(m_ref[...])
        dgt1_ref[...] += jnp.sum(d_x2 * (mhat * gpm), axis=0, keepdims=True)
        d_mn = d_x2 * gt1
        dgpm_ref[...] += jnp.sum(d_mn * mhat, axis=0, keepdims=True)
        dm_ref[...] = _rms_bwd(d_mn * gpm, mhat, rm).astype(MXU)

    vec = pl.BlockSpec((1, D_MODEL), lambda i: (0, 0))
    tok = pl.BlockSpec((half, tm, FF_BLK), lambda i: (0, i, 0))
    return _call(
        body, name="ffn_up_bwd", grid=(s // tm,), sem=("arbitrary",),
        in_specs=[tok, tok, _full(w_up.shape), _rows(tm, D_MODEL), _rows(tm, D_MODEL), _rows(tm, D_MODEL),
                  _full(ada_raw.shape), _full(ada_b.shape), _full(g_pre_ffn.shape), _full(g_post_mix.shape)],
        out_specs=[_rows(tm, D_MODEL), _rows(tm, D_MODEL), vec, vec, vec, vec, vec],
        out_shape=[_sds((s, D_MODEL), F32), _sds((s, D_MODEL), MXU)] + [_sds((1, D_MODEL), F32)] * 5,
    )(d_up_a, d_up_b, w_up, x2, m, d_out, ada_raw, ada_b, g_pre_ffn, g_post_mix)


def _out_proj_bwd(d_m, wo, o_pad, tm):
    s = d_m.shape[0]
    hp = HEADS * HEAD_PAD

    def body(dm_ref, wo_ref, o_ref, do_ref, dsgu_ref, delta_ref):
        d_cat = _dot_nt(dm_ref[...], wo_ref[...])
        d_o = d_cat[:, :hp]
        do_ref[...] = d_o.astype(MXU)
        dsgu_ref[...] = d_cat[:, hp:]
        prod = d_o * o_ref[...].astype(F32)
        for h in range(HEADS):
            delta_ref[h] = jnp.sum(prod[:, h * HEAD_PAD:(h + 1) * HEAD_PAD], axis=-1, keepdims=True)

    return _call(
        body, name="out_proj_bwd", grid=(s // tm,), sem=("parallel",),
        in_specs=[_rows(tm, D_MODEL), _full(wo.shape), _rows(tm, hp)],
        out_specs=[_rows(tm, hp), _rows(tm, GM_WIDTH), pl.BlockSpec((HEADS, tm, 1), lambda i: (0, i, 0))],
        out_shape=[_sds((s, hp), MXU), _sds((s, GM_WIDTH), F32), _sds((HEADS, s, 1), F32)],
    )(d_m, wo, o_pad)


def _attn_bwd(qp, kp, vp, d_o, lse, delta, tq):
    s = qp.shape[0]
    nq = s // tq
    hb = ATTN_HEADS_PER_STEP
    width = hb * HEAD_PAD

    def body(q_ref, k_ref, v_ref, do_ref, lse_ref, dl_ref, dq_ref, dk_ref, dv_ref, dk_sc, dv_sc):
        j = pl.program_id(1)

        @pl.when(j == 0)
        def _():
            dq_ref[...] = jnp.zeros(dq_ref.shape, F32)

        dk_sc[...] = jnp.zeros(dk_sc.shape, F32)
        dv_sc[...] = jnp.zeros(dv_sc.shape, F32)

        def tile(i, masked):
            rows = pl.ds(pl.multiple_of(i * tq, tq), tq)
            for hh in range(hb):
                lanes = slice(hh * HEAD_PAD, (hh + 1) * HEAD_PAD)
                q, do, k = q_ref[rows, lanes], do_ref[rows, lanes], k_ref[:, lanes]
                sc = _dot_nt(q, k) * ATTN_SCALE
                if masked:
                    sc = jnp.where(_diag_mask(tq), sc, NEG_BIG)
                p = jnp.exp(sc - lse_ref[hh, rows, :])
                dv_sc[hh] += _dot_tn(p.astype(MXU), do)
                dp = _dot_nt(do, v_ref[:, lanes])
                ds = ((p * (dp - dl_ref[hh, rows, :])) * ATTN_SCALE).astype(MXU)
                dk_sc[hh] += _dot_tn(ds, q)
                dq_ref[rows, lanes] += _dot(ds, k)

        def off_diagonal(i, carry):
            tile(i, False)
            return carry

        tile(j, True)
        lax.fori_loop(j + 1, nq, off_diagonal, 0)
        for hh in range(hb):
            lanes = slice(hh * HEAD_PAD, (hh + 1) * HEAD_PAD)
            dk_ref[:, lanes] = dk_sc[hh]
            dv_ref[:, lanes] = dv_sc[hh]

    seq_spec = pl.BlockSpec((s, width), lambda g, j: (0, g))
    kv_spec = pl.BlockSpec((tq, width), lambda g, j: (j, g))
    col_spec = pl.BlockSpec((hb, s, 1), lambda g, j: (g, 0, 0))
    return _call(
        body, name="attn_bwd", grid=(HEADS // hb, nq), sem=("parallel", "arbitrary"),
        in_specs=[seq_spec, kv_spec, kv_spec, seq_spec, col_spec, col_spec],
        out_specs=[seq_spec, kv_spec, kv_spec],
        out_shape=[_sds(qp.shape, F32), _sds(qp.shape, F32), _sds(qp.shape, F32)],
        scratch=[pltpu.VMEM((hb, tq, HEAD_PAD), F32), pltpu.VMEM((hb, tq, HEAD_PAD), F32)],
    )(qp, kp, vp, d_o, lse, delta)


def _gmlp_bwd(z, d_sgu, ln_g, ln_b, w_sp, bias_exp, tm):
    s = z.shape[0]
    nblk = tm // GM_CHUNK

    def body(zu_ref, zv_ref, dsgu_ref, lg_ref, lb_ref, w_ref, be_ref,
             dguv_ref, dws_ref, dbs_ref, dlg_ref, dlb_ref, dbe_sc, dvln_sc):
        i = pl.program_id(0)

        @pl.when(i == 0)
        def _():
            for r in (dws_ref, dlg_ref, dlb_ref, dbe_sc):
                r[...] = jnp.zeros(r.shape, F32)

        seg = _seg_matrix()
        mask = _spatial_mask()
        wm = [(w_ref[h] * mask).astype(MXU) for h in range(HEADS)]
        zu, zv = zu_ref[...], zv_ref[...]
        gu = _gelu(zu)
        _, vhat, rstd = _gm_norm(zv, seg)
        lg = lg_ref[...]
        vln = (vhat * lg + lb_ref[...]).astype(MXU)
        d_sgu = dsgu_ref[...]
        for n in range(nblk):
            rows = slice(n * GM_CHUNK, (n + 1) * GM_CHUNK)
            vb = vln[rows]
            mixed = _gm_mix(wm, vb, GM_CHUNK) + be_ref[...]
            d_mixed = d_sgu[rows] * gu[rows]
            dguv_ref[rows, pl.ds(0, GM_WIDTH)] = ((d_sgu[rows] * mixed) * _gelu_grad(zu[rows])).astype(MXU)
            dbe_sc[...] += d_mixed
            dmb = d_mixed.astype(MXU)
            d_vln = jnp.zeros((GM_CHUNK, GM_WIDTH), F32)
            for h in range(HEADS):
                hm = _head_lane_mask(h, GM_CHUNK)
                dws_ref[h] += _dot_nt(jnp.where(hm, dmb, jnp.zeros_like(dmb)), vb)
                d_vln = d_vln + jnp.where(hm, _dot_tn(wm[h], dmb), 0.0)
            dvln_sc[rows, :] = d_vln
        d_vln = dvln_sc[...]
        dlg_ref[...] += jnp.sum(d_vln * vhat, axis=0, keepdims=True)
        dlb_ref[...] += jnp.sum(d_vln, axis=0, keepdims=True)
        d_vhat = d_vln * lg
        d_gv = rstd * ((d_vhat - _split_dot(d_vhat, seg)) - vhat * _split_dot(d_vhat * vhat, seg))
        dguv_ref[:, pl.ds(GM_WIDTH, GM_WIDTH)] = (d_gv * _gelu_grad(zv)).astype(MXU)

        @pl.when(i == pl.num_programs(0) - 1)
        def _():
            for h in range(HEADS):
                dws_ref[h] = dws_ref[h] * mask
            hrow = lax.broadcasted_iota(jnp.int32, (HEADS, GM_WIDTH), 0)
            hlane = lax.broadcasted_iota(jnp.int32, (HEADS, GM_WIDTH), 1) >> 6
            ind = jnp.where(hrow == hlane, 1.0, 0.0).astype(MXU)
            acc = dbe_sc[...]
            hi = acc.astype(MXU)
            lo = (acc - hi.astype(F32)).astype(MXU)
            dbs_ref[...] = _dot_nt(ind, hi) + _dot_nt(ind, lo)

    return _call(
        body, name="gmlp_bwd", grid=(s // tm,), sem=("arbitrary",),
        in_specs=[_rows(tm, GM_WIDTH, 1), _rows(tm, GM_WIDTH, 2), _rows(tm, GM_WIDTH), _full(ln_g.shape),
                  _full(ln_b.shape), _full(w_sp.shape), _full(bias_exp.shape)],
        out_specs=[_rows(tm, 2 * GM_WIDTH), _full(w_sp.shape), _full((HEADS, GM_CHUNK)), _full(ln_g.shape),
                   _full(ln_b.shape)],
        out_shape=[_sds((s, 2 * GM_WIDTH), MXU), _sds(w_sp.shape, F32), _sds((HEADS, GM_CHUNK), F32),
                   _sds(ln_g.shape, F32), _sds(ln_b.shape, F32)],
        scratch=[pltpu.VMEM((GM_CHUNK, GM_WIDTH), F32), pltpu.VMEM((tm, GM_WIDTH), F32)],
    )(z, z, d_sgu, ln_g, ln_b, w_sp, bias_exp)


def _mix_in_bwd(dq, dk, dv, z, d_guv, x, d_x_part, ada_raw, ada_b, g_pre, g_q, g_kv, w1a, w1b, wq, wkv,
                cos_t, sin_t, tm):
    s = x.shape[0]
    hp = HEADS * HEAD_PAD
    za = Q_LORA + KV_LORA + HEAD_PAD

    def body(dq_ref, dk_ref, dv_ref, z_ref, dguv_ref, x_ref, dxp_ref, ar_ref, ab_ref, g_ref, gq_ref, gkv_ref,
             w1a_ref, w1b_ref, wq_ref, wkv_ref, cos_ref, sin_ref,
             gx_ref, dza_ref, dqp_ref, dkvp_ref, dsh_ref, dsc_ref, dg_ref, dgq_ref, dgkv_ref):
        i = pl.program_id(0)

        @pl.when(i == 0)
        def _():
            for r in (dsh_ref, dsc_ref, dg_ref, dgq_ref, dgkv_ref):
                r[...] = jnp.zeros(r.shape, F32)

        cos, sin = cos_ref[...], sin_ref[...]
        d_krot = jnp.zeros((tm, HEAD_PAD), F32)
        for h in range(HEADS):
            blk = slice(h * HEAD_PAD, (h + 1) * HEAD_PAD)
            dqp_ref[:, blk] = _rope_transposed(dq_ref[:, blk], cos, sin).astype(MXU)
            dk_h = dk_ref[:, blk]
            d_krot = d_krot + dk_h
            dkvp_ref[:, blk] = dk_h.astype(MXU)
        dkvp_ref[:, pl.ds(hp, hp)] = dv_ref[...].astype(MXU)
        lane = lax.broadcasted_iota(jnp.int32, (tm, HEAD_PAD), 1)
        d_kr = jnp.where((lane >= NOPE) & (lane < NOPE + ROPE), _rope_transposed(d_krot, cos, sin), 0.0)
        d_cqn = _dot_nt(dqp_ref[...], wq_ref[...])
        d_ckvn = _dot_nt(dkvp_ref[...], wkv_ref[...])
        zt = z_ref[...]
        gq, gkv = gq_ref[...], gkv_ref[...]
        cq_hat, rq = _rms(zt[:, :Q_LORA])
        ckv_hat, rkv = _rms(zt[:, Q_LORA:Q_LORA + KV_LORA])
        dgq_ref[...] += jnp.sum(d_cqn * cq_hat, axis=0, keepdims=True)
        dgkv_ref[...] += jnp.sum(d_ckvn * ckv_hat, axis=0, keepdims=True)
        d_cq = _rms_bwd(d_cqn * gq, cq_hat, rq)
        d_ckv = _rms_bwd(d_ckvn * gkv, ckv_hat, rkv)
        d_za = jnp.concatenate([d_cq, d_ckv, d_kr], axis=1).astype(MXU)
        dza_ref[...] = d_za
        d_h1 = _dot_nt(d_za, w1a_ref[...]) + _dot_nt(dguv_ref[...], w1b_ref[...])
        sc1 = _row(ar_ref, 1) + _row(ab_ref, 1)
        g = g_ref[...]
        xn, r1 = _rms(x_ref[...])
        dsh_ref[...] += jnp.sum(d_h1, axis=0, keepdims=True)
        dsc_ref[...] += jnp.sum(d_h1 * (xn * g), axis=0, keepdims=True)
        d_mod = d_h1 * (1.0 + sc1)
        dg_ref[...] += jnp.sum(d_mod * xn, axis=0, keepdims=True)
        gx_ref[...] = dxp_ref[...] + _rms_bwd(d_mod * g, xn, r1)

    vec = pl.BlockSpec((1, D_MODEL), lambda i: (0, 0))
    return _call(
        body, name="mix_in_bwd", grid=(s // tm,), sem=("arbitrary",),
        in_specs=[_rows(tm, hp), _rows(tm, hp), _rows(tm, hp), _rows(tm, za), _rows(tm, 2 * GM_WIDTH),
                  _rows(tm, D_MODEL), _rows(tm, D_MODEL), _full(ada_raw.shape), _full(ada_b.shape), _full(g_pre.shape),
                  _full(g_q.shape), _full(g_kv.shape), _full(w1a.shape), _full(w1b.shape), _full(wq.shape),
                  _full(wkv.shape), _rows(tm, HEAD_PAD), _rows(tm, HEAD_PAD)],
        out_specs=[_rows(tm, D_MODEL), _rows(tm, za), _rows(tm, hp), _rows(tm, 2 * hp), vec, vec, vec,
                   _full(g_q.shape), _full(g_kv.shape)],
        out_shape=[_sds((s, D_MODEL), F32), _sds((s, za), MXU), _sds((s, hp), MXU), _sds((s, 2 * hp), MXU),
                   _sds((1, D_MODEL), F32), _sds((1, D_MODEL), F32), _sds((1, D_MODEL), F32),
                   _sds(g_q.shape, F32), _sds(g_kv.shape, F32)],
    )(dq, dk, dv, z, d_guv, x, d_x_part, ada_raw, ada_b, g_pre, g_q, g_kv, w1a, w1b, wq, wkv, cos_t, sin_t)


def _tn_matmul(a, b, name, ts):
    ga, s, m = a.shape
    gb, _, n = b.shape
    g = max(ga, gb)
    tn = n if n <= 1024 else 1024
    steps = s // ts

    def body(a_ref, b_ref, o_ref, acc):
        k = pl.program_id(2)

        @pl.when(k == 0)
        def _():
            acc[...] = jnp.zeros(acc.shape, F32)

        acc[...] += _dot_tn(a_ref[0], b_ref[0])

        @pl.when(k == steps - 1)
        def _():
            o_ref[0] = acc[...].astype(MXU)

    return _call(
        body, name=name, grid=(g, n // tn, steps), sem=("parallel", "parallel", "arbitrary"),
        in_specs=[pl.BlockSpec((1, ts, m), lambda gi, ni, k: (gi if ga > 1 else 0, k, 0)),
                  pl.BlockSpec((1, ts, tn), lambda gi, ni, k: (gi if gb > 1 else 0, k, ni))],
        out_specs=pl.BlockSpec((1, m, tn), lambda gi, ni, k: (gi, 0, ni)),
        out_shape=_sds((g, m, n), MXU),
        scratch=[pltpu.VMEM((m, tn), F32)],
    )(a, b)


def _adamw(w, g, m, v):
    m2 = ADAM_B1 * m + (1.0 - ADAM_B1) * g
    v2 = ADAM_B2 * v + (1.0 - ADAM_B2) * (g * g)
    m_hat = m2 / (1.0 - ADAM_B1 ** ADAM_STEP)
    v_hat = v2 / (1.0 - ADAM_B2 ** ADAM_STEP)
    delta = -ADAM_LR * (m_hat / (jnp.sqrt(v_hat) + ADAM_EPS) + ADAM_WD * w)
    return delta, m2, v2


def _adam_reduce(recv, w, m, v, name):
    r, c = w.shape
    tr = r if r <= 512 else 256

    def body(p_ref, w_ref, m_ref, v_ref, g_ref, d_ref, mo_ref, vo_ref):
        g = p_ref[0].astype(F32)
        for j in range(1, N_DEV):
            g = g + p_ref[j].astype(F32)
        g_ref[...] = g
        d_ref[...], mo_ref[...], vo_ref[...] = _adamw(w_ref[...], g, m_ref[...], v_ref[...])

    blk = pl.BlockSpec((tr, c), lambda i: (i, 0))
    return _call(
        body, name=name, grid=(r // tr,), sem=("parallel",),
        in_specs=[pl.BlockSpec((N_DEV, tr, c), lambda i: (0, i, 0)), blk, blk, blk],
        out_specs=[blk] * 4, out_shape=[_sds((r, c), F32)] * 4,
    )(recv, w, m, v)


def _adam_direct(g, w, m, v, name):
    def body(g_ref, w_ref, m_ref, v_ref, d_ref, mo_ref, vo_ref):
        d_ref[...], mo_ref[...], vo_ref[...] = _adamw(w_ref[...], g_ref[...], m_ref[...], v_ref[...])

    return _call(body, name=name, grid=(1,), in_specs=[_full(w.shape)] * 4, out_specs=[_full(w.shape)] * 3,
                 out_shape=[_sds(w.shape, F32)] * 3)(g, w, m, v)


def _adam_w_ada(c_act_t, d_ada_cols, w, m, v):
    r, c = w.shape
    tr = 256

    def body(ct_ref, da_ref, w_ref, m_ref, v_ref, g_ref, d_ref, mo_ref, vo_ref):
        g = ct_ref[:, pl.ds(0, 1)] * da_ref[pl.ds(0, 1), :]
        for b in range(1, N_DEV):
            g = g + ct_ref[:, pl.ds(b, 1)] * da_ref[pl.ds(b, 1), :]
        g_ref[...] = g
        d_ref[...], mo_ref[...], vo_ref[...] = _adamw(w_ref[...], g, m_ref[...], v_ref[...])

    blk = pl.BlockSpec((tr, c), lambda i: (i, 0))
    return _call(
        body, name="adam_w_ada", grid=(r // tr,), sem=("parallel",),
        in_specs=[pl.BlockSpec((tr, N_DEV), lambda i: (i, 0)), _full(d_ada_cols.shape), blk, blk, blk],
        out_specs=[blk] * 4, out_shape=[_sds((r, c), F32)] * 4,
    )(c_act_t, d_ada_cols, w, m, v)


def _adam_small(gathered, w, m, v):
    rows, lanes = w.shape

    def body(p_ref, w_ref, m_ref, v_ref, g_ref, d_ref, mo_ref, vo_ref):
        g = p_ref[0]
        for j in range(1, N_DEV):
            g = g + p_ref[j]
        g_ref[...] = g
        d_ref[...], mo_ref[...], vo_ref[...] = _adamw(w_ref[...], g, m_ref[...], v_ref[...])

    return _call(body, name="adam_small", grid=(1,),
                 in_specs=[_full(gathered.shape)] + [_full(w.shape)] * 3, out_specs=[_full(w.shape)] * 4,
                 out_shape=[_sds(w.shape, F32)] * 4)(gathered, w, m, v)


def _rope_tables(s):
    pos = jnp.arange(s, dtype=F32)
    inv = ROPE_THETA ** (-jnp.arange(0, ROPE, 2, dtype=F32) / ROPE)
    ang = pos[:, None] * inv[None, :]
    cos, sin = jnp.cos(ang), jnp.sin(ang)
    ones, zeros = jnp.ones((s, NOPE), F32), jnp.zeros((s, NOPE), F32)
    cos_t = jnp.concatenate([ones, cos, cos, ones[:, :HEAD_PAD - NOPE - ROPE]], axis=1)
    sin_t = jnp.concatenate([zeros, sin, sin, zeros[:, :HEAD_PAD - NOPE - ROPE]], axis=1)
    return cos_t, sin_t


def _pack(parts, rows):
    flat = jnp.concatenate([p.reshape(-1) for p in parts])
    return jnp.pad(flat, (0, rows * 128 - flat.shape[0])).reshape(rows, 128)


def kernel(x, c, w_ada, b_ada, g_pre_mix, g_post_mix, w_in, g_q, w_uq, g_kv, w_ukv, gm_ln_g, gm_ln_b, w_spatial, b_spatial, w_out, g_pre_ffn, g_post_ffn, w_up, conv_w, conv_b, w_down, loss_target, m_w_ada, m_b_ada, m_g_pre_mix, m_g_post_mix, m_w_in, m_g_q, m_w_uq, m_g_kv, m_w_ukv, m_gm_ln_g, m_gm_ln_b, m_w_spatial, m_b_spatial, m_w_out, m_g_pre_ffn, m_g_post_ffn, m_w_up, m_conv_w, m_conv_b, m_w_down, v_w_ada, v_b_ada, v_g_pre_mix, v_g_post_mix, v_w_in, v_g_q, v_w_uq, v_g_kv, v_w_ukv, v_gm_ln_g, v_gm_ln_b, v_w_spatial, v_b_spatial, v_w_out, v_g_pre_ffn, v_g_post_ffn, v_w_up, v_conv_w, v_conv_b, v_w_down):
    s = x.shape[1]
    tm = min(256, s)
    tq = min(512, s)
    ts = min(512, s)
    hp = HEADS * HEAD_PAD
    half = N_DEV // 2
    my_slot = 4 * lax.axis_index("x") + 2 * lax.axis_index("y") + lax.axis_index("c")
    x2d, target = x[0], loss_target[0]

    g_c, g_in, g_uq, g_ukv, g_out, g_up, g_down, g_cw = _all_gather(
        [c, w_in[0].astype(MXU), w_uq[0].astype(MXU), w_ukv[0].astype(MXU), w_out[0].astype(MXU),
         w_up[0].astype(MXU), w_down[0].astype(MXU), conv_w[0]], "gather_weights")

    w_in_f = jnp.transpose(g_in, (1, 0, 2)).reshape(D_MODEL, -1)
    o1, o2, o3 = Q_LORA, Q_LORA + KV_LORA, Q_LORA + KV_LORA + ROPE
    w1 = jnp.concatenate([w_in_f[:, :o2], jnp.zeros((D_MODEL, NOPE), MXU), w_in_f[:, o2:o3],
                          jnp.zeros((D_MODEL, HEAD_PAD - NOPE - ROPE), MXU), w_in_f[:, o3:]], axis=1)
    w_uq_f = jnp.transpose(g_uq, (1, 0, 2)).reshape(Q_LORA, HEADS, NOPE + ROPE)
    wq = jnp.pad(w_uq_f, ((0, 0), (0, 0), (0, HEAD_PAD - NOPE - ROPE))).reshape(Q_LORA, hp)
    w_ukv_f = jnp.transpose(g_ukv, (1, 0, 2)).reshape(KV_LORA, HEADS, 2 * NOPE)
    pad_head = ((0, 0), (0, 0), (0, HEAD_PAD - NOPE))
    wkv = jnp.concatenate([jnp.pad(w_ukv_f[:, :, :NOPE], pad_head).reshape(KV_LORA, hp),
                           jnp.pad(w_ukv_f[:, :, NOPE:], pad_head).reshape(KV_LORA, hp)], axis=1)
    w_out_f = g_out.reshape(2 * GM_WIDTH, D_MODEL)
    wo_attn = jnp.pad(w_out_f[:GM_WIDTH].reshape(HEADS, NOPE, D_MODEL), ((0, 0), (0, HEAD_PAD - NOPE), (0, 0)))
    wo = jnp.concatenate([wo_attn.reshape(hp, D_MODEL), w_out_f[GM_WIDTH:]], axis=0)
    wd = g_down.reshape(half, FF_BLK, D_MODEL)
    cb8 = conv_b.reshape(N_DEV, 1, FF_BLK)
    bias_exp = jnp.repeat(b_spatial[0].T, GM_DIM, axis=1)
    ln_g, ln_b = gm_ln_g.reshape(1, GM_WIDTH), gm_ln_b.reshape(1, GM_WIDTH)
    w_sp = w_spatial[0]
    cos_t, sin_t = _rope_tables(s)

    ada_part, c_act = _ada_fwd(g_c.reshape(N_DEV, D_MODEL), w_ada[0])
    ada_recv, = _all_to_all([ada_part.reshape(N_DEV, 1, -1)], "ada_rows")
    ada_raw = ada_recv.reshape(6, D_MODEL)
    ada_b = b_ada.reshape(6, D_MODEL)

    h1, z, qp, kp, vp, cqn, ckvn = _mix_in_fwd(x2d, ada_raw, ada_b, g_pre_mix, w1, g_q, g_kv, wq, wkv, cos_t, sin_t, tm)
    sgu = _gmlp_fwd(z, ln_g, ln_b, w_sp, bias_exp, tm)
    o_pad, lse = _attn_fwd(qp, kp, vp, tq)
    m_mix, x2, h2 = _out_proj_fwd(o_pad, sgu, wo, x2d, ada_raw, ada_b, g_post_mix, g_pre_ffn, tm)
    up_a, up_b, act = _ffn_up_fwd(h2, g_up, g_cw, cb8, tm)
    d_out, d_f, loss_part, d_gt2, d_g_post_ffn = _ffn_down_fwd(act, wd, x2, target, ada_raw, ada_b, g_post_ffn, tm)
    loss = lax.psum(loss_part[0, 0], ("x", "y", "c"))

    d_up_a, d_up_b, dcw_a, dcw_b, dcb_a, dcb_b = _ffn_down_bwd(d_f, wd, up_a, up_b, g_cw, cb8, tm)
    d_x2, d_m, d_sh2, d_sc2, d_g_pre_ffn, d_gt1, d_g_post_mix = _ffn_up_bwd(
        d_up_a, d_up_b, g_up, x2, m_mix, d_out, ada_raw, ada_b, g_pre_ffn, g_post_mix, tm)
    d_o, d_sgu, delta = _out_proj_bwd(d_m, wo, o_pad, tm)
    dq, dk, dv = _attn_bwd(qp, kp, vp, d_o, lse, delta, tq)
    d_guv, d_ws, d_bs, d_ln_g, d_ln_b = _gmlp_bwd(z, d_sgu, ln_g, ln_b, w_sp, bias_exp, tm)
    za = Q_LORA + KV_LORA + HEAD_PAD
    grad_x, d_za, d_qp, d_kvp, d_sh1, d_sc1, d_g_pre_mix, d_g_q, d_g_kv = _mix_in_bwd(
        dq, dk, dv, z, d_guv, x2d, d_x2, ada_raw, ada_b, g_pre_mix, g_q, g_kv, w1[:, :za], w1[:, za:], wq, wkv,
        cos_t, sin_t, tm)

    p_down = _tn_matmul(act, d_f[None], "dw_down", ts).reshape(N_DEV, -1, D_MODEL)
    h2_3 = h2[None]
    p_up = jnp.concatenate([_tn_matmul(h2_3, d_up_a, "dw_up_a", ts), _tn_matmul(h2_3, d_up_b, "dw_up_b", ts)], axis=0)
    d_m3 = d_m[None]
    dwo_attn = _tn_matmul(o_pad[None], d_m3, "dw_out_attn", ts)[0].reshape(HEADS, HEAD_PAD, D_MODEL)[:, :NOPE]
    dwo_sgu = _tn_matmul(sgu[None], d_m3, "dw_out_sgu", ts)[0]
    p_out = jnp.concatenate([dwo_attn.reshape(GM_WIDTH, D_MODEL), dwo_sgu], axis=0).reshape(N_DEV, -1, D_MODEL)
    h1_3 = h1[None]
    dw1a = _tn_matmul(h1_3, d_za[None], "dw_in_a", ts)[0]
    dw1b = _tn_matmul(h1_3, d_guv[None], "dw_in_b", ts)[0]
    d_w_in = jnp.concatenate([dw1a[:, :o2], dw1a[:, o2 + NOPE:o2 + NOPE + ROPE], dw1b], axis=1)
    p_in = jnp.transpose(d_w_in.reshape(D_MODEL, N_DEV, -1), (1, 0, 2))
    dwq = _tn_matmul(cqn[None], d_qp[None], "dw_uq", ts)[0].reshape(Q_LORA, HEADS, HEAD_PAD)[:, :, :NOPE + ROPE]
    p_uq = jnp.transpose(dwq.reshape(Q_LORA, N_DEV, -1), (1, 0, 2))
    dwkv = _tn_matmul(ckvn[None], d_kvp[None], "dw_ukv", ts)[0]
    dwk = dwkv[:, :hp].reshape(KV_LORA, HEADS, HEAD_PAD)[:, :, :NOPE]
    dwv = dwkv[:, hp:].reshape(KV_LORA, HEADS, HEAD_PAD)[:, :, :NOPE]
    p_ukv = jnp.transpose(jnp.concatenate([dwk, dwv], axis=2), (1, 0, 2))

    r_in, r_uq, r_ukv, r_out, r_up, r_down = _all_to_all([p_in, p_uq, p_ukv, p_out, p_up, p_down], "scatter_grads")

    d_ada = jnp.concatenate([d_sh1, d_sc1, d_gt1, d_sh2, d_sc2, d_gt2], axis=1)
    d_cw = jnp.concatenate([dcw_a, dcw_b], axis=0)
    d_cb = jnp.concatenate([dcb_a, dcb_b], axis=0)
    small_g = [d_ada, d_g_pre_mix, d_g_post_mix, d_g_pre_ffn, d_g_post_ffn, d_g_q, d_g_kv, d_ln_g, d_ln_b, d_ws, d_bs,
               d_cb, d_cw]
    sizes = [int(p.size) for p in small_g]
    rows = -(-sum(sizes) // (8 * 128)) * 8
    zero_cw = jnp.zeros_like(d_cw)
    small_w = [b_ada, g_pre_mix, g_post_mix, g_pre_ffn, g_post_ffn, g_q, g_kv, gm_ln_g, gm_ln_b, w_spatial, b_spatial,
               conv_b, zero_cw]
    small_m = [m_b_ada, m_g_pre_mix, m_g_post_mix, m_g_pre_ffn, m_g_post_ffn, m_g_q, m_g_kv, m_gm_ln_g, m_gm_ln_b,
               m_w_spatial, m_b_spatial, m_conv_b, zero_cw]
    small_v = [v_b_ada, v_g_pre_mix, v_g_post_mix, v_g_pre_ffn, v_g_post_ffn, v_g_q, v_g_kv, v_gm_ln_g, v_gm_ln_b,
               v_w_spatial, v_b_spatial, v_conv_b, zero_cw]
    g_small, = _all_gather([_pack(small_g, rows)], "gather_small_grads")
    sm_g, sm_d, sm_m, sm_v = _adam_small(g_small, _pack(small_w, rows), _pack(small_m, rows), _pack(small_v, rows))

    def unpack(packed, shapes):
        flat = packed.reshape(-1)
        out, off = [], 0
        for size, shape in zip(sizes, shapes):
            out.append(flat[off:off + size].reshape(shape))
            off += size
        return out

    small_shapes = [p.shape for p in small_w]
    sg, sd, smm, svv = (unpack(p, small_shapes) for p in (sm_g, sm_d, sm_m, sm_v))
    (g_b_ada, g_g_pre_mix, g_g_post_mix, g_g_pre_ffn, g_g_post_ffn, g_g_q, g_g_kv, g_ln_g, g_ln_b, g_w_sp, g_b_sp,
     g_conv_b, g_cw_all) = sg

    def big(recv, w, m, v, name):
        g, d, m2, v2 = _adam_reduce(recv, w[0], m[0], v[0], name)
        return g[None], d[None], m2[None], v2[None]

    a_in = big(r_in, w_in, m_w_in, v_w_in, "adam_w_in")
    a_uq = big(r_uq, w_uq, m_w_uq, v_w_uq, "adam_w_uq")
    a_ukv = big(r_ukv, w_ukv, m_w_ukv, v_w_ukv, "adam_w_ukv")
    a_out = big(r_out, w_out, m_w_out, v_w_out, "adam_w_out")
    a_up = big(r_up, w_up, m_w_up, v_w_up, "adam_w_up")
    a_down = big(r_down, w_down, m_w_down, v_w_down, "adam_w_down")
    ada_cols = w_ada.shape[2]
    d_ada_all = g_small.reshape(N_DEV, -1)[:, :6 * D_MODEL]
    d_ada_cols = lax.dynamic_slice(d_ada_all, (0, my_slot * ada_cols), (N_DEV, ada_cols))
    a_ada = tuple(t[None] for t in _adam_w_ada(c_act.T, d_ada_cols, w_ada[0], m_w_ada[0], v_w_ada[0]))
    g_cw_mine = lax.dynamic_slice(g_cw_all, (my_slot, 0, 0), (1, 3, FF_BLK))
    a_cw = (g_cw_mine,) + tuple(_adam_direct(g_cw_mine, conv_w, m_conv_w, v_conv_w, "adam_conv_w"))

    def small(k):
        return sg[k], sd[k], smm[k], svv[k]

    per_weight = [a_ada, small(0), small(1), small(2), a_in, small(5), a_uq, small(6), a_ukv, small(7), small(8),
                  small(9), small(10), a_out, small(3), small(4), a_up, a_cw, small(11), a_down]
    outs = [loss, grad_x[None]]
    for k in range(4):
        outs += [t[k] for t in per_weight]
    return tuple(outs)
```

```python
import functools

import jax
import jax.numpy as jnp
from jax import lax
from jax.experimental import pallas as pl
from jax.experimental.pallas import tpu as pltpu

F32 = jnp.float32
MXU = jnp.bfloat16

N_DEV = 8
D_MODEL = 1024
HEADS = 8
HEAD_PAD = 128
NOPE = 64
ROPE = 32
Q_LORA = 256
KV_LORA = 128
GM_WIDTH = 512
GM_DIM = 64
GM_CHUNK = 128
CHUNK_SHIFT = 6
ROPE_THETA = 10000.0
ATTN_SCALE = (NOPE + ROPE) ** -0.5
Z_COLS = 1536
FF_BLK = 704
EPS = 1e-6
ADAM_LR = 0.001
ADAM_B1 = 0.9
ADAM_B2 = 0.999
ADAM_EPS = 1e-08
ADAM_WD = 0.01
ADAM_STEP = 10
VMEM_LIMIT = 56 * 1024 * 1024
MESH = pl.DeviceIdType.MESH


def _dot(a, b):
    return jnp.dot(a, b, preferred_element_type=F32)


def _dot_nt(a, b):
    return lax.dot_general(a, b, (((1,), (1,)), ((), ())), preferred_element_type=F32)


def _dot_tn(a, b):
    return lax.dot_general(a, b, (((0,), (0,)), ((), ())), preferred_element_type=F32)


def _call(body, *, name, grid, in_specs, out_specs, out_shape, scratch=(), sem=None):
    params = pltpu.CompilerParams(dimension_semantics=sem, vmem_limit_bytes=VMEM_LIMIT)
    return pl.pallas_call(body, name=name, grid=grid, in_specs=in_specs, out_specs=out_specs,
                          out_shape=out_shape, scratch_shapes=list(scratch), compiler_params=params)


def _full(shape):
    n = len(shape)
    return pl.BlockSpec(shape, lambda *_: (0,) * n)


def _rows(tm, cols, col_block=0):
    return pl.BlockSpec((tm, cols), lambda i: (i, col_block))


def _sds(shape, dtype):
    return jax.ShapeDtypeStruct(shape, dtype)


def _row(ref, k):
    return ref[pl.ds(k, 1), :]


def _rms(x):
    r = lax.rsqrt(jnp.mean(x * x, axis=-1, keepdims=True) + EPS)
    return x * r, r


def _rms_bwd(d_hat, hat, r):
    return r * (d_hat - hat * jnp.mean(d_hat * hat, axis=-1, keepdims=True))


def _rope_partner(t):
    lane = lax.broadcasted_iota(jnp.int32, t.shape, 1)
    swapped = jnp.where(lane < NOPE + ROPE // 2, -pltpu.roll(t, HEAD_PAD - ROPE // 2, 1), pltpu.roll(t, ROPE // 2, 1))
    return jnp.where((lane >= NOPE) & (lane < NOPE + ROPE), swapped, 0.0)


def _rope(t, cos, sin):
    return t * cos + _rope_partner(t) * sin


def _rope_transposed(g, cos, sin):
    return g * cos - _rope_partner(g * sin)


def _gelu(x):
    return x * (0.5 * (1.0 + jnp.tanh(0.7978845608028654 * (x + 0.044715 * (x * x * x)))))


def _gelu_grad(x):
    t = jnp.tanh(0.7978845608028654 * (x + 0.044715 * (x * x * x)))
    return 0.5 * (1.0 + t) + 0.5 * x * (1.0 - t * t) * (0.7978845608028654 * (1.0 + 3.0 * 0.044715 * (x * x)))


def _split_dot(x, mat):
    hi = x.astype(MXU)
    lo = (x - hi.astype(F32)).astype(MXU)
    return _dot(hi, mat) + _dot(lo, mat)


def _seg_matrix():
    r = lax.broadcasted_iota(jnp.int32, (GM_WIDTH, GM_WIDTH), 0) >> 6
    c = lax.broadcasted_iota(jnp.int32, (GM_WIDTH, GM_WIDTH), 1) >> 6
    return jnp.where(r == c, 1.0 / GM_DIM, 0.0).astype(MXU)


def _spatial_mask():
    i = lax.broadcasted_iota(jnp.int32, (GM_CHUNK, GM_CHUNK), 0) >> CHUNK_SHIFT
    j = lax.broadcasted_iota(jnp.int32, (GM_CHUNK, GM_CHUNK), 1) >> CHUNK_SHIFT
    return (j <= i).astype(F32)


def _head_lane_mask(h, rows):
    lane = lax.broadcasted_iota(jnp.int32, (rows, GM_WIDTH), 1) >> 6
    return lane == h


def _my_place():
    return lax.axis_index("x"), lax.axis_index("y"), lax.axis_index("c")


def _flat(p):
    return 4 * p[0] + 2 * p[1] + p[2]


def _comm_sems(n):
    return [pltpu.SemaphoreType.DMA((7 * n,)), pltpu.SemaphoreType.DMA((7 * n,)), pltpu.SemaphoreType.DMA((n,))]


def _gather_steps(ins, outs, sems):
    send_sems, recv_sems, local_sems = sems
    n = len(ins)
    x, y, c = _my_place()
    me, sibling = (x, y, c), (x, y, 1 - c)
    chips = [(1 - x, y), (x, 1 - y), (1 - x, 1 - y)]

    def copy(a, k, block, to, src=None):
        slot = outs[a].at[_flat(block)]
        return pltpu.make_async_remote_copy(
            src_ref=slot if src is None else src, dst_ref=slot,
            send_sem=send_sems.at[7 * a + k], recv_sem=recv_sems.at[7 * a + k],
            device_id=to, device_id_type=MESH)

    def mine():
        return [pltpu.make_async_copy(ins[a], outs[a].at[_flat(me)], local_sems.at[a]) for a in range(n)]

    def first():
        cps = []
        for a in range(n):
            cps.append(copy(a, 0, me, sibling, src=ins[a]))
            cps += [copy(a, 1 + j, me, (*chip, c), src=ins[a]) for j, chip in enumerate(chips)]
        return cps

    def passed():
        return [copy(a, 4 + j, (*chip, c), sibling) for a in range(n) for j, chip in enumerate(chips)]

    def start():
        for cp in mine() + first():
            cp.start()

    def forward():
        for a in range(n):
            for j, chip in enumerate(chips):
                copy(a, 1 + j, (*chip, c), me).wait_recv()
                copy(a, 4 + j, (*chip, c), sibling).start()

    def finish():
        for a in range(n):
            copy(a, 0, sibling, me).wait_recv()
            for j, chip in enumerate(chips):
                copy(a, 4 + j, (*chip, 1 - c), me).wait_recv()
        for cp in first() + passed():
            cp.wait_send()
        for cp in mine():
            cp.wait()

    return start, forward, finish


def _scatter_steps(ins, outs, sems, slots):
    send_sems, recv_sems, local_sems = sems
    n = len(ins)
    flips = [(fx, fy, fc) for fx in (0, 1) for fy in (0, 1) for fc in (0, 1)][1:]
    me = _my_place()

    def peer(f):
        return tuple(1 - v if b else v for v, b in zip(me, f))

    def copy(a, k, arriving=False):
        p = peer(flips[k])
        return pltpu.make_async_remote_copy(
            src_ref=ins[a].at[slots[a](_flat(p))], dst_ref=outs[a].at[_flat(p if arriving else me)],
            send_sem=send_sems.at[7 * a + k], recv_sem=recv_sems.at[7 * a + k],
            device_id=p, device_id_type=MESH)

    def mine():
        return [pltpu.make_async_copy(ins[a].at[slots[a](_flat(me))], outs[a].at[_flat(me)], local_sems.at[a])
                for a in range(n)]

    def start():
        for cp in mine() + [copy(a, k) for a in range(n) for k in range(7)]:
            cp.start()

    def finish():
        for a in range(n):
            for k in range(7):
                copy(a, k, arriving=True).wait_recv()
        for a in range(n):
            for k in range(7):
                copy(a, k).wait_send()
        for cp in mine():
            cp.wait()

    return start, finish


def _plain_slot(j):
    return (j,)


def _scatter_out_shape(arr, slot):
    return _sds((N_DEV,) + arr.shape[len(slot(0)):], arr.dtype)


def _exchange(gathered, scattered, name):
    ng, ns = len(gathered), len(scattered)
    slots = [slot for _, slot in scattered]

    def body(*refs):
        g_in, s_in = refs[:ng], refs[ng:ng + ns]
        g_out, s_out = refs[ng + ns:2 * ng + ns], refs[2 * ng + ns:2 * (ng + ns)]
        sems = refs[2 * (ng + ns):]
        g_start, g_forward, g_finish = _gather_steps(g_in, g_out, sems[:3])
        s_start, s_finish = _scatter_steps(s_in, s_out, sems[3:], slots)
        g_start()
        s_start()
        g_forward()
        g_finish()
        s_finish()

    any_spec = pl.BlockSpec(memory_space=pl.ANY)
    outs = pl.pallas_call(
        body, name=name,
        in_specs=[any_spec] * (ng + ns), out_specs=[any_spec] * (ng + ns),
        out_shape=[_sds((N_DEV,) + a.shape, a.dtype) for a in gathered]
        + [_scatter_out_shape(a, slot) for a, slot in scattered],
        scratch_shapes=_comm_sems(max(ng, 1)) + _comm_sems(max(ns, 1)),
    )(*gathered, *[a for a, _ in scattered])
    return outs[:ng], outs[ng:]


def _ada_fwd(c_all, w_ada):
    def body(c_ref, w_ref, part_ref, act_ref):
        cv = c_ref[...]
        act = cv * jax.nn.sigmoid(cv)
        act_ref[...] = act
        part_ref[...] = _dot(act.astype(MXU), w_ref[...].astype(MXU))

    cols = w_ada.shape[1]
    return _call(body, name="ada_fwd", grid=(1,),
                 in_specs=[_full(c_all.shape), _full(w_ada.shape)],
                 out_specs=[_full((N_DEV, cols)), _full(c_all.shape)],
                 out_shape=[_sds((N_DEV, cols), F32), _sds(c_all.shape, F32)])(c_all, w_ada)


def _mix_in_fwd(x, ada_raw, ada_b, g_pre, w1, g_q, g_kv, wq, wkv, cos_t, sin_t, tm):
    s = x.shape[0]

    def body(x_ref, ar_ref, ab_ref, g_ref, w1_ref, gq_ref, gkv_ref, wq_ref, wkv_ref, cos_ref, sin_ref,
             h1_ref, z_ref, qp_ref, kp_ref, vp_ref, cqn_ref, ckvn_ref):
        sh = _row(ar_ref, 0) + _row(ab_ref, 0)
        sc = _row(ar_ref, 1) + _row(ab_ref, 1)
        xn, _ = _rms(x_ref[...])
        hb = ((xn * g_ref[...]) * (1.0 + sc) + sh).astype(MXU)
        h1_ref[...] = hb
        z = _dot(hb, w1_ref[...])
        z_ref[...] = z
        cos, sin = cos_ref[...], sin_ref[...]
        cqn = (_rms(z[:, :Q_LORA])[0] * gq_ref[...]).astype(MXU)
        ckvn = (_rms(z[:, Q_LORA:Q_LORA + KV_LORA])[0] * gkv_ref[...]).astype(MXU)
        cqn_ref[...] = cqn
        ckvn_ref[...] = ckvn
        q = _dot(cqn, wq_ref[...])
        kv = _dot(ckvn, wkv_ref[...])
        k_rope = _rope(z[:, Q_LORA + KV_LORA:Q_LORA + KV_LORA + HEAD_PAD], cos, sin)
        for h in range(HEADS):
            blk = slice(h * HEAD_PAD, (h + 1) * HEAD_PAD)
            qp_ref[:, blk] = _rope(q[:, blk], cos, sin).astype(MXU)
            kp_ref[:, blk] = (kv[:, blk] + k_rope).astype(MXU)
        v_lane = lax.broadcasted_iota(jnp.int32, (tm, HEADS * HEAD_PAD), 1) & (HEAD_PAD - 1)
        vp_ref[...] = jnp.where(v_lane == NOPE, 1.0, kv[:, HEADS * HEAD_PAD:]).astype(MXU)

    hp = HEADS * HEAD_PAD
    return _call(
        body, name="mix_in_fwd", grid=(s // tm,), sem=("parallel",),
        in_specs=[_rows(tm, D_MODEL), _full(ada_raw.shape), _full(ada_b.shape), _full(g_pre.shape), _full(w1.shape),
                  _full(g_q.shape), _full(g_kv.shape), _full(wq.shape), _full(wkv.shape),
                  _rows(tm, HEAD_PAD), _rows(tm, HEAD_PAD)],
        out_specs=[_rows(tm, D_MODEL), _rows(tm, Z_COLS), _rows(tm, hp), _rows(tm, hp), _rows(tm, hp),
                   _rows(tm, Q_LORA), _rows(tm, KV_LORA)],
        out_shape=[_sds((s, D_MODEL), MXU), _sds((s, Z_COLS), F32), _sds((s, hp), MXU), _sds((s, hp), MXU),
                   _sds((s, hp), MXU), _sds((s, Q_LORA), MXU), _sds((s, KV_LORA), MXU)],
    )(x, ada_raw, ada_b, g_pre, w1, g_q, g_kv, wq, wkv, cos_t, sin_t)


def _gm_norm(zv, seg):
    gv = _gelu(zv)
    cen = gv - _split_dot(gv, seg)
    rstd = lax.rsqrt(_split_dot(cen * cen, seg) + EPS)
    return gv, cen * rstd, rstd


def _gm_mix(wm, vb, rows):
    out = jnp.zeros((rows, GM_WIDTH), F32)
    for h in range(HEADS):
        out = out + jnp.where(_head_lane_mask(h, rows), _dot(wm[h], vb), 0.0)
    return out


def _gmlp_fwd(z, ln_g, ln_b, w_sp, bias_exp, tm):
    s = z.shape[0]
    nblk = tm // GM_CHUNK

    def body(zu_ref, zv_ref, lg_ref, lb_ref, w_ref, be_ref, sgu_ref):
        seg = _seg_matrix()
        mask = _spatial_mask()
        wm = [(w_ref[h] * mask).astype(MXU) for h in range(HEADS)]
        gu = _gelu(zu_ref[...])
        _, vhat, _ = _gm_norm(zv_ref[...], seg)
        vln = (vhat * lg_ref[...] + lb_ref[...]).astype(MXU)
        for n in range(nblk):
            rows = slice(n * GM_CHUNK, (n + 1) * GM_CHUNK)
            mixed = _gm_mix(wm, vln[rows], GM_CHUNK) + be_ref[...]
            sgu_ref[rows, :] = (gu[rows] * mixed).astype(MXU)

    return _call(
        body, name="gmlp_fwd", grid=(s // tm,), sem=("parallel",),
        in_specs=[_rows(tm, GM_WIDTH, 1), _rows(tm, GM_WIDTH, 2), _full(ln_g.shape), _full(ln_b.shape),
                  _full(w_sp.shape), _full(bias_exp.shape)],
        out_specs=_rows(tm, GM_WIDTH), out_shape=_sds((s, GM_WIDTH), MXU),
    )(z, z, ln_g, ln_b, w_sp, bias_exp)


def _diag_mask(t):
    qc = lax.broadcasted_iota(jnp.int32, (t, t), 0) >> CHUNK_SHIFT
    kc = lax.broadcasted_iota(jnp.int32, (t, t), 1) >> CHUNK_SHIFT
    return kc <= qc


NEG_BIG = -1e30
ATTN_HEADS_PER_STEP = 2


def _attn_fwd(qp, kp, vp, tq, gathered):
    s = qp.shape[0]
    nq = s // tq
    hb = ATTN_HEADS_PER_STEP
    groups = HEADS // hb
    width = hb * HEAD_PAD
    ng = len(gathered)

    def body(q_ref, k_ref, v_ref, *rest):
        g_in, (o_ref, lse_ref), g_out = rest[:ng], rest[ng:ng + 2], rest[ng + 2:2 * ng + 2]
        m_sc, acc_sc = rest[2 * ng + 2:2 * ng + 4]
        g_start, g_forward, g_finish = _gather_steps(g_in, g_out, rest[2 * ng + 4:])
        g, i = pl.program_id(0), pl.program_id(1)
        pl.when((g == 0) & (i == 0))(g_start)
        pl.when((g == groups - 1) & (i == 0))(g_forward)
        m_sc[...] = jnp.full(m_sc.shape, NEG_BIG, F32)
        acc_sc[...] = jnp.zeros(acc_sc.shape, F32)

        def tile(j, masked):
            rows = pl.ds(pl.multiple_of(j * tq, tq), tq)
            for hh in range(hb):
                lanes = slice(hh * HEAD_PAD, (hh + 1) * HEAD_PAD)
                sc = _dot_nt(q_ref[:, lanes], k_ref[rows, lanes]) * ATTN_SCALE
                if masked:
                    sc = jnp.where(_diag_mask(tq), sc, NEG_BIG)
                blocks = [sc[:, b * 128:(b + 1) * 128] for b in range(tq // 128)]
                m_prev = m_sc[hh]
                m_tile = jnp.max(functools.reduce(jnp.maximum, blocks), axis=-1, keepdims=True)
                m_new = jnp.maximum(m_prev, m_tile)
                alpha = jnp.exp(m_prev - m_new)
                p = jnp.concatenate([jnp.exp(b - m_new) for b in blocks], axis=1).astype(MXU)
                acc_sc[hh] = alpha * acc_sc[hh] + _dot(p, v_ref[rows, lanes])
                m_sc[hh] = m_new

        def off_diagonal(j, carry):
            tile(j, False)
            return carry

        lax.fori_loop(0, i, off_diagonal, 0)
        tile(i, True)
        for hh in range(hb):
            lanes = slice(hh * HEAD_PAD, (hh + 1) * HEAD_PAD)
            acc = acc_sc[hh]
            denom = acc[:, NOPE:NOPE + 1]
            o_ref[:, lanes] = (acc / denom).astype(MXU)
            lse_ref[hh] = m_sc[hh][:, :1] + jnp.log(denom)
        pl.when((g == groups - 1) & (i == nq - 1))(g_finish)

    q_spec = pl.BlockSpec((tq, width), lambda g, i: (i, g))
    kv_spec = pl.BlockSpec((s, width), lambda g, i: (0, g))
    any_spec = pl.BlockSpec(memory_space=pl.ANY)
    outs = _call(
        body, name="attn_fwd", grid=(groups, nq), sem=("arbitrary", "arbitrary"),
        in_specs=[q_spec, kv_spec, kv_spec] + [any_spec] * ng,
        out_specs=[q_spec, pl.BlockSpec((hb, tq, 1), lambda g, i: (g, i, 0))] + [any_spec] * ng,
        out_shape=[_sds(qp.shape, MXU), _sds((HEADS, s, 1), F32)]
        + [_sds((N_DEV,) + a.shape, a.dtype) for a in gathered],
        scratch=[pltpu.VMEM((hb, tq, HEAD_PAD), F32), pltpu.VMEM((hb, tq, HEAD_PAD), F32)] + _comm_sems(ng),
    )(qp, kp, vp, *gathered)
    return outs[0], outs[1], outs[2:]


def _out_proj_fwd(o_pad, sgu, wo, x, ada_raw, ada_b, g_post_mix, g_pre_ffn, tm):
    s = x.shape[0]
    hp = HEADS * HEAD_PAD

    def body(o_ref, sgu_ref, wo_ref, x_ref, ar_ref, ab_ref, gpm_ref, gpf_ref, m_ref, x2_ref, h2_ref):
        gt1 = _row(ar_ref, 2) + _row(ab_ref, 2)
        sh2 = _row(ar_ref, 3) + _row(ab_ref, 3)
        sc2 = _row(ar_ref, 4) + _row(ab_ref, 4)
        m = _dot(o_ref[...], wo_ref[pl.ds(0, hp), :]) + _dot(sgu_ref[...], wo_ref[pl.ds(hp, GM_WIDTH), :])
        m_ref[...] = m
        x2 = x_ref[...] + gt1 * (_rms(m)[0] * gpm_ref[...])
        x2_ref[...] = x2
        h2_ref[...] = ((_rms(x2)[0] * gpf_ref[...]) * (1.0 + sc2) + sh2).astype(MXU)

    return _call(
        body, name="out_proj_fwd", grid=(s // tm,), sem=("parallel",),
        in_specs=[_rows(tm, hp), _rows(tm, GM_WIDTH), _full(wo.shape), _rows(tm, D_MODEL), _full(ada_raw.shape),
                  _full(ada_b.shape), _full(g_post_mix.shape), _full(g_pre_ffn.shape)],
        out_specs=[_rows(tm, D_MODEL)] * 3,
        out_shape=[_sds((s, D_MODEL), F32), _sds((s, D_MODEL), F32), _sds((s, D_MODEL), MXU)],
    )(o_pad, sgu, wo, x, ada_raw, ada_b, g_post_mix, g_pre_ffn)


def _conv_taps(u, halo):
    ext = jnp.concatenate([halo, u], axis=0)
    return pltpu.roll(ext, 1, 0)[8:], pltpu.roll(ext, 2, 0)[8:]


def _conv(u, halo, cw_ref, cb_ref):
    m1, m2 = _conv_taps(u, halo)
    y = cb_ref[0] + ((m2 * cw_ref[0, pl.ds(0, 1), :] + m1 * cw_ref[0, pl.ds(1, 1), :]) + u * cw_ref[0, pl.ds(2, 1), :])
    return y, m1, m2


def _ffn_up_fwd(h2, w_up, conv_w, conv_b, tm):
    s = h2.shape[0]
    half = N_DEV // 2

    def body(h_ref, wa_ref, wb_ref, cwa_ref, cwb_ref, cba_ref, cbb_ref, ua_ref, ub_ref, act_ref, halo_a, halo_b):
        i = pl.program_id(1)

        @pl.when(i == 0)
        def _():
            halo_a[...] = jnp.zeros(halo_a.shape, F32)
            halo_b[...] = jnp.zeros(halo_b.shape, F32)

        hb = h_ref[...]
        ua = _dot(hb, wa_ref[0])
        ub = _dot(hb, wb_ref[0])
        ua_ref[0] = ua
        ub_ref[0] = ub
        ya, _, _ = _conv(ua, halo_a[...], cwa_ref, cba_ref)
        yb, _, _ = _conv(ub, halo_b[...], cwb_ref, cbb_ref)
        halo_a[...] = ua[tm - 8:]
        halo_b[...] = ub[tm - 8:]
        act_ref[0] = ((ya * jax.nn.sigmoid(ya)) * yb).astype(MXU)

    def blk(shape, off):
        return pl.BlockSpec(shape, lambda j, i: (j + off, 0, 0))

    def tok(off=0):
        return pl.BlockSpec((1, tm, FF_BLK), lambda j, i: (j + off, i, 0))

    return _call(
        body, name="ffn_up_fwd", grid=(half, s // tm), sem=("parallel", "arbitrary"),
        in_specs=[pl.BlockSpec((tm, D_MODEL), lambda j, i: (i, 0)),
                  blk((1, D_MODEL, FF_BLK), 0), blk((1, D_MODEL, FF_BLK), half),
                  blk((1, 3, FF_BLK), 0), blk((1, 3, FF_BLK), half), blk((1, 1, FF_BLK), 0), blk((1, 1, FF_BLK), half)],
        out_specs=[tok(), tok(), tok()],
        out_shape=[_sds((half, s, FF_BLK), F32), _sds((half, s, FF_BLK), F32), _sds((half, s, FF_BLK), MXU)],
        scratch=[pltpu.VMEM((8, FF_BLK), F32), pltpu.VMEM((8, FF_BLK), F32)],
    )(h2, w_up, w_up, conv_w, conv_w, conv_b, conv_b)


def _ffn_down_fwd(act, wd, x2, target, ada_raw, ada_b, g_post_ffn, tm):
    s = x2.shape[0]
    half = N_DEV // 2

    def body(act_ref, wd_ref, x2_ref, t_ref, ar_ref, ab_ref, g_ref, dout_ref, df_ref, loss_ref, dgt_ref, dg_ref):
        i = pl.program_id(0)

        @pl.when(i == 0)
        def _():
            loss_ref[...] = jnp.zeros(loss_ref.shape, F32)
            dgt_ref[...] = jnp.zeros(dgt_ref.shape, F32)
            dg_ref[...] = jnp.zeros(dg_ref.shape, F32)

        gt2 = _row(ar_ref, 5) + _row(ab_ref, 5)
        g = g_ref[...]
        f = _dot(act_ref[0], wd_ref[0])
        for j in range(1, half):
            f = f + _dot(act_ref[j], wd_ref[j])
        fhat, rf = _rms(f)
        fn = fhat * g
        err = (x2_ref[...] + gt2 * fn) - t_ref[...]
        loss_ref[...] += 0.5 * jnp.sum(jnp.mean(err * err, axis=-1, keepdims=True))
        d_out = err * (1.0 / D_MODEL)
        dout_ref[...] = d_out
        dgt_ref[...] += jnp.sum(d_out * fn, axis=0, keepdims=True)
        d_fn = d_out * gt2
        dg_ref[...] += jnp.sum(d_fn * fhat, axis=0, keepdims=True)
        df_ref[...] = _rms_bwd(d_fn * g, fhat, rf).astype(MXU)

    vec = pl.BlockSpec((1, D_MODEL), lambda i: (0, 0))
    return _call(
        body, name="ffn_down_fwd", grid=(s // tm,), sem=("arbitrary",),
        in_specs=[pl.BlockSpec((half, tm, FF_BLK), lambda i: (0, i, 0)), _full(wd.shape), _rows(tm, D_MODEL),
                  _rows(tm, D_MODEL), _full(ada_raw.shape), _full(ada_b.shape), _full(g_post_ffn.shape)],
        out_specs=[_rows(tm, D_MODEL), _rows(tm, D_MODEL), pl.BlockSpec((1, 128), lambda i: (0, 0)), vec, vec],
        out_shape=[_sds((s, D_MODEL), F32), _sds((s, D_MODEL), MXU), _sds((1, 128), F32),
                   _sds((1, D_MODEL), F32), _sds((1, D_MODEL), F32)],
    )(act, wd, x2, target, ada_raw, ada_b, g_post_ffn)


def _ffn_down_bwd(d_f, wd, up_a, up_b, conv_w, conv_b, tm):
    s = d_f.shape[0]
    half = N_DEV // 2
    nt = s // tm
    hb = tm // 8

    def body(df_ref, wd_ref, ua_ref, ub_ref, pa_ref, pb_ref, cwa_ref, cwb_ref, cba_ref, cbb_ref,
             dup_ref, dcwa_ref, dcwb_ref, dcba_ref, dcbb_ref, next_a, next_b):
        i = pl.program_id(1)
        first_tile = i == nt - 1

        @pl.when(i == 0)
        def _():
            next_a[...] = jnp.zeros(next_a.shape, F32)
            next_b[...] = jnp.zeros(next_b.shape, F32)
            for r in (dcwa_ref, dcwb_ref, dcba_ref, dcbb_ref):
                r[...] = jnp.zeros(r.shape, F32)

        d_act = _dot_nt(df_ref[...], wd_ref[0])
        keep = jnp.where(first_tile, 0.0, 1.0)
        ua, ub = ua_ref[0], ub_ref[0]
        ya, ma1, ma2 = _conv(ua, pa_ref[0] * keep, cwa_ref, cba_ref)
        yb, mb1, mb2 = _conv(ub, pb_ref[0] * keep, cwb_ref, cbb_ref)
        sig = jax.nn.sigmoid(ya)
        d_ya = d_act * yb * (sig * (1.0 + ya * (1.0 - sig)))
        d_yb = d_act * (ya * sig)

        def conv_bwd(d_y, u, m1, m2, nxt, cw_ref, part, dcw_ref, dcb_ref):
            ext = jnp.concatenate([d_y, nxt[...]], axis=0)
            p1 = pltpu.roll(ext, tm + 7, 0)[:tm]
            p2 = pltpu.roll(ext, tm + 6, 0)[:tm]
            d_u = (d_y * cw_ref[0, pl.ds(2, 1), :] + p1 * cw_ref[0, pl.ds(1, 1), :]) + p2 * cw_ref[0, pl.ds(0, 1), :]
            dup_ref[0, part] = d_u.astype(MXU)
            dcb_ref[0] += jnp.sum(d_y, axis=0, keepdims=True)
            dcw_ref[0, pl.ds(0, 1), :] += jnp.sum(d_y * m2, axis=0, keepdims=True)
            dcw_ref[0, pl.ds(1, 1), :] += jnp.sum(d_y * m1, axis=0, keepdims=True)
            dcw_ref[0, pl.ds(2, 1), :] += jnp.sum(d_y * u, axis=0, keepdims=True)
            nxt[...] = d_y[:8]

        conv_bwd(d_ya, ua, ma1, ma2, next_a, cwa_ref, 0, dcwa_ref, dcba_ref)
        conv_bwd(d_yb, ub, mb1, mb2, next_b, cwb_ref, 1, dcwb_ref, dcbb_ref)

    def rev(i):
        return nt - 1 - i

    def blk(shape, off):
        return pl.BlockSpec(shape, lambda j, i: (j + off, 0, 0))

    tok = pl.BlockSpec((1, tm, FF_BLK), lambda j, i: (j, rev(i), 0))
    prev = pl.BlockSpec((1, 8, FF_BLK), lambda j, i: (j, jnp.maximum(rev(i) * hb - 1, 0), 0))
    acc3 = pl.BlockSpec((1, 3, FF_BLK), lambda j, i: (j, 0, 0))
    acc1 = pl.BlockSpec((1, 1, FF_BLK), lambda j, i: (j, 0, 0))
    return _call(
        body, name="ffn_down_bwd", grid=(half, nt), sem=("parallel", "arbitrary"),
        in_specs=[pl.BlockSpec((tm, D_MODEL), lambda j, i: (rev(i), 0)), blk((1, FF_BLK, D_MODEL), 0),
                  tok, tok, prev, prev,
                  blk((1, 3, FF_BLK), 0), blk((1, 3, FF_BLK), half), blk((1, 1, FF_BLK), 0), blk((1, 1, FF_BLK), half)],
        out_specs=[pl.BlockSpec((1, 2, tm, FF_BLK), lambda j, i: (j, 0, rev(i), 0)), acc3, acc3, acc1, acc1],
        out_shape=[_sds((half, 2, s, FF_BLK), MXU),
                   _sds((half, 3, FF_BLK), F32), _sds((half, 3, FF_BLK), F32),
                   _sds((half, 1, FF_BLK), F32), _sds((half, 1, FF_BLK), F32)],
        scratch=[pltpu.VMEM((8, FF_BLK), F32), pltpu.VMEM((8, FF_BLK), F32)],
    )(d_f, wd, up_a, up_b, up_a, up_b, conv_w, conv_w, conv_b, conv_b)


def _ffn_up_bwd(d_up, w_up, x2, m, d_out, ada_raw, ada_b, g_pre_ffn, g_post_mix, tm):
    s = x2.shape[0]
    half = N_DEV // 2

    def body(dup_ref, w_ref, x2_ref, m_ref, dout_ref, ar_ref, ab_ref, gpf_ref, gpm_ref,
             dx_ref, dm_ref, dsh_ref, dsc_ref, dgpf_ref, dgt1_ref, dgpm_ref):
        i = pl.program_id(0)

        @pl.when(i == 0)
        def _():
            for r in (dsh_ref, dsc_ref, dgpf_ref, dgt1_ref, dgpm_ref):
                r[...] = jnp.zeros(r.shape, F32)

        gt1 = _row(ar_ref, 2) + _row(ab_ref, 2)
        sc2 = _row(ar_ref, 4) + _row(ab_ref, 4)
        gpf, gpm = gpf_ref[...], gpm_ref[...]
        d_h2 = _dot_nt(dup_ref[0, 0], w_ref[0])
        for j in range(1, half):
            d_h2 = d_h2 + _dot_nt(dup_ref[j, 0], w_ref[j])
        for j in range(half):
            d_h2 = d_h2 + _dot_nt(dup_ref[j, 1], w_ref[half + j])
        x2n, r2 = _rms(x2_ref[...])
        dsh_ref[...] += jnp.sum(d_h2, axis=0, keepdims=True)
        dsc_ref[...] += jnp.sum(d_h2 * (x2n * gpf), axis=0, keepdims=True)
        d_mod = d_h2 * (1.0 + sc2)
        dgpf_ref[...] += jnp.sum(d_mod * x2n, axis=0, keepdims=True)
        d_x2 = dout_ref[...] + _rms_bwd(d_mod * gpf, x2n, r2)
        dx_ref[...] = d_x2
        mhat, rm = _rms(m_ref[...])
        dgt1_ref[...] += jnp.sum(d_x2 * (mhat * gpm), axis=0, keepdims=True)
        d_mn = d_x2 * gt1
        dgpm_ref[...] += jnp.sum(d_mn * mhat, axis=0, keepdims=True)
        dm_ref[...] = _rms_bwd(d_mn * gpm, mhat, rm).astype(MXU)

    vec = pl.BlockSpec((1, D_MODEL), lambda i: (0, 0))
    tok = pl.BlockSpec((half, 2, tm, FF_BLK), lambda i: (0, 0, i, 0))
    return _call(
        body, name="ffn_up_bwd", grid=(s // tm,), sem=("arbitrary",),
        in_specs=[tok, _full(w_up.shape), _rows(tm, D_MODEL), _rows(tm, D_MODEL), _rows(tm, D_MODEL),
                  _full(ada_raw.shape), _full(ada_b.shape), _full(g_pre_ffn.shape), _full(g_post_mix.shape)],
        out_specs=[_rows(tm, D_MODEL), _rows(tm, D_MODEL), vec, vec, vec, vec, vec],
        out_shape=[_sds((s, D_MODEL), F32), _sds((s, D_MODEL), MXU)] + [_sds((1, D_MODEL), F32)] * 5,
    )(d_up, w_up, x2, m, d_out, ada_raw, ada_b, g_pre_ffn, g_post_mix)


def _out_proj_bwd(d_m, wo, o_pad, tm):
    s = d_m.shape[0]
    hp = HEADS * HEAD_PAD

    def body(dm_ref, wo_ref, o_ref, do_ref, dsgu_ref, delta_ref):
        d_cat = _dot_nt(dm_ref[...], wo_ref[...])
        d_o = d_cat[:, :hp]
        do_ref[...] = d_o.astype(MXU)
        dsgu_ref[...] = d_cat[:, hp:]
        prod = d_o * o_ref[...].astype(F32)
        for h in range(HEADS):
            delta_ref[h] = jnp.sum(prod[:, h * HEAD_PAD:(h + 1) * HEAD_PAD], axis=-1, keepdims=True)

    return _call(
        body, name="out_proj_bwd", grid=(s // tm,), sem=("parallel",),
        in_specs=[_rows(tm, D_MODEL), _full(wo.shape), _rows(tm, hp)],
        out_specs=[_rows(tm, hp), _rows(tm, GM_WIDTH), pl.BlockSpec((HEADS, tm, 1), lambda i: (0, i, 0))],
        out_shape=[_sds((s, hp), MXU), _sds((s, GM_WIDTH), F32), _sds((HEADS, s, 1), F32)],
    )(d_m, wo, o_pad)


def _attn_bwd(qp, kp, vp, d_o, lse, delta, tq, scattered):
    s = qp.shape[0]
    nq = s // tq
    hb = ATTN_HEADS_PER_STEP
    groups = HEADS // hb
    width = hb * HEAD_PAD
    ns = len(scattered)
    slots = [slot for _, slot in scattered]

    def body(q_ref, k_ref, v_ref, do_ref, lse_ref, dl_ref, *rest):
        s_in, (dq_ref, dk_ref, dv_ref), s_out = rest[:ns], rest[ns:ns + 3], rest[ns + 3:2 * ns + 3]
        dk_sc, dv_sc = rest[2 * ns + 3:2 * ns + 5]
        s_start, s_finish = _scatter_steps(s_in, s_out, rest[2 * ns + 5:], slots)
        g, j = pl.program_id(0), pl.program_id(1)
        pl.when((g == 0) & (j == 0))(s_start)

        @pl.when(j == 0)
        def _():
            dq_ref[...] = jnp.zeros(dq_ref.shape, F32)

        dk_sc[...] = jnp.zeros(dk_sc.shape, F32)
        dv_sc[...] = jnp.zeros(dv_sc.shape, F32)

        def tile(i, masked):
            rows = pl.ds(pl.multiple_of(i * tq, tq), tq)
            for hh in range(hb):
                lanes = slice(hh * HEAD_PAD, (hh + 1) * HEAD_PAD)
                q, do, k = q_ref[rows, lanes], do_ref[rows, lanes], k_ref[:, lanes]
                sc = _dot_nt(q, k) * ATTN_SCALE
                if masked:
                    sc = jnp.where(_diag_mask(tq), sc, NEG_BIG)
                p = jnp.exp(sc - lse_ref[hh, rows, :])
                dv_sc[hh] += _dot_tn(p.astype(MXU), do)
                dp = _dot_nt(do, v_ref[:, lanes])
                ds = ((p * (dp - dl_ref[hh, rows, :])) * ATTN_SCALE).astype(MXU)
                dk_sc[hh] += _dot_tn(ds, q)
                dq_ref[rows, lanes] += _dot(ds, k)

        def off_diagonal(i, carry):
            tile(i, False)
            return carry

        tile(j, True)
        lax.fori_loop(j + 1, nq, off_diagonal, 0)
        for hh in range(hb):
            lanes = slice(hh * HEAD_PAD, (hh + 1) * HEAD_PAD)
            dk_ref[:, lanes] = dk_sc[hh]
            dv_ref[:, lanes] = dv_sc[hh]
        pl.when((g == groups - 1) & (j == nq - 1))(s_finish)

    seq_spec = pl.BlockSpec((s, width), lambda g, j: (0, g))
    kv_spec = pl.BlockSpec((tq, width), lambda g, j: (j, g))
    col_spec = pl.BlockSpec((hb, s, 1), lambda g, j: (g, 0, 0))
    any_spec = pl.BlockSpec(memory_space=pl.ANY)
    outs = _call(
        body, name="attn_bwd", grid=(groups, nq), sem=("arbitrary", "arbitrary"),
        in_specs=[seq_spec, kv_spec, kv_spec, seq_spec, col_spec, col_spec] + [any_spec] * ns,
        out_specs=[seq_spec, kv_spec, kv_spec] + [any_spec] * ns,
        out_shape=[_sds(qp.shape, F32), _sds(qp.shape, F32), _sds(qp.shape, F32)]
        + [_scatter_out_shape(a, slot) for a, slot in scattered],
        scratch=[pltpu.VMEM((hb, tq, HEAD_PAD), F32), pltpu.VMEM((hb, tq, HEAD_PAD), F32)] + _comm_sems(ns),
    )(qp, kp, vp, d_o, lse, delta, *[a for a, _ in scattered])
    return outs[0], outs[1], outs[2], outs[3:]


def _gmlp_bwd(z, d_sgu, ln_g, ln_b, w_sp, bias_exp, tm):
    s = z.shape[0]
    nblk = tm // GM_CHUNK

    def body(zu_ref, zv_ref, dsgu_ref, lg_ref, lb_ref, w_ref, be_ref,
             dguv_ref, dws_ref, dbs_ref, dlg_ref, dlb_ref, dbe_sc, dvln_sc):
        i = pl.program_id(0)

        @pl.when(i == 0)
        def _():
            for r in (dws_ref, dlg_ref, dlb_ref, dbe_sc):
                r[...] = jnp.zeros(r.shape, F32)

        seg = _seg_matrix()
        mask = _spatial_mask()
        wm = [(w_ref[h] * mask).astype(MXU) for h in range(HEADS)]
        zu, zv = zu_ref[...], zv_ref[...]
        gu = _gelu(zu)
        _, vhat, rstd = _gm_norm(zv, seg)
        lg = lg_ref[...]
        vln = (vhat * lg + lb_ref[...]).astype(MXU)
        d_sgu = dsgu_ref[...]
        for n in range(nblk):
            rows = slice(n * GM_CHUNK, (n + 1) * GM_CHUNK)
            vb = vln[rows]
            mixed = _gm_mix(wm, vb, GM_CHUNK) + be_ref[...]
            d_mixed = d_sgu[rows] * gu[rows]
            dguv_ref[rows, pl.ds(0, GM_WIDTH)] = ((d_sgu[rows] * mixed) * _gelu_grad(zu[rows])).astype(MXU)
            dbe_sc[...] += d_mixed
            dmb = d_mixed.astype(MXU)
            d_vln = jnp.zeros((GM_CHUNK, GM_WIDTH), F32)
            for h in range(HEADS):
                hm = _head_lane_mask(h, GM_CHUNK)
                dws_ref[h] += _dot_nt(jnp.where(hm, dmb, jnp.zeros_like(dmb)), vb)
                d_vln = d_vln + jnp.where(hm, _dot_tn(wm[h], dmb), 0.0)
            dvln_sc[rows, :] = d_vln
        d_vln = dvln_sc[...]
        dlg_ref[...] += jnp.sum(d_vln * vhat, axis=0, keepdims=True)
        dlb_ref[...] += jnp.sum(d_vln, axis=0, keepdims=True)
        d_vhat = d_vln * lg
        d_gv = rstd * ((d_vhat - _split_dot(d_vhat, seg)) - vhat * _split_dot(d_vhat * vhat, seg))
        dguv_ref[:, pl.ds(GM_WIDTH, GM_WIDTH)] = (d_gv * _gelu_grad(zv)).astype(MXU)

        @pl.when(i == pl.num_programs(0) - 1)
        def _():
            for h in range(HEADS):
                dws_ref[h] = dws_ref[h] * mask
            hrow = lax.broadcasted_iota(jnp.int32, (HEADS, GM_WIDTH), 0)
            hlane = lax.broadcasted_iota(jnp.int32, (HEADS, GM_WIDTH), 1) >> 6
            ind = jnp.where(hrow == hlane, 1.0, 0.0).astype(MXU)
            acc = dbe_sc[...]
            hi = acc.astype(MXU)
            lo = (acc - hi.astype(F32)).astype(MXU)
            dbs_ref[...] = _dot_nt(ind, hi) + _dot_nt(ind, lo)

    return _call(
        body, name="gmlp_bwd", grid=(s // tm,), sem=("arbitrary",),
        in_specs=[_rows(tm, GM_WIDTH, 1), _rows(tm, GM_WIDTH, 2), _rows(tm, GM_WIDTH), _full(ln_g.shape),
                  _full(ln_b.shape), _full(w_sp.shape), _full(bias_exp.shape)],
        out_specs=[_rows(tm, 2 * GM_WIDTH), _full(w_sp.shape), _full((HEADS, GM_CHUNK)), _full(ln_g.shape),
                   _full(ln_b.shape)],
        out_shape=[_sds((s, 2 * GM_WIDTH), MXU), _sds(w_sp.shape, F32), _sds((HEADS, GM_CHUNK), F32),
                   _sds(ln_g.shape, F32), _sds(ln_b.shape, F32)],
        scratch=[pltpu.VMEM((GM_CHUNK, GM_WIDTH), F32), pltpu.VMEM((tm, GM_WIDTH), F32)],
    )(z, z, d_sgu, ln_g, ln_b, w_sp, bias_exp)


def _mix_in_bwd(dq, dk, dv, z, d_guv, x, d_x_part, ada_raw, ada_b, g_pre, g_q, g_kv, w1a, w1b, wq, wkv,
                cos_t, sin_t, tm):
    s = x.shape[0]
    hp = HEADS * HEAD_PAD
    za = Q_LORA + KV_LORA + HEAD_PAD

    def body(dq_ref, dk_ref, dv_ref, z_ref, dguv_ref, x_ref, dxp_ref, ar_ref, ab_ref, g_ref, gq_ref, gkv_ref,
             w1a_ref, w1b_ref, wq_ref, wkv_ref, cos_ref, sin_ref,
             gx_ref, dza_ref, dqp_ref, dkvp_ref, dsh_ref, dsc_ref, dg_ref, dgq_ref, dgkv_ref):
        i = pl.program_id(0)

        @pl.when(i == 0)
        def _():
            for r in (dsh_ref, dsc_ref, dg_ref, dgq_ref, dgkv_ref):
                r[...] = jnp.zeros(r.shape, F32)

        cos, sin = cos_ref[...], sin_ref[...]
        d_krot = jnp.zeros((tm, HEAD_PAD), F32)
        for h in range(HEADS):
            blk = slice(h * HEAD_PAD, (h + 1) * HEAD_PAD)
            dqp_ref[:, blk] = _rope_transposed(dq_ref[:, blk], cos, sin).astype(MXU)
            dk_h = dk_ref[:, blk]
            d_krot = d_krot + dk_h
            dkvp_ref[:, blk] = dk_h.astype(MXU)
        dkvp_ref[:, pl.ds(hp, hp)] = dv_ref[...].astype(MXU)
        lane = lax.broadcasted_iota(jnp.int32, (tm, HEAD_PAD), 1)
        d_kr = jnp.where((lane >= NOPE) & (lane < NOPE + ROPE), _rope_transposed(d_krot, cos, sin), 0.0)
        d_cqn = _dot_nt(dqp_ref[...], wq_ref[...])
        d_ckvn = _dot_nt(dkvp_ref[...], wkv_ref[...])
        zt = z_ref[...]
        gq, gkv = gq_ref[...], gkv_ref[...]
        cq_hat, rq = _rms(zt[:, :Q_LORA])
        ckv_hat, rkv = _rms(zt[:, Q_LORA:Q_LORA + KV_LORA])
        dgq_ref[...] += jnp.sum(d_cqn * cq_hat, axis=0, keepdims=True)
        dgkv_ref[...] += jnp.sum(d_ckvn * ckv_hat, axis=0, keepdims=True)
        d_cq = _rms_bwd(d_cqn * gq, cq_hat, rq)
        d_ckv = _rms_bwd(d_ckvn * gkv, ckv_hat, rkv)
        d_za = jnp.concatenate([d_cq, d_ckv, d_kr], axis=1).astype(MXU)
        dza_ref[...] = d_za
        d_h1 = _dot_nt(d_za, w1a_ref[...]) + _dot_nt(dguv_ref[...], w1b_ref[...])
        sc1 = _row(ar_ref, 1) + _row(ab_ref, 1)
        g = g_ref[...]
        xn, r1 = _rms(x_ref[...])
        dsh_ref[...] += jnp.sum(d_h1, axis=0, keepdims=True)
        dsc_ref[...] += jnp.sum(d_h1 * (xn * g), axis=0, keepdims=True)
        d_mod = d_h1 * (1.0 + sc1)
        dg_ref[...] += jnp.sum(d_mod * xn, axis=0, keepdims=True)
        gx_ref[...] = dxp_ref[...] + _rms_bwd(d_mod * g, xn, r1)

    vec = pl.BlockSpec((1, D_MODEL), lambda i: (0, 0))
    return _call(
        body, name="mix_in_bwd", grid=(s // tm,), sem=("arbitrary",),
        in_specs=[_rows(tm, hp), _rows(tm, hp), _rows(tm, hp), _rows(tm, za), _rows(tm, 2 * GM_WIDTH),
                  _rows(tm, D_MODEL), _rows(tm, D_MODEL), _full(ada_raw.shape), _full(ada_b.shape), _full(g_pre.shape),
                  _full(g_q.shape), _full(g_kv.shape), _full(w1a.shape), _full(w1b.shape), _full(wq.shape),
                  _full(wkv.shape), _rows(tm, HEAD_PAD), _rows(tm, HEAD_PAD)],
        out_specs=[_rows(tm, D_MODEL), _rows(tm, za), _rows(tm, hp), _rows(tm, 2 * hp), vec, vec, vec,
                   _full(g_q.shape), _full(g_kv.shape)],
        out_shape=[_sds((s, D_MODEL), F32), _sds((s, za), MXU), _sds((s, hp), MXU), _sds((s, 2 * hp), MXU),
                   _sds((1, D_MODEL), F32), _sds((1, D_MODEL), F32), _sds((1, D_MODEL), F32),
                   _sds(g_q.shape, F32), _sds(g_kv.shape, F32)],
    )(dq, dk, dv, z, d_guv, x, d_x_part, ada_raw, ada_b, g_pre, g_q, g_kv, w1a, w1b, wq, wkv, cos_t, sin_t)


def _tn_matmul(a, b, name, ts):
    ga, s, m = a.shape
    gb, _, n = b.shape
    g = max(ga, gb)
    tn = n if n <= 1024 else 1024
    steps = s // ts

    def body(a_ref, b_ref, o_ref, acc):
        k = pl.program_id(2)

        @pl.when(k == 0)
        def _():
            acc[...] = jnp.zeros(acc.shape, F32)

        acc[...] += _dot_tn(a_ref[0], b_ref[0])

        @pl.when(k == steps - 1)
        def _():
            o_ref[0] = acc[...].astype(MXU)

    return _call(
        body, name=name, grid=(g, n // tn, steps), sem=("parallel", "parallel", "arbitrary"),
        in_specs=[pl.BlockSpec((1, ts, m), lambda gi, ni, k: (gi if ga > 1 else 0, k, 0)),
                  pl.BlockSpec((1, ts, tn), lambda gi, ni, k: (gi if gb > 1 else 0, k, ni))],
        out_specs=pl.BlockSpec((1, m, tn), lambda gi, ni, k: (gi, 0, ni)),
        out_shape=_sds((g, m, n), MXU),
        scratch=[pltpu.VMEM((m, tn), F32)],
    )(a, b)


def _adamw(w, g, m, v):
    m2 = ADAM_B1 * m + (1.0 - ADAM_B1) * g
    v2 = ADAM_B2 * v + (1.0 - ADAM_B2) * (g * g)
    m_hat = m2 / (1.0 - ADAM_B1 ** ADAM_STEP)
    v_hat = v2 / (1.0 - ADAM_B2 ** ADAM_STEP)
    delta = -ADAM_LR * (m_hat / (jnp.sqrt(v_hat) + ADAM_EPS) + ADAM_WD * w)
    return delta, m2, v2


def _adam_reduce(recv, w, m, v, name):
    r, c = w.shape
    tr = r if r <= 512 else 256

    def body(p_ref, w_ref, m_ref, v_ref, g_ref, d_ref, mo_ref, vo_ref):
        g = p_ref[0].astype(F32)
        for j in range(1, N_DEV):
            g = g + p_ref[j].astype(F32)
        g_ref[...] = g
        d_ref[...], mo_ref[...], vo_ref[...] = _adamw(w_ref[...], g, m_ref[...], v_ref[...])

    blk = pl.BlockSpec((tr, c), lambda i: (i, 0))
    return _call(
        body, name=name, grid=(r // tr,), sem=("parallel",),
        in_specs=[pl.BlockSpec((N_DEV, tr, c), lambda i: (0, i, 0)), blk, blk, blk],
        out_specs=[blk] * 4, out_shape=[_sds((r, c), F32)] * 4,
    )(recv, w, m, v)


def _adam_direct(g, w, m, v, name):
    def body(g_ref, w_ref, m_ref, v_ref, d_ref, mo_ref, vo_ref):
        d_ref[...], mo_ref[...], vo_ref[...] = _adamw(w_ref[...], g_ref[...], m_ref[...], v_ref[...])

    return _call(body, name=name, grid=(1,), in_specs=[_full(w.shape)] * 4, out_specs=[_full(w.shape)] * 3,
                 out_shape=[_sds(w.shape, F32)] * 3)(g, w, m, v)


def _adam_w_ada(c_act_t, d_ada_cols, w, m, v):
    r, c = w.shape
    tr = 256

    def body(ct_ref, da_ref, w_ref, m_ref, v_ref, g_ref, d_ref, mo_ref, vo_ref):
        g = ct_ref[:, pl.ds(0, 1)] * da_ref[pl.ds(0, 1), :]
        for b in range(1, N_DEV):
            g = g + ct_ref[:, pl.ds(b, 1)] * da_ref[pl.ds(b, 1), :]
        g_ref[...] = g
        d_ref[...], mo_ref[...], vo_ref[...] = _adamw(w_ref[...], g, m_ref[...], v_ref[...])

    blk = pl.BlockSpec((tr, c), lambda i: (i, 0))
    return _call(
        body, name="adam_w_ada", grid=(r // tr,), sem=("parallel",),
        in_specs=[pl.BlockSpec((tr, N_DEV), lambda i: (i, 0)), _full(d_ada_cols.shape), blk, blk, blk],
        out_specs=[blk] * 4, out_shape=[_sds((r, c), F32)] * 4,
    )(c_act_t, d_ada_cols, w, m, v)


def _adam_small(gathered, w, m, v):
    rows, lanes = w.shape

    def body(p_ref, w_ref, m_ref, v_ref, g_ref, d_ref, mo_ref, vo_ref):
        g = p_ref[0]
        for j in range(1, N_DEV):
            g = g + p_ref[j]
        g_ref[...] = g
        d_ref[...], mo_ref[...], vo_ref[...] = _adamw(w_ref[...], g, m_ref[...], v_ref[...])

    return _call(body, name="adam_small", grid=(1,),
                 in_specs=[_full(gathered.shape)] + [_full(w.shape)] * 3, out_specs=[_full(w.shape)] * 4,
                 out_shape=[_sds(w.shape, F32)] * 4)(gathered, w, m, v)


def _rope_tables(s):
    pos = jnp.arange(s, dtype=F32)
    inv = ROPE_THETA ** (-jnp.arange(0, ROPE, 2, dtype=F32) / ROPE)
    ang = pos[:, None] * inv[None, :]
    cos, sin = jnp.cos(ang), jnp.sin(ang)
    ones, zeros = jnp.ones((s, NOPE), F32), jnp.zeros((s, NOPE), F32)
    cos_t = jnp.concatenate([ones, cos, cos, ones[:, :HEAD_PAD - NOPE - ROPE]], axis=1)
    sin_t = jnp.concatenate([zeros, sin, sin, zeros[:, :HEAD_PAD - NOPE - ROPE]], axis=1)
    return cos_t, sin_t


def _pack(parts, rows):
    flat = jnp.concatenate([p.reshape(-1) for p in parts])
    return jnp.pad(flat, (0, rows * 128 - flat.shape[0])).reshape(rows, 128)


def kernel(x, c, w_ada, b_ada, g_pre_mix, g_post_mix, w_in, g_q, w_uq, g_kv, w_ukv, gm_ln_g, gm_ln_b, w_spatial, b_spatial, w_out, g_pre_ffn, g_post_ffn, w_up, conv_w, conv_b, w_down, loss_target, m_w_ada, m_b_ada, m_g_pre_mix, m_g_post_mix, m_w_in, m_g_q, m_w_uq, m_g_kv, m_w_ukv, m_gm_ln_g, m_gm_ln_b, m_w_spatial, m_b_spatial, m_w_out, m_g_pre_ffn, m_g_post_ffn, m_w_up, m_conv_w, m_conv_b, m_w_down, v_w_ada, v_b_ada, v_g_pre_mix, v_g_post_mix, v_w_in, v_g_q, v_w_uq, v_g_kv, v_w_ukv, v_gm_ln_g, v_gm_ln_b, v_w_spatial, v_b_spatial, v_w_out, v_g_pre_ffn, v_g_post_ffn, v_w_up, v_conv_w, v_conv_b, v_w_down):
    s = x.shape[1]
    tm = min(256, s)
    tq = min(512, s)
    ts = min(512, s)
    hp = HEADS * HEAD_PAD
    half = N_DEV // 2
    my_slot = 4 * lax.axis_index("x") + 2 * lax.axis_index("y") + lax.axis_index("c")
    x2d, target = x[0], loss_target[0]

    (g_c, g_in, g_uq, g_ukv, g_cw), _ = _exchange(
        [c, w_in[0].astype(MXU), w_uq[0].astype(MXU), w_ukv[0].astype(MXU), conv_w[0]], [], "gather_mixer_weights")

    w_in_f = jnp.transpose(g_in, (1, 0, 2)).reshape(D_MODEL, -1)
    o1, o2, o3 = Q_LORA, Q_LORA + KV_LORA, Q_LORA + KV_LORA + ROPE
    w1 = jnp.concatenate([w_in_f[:, :o2], jnp.zeros((D_MODEL, NOPE), MXU), w_in_f[:, o2:o3],
                          jnp.zeros((D_MODEL, HEAD_PAD - NOPE - ROPE), MXU), w_in_f[:, o3:]], axis=1)
    w_uq_f = jnp.transpose(g_uq, (1, 0, 2)).reshape(Q_LORA, HEADS, NOPE + ROPE)
    wq = jnp.pad(w_uq_f, ((0, 0), (0, 0), (0, HEAD_PAD - NOPE - ROPE))).reshape(Q_LORA, hp)
    w_ukv_f = jnp.transpose(g_ukv, (1, 0, 2)).reshape(KV_LORA, HEADS, 2 * NOPE)
    pad_head = ((0, 0), (0, 0), (0, HEAD_PAD - NOPE))
    wkv = jnp.concatenate([jnp.pad(w_ukv_f[:, :, :NOPE], pad_head).reshape(KV_LORA, hp),
                           jnp.pad(w_ukv_f[:, :, NOPE:], pad_head).reshape(KV_LORA, hp)], axis=1)
    cb8 = conv_b.reshape(N_DEV, 1, FF_BLK)
    bias_exp = jnp.repeat(b_spatial[0].T, GM_DIM, axis=1)
    ln_g, ln_b = gm_ln_g.reshape(1, GM_WIDTH), gm_ln_b.reshape(1, GM_WIDTH)
    w_sp = w_spatial[0]
    cos_t, sin_t = _rope_tables(s)

    ada_part, c_act = _ada_fwd(g_c.reshape(N_DEV, D_MODEL), w_ada[0])
    _, (ada_recv,) = _exchange([], [(ada_part.reshape(N_DEV, 1, -1), _plain_slot)], "ada_rows")
    ada_raw = ada_recv.reshape(6, D_MODEL)
    ada_b = b_ada.reshape(6, D_MODEL)

    h1, z, qp, kp, vp, cqn, ckvn = _mix_in_fwd(x2d, ada_raw, ada_b, g_pre_mix, w1, g_q, g_kv, wq, wkv, cos_t, sin_t, tm)
    sgu = _gmlp_fwd(z, ln_g, ln_b, w_sp, bias_exp, tm)
    o_pad, lse, (g_out, g_up, g_down) = _attn_fwd(
        qp, kp, vp, tq, [w_out[0].astype(MXU), w_up[0].astype(MXU), w_down[0].astype(MXU)])
    w_out_f = g_out.reshape(2 * GM_WIDTH, D_MODEL)
    wo_attn = jnp.pad(w_out_f[:GM_WIDTH].reshape(HEADS, NOPE, D_MODEL), ((0, 0), (0, HEAD_PAD - NOPE), (0, 0)))
    wo = jnp.concatenate([wo_attn.reshape(hp, D_MODEL), w_out_f[GM_WIDTH:]], axis=0)
    wd = g_down.reshape(half, FF_BLK, D_MODEL)
    m_mix, x2, h2 = _out_proj_fwd(o_pad, sgu, wo, x2d, ada_raw, ada_b, g_post_mix, g_pre_ffn, tm)
    up_a, up_b, act = _ffn_up_fwd(h2, g_up, g_cw, cb8, tm)
    d_out, d_f, loss_part, d_gt2, d_g_post_ffn = _ffn_down_fwd(act, wd, x2, target, ada_raw, ada_b, g_post_ffn, tm)
    loss = lax.psum(loss_part[0, 0], ("x", "y", "c"))

    d_up, dcw_a, dcw_b, dcb_a, dcb_b = _ffn_down_bwd(d_f, wd, up_a, up_b, g_cw, cb8, tm)
    d_x2, d_m, d_sh2, d_sc2, d_g_pre_ffn, d_gt1, d_g_post_mix = _ffn_up_bwd(
        d_up, g_up, x2, m_mix, d_out, ada_raw, ada_b, g_pre_ffn, g_post_mix, tm)
    p_down = _tn_matmul(act, d_f[None], "dw_down", ts).reshape(N_DEV, -1, D_MODEL)
    p_up = _tn_matmul(h2[None], d_up.reshape(N_DEV, s, FF_BLK), "dw_up", ts).reshape(half, 2, D_MODEL, FF_BLK)
    d_m3 = d_m[None]
    dwo_attn = _tn_matmul(o_pad[None], d_m3, "dw_out_attn", ts)[0].reshape(HEADS, HEAD_PAD, D_MODEL)[:, :NOPE]
    dwo_sgu = _tn_matmul(sgu[None], d_m3, "dw_out_sgu", ts)[0]
    p_out = jnp.concatenate([dwo_attn.reshape(GM_WIDTH, D_MODEL), dwo_sgu], axis=0).reshape(N_DEV, -1, D_MODEL)
    d_o, d_sgu, delta = _out_proj_bwd(d_m, wo, o_pad, tm)
    dq, dk, dv, (r_out, r_up, r_down) = _attn_bwd(
        qp, kp, vp, d_o, lse, delta, tq,
        [(p_out, _plain_slot), (p_up, lambda j: (j % half, j // half)), (p_down, _plain_slot)])
    d_guv, d_ws, d_bs, d_ln_g, d_ln_b = _gmlp_bwd(z, d_sgu, ln_g, ln_b, w_sp, bias_exp, tm)
    za = Q_LORA + KV_LORA + HEAD_PAD
    grad_x, d_za, d_qp, d_kvp, d_sh1, d_sc1, d_g_pre_mix, d_g_q, d_g_kv = _mix_in_bwd(
        dq, dk, dv, z, d_guv, x2d, d_x2, ada_raw, ada_b, g_pre_mix, g_q, g_kv, w1[:, :za], w1[:, za:], wq, wkv,
        cos_t, sin_t, tm)
    h1_3 = h1[None]
    dw1a = _tn_matmul(h1_3, d_za[None], "dw_in_a", ts)[0]
    dw1b = _tn_matmul(h1_3, d_guv[None], "dw_in_b", ts)[0]
    d_w_in = jnp.concatenate([dw1a[:, :o2], dw1a[:, o2 + NOPE:o2 + NOPE + ROPE], dw1b], axis=1)
    p_in = jnp.transpose(d_w_in.reshape(D_MODEL, N_DEV, -1), (1, 0, 2))
    dwq = _tn_matmul(cqn[None], d_qp[None], "dw_uq", ts)[0].reshape(Q_LORA, HEADS, HEAD_PAD)[:, :, :NOPE + ROPE]
    p_uq = jnp.transpose(dwq.reshape(Q_LORA, N_DEV, -1), (1, 0, 2))
    dwkv = _tn_matmul(ckvn[None], d_kvp[None], "dw_ukv", ts)[0]
    dwk = dwkv[:, :hp].reshape(KV_LORA, HEADS, HEAD_PAD)[:, :, :NOPE]
    dwv = dwkv[:, hp:].reshape(KV_LORA, HEADS, HEAD_PAD)[:, :, :NOPE]
    p_ukv = jnp.transpose(jnp.concatenate([dwk, dwv], axis=2), (1, 0, 2))

    d_ada = jnp.concatenate([d_sh1, d_sc1, d_gt1, d_sh2, d_sc2, d_gt2], axis=1)
    d_cw = jnp.concatenate([dcw_a, dcw_b], axis=0)
    d_cb = jnp.concatenate([dcb_a, dcb_b], axis=0)
    small_g = [d_ada, d_g_pre_mix, d_g_post_mix, d_g_pre_ffn, d_g_post_ffn, d_g_q, d_g_kv, d_ln_g, d_ln_b, d_ws, d_bs,
               d_cb, d_cw]
    sizes = [int(p.size) for p in small_g]
    rows = -(-sum(sizes) // (8 * 128)) * 8
    zero_cw = jnp.zeros_like(d_cw)
    small_w = [b_ada, g_pre_mix, g_post_mix, g_pre_ffn, g_post_ffn, g_q, g_kv, gm_ln_g, gm_ln_b, w_spatial, b_spatial,
               conv_b, zero_cw]
    small_m = [m_b_ada, m_g_pre_mix, m_g_post_mix, m_g_pre_ffn, m_g_post_ffn, m_g_q, m_g_kv, m_gm_ln_g, m_gm_ln_b,
               m_w_spatial, m_b_spatial, m_conv_b, zero_cw]
    small_v = [v_b_ada, v_g_pre_mix, v_g_post_mix, v_g_pre_ffn, v_g_post_ffn, v_g_q, v_g_kv, v_gm_ln_g, v_gm_ln_b,
               v_w_spatial, v_b_spatial, v_conv_b, zero_cw]
    (g_small,), (r_in, r_uq, r_ukv) = _exchange(
        [_pack(small_g, rows)], [(p_in, _plain_slot), (p_uq, _plain_slot), (p_ukv, _plain_slot)], "final_exchange")
    sm_g, sm_d, sm_m, sm_v = _adam_small(g_small, _pack(small_w, rows), _pack(small_m, rows), _pack(small_v, rows))

    def unpack(packed, shapes):
        flat = packed.reshape(-1)
        out, off = [], 0
        for size, shape in zip(sizes, shapes):
            out.append(flat[off:off + size].reshape(shape))
            off += size
        return out

    small_shapes = [p.shape for p in small_w]
    sg, sd, smm, svv = (unpack(p, small_shapes) for p in (sm_g, sm_d, sm_m, sm_v))
    (g_b_ada, g_g_pre_mix, g_g_post_mix, g_g_pre_ffn, g_g_post_ffn, g_g_q, g_g_kv, g_ln_g, g_ln_b, g_w_sp, g_b_sp,
     g_conv_b, g_cw_all) = sg

    def big(recv, w, m, v, name):
        g, d, m2, v2 = _adam_reduce(recv, w[0], m[0], v[0], name)
        return g[None], d[None], m2[None], v2[None]

    a_in = big(r_in, w_in, m_w_in, v_w_in, "adam_w_in")
    a_uq = big(r_uq, w_uq, m_w_uq, v_w_uq, "adam_w_uq")
    a_ukv = big(r_ukv, w_ukv, m_w_ukv, v_w_ukv, "adam_w_ukv")
    a_out = big(r_out, w_out, m_w_out, v_w_out, "adam_w_out")
    a_up = big(r_up, w_up, m_w_up, v_w_up, "adam_w_up")
    a_down = big(r_down, w_down, m_w_down, v_w_down, "adam_w_down")
    ada_cols = w_ada.shape[2]
    d_ada_all = g_small.reshape(N_DEV, -1)[:, :6 * D_MODEL]
    d_ada_cols = lax.dynamic_slice(d_ada_all, (0, my_slot * ada_cols), (N_DEV, ada_cols))
    a_ada = tuple(t[None] for t in _adam_w_ada(c_act.T, d_ada_cols, w_ada[0], m_w_ada[0], v_w_ada[0]))
    g_cw_mine = lax.dynamic_slice(g_cw_all, (my_slot, 0, 0), (1, 3, FF_BLK))
    a_cw = (g_cw_mine,) + tuple(_adam_direct(g_cw_mine, conv_w, m_conv_w, v_conv_w, "adam_conv_w"))

    def small(k):
        return sg[k], sd[k], smm[k], svv[k]

    per_weight = [a_ada, small(0), small(1), small(2), a_in, small(5), a_uq, small(6), a_ukv, small(7), small(8),
                  small(9), small(10), a_out, small(3), small(4), a_up, a_cw, small(11), a_down]
    outs = [loss, grad_x[None]]
    for k in range(4):
        outs += [t[k] for t in per_weight]
    return tuple(outs)
```

```python
import functools

import jax
import jax.numpy as jnp
from jax import lax
from jax.experimental import pallas as pl
from jax.experimental.pallas import tpu as pltpu

F32 = jnp.float32
MXU = jnp.bfloat16

N_DEV = 8
D_MODEL = 1024
HEADS = 8
HEAD_PAD = 128
NOPE = 64
ROPE = 32
Q_LORA = 256
KV_LORA = 128
GM_WIDTH = 512
GM_DIM = 64
GM_CHUNK = 128
CHUNK_SHIFT = 6
ROPE_THETA = 10000.0
ATTN_SCALE = (NOPE + ROPE) ** -0.5
Z_COLS = 1536
FF_BLK = 704
EPS = 1e-6
ADAM_LR = 0.001
ADAM_B1 = 0.9
ADAM_B2 = 0.999
ADAM_EPS = 1e-08
ADAM_WD = 0.01
ADAM_STEP = 10
VMEM_LIMIT = 56 * 1024 * 1024
MESH = pl.DeviceIdType.MESH


def _dot(a, b):
    return jnp.dot(a, b, preferred_element_type=F32)


def _dot_nt(a, b):
    return lax.dot_general(a, b, (((1,), (1,)), ((), ())), preferred_element_type=F32)


def _dot_tn(a, b):
    return lax.dot_general(a, b, (((0,), (0,)), ((), ())), preferred_element_type=F32)


def _call(body, *, name, grid, in_specs, out_specs, out_shape, scratch=(), sem=None):
    params = pltpu.CompilerParams(dimension_semantics=sem, vmem_limit_bytes=VMEM_LIMIT)
    return pl.pallas_call(body, name=name, grid=grid, in_specs=in_specs, out_specs=out_specs,
                          out_shape=out_shape, scratch_shapes=list(scratch), compiler_params=params)


def _full(shape):
    n = len(shape)
    return pl.BlockSpec(shape, lambda *_: (0,) * n)


def _rows(tm, cols, col_block=0):
    return pl.BlockSpec((tm, cols), lambda i: (i, col_block))


def _sds(shape, dtype):
    return jax.ShapeDtypeStruct(shape, dtype)


def _row(ref, k):
    return ref[pl.ds(k, 1), :]


def _rms(x):
    r = lax.rsqrt(jnp.mean(x * x, axis=-1, keepdims=True) + EPS)
    return x * r, r


def _rms_bwd(d_hat, hat, r):
    return r * (d_hat - hat * jnp.mean(d_hat * hat, axis=-1, keepdims=True))


def _rope_partner(t):
    lane = lax.broadcasted_iota(jnp.int32, t.shape, 1)
    swapped = jnp.where(lane < NOPE + ROPE // 2, -pltpu.roll(t, HEAD_PAD - ROPE // 2, 1), pltpu.roll(t, ROPE // 2, 1))
    return jnp.where((lane >= NOPE) & (lane < NOPE + ROPE), swapped, 0.0)


def _rope(t, cos, sin):
    return t * cos + _rope_partner(t) * sin


def _rope_transposed(g, cos, sin):
    return g * cos - _rope_partner(g * sin)


def _gelu(x):
    return x * (0.5 * (1.0 + jnp.tanh(0.7978845608028654 * (x + 0.044715 * (x * x * x)))))


def _gelu_grad(x):
    t = jnp.tanh(0.7978845608028654 * (x + 0.044715 * (x * x * x)))
    return 0.5 * (1.0 + t) + 0.5 * x * (1.0 - t * t) * (0.7978845608028654 * (1.0 + 3.0 * 0.044715 * (x * x)))


def _split_dot(x, mat):
    hi = x.astype(MXU)
    lo = (x - hi.astype(F32)).astype(MXU)
    return _dot(hi, mat) + _dot(lo, mat)


def _seg_matrix():
    r = lax.broadcasted_iota(jnp.int32, (GM_WIDTH, GM_WIDTH), 0) >> 6
    c = lax.broadcasted_iota(jnp.int32, (GM_WIDTH, GM_WIDTH), 1) >> 6
    return jnp.where(r == c, 1.0 / GM_DIM, 0.0).astype(MXU)


def _spatial_mask():
    i = lax.broadcasted_iota(jnp.int32, (GM_CHUNK, GM_CHUNK), 0) >> CHUNK_SHIFT
    j = lax.broadcasted_iota(jnp.int32, (GM_CHUNK, GM_CHUNK), 1) >> CHUNK_SHIFT
    return (j <= i).astype(F32)


def _head_lane_mask(h, rows):
    lane = lax.broadcasted_iota(jnp.int32, (rows, GM_WIDTH), 1) >> 6
    return lane == h


def _my_place():
    return lax.axis_index("x"), lax.axis_index("y"), lax.axis_index("c")


def _flat(p):
    return 4 * p[0] + 2 * p[1] + p[2]


def _comm_sems(n):
    return [pltpu.SemaphoreType.DMA((7 * n,)), pltpu.SemaphoreType.DMA((7 * n,)), pltpu.SemaphoreType.DMA((n,))]


def _gather_steps(ins, outs, sems):
    send_sems, recv_sems, local_sems = sems
    n = len(ins)
    x, y, c = _my_place()
    me, sibling = (x, y, c), (x, y, 1 - c)
    chips = [(1 - x, y), (x, 1 - y), (1 - x, 1 - y)]

    def copy(a, k, block, to, src=None):
        slot = outs[a].at[_flat(block)]
        return pltpu.make_async_remote_copy(
            src_ref=slot if src is None else src, dst_ref=slot,
            send_sem=send_sems.at[7 * a + k], recv_sem=recv_sems.at[7 * a + k],
            device_id=to, device_id_type=MESH)

    def mine():
        return [pltpu.make_async_copy(ins[a], outs[a].at[_flat(me)], local_sems.at[a]) for a in range(n)]

    def first():
        cps = []
        for a in range(n):
            cps.append(copy(a, 0, me, sibling, src=ins[a]))
            cps += [copy(a, 1 + j, me, (*chip, c), src=ins[a]) for j, chip in enumerate(chips)]
        return cps

    def passed():
        return [copy(a, 4 + j, (*chip, c), sibling) for a in range(n) for j, chip in enumerate(chips)]

    def start():
        for cp in mine() + first():
            cp.start()

    def forward():
        for a in range(n):
            for j, chip in enumerate(chips):
                copy(a, 1 + j, (*chip, c), me).wait_recv()
                copy(a, 4 + j, (*chip, c), sibling).start()

    def finish():
        for a in range(n):
            copy(a, 0, sibling, me).wait_recv()
            for j, chip in enumerate(chips):
                copy(a, 4 + j, (*chip, 1 - c), me).wait_recv()
        for cp in first() + passed():
            cp.wait_send()
        for cp in mine():
            cp.wait()

    return start, forward, finish


def _scatter_steps(ins, outs, sems, slots):
    send_sems, recv_sems, local_sems = sems
    n = len(ins)
    flips = [(fx, fy, fc) for fx in (0, 1) for fy in (0, 1) for fc in (0, 1)][1:]
    me = _my_place()

    def peer(f):
        return tuple(1 - v if b else v for v, b in zip(me, f))

    def copy(a, k, arriving=False):
        p = peer(flips[k])
        return pltpu.make_async_remote_copy(
            src_ref=ins[a].at[slots[a](_flat(p))], dst_ref=outs[a].at[_flat(p if arriving else me)],
            send_sem=send_sems.at[7 * a + k], recv_sem=recv_sems.at[7 * a + k],
            device_id=p, device_id_type=MESH)

    def mine():
        return [pltpu.make_async_copy(ins[a].at[slots[a](_flat(me))], outs[a].at[_flat(me)], local_sems.at[a])
                for a in range(n)]

    def start():
        for cp in mine() + [copy(a, k) for a in range(n) for k in range(7)]:
            cp.start()

    def finish():
        for a in range(n):
            for k in range(7):
                copy(a, k, arriving=True).wait_recv()
        for a in range(n):
            for k in range(7):
                copy(a, k).wait_send()
        for cp in mine():
            cp.wait()

    return start, finish


def _plain_slot(j):
    return (j,)


def _scatter_out_shape(arr, slot):
    return _sds((N_DEV,) + arr.shape[len(slot(0)):], arr.dtype)


def _exchange(gathered, scattered, name):
    ng, ns = len(gathered), len(scattered)
    slots = [slot for _, slot in scattered]

    def body(*refs):
        g_in, s_in = refs[:ng], refs[ng:ng + ns]
        g_out, s_out = refs[ng + ns:2 * ng + ns], refs[2 * ng + ns:2 * (ng + ns)]
        sems = refs[2 * (ng + ns):]
        g_start, g_forward, g_finish = _gather_steps(g_in, g_out, sems[:3])
        s_start, s_finish = _scatter_steps(s_in, s_out, sems[3:], slots)
        g_start()
        s_start()
        g_forward()
        g_finish()
        s_finish()

    any_spec = pl.BlockSpec(memory_space=pl.ANY)
    outs = pl.pallas_call(
        body, name=name,
        in_specs=[any_spec] * (ng + ns), out_specs=[any_spec] * (ng + ns),
        out_shape=[_sds((N_DEV,) + a.shape, a.dtype) for a in gathered]
        + [_scatter_out_shape(a, slot) for a, slot in scattered],
        scratch_shapes=_comm_sems(max(ng, 1)) + _comm_sems(max(ns, 1)),
    )(*gathered, *[a for a, _ in scattered])
    return outs[:ng], outs[ng:]


def _ada_fwd(c_all, w_ada):
    def body(c_ref, w_ref, part_ref, act_ref):
        cv = c_ref[...]
        act = cv * jax.nn.sigmoid(cv)
        act_ref[...] = act
        part_ref[...] = _dot(act.astype(MXU), w_ref[...].astype(MXU))

    cols = w_ada.shape[1]
    return _call(body, name="ada_fwd", grid=(1,),
                 in_specs=[_full(c_all.shape), _full(w_ada.shape)],
                 out_specs=[_full((N_DEV, cols)), _full(c_all.shape)],
                 out_shape=[_sds((N_DEV, cols), F32), _sds(c_all.shape, F32)])(c_all, w_ada)


def _mix_in_fwd(x, ada_raw, ada_b, g_pre, w1, g_q, g_kv, wq, wkv, cos_t, sin_t, tm):
    s = x.shape[0]

    def body(x_ref, ar_ref, ab_ref, g_ref, w1_ref, gq_ref, gkv_ref, wq_ref, wkv_ref, cos_ref, sin_ref,
             h1_ref, z_ref, qp_ref, kp_ref, vp_ref, cqn_ref, ckvn_ref):
        sh = _row(ar_ref, 0) + _row(ab_ref, 0)
        sc = _row(ar_ref, 1) + _row(ab_ref, 1)
        xn, _ = _rms(x_ref[...])
        hb = ((xn * g_ref[...]) * (1.0 + sc) + sh).astype(MXU)
        h1_ref[...] = hb
        z = _dot(hb, w1_ref[...])
        z_ref[...] = z
        cos, sin = cos_ref[...], sin_ref[...]
        cqn = (_rms(z[:, :Q_LORA])[0] * gq_ref[...]).astype(MXU)
        ckvn = (_rms(z[:, Q_LORA:Q_LORA + KV_LORA])[0] * gkv_ref[...]).astype(MXU)
        cqn_ref[...] = cqn
        ckvn_ref[...] = ckvn
        q = _dot(cqn, wq_ref[...])
        kv = _dot(ckvn, wkv_ref[...])
        k_rope = _rope(z[:, Q_LORA + KV_LORA:Q_LORA + KV_LORA + HEAD_PAD], cos, sin)
        for h in range(HEADS):
            blk = slice(h * HEAD_PAD, (h + 1) * HEAD_PAD)
            qp_ref[:, blk] = _rope(q[:, blk], cos, sin).astype(MXU)
            kp_ref[:, blk] = (kv[:, blk] + k_rope).astype(MXU)
        v_lane = lax.broadcasted_iota(jnp.int32, (tm, HEADS * HEAD_PAD), 1) & (HEAD_PAD - 1)
        vp_ref[...] = jnp.where(v_lane == NOPE, 1.0, kv[:, HEADS * HEAD_PAD:]).astype(MXU)

    hp = HEADS * HEAD_PAD
    return _call(
        body, name="mix_in_fwd", grid=(s // tm,), sem=("parallel",),
        in_specs=[_rows(tm, D_MODEL), _full(ada_raw.shape), _full(ada_b.shape), _full(g_pre.shape), _full(w1.shape),
                  _full(g_q.shape), _full(g_kv.shape), _full(wq.shape), _full(wkv.shape),
                  _rows(tm, HEAD_PAD), _rows(tm, HEAD_PAD)],
        out_specs=[_rows(tm, D_MODEL), _rows(tm, Z_COLS), _rows(tm, hp), _rows(tm, hp), _rows(tm, hp),
                   _rows(tm, Q_LORA), _rows(tm, KV_LORA)],
        out_shape=[_sds((s, D_MODEL), MXU), _sds((s, Z_COLS), F32), _sds((s, hp), MXU), _sds((s, hp), MXU),
                   _sds((s, hp), MXU), _sds((s, Q_LORA), MXU), _sds((s, KV_LORA), MXU)],
    )(x, ada_raw, ada_b, g_pre, w1, g_q, g_kv, wq, wkv, cos_t, sin_t)


def _gm_norm(zv, seg):
    gv = _gelu(zv)
    cen = gv - _split_dot(gv, seg)
    rstd = lax.rsqrt(_split_dot(cen * cen, seg) + EPS)
    return gv, cen * rstd, rstd


def _gm_mix(wm, vb, rows):
    out = jnp.zeros((rows, GM_WIDTH), F32)
    for h in range(HEADS):
        out = out + jnp.where(_head_lane_mask(h, rows), _dot(wm[h], vb), 0.0)
    return out


def _gmlp_fwd(z, ln_g, ln_b, w_sp, bias_exp, tm):
    s = z.shape[0]
    nblk = tm // GM_CHUNK

    def body(zu_ref, zv_ref, lg_ref, lb_ref, w_ref, be_ref, sgu_ref):
        seg = _seg_matrix()
        mask = _spatial_mask()
        wm = [(w_ref[h] * mask).astype(MXU) for h in range(HEADS)]
        gu = _gelu(zu_ref[...])
        _, vhat, _ = _gm_norm(zv_ref[...], seg)
        vln = (vhat * lg_ref[...] + lb_ref[...]).astype(MXU)
        for n in range(nblk):
            rows = slice(n * GM_CHUNK, (n + 1) * GM_CHUNK)
            mixed = _gm_mix(wm, vln[rows], GM_CHUNK) + be_ref[...]
            sgu_ref[rows, :] = (gu[rows] * mixed).astype(MXU)

    return _call(
        body, name="gmlp_fwd", grid=(s // tm,), sem=("parallel",),
        in_specs=[_rows(tm, GM_WIDTH, 1), _rows(tm, GM_WIDTH, 2), _full(ln_g.shape), _full(ln_b.shape),
                  _full(w_sp.shape), _full(bias_exp.shape)],
        out_specs=_rows(tm, GM_WIDTH), out_shape=_sds((s, GM_WIDTH), MXU),
    )(z, z, ln_g, ln_b, w_sp, bias_exp)


def _diag_mask(t):
    qc = lax.broadcasted_iota(jnp.int32, (t, t), 0) >> CHUNK_SHIFT
    kc = lax.broadcasted_iota(jnp.int32, (t, t), 1) >> CHUNK_SHIFT
    return kc <= qc


NEG_BIG = -1e30
ATTN_HEADS_PER_STEP = 2


def _attn_fwd(qp, kp, vp, tq, gathered):
    s = qp.shape[0]
    nq = s // tq
    hb = ATTN_HEADS_PER_STEP
    groups = HEADS // hb
    width = hb * HEAD_PAD
    ng = len(gathered)

    def body(q_ref, k_ref, v_ref, *rest):
        g_in, (o_ref, lse_ref), g_out = rest[:ng], rest[ng:ng + 2], rest[ng + 2:2 * ng + 2]
        m_sc, acc_sc = rest[2 * ng + 2:2 * ng + 4]
        g_start, g_forward, g_finish = _gather_steps(g_in, g_out, rest[2 * ng + 4:])
        g, i = pl.program_id(0), pl.program_id(1)
        pl.when((g == 0) & (i == 0))(g_start)
        pl.when((g == groups - 1) & (i == 0))(g_forward)
        m_sc[...] = jnp.full(m_sc.shape, NEG_BIG, F32)
        acc_sc[...] = jnp.zeros(acc_sc.shape, F32)

        def tile(j, masked):
            rows = pl.ds(pl.multiple_of(j * tq, tq), tq)
            for hh in range(hb):
                lanes = slice(hh * HEAD_PAD, (hh + 1) * HEAD_PAD)
                sc = _dot_nt(q_ref[:, lanes], k_ref[rows, lanes]) * ATTN_SCALE
                if masked:
                    sc = jnp.where(_diag_mask(tq), sc, NEG_BIG)
                blocks = [sc[:, b * 128:(b + 1) * 128] for b in range(tq // 128)]
                m_prev = m_sc[hh]
                m_tile = jnp.max(functools.reduce(jnp.maximum, blocks), axis=-1, keepdims=True)
                m_new = jnp.maximum(m_prev, m_tile)
                alpha = jnp.exp(m_prev - m_new)
                p = jnp.concatenate([jnp.exp(b - m_new) for b in blocks], axis=1).astype(MXU)
                acc_sc[hh] = alpha * acc_sc[hh] + _dot(p, v_ref[rows, lanes])
                m_sc[hh] = m_new

        def off_diagonal(j, carry):
            tile(j, False)
            return carry

        lax.fori_loop(0, i, off_diagonal, 0)
        tile(i, True)
        for hh in range(hb):
            lanes = slice(hh * HEAD_PAD, (hh + 1) * HEAD_PAD)
            acc = acc_sc[hh]
            denom = acc[:, NOPE:NOPE + 1]
            o_ref[:, lanes] = (acc / denom).astype(MXU)
            lse_ref[hh] = m_sc[hh][:, :1] + jnp.log(denom)
        pl.when((g == groups - 1) & (i == nq - 1))(g_finish)

    q_spec = pl.BlockSpec((tq, width), lambda g, i: (i, g))
    kv_spec = pl.BlockSpec((s, width), lambda g, i: (0, g))
    any_spec = pl.BlockSpec(memory_space=pl.ANY)
    outs = _call(
        body, name="attn_fwd", grid=(groups, nq), sem=("arbitrary", "arbitrary"),
        in_specs=[q_spec, kv_spec, kv_spec] + [any_spec] * ng,
        out_specs=[q_spec, pl.BlockSpec((hb, tq, 1), lambda g, i: (g, i, 0))] + [any_spec] * ng,
        out_shape=[_sds(qp.shape, MXU), _sds((HEADS, s, 1), F32)]
        + [_sds((N_DEV,) + a.shape, a.dtype) for a in gathered],
        scratch=[pltpu.VMEM((hb, tq, HEAD_PAD), F32), pltpu.VMEM((hb, tq, HEAD_PAD), F32)] + _comm_sems(ng),
    )(qp, kp, vp, *gathered)
    return outs[0], outs[1], outs[2:]


def _out_proj_fwd(o_pad, sgu, wo, x, ada_raw, ada_b, g_post_mix, g_pre_ffn, tm):
    s = x.shape[0]
    hp = HEADS * HEAD_PAD

    def body(o_ref, sgu_ref, wo_ref, x_ref, ar_ref, ab_ref, gpm_ref, gpf_ref, m_ref, x2_ref, h2_ref):
        gt1 = _row(ar_ref, 2) + _row(ab_ref, 2)
        sh2 = _row(ar_ref, 3) + _row(ab_ref, 3)
        sc2 = _row(ar_ref, 4) + _row(ab_ref, 4)
        m = _dot(o_ref[...], wo_ref[pl.ds(0, hp), :]) + _dot(sgu_ref[...], wo_ref[pl.ds(hp, GM_WIDTH), :])
        m_ref[...] = m
        x2 = x_ref[...] + gt1 * (_rms(m)[0] * gpm_ref[...])
        x2_ref[...] = x2
        h2_ref[...] = ((_rms(x2)[0] * gpf_ref[...]) * (1.0 + sc2) + sh2).astype(MXU)

    return _call(
        body, name="out_proj_fwd", grid=(s // tm,), sem=("parallel",),
        in_specs=[_rows(tm, hp), _rows(tm, GM_WIDTH), _full(wo.shape), _rows(tm, D_MODEL), _full(ada_raw.shape),
                  _full(ada_b.shape), _full(g_post_mix.shape), _full(g_pre_ffn.shape)],
        out_specs=[_rows(tm, D_MODEL)] * 3,
        out_shape=[_sds((s, D_MODEL), F32), _sds((s, D_MODEL), F32), _sds((s, D_MODEL), MXU)],
    )(o_pad, sgu, wo, x, ada_raw, ada_b, g_post_mix, g_pre_ffn)


def _conv(u, halo, cw_ref, cb_ref):
    ext = jnp.concatenate([halo, u], axis=0)
    m1, m2 = pltpu.roll(ext, 1, 0)[8:], pltpu.roll(ext, 2, 0)[8:]
    return cb_ref[0] + ((m2 * cw_ref[0, pl.ds(0, 1), :] + m1 * cw_ref[0, pl.ds(1, 1), :]) + u * cw_ref[0, pl.ds(2, 1), :])


ROW_SUB = 256


def _sub_blocks(tm):
    return [slice(r, r + ROW_SUB) for r in range(0, tm, ROW_SUB)]


def _ffn_up_fwd(h2, w_up, conv_w, conv_b, tm):
    s = h2.shape[0]
    half = N_DEV // 2

    def body(h_ref, wa_ref, wb_ref, cwa_ref, cwb_ref, cba_ref, cbb_ref,
             ua_ref, ub_ref, ya_ref, yb_ref, act_ref, halo_a, halo_b):
        i = pl.program_id(1)

        @pl.when(i == 0)
        def _():
            halo_a[...] = jnp.zeros(halo_a.shape, F32)
            halo_b[...] = jnp.zeros(halo_b.shape, F32)

        ha, hb = halo_a[...], halo_b[...]
        for rows in _sub_blocks(tm):
            h = h_ref[rows, :]
            ua = _dot(h, wa_ref[0])
            ub = _dot(h, wb_ref[0])
            ua_ref[0, rows, :] = ua
            ub_ref[0, rows, :] = ub
            ya = _conv(ua, ha, cwa_ref, cba_ref)
            yb = _conv(ub, hb, cwb_ref, cbb_ref)
            ya_ref[0, rows, :] = ya
            yb_ref[0, rows, :] = yb
            ha, hb = ua[ROW_SUB - 8:], ub[ROW_SUB - 8:]
            act_ref[0, rows, :] = ((ya * jax.nn.sigmoid(ya)) * yb).astype(MXU)
        halo_a[...] = ha
        halo_b[...] = hb

    def blk(shape, off):
        return pl.BlockSpec(shape, lambda j, i: (j + off, 0, 0))

    def tok(off=0):
        return pl.BlockSpec((1, tm, FF_BLK), lambda j, i: (j + off, i, 0))

    return _call(
        body, name="ffn_up_fwd", grid=(half, s // tm), sem=("parallel", "arbitrary"),
        in_specs=[pl.BlockSpec((tm, D_MODEL), lambda j, i: (i, 0)),
                  blk((1, D_MODEL, FF_BLK), 0), blk((1, D_MODEL, FF_BLK), half),
                  blk((1, 3, FF_BLK), 0), blk((1, 3, FF_BLK), half), blk((1, 1, FF_BLK), 0), blk((1, 1, FF_BLK), half)],
        out_specs=[tok()] * 5,
        out_shape=[_sds((half, s, FF_BLK), F32)] * 4 + [_sds((half, s, FF_BLK), MXU)],
        scratch=[pltpu.VMEM((8, FF_BLK), F32), pltpu.VMEM((8, FF_BLK), F32)],
    )(h2, w_up, w_up, conv_w, conv_w, conv_b, conv_b)


def _ffn_down_fwd(act, wd, x2, target, ada_raw, ada_b, g_post_ffn, tm):
    s = x2.shape[0]
    half = N_DEV // 2

    def body(act_ref, wd_ref, x2_ref, t_ref, ar_ref, ab_ref, g_ref, dout_ref, df_ref, loss_ref, dgt_ref, dg_ref):
        i = pl.program_id(0)

        @pl.when(i == 0)
        def _():
            loss_ref[...] = jnp.zeros(loss_ref.shape, F32)
            dgt_ref[...] = jnp.zeros(dgt_ref.shape, F32)
            dg_ref[...] = jnp.zeros(dg_ref.shape, F32)

        gt2 = _row(ar_ref, 5) + _row(ab_ref, 5)
        g = g_ref[...]
        for rows in _sub_blocks(tm):
            f = _dot(act_ref[0, rows, :], wd_ref[0])
            for j in range(1, half):
                f = f + _dot(act_ref[j, rows, :], wd_ref[j])
            fhat, rf = _rms(f)
            fn = fhat * g
            err = (x2_ref[rows, :] + gt2 * fn) - t_ref[rows, :]
            loss_ref[...] += 0.5 * jnp.sum(jnp.mean(err * err, axis=-1, keepdims=True))
            d_out = err * (1.0 / D_MODEL)
            dout_ref[rows, :] = d_out
            dgt_ref[...] += jnp.sum(d_out * fn, axis=0, keepdims=True)
            d_fn = d_out * gt2
            dg_ref[...] += jnp.sum(d_fn * fhat, axis=0, keepdims=True)
            df_ref[rows, :] = _rms_bwd(d_fn * g, fhat, rf).astype(MXU)

    vec = pl.BlockSpec((1, D_MODEL), lambda i: (0, 0))
    return _call(
        body, name="ffn_down_fwd", grid=(s // tm,), sem=("arbitrary",),
        in_specs=[pl.BlockSpec((half, tm, FF_BLK), lambda i: (0, i, 0)), _full(wd.shape), _rows(tm, D_MODEL),
                  _rows(tm, D_MODEL), _full(ada_raw.shape), _full(ada_b.shape), _full(g_post_ffn.shape)],
        out_specs=[_rows(tm, D_MODEL), _rows(tm, D_MODEL), pl.BlockSpec((1, 128), lambda i: (0, 0)), vec, vec],
        out_shape=[_sds((s, D_MODEL), F32), _sds((s, D_MODEL), MXU), _sds((1, 128), F32),
                   _sds((1, D_MODEL), F32), _sds((1, D_MODEL), F32)],
    )(act, wd, x2, target, ada_raw, ada_b, g_post_ffn)


def _ffn_down_bwd(d_f, wd, up_a, up_b, y_a, y_b, conv_w, tm):
    s = d_f.shape[0]
    half = N_DEV // 2
    nt = s // tm

    def body(df_ref, wd_ref, ua_ref, ub_ref, ya_ref, yb_ref, cwa_ref, cwb_ref,
             dup_ref, dcwa_ref, dcwb_ref, dcba_ref, dcbb_ref, next_a, next_b):
        i = pl.program_id(1)

        @pl.when(i == 0)
        def _():
            next_a[...] = jnp.zeros(next_a.shape, F32)
            next_b[...] = jnp.zeros(next_b.shape, F32)
            for r in (dcwa_ref, dcwb_ref, dcba_ref, dcbb_ref):
                r[...] = jnp.zeros(r.shape, F32)

        def conv_bwd(d_y, u, nxt, cw_ref, part, rows, dcw_ref, dcb_ref):
            ext = jnp.concatenate([d_y, nxt], axis=0)
            p1 = pltpu.roll(ext, ROW_SUB + 7, 0)[:ROW_SUB]
            p2 = pltpu.roll(ext, ROW_SUB + 6, 0)[:ROW_SUB]
            d_u = (d_y * cw_ref[0, pl.ds(2, 1), :] + p1 * cw_ref[0, pl.ds(1, 1), :]) + p2 * cw_ref[0, pl.ds(0, 1), :]
            dup_ref[0, part, rows, :] = d_u.astype(MXU)
            dcb_ref[0] += jnp.sum(d_y, axis=0, keepdims=True)
            dcw_ref[0, pl.ds(0, 1), :] += jnp.sum(p2 * u, axis=0, keepdims=True)
            dcw_ref[0, pl.ds(1, 1), :] += jnp.sum(p1 * u, axis=0, keepdims=True)
            dcw_ref[0, pl.ds(2, 1), :] += jnp.sum(d_y * u, axis=0, keepdims=True)
            return d_y[:8]

        nxa, nxb = next_a[...], next_b[...]
        for rows in reversed(_sub_blocks(tm)):
            d_act = _dot_nt(df_ref[rows, :], wd_ref[0])
            ya, yb = ya_ref[0, rows, :], yb_ref[0, rows, :]
            sig = jax.nn.sigmoid(ya)
            d_ya = d_act * yb * (sig * (1.0 + ya * (1.0 - sig)))
            d_yb = d_act * (ya * sig)
            nxa = conv_bwd(d_ya, ua_ref[0, rows, :], nxa, cwa_ref, 0, rows, dcwa_ref, dcba_ref)
            nxb = conv_bwd(d_yb, ub_ref[0, rows, :], nxb, cwb_ref, 1, rows, dcwb_ref, dcbb_ref)
        next_a[...] = nxa
        next_b[...] = nxb

    def rev(i):
        return nt - 1 - i

    def blk(shape, off):
        return pl.BlockSpec(shape, lambda j, i: (j + off, 0, 0))

    tok = pl.BlockSpec((1, tm, FF_BLK), lambda j, i: (j, rev(i), 0))
    acc3 = pl.BlockSpec((1, 3, FF_BLK), lambda j, i: (j, 0, 0))
    acc1 = pl.BlockSpec((1, 1, FF_BLK), lambda j, i: (j, 0, 0))
    return _call(
        body, name="ffn_down_bwd", grid=(half, nt), sem=("parallel", "arbitrary"),
        in_specs=[pl.BlockSpec((tm, D_MODEL), lambda j, i: (rev(i), 0)), blk((1, FF_BLK, D_MODEL), 0),
                  tok, tok, tok, tok, blk((1, 3, FF_BLK), 0), blk((1, 3, FF_BLK), half)],
        out_specs=[pl.BlockSpec((1, 2, tm, FF_BLK), lambda j, i: (j, 0, rev(i), 0)), acc3, acc3, acc1, acc1],
        out_shape=[_sds((half, 2, s, FF_BLK), MXU),
                   _sds((half, 3, FF_BLK), F32), _sds((half, 3, FF_BLK), F32),
                   _sds((half, 1, FF_BLK), F32), _sds((half, 1, FF_BLK), F32)],
        scratch=[pltpu.VMEM((8, FF_BLK), F32), pltpu.VMEM((8, FF_BLK), F32)],
    )(d_f, wd, up_a, up_b, y_a, y_b, conv_w, conv_w)


def _ffn_up_bwd(d_up, w_up, x2, m, d_out, ada_raw, ada_b, g_pre_ffn, g_post_mix, tm):
    s = x2.shape[0]
    half = N_DEV // 2

    def body(dup_ref, w_ref, x2_ref, m_ref, dout_ref, ar_ref, ab_ref, gpf_ref, gpm_ref,
             dx_ref, dm_ref, dsh_ref, dsc_ref, dgpf_ref, dgt1_ref, dgpm_ref):
        i = pl.program_id(0)

        @pl.when(i == 0)
        def _():
            for r in (dsh_ref, dsc_ref, dgpf_ref, dgt1_ref, dgpm_ref):
                r[...] = jnp.zeros(r.shape, F32)

        gt1 = _row(ar_ref, 2) + _row(ab_ref, 2)
        sc2 = _row(ar_ref, 4) + _row(ab_ref, 4)
        gpf, gpm = gpf_ref[...], gpm_ref[...]
        d_h2 = _dot_nt(dup_ref[0, 0], w_ref[0])
        for j in range(1, half):
            d_h2 = d_h2 + _dot_nt(dup_ref[j, 0], w_ref[j])
        for j in range(half):
            d_h2 = d_h2 + _dot_nt(dup_ref[j, 1], w_ref[half + j])
        x2n, r2 = _rms(x2_ref[...])
        dsh_ref[...] += jnp.sum(d_h2, axis=0, keepdims=True)
        dsc_ref[...] += jnp.sum(d_h2 * (x2n * gpf), axis=0, keepdims=True)
        d_mod = d_h2 * (1.0 + sc2)
        dgpf_ref[...] += jnp.sum(d_mod * x2n, axis=0, keepdims=True)
        d_x2 = dout_ref[...] + _rms_bwd(d_mod * gpf, x2n, r2)
        dx_ref[...] = d_x2
        mhat, rm = _rms(m_ref[...])
        dgt1_ref[...] += jnp.sum(d_x2 * (mhat * gpm), axis=0, keepdims=True)
        d_mn = d_x2 * gt1
        dgpm_ref[...] += jnp.sum(d_mn * mhat, axis=0, keepdims=True)
        dm_ref[...] = _rms_bwd(d_mn * gpm, mhat, rm).astype(MXU)

    vec = pl.BlockSpec((1, D_MODEL), lambda i: (0, 0))
    tok = pl.BlockSpec((half, 2, tm, FF_BLK), lambda i: (0, 0, i, 0))
    return _call(
        body, name="ffn_up_bwd", grid=(s // tm,), sem=("arbitrary",),
        in_specs=[tok, _full(w_up.shape), _rows(tm, D_MODEL), _rows(tm, D_MODEL), _rows(tm, D_MODEL),
                  _full(ada_raw.shape), _full(ada_b.shape), _full(g_pre_ffn.shape), _full(g_post_mix.shape)],
        out_specs=[_rows(tm, D_MODEL), _rows(tm, D_MODEL), vec, vec, vec, vec, vec],
        out_shape=[_sds((s, D_MODEL), F32), _sds((s, D_MODEL), MXU)] + [_sds((1, D_MODEL), F32)] * 5,
    )(d_up, w_up, x2, m, d_out, ada_raw, ada_b, g_pre_ffn, g_post_mix)


def _out_proj_bwd(d_m, wo, o_pad, tm):
    s = d_m.shape[0]
    hp = HEADS * HEAD_PAD

    def body(dm_ref, wo_ref, o_ref, do_ref, dsgu_ref, delta_ref):
        d_cat = _dot_nt(dm_ref[...], wo_ref[...])
        d_o = d_cat[:, :hp]
        do_ref[...] = d_o.astype(MXU)
        dsgu_ref[...] = d_cat[:, hp:]
        prod = d_o * o_ref[...].astype(F32)
        for h in range(HEADS):
            delta_ref[h] = jnp.sum(prod[:, h * HEAD_PAD:(h + 1) * HEAD_PAD], axis=-1, keepdims=True)

    return _call(
        body, name="out_proj_bwd", grid=(s // tm,), sem=("parallel",),
        in_specs=[_rows(tm, D_MODEL), _full(wo.shape), _rows(tm, hp)],
        out_specs=[_rows(tm, hp), _rows(tm, GM_WIDTH), pl.BlockSpec((HEADS, tm, 1), lambda i: (0, i, 0))],
        out_shape=[_sds((s, hp), MXU), _sds((s, GM_WIDTH), F32), _sds((HEADS, s, 1), F32)],
    )(d_m, wo, o_pad)


def _attn_bwd(qp, kp, vp, d_o, lse, delta, tq, scattered):
    s = qp.shape[0]
    nq = s // tq
    hb = ATTN_HEADS_PER_STEP
    groups = HEADS // hb
    width = hb * HEAD_PAD
    ns = len(scattered)
    slots = [slot for _, slot in scattered]

    def body(q_ref, k_ref, v_ref, do_ref, lse_ref, dl_ref, *rest):
        s_in, (dq_ref, dk_ref, dv_ref), s_out = rest[:ns], rest[ns:ns + 3], rest[ns + 3:2 * ns + 3]
        dk_sc, dv_sc = rest[2 * ns + 3:2 * ns + 5]
        s_start, s_finish = _scatter_steps(s_in, s_out, rest[2 * ns + 5:], slots)
        g, j = pl.program_id(0), pl.program_id(1)
        pl.when((g == 0) & (j == 0))(s_start)

        @pl.when(j == 0)
        def _():
            dq_ref[...] = jnp.zeros(dq_ref.shape, F32)

        dk_sc[...] = jnp.zeros(dk_sc.shape, F32)
        dv_sc[...] = jnp.zeros(dv_sc.shape, F32)

        def tile(i, masked):
            rows = pl.ds(pl.multiple_of(i * tq, tq), tq)
            for hh in range(hb):
                lanes = slice(hh * HEAD_PAD, (hh + 1) * HEAD_PAD)
                q, do, k = q_ref[rows, lanes], do_ref[rows, lanes], k_ref[:, lanes]
                sc = _dot_nt(q, k) * ATTN_SCALE
                if masked:
                    sc = jnp.where(_diag_mask(tq), sc, NEG_BIG)
                p = jnp.exp(sc - lse_ref[hh, rows, :])
                dv_sc[hh] += _dot_tn(p.astype(MXU), do)
                dp = _dot_nt(do, v_ref[:, lanes])
                ds = ((p * (dp - dl_ref[hh, rows, :])) * ATTN_SCALE).astype(MXU)
                dk_sc[hh] += _dot_tn(ds, q)
                dq_ref[rows, lanes] += _dot(ds, k)

        def off_diagonal(i, carry):
            tile(i, False)
            return carry

        tile(j, True)
        lax.fori_loop(j + 1, nq, off_diagonal, 0)
        for hh in range(hb):
            lanes = slice(hh * HEAD_PAD, (hh + 1) * HEAD_PAD)
            dk_ref[:, lanes] = dk_sc[hh]
            dv_ref[:, lanes] = dv_sc[hh]
        pl.when((g == groups - 1) & (j == nq - 1))(s_finish)

    seq_spec = pl.BlockSpec((s, width), lambda g, j: (0, g))
    kv_spec = pl.BlockSpec((tq, width), lambda g, j: (j, g))
    col_spec = pl.BlockSpec((hb, s, 1), lambda g, j: (g, 0, 0))
    any_spec = pl.BlockSpec(memory_space=pl.ANY)
    outs = _call(
        body, name="attn_bwd", grid=(groups, nq), sem=("arbitrary", "arbitrary"),
        in_specs=[seq_spec, kv_spec, kv_spec, seq_spec, col_spec, col_spec] + [any_spec] * ns,
        out_specs=[seq_spec, kv_spec, kv_spec] + [any_spec] * ns,
        out_shape=[_sds(qp.shape, F32), _sds(qp.shape, F32), _sds(qp.shape, F32)]
        + [_scatter_out_shape(a, slot) for a, slot in scattered],
        scratch=[pltpu.VMEM((hb, tq, HEAD_PAD), F32), pltpu.VMEM((hb, tq, HEAD_PAD), F32)] + _comm_sems(ns),
    )(qp, kp, vp, d_o, lse, delta, *[a for a, _ in scattered])
    return outs[0], outs[1], outs[2], outs[3:]


def _gmlp_bwd(z, d_sgu, ln_g, ln_b, w_sp, bias_exp, tm):
    s = z.shape[0]
    nblk = tm // GM_CHUNK

    def body(zu_ref, zv_ref, dsgu_ref, lg_ref, lb_ref, w_ref, be_ref,
             dguv_ref, dws_ref, dbs_ref, dlg_ref, dlb_ref, dbe_sc, dvln_sc):
        i = pl.program_id(0)

        @pl.when(i == 0)
        def _():
            for r in (dws_ref, dlg_ref, dlb_ref, dbe_sc):
                r[...] = jnp.zeros(r.shape, F32)

        seg = _seg_matrix()
        mask = _spatial_mask()
        wm = [(w_ref[h] * mask).astype(MXU) for h in range(HEADS)]
        zu, zv = zu_ref[...], zv_ref[...]
        gu = _gelu(zu)
        _, vhat, rstd = _gm_norm(zv, seg)
        lg = lg_ref[...]
        vln = (vhat * lg + lb_ref[...]).astype(MXU)
        d_sgu = dsgu_ref[...]
        for n in range(nblk):
            rows = slice(n * GM_CHUNK, (n + 1) * GM_CHUNK)
            vb = vln[rows]
            mixed = _gm_mix(wm, vb, GM_CHUNK) + be_ref[...]
            d_mixed = d_sgu[rows] * gu[rows]
            dguv_ref[rows, pl.ds(0, GM_WIDTH)] = ((d_sgu[rows] * mixed) * _gelu_grad(zu[rows])).astype(MXU)
            dbe_sc[...] += d_mixed
            dmb = d_mixed.astype(MXU)
            d_vln = jnp.zeros((GM_CHUNK, GM_WIDTH), F32)
            for h in range(HEADS):
                hm = _head_lane_mask(h, GM_CHUNK)
                dws_ref[h] += _dot_nt(jnp.where(hm, dmb, jnp.zeros_like(dmb)), vb)
                d_vln = d_vln + jnp.where(hm, _dot_tn(wm[h], dmb), 0.0)
            dvln_sc[rows, :] = d_vln
        d_vln = dvln_sc[...]
        dlg_ref[...] += jnp.sum(d_vln * vhat, axis=0, keepdims=True)
        dlb_ref[...] += jnp.sum(d_vln, axis=0, keepdims=True)
        d_vhat = d_vln * lg
        d_gv = rstd * ((d_vhat - _split_dot(d_vhat, seg)) - vhat * _split_dot(d_vhat * vhat, seg))
        dguv_ref[:, pl.ds(GM_WIDTH, GM_WIDTH)] = (d_gv * _gelu_grad(zv)).astype(MXU)

        @pl.when(i == pl.num_programs(0) - 1)
        def _():
            for h in range(HEADS):
                dws_ref[h] = dws_ref[h] * mask
            hrow = lax.broadcasted_iota(jnp.int32, (HEADS, GM_WIDTH), 0)
            hlane = lax.broadcasted_iota(jnp.int32, (HEADS, GM_WIDTH), 1) >> 6
            ind = jnp.where(hrow == hlane, 1.0, 0.0).astype(MXU)
            acc = dbe_sc[...]
            hi = acc.astype(MXU)
            lo = (acc - hi.astype(F32)).astype(MXU)
            dbs_ref[...] = _dot_nt(ind, hi) + _dot_nt(ind, lo)

    return _call(
        body, name="gmlp_bwd", grid=(s // tm,), sem=("arbitrary",),
        in_specs=[_rows(tm, GM_WIDTH, 1), _rows(tm, GM_WIDTH, 2), _rows(tm, GM_WIDTH), _full(ln_g.shape),
                  _full(ln_b.shape), _full(w_sp.shape), _full(bias_exp.shape)],
        out_specs=[_rows(tm, 2 * GM_WIDTH), _full(w_sp.shape), _full((HEADS, GM_CHUNK)), _full(ln_g.shape),
                   _full(ln_b.shape)],
        out_shape=[_sds((s, 2 * GM_WIDTH), MXU), _sds(w_sp.shape, F32), _sds((HEADS, GM_CHUNK), F32),
                   _sds(ln_g.shape, F32), _sds(ln_b.shape, F32)],
        scratch=[pltpu.VMEM((GM_CHUNK, GM_WIDTH), F32), pltpu.VMEM((tm, GM_WIDTH), F32)],
    )(z, z, d_sgu, ln_g, ln_b, w_sp, bias_exp)


def _mix_in_bwd(dq, dk, dv, z, d_guv, x, d_x_part, ada_raw, ada_b, g_pre, g_q, g_kv, w1a, w1b, wq, wkv,
                cos_t, sin_t, tm):
    s = x.shape[0]
    hp = HEADS * HEAD_PAD
    za = Q_LORA + KV_LORA + HEAD_PAD

    def body(dq_ref, dk_ref, dv_ref, z_ref, dguv_ref, x_ref, dxp_ref, ar_ref, ab_ref, g_ref, gq_ref, gkv_ref,
             w1a_ref, w1b_ref, wq_ref, wkv_ref, cos_ref, sin_ref,
             gx_ref, dza_ref, dqp_ref, dkvp_ref, dsh_ref, dsc_ref, dg_ref, dgq_ref, dgkv_ref):
        i = pl.program_id(0)

        @pl.when(i == 0)
        def _():
            for r in (dsh_ref, dsc_ref, dg_ref, dgq_ref, dgkv_ref):
                r[...] = jnp.zeros(r.shape, F32)

        cos, sin = cos_ref[...], sin_ref[...]
        d_krot = jnp.zeros((tm, HEAD_PAD), F32)
        for h in range(HEADS):
            blk = slice(h * HEAD_PAD, (h + 1) * HEAD_PAD)
            dqp_ref[:, blk] = _rope_transposed(dq_ref[:, blk], cos, sin).astype(MXU)
            dk_h = dk_ref[:, blk]
            d_krot = d_krot + dk_h
            dkvp_ref[:, blk] = dk_h.astype(MXU)
        dkvp_ref[:, pl.ds(hp, hp)] = dv_ref[...].astype(MXU)
        lane = lax.broadcasted_iota(jnp.int32, (tm, HEAD_PAD), 1)
        d_kr = jnp.where((lane >= NOPE) & (lane < NOPE + ROPE), _rope_transposed(d_krot, cos, sin), 0.0)
        d_cqn = _dot_nt(dqp_ref[...], wq_ref[...])
        d_ckvn = _dot_nt(dkvp_ref[...], wkv_ref[...])
        zt = z_ref[...]
        gq, gkv = gq_ref[...], gkv_ref[...]
        cq_hat, rq = _rms(zt[:, :Q_LORA])
        ckv_hat, rkv = _rms(zt[:, Q_LORA:Q_LORA + KV_LORA])
        dgq_ref[...] += jnp.sum(d_cqn * cq_hat, axis=0, keepdims=True)
        dgkv_ref[...] += jnp.sum(d_ckvn * ckv_hat, axis=0, keepdims=True)
        d_cq = _rms_bwd(d_cqn * gq, cq_hat, rq)
        d_ckv = _rms_bwd(d_ckvn * gkv, ckv_hat, rkv)
        d_za = jnp.concatenate([d_cq, d_ckv, d_kr], axis=1).astype(MXU)
        dza_ref[...] = d_za
        d_h1 = _dot_nt(d_za, w1a_ref[...]) + _dot_nt(dguv_ref[...], w1b_ref[...])
        sc1 = _row(ar_ref, 1) + _row(ab_ref, 1)
        g = g_ref[...]
        xn, r1 = _rms(x_ref[...])
        dsh_ref[...] += jnp.sum(d_h1, axis=0, keepdims=True)
        dsc_ref[...] += jnp.sum(d_h1 * (xn * g), axis=0, keepdims=True)
        d_mod = d_h1 * (1.0 + sc1)
        dg_ref[...] += jnp.sum(d_mod * xn, axis=0, keepdims=True)
        gx_ref[...] = dxp_ref[...] + _rms_bwd(d_mod * g, xn, r1)

    vec = pl.BlockSpec((1, D_MODEL), lambda i: (0, 0))
    return _call(
        body, name="mix_in_bwd", grid=(s // tm,), sem=("arbitrary",),
        in_specs=[_rows(tm, hp), _rows(tm, hp), _rows(tm, hp), _rows(tm, za), _rows(tm, 2 * GM_WIDTH),
                  _rows(tm, D_MODEL), _rows(tm, D_MODEL), _full(ada_raw.shape), _full(ada_b.shape), _full(g_pre.shape),
                  _full(g_q.shape), _full(g_kv.shape), _full(w1a.shape), _full(w1b.shape), _full(wq.shape),
                  _full(wkv.shape), _rows(tm, HEAD_PAD), _rows(tm, HEAD_PAD)],
        out_specs=[_rows(tm, D_MODEL), _rows(tm, za), _rows(tm, hp), _rows(tm, 2 * hp), vec, vec, vec,
                   _full(g_q.shape), _full(g_kv.shape)],
        out_shape=[_sds((s, D_MODEL), F32), _sds((s, za), MXU), _sds((s, hp), MXU), _sds((s, 2 * hp), MXU),
                   _sds((1, D_MODEL), F32), _sds((1, D_MODEL), F32), _sds((1, D_MODEL), F32),
                   _sds(g_q.shape, F32), _sds(g_kv.shape, F32)],
    )(dq, dk, dv, z, d_guv, x, d_x_part, ada_raw, ada_b, g_pre, g_q, g_kv, w1a, w1b, wq, wkv, cos_t, sin_t)


def _tn_matmul(a, b, name, ts):
    ga, s, m = a.shape
    gb, _, n = b.shape
    g = max(ga, gb)
    tn = n if n <= 1024 else 1024
    steps = s // ts

    def body(a_ref, b_ref, o_ref, acc):
        k = pl.program_id(2)

        @pl.when(k == 0)
        def _():
            acc[...] = jnp.zeros(acc.shape, F32)

        acc[...] += _dot_tn(a_ref[0], b_ref[0])

        @pl.when(k == steps - 1)
        def _():
            o_ref[0] = acc[...].astype(MXU)

    return _call(
        body, name=name, grid=(g, n // tn, steps), sem=("parallel", "parallel", "arbitrary"),
        in_specs=[pl.BlockSpec((1, ts, m), lambda gi, ni, k: (gi if ga > 1 else 0, k, 0)),
                  pl.BlockSpec((1, ts, tn), lambda gi, ni, k: (gi if gb > 1 else 0, k, ni))],
        out_specs=pl.BlockSpec((1, m, tn), lambda gi, ni, k: (gi, 0, ni)),
        out_shape=_sds((g, m, n), MXU),
        scratch=[pltpu.VMEM((m, tn), F32)],
    )(a, b)


def _adamw(w, g, m, v):
    m2 = ADAM_B1 * m + (1.0 - ADAM_B1) * g
    v2 = ADAM_B2 * v + (1.0 - ADAM_B2) * (g * g)
    m_hat = m2 / (1.0 - ADAM_B1 ** ADAM_STEP)
    v_hat = v2 / (1.0 - ADAM_B2 ** ADAM_STEP)
    delta = -ADAM_LR * (m_hat / (jnp.sqrt(v_hat) + ADAM_EPS) + ADAM_WD * w)
    return delta, m2, v2


def _adam_reduce(recv, w, m, v, name):
    r, c = w.shape
    tr = r if r <= 512 else 256

    def body(p_ref, w_ref, m_ref, v_ref, g_ref, d_ref, mo_ref, vo_ref):
        g = p_ref[0].astype(F32)
        for j in range(1, N_DEV):
            g = g + p_ref[j].astype(F32)
        g_ref[...] = g
        d_ref[...], mo_ref[...], vo_ref[...] = _adamw(w_ref[...], g, m_ref[...], v_ref[...])

    blk = pl.BlockSpec((tr, c), lambda i: (i, 0))
    return _call(
        body, name=name, grid=(r // tr,), sem=("parallel",),
        in_specs=[pl.BlockSpec((N_DEV, tr, c), lambda i: (0, i, 0)), blk, blk, blk],
        out_specs=[blk] * 4, out_shape=[_sds((r, c), F32)] * 4,
    )(recv, w, m, v)


def _adam_direct(g, w, m, v, name):
    def body(g_ref, w_ref, m_ref, v_ref, d_ref, mo_ref, vo_ref):
        d_ref[...], mo_ref[...], vo_ref[...] = _adamw(w_ref[...], g_ref[...], m_ref[...], v_ref[...])

    return _call(body, name=name, grid=(1,), in_specs=[_full(w.shape)] * 4, out_specs=[_full(w.shape)] * 3,
                 out_shape=[_sds(w.shape, F32)] * 3)(g, w, m, v)


def _adam_w_ada(c_act_t, d_ada_cols, w, m, v):
    r, c = w.shape
    tr = 256

    def body(ct_ref, da_ref, w_ref, m_ref, v_ref, g_ref, d_ref, mo_ref, vo_ref):
        g = ct_ref[:, pl.ds(0, 1)] * da_ref[pl.ds(0, 1), :]
        for b in range(1, N_DEV):
            g = g + ct_ref[:, pl.ds(b, 1)] * da_ref[pl.ds(b, 1), :]
        g_ref[...] = g
        d_ref[...], mo_ref[...], vo_ref[...] = _adamw(w_ref[...], g, m_ref[...], v_ref[...])

    blk = pl.BlockSpec((tr, c), lambda i: (i, 0))
    return _call(
        body, name="adam_w_ada", grid=(r // tr,), sem=("parallel",),
        in_specs=[pl.BlockSpec((tr, N_DEV), lambda i: (i, 0)), _full(d_ada_cols.shape), blk, blk, blk],
        out_specs=[blk] * 4, out_shape=[_sds((r, c), F32)] * 4,
    )(c_act_t, d_ada_cols, w, m, v)


def _adam_small(gathered, w, m, v):
    rows, lanes = w.shape

    def body(p_ref, w_ref, m_ref, v_ref, g_ref, d_ref, mo_ref, vo_ref):
        g = p_ref[0]
        for j in range(1, N_DEV):
            g = g + p_ref[j]
        g_ref[...] = g
        d_ref[...], mo_ref[...], vo_ref[...] = _adamw(w_ref[...], g, m_ref[...], v_ref[...])

    return _call(body, name="adam_small", grid=(1,),
                 in_specs=[_full(gathered.shape)] + [_full(w.shape)] * 3, out_specs=[_full(w.shape)] * 4,
                 out_shape=[_sds(w.shape, F32)] * 4)(gathered, w, m, v)


def _rope_tables(s):
    pos = jnp.arange(s, dtype=F32)
    inv = ROPE_THETA ** (-jnp.arange(0, ROPE, 2, dtype=F32) / ROPE)
    ang = pos[:, None] * inv[None, :]
    cos, sin = jnp.cos(ang), jnp.sin(ang)
    ones, zeros = jnp.ones((s, NOPE), F32), jnp.zeros((s, NOPE), F32)
    cos_t = jnp.concatenate([ones, cos, cos, ones[:, :HEAD_PAD - NOPE - ROPE]], axis=1)
    sin_t = jnp.concatenate([zeros, sin, sin, zeros[:, :HEAD_PAD - NOPE - ROPE]], axis=1)
    return cos_t, sin_t


def _pack(parts, rows):
    flat = jnp.concatenate([p.reshape(-1) for p in parts])
    return jnp.pad(flat, (0, rows * 128 - flat.shape[0])).reshape(rows, 128)


def kernel(x, c, w_ada, b_ada, g_pre_mix, g_post_mix, w_in, g_q, w_uq, g_kv, w_ukv, gm_ln_g, gm_ln_b, w_spatial, b_spatial, w_out, g_pre_ffn, g_post_ffn, w_up, conv_w, conv_b, w_down, loss_target, m_w_ada, m_b_ada, m_g_pre_mix, m_g_post_mix, m_w_in, m_g_q, m_w_uq, m_g_kv, m_w_ukv, m_gm_ln_g, m_gm_ln_b, m_w_spatial, m_b_spatial, m_w_out, m_g_pre_ffn, m_g_post_ffn, m_w_up, m_conv_w, m_conv_b, m_w_down, v_w_ada, v_b_ada, v_g_pre_mix, v_g_post_mix, v_w_in, v_g_q, v_w_uq, v_g_kv, v_w_ukv, v_gm_ln_g, v_gm_ln_b, v_w_spatial, v_b_spatial, v_w_out, v_g_pre_ffn, v_g_post_ffn, v_w_up, v_conv_w, v_conv_b, v_w_down):
    s = x.shape[1]
    tm = min(256, s)
    tf = min(2 * ROW_SUB, s)
    tq = min(512, s)
    ts = min(2048, s)
    hp = HEADS * HEAD_PAD
    half = N_DEV // 2
    my_slot = 4 * lax.axis_index("x") + 2 * lax.axis_index("y") + lax.axis_index("c")
    x2d, target = x[0], loss_target[0]

    (g_c, g_in, g_uq, g_ukv, g_cw), _ = _exchange(
        [c, w_in[0].astype(MXU), w_uq[0].astype(MXU), w_ukv[0].astype(MXU), conv_w[0]], [], "gather_mixer_weights")

    w_in_f = jnp.transpose(g_in, (1, 0, 2)).reshape(D_MODEL, -1)
    o1, o2, o3 = Q_LORA, Q_LORA + KV_LORA, Q_LORA + KV_LORA + ROPE
    w1 = jnp.concatenate([w_in_f[:, :o2], jnp.zeros((D_MODEL, NOPE), MXU), w_in_f[:, o2:o3],
                          jnp.zeros((D_MODEL, HEAD_PAD - NOPE - ROPE), MXU), w_in_f[:, o3:]], axis=1)
    w_uq_f = jnp.transpose(g_uq, (1, 0, 2)).reshape(Q_LORA, HEADS, NOPE + ROPE)
    wq = jnp.pad(w_uq_f, ((0, 0), (0, 0), (0, HEAD_PAD - NOPE - ROPE))).reshape(Q_LORA, hp)
    w_ukv_f = jnp.transpose(g_ukv, (1, 0, 2)).reshape(KV_LORA, HEADS, 2 * NOPE)
    pad_head = ((0, 0), (0, 0), (0, HEAD_PAD - NOPE))
    wkv = jnp.concatenate([jnp.pad(w_ukv_f[:, :, :NOPE], pad_head).reshape(KV_LORA, hp),
                           jnp.pad(w_ukv_f[:, :, NOPE:], pad_head).reshape(KV_LORA, hp)], axis=1)
    cb8 = conv_b.reshape(N_DEV, 1, FF_BLK)
    bias_exp = jnp.repeat(b_spatial[0].T, GM_DIM, axis=1)
    ln_g, ln_b = gm_ln_g.reshape(1, GM_WIDTH), gm_ln_b.reshape(1, GM_WIDTH)
    w_sp = w_spatial[0]
    cos_t, sin_t = _rope_tables(s)

    ada_part, c_act = _ada_fwd(g_c.reshape(N_DEV, D_MODEL), w_ada[0])
    _, (ada_recv,) = _exchange([], [(ada_part.reshape(N_DEV, 1, -1), _plain_slot)], "ada_rows")
    ada_raw = ada_recv.reshape(6, D_MODEL)
    ada_b = b_ada.reshape(6, D_MODEL)

    h1, z, qp, kp, vp, cqn, ckvn = _mix_in_fwd(x2d, ada_raw, ada_b, g_pre_mix, w1, g_q, g_kv, wq, wkv, cos_t, sin_t, tm)
    sgu = _gmlp_fwd(z, ln_g, ln_b, w_sp, bias_exp, tm)
    o_pad, lse, (g_out, g_up, g_down) = _attn_fwd(
        qp, kp, vp, tq, [w_out[0].astype(MXU), w_up[0].astype(MXU), w_down[0].astype(MXU)])
    w_out_f = g_out.reshape(2 * GM_WIDTH, D_MODEL)
    wo_attn = jnp.pad(w_out_f[:GM_WIDTH].reshape(HEADS, NOPE, D_MODEL), ((0, 0), (0, HEAD_PAD - NOPE), (0, 0)))
    wo = jnp.concatenate([wo_attn.reshape(hp, D_MODEL), w_out_f[GM_WIDTH:]], axis=0)
    wd = g_down.reshape(half, FF_BLK, D_MODEL)
    m_mix, x2, h2 = _out_proj_fwd(o_pad, sgu, wo, x2d, ada_raw, ada_b, g_post_mix, g_pre_ffn, tm)
    up_a, up_b, y_a, y_b, act = _ffn_up_fwd(h2, g_up, g_cw, cb8, tf)
    d_out, d_f, loss_part, d_gt2, d_g_post_ffn = _ffn_down_fwd(act, wd, x2, target, ada_raw, ada_b, g_post_ffn, tf)
    loss = lax.psum(loss_part[0, 0], ("x", "y", "c"))

    d_up, dcw_a, dcw_b, dcb_a, dcb_b = _ffn_down_bwd(d_f, wd, up_a, up_b, y_a, y_b, g_cw, tf)
    d_x2, d_m, d_sh2, d_sc2, d_g_pre_ffn, d_gt1, d_g_post_mix = _ffn_up_bwd(
        d_up, g_up, x2, m_mix, d_out, ada_raw, ada_b, g_pre_ffn, g_post_mix, tm)
    p_down = _tn_matmul(act, d_f[None], "dw_down", ts).reshape(N_DEV, -1, D_MODEL)
    p_up = _tn_matmul(h2[None], d_up.reshape(N_DEV, s, FF_BLK), "dw_up", ts).reshape(half, 2, D_MODEL, FF_BLK)
    d_m3 = d_m[None]
    dwo_attn = _tn_matmul(o_pad[None], d_m3, "dw_out_attn", ts)[0].reshape(HEADS, HEAD_PAD, D_MODEL)[:, :NOPE]
    dwo_sgu = _tn_matmul(sgu[None], d_m3, "dw_out_sgu", ts)[0]
    p_out = jnp.concatenate([dwo_attn.reshape(GM_WIDTH, D_MODEL), dwo_sgu], axis=0).reshape(N_DEV, -1, D_MODEL)
    d_o, d_sgu, delta = _out_proj_bwd(d_m, wo, o_pad, tm)
    dq, dk, dv, (r_out, r_up, r_down) = _attn_bwd(
        qp, kp, vp, d_o, lse, delta, tq,
        [(p_out, _plain_slot), (p_up, lambda j: (j % half, j // half)), (p_down, _plain_slot)])
    d_guv, d_ws, d_bs, d_ln_g, d_ln_b = _gmlp_bwd(z, d_sgu, ln_g, ln_b, w_sp, bias_exp, tm)
    za = Q_LORA + KV_LORA + HEAD_PAD
    grad_x, d_za, d_qp, d_kvp, d_sh1, d_sc1, d_g_pre_mix, d_g_q, d_g_kv = _mix_in_bwd(
        dq, dk, dv, z, d_guv, x2d, d_x2, ada_raw, ada_b, g_pre_mix, g_q, g_kv, w1[:, :za], w1[:, za:], wq, wkv,
        cos_t, sin_t, tm)
    h1_3 = h1[None]
    dw1a = _tn_matmul(h1_3, d_za[None], "dw_in_a", ts)[0]
    dw1b = _tn_matmul(h1_3, d_guv[None], "dw_in_b", ts)[0]
    d_w_in = jnp.concatenate([dw1a[:, :o2], dw1a[:, o2 + NOPE:o2 + NOPE + ROPE], dw1b], axis=1)
    p_in = jnp.transpose(d_w_in.reshape(D_MODEL, N_DEV, -1), (1, 0, 2))
    dwq = _tn_matmul(cqn[None], d_qp[None], "dw_uq", ts)[0].reshape(Q_LORA, HEADS, HEAD_PAD)[:, :, :NOPE + ROPE]
    p_uq = jnp.transpose(dwq.reshape(Q_LORA, N_DEV, -1), (1, 0, 2))
    dwkv = _tn_matmul(ckvn[None], d_kvp[None], "dw_ukv", ts)[0]
    dwk = dwkv[:, :hp].reshape(KV_LORA, HEADS, HEAD_PAD)[:, :, :NOPE]
    dwv = dwkv[:, hp:].reshape(KV_LORA, HEADS, HEAD_PAD)[:, :, :NOPE]
    p_ukv = jnp.transpose(jnp.concatenate([dwk, dwv], axis=2), (1, 0, 2))

    d_ada = jnp.concatenate([d_sh1, d_sc1, d_gt1, d_sh2, d_sc2, d_gt2], axis=1)
    d_cw = jnp.concatenate([dcw_a, dcw_b], axis=0)
    d_cb = jnp.concatenate([dcb_a, dcb_b], axis=0)
    small_g = [d_ada, d_g_pre_mix, d_g_post_mix, d_g_pre_ffn, d_g_post_ffn, d_g_q, d_g_kv, d_ln_g, d_ln_b, d_ws, d_bs,
               d_cb, d_cw]
    sizes = [int(p.size) for p in small_g]
    rows = -(-sum(sizes) // (8 * 128)) * 8
    zero_cw = jnp.zeros_like(d_cw)
    small_w = [b_ada, g_pre_mix, g_post_mix, g_pre_ffn, g_post_ffn, g_q, g_kv, gm_ln_g, gm_ln_b, w_spatial, b_spatial,
               conv_b, zero_cw]
    small_m = [m_b_ada, m_g_pre_mix, m_g_post_mix, m_g_pre_ffn, m_g_post_ffn, m_g_q, m_g_kv, m_gm_ln_g, m_gm_ln_b,
               m_w_spatial, m_b_spatial, m_conv_b, zero_cw]
    small_v = [v_b_ada, v_g_pre_mix, v_g_post_mix, v_g_pre_ffn, v_g_post_ffn, v_g_q, v_g_kv, v_gm_ln_g, v_gm_ln_b,
               v_w_spatial, v_b_spatial, v_conv_b, zero_cw]
    (g_small,), (r_in, r_uq, r_ukv) = _exchange(
        [_pack(small_g, rows)], [(p_in, _plain_slot), (p_uq, _plain_slot), (p_ukv, _plain_slot)], "final_exchange")
    sm_g, sm_d, sm_m, sm_v = _adam_small(g_small, _pack(small_w, rows), _pack(small_m, rows), _pack(small_v, rows))

    def unpack(packed, shapes):
        flat = packed.reshape(-1)
        out, off = [], 0
        for size, shape in zip(sizes, shapes):
            out.append(flat[off:off + size].reshape(shape))
            off += size
        return out

    small_shapes = [p.shape for p in small_w]
    sg, sd, smm, svv = (unpack(p, small_shapes) for p in (sm_g, sm_d, sm_m, sm_v))
    (g_b_ada, g_g_pre_mix, g_g_post_mix, g_g_pre_ffn, g_g_post_ffn, g_g_q, g_g_kv, g_ln_g, g_ln_b, g_w_sp, g_b_sp,
     g_conv_b, g_cw_all) = sg

    def big(recv, w, m, v, name):
        g, d, m2, v2 = _adam_reduce(recv, w[0], m[0], v[0], name)
        return g[None], d[None], m2[None], v2[None]

    a_in = big(r_in, w_in, m_w_in, v_w_in, "adam_w_in")
    a_uq = big(r_uq, w_uq, m_w_uq, v_w_uq, "adam_w_uq")
    a_ukv = big(r_ukv, w_ukv, m_w_ukv, v_w_ukv, "adam_w_ukv")
    a_out = big(r_out, w_out, m_w_out, v_w_out, "adam_w_out")
    a_up = big(r_up, w_up, m_w_up, v_w_up, "adam_w_up")
    a_down = big(r_down, w_down, m_w_down, v_w_down, "adam_w_down")
    ada_cols = w_ada.shape[2]
    d_ada_all = g_small.reshape(N_DEV, -1)[:, :6 * D_MODEL]
    d_ada_cols = lax.dynamic_slice(d_ada_all, (0, my_slot * ada_cols), (N_DEV, ada_cols))
    a_ada = tuple(t[None] for t in _adam_w_ada(c_act.T, d_ada_cols, w_ada[0], m_w_ada[0], v_w_ada[0]))
    g_cw_mine = lax.dynamic_slice(g_cw_all, (my_slot, 0, 0), (1, 3, FF_BLK))
    a_cw = (g_cw_mine,) + tuple(_adam_direct(g_cw_mine, conv_w, m_conv_w, v_conv_w, "adam_conv_w"))

    def small(k):
        return sg[k], sd[k], smm[k], svv[k]

    per_weight = [a_ada, small(0), small(1), small(2), a_in, small(5), a_uq, small(6), a_ukv, small(7), small(8),
                  small(9), small(10), a_out, small(3), small(4), a_up, a_cw, small(11), a_down]
    outs = [loss, grad_x[None]]
    for k in range(4):
        outs += [t[k] for t in per_weight]
    return tuple(outs)
```

```python
import functools

import jax
import jax.numpy as jnp
from jax import lax
from jax.experimental import pallas as pl
from jax.experimental.pallas import tpu as pltpu

F32 = jnp.float32
MXU = jnp.bfloat16

N_DEV = 8
D_MODEL = 1024
HEADS = 8
HEAD_PAD = 128
NOPE = 64
ROPE = 32
Q_LORA = 256
KV_LORA = 128
GM_WIDTH = 512
GM_DIM = 64
GM_CHUNK = 128
CHUNK_SHIFT = 6
ROPE_THETA = 10000.0
ATTN_SCALE = (NOPE + ROPE) ** -0.5
Z_COLS = 1536
FF_BLK = 704
EPS = 1e-6
ADAM_LR = 0.001
ADAM_B1 = 0.9
ADAM_B2 = 0.999
ADAM_EPS = 1e-08
ADAM_WD = 0.01
ADAM_STEP = 10
VMEM_LIMIT = 56 * 1024 * 1024
MESH = pl.DeviceIdType.MESH


def _dot(a, b):
    return jnp.dot(a, b, preferred_element_type=F32)


def _dot_nt(a, b):
    return lax.dot_general(a, b, (((1,), (1,)), ((), ())), preferred_element_type=F32)


def _dot_tn(a, b):
    return lax.dot_general(a, b, (((0,), (0,)), ((), ())), preferred_element_type=F32)


def _call(body, *, name, grid, in_specs, out_specs, out_shape, scratch=(), sem=None):
    params = pltpu.CompilerParams(dimension_semantics=sem, vmem_limit_bytes=VMEM_LIMIT)
    return pl.pallas_call(body, name=name, grid=grid, in_specs=in_specs, out_specs=out_specs,
                          out_shape=out_shape, scratch_shapes=list(scratch), compiler_params=params)


def _full(shape):
    n = len(shape)
    return pl.BlockSpec(shape, lambda *_: (0,) * n)


def _rows(tm, cols, col_block=0):
    return pl.BlockSpec((tm, cols), lambda i: (i, col_block))


def _sds(shape, dtype):
    return jax.ShapeDtypeStruct(shape, dtype)


def _row(ref, k):
    return ref[pl.ds(k, 1), :]


def _rms(x):
    r = lax.rsqrt(jnp.mean(x * x, axis=-1, keepdims=True) + EPS)
    return x * r, r


def _rms_bwd(d_hat, hat, r):
    return r * (d_hat - hat * jnp.mean(d_hat * hat, axis=-1, keepdims=True))


def _rope_partner(t):
    lane = lax.broadcasted_iota(jnp.int32, t.shape, 1)
    swapped = jnp.where(lane < NOPE + ROPE // 2, -pltpu.roll(t, HEAD_PAD - ROPE // 2, 1), pltpu.roll(t, ROPE // 2, 1))
    return jnp.where((lane >= NOPE) & (lane < NOPE + ROPE), swapped, 0.0)


def _rope(t, cos, sin):
    return t * cos + _rope_partner(t) * sin


def _rope_transposed(g, cos, sin):
    return g * cos - _rope_partner(g * sin)


def _gelu(x):
    return x * (0.5 * (1.0 + jnp.tanh(0.7978845608028654 * (x + 0.044715 * (x * x * x)))))


def _gelu_grad(x):
    t = jnp.tanh(0.7978845608028654 * (x + 0.044715 * (x * x * x)))
    return 0.5 * (1.0 + t) + 0.5 * x * (1.0 - t * t) * (0.7978845608028654 * (1.0 + 3.0 * 0.044715 * (x * x)))


def _split_dot(x, mat):
    hi = x.astype(MXU)
    lo = (x - hi.astype(F32)).astype(MXU)
    return _dot(hi, mat) + _dot(lo, mat)


def _split_dot3(x, mat):
    hi = x.astype(MXU)
    r1 = x - hi.astype(F32)
    mid = r1.astype(MXU)
    lo = (r1 - mid.astype(F32)).astype(MXU)
    return (_dot(hi, mat) + _dot(mid, mat)) + _dot(lo, mat)


def _seg_matrix():
    r = lax.broadcasted_iota(jnp.int32, (GM_WIDTH, GM_WIDTH), 0) >> 6
    c = lax.broadcasted_iota(jnp.int32, (GM_WIDTH, GM_WIDTH), 1) >> 6
    return jnp.where(r == c, 1.0 / GM_DIM, 0.0).astype(MXU)


def _spatial_mask():
    i = lax.broadcasted_iota(jnp.int32, (GM_CHUNK, GM_CHUNK), 0) >> CHUNK_SHIFT
    j = lax.broadcasted_iota(jnp.int32, (GM_CHUNK, GM_CHUNK), 1) >> CHUNK_SHIFT
    return (j <= i).astype(F32)


def _head_lane_mask(h, rows):
    lane = lax.broadcasted_iota(jnp.int32, (rows, GM_WIDTH), 1) >> 6
    return lane == h


def _my_place():
    return lax.axis_index("x"), lax.axis_index("y"), lax.axis_index("c")


def _flat(p):
    return 4 * p[0] + 2 * p[1] + p[2]


def _comm_sems(n):
    return [pltpu.SemaphoreType.DMA((7 * n,)), pltpu.SemaphoreType.DMA((7 * n,)), pltpu.SemaphoreType.DMA((n,))]


def _gather_steps(ins, outs, sems):
    send_sems, recv_sems, local_sems = sems
    n = len(ins)
    x, y, c = _my_place()
    me, sibling = (x, y, c), (x, y, 1 - c)
    chips = [(1 - x, y), (x, 1 - y), (1 - x, 1 - y)]

    def copy(a, k, block, to, src=None):
        slot = outs[a].at[_flat(block)]
        return pltpu.make_async_remote_copy(
            src_ref=slot if src is None else src, dst_ref=slot,
            send_sem=send_sems.at[7 * a + k], recv_sem=recv_sems.at[7 * a + k],
            device_id=to, device_id_type=MESH)

    def mine():
        return [pltpu.make_async_copy(ins[a], outs[a].at[_flat(me)], local_sems.at[a]) for a in range(n)]

    def first():
        cps = []
        for a in range(n):
            cps.append(copy(a, 0, me, sibling, src=ins[a]))
            cps += [copy(a, 1 + j, me, (*chip, c), src=ins[a]) for j, chip in enumerate(chips)]
        return cps

    def passed():
        return [copy(a, 4 + j, (*chip, c), sibling) for a in range(n) for j, chip in enumerate(chips)]

    def start():
        for cp in mine() + first():
            cp.start()

    def forward():
        for a in range(n):
            for j, chip in enumerate(chips):
                copy(a, 1 + j, (*chip, c), me).wait_recv()
                copy(a, 4 + j, (*chip, c), sibling).start()

    def finish():
        for a in range(n):
            copy(a, 0, sibling, me).wait_recv()
            for j, chip in enumerate(chips):
                copy(a, 4 + j, (*chip, 1 - c), me).wait_recv()
        for cp in first() + passed():
            cp.wait_send()
        for cp in mine():
            cp.wait()

    return start, forward, finish


def _scatter_steps(ins, outs, sems, slots):
    send_sems, recv_sems, local_sems = sems
    n = len(ins)
    flips = [(fx, fy, fc) for fx in (0, 1) for fy in (0, 1) for fc in (0, 1)][1:]
    me = _my_place()

    def peer(f):
        return tuple(1 - v if b else v for v, b in zip(me, f))

    def copy(a, k, arriving=False):
        p = peer(flips[k])
        return pltpu.make_async_remote_copy(
            src_ref=ins[a].at[slots[a](_flat(p))], dst_ref=outs[a].at[_flat(p if arriving else me)],
            send_sem=send_sems.at[7 * a + k], recv_sem=recv_sems.at[7 * a + k],
            device_id=p, device_id_type=MESH)

    def mine():
        return [pltpu.make_async_copy(ins[a].at[slots[a](_flat(me))], outs[a].at[_flat(me)], local_sems.at[a])
                for a in range(n)]

    def start():
        for cp in mine() + [copy(a, k) for a in range(n) for k in range(7)]:
            cp.start()

    def finish():
        for a in range(n):
            for k in range(7):
                copy(a, k, arriving=True).wait_recv()
        for a in range(n):
            for k in range(7):
                copy(a, k).wait_send()
        for cp in mine():
            cp.wait()

    return start, finish


def _plain_slot(j):
    return (j,)


def _scatter_out_shape(arr, slot):
    return _sds((N_DEV,) + arr.shape[len(slot(0)):], arr.dtype)


def _exchange(gathered, scattered, name):
    ng, ns = len(gathered), len(scattered)
    slots = [slot for _, slot in scattered]

    def body(*refs):
        g_in, s_in = refs[:ng], refs[ng:ng + ns]
        g_out, s_out = refs[ng + ns:2 * ng + ns], refs[2 * ng + ns:2 * (ng + ns)]
        sems = refs[2 * (ng + ns):]
        g_start, g_forward, g_finish = _gather_steps(g_in, g_out, sems[:3])
        s_start, s_finish = _scatter_steps(s_in, s_out, sems[3:], slots)
        g_start()
        s_start()
        g_forward()
        g_finish()
        s_finish()

    any_spec = pl.BlockSpec(memory_space=pl.ANY)
    outs = pl.pallas_call(
        body, name=name,
        in_specs=[any_spec] * (ng + ns), out_specs=[any_spec] * (ng + ns),
        out_shape=[_sds((N_DEV,) + a.shape, a.dtype) for a in gathered]
        + [_scatter_out_shape(a, slot) for a, slot in scattered],
        scratch_shapes=_comm_sems(max(ng, 1)) + _comm_sems(max(ns, 1)),
    )(*gathered, *[a for a, _ in scattered])
    return outs[:ng], outs[ng:]


def _ada_fwd(c_all, w_ada):
    def body(c_ref, w_ref, part_ref, act_ref):
        cv = c_ref[...]
        act = cv * jax.nn.sigmoid(cv)
        act_ref[...] = act
        part_ref[...] = _dot(act.astype(MXU), w_ref[...].astype(MXU))

    cols = w_ada.shape[1]
    return _call(body, name="ada_fwd", grid=(1,),
                 in_specs=[_full(c_all.shape), _full(w_ada.shape)],
                 out_specs=[_full((N_DEV, cols)), _full(c_all.shape)],
                 out_shape=[_sds((N_DEV, cols), F32), _sds(c_all.shape, F32)])(c_all, w_ada)


def _mix_in_fwd(x, ada_raw, ada_b, g_pre, w1, g_q, g_kv, wq, wkv, cos_t, sin_t, tm):
    s = x.shape[0]

    def body(x_ref, ar_ref, ab_ref, g_ref, w1_ref, gq_ref, gkv_ref, wq_ref, wkv_ref, cos_ref, sin_ref,
             h1_ref, z_ref, qp_ref, kp_ref, vp_ref, cqn_ref, ckvn_ref):
        sh = _row(ar_ref, 0) + _row(ab_ref, 0)
        sc = _row(ar_ref, 1) + _row(ab_ref, 1)
        xn, _ = _rms(x_ref[...])
        hb = ((xn * g_ref[...]) * (1.0 + sc) + sh).astype(MXU)
        h1_ref[...] = hb
        z = _dot_nt(hb, w1_ref[...])
        z_ref[...] = z
        cos, sin = cos_ref[...], sin_ref[...]
        cqn = (_rms(z[:, :Q_LORA])[0] * gq_ref[...]).astype(MXU)
        ckvn = (_rms(z[:, Q_LORA:Q_LORA + KV_LORA])[0] * gkv_ref[...]).astype(MXU)
        cqn_ref[...] = cqn
        ckvn_ref[...] = ckvn
        q = _dot_nt(cqn, wq_ref[...])
        kv = _dot(ckvn, wkv_ref[...])
        k_rope = _rope(z[:, Q_LORA + KV_LORA:Q_LORA + KV_LORA + HEAD_PAD], cos, sin)
        for h in range(HEADS):
            blk = slice(h * HEAD_PAD, (h + 1) * HEAD_PAD)
            qp_ref[:, blk] = _rope(q[:, blk], cos, sin).astype(MXU)
            kp_ref[:, blk] = (kv[:, blk] + k_rope).astype(MXU)
        v_lane = lax.broadcasted_iota(jnp.int32, (tm, HEADS * HEAD_PAD), 1) & (HEAD_PAD - 1)
        vp_ref[...] = jnp.where(v_lane == NOPE, 1.0, kv[:, HEADS * HEAD_PAD:]).astype(MXU)

    hp = HEADS * HEAD_PAD
    return _call(
        body, name="mix_in_fwd", grid=(s // tm,), sem=("parallel",),
        in_specs=[_rows(tm, D_MODEL), _full(ada_raw.shape), _full(ada_b.shape), _full(g_pre.shape), _full(w1.shape),
                  _full(g_q.shape), _full(g_kv.shape), _full(wq.shape), _full(wkv.shape),
                  _rows(tm, HEAD_PAD), _rows(tm, HEAD_PAD)],
        out_specs=[_rows(tm, D_MODEL), _rows(tm, Z_COLS), _rows(tm, hp), _rows(tm, hp), _rows(tm, hp),
                   _rows(tm, Q_LORA), _rows(tm, KV_LORA)],
        out_shape=[_sds((s, D_MODEL), MXU), _sds((s, Z_COLS), F32), _sds((s, hp), MXU), _sds((s, hp), MXU),
                   _sds((s, hp), MXU), _sds((s, Q_LORA), MXU), _sds((s, KV_LORA), MXU)],
    )(x, ada_raw, ada_b, g_pre, w1, g_q, g_kv, wq, wkv, cos_t, sin_t)


def _gm_norm(zv, seg):
    gv = _gelu(zv)
    cen = gv - _split_dot(gv, seg)
    rstd = lax.rsqrt(_split_dot(cen * cen, seg) + EPS)
    return gv, cen * rstd, rstd


def _gm_mix(wm, vb, rows):
    out = jnp.zeros((rows, GM_WIDTH), F32)
    for h in range(HEADS):
        out = out + jnp.where(_head_lane_mask(h, rows), _dot(wm[h], vb), 0.0)
    return out


def _gmlp_fwd(z, ln_g, ln_b, w_sp, bias_exp, tm):
    s = z.shape[0]
    nblk = tm // GM_CHUNK

    def body(zu_ref, zv_ref, lg_ref, lb_ref, w_ref, be_ref, sgu_ref):
        seg = _seg_matrix()
        mask = _spatial_mask()
        wm = [(w_ref[h] * mask).astype(MXU) for h in range(HEADS)]
        gu = _gelu(zu_ref[...])
        _, vhat, _ = _gm_norm(zv_ref[...], seg)
        vln = (vhat * lg_ref[...] + lb_ref[...]).astype(MXU)
        for n in range(nblk):
            rows = slice(n * GM_CHUNK, (n + 1) * GM_CHUNK)
            mixed = _gm_mix(wm, vln[rows], GM_CHUNK) + be_ref[...]
            sgu_ref[rows, :] = (gu[rows] * mixed).astype(MXU)

    return _call(
        body, name="gmlp_fwd", grid=(s // tm,), sem=("parallel",),
        in_specs=[_rows(tm, GM_WIDTH, 1), _rows(tm, GM_WIDTH, 2), _full(ln_g.shape), _full(ln_b.shape),
                  _full(w_sp.shape), _full(bias_exp.shape)],
        out_specs=_rows(tm, GM_WIDTH), out_shape=_sds((s, GM_WIDTH), MXU),
    )(z, z, ln_g, ln_b, w_sp, bias_exp)


def _diag_mask(t):
    qc = lax.broadcasted_iota(jnp.int32, (t, t), 0) >> CHUNK_SHIFT
    kc = lax.broadcasted_iota(jnp.int32, (t, t), 1) >> CHUNK_SHIFT
    return kc <= qc


NEG_BIG = -1e30
ATTN_HEADS_PER_STEP = 2


def _attn_fwd(qp, kp, vp, tq, gathered):
    s = qp.shape[0]
    nq = s // tq
    hb = ATTN_HEADS_PER_STEP
    groups = HEADS // hb
    width = hb * HEAD_PAD
    ng = len(gathered)

    def body(q_ref, k_ref, v_ref, *rest):
        g_in, (o_ref, lse_ref), g_out = rest[:ng], rest[ng:ng + 2], rest[ng + 2:2 * ng + 2]
        m_sc, acc_sc = rest[2 * ng + 2:2 * ng + 4]
        g_start, g_forward, g_finish = _gather_steps(g_in, g_out, rest[2 * ng + 4:])
        g, i = pl.program_id(0), pl.program_id(1)
        pl.when((g == 0) & (i == 0))(g_start)
        pl.when((g == groups - 1) & (i == 0))(g_forward)
        m_sc[...] = jnp.full(m_sc.shape, NEG_BIG, F32)
        acc_sc[...] = jnp.zeros(acc_sc.shape, F32)

        def tile(j, masked):
            rows = pl.ds(pl.multiple_of(j * tq, tq), tq)
            for hh in range(hb):
                lanes = slice(hh * HEAD_PAD, (hh + 1) * HEAD_PAD)
                sc = _dot_nt(q_ref[:, lanes], k_ref[rows, lanes]) * ATTN_SCALE
                if masked:
                    sc = jnp.where(_diag_mask(tq), sc, NEG_BIG)
                blocks = [sc[:, b * 128:(b + 1) * 128] for b in range(tq // 128)]
                m_prev = m_sc[hh]
                m_tile = jnp.max(functools.reduce(jnp.maximum, blocks), axis=-1, keepdims=True)
                m_new = jnp.maximum(m_prev, m_tile)
                alpha = jnp.exp(m_prev - m_new)
                p = jnp.concatenate([jnp.exp(b - m_new) for b in blocks], axis=1).astype(MXU)
                acc_sc[hh] = alpha * acc_sc[hh] + _dot(p, v_ref[rows, lanes])
                m_sc[hh] = m_new

        def off_diagonal(j, carry):
            tile(j, False)
            return carry

        lax.fori_loop(0, i, off_diagonal, 0)
        tile(i, True)
        for hh in range(hb):
            lanes = slice(hh * HEAD_PAD, (hh + 1) * HEAD_PAD)
            acc = acc_sc[hh]
            denom = acc[:, NOPE:NOPE + 1]
            o_ref[:, lanes] = (acc / denom).astype(MXU)
            lse_ref[hh] = m_sc[hh][:, :1] + jnp.log(denom)
        pl.when((g == groups - 1) & (i == nq - 1))(g_finish)

    q_spec = pl.BlockSpec((tq, width), lambda g, i: (i, g))
    kv_spec = pl.BlockSpec((s, width), lambda g, i: (0, g))
    any_spec = pl.BlockSpec(memory_space=pl.ANY)
    outs = _call(
        body, name="attn_fwd", grid=(groups, nq), sem=("arbitrary", "arbitrary"),
        in_specs=[q_spec, kv_spec, kv_spec] + [any_spec] * ng,
        out_specs=[q_spec, pl.BlockSpec((hb, tq, 1), lambda g, i: (g, i, 0))] + [any_spec] * ng,
        out_shape=[_sds(qp.shape, MXU), _sds((HEADS, s, 1), F32)]
        + [_sds((N_DEV,) + a.shape, a.dtype) for a in gathered],
        scratch=[pltpu.VMEM((hb, tq, HEAD_PAD), F32), pltpu.VMEM((hb, tq, HEAD_PAD), F32)] + _comm_sems(ng),
    )(qp, kp, vp, *gathered)
    return outs[0], outs[1], outs[2:]


def _out_proj_fwd(o_pad, sgu, wo, x, ada_raw, ada_b, g_post_mix, g_pre_ffn, tm):
    s = x.shape[0]
    hp = HEADS * HEAD_PAD

    def body(o_ref, sgu_ref, wo_ref, x_ref, ar_ref, ab_ref, gpm_ref, gpf_ref, m_ref, x2_ref, h2_ref):
        gt1 = _row(ar_ref, 2) + _row(ab_ref, 2)
        sh2 = _row(ar_ref, 3) + _row(ab_ref, 3)
        sc2 = _row(ar_ref, 4) + _row(ab_ref, 4)
        m = _dot(o_ref[...], wo_ref[pl.ds(0, hp), :]) + _dot(sgu_ref[...], wo_ref[pl.ds(hp, GM_WIDTH), :])
        m_ref[...] = m
        x2 = x_ref[...] + gt1 * (_rms(m)[0] * gpm_ref[...])
        x2_ref[...] = x2
        h2_ref[...] = ((_rms(x2)[0] * gpf_ref[...]) * (1.0 + sc2) + sh2).astype(MXU)

    return _call(
        body, name="out_proj_fwd", grid=(s // tm,), sem=("parallel",),
        in_specs=[_rows(tm, hp), _rows(tm, GM_WIDTH), _full(wo.shape), _rows(tm, D_MODEL), _full(ada_raw.shape),
                  _full(ada_b.shape), _full(g_post_mix.shape), _full(g_pre_ffn.shape)],
        out_specs=[_rows(tm, D_MODEL)] * 3,
        out_shape=[_sds((s, D_MODEL), F32), _sds((s, D_MODEL), F32), _sds((s, D_MODEL), MXU)],
    )(o_pad, sgu, wo, x, ada_raw, ada_b, g_post_mix, g_pre_ffn)


def _conv(u, halo, cw_ref, cb_ref):
    ext = jnp.concatenate([halo, u], axis=0)
    m1, m2 = pltpu.roll(ext, 1, 0)[8:], pltpu.roll(ext, 2, 0)[8:]
    return cb_ref[0] + ((m2 * cw_ref[0, pl.ds(0, 1), :] + m1 * cw_ref[0, pl.ds(1, 1), :]) + u * cw_ref[0, pl.ds(2, 1), :])


ROW_SUB = 256


def _sub_blocks(tm):
    return [slice(r, r + ROW_SUB) for r in range(0, tm, ROW_SUB)]


def _ffn_up_fwd(h2, w_up, conv_w, conv_b, tm):
    s = h2.shape[0]
    half = N_DEV // 2

    def body(h_ref, wa_ref, wb_ref, cwa_ref, cwb_ref, cba_ref, cbb_ref,
             ua_ref, ub_ref, ya_ref, yb_ref, act_ref, halo_a, halo_b):
        i = pl.program_id(1)

        @pl.when(i == 0)
        def _():
            halo_a[...] = jnp.zeros(halo_a.shape, F32)
            halo_b[...] = jnp.zeros(halo_b.shape, F32)

        ha, hb = halo_a[...], halo_b[...]
        for rows in _sub_blocks(tm):
            h = h_ref[rows, :]
            ua = _dot_nt(h, wa_ref[0])
            ub = _dot_nt(h, wb_ref[0])
            ua_ref[0, rows, :] = ua
            ub_ref[0, rows, :] = ub
            ya = _conv(ua, ha, cwa_ref, cba_ref)
            yb = _conv(ub, hb, cwb_ref, cbb_ref)
            ya_ref[0, rows, :] = ya
            yb_ref[0, rows, :] = yb
            ha, hb = ua[ROW_SUB - 8:], ub[ROW_SUB - 8:]
            act_ref[0, rows, :] = ((ya * jax.nn.sigmoid(ya)) * yb).astype(MXU)
        halo_a[...] = ha
        halo_b[...] = hb

    def blk(shape, off):
        return pl.BlockSpec(shape, lambda j, i: (j + off, 0, 0))

    def tok(off=0):
        return pl.BlockSpec((1, tm, FF_BLK), lambda j, i: (j + off, i, 0))

    return _call(
        body, name="ffn_up_fwd", grid=(half, s // tm), sem=("parallel", "arbitrary"),
        in_specs=[pl.BlockSpec((tm, D_MODEL), lambda j, i: (i, 0)),
                  blk((1, FF_BLK, D_MODEL), 0), blk((1, FF_BLK, D_MODEL), half),
                  blk((1, 3, FF_BLK), 0), blk((1, 3, FF_BLK), half), blk((1, 1, FF_BLK), 0), blk((1, 1, FF_BLK), half)],
        out_specs=[tok()] * 5,
        out_shape=[_sds((half, s, FF_BLK), F32)] * 4 + [_sds((half, s, FF_BLK), MXU)],
        scratch=[pltpu.VMEM((8, FF_BLK), F32), pltpu.VMEM((8, FF_BLK), F32)],
    )(h2, w_up, w_up, conv_w, conv_w, conv_b, conv_b)


def _ffn_down_fwd(act, wd, x2, target, ada_raw, ada_b, g_post_ffn, tm):
    s = x2.shape[0]
    half = N_DEV // 2

    def body(act_ref, wd_ref, x2_ref, t_ref, ar_ref, ab_ref, g_ref, dout_ref, df_ref, loss_ref, dgt_ref, dg_ref):
        i = pl.program_id(0)

        @pl.when(i == 0)
        def _():
            loss_ref[...] = jnp.zeros(loss_ref.shape, F32)
            dgt_ref[...] = jnp.zeros(dgt_ref.shape, F32)
            dg_ref[...] = jnp.zeros(dg_ref.shape, F32)

        gt2 = _row(ar_ref, 5) + _row(ab_ref, 5)
        g = g_ref[...]
        for rows in _sub_blocks(tm):
            f = _dot(act_ref[0, rows, :], wd_ref[0])
            for j in range(1, half):
                f = f + _dot(act_ref[j, rows, :], wd_ref[j])
            fhat, rf = _rms(f)
            fn = fhat * g
            err = (x2_ref[rows, :] + gt2 * fn) - t_ref[rows, :]
            loss_ref[...] += 0.5 * jnp.sum(jnp.mean(err * err, axis=-1, keepdims=True))
            d_out = err * (1.0 / D_MODEL)
            dout_ref[rows, :] = d_out
            dgt_ref[...] += jnp.sum(d_out * fn, axis=0, keepdims=True)
            d_fn = d_out * gt2
            dg_ref[...] += jnp.sum(d_fn * fhat, axis=0, keepdims=True)
            df_ref[rows, :] = _rms_bwd(d_fn * g, fhat, rf).astype(MXU)

    vec = pl.BlockSpec((1, D_MODEL), lambda i: (0, 0))
    return _call(
        body, name="ffn_down_fwd", grid=(s // tm,), sem=("arbitrary",),
        in_specs=[pl.BlockSpec((half, tm, FF_BLK), lambda i: (0, i, 0)), _full(wd.shape), _rows(tm, D_MODEL),
                  _rows(tm, D_MODEL), _full(ada_raw.shape), _full(ada_b.shape), _full(g_post_ffn.shape)],
        out_specs=[_rows(tm, D_MODEL), _rows(tm, D_MODEL), pl.BlockSpec((1, 128), lambda i: (0, 0)), vec, vec],
        out_shape=[_sds((s, D_MODEL), F32), _sds((s, D_MODEL), MXU), _sds((1, 128), F32),
                   _sds((1, D_MODEL), F32), _sds((1, D_MODEL), F32)],
    )(act, wd, x2, target, ada_raw, ada_b, g_post_ffn)


def _ffn_down_bwd(d_f, wd, up_a, up_b, y_a, y_b, conv_w, tm):
    s = d_f.shape[0]
    half = N_DEV // 2
    nt = s // tm

    def body(df_ref, wd_ref, ua_ref, ub_ref, ya_ref, yb_ref, cwa_ref, cwb_ref,
             dup_ref, dcw_ref, dcb_ref, next_a, next_b):
        i = pl.program_id(1)

        @pl.when(i == 0)
        def _():
            next_a[...] = jnp.zeros(next_a.shape, F32)
            next_b[...] = jnp.zeros(next_b.shape, F32)
            dcw_ref[...] = jnp.zeros(dcw_ref.shape, F32)
            dcb_ref[...] = jnp.zeros(dcb_ref.shape, F32)

        def conv_bwd(d_y, u, nxt, cw_ref, part, rows):
            ext = jnp.concatenate([d_y, nxt], axis=0)
            p1 = pltpu.roll(ext, ROW_SUB + 7, 0)[:ROW_SUB]
            p2 = pltpu.roll(ext, ROW_SUB + 6, 0)[:ROW_SUB]
            d_u = (d_y * cw_ref[0, pl.ds(2, 1), :] + p1 * cw_ref[0, pl.ds(1, 1), :]) + p2 * cw_ref[0, pl.ds(0, 1), :]
            dup_ref[0, part, rows, :] = d_u.astype(MXU)
            dcb_ref[0, part] += jnp.sum(d_y, axis=0, keepdims=True)
            dcw_ref[0, part, pl.ds(0, 1), :] += jnp.sum(p2 * u, axis=0, keepdims=True)
            dcw_ref[0, part, pl.ds(1, 1), :] += jnp.sum(p1 * u, axis=0, keepdims=True)
            dcw_ref[0, part, pl.ds(2, 1), :] += jnp.sum(d_y * u, axis=0, keepdims=True)
            return d_y[:8]

        nxa, nxb = next_a[...], next_b[...]
        for rows in reversed(_sub_blocks(tm)):
            d_act = _dot_nt(df_ref[rows, :], wd_ref[0])
            ya, yb = ya_ref[0, rows, :], yb_ref[0, rows, :]
            sig = jax.nn.sigmoid(ya)
            d_ya = d_act * yb * (sig * (1.0 + ya * (1.0 - sig)))
            d_yb = d_act * (ya * sig)
            nxa = conv_bwd(d_ya, ua_ref[0, rows, :], nxa, cwa_ref, 0, rows)
            nxb = conv_bwd(d_yb, ub_ref[0, rows, :], nxb, cwb_ref, 1, rows)
        next_a[...] = nxa
        next_b[...] = nxb

    def rev(i):
        return nt - 1 - i

    def blk(shape, off):
        return pl.BlockSpec(shape, lambda j, i: (j + off, 0, 0))

    tok = pl.BlockSpec((1, tm, FF_BLK), lambda j, i: (j, rev(i), 0))
    acc3 = pl.BlockSpec((1, 2, 3, FF_BLK), lambda j, i: (j, 0, 0, 0))
    acc1 = pl.BlockSpec((1, 2, 1, FF_BLK), lambda j, i: (j, 0, 0, 0))
    return _call(
        body, name="ffn_down_bwd", grid=(half, nt), sem=("parallel", "arbitrary"),
        in_specs=[pl.BlockSpec((tm, D_MODEL), lambda j, i: (rev(i), 0)), blk((1, FF_BLK, D_MODEL), 0),
                  tok, tok, tok, tok, blk((1, 3, FF_BLK), 0), blk((1, 3, FF_BLK), half)],
        out_specs=[pl.BlockSpec((1, 2, tm, FF_BLK), lambda j, i: (j, 0, rev(i), 0)), acc3, acc1],
        out_shape=[_sds((half, 2, s, FF_BLK), MXU), _sds((half, 2, 3, FF_BLK), F32), _sds((half, 2, 1, FF_BLK), F32)],
        scratch=[pltpu.VMEM((8, FF_BLK), F32), pltpu.VMEM((8, FF_BLK), F32)],
    )(d_f, wd, up_a, up_b, y_a, y_b, conv_w, conv_w)


def _ffn_up_bwd(d_up, w_up, x2, m, d_out, ada_raw, ada_b, g_pre_ffn, g_post_mix, tm):
    s = x2.shape[0]
    half = N_DEV // 2

    def body(dup_ref, w_ref, x2_ref, m_ref, dout_ref, ar_ref, ab_ref, gpf_ref, gpm_ref,
             dx_ref, dm_ref, dsh_ref, dsc_ref, dgpf_ref, dgt1_ref, dgpm_ref):
        i = pl.program_id(0)

        @pl.when(i == 0)
        def _():
            for r in (dsh_ref, dsc_ref, dgpf_ref, dgt1_ref, dgpm_ref):
                r[...] = jnp.zeros(r.shape, F32)

        gt1 = _row(ar_ref, 2) + _row(ab_ref, 2)
        sc2 = _row(ar_ref, 4) + _row(ab_ref, 4)
        gpf, gpm = gpf_ref[...], gpm_ref[...]
        d_h2 = _dot(dup_ref[0, 0], w_ref[0])
        for j in range(1, half):
            d_h2 = d_h2 + _dot(dup_ref[j, 0], w_ref[j])
        for j in range(half):
            d_h2 = d_h2 + _dot(dup_ref[j, 1], w_ref[half + j])
        x2n, r2 = _rms(x2_ref[...])
        dsh_ref[...] += jnp.sum(d_h2, axis=0, keepdims=True)
        dsc_ref[...] += jnp.sum(d_h2 * (x2n * gpf), axis=0, keepdims=True)
        d_mod = d_h2 * (1.0 + sc2)
        dgpf_ref[...] += jnp.sum(d_mod * x2n, axis=0, keepdims=True)
        d_x2 = dout_ref[...] + _rms_bwd(d_mod * gpf, x2n, r2)
        dx_ref[...] = d_x2
        mhat, rm = _rms(m_ref[...])
        dgt1_ref[...] += jnp.sum(d_x2 * (mhat * gpm), axis=0, keepdims=True)
        d_mn = d_x2 * gt1
        dgpm_ref[...] += jnp.sum(d_mn * mhat, axis=0, keepdims=True)
        dm_ref[...] = _rms_bwd(d_mn * gpm, mhat, rm).astype(MXU)

    vec = pl.BlockSpec((1, D_MODEL), lambda i: (0, 0))
    tok = pl.BlockSpec((half, 2, tm, FF_BLK), lambda i: (0, 0, i, 0))
    return _call(
        body, name="ffn_up_bwd", grid=(s // tm,), sem=("arbitrary",),
        in_specs=[tok, _full(w_up.shape), _rows(tm, D_MODEL), _rows(tm, D_MODEL), _rows(tm, D_MODEL),
                  _full(ada_raw.shape), _full(ada_b.shape), _full(g_pre_ffn.shape), _full(g_post_mix.shape)],
        out_specs=[_rows(tm, D_MODEL), _rows(tm, D_MODEL), vec, vec, vec, vec, vec],
        out_shape=[_sds((s, D_MODEL), F32), _sds((s, D_MODEL), MXU)] + [_sds((1, D_MODEL), F32)] * 5,
    )(d_up, w_up, x2, m, d_out, ada_raw, ada_b, g_pre_ffn, g_post_mix)


def _out_proj_bwd(d_m, wo, o_pad, tm):
    s = d_m.shape[0]
    hp = HEADS * HEAD_PAD

    def body(dm_ref, wo_ref, o_ref, do_ref, dsgu_ref, delta_ref):
        d_cat = _dot_nt(dm_ref[...], wo_ref[...])
        d_o = d_cat[:, :hp]
        do_ref[...] = d_o.astype(MXU)
        dsgu_ref[...] = d_cat[:, hp:]
        prod = d_o * o_ref[...].astype(F32)
        for h in range(HEADS):
            delta_ref[h] = jnp.sum(prod[:, h * HEAD_PAD:(h + 1) * HEAD_PAD], axis=-1, keepdims=True)

    return _call(
        body, name="out_proj_bwd", grid=(s // tm,), sem=("parallel",),
        in_specs=[_rows(tm, D_MODEL), _full(wo.shape), _rows(tm, hp)],
        out_specs=[_rows(tm, hp), _rows(tm, GM_WIDTH), pl.BlockSpec((HEADS, tm, 1), lambda i: (0, i, 0))],
        out_shape=[_sds((s, hp), MXU), _sds((s, GM_WIDTH), F32), _sds((HEADS, s, 1), F32)],
    )(d_m, wo, o_pad)


def _attn_bwd(qp, kp, vp, d_o, lse, delta, tq, scattered):
    s = qp.shape[0]
    nq = s // tq
    hb = ATTN_HEADS_PER_STEP
    groups = HEADS // hb
    width = hb * HEAD_PAD
    ns = len(scattered)
    slots = [slot for _, slot in scattered]

    def body(q_ref, k_ref, v_ref, do_ref, lse_ref, dl_ref, *rest):
        s_in, (dq_ref, dk_ref, dv_ref), s_out = rest[:ns], rest[ns:ns + 3], rest[ns + 3:2 * ns + 3]
        dk_sc, dv_sc = rest[2 * ns + 3:2 * ns + 5]
        s_start, s_finish = _scatter_steps(s_in, s_out, rest[2 * ns + 5:], slots)
        g, j = pl.program_id(0), pl.program_id(1)
        pl.when((g == 0) & (j == 0))(s_start)

        @pl.when(j == 0)
        def _():
            dq_ref[...] = jnp.zeros(dq_ref.shape, F32)

        dk_sc[...] = jnp.zeros(dk_sc.shape, F32)
        dv_sc[...] = jnp.zeros(dv_sc.shape, F32)

        def tile(i, masked):
            rows = pl.ds(pl.multiple_of(i * tq, tq), tq)
            for hh in range(hb):
                lanes = slice(hh * HEAD_PAD, (hh + 1) * HEAD_PAD)
                q, do, k = q_ref[rows, lanes], do_ref[rows, lanes], k_ref[:, lanes]
                sc = _dot_nt(q, k) * ATTN_SCALE
                if masked:
                    sc = jnp.where(_diag_mask(tq), sc, NEG_BIG)
                p = jnp.exp(sc - lse_ref[hh, rows, :])
                dv_sc[hh] += _dot_tn(p.astype(MXU), do)
                dp = _dot_nt(do, v_ref[:, lanes])
                ds = ((p * (dp - dl_ref[hh, rows, :])) * ATTN_SCALE).astype(MXU)
                dk_sc[hh] += _dot_tn(ds, q)
                dq_ref[rows, lanes] += _dot(ds, k)

        def off_diagonal(i, carry):
            tile(i, False)
            return carry

        tile(j, True)
        lax.fori_loop(j + 1, nq, off_diagonal, 0)
        for hh in range(hb):
            lanes = slice(hh * HEAD_PAD, (hh + 1) * HEAD_PAD)
            dk_ref[:, lanes] = dk_sc[hh]
            dv_ref[:, lanes] = dv_sc[hh]
        pl.when((g == groups - 1) & (j == nq - 1))(s_finish)

    seq_spec = pl.BlockSpec((s, width), lambda g, j: (0, g))
    kv_spec = pl.BlockSpec((tq, width), lambda g, j: (j, g))
    col_spec = pl.BlockSpec((hb, s, 1), lambda g, j: (g, 0, 0))
    any_spec = pl.BlockSpec(memory_space=pl.ANY)
    outs = _call(
        body, name="attn_bwd", grid=(groups, nq), sem=("arbitrary", "arbitrary"),
        in_specs=[seq_spec, kv_spec, kv_spec, seq_spec, col_spec, col_spec] + [any_spec] * ns,
        out_specs=[seq_spec, kv_spec, kv_spec] + [any_spec] * ns,
        out_shape=[_sds(qp.shape, F32), _sds(qp.shape, F32), _sds(qp.shape, F32)]
        + [_scatter_out_shape(a, slot) for a, slot in scattered],
        scratch=[pltpu.VMEM((hb, tq, HEAD_PAD), F32), pltpu.VMEM((hb, tq, HEAD_PAD), F32)] + _comm_sems(ns),
    )(qp, kp, vp, d_o, lse, delta, *[a for a, _ in scattered])
    return outs[0], outs[1], outs[2], outs[3:]


def _gmlp_bwd(z, d_sgu, ln_g, ln_b, w_sp, bias_exp, tm):
    s = z.shape[0]
    nblk = tm // GM_CHUNK

    def body(zu_ref, zv_ref, dsgu_ref, lg_ref, lb_ref, w_ref, be_ref,
             dguv_ref, dws_ref, dbs_ref, dlg_ref, dlb_ref, dbe_sc, dvln_sc, dlg_sc, dlb_sc):
        i = pl.program_id(0)

        @pl.when(i == 0)
        def _():
            for r in (dws_ref, dlg_sc, dlb_sc, dbe_sc):
                r[...] = jnp.zeros(r.shape, F32)

        seg = _seg_matrix()
        mask = _spatial_mask()
        wm = [(w_ref[h] * mask).astype(MXU) for h in range(HEADS)]
        zu, zv = zu_ref[...], zv_ref[...]
        gu = _gelu(zu)
        _, vhat, rstd = _gm_norm(zv, seg)
        lg = lg_ref[...]
        vln = (vhat * lg + lb_ref[...]).astype(MXU)
        d_sgu = dsgu_ref[...]
        for n in range(nblk):
            rows = slice(n * GM_CHUNK, (n + 1) * GM_CHUNK)
            vb = vln[rows]
            mixed = _gm_mix(wm, vb, GM_CHUNK) + be_ref[...]
            d_mixed = d_sgu[rows] * gu[rows]
            dguv_ref[rows, pl.ds(0, GM_WIDTH)] = ((d_sgu[rows] * mixed) * _gelu_grad(zu[rows])).astype(MXU)
            dbe_sc[...] += d_mixed
            dmb = d_mixed.astype(MXU)
            d_vln = jnp.zeros((GM_CHUNK, GM_WIDTH), F32)
            for h in range(HEADS):
                hm = _head_lane_mask(h, GM_CHUNK)
                dws_ref[h] += _dot_nt(jnp.where(hm, dmb, jnp.zeros_like(dmb)), vb)
                d_vln = d_vln + jnp.where(hm, _dot_tn(wm[h], dmb), 0.0)
            dvln_sc[rows, :] = d_vln
        d_vln = dvln_sc[...]
        dlg_sc[...] += jnp.sum(d_vln * vhat, axis=0, keepdims=True)
        dlb_sc[...] += jnp.sum(d_vln, axis=0, keepdims=True)
        d_vhat = d_vln * lg
        d_gv = rstd * ((d_vhat - _split_dot(d_vhat, seg)) - vhat * _split_dot(d_vhat * vhat, seg))
        dguv_ref[:, pl.ds(GM_WIDTH, GM_WIDTH)] = (d_gv * _gelu_grad(zv)).astype(MXU)

        @pl.when(i == pl.num_programs(0) - 1)
        def _():
            for h in range(HEADS):
                dws_ref[h] = dws_ref[h] * mask
            hrow = lax.broadcasted_iota(jnp.int32, (HEADS, GM_WIDTH), 0)
            hlane = lax.broadcasted_iota(jnp.int32, (HEADS, GM_WIDTH), 1) >> 6
            ind = jnp.where(hrow == hlane, 1.0, 0.0).astype(MXU)
            acc = dbe_sc[...]
            hi = acc.astype(MXU)
            lo = (acc - hi.astype(F32)).astype(MXU)
            dbs_ref[...] = _dot_nt(ind, hi) + _dot_nt(ind, lo)
            pick = (lax.broadcasted_iota(jnp.int32, (GM_WIDTH, GM_DIM), 0) & (GM_DIM - 1)
                    == lax.broadcasted_iota(jnp.int32, (GM_WIDTH, GM_DIM), 1))
            pick = jnp.where(pick, 1.0, 0.0).astype(MXU)
            for src, dst in ((dlg_sc, dlg_ref), (dlb_sc, dlb_ref)):
                spread = jnp.where(hrow == hlane, jnp.broadcast_to(src[...], (HEADS, GM_WIDTH)), 0.0)
                dst[...] = _split_dot3(spread, pick)

    return _call(
        body, name="gmlp_bwd", grid=(s // tm,), sem=("arbitrary",),
        in_specs=[_rows(tm, GM_WIDTH, 1), _rows(tm, GM_WIDTH, 2), _rows(tm, GM_WIDTH), _full(ln_g.shape),
                  _full(ln_b.shape), _full(w_sp.shape), _full(bias_exp.shape)],
        out_specs=[_rows(tm, 2 * GM_WIDTH), _full(w_sp.shape), _full((HEADS, GM_CHUNK)), _full((HEADS, GM_DIM)),
                   _full((HEADS, GM_DIM))],
        out_shape=[_sds((s, 2 * GM_WIDTH), MXU), _sds(w_sp.shape, F32), _sds((HEADS, GM_CHUNK), F32),
                   _sds((HEADS, GM_DIM), F32), _sds((HEADS, GM_DIM), F32)],
        scratch=[pltpu.VMEM((GM_CHUNK, GM_WIDTH), F32), pltpu.VMEM((tm, GM_WIDTH), F32),
                 pltpu.VMEM((1, GM_WIDTH), F32), pltpu.VMEM((1, GM_WIDTH), F32)],
    )(z, z, d_sgu, ln_g, ln_b, w_sp, bias_exp)


def _mix_in_bwd(dq, dk, dv, z, d_guv, x, d_x_part, ada_raw, ada_b, g_pre, g_q, g_kv, w1t, wqt, wkv,
                cos_t, sin_t, tm):
    s = x.shape[0]
    hp = HEADS * HEAD_PAD
    za = Q_LORA + KV_LORA + HEAD_PAD

    def body(dq_ref, dk_ref, dv_ref, z_ref, dguv_ref, x_ref, dxp_ref, ar_ref, ab_ref, g_ref, gq_ref, gkv_ref,
             w1_ref, wq_ref, wkv_ref, cos_ref, sin_ref,
             gx_ref, dza_ref, dqp_ref, dkvp_ref, dsh_ref, dsc_ref, dg_ref, dgq_ref, dgkv_ref):
        i = pl.program_id(0)

        @pl.when(i == 0)
        def _():
            for r in (dsh_ref, dsc_ref, dg_ref, dgq_ref, dgkv_ref):
                r[...] = jnp.zeros(r.shape, F32)

        cos, sin = cos_ref[...], sin_ref[...]
        d_krot = jnp.zeros((tm, HEAD_PAD), F32)
        for h in range(HEADS):
            blk = slice(h * HEAD_PAD, (h + 1) * HEAD_PAD)
            dqp_ref[:, blk] = _rope_transposed(dq_ref[:, blk], cos, sin).astype(MXU)
            dk_h = dk_ref[:, blk]
            d_krot = d_krot + dk_h
            dkvp_ref[:, blk] = dk_h.astype(MXU)
        dkvp_ref[:, pl.ds(hp, hp)] = dv_ref[...].astype(MXU)
        lane = lax.broadcasted_iota(jnp.int32, (tm, HEAD_PAD), 1)
        d_kr = jnp.where((lane >= NOPE) & (lane < NOPE + ROPE), _rope_transposed(d_krot, cos, sin), 0.0)
        d_cqn = _dot(dqp_ref[...], wq_ref[...])
        d_ckvn = _dot_nt(dkvp_ref[...], wkv_ref[...])
        zt = z_ref[...]
        gq, gkv = gq_ref[...], gkv_ref[...]
        cq_hat, rq = _rms(zt[:, :Q_LORA])
        ckv_hat, rkv = _rms(zt[:, Q_LORA:Q_LORA + KV_LORA])
        dgq_ref[...] += jnp.sum(d_cqn * cq_hat, axis=0, keepdims=True)
        dgkv_ref[...] += jnp.sum(d_ckvn * ckv_hat, axis=0, keepdims=True)
        d_cq = _rms_bwd(d_cqn * gq, cq_hat, rq)
        d_ckv = _rms_bwd(d_ckvn * gkv, ckv_hat, rkv)
        d_za = jnp.concatenate([d_cq, d_ckv, d_kr], axis=1).astype(MXU)
        dza_ref[...] = d_za
        d_h1 = _dot(d_za, w1_ref[pl.ds(0, za), :]) + _dot(dguv_ref[...], w1_ref[pl.ds(za, 2 * GM_WIDTH), :])
        sc1 = _row(ar_ref, 1) + _row(ab_ref, 1)
        g = g_ref[...]
        xn, r1 = _rms(x_ref[...])
        dsh_ref[...] += jnp.sum(d_h1, axis=0, keepdims=True)
        dsc_ref[...] += jnp.sum(d_h1 * (xn * g), axis=0, keepdims=True)
        d_mod = d_h1 * (1.0 + sc1)
        dg_ref[...] += jnp.sum(d_mod * xn, axis=0, keepdims=True)
        gx_ref[...] = dxp_ref[...] + _rms_bwd(d_mod * g, xn, r1)

    vec = pl.BlockSpec((1, D_MODEL), lambda i: (0, 0))
    return _call(
        body, name="mix_in_bwd", grid=(s // tm,), sem=("arbitrary",),
        in_specs=[_rows(tm, hp), _rows(tm, hp), _rows(tm, hp), _rows(tm, za), _rows(tm, 2 * GM_WIDTH),
                  _rows(tm, D_MODEL), _rows(tm, D_MODEL), _full(ada_raw.shape), _full(ada_b.shape), _full(g_pre.shape),
                  _full(g_q.shape), _full(g_kv.shape), _full(w1t.shape), _full(wqt.shape),
                  _full(wkv.shape), _rows(tm, HEAD_PAD), _rows(tm, HEAD_PAD)],
        out_specs=[_rows(tm, D_MODEL), _rows(tm, za), _rows(tm, hp), _rows(tm, 2 * hp), vec, vec, vec,
                   _full(g_q.shape), _full(g_kv.shape)],
        out_shape=[_sds((s, D_MODEL), F32), _sds((s, za), MXU), _sds((s, hp), MXU), _sds((s, 2 * hp), MXU),
                   _sds((1, D_MODEL), F32), _sds((1, D_MODEL), F32), _sds((1, D_MODEL), F32),
                   _sds(g_q.shape, F32), _sds(g_kv.shape, F32)],
    )(dq, dk, dv, z, d_guv, x, d_x_part, ada_raw, ada_b, g_pre, g_q, g_kv, w1t, wqt, wkv, cos_t, sin_t)


def _tn_matmul(a, b, name, ts):
    ga, s, m = a.shape
    gb, _, n = b.shape
    g = max(ga, gb)
    tn = n if n <= 1024 else 1024
    steps = s // ts

    def body(a_ref, b_ref, o_ref, acc):
        k = pl.program_id(2)

        @pl.when(k == 0)
        def _():
            acc[...] = jnp.zeros(acc.shape, F32)

        acc[...] += _dot_tn(a_ref[0], b_ref[0])

        @pl.when(k == steps - 1)
        def _():
            o_ref[0] = acc[...].astype(MXU)

    return _call(
        body, name=name, grid=(g, n // tn, steps), sem=("parallel", "parallel", "arbitrary"),
        in_specs=[pl.BlockSpec((1, ts, m), lambda gi, ni, k: (gi if ga > 1 else 0, k, 0)),
                  pl.BlockSpec((1, ts, tn), lambda gi, ni, k: (gi if gb > 1 else 0, k, ni))],
        out_specs=pl.BlockSpec((1, m, tn), lambda gi, ni, k: (gi, 0, ni)),
        out_shape=_sds((g, m, n), MXU),
        scratch=[pltpu.VMEM((m, tn), F32)],
    )(a, b)


def _adamw(w, g, m, v):
    m2 = ADAM_B1 * m + (1.0 - ADAM_B1) * g
    v2 = ADAM_B2 * v + (1.0 - ADAM_B2) * (g * g)
    m_hat = m2 / (1.0 - ADAM_B1 ** ADAM_STEP)
    v_hat = v2 / (1.0 - ADAM_B2 ** ADAM_STEP)
    delta = -ADAM_LR * (m_hat / (jnp.sqrt(v_hat) + ADAM_EPS) + ADAM_WD * w)
    return delta, m2, v2


def _adam_reduce(recv, w, m, v, name):
    r, c = w.shape
    tr = r if r <= 512 else max(t for t in range(16, 513, 16) if r % t == 0)

    def body(p_ref, w_ref, m_ref, v_ref, g_ref, d_ref, mo_ref, vo_ref):
        g = p_ref[0].astype(F32)
        for j in range(1, N_DEV):
            g = g + p_ref[j].astype(F32)
        g_ref[...] = g
        d_ref[...], mo_ref[...], vo_ref[...] = _adamw(w_ref[...], g, m_ref[...], v_ref[...])

    blk = pl.BlockSpec((tr, c), lambda i: (i, 0))
    return _call(
        body, name=name, grid=(r // tr,), sem=("parallel",),
        in_specs=[pl.BlockSpec((N_DEV, tr, c), lambda i: (0, i, 0)), blk, blk, blk],
        out_specs=[blk] * 4, out_shape=[_sds((r, c), F32)] * 4,
    )(recv, w, m, v)


def _adam_w_ada(c_act_t, d_ada_cols, w, m, v):
    r, c = w.shape
    tr = 256

    def body(ct_ref, da_ref, w_ref, m_ref, v_ref, g_ref, d_ref, mo_ref, vo_ref):
        g = ct_ref[:, pl.ds(0, 1)] * da_ref[pl.ds(0, 1), :]
        for b in range(1, N_DEV):
            g = g + ct_ref[:, pl.ds(b, 1)] * da_ref[pl.ds(b, 1), :]
        g_ref[...] = g
        d_ref[...], mo_ref[...], vo_ref[...] = _adamw(w_ref[...], g, m_ref[...], v_ref[...])

    blk = pl.BlockSpec((tr, c), lambda i: (i, 0))
    return _call(
        body, name="adam_w_ada", grid=(r // tr,), sem=("parallel",),
        in_specs=[pl.BlockSpec((tr, N_DEV), lambda i: (i, 0)), _full(d_ada_cols.shape), blk, blk, blk],
        out_specs=[blk] * 4, out_shape=[_sds((r, c), F32)] * 4,
    )(c_act_t, d_ada_cols, w, m, v)


VEC_ROWS = D_MODEL // 128
PK_ADA = 0
PK_GAIN = PK_ADA + 6 * VEC_ROWS
PK_GQ = PK_GAIN + 4 * VEC_ROWS
PK_GKV = PK_GQ + Q_LORA // 128
PK_LOSS = PK_GKV + KV_LORA // 128
PK_LNG = 88
PK_LNB = PK_LNG + HEADS
PK_BS = PK_LNB + HEADS
PK_CB = PK_BS + HEADS
CB_ROWS = 6
PK_WS = PK_CB + N_DEV * CB_ROWS
PK_ROWS = PK_WS + HEADS * GM_CHUNK
assert PK_LOSS < PK_LNG and PK_ROWS % 8 == 0


def _cb_chunks():
    return [(k, k * 128, min(128, FF_BLK - k * 128)) for k in range(CB_ROWS)]


def _pack_small(vectors, d_g_q, d_g_kv, loss_part, d_ln_g, d_ln_b, d_bs, d_cb, d_ws):
    half = N_DEV // 2

    def body(*refs):
        vec_refs = refs[:10]
        gq_ref, gkv_ref, loss_ref, lng_ref, lnb_ref, bs_ref, cb_ref, ws_ref, out_ref = refs[10:]
        out_ref[pl.ds(0, PK_WS), :] = jnp.zeros((PK_WS, 128), F32)

        def put(row0, ref, width):
            for k in range(width // 128):
                out_ref[pl.ds(row0 + k, 1), :] = ref[:, pl.ds(k * 128, 128)]

        for n, ref in enumerate(vec_refs):
            put(PK_ADA + n * VEC_ROWS, ref, D_MODEL)
        put(PK_GQ, gq_ref, Q_LORA)
        put(PK_GKV, gkv_ref, KV_LORA)
        put(PK_LOSS, loss_ref, 128)
        out_ref[pl.ds(PK_LNG, HEADS), pl.ds(0, GM_DIM)] = lng_ref[...]
        out_ref[pl.ds(PK_LNB, HEADS), pl.ds(0, GM_DIM)] = lnb_ref[...]
        out_ref[pl.ds(PK_BS, HEADS), :] = bs_ref[...]
        for j in range(N_DEV):
            for k, lane, width in _cb_chunks():
                out_ref[pl.ds(PK_CB + j * CB_ROWS + k, 1), pl.ds(0, width)] = cb_ref[j % half, j // half, :, pl.ds(lane, width)]
        for h in range(HEADS):
            out_ref[pl.ds(PK_WS + h * GM_CHUNK, GM_CHUNK), :] = ws_ref[h]

    ins = list(vectors) + [d_g_q, d_g_kv, loss_part, d_ln_g, d_ln_b, d_bs, d_cb, d_ws]
    return _call(body, name="pack_small", grid=(1,), in_specs=[_full(a.shape) for a in ins],
                 out_specs=_full((PK_ROWS, 128)), out_shape=_sds((PK_ROWS, 128), F32))(*ins)


def _adam_small(gathered, params):
    n_par = len(params)

    def body(p_ref, *refs):
        ins = [refs[3 * n:3 * n + 3] for n in range(n_par)]
        outs = [refs[3 * n_par + 4 * n:3 * n_par + 4 * n + 4] for n in range(n_par)]
        loss_ref, dada_ref = refs[7 * n_par:]

        def total(rows, lanes=slice(None)):
            g = p_ref[0, rows, lanes]
            for j in range(1, N_DEV):
                g = g + p_ref[j, rows, lanes]
            return g

        def apply(n, g, idx):
            w_ref, m_ref, v_ref = ins[n]
            d, m2, v2 = _adamw(w_ref[idx], g, m_ref[idx], v_ref[idx])
            for ref, val in zip(outs[n], (g, d, m2, v2)):
                ref[idx] = val

        def vector(n, row0, width):
            for k in range(width // 128):
                apply(n, total(pl.ds(row0 + k, 1)), (slice(None), pl.ds(k * 128, 128)))

        vector(0, PK_ADA, 6 * D_MODEL)
        for n in range(4):
            vector(1 + n, PK_GAIN + n * VEC_ROWS, D_MODEL)
        vector(5, PK_GQ, Q_LORA)
        vector(6, PK_GKV, KV_LORA)
        apply(7, total(pl.ds(PK_LNG, HEADS), pl.ds(0, GM_DIM)), (0,))
        apply(8, total(pl.ds(PK_LNB, HEADS), pl.ds(0, GM_DIM)), (0,))
        for h in range(HEADS):
            apply(9, total(pl.ds(PK_WS + h * GM_CHUNK, GM_CHUNK)), (0, h))
        apply(10, total(pl.ds(PK_BS, HEADS)), (0,))
        for j in range(N_DEV):
            for k, lane, width in _cb_chunks():
                apply(11, total(pl.ds(PK_CB + j * CB_ROWS + k, 1), pl.ds(0, width)), (pl.ds(j, 1), pl.ds(lane, width)))
        loss_ref[...] = total(pl.ds(PK_LOSS, 1))
        dada_ref[...] = p_ref[:, pl.ds(PK_ADA, 6 * VEC_ROWS), :]

    flat = [a for triple in params for a in triple]
    out_shape = [_sds(w.shape, F32) for w, _, _ in params for _ in range(4)]
    out_shape += [_sds((1, 128), F32), _sds((N_DEV, 6 * VEC_ROWS, 128), F32)]
    outs = _call(body, name="adam_small", grid=(1,),
                 in_specs=[_full(gathered.shape)] + [_full(a.shape) for a in flat],
                 out_specs=[_full(o.shape) for o in out_shape], out_shape=out_shape)(gathered, *flat)
    return [tuple(outs[4 * n:4 * n + 4]) for n in range(n_par)], outs[-2], outs[-1]


def _rope_tables(s):
    pos = jnp.arange(s, dtype=F32)
    inv = ROPE_THETA ** (-jnp.arange(0, ROPE, 2, dtype=F32) / ROPE)
    ang = pos[:, None] * inv[None, :]
    cos, sin = jnp.cos(ang), jnp.sin(ang)
    ones, zeros = jnp.ones((s, NOPE), F32), jnp.zeros((s, NOPE), F32)
    cos_t = jnp.concatenate([ones, cos, cos, ones[:, :HEAD_PAD - NOPE - ROPE]], axis=1)
    sin_t = jnp.concatenate([zeros, sin, sin, zeros[:, :HEAD_PAD - NOPE - ROPE]], axis=1)
    return cos_t, sin_t


def kernel(x, c, w_ada, b_ada, g_pre_mix, g_post_mix, w_in, g_q, w_uq, g_kv, w_ukv, gm_ln_g, gm_ln_b, w_spatial, b_spatial, w_out, g_pre_ffn, g_post_ffn, w_up, conv_w, conv_b, w_down, loss_target, m_w_ada, m_b_ada, m_g_pre_mix, m_g_post_mix, m_w_in, m_g_q, m_w_uq, m_g_kv, m_w_ukv, m_gm_ln_g, m_gm_ln_b, m_w_spatial, m_b_spatial, m_w_out, m_g_pre_ffn, m_g_post_ffn, m_w_up, m_conv_w, m_conv_b, m_w_down, v_w_ada, v_b_ada, v_g_pre_mix, v_g_post_mix, v_w_in, v_g_q, v_w_uq, v_g_kv, v_w_ukv, v_gm_ln_g, v_gm_ln_b, v_w_spatial, v_b_spatial, v_w_out, v_g_pre_ffn, v_g_post_ffn, v_w_up, v_conv_w, v_conv_b, v_w_down):
    s = x.shape[1]
    tm = min(256, s)
    tf = min(2 * ROW_SUB, s)
    tq = min(512, s)
    ts = min(2048, s)
    hp = HEADS * HEAD_PAD
    half = N_DEV // 2
    my_slot = 4 * lax.axis_index("x") + 2 * lax.axis_index("y") + lax.axis_index("c")
    x2d, target = x[0], loss_target[0]

    def t_(a):
        return jnp.swapaxes(a[0], 0, 1)

    w_in_t, m_in_t, v_in_t = t_(w_in), t_(m_w_in), t_(v_w_in)
    w_uq_t, m_uq_t, v_uq_t = t_(w_uq), t_(m_w_uq), t_(v_w_uq)
    w_up_t, m_up_t, v_up_t = t_(w_up), t_(m_w_up), t_(v_w_up)
    (g_c, g_in_t, g_uq_t, g_ukv, g_cw), _ = _exchange(
        [c, w_in_t.astype(MXU), w_uq_t.astype(MXU), w_ukv[0].astype(MXU), conv_w[0]], [], "gather_mixer_weights")

    w_in_f = g_in_t.reshape(-1, D_MODEL)
    o1, o2, o3 = Q_LORA, Q_LORA + KV_LORA, Q_LORA + KV_LORA + ROPE
    w1t = jnp.concatenate([w_in_f[:o2], jnp.zeros((NOPE, D_MODEL), MXU), w_in_f[o2:o3],
                           jnp.zeros((HEAD_PAD - NOPE - ROPE, D_MODEL), MXU), w_in_f[o3:]], axis=0)
    wqt = jnp.pad(g_uq_t, ((0, 0), (0, HEAD_PAD - NOPE - ROPE), (0, 0))).reshape(hp, Q_LORA)
    w_ukv_f = jnp.transpose(g_ukv, (1, 0, 2)).reshape(KV_LORA, HEADS, 2 * NOPE)
    pad_head = ((0, 0), (0, 0), (0, HEAD_PAD - NOPE))
    wkv = jnp.concatenate([jnp.pad(w_ukv_f[:, :, :NOPE], pad_head).reshape(KV_LORA, hp),
                           jnp.pad(w_ukv_f[:, :, NOPE:], pad_head).reshape(KV_LORA, hp)], axis=1)
    cb8 = conv_b.reshape(N_DEV, 1, FF_BLK)
    bias_exp = jnp.repeat(b_spatial[0].T, GM_DIM, axis=1)
    ln_g, ln_b = gm_ln_g.reshape(1, GM_WIDTH), gm_ln_b.reshape(1, GM_WIDTH)
    w_sp = w_spatial[0]
    cos_t, sin_t = _rope_tables(s)

    ada_part, c_act = _ada_fwd(g_c.reshape(N_DEV, D_MODEL), w_ada[0])
    _, (ada_recv,) = _exchange([], [(ada_part.reshape(N_DEV, 1, -1), _plain_slot)], "ada_rows")
    ada_raw = ada_recv.reshape(6, D_MODEL)
    ada_b = b_ada.reshape(6, D_MODEL)

    h1, z, qp, kp, vp, cqn, ckvn = _mix_in_fwd(x2d, ada_raw, ada_b, g_pre_mix, w1t, g_q, g_kv, wqt, wkv, cos_t, sin_t, tm)
    sgu = _gmlp_fwd(z, ln_g, ln_b, w_sp, bias_exp, tm)
    o_pad, lse, (g_out, g_up, g_down) = _attn_fwd(
        qp, kp, vp, tq, [w_out[0].astype(MXU), w_up_t.astype(MXU), w_down[0].astype(MXU)])
    w_out_f = g_out.reshape(2 * GM_WIDTH, D_MODEL)
    wo_attn = jnp.pad(w_out_f[:GM_WIDTH].reshape(HEADS, NOPE, D_MODEL), ((0, 0), (0, HEAD_PAD - NOPE), (0, 0)))
    wo = jnp.concatenate([wo_attn.reshape(hp, D_MODEL), w_out_f[GM_WIDTH:]], axis=0)
    wd = g_down.reshape(half, FF_BLK, D_MODEL)
    m_mix, x2, h2 = _out_proj_fwd(o_pad, sgu, wo, x2d, ada_raw, ada_b, g_post_mix, g_pre_ffn, tm)
    up_a, up_b, y_a, y_b, act = _ffn_up_fwd(h2, g_up, g_cw, cb8, tf)
    d_out, d_f, loss_part, d_gt2, d_g_post_ffn = _ffn_down_fwd(act, wd, x2, target, ada_raw, ada_b, g_post_ffn, tf)

    d_up, d_cw, d_cb = _ffn_down_bwd(d_f, wd, up_a, up_b, y_a, y_b, g_cw, tf)
    d_x2, d_m, d_sh2, d_sc2, d_g_pre_ffn, d_gt1, d_g_post_mix = _ffn_up_bwd(
        d_up, g_up, x2, m_mix, d_out, ada_raw, ada_b, g_pre_ffn, g_post_mix, tm)
    p_down = _tn_matmul(act, d_f[None], "dw_down", ts).reshape(N_DEV, -1, D_MODEL)
    p_up = _tn_matmul(d_up.reshape(N_DEV, s, FF_BLK), h2[None], "dw_up", ts).reshape(half, 2, FF_BLK, D_MODEL)
    d_m3 = d_m[None]
    dwo_attn = _tn_matmul(o_pad[None], d_m3, "dw_out_attn", ts)[0].reshape(HEADS, HEAD_PAD, D_MODEL)[:, :NOPE]
    dwo_sgu = _tn_matmul(sgu[None], d_m3, "dw_out_sgu", ts)[0]
    p_out = jnp.concatenate([dwo_attn.reshape(GM_WIDTH, D_MODEL), dwo_sgu], axis=0).reshape(N_DEV, -1, D_MODEL)
    d_o, d_sgu, delta = _out_proj_bwd(d_m, wo, o_pad, tm)
    def ffn_slot(j):
        return (j % half, j // half)

    dq, dk, dv, (r_out, r_up, r_down, r_cw) = _attn_bwd(
        qp, kp, vp, d_o, lse, delta, tq,
        [(p_out, _plain_slot), (p_up, ffn_slot), (p_down, _plain_slot), (d_cw, ffn_slot)])
    d_guv, d_ws, d_bs, d_ln_g, d_ln_b = _gmlp_bwd(z, d_sgu, ln_g, ln_b, w_sp, bias_exp, tm)
    za = Q_LORA + KV_LORA + HEAD_PAD
    grad_x, d_za, d_qp, d_kvp, d_sh1, d_sc1, d_g_pre_mix, d_g_q, d_g_kv = _mix_in_bwd(
        dq, dk, dv, z, d_guv, x2d, d_x2, ada_raw, ada_b, g_pre_mix, g_q, g_kv, w1t, wqt, wkv, cos_t, sin_t, tm)
    h1_3 = h1[None]
    dw1a = _tn_matmul(d_za[None], h1_3, "dw_in_a", ts)[0]
    dw1b = _tn_matmul(d_guv[None], h1_3, "dw_in_b", ts)[0]
    d_w_in_t = jnp.concatenate([dw1a[:o2], dw1a[o2 + NOPE:o2 + NOPE + ROPE], dw1b], axis=0)
    p_in = d_w_in_t.reshape(N_DEV, -1, D_MODEL)
    p_uq = _tn_matmul(d_qp[None], cqn[None], "dw_uq", ts)[0].reshape(HEADS, HEAD_PAD, Q_LORA)[:, :NOPE + ROPE]
    dwkv = _tn_matmul(ckvn[None], d_kvp[None], "dw_ukv", ts)[0]
    dwk = dwkv[:, :hp].reshape(KV_LORA, HEADS, HEAD_PAD)[:, :, :NOPE]
    dwv = dwkv[:, hp:].reshape(KV_LORA, HEADS, HEAD_PAD)[:, :, :NOPE]
    p_ukv = jnp.transpose(jnp.concatenate([dwk, dwv], axis=2), (1, 0, 2))

    vectors = [d_sh1, d_sc1, d_gt1, d_sh2, d_sc2, d_gt2, d_g_pre_mix, d_g_post_mix, d_g_pre_ffn, d_g_post_ffn]
    packed = _pack_small(vectors, d_g_q, d_g_kv, loss_part, d_ln_g, d_ln_b, d_bs, d_cb, d_ws)
    (g_small,), (r_in, r_uq, r_ukv) = _exchange(
        [packed], [(p_in, _plain_slot), (p_uq, _plain_slot), (p_ukv, _plain_slot)], "final_exchange")
    small_params = [(b_ada, m_b_ada, v_b_ada), (g_pre_mix, m_g_pre_mix, v_g_pre_mix),
                    (g_post_mix, m_g_post_mix, v_g_post_mix), (g_pre_ffn, m_g_pre_ffn, v_g_pre_ffn),
                    (g_post_ffn, m_g_post_ffn, v_g_post_ffn), (g_q, m_g_q, v_g_q), (g_kv, m_g_kv, v_g_kv),
                    (gm_ln_g, m_gm_ln_g, v_gm_ln_g), (gm_ln_b, m_gm_ln_b, v_gm_ln_b),
                    (w_spatial, m_w_spatial, v_w_spatial), (b_spatial, m_b_spatial, v_b_spatial),
                    tuple(a.reshape(N_DEV, FF_BLK) for a in (conv_b, m_conv_b, v_conv_b))]
    small_out, loss_row, d_ada_all = _adam_small(g_small, small_params)
    small_out[11] = tuple(o.reshape(conv_b.shape) for o in small_out[11])
    loss = loss_row[0, 0]

    def big(recv, w, m, v, name):
        g, d, m2, v2 = _adam_reduce(recv, w[0], m[0], v[0], name)
        return g[None], d[None], m2[None], v2[None]

    def big_t(recv, w_t, m_t, v_t, name):
        return tuple(jnp.swapaxes(o, 0, 1)[None] for o in _adam_reduce(recv, w_t, m_t, v_t, name))

    a_in = big_t(r_in, w_in_t, m_in_t, v_in_t, "adam_w_in")
    a_uq = big_t(r_uq, w_uq_t, m_uq_t, v_uq_t, "adam_w_uq")
    a_ukv = big(r_ukv, w_ukv, m_w_ukv, v_w_ukv, "adam_w_ukv")
    a_out = big(r_out, w_out, m_w_out, v_w_out, "adam_w_out")
    a_up = big_t(r_up, w_up_t, m_up_t, v_up_t, "adam_w_up")
    a_down = big(r_down, w_down, m_w_down, v_w_down, "adam_w_down")
    ada_cols = w_ada.shape[2]
    d_ada_cols = lax.dynamic_slice(d_ada_all.reshape(N_DEV, 6 * D_MODEL), (0, my_slot * ada_cols), (N_DEV, ada_cols))
    a_ada = tuple(t[None] for t in _adam_w_ada(c_act.T, d_ada_cols, w_ada[0], m_w_ada[0], v_w_ada[0]))
    a_cw = big(r_cw, conv_w, m_conv_w, v_conv_w, "adam_conv_w")

    def small(k):
        return small_out[k]

    per_weight = [a_ada, small(0), small(1), small(2), a_in, small(5), a_uq, small(6), a_ukv, small(7), small(8),
                  small(9), small(10), a_out, small(3), small(4), a_up, a_cw, small(11), a_down]
    outs = [loss, grad_x[None]]
    for k in range(4):
        outs += [t[k] for t in per_weight]
    return tuple(outs)
```

```python
import functools

import jax
import jax.numpy as jnp
from jax import lax
from jax.experimental import pallas as pl
from jax.experimental.pallas import tpu as pltpu

F32 = jnp.float32
MXU = jnp.bfloat16

N_DEV = 8
D_MODEL = 1024
HEADS = 8
HEAD_PAD = 128
NOPE = 64
ROPE = 32
Q_LORA = 256
KV_LORA = 128
GM_WIDTH = 512
GM_DIM = 64
GM_CHUNK = 128
CHUNK_SHIFT = 6
ROPE_THETA = 10000.0
ATTN_SCALE = (NOPE + ROPE) ** -0.5
Z_COLS = 1536
FF_BLK = 704
EPS = 1e-6
ADAM_LR = 0.001
ADAM_B1 = 0.9
ADAM_B2 = 0.999
ADAM_EPS = 1e-08
ADAM_WD = 0.01
ADAM_STEP = 10
VMEM_LIMIT = 56 * 1024 * 1024
MESH = pl.DeviceIdType.MESH


def _dot(a, b):
    return jnp.dot(a, b, preferred_element_type=F32)


def _dot_nt(a, b):
    return lax.dot_general(a, b, (((1,), (1,)), ((), ())), preferred_element_type=F32)


def _dot_tn(a, b):
    return lax.dot_general(a, b, (((0,), (0,)), ((), ())), preferred_element_type=F32)


def _call(body, *, name, grid, in_specs, out_specs, out_shape, scratch=(), sem=None):
    params = pltpu.CompilerParams(dimension_semantics=sem, vmem_limit_bytes=VMEM_LIMIT)
    return pl.pallas_call(body, name=name, grid=grid, in_specs=in_specs, out_specs=out_specs,
                          out_shape=out_shape, scratch_shapes=list(scratch), compiler_params=params)


def _full(shape):
    n = len(shape)
    return pl.BlockSpec(shape, lambda *_: (0,) * n)


def _rows(tm, cols, col_block=0):
    return pl.BlockSpec((tm, cols), lambda i: (i, col_block))


def _sds(shape, dtype):
    return jax.ShapeDtypeStruct(shape, dtype)


def _row(ref, k):
    return ref[pl.ds(k, 1), :]


def _rms(x):
    r = lax.rsqrt(jnp.mean(x * x, axis=-1, keepdims=True) + EPS)
    return x * r, r


def _rms_bwd(d_hat, hat, r):
    return r * (d_hat - hat * jnp.mean(d_hat * hat, axis=-1, keepdims=True))


def _rope_partner(t):
    lane = lax.broadcasted_iota(jnp.int32, t.shape, 1)
    swapped = jnp.where(lane < NOPE + ROPE // 2, -pltpu.roll(t, HEAD_PAD - ROPE // 2, 1), pltpu.roll(t, ROPE // 2, 1))
    return jnp.where((lane >= NOPE) & (lane < NOPE + ROPE), swapped, 0.0)


def _rope(t, cos, sin):
    return t * cos + _rope_partner(t) * sin


def _rope_transposed(g, cos, sin):
    return g * cos - _rope_partner(g * sin)


def _gelu(x):
    return x * (0.5 * (1.0 + jnp.tanh(0.7978845608028654 * (x + 0.044715 * (x * x * x)))))


def _gelu_grad(x):
    t = jnp.tanh(0.7978845608028654 * (x + 0.044715 * (x * x * x)))
    return 0.5 * (1.0 + t) + 0.5 * x * (1.0 - t * t) * (0.7978845608028654 * (1.0 + 3.0 * 0.044715 * (x * x)))


def _split_dot(x, mat):
    hi = x.astype(MXU)
    lo = (x - hi.astype(F32)).astype(MXU)
    return _dot(hi, mat) + _dot(lo, mat)


def _split_dot3(x, mat):
    hi = x.astype(MXU)
    r1 = x - hi.astype(F32)
    mid = r1.astype(MXU)
    lo = (r1 - mid.astype(F32)).astype(MXU)
    return (_dot(hi, mat) + _dot(mid, mat)) + _dot(lo, mat)


def _seg_matrix():
    r = lax.broadcasted_iota(jnp.int32, (GM_WIDTH, GM_WIDTH), 0) >> 6
    c = lax.broadcasted_iota(jnp.int32, (GM_WIDTH, GM_WIDTH), 1) >> 6
    return jnp.where(r == c, 1.0 / GM_DIM, 0.0).astype(MXU)


def _spatial_mask():
    i = lax.broadcasted_iota(jnp.int32, (GM_CHUNK, GM_CHUNK), 0) >> CHUNK_SHIFT
    j = lax.broadcasted_iota(jnp.int32, (GM_CHUNK, GM_CHUNK), 1) >> CHUNK_SHIFT
    return (j <= i).astype(F32)


def _head_lane_mask(h, rows):
    lane = lax.broadcasted_iota(jnp.int32, (rows, GM_WIDTH), 1) >> 6
    return lane == h


def _my_place():
    return lax.axis_index("x"), lax.axis_index("y"), lax.axis_index("c")


def _flat(p):
    return 4 * p[0] + 2 * p[1] + p[2]


def _comm_sems(n):
    return [pltpu.SemaphoreType.DMA((7 * n,)), pltpu.SemaphoreType.DMA((7 * n,)), pltpu.SemaphoreType.DMA((n,))]


def _gather_steps(ins, outs, sems):
    send_sems, recv_sems, local_sems = sems
    n = len(ins)
    x, y, c = _my_place()
    me, sibling = (x, y, c), (x, y, 1 - c)
    chips = [(1 - x, y), (x, 1 - y), (1 - x, 1 - y)]

    def copy(a, k, block, to, src=None):
        slot = outs[a].at[_flat(block)]
        return pltpu.make_async_remote_copy(
            src_ref=slot if src is None else src, dst_ref=slot,
            send_sem=send_sems.at[7 * a + k], recv_sem=recv_sems.at[7 * a + k],
            device_id=to, device_id_type=MESH)

    def mine():
        return [pltpu.make_async_copy(ins[a], outs[a].at[_flat(me)], local_sems.at[a]) for a in range(n)]

    def first():
        cps = []
        for a in range(n):
            cps.append(copy(a, 0, me, sibling, src=ins[a]))
            cps += [copy(a, 1 + j, me, (*chip, c), src=ins[a]) for j, chip in enumerate(chips)]
        return cps

    def passed():
        return [copy(a, 4 + j, (*chip, c), sibling) for a in range(n) for j, chip in enumerate(chips)]

    def start():
        for cp in mine() + first():
            cp.start()

    def forward():
        for a in range(n):
            for j, chip in enumerate(chips):
                copy(a, 1 + j, (*chip, c), me).wait_recv()
                copy(a, 4 + j, (*chip, c), sibling).start()

    def finish():
        for a in range(n):
            copy(a, 0, sibling, me).wait_recv()
            for j, chip in enumerate(chips):
                copy(a, 4 + j, (*chip, 1 - c), me).wait_recv()
        for cp in first() + passed():
            cp.wait_send()
        for cp in mine():
            cp.wait()

    return start, forward, finish


def _scatter_steps(ins, outs, sems, slots):
    send_sems, recv_sems, local_sems = sems
    n = len(ins)
    flips = [(fx, fy, fc) for fx in (0, 1) for fy in (0, 1) for fc in (0, 1)][1:]
    me = _my_place()

    def peer(f):
        return tuple(1 - v if b else v for v, b in zip(me, f))

    def copy(a, k, arriving=False):
        p = peer(flips[k])
        return pltpu.make_async_remote_copy(
            src_ref=ins[a].at[slots[a](_flat(p))], dst_ref=outs[a].at[_flat(p if arriving else me)],
            send_sem=send_sems.at[7 * a + k], recv_sem=recv_sems.at[7 * a + k],
            device_id=p, device_id_type=MESH)

    def mine():
        return [pltpu.make_async_copy(ins[a].at[slots[a](_flat(me))], outs[a].at[_flat(me)], local_sems.at[a])
                for a in range(n)]

    def start():
        for cp in mine() + [copy(a, k) for a in range(n) for k in range(7)]:
            cp.start()

    def finish():
        for a in range(n):
            for k in range(7):
                copy(a, k, arriving=True).wait_recv()
        for a in range(n):
            for k in range(7):
                copy(a, k).wait_send()
        for cp in mine():
            cp.wait()

    return start, finish


def _plain_slot(j):
    return (j,)


def _scatter_out_shape(arr, slot):
    return _sds((N_DEV,) + arr.shape[len(slot(0)):], arr.dtype)


def _exchange(gathered, scattered, name):
    ng, ns = len(gathered), len(scattered)
    slots = [slot for _, slot in scattered]

    def body(*refs):
        g_in, s_in = refs[:ng], refs[ng:ng + ns]
        g_out, s_out = refs[ng + ns:2 * ng + ns], refs[2 * ng + ns:2 * (ng + ns)]
        sems = refs[2 * (ng + ns):]
        g_start, g_forward, g_finish = _gather_steps(g_in, g_out, sems[:3])
        s_start, s_finish = _scatter_steps(s_in, s_out, sems[3:], slots)
        g_start()
        s_start()
        g_forward()
        g_finish()
        s_finish()

    any_spec = pl.BlockSpec(memory_space=pl.ANY)
    outs = pl.pallas_call(
        body, name=name,
        in_specs=[any_spec] * (ng + ns), out_specs=[any_spec] * (ng + ns),
        out_shape=[_sds((N_DEV,) + a.shape, a.dtype) for a in gathered]
        + [_scatter_out_shape(a, slot) for a, slot in scattered],
        scratch_shapes=_comm_sems(max(ng, 1)) + _comm_sems(max(ns, 1)),
    )(*gathered, *[a for a, _ in scattered])
    return outs[:ng], outs[ng:]


def _ada_fwd(c_all, w_ada):
    def body(c_ref, w_ref, part_ref, act_ref):
        cv = c_ref[...]
        act = cv * jax.nn.sigmoid(cv)
        act_ref[...] = act
        part_ref[...] = _dot(act.astype(MXU), w_ref[...].astype(MXU))

    cols = w_ada.shape[1]
    return _call(body, name="ada_fwd", grid=(1,),
                 in_specs=[_full(c_all.shape), _full(w_ada.shape)],
                 out_specs=[_full((N_DEV, cols)), _full(c_all.shape)],
                 out_shape=[_sds((N_DEV, cols), F32), _sds(c_all.shape, F32)])(c_all, w_ada)


def _mix_in_fwd(x, ada_raw, ada_b, g_pre, w1, g_q, g_kv, wq, wkv, cos_t, sin_t, tm):
    s = x.shape[0]

    def body(x_ref, ar_ref, ab_ref, g_ref, w1_ref, gq_ref, gkv_ref, wq_ref, wkv_ref, cos_ref, sin_ref,
             h1_ref, z_ref, qp_ref, kp_ref, vp_ref, cqn_ref, ckvn_ref):
        sh = _row(ar_ref, 0) + _row(ab_ref, 0)
        sc = _row(ar_ref, 1) + _row(ab_ref, 1)
        xn, _ = _rms(x_ref[...])
        hb = ((xn * g_ref[...]) * (1.0 + sc) + sh).astype(MXU)
        h1_ref[...] = hb
        z = _dot_nt(hb, w1_ref[...])
        z_ref[...] = z
        cos, sin = cos_ref[...], sin_ref[...]
        cqn = (_rms(z[:, :Q_LORA])[0] * gq_ref[...]).astype(MXU)
        ckvn = (_rms(z[:, Q_LORA:Q_LORA + KV_LORA])[0] * gkv_ref[...]).astype(MXU)
        cqn_ref[...] = cqn
        ckvn_ref[...] = ckvn
        q = _dot_nt(cqn, wq_ref[...])
        kv = _dot(ckvn, wkv_ref[...])
        k_rope = _rope(z[:, Q_LORA + KV_LORA:Q_LORA + KV_LORA + HEAD_PAD], cos, sin)
        for h in range(HEADS):
            blk = slice(h * HEAD_PAD, (h + 1) * HEAD_PAD)
            qp_ref[:, blk] = _rope(q[:, blk], cos, sin).astype(MXU)
            kp_ref[:, blk] = (kv[:, blk] + k_rope).astype(MXU)
        v_lane = lax.broadcasted_iota(jnp.int32, (tm, HEADS * HEAD_PAD), 1) & (HEAD_PAD - 1)
        vp_ref[...] = jnp.where(v_lane == NOPE, 1.0, kv[:, HEADS * HEAD_PAD:]).astype(MXU)

    hp = HEADS * HEAD_PAD
    return _call(
        body, name="mix_in_fwd", grid=(s // tm,), sem=("parallel",),
        in_specs=[_rows(tm, D_MODEL), _full(ada_raw.shape), _full(ada_b.shape), _full(g_pre.shape), _full(w1.shape),
                  _full(g_q.shape), _full(g_kv.shape), _full(wq.shape), _full(wkv.shape),
                  _rows(tm, HEAD_PAD), _rows(tm, HEAD_PAD)],
        out_specs=[_rows(tm, D_MODEL), _rows(tm, Z_COLS), _rows(tm, hp), _rows(tm, hp), _rows(tm, hp),
                   _rows(tm, Q_LORA), _rows(tm, KV_LORA)],
        out_shape=[_sds((s, D_MODEL), MXU), _sds((s, Z_COLS), F32), _sds((s, hp), MXU), _sds((s, hp), MXU),
                   _sds((s, hp), MXU), _sds((s, Q_LORA), MXU), _sds((s, KV_LORA), MXU)],
    )(x, ada_raw, ada_b, g_pre, w1, g_q, g_kv, wq, wkv, cos_t, sin_t)


def _gm_norm(zv, seg):
    gv = _gelu(zv)
    cen = gv - _split_dot(gv, seg)
    rstd = lax.rsqrt(_split_dot(cen * cen, seg) + EPS)
    return gv, cen * rstd, rstd


def _gm_mix(wm, vb, rows):
    out = jnp.zeros((rows, GM_WIDTH), F32)
    for h in range(HEADS):
        out = out + jnp.where(_head_lane_mask(h, rows), _dot(wm[h], vb), 0.0)
    return out


def _gmlp_fwd(z, ln_g, ln_b, w_sp, bias_exp, tm):
    s = z.shape[0]
    nblk = tm // GM_CHUNK

    def body(zu_ref, zv_ref, lg_ref, lb_ref, w_ref, be_ref, sgu_ref):
        seg = _seg_matrix()
        mask = _spatial_mask()
        wm = [(w_ref[h] * mask).astype(MXU) for h in range(HEADS)]
        gu = _gelu(zu_ref[...])
        _, vhat, _ = _gm_norm(zv_ref[...], seg)
        vln = (vhat * lg_ref[...] + lb_ref[...]).astype(MXU)
        for n in range(nblk):
            rows = slice(n * GM_CHUNK, (n + 1) * GM_CHUNK)
            mixed = _gm_mix(wm, vln[rows], GM_CHUNK) + be_ref[...]
            sgu_ref[rows, :] = (gu[rows] * mixed).astype(MXU)

    return _call(
        body, name="gmlp_fwd", grid=(s // tm,), sem=("parallel",),
        in_specs=[_rows(tm, GM_WIDTH, 1), _rows(tm, GM_WIDTH, 2), _full(ln_g.shape), _full(ln_b.shape),
                  _full(w_sp.shape), _full(bias_exp.shape)],
        out_specs=_rows(tm, GM_WIDTH), out_shape=_sds((s, GM_WIDTH), MXU),
    )(z, z, ln_g, ln_b, w_sp, bias_exp)


def _diag_mask(t):
    qc = lax.broadcasted_iota(jnp.int32, (t, t), 0) >> CHUNK_SHIFT
    kc = lax.broadcasted_iota(jnp.int32, (t, t), 1) >> CHUNK_SHIFT
    return kc <= qc


NEG_BIG = -1e30
ATTN_HEADS_PER_STEP = 2


def _attn_fwd(qp, kp, vp, tq, gathered):
    s = qp.shape[0]
    nq = s // tq
    hb = ATTN_HEADS_PER_STEP
    groups = HEADS // hb
    width = hb * HEAD_PAD
    ng = len(gathered)

    def body(q_ref, k_ref, v_ref, *rest):
        g_in, (o_ref, lse_ref), g_out = rest[:ng], rest[ng:ng + 2], rest[ng + 2:2 * ng + 2]
        m_sc, acc_sc = rest[2 * ng + 2:2 * ng + 4]
        g_start, g_forward, g_finish = _gather_steps(g_in, g_out, rest[2 * ng + 4:])
        g, i = pl.program_id(0), pl.program_id(1)
        pl.when((g == 0) & (i == 0))(g_start)
        pl.when((g == groups - 1) & (i == 0))(g_forward)
        m_sc[...] = jnp.full(m_sc.shape, NEG_BIG, F32)
        acc_sc[...] = jnp.zeros(acc_sc.shape, F32)

        def tile(j, masked):
            rows = pl.ds(pl.multiple_of(j * tq, tq), tq)
            for hh in range(hb):
                lanes = slice(hh * HEAD_PAD, (hh + 1) * HEAD_PAD)
                sc = _dot_nt(q_ref[:, lanes], k_ref[rows, lanes]) * ATTN_SCALE
                if masked:
                    sc = jnp.where(_diag_mask(tq), sc, NEG_BIG)
                blocks = [sc[:, b * 128:(b + 1) * 128] for b in range(tq // 128)]
                m_prev = m_sc[hh]
                m_tile = jnp.max(functools.reduce(jnp.maximum, blocks), axis=-1, keepdims=True)
                m_new = jnp.maximum(m_prev, m_tile)
                alpha = jnp.exp(m_prev - m_new)
                p = jnp.concatenate([jnp.exp(b - m_new) for b in blocks], axis=1).astype(MXU)
                acc_sc[hh] = alpha * acc_sc[hh] + _dot(p, v_ref[rows, lanes])
                m_sc[hh] = m_new

        def off_diagonal_pair(p, carry):
            tile(2 * p, False)
            tile(2 * p + 1, False)
            return carry

        lax.fori_loop(0, i // 2, off_diagonal_pair, 0)

        @pl.when(i % 2 == 1)
        def _():
            tile(i - 1, False)

        tile(i, True)
        for hh in range(hb):
            lanes = slice(hh * HEAD_PAD, (hh + 1) * HEAD_PAD)
            acc = acc_sc[hh]
            denom = acc[:, NOPE:NOPE + 1]
            o_ref[:, lanes] = (acc / denom).astype(MXU)
            lse_ref[hh] = m_sc[hh][:, :1] + jnp.log(denom)
        pl.when((g == groups - 1) & (i == nq - 1))(g_finish)

    q_spec = pl.BlockSpec((tq, width), lambda g, i: (i, g))
    kv_spec = pl.BlockSpec((s, width), lambda g, i: (0, g))
    any_spec = pl.BlockSpec(memory_space=pl.ANY)
    outs = _call(
        body, name="attn_fwd", grid=(groups, nq), sem=("arbitrary", "arbitrary"),
        in_specs=[q_spec, kv_spec, kv_spec] + [any_spec] * ng,
        out_specs=[q_spec, pl.BlockSpec((hb, tq, 1), lambda g, i: (g, i, 0))] + [any_spec] * ng,
        out_shape=[_sds(qp.shape, MXU), _sds((HEADS, s, 1), F32)]
        + [_sds((N_DEV,) + a.shape, a.dtype) for a in gathered],
        scratch=[pltpu.VMEM((hb, tq, HEAD_PAD), F32), pltpu.VMEM((hb, tq, HEAD_PAD), F32)] + _comm_sems(ng),
    )(qp, kp, vp, *gathered)
    return outs[0], outs[1], outs[2:]


def _out_proj_fwd(o_pad, sgu, wo, x, ada_raw, ada_b, g_post_mix, g_pre_ffn, tm):
    s = x.shape[0]
    hp = HEADS * HEAD_PAD

    def body(o_ref, sgu_ref, wo_ref, x_ref, ar_ref, ab_ref, gpm_ref, gpf_ref, m_ref, x2_ref, h2_ref):
        gt1 = _row(ar_ref, 2) + _row(ab_ref, 2)
        sh2 = _row(ar_ref, 3) + _row(ab_ref, 3)
        sc2 = _row(ar_ref, 4) + _row(ab_ref, 4)
        m = _dot(o_ref[...], wo_ref[pl.ds(0, hp), :]) + _dot(sgu_ref[...], wo_ref[pl.ds(hp, GM_WIDTH), :])
        m_ref[...] = m
        x2 = x_ref[...] + gt1 * (_rms(m)[0] * gpm_ref[...])
        x2_ref[...] = x2
        h2_ref[...] = ((_rms(x2)[0] * gpf_ref[...]) * (1.0 + sc2) + sh2).astype(MXU)

    return _call(
        body, name="out_proj_fwd", grid=(s // tm,), sem=("parallel",),
        in_specs=[_rows(tm, hp), _rows(tm, GM_WIDTH), _full(wo.shape), _rows(tm, D_MODEL), _full(ada_raw.shape),
                  _full(ada_b.shape), _full(g_post_mix.shape), _full(g_pre_ffn.shape)],
        out_specs=[_rows(tm, D_MODEL)] * 3,
        out_shape=[_sds((s, D_MODEL), F32), _sds((s, D_MODEL), F32), _sds((s, D_MODEL), MXU)],
    )(o_pad, sgu, wo, x, ada_raw, ada_b, g_post_mix, g_pre_ffn)


def _conv(u, halo, cw_ref, cb_ref):
    ext = jnp.concatenate([halo, u], axis=0)
    m1, m2 = pltpu.roll(ext, 1, 0)[8:], pltpu.roll(ext, 2, 0)[8:]
    return cb_ref[0] + ((m2 * cw_ref[0, pl.ds(0, 1), :] + m1 * cw_ref[0, pl.ds(1, 1), :]) + u * cw_ref[0, pl.ds(2, 1), :])


ROW_SUB = 256


def _sub_blocks(tm):
    return [slice(r, r + ROW_SUB) for r in range(0, tm, ROW_SUB)]


def _ffn_up_fwd(h2, w_up, conv_w, conv_b, tm):
    s = h2.shape[0]
    half = N_DEV // 2

    def body(h_ref, wa_ref, wb_ref, cwa_ref, cwb_ref, cba_ref, cbb_ref,
             ua_ref, ub_ref, ya_ref, yb_ref, act_ref, halo_a, halo_b):
        i = pl.program_id(1)

        @pl.when(i == 0)
        def _():
            halo_a[...] = jnp.zeros(halo_a.shape, F32)
            halo_b[...] = jnp.zeros(halo_b.shape, F32)

        ha, hb = halo_a[...], halo_b[...]
        for rows in _sub_blocks(tm):
            h = h_ref[rows, :]
            ua = _dot_nt(h, wa_ref[0])
            ub = _dot_nt(h, wb_ref[0])
            ua_ref[0, rows, :] = ua
            ub_ref[0, rows, :] = ub
            ya = _conv(ua, ha, cwa_ref, cba_ref)
            yb = _conv(ub, hb, cwb_ref, cbb_ref)
            ya_ref[0, rows, :] = ya
            yb_ref[0, rows, :] = yb
            ha, hb = ua[ROW_SUB - 8:], ub[ROW_SUB - 8:]
            act_ref[0, rows, :] = ((ya * jax.nn.sigmoid(ya)) * yb).astype(MXU)
        halo_a[...] = ha
        halo_b[...] = hb

    def blk(shape, off):
        return pl.BlockSpec(shape, lambda j, i: (j + off, 0, 0))

    def tok(off=0):
        return pl.BlockSpec((1, tm, FF_BLK), lambda j, i: (j + off, i, 0))

    return _call(
        body, name="ffn_up_fwd", grid=(half, s // tm), sem=("parallel", "arbitrary"),
        in_specs=[pl.BlockSpec((tm, D_MODEL), lambda j, i: (i, 0)),
                  blk((1, FF_BLK, D_MODEL), 0), blk((1, FF_BLK, D_MODEL), half),
                  blk((1, 3, FF_BLK), 0), blk((1, 3, FF_BLK), half), blk((1, 1, FF_BLK), 0), blk((1, 1, FF_BLK), half)],
        out_specs=[tok()] * 5,
        out_shape=[_sds((half, s, FF_BLK), F32)] * 4 + [_sds((half, s, FF_BLK), MXU)],
        scratch=[pltpu.VMEM((8, FF_BLK), F32), pltpu.VMEM((8, FF_BLK), F32)],
    )(h2, w_up, w_up, conv_w, conv_w, conv_b, conv_b)


def _ffn_down_fwd(act, wd, x2, target, ada_raw, ada_b, g_post_ffn, tm):
    s = x2.shape[0]
    half = N_DEV // 2

    def body(act_ref, wd_ref, x2_ref, t_ref, ar_ref, ab_ref, g_ref, dout_ref, df_ref, loss_ref, dgt_ref, dg_ref):
        i = pl.program_id(0)

        @pl.when(i == 0)
        def _():
            loss_ref[...] = jnp.zeros(loss_ref.shape, F32)
            dgt_ref[...] = jnp.zeros(dgt_ref.shape, F32)
            dg_ref[...] = jnp.zeros(dg_ref.shape, F32)

        gt2 = _row(ar_ref, 5) + _row(ab_ref, 5)
        g = g_ref[...]
        for rows in _sub_blocks(tm):
            f = _dot(act_ref[0, rows, :], wd_ref[0])
            for j in range(1, half):
                f = f + _dot(act_ref[j, rows, :], wd_ref[j])
            fhat, rf = _rms(f)
            fn = fhat * g
            err = (x2_ref[rows, :] + gt2 * fn) - t_ref[rows, :]
            loss_ref[...] += 0.5 * jnp.sum(jnp.mean(err * err, axis=-1, keepdims=True))
            d_out = err * (1.0 / D_MODEL)
            dout_ref[rows, :] = d_out
            dgt_ref[...] += jnp.sum(d_out * fn, axis=0, keepdims=True)
            d_fn = d_out * gt2
            dg_ref[...] += jnp.sum(d_fn * fhat, axis=0, keepdims=True)
            df_ref[rows, :] = _rms_bwd(d_fn * g, fhat, rf).astype(MXU)

    vec = pl.BlockSpec((1, D_MODEL), lambda i: (0, 0))
    return _call(
        body, name="ffn_down_fwd", grid=(s // tm,), sem=("arbitrary",),
        in_specs=[pl.BlockSpec((half, tm, FF_BLK), lambda i: (0, i, 0)), _full(wd.shape), _rows(tm, D_MODEL),
                  _rows(tm, D_MODEL), _full(ada_raw.shape), _full(ada_b.shape), _full(g_post_ffn.shape)],
        out_specs=[_rows(tm, D_MODEL), _rows(tm, D_MODEL), pl.BlockSpec((1, 128), lambda i: (0, 0)), vec, vec],
        out_shape=[_sds((s, D_MODEL), F32), _sds((s, D_MODEL), MXU), _sds((1, 128), F32),
                   _sds((1, D_MODEL), F32), _sds((1, D_MODEL), F32)],
    )(act, wd, x2, target, ada_raw, ada_b, g_post_ffn)


def _ffn_down_bwd(d_f, wd, up_a, up_b, y_a, y_b, conv_w, tm):
    s = d_f.shape[0]
    half = N_DEV // 2
    nt = s // tm

    def body(df_ref, wd_ref, ua_ref, ub_ref, ya_ref, yb_ref, cwa_ref, cwb_ref,
             dup_ref, dcw_ref, dcb_ref, next_a, next_b):
        i = pl.program_id(1)

        @pl.when(i == 0)
        def _():
            next_a[...] = jnp.zeros(next_a.shape, F32)
            next_b[...] = jnp.zeros(next_b.shape, F32)
            dcw_ref[...] = jnp.zeros(dcw_ref.shape, F32)
            dcb_ref[...] = jnp.zeros(dcb_ref.shape, F32)

        def conv_bwd(d_y, u, nxt, cw_ref, part, rows):
            ext = jnp.concatenate([d_y, nxt], axis=0)
            p1 = pltpu.roll(ext, ROW_SUB + 7, 0)[:ROW_SUB]
            p2 = pltpu.roll(ext, ROW_SUB + 6, 0)[:ROW_SUB]
            d_u = (d_y * cw_ref[0, pl.ds(2, 1), :] + p1 * cw_ref[0, pl.ds(1, 1), :]) + p2 * cw_ref[0, pl.ds(0, 1), :]
            dup_ref[0, part, rows, :] = d_u.astype(MXU)
            dcb_ref[0, part] += jnp.sum(d_y, axis=0, keepdims=True)
            dcw_ref[0, part, pl.ds(0, 1), :] += jnp.sum(p2 * u, axis=0, keepdims=True)
            dcw_ref[0, part, pl.ds(1, 1), :] += jnp.sum(p1 * u, axis=0, keepdims=True)
            dcw_ref[0, part, pl.ds(2, 1), :] += jnp.sum(d_y * u, axis=0, keepdims=True)
            return d_y[:8]

        nxa, nxb = next_a[...], next_b[...]
        for rows in reversed(_sub_blocks(tm)):
            d_act = _dot_nt(df_ref[rows, :], wd_ref[0])
            ya, yb = ya_ref[0, rows, :], yb_ref[0, rows, :]
            sig = jax.nn.sigmoid(ya)
            d_ya = d_act * yb * (sig * (1.0 + ya * (1.0 - sig)))
            d_yb = d_act * (ya * sig)
            nxa = conv_bwd(d_ya, ua_ref[0, rows, :], nxa, cwa_ref, 0, rows)
            nxb = conv_bwd(d_yb, ub_ref[0, rows, :], nxb, cwb_ref, 1, rows)
        next_a[...] = nxa
        next_b[...] = nxb

    def rev(i):
        return nt - 1 - i

    def blk(shape, off):
        return pl.BlockSpec(shape, lambda j, i: (j + off, 0, 0))

    tok = pl.BlockSpec((1, tm, FF_BLK), lambda j, i: (j, rev(i), 0))
    acc3 = pl.BlockSpec((1, 2, 3, FF_BLK), lambda j, i: (j, 0, 0, 0))
    acc1 = pl.BlockSpec((1, 2, 1, FF_BLK), lambda j, i: (j, 0, 0, 0))
    return _call(
        body, name="ffn_down_bwd", grid=(half, nt), sem=("parallel", "arbitrary"),
        in_specs=[pl.BlockSpec((tm, D_MODEL), lambda j, i: (rev(i), 0)), blk((1, FF_BLK, D_MODEL), 0),
                  tok, tok, tok, tok, blk((1, 3, FF_BLK), 0), blk((1, 3, FF_BLK), half)],
        out_specs=[pl.BlockSpec((1, 2, tm, FF_BLK), lambda j, i: (j, 0, rev(i), 0)), acc3, acc1],
        out_shape=[_sds((half, 2, s, FF_BLK), MXU), _sds((half, 2, 3, FF_BLK), F32), _sds((half, 2, 1, FF_BLK), F32)],
        scratch=[pltpu.VMEM((8, FF_BLK), F32), pltpu.VMEM((8, FF_BLK), F32)],
    )(d_f, wd, up_a, up_b, y_a, y_b, conv_w, conv_w)


def _ffn_up_bwd(d_up, w_up, x2, m, d_out, ada_raw, ada_b, g_pre_ffn, g_post_mix, tm):
    s = x2.shape[0]
    half = N_DEV // 2

    def body(dup_ref, w_ref, x2_ref, m_ref, dout_ref, ar_ref, ab_ref, gpf_ref, gpm_ref,
             dx_ref, dm_ref, dsh_ref, dsc_ref, dgpf_ref, dgt1_ref, dgpm_ref):
        i = pl.program_id(0)

        @pl.when(i == 0)
        def _():
            for r in (dsh_ref, dsc_ref, dgpf_ref, dgt1_ref, dgpm_ref):
                r[...] = jnp.zeros(r.shape, F32)

        gt1 = _row(ar_ref, 2) + _row(ab_ref, 2)
        sc2 = _row(ar_ref, 4) + _row(ab_ref, 4)
        gpf, gpm = gpf_ref[...], gpm_ref[...]
        d_h2 = _dot(dup_ref[0, 0], w_ref[0])
        for j in range(1, half):
            d_h2 = d_h2 + _dot(dup_ref[j, 0], w_ref[j])
        for j in range(half):
            d_h2 = d_h2 + _dot(dup_ref[j, 1], w_ref[half + j])
        x2n, r2 = _rms(x2_ref[...])
        dsh_ref[...] += jnp.sum(d_h2, axis=0, keepdims=True)
        dsc_ref[...] += jnp.sum(d_h2 * (x2n * gpf), axis=0, keepdims=True)
        d_mod = d_h2 * (1.0 + sc2)
        dgpf_ref[...] += jnp.sum(d_mod * x2n, axis=0, keepdims=True)
        d_x2 = dout_ref[...] + _rms_bwd(d_mod * gpf, x2n, r2)
        dx_ref[...] = d_x2
        mhat, rm = _rms(m_ref[...])
        dgt1_ref[...] += jnp.sum(d_x2 * (mhat * gpm), axis=0, keepdims=True)
        d_mn = d_x2 * gt1
        dgpm_ref[...] += jnp.sum(d_mn * mhat, axis=0, keepdims=True)
        dm_ref[...] = _rms_bwd(d_mn * gpm, mhat, rm).astype(MXU)

    vec = pl.BlockSpec((1, D_MODEL), lambda i: (0, 0))
    tok = pl.BlockSpec((half, 2, tm, FF_BLK), lambda i: (0, 0, i, 0))
    return _call(
        body, name="ffn_up_bwd", grid=(s // tm,), sem=("arbitrary",),
        in_specs=[tok, _full(w_up.shape), _rows(tm, D_MODEL), _rows(tm, D_MODEL), _rows(tm, D_MODEL),
                  _full(ada_raw.shape), _full(ada_b.shape), _full(g_pre_ffn.shape), _full(g_post_mix.shape)],
        out_specs=[_rows(tm, D_MODEL), _rows(tm, D_MODEL), vec, vec, vec, vec, vec],
        out_shape=[_sds((s, D_MODEL), F32), _sds((s, D_MODEL), MXU)] + [_sds((1, D_MODEL), F32)] * 5,
    )(d_up, w_up, x2, m, d_out, ada_raw, ada_b, g_pre_ffn, g_post_mix)


def _out_proj_bwd(d_m, wo, o_pad, tm):
    s = d_m.shape[0]
    hp = HEADS * HEAD_PAD

    def body(dm_ref, wo_ref, o_ref, do_ref, dsgu_ref, delta_ref):
        d_cat = _dot_nt(dm_ref[...], wo_ref[...])
        d_o = d_cat[:, :hp]
        do_ref[...] = d_o.astype(MXU)
        dsgu_ref[...] = d_cat[:, hp:]
        prod = d_o * o_ref[...].astype(F32)
        for h in range(HEADS):
            delta_ref[h] = jnp.sum(prod[:, h * HEAD_PAD:(h + 1) * HEAD_PAD], axis=-1, keepdims=True)

    return _call(
        body, name="out_proj_bwd", grid=(s // tm,), sem=("parallel",),
        in_specs=[_rows(tm, D_MODEL), _full(wo.shape), _rows(tm, hp)],
        out_specs=[_rows(tm, hp), _rows(tm, GM_WIDTH), pl.BlockSpec((HEADS, tm, 1), lambda i: (0, i, 0))],
        out_shape=[_sds((s, hp), MXU), _sds((s, GM_WIDTH), F32), _sds((HEADS, s, 1), F32)],
    )(d_m, wo, o_pad)


def _attn_bwd(qp, kp, vp, d_o, lse, delta, tq, scattered):
    s = qp.shape[0]
    nq = s // tq
    hb = ATTN_HEADS_PER_STEP
    groups = HEADS // hb
    width = hb * HEAD_PAD
    ns = len(scattered)
    slots = [slot for _, slot in scattered]

    def body(q_ref, k_ref, v_ref, do_ref, lse_ref, dl_ref, *rest):
        s_in, (dq_ref, dk_ref, dv_ref), s_out = rest[:ns], rest[ns:ns + 3], rest[ns + 3:2 * ns + 3]
        dk_sc, dv_sc = rest[2 * ns + 3:2 * ns + 5]
        s_start, s_finish = _scatter_steps(s_in, s_out, rest[2 * ns + 5:], slots)
        g, j = pl.program_id(0), pl.program_id(1)
        pl.when((g == 0) & (j == 0))(s_start)

        @pl.when(j == 0)
        def _():
            dq_ref[...] = jnp.zeros(dq_ref.shape, F32)

        dk_sc[...] = jnp.zeros(dk_sc.shape, F32)
        dv_sc[...] = jnp.zeros(dv_sc.shape, F32)

        def tile(i, masked):
            rows = pl.ds(pl.multiple_of(i * tq, tq), tq)
            for hh in range(hb):
                lanes = slice(hh * HEAD_PAD, (hh + 1) * HEAD_PAD)
                q, do, k = q_ref[rows, lanes], do_ref[rows, lanes], k_ref[:, lanes]
                sc = _dot_nt(q, k) * ATTN_SCALE
                if masked:
                    sc = jnp.where(_diag_mask(tq), sc, NEG_BIG)
                p = jnp.exp(sc - lse_ref[hh, rows, :])
                dv_sc[hh] += _dot_tn(p.astype(MXU), do)
                dp = _dot_nt(do, v_ref[:, lanes])
                ds = ((p * (dp - dl_ref[hh, rows, :])) * ATTN_SCALE).astype(MXU)
                dk_sc[hh] += _dot_tn(ds, q)
                dq_ref[rows, lanes] += _dot(ds, k)

        def off_diagonal_pair(p, carry):
            tile(j + 1 + 2 * p, False)
            tile(j + 2 + 2 * p, False)
            return carry

        below = nq - 1 - j
        tile(j, True)
        lax.fori_loop(0, below // 2, off_diagonal_pair, 0)

        @pl.when(below % 2 == 1)
        def _():
            tile(nq - 1, False)
        for hh in range(hb):
            lanes = slice(hh * HEAD_PAD, (hh + 1) * HEAD_PAD)
            dk_ref[:, lanes] = dk_sc[hh]
            dv_ref[:, lanes] = dv_sc[hh]
        pl.when((g == groups - 1) & (j == nq - 1))(s_finish)

    seq_spec = pl.BlockSpec((s, width), lambda g, j: (0, g))
    kv_spec = pl.BlockSpec((tq, width), lambda g, j: (j, g))
    col_spec = pl.BlockSpec((hb, s, 1), lambda g, j: (g, 0, 0))
    any_spec = pl.BlockSpec(memory_space=pl.ANY)
    outs = _call(
        body, name="attn_bwd", grid=(groups, nq), sem=("arbitrary", "arbitrary"),
        in_specs=[seq_spec, kv_spec, kv_spec, seq_spec, col_spec, col_spec] + [any_spec] * ns,
        out_specs=[seq_spec, kv_spec, kv_spec] + [any_spec] * ns,
        out_shape=[_sds(qp.shape, F32), _sds(qp.shape, F32), _sds(qp.shape, F32)]
        + [_scatter_out_shape(a, slot) for a, slot in scattered],
        scratch=[pltpu.VMEM((hb, tq, HEAD_PAD), F32), pltpu.VMEM((hb, tq, HEAD_PAD), F32)] + _comm_sems(ns),
    )(qp, kp, vp, d_o, lse, delta, *[a for a, _ in scattered])
    return outs[0], outs[1], outs[2], outs[3:]


def _gmlp_bwd(z, d_sgu, ln_g, ln_b, w_sp, bias_exp, tm):
    s = z.shape[0]
    nblk = tm // GM_CHUNK

    def body(zu_ref, zv_ref, dsgu_ref, lg_ref, lb_ref, w_ref, be_ref,
             dguv_ref, dws_ref, dbs_ref, dlg_ref, dlb_ref, dbe_sc, dvln_sc, dlg_sc, dlb_sc):
        i = pl.program_id(0)

        @pl.when(i == 0)
        def _():
            for r in (dws_ref, dlg_sc, dlb_sc, dbe_sc):
                r[...] = jnp.zeros(r.shape, F32)

        seg = _seg_matrix()
        mask = _spatial_mask()
        wm = [(w_ref[h] * mask).astype(MXU) for h in range(HEADS)]
        zu, zv = zu_ref[...], zv_ref[...]
        gu = _gelu(zu)
        _, vhat, rstd = _gm_norm(zv, seg)
        lg = lg_ref[...]
        vln = (vhat * lg + lb_ref[...]).astype(MXU)
        d_sgu = dsgu_ref[...]
        for n in range(nblk):
            rows = slice(n * GM_CHUNK, (n + 1) * GM_CHUNK)
            vb = vln[rows]
            mixed = _gm_mix(wm, vb, GM_CHUNK) + be_ref[...]
            d_mixed = d_sgu[rows] * gu[rows]
            dguv_ref[rows, pl.ds(0, GM_WIDTH)] = ((d_sgu[rows] * mixed) * _gelu_grad(zu[rows])).astype(MXU)
            dbe_sc[...] += d_mixed
            dmb = d_mixed.astype(MXU)
            d_vln = jnp.zeros((GM_CHUNK, GM_WIDTH), F32)
            for h in range(HEADS):
                hm = _head_lane_mask(h, GM_CHUNK)
                dws_ref[h] += _dot_nt(jnp.where(hm, dmb, jnp.zeros_like(dmb)), vb)
                d_vln = d_vln + jnp.where(hm, _dot_tn(wm[h], dmb), 0.0)
            dvln_sc[rows, :] = d_vln
        d_vln = dvln_sc[...]
        dlg_sc[...] += jnp.sum(d_vln * vhat, axis=0, keepdims=True)
        dlb_sc[...] += jnp.sum(d_vln, axis=0, keepdims=True)
        d_vhat = d_vln * lg
        d_gv = rstd * ((d_vhat - _split_dot(d_vhat, seg)) - vhat * _split_dot(d_vhat * vhat, seg))
        dguv_ref[:, pl.ds(GM_WIDTH, GM_WIDTH)] = (d_gv * _gelu_grad(zv)).astype(MXU)

        @pl.when(i == pl.num_programs(0) - 1)
        def _():
            for h in range(HEADS):
                dws_ref[h] = dws_ref[h] * mask
            hrow = lax.broadcasted_iota(jnp.int32, (HEADS, GM_WIDTH), 0)
            hlane = lax.broadcasted_iota(jnp.int32, (HEADS, GM_WIDTH), 1) >> 6
            ind = jnp.where(hrow == hlane, 1.0, 0.0).astype(MXU)
            acc = dbe_sc[...]
            hi = acc.astype(MXU)
            lo = (acc - hi.astype(F32)).astype(MXU)
            dbs_ref[...] = _dot_nt(ind, hi) + _dot_nt(ind, lo)
            pick = (lax.broadcasted_iota(jnp.int32, (GM_WIDTH, GM_DIM), 0) & (GM_DIM - 1)
                    == lax.broadcasted_iota(jnp.int32, (GM_WIDTH, GM_DIM), 1))
            pick = jnp.where(pick, 1.0, 0.0).astype(MXU)
            for src, dst in ((dlg_sc, dlg_ref), (dlb_sc, dlb_ref)):
                spread = jnp.where(hrow == hlane, jnp.broadcast_to(src[...], (HEADS, GM_WIDTH)), 0.0)
                dst[...] = _split_dot3(spread, pick)

    return _call(
        body, name="gmlp_bwd", grid=(s // tm,), sem=("arbitrary",),
        in_specs=[_rows(tm, GM_WIDTH, 1), _rows(tm, GM_WIDTH, 2), _rows(tm, GM_WIDTH), _full(ln_g.shape),
                  _full(ln_b.shape), _full(w_sp.shape), _full(bias_exp.shape)],
        out_specs=[_rows(tm, 2 * GM_WIDTH), _full(w_sp.shape), _full((HEADS, GM_CHUNK)), _full((HEADS, GM_DIM)),
                   _full((HEADS, GM_DIM))],
        out_shape=[_sds((s, 2 * GM_WIDTH), MXU), _sds(w_sp.shape, F32), _sds((HEADS, GM_CHUNK), F32),
                   _sds((HEADS, GM_DIM), F32), _sds((HEADS, GM_DIM), F32)],
        scratch=[pltpu.VMEM((GM_CHUNK, GM_WIDTH), F32), pltpu.VMEM((tm, GM_WIDTH), F32),
                 pltpu.VMEM((1, GM_WIDTH), F32), pltpu.VMEM((1, GM_WIDTH), F32)],
    )(z, z, d_sgu, ln_g, ln_b, w_sp, bias_exp)


def _mix_in_bwd(dq, dk, dv, z, d_guv, x, d_x_part, ada_raw, ada_b, g_pre, g_q, g_kv, w1t, wqt, wkv,
                cos_t, sin_t, tm):
    s = x.shape[0]
    hp = HEADS * HEAD_PAD
    za = Q_LORA + KV_LORA + HEAD_PAD

    def body(dq_ref, dk_ref, dv_ref, z_ref, dguv_ref, x_ref, dxp_ref, ar_ref, ab_ref, g_ref, gq_ref, gkv_ref,
             w1_ref, wq_ref, wkv_ref, cos_ref, sin_ref,
             gx_ref, dza_ref, dqp_ref, dkvp_ref, dsh_ref, dsc_ref, dg_ref, dgq_ref, dgkv_ref):
        i = pl.program_id(0)

        @pl.when(i == 0)
        def _():
            for r in (dsh_ref, dsc_ref, dg_ref, dgq_ref, dgkv_ref):
                r[...] = jnp.zeros(r.shape, F32)

        cos, sin = cos_ref[...], sin_ref[...]
        d_krot = jnp.zeros((tm, HEAD_PAD), F32)
        for h in range(HEADS):
            blk = slice(h * HEAD_PAD, (h + 1) * HEAD_PAD)
            dqp_ref[:, blk] = _rope_transposed(dq_ref[:, blk], cos, sin).astype(MXU)
            dk_h = dk_ref[:, blk]
            d_krot = d_krot + dk_h
            dkvp_ref[:, blk] = dk_h.astype(MXU)
        dkvp_ref[:, pl.ds(hp, hp)] = dv_ref[...].astype(MXU)
        lane = lax.broadcasted_iota(jnp.int32, (tm, HEAD_PAD), 1)
        d_kr = jnp.where((lane >= NOPE) & (lane < NOPE + ROPE), _rope_transposed(d_krot, cos, sin), 0.0)
        d_cqn = _dot(dqp_ref[...], wq_ref[...])
        d_ckvn = _dot_nt(dkvp_ref[...], wkv_ref[...])
        zt = z_ref[...]
        gq, gkv = gq_ref[...], gkv_ref[...]
        cq_hat, rq = _rms(zt[:, :Q_LORA])
        ckv_hat, rkv = _rms(zt[:, Q_LORA:Q_LORA + KV_LORA])
        dgq_ref[...] += jnp.sum(d_cqn * cq_hat, axis=0, keepdims=True)
        dgkv_ref[...] += jnp.sum(d_ckvn * ckv_hat, axis=0, keepdims=True)
        d_cq = _rms_bwd(d_cqn * gq, cq_hat, rq)
        d_ckv = _rms_bwd(d_ckvn * gkv, ckv_hat, rkv)
        d_za = jnp.concatenate([d_cq, d_ckv, d_kr], axis=1).astype(MXU)
        dza_ref[...] = d_za
        d_h1 = _dot(d_za, w1_ref[pl.ds(0, za), :]) + _dot(dguv_ref[...], w1_ref[pl.ds(za, 2 * GM_WIDTH), :])
        sc1 = _row(ar_ref, 1) + _row(ab_ref, 1)
        g = g_ref[...]
        xn, r1 = _rms(x_ref[...])
        dsh_ref[...] += jnp.sum(d_h1, axis=0, keepdims=True)
        dsc_ref[...] += jnp.sum(d_h1 * (xn * g), axis=0, keepdims=True)
        d_mod = d_h1 * (1.0 + sc1)
        dg_ref[...] += jnp.sum(d_mod * xn, axis=0, keepdims=True)
        gx_ref[...] = dxp_ref[...] + _rms_bwd(d_mod * g, xn, r1)

    vec = pl.BlockSpec((1, D_MODEL), lambda i: (0, 0))
    return _call(
        body, name="mix_in_bwd", grid=(s // tm,), sem=("arbitrary",),
        in_specs=[_rows(tm, hp), _rows(tm, hp), _rows(tm, hp), _rows(tm, za), _rows(tm, 2 * GM_WIDTH),
                  _rows(tm, D_MODEL), _rows(tm, D_MODEL), _full(ada_raw.shape), _full(ada_b.shape), _full(g_pre.shape),
                  _full(g_q.shape), _full(g_kv.shape), _full(w1t.shape), _full(wqt.shape),
                  _full(wkv.shape), _rows(tm, HEAD_PAD), _rows(tm, HEAD_PAD)],
        out_specs=[_rows(tm, D_MODEL), _rows(tm, za), _rows(tm, hp), _rows(tm, 2 * hp), vec, vec, vec,
                   _full(g_q.shape), _full(g_kv.shape)],
        out_shape=[_sds((s, D_MODEL), F32), _sds((s, za), MXU), _sds((s, hp), MXU), _sds((s, 2 * hp), MXU),
                   _sds((1, D_MODEL), F32), _sds((1, D_MODEL), F32), _sds((1, D_MODEL), F32),
                   _sds(g_q.shape, F32), _sds(g_kv.shape, F32)],
    )(dq, dk, dv, z, d_guv, x, d_x_part, ada_raw, ada_b, g_pre, g_q, g_kv, w1t, wqt, wkv, cos_t, sin_t)


def _tn_matmul(a, b, name, ts):
    ga, s, m = a.shape
    gb, _, n = b.shape
    g = max(ga, gb)
    tn = n if n <= 1024 else 1024
    steps = s // ts

    def body(a_ref, b_ref, o_ref, acc):
        k = pl.program_id(2)

        @pl.when(k == 0)
        def _():
            acc[...] = jnp.zeros(acc.shape, F32)

        acc[...] += _dot_tn(a_ref[0], b_ref[0])

        @pl.when(k == steps - 1)
        def _():
            o_ref[0] = acc[...].astype(MXU)

    return _call(
        body, name=name, grid=(g, n // tn, steps), sem=("parallel", "parallel", "arbitrary"),
        in_specs=[pl.BlockSpec((1, ts, m), lambda gi, ni, k: (gi if ga > 1 else 0, k, 0)),
                  pl.BlockSpec((1, ts, tn), lambda gi, ni, k: (gi if gb > 1 else 0, k, ni))],
        out_specs=pl.BlockSpec((1, m, tn), lambda gi, ni, k: (gi, 0, ni)),
        out_shape=_sds((g, m, n), MXU),
        scratch=[pltpu.VMEM((m, tn), F32)],
    )(a, b)


def _adamw(w, g, m, v):
    m2 = ADAM_B1 * m + (1.0 - ADAM_B1) * g
    v2 = ADAM_B2 * v + (1.0 - ADAM_B2) * (g * g)
    m_hat = m2 / (1.0 - ADAM_B1 ** ADAM_STEP)
    v_hat = v2 / (1.0 - ADAM_B2 ** ADAM_STEP)
    delta = -ADAM_LR * (m_hat / (jnp.sqrt(v_hat) + ADAM_EPS) + ADAM_WD * w)
    return delta, m2, v2


def _adam_reduce(recv, w, m, v, name):
    r, c = w.shape
    tr = r if r <= 512 else max(t for t in range(16, 513, 16) if r % t == 0)

    def body(p_ref, w_ref, m_ref, v_ref, g_ref, d_ref, mo_ref, vo_ref):
        g = p_ref[0].astype(F32)
        for j in range(1, N_DEV):
            g = g + p_ref[j].astype(F32)
        g_ref[...] = g
        d_ref[...], mo_ref[...], vo_ref[...] = _adamw(w_ref[...], g, m_ref[...], v_ref[...])

    blk = pl.BlockSpec((tr, c), lambda i: (i, 0))
    return _call(
        body, name=name, grid=(r // tr,), sem=("parallel",),
        in_specs=[pl.BlockSpec((N_DEV, tr, c), lambda i: (0, i, 0)), blk, blk, blk],
        out_specs=[blk] * 4, out_shape=[_sds((r, c), F32)] * 4,
    )(recv, w, m, v)


def _adam_w_ada(c_act_t, d_ada_cols, w, m, v):
    r, c = w.shape
    tr = 256

    def body(ct_ref, da_ref, w_ref, m_ref, v_ref, g_ref, d_ref, mo_ref, vo_ref):
        g = ct_ref[:, pl.ds(0, 1)] * da_ref[pl.ds(0, 1), :]
        for b in range(1, N_DEV):
            g = g + ct_ref[:, pl.ds(b, 1)] * da_ref[pl.ds(b, 1), :]
        g_ref[...] = g
        d_ref[...], mo_ref[...], vo_ref[...] = _adamw(w_ref[...], g, m_ref[...], v_ref[...])

    blk = pl.BlockSpec((tr, c), lambda i: (i, 0))
    return _call(
        body, name="adam_w_ada", grid=(r // tr,), sem=("parallel",),
        in_specs=[pl.BlockSpec((tr, N_DEV), lambda i: (i, 0)), _full(d_ada_cols.shape), blk, blk, blk],
        out_specs=[blk] * 4, out_shape=[_sds((r, c), F32)] * 4,
    )(c_act_t, d_ada_cols, w, m, v)


VEC_ROWS = D_MODEL // 128
PK_ADA = 0
PK_GAIN = PK_ADA + 6 * VEC_ROWS
PK_GQ = PK_GAIN + 4 * VEC_ROWS
PK_GKV = PK_GQ + Q_LORA // 128
PK_LOSS = PK_GKV + KV_LORA // 128
PK_LNG = 88
PK_LNB = PK_LNG + HEADS
PK_BS = PK_LNB + HEADS
PK_CB = PK_BS + HEADS
CB_ROWS = 6
PK_WS = PK_CB + N_DEV * CB_ROWS
PK_ROWS = PK_WS + HEADS * GM_CHUNK
assert PK_LOSS < PK_LNG and PK_ROWS % 8 == 0


def _cb_chunks():
    return [(k, k * 128, min(128, FF_BLK - k * 128)) for k in range(CB_ROWS)]


def _pack_small(vectors, d_g_q, d_g_kv, loss_part, d_ln_g, d_ln_b, d_bs, d_cb, d_ws):
    half = N_DEV // 2

    def body(*refs):
        vec_refs = refs[:10]
        gq_ref, gkv_ref, loss_ref, lng_ref, lnb_ref, bs_ref, cb_ref, ws_ref, out_ref = refs[10:]
        out_ref[pl.ds(0, PK_WS), :] = jnp.zeros((PK_WS, 128), F32)

        def put(row0, ref, width):
            for k in range(width // 128):
                out_ref[pl.ds(row0 + k, 1), :] = ref[:, pl.ds(k * 128, 128)]

        for n, ref in enumerate(vec_refs):
            put(PK_ADA + n * VEC_ROWS, ref, D_MODEL)
        put(PK_GQ, gq_ref, Q_LORA)
        put(PK_GKV, gkv_ref, KV_LORA)
        put(PK_LOSS, loss_ref, 128)
        out_ref[pl.ds(PK_LNG, HEADS), pl.ds(0, GM_DIM)] = lng_ref[...]
        out_ref[pl.ds(PK_LNB, HEADS), pl.ds(0, GM_DIM)] = lnb_ref[...]
        out_ref[pl.ds(PK_BS, HEADS), :] = bs_ref[...]
        for j in range(N_DEV):
            for k, lane, width in _cb_chunks():
                out_ref[pl.ds(PK_CB + j * CB_ROWS + k, 1), pl.ds(0, width)] = cb_ref[j % half, j // half, :, pl.ds(lane, width)]
        for h in range(HEADS):
            out_ref[pl.ds(PK_WS + h * GM_CHUNK, GM_CHUNK), :] = ws_ref[h]

    ins = list(vectors) + [d_g_q, d_g_kv, loss_part, d_ln_g, d_ln_b, d_bs, d_cb, d_ws]
    return _call(body, name="pack_small", grid=(1,), in_specs=[_full(a.shape) for a in ins],
                 out_specs=_full((PK_ROWS, 128)), out_shape=_sds((PK_ROWS, 128), F32))(*ins)


def _adam_small(gathered, params):
    n_par = len(params)

    def body(p_ref, *refs):
        ins = [refs[3 * n:3 * n + 3] for n in range(n_par)]
        outs = [refs[3 * n_par + 4 * n:3 * n_par + 4 * n + 4] for n in range(n_par)]
        loss_ref, dada_ref = refs[7 * n_par:]

        def total(rows, lanes=slice(None)):
            g = p_ref[0, rows, lanes]
            for j in range(1, N_DEV):
                g = g + p_ref[j, rows, lanes]
            return g

        def apply(n, g, idx):
            w_ref, m_ref, v_ref = ins[n]
            d, m2, v2 = _adamw(w_ref[idx], g, m_ref[idx], v_ref[idx])
            for ref, val in zip(outs[n], (g, d, m2, v2)):
                ref[idx] = val

        def vector(n, row0, width):
            for k in range(width // 128):
                apply(n, total(pl.ds(row0 + k, 1)), (slice(None), pl.ds(k * 128, 128)))

        vector(0, PK_ADA, 6 * D_MODEL)
        for n in range(4):
            vector(1 + n, PK_GAIN + n * VEC_ROWS, D_MODEL)
        vector(5, PK_GQ, Q_LORA)
        vector(6, PK_GKV, KV_LORA)
        apply(7, total(pl.ds(PK_LNG, HEADS), pl.ds(0, GM_DIM)), (0,))
        apply(8, total(pl.ds(PK_LNB, HEADS), pl.ds(0, GM_DIM)), (0,))
        for h in range(HEADS):
            apply(9, total(pl.ds(PK_WS + h * GM_CHUNK, GM_CHUNK)), (0, h))
        apply(10, total(pl.ds(PK_BS, HEADS)), (0,))
        for j in range(N_DEV):
            for k, lane, width in _cb_chunks():
                apply(11, total(pl.ds(PK_CB + j * CB_ROWS + k, 1), pl.ds(0, width)), (pl.ds(j, 1), pl.ds(lane, width)))
        loss_ref[...] = total(pl.ds(PK_LOSS, 1))
        dada_ref[...] = p_ref[:, pl.ds(PK_ADA, 6 * VEC_ROWS), :]

    flat = [a for triple in params for a in triple]
    out_shape = [_sds(w.shape, F32) for w, _, _ in params for _ in range(4)]
    out_shape += [_sds((1, 128), F32), _sds((N_DEV, 6 * VEC_ROWS, 128), F32)]
    outs = _call(body, name="adam_small", grid=(1,),
                 in_specs=[_full(gathered.shape)] + [_full(a.shape) for a in flat],
                 out_specs=[_full(o.shape) for o in out_shape], out_shape=out_shape)(gathered, *flat)
    return [tuple(outs[4 * n:4 * n + 4]) for n in range(n_par)], outs[-2], outs[-1]


def _rope_tables(s):
    pos = jnp.arange(s, dtype=F32)
    inv = ROPE_THETA ** (-jnp.arange(0, ROPE, 2, dtype=F32) / ROPE)
    ang = pos[:, None] * inv[None, :]
    cos, sin = jnp.cos(ang), jnp.sin(ang)
    ones, zeros = jnp.ones((s, NOPE), F32), jnp.zeros((s, NOPE), F32)
    cos_t = jnp.concatenate([ones, cos, cos, ones[:, :HEAD_PAD - NOPE - ROPE]], axis=1)
    sin_t = jnp.concatenate([zeros, sin, sin, zeros[:, :HEAD_PAD - NOPE - ROPE]], axis=1)
    return cos_t, sin_t


def kernel(x, c, w_ada, b_ada, g_pre_mix, g_post_mix, w_in, g_q, w_uq, g_kv, w_ukv, gm_ln_g, gm_ln_b, w_spatial, b_spatial, w_out, g_pre_ffn, g_post_ffn, w_up, conv_w, conv_b, w_down, loss_target, m_w_ada, m_b_ada, m_g_pre_mix, m_g_post_mix, m_w_in, m_g_q, m_w_uq, m_g_kv, m_w_ukv, m_gm_ln_g, m_gm_ln_b, m_w_spatial, m_b_spatial, m_w_out, m_g_pre_ffn, m_g_post_ffn, m_w_up, m_conv_w, m_conv_b, m_w_down, v_w_ada, v_b_ada, v_g_pre_mix, v_g_post_mix, v_w_in, v_g_q, v_w_uq, v_g_kv, v_w_ukv, v_gm_ln_g, v_gm_ln_b, v_w_spatial, v_b_spatial, v_w_out, v_g_pre_ffn, v_g_post_ffn, v_w_up, v_conv_w, v_conv_b, v_w_down):
    s = x.shape[1]
    tm = min(256, s)
    tf = min(2 * ROW_SUB, s)
    tq = min(512, s)
    ts = min(2048, s)
    hp = HEADS * HEAD_PAD
    half = N_DEV // 2
    my_slot = 4 * lax.axis_index("x") + 2 * lax.axis_index("y") + lax.axis_index("c")
    x2d, target = x[0], loss_target[0]

    def t_(a):
        return jnp.swapaxes(a[0], 0, 1)

    w_in_t, m_in_t, v_in_t = t_(w_in), t_(m_w_in), t_(v_w_in)
    w_uq_t, m_uq_t, v_uq_t = t_(w_uq), t_(m_w_uq), t_(v_w_uq)
    w_up_t, m_up_t, v_up_t = t_(w_up), t_(m_w_up), t_(v_w_up)
    (g_c, g_in_t, g_uq_t, g_ukv, g_cw), _ = _exchange(
        [c, w_in_t.astype(MXU), w_uq_t.astype(MXU), w_ukv[0].astype(MXU), conv_w[0]], [], "gather_mixer_weights")

    w_in_f = g_in_t.reshape(-1, D_MODEL)
    o1, o2, o3 = Q_LORA, Q_LORA + KV_LORA, Q_LORA + KV_LORA + ROPE
    w1t = jnp.concatenate([w_in_f[:o2], jnp.zeros((NOPE, D_MODEL), MXU), w_in_f[o2:o3],
                           jnp.zeros((HEAD_PAD - NOPE - ROPE, D_MODEL), MXU), w_in_f[o3:]], axis=0)
    wqt = jnp.pad(g_uq_t, ((0, 0), (0, HEAD_PAD - NOPE - ROPE), (0, 0))).reshape(hp, Q_LORA)
    w_ukv_f = jnp.transpose(g_ukv, (1, 0, 2)).reshape(KV_LORA, HEADS, 2 * NOPE)
    pad_head = ((0, 0), (0, 0), (0, HEAD_PAD - NOPE))
    wkv = jnp.concatenate([jnp.pad(w_ukv_f[:, :, :NOPE], pad_head).reshape(KV_LORA, hp),
                           jnp.pad(w_ukv_f[:, :, NOPE:], pad_head).reshape(KV_LORA, hp)], axis=1)
    cb8 = conv_b.reshape(N_DEV, 1, FF_BLK)
    bias_exp = jnp.repeat(b_spatial[0].T, GM_DIM, axis=1)
    ln_g, ln_b = gm_ln_g.reshape(1, GM_WIDTH), gm_ln_b.reshape(1, GM_WIDTH)
    w_sp = w_spatial[0]
    cos_t, sin_t = _rope_tables(s)

    ada_part, c_act = _ada_fwd(g_c.reshape(N_DEV, D_MODEL), w_ada[0])
    _, (ada_recv,) = _exchange([], [(ada_part.reshape(N_DEV, 1, -1), _plain_slot)], "ada_rows")
    ada_raw = ada_recv.reshape(6, D_MODEL)
    ada_b = b_ada.reshape(6, D_MODEL)

    h1, z, qp, kp, vp, cqn, ckvn = _mix_in_fwd(x2d, ada_raw, ada_b, g_pre_mix, w1t, g_q, g_kv, wqt, wkv, cos_t, sin_t, tm)
    sgu = _gmlp_fwd(z, ln_g, ln_b, w_sp, bias_exp, tm)
    o_pad, lse, (g_out, g_up, g_down) = _attn_fwd(
        qp, kp, vp, tq, [w_out[0].astype(MXU), w_up_t.astype(MXU), w_down[0].astype(MXU)])
    w_out_f = g_out.reshape(2 * GM_WIDTH, D_MODEL)
    wo_attn = jnp.pad(w_out_f[:GM_WIDTH].reshape(HEADS, NOPE, D_MODEL), ((0, 0), (0, HEAD_PAD - NOPE), (0, 0)))
    wo = jnp.concatenate([wo_attn.reshape(hp, D_MODEL), w_out_f[GM_WIDTH:]], axis=0)
    wd = g_down.reshape(half, FF_BLK, D_MODEL)
    m_mix, x2, h2 = _out_proj_fwd(o_pad, sgu, wo, x2d, ada_raw, ada_b, g_post_mix, g_pre_ffn, tm)
    up_a, up_b, y_a, y_b, act = _ffn_up_fwd(h2, g_up, g_cw, cb8, tf)
    d_out, d_f, loss_part, d_gt2, d_g_post_ffn = _ffn_down_fwd(act, wd, x2, target, ada_raw, ada_b, g_post_ffn, tf)

    d_up, d_cw, d_cb = _ffn_down_bwd(d_f, wd, up_a, up_b, y_a, y_b, g_cw, tf)
    d_x2, d_m, d_sh2, d_sc2, d_g_pre_ffn, d_gt1, d_g_post_mix = _ffn_up_bwd(
        d_up, g_up, x2, m_mix, d_out, ada_raw, ada_b, g_pre_ffn, g_post_mix, tm)
    p_down = _tn_matmul(act, d_f[None], "dw_down", ts).reshape(N_DEV, -1, D_MODEL)
    p_up = _tn_matmul(d_up.reshape(N_DEV, s, FF_BLK), h2[None], "dw_up", ts).reshape(half, 2, FF_BLK, D_MODEL)
    d_m3 = d_m[None]
    dwo_attn = _tn_matmul(o_pad[None], d_m3, "dw_out_attn", ts)[0].reshape(HEADS, HEAD_PAD, D_MODEL)[:, :NOPE]
    dwo_sgu = _tn_matmul(sgu[None], d_m3, "dw_out_sgu", ts)[0]
    p_out = jnp.concatenate([dwo_attn.reshape(GM_WIDTH, D_MODEL), dwo_sgu], axis=0).reshape(N_DEV, -1, D_MODEL)
    d_o, d_sgu, delta = _out_proj_bwd(d_m, wo, o_pad, tm)
    def ffn_slot(j):
        return (j % half, j // half)

    dq, dk, dv, (r_out, r_up, r_down, r_cw) = _attn_bwd(
        qp, kp, vp, d_o, lse, delta, tq,
        [(p_out, _plain_slot), (p_up, ffn_slot), (p_down, _plain_slot), (d_cw, ffn_slot)])
    d_guv, d_ws, d_bs, d_ln_g, d_ln_b = _gmlp_bwd(z, d_sgu, ln_g, ln_b, w_sp, bias_exp, tm)
    za = Q_LORA + KV_LORA + HEAD_PAD
    grad_x, d_za, d_qp, d_kvp, d_sh1, d_sc1, d_g_pre_mix, d_g_q, d_g_kv = _mix_in_bwd(
        dq, dk, dv, z, d_guv, x2d, d_x2, ada_raw, ada_b, g_pre_mix, g_q, g_kv, w1t, wqt, wkv, cos_t, sin_t, tm)
    h1_3 = h1[None]
    dw1a = _tn_matmul(d_za[None], h1_3, "dw_in_a", ts)[0]
    dw1b = _tn_matmul(d_guv[None], h1_3, "dw_in_b", ts)[0]
    d_w_in_t = jnp.concatenate([dw1a[:o2], dw1a[o2 + NOPE:o2 + NOPE + ROPE], dw1b], axis=0)
    p_in = d_w_in_t.reshape(N_DEV, -1, D_MODEL)
    p_uq = _tn_matmul(d_qp[None], cqn[None], "dw_uq", ts)[0].reshape(HEADS, HEAD_PAD, Q_LORA)[:, :NOPE + ROPE]
    dwkv = _tn_matmul(ckvn[None], d_kvp[None], "dw_ukv", ts)[0]
    dwk = dwkv[:, :hp].reshape(KV_LORA, HEADS, HEAD_PAD)[:, :, :NOPE]
    dwv = dwkv[:, hp:].reshape(KV_LORA, HEADS, HEAD_PAD)[:, :, :NOPE]
    p_ukv = jnp.transpose(jnp.concatenate([dwk, dwv], axis=2), (1, 0, 2))

    vectors = [d_sh1, d_sc1, d_gt1, d_sh2, d_sc2, d_gt2, d_g_pre_mix, d_g_post_mix, d_g_pre_ffn, d_g_post_ffn]
    packed = _pack_small(vectors, d_g_q, d_g_kv, loss_part, d_ln_g, d_ln_b, d_bs, d_cb, d_ws)
    (g_small,), (r_in, r_uq, r_ukv) = _exchange(
        [packed], [(p_in, _plain_slot), (p_uq, _plain_slot), (p_ukv, _plain_slot)], "final_exchange")
    small_params = [(b_ada, m_b_ada, v_b_ada), (g_pre_mix, m_g_pre_mix, v_g_pre_mix),
                    (g_post_mix, m_g_post_mix, v_g_post_mix), (g_pre_ffn, m_g_pre_ffn, v_g_pre_ffn),
                    (g_post_ffn, m_g_post_ffn, v_g_post_ffn), (g_q, m_g_q, v_g_q), (g_kv, m_g_kv, v_g_kv),
                    (gm_ln_g, m_gm_ln_g, v_gm_ln_g), (gm_ln_b, m_gm_ln_b, v_gm_ln_b),
                    (w_spatial, m_w_spatial, v_w_spatial), (b_spatial, m_b_spatial, v_b_spatial),
                    tuple(a.reshape(N_DEV, FF_BLK) for a in (conv_b, m_conv_b, v_conv_b))]
    small_out, loss_row, d_ada_all = _adam_small(g_small, small_params)
    small_out[11] = tuple(o.reshape(conv_b.shape) for o in small_out[11])
    loss = loss_row[0, 0]

    def big(recv, w, m, v, name):
        g, d, m2, v2 = _adam_reduce(recv, w[0], m[0], v[0], name)
        return g[None], d[None], m2[None], v2[None]

    def big_t(recv, w_t, m_t, v_t, name):
        return tuple(jnp.swapaxes(o, 0, 1)[None] for o in _adam_reduce(recv, w_t, m_t, v_t, name))

    a_in = big_t(r_in, w_in_t, m_in_t, v_in_t, "adam_w_in")
    a_uq = big_t(r_uq, w_uq_t, m_uq_t, v_uq_t, "adam_w_uq")
    a_ukv = big(r_ukv, w_ukv, m_w_ukv, v_w_ukv, "adam_w_ukv")
    a_out = big(r_out, w_out, m_w_out, v_w_out, "adam_w_out")
    a_up = big_t(r_up, w_up_t, m_up_t, v_up_t, "adam_w_up")
    a_down = big(r_down, w_down, m_w_down, v_w_down, "adam_w_down")
    ada_cols = w_ada.shape[2]
    d_ada_cols = lax.dynamic_slice(d_ada_all.reshape(N_DEV, 6 * D_MODEL), (0, my_slot * ada_cols), (N_DEV, ada_cols))
    a_ada = tuple(t[None] for t in _adam_w_ada(c_act.T, d_ada_cols, w_ada[0], m_w_ada[0], v_w_ada[0]))
    a_cw = big(r_cw, conv_w, m_conv_w, v_conv_w, "adam_conv_w")

    def small(k):
        return small_out[k]

    per_weight = [a_ada, small(0), small(1), small(2), a_in, small(5), a_uq, small(6), a_ukv, small(7), small(8),
                  small(9), small(10), a_out, small(3), small(4), a_up, a_cw, small(11), a_down]
    outs = [loss, grad_x[None]]
    for k in range(4):
        outs += [t[k] for t in per_weight]
    return tuple(outs)
```

```python
import functools

import jax
import jax.numpy as jnp
from jax import lax
from jax.experimental import pallas as pl
from jax.experimental.pallas import tpu as pltpu

F32 = jnp.float32
MXU = jnp.bfloat16

N_DEV = 8
D_MODEL = 1024
HEADS = 8
HEAD_PAD = 128
NOPE = 64
ROPE = 32
Q_LORA = 256
KV_LORA = 128
GM_WIDTH = 512
GM_DIM = 64
GM_CHUNK = 128
CHUNK_SHIFT = 6
ROPE_THETA = 10000.0
ATTN_SCALE = (NOPE + ROPE) ** -0.5
LOG2E = 1.4426950408889634
SCALE_LOG2E = ATTN_SCALE * LOG2E
Z_COLS = 1536
FF_BLK = 704
EPS = 1e-6
ADAM_LR = 0.001
ADAM_B1 = 0.9
ADAM_B2 = 0.999
ADAM_EPS = 1e-08
ADAM_WD = 0.01
ADAM_STEP = 10
VMEM_LIMIT = 56 * 1024 * 1024
MESH = pl.DeviceIdType.MESH


def _dot(a, b):
    return jnp.dot(a, b, preferred_element_type=F32)


def _dot_nt(a, b):
    return lax.dot_general(a, b, (((1,), (1,)), ((), ())), preferred_element_type=F32)


def _dot_tn(a, b):
    return lax.dot_general(a, b, (((0,), (0,)), ((), ())), preferred_element_type=F32)


def _call(body, *, name, grid, in_specs, out_specs, out_shape, scratch=(), sem=None):
    params = pltpu.CompilerParams(dimension_semantics=sem, vmem_limit_bytes=VMEM_LIMIT)
    return pl.pallas_call(body, name=name, grid=grid, in_specs=in_specs, out_specs=out_specs,
                          out_shape=out_shape, scratch_shapes=list(scratch), compiler_params=params)


def _full(shape):
    n = len(shape)
    return pl.BlockSpec(shape, lambda *_: (0,) * n)


def _rows(tm, cols, col_block=0):
    return pl.BlockSpec((tm, cols), lambda i: (i, col_block))


def _sds(shape, dtype):
    return jax.ShapeDtypeStruct(shape, dtype)


def _row(ref, k):
    return ref[pl.ds(k, 1), :]


def _rms(x):
    r = lax.rsqrt(jnp.mean(x * x, axis=-1, keepdims=True) + EPS)
    return x * r, r


def _rms_bwd(d_hat, hat, r):
    return r * (d_hat - hat * jnp.mean(d_hat * hat, axis=-1, keepdims=True))


def _rope_partner(t):
    lane = lax.broadcasted_iota(jnp.int32, t.shape, 1)
    swapped = jnp.where(lane < NOPE + ROPE // 2, -pltpu.roll(t, HEAD_PAD - ROPE // 2, 1), pltpu.roll(t, ROPE // 2, 1))
    return jnp.where((lane >= NOPE) & (lane < NOPE + ROPE), swapped, 0.0)


def _rope(t, cos, sin):
    return t * cos + _rope_partner(t) * sin


def _rope_transposed(g, cos, sin):
    return g * cos - _rope_partner(g * sin)


def _gelu(x):
    return x * (0.5 * (1.0 + jnp.tanh(0.7978845608028654 * (x + 0.044715 * (x * x * x)))))


def _gelu_grad(x):
    t = jnp.tanh(0.7978845608028654 * (x + 0.044715 * (x * x * x)))
    return 0.5 * (1.0 + t) + 0.5 * x * (1.0 - t * t) * (0.7978845608028654 * (1.0 + 3.0 * 0.044715 * (x * x)))


def _split_dot(x, mat):
    hi = x.astype(MXU)
    lo = (x - hi.astype(F32)).astype(MXU)
    return _dot(hi, mat) + _dot(lo, mat)


def _split_dot3(x, mat):
    hi = x.astype(MXU)
    r1 = x - hi.astype(F32)
    mid = r1.astype(MXU)
    lo = (r1 - mid.astype(F32)).astype(MXU)
    return (_dot(hi, mat) + _dot(mid, mat)) + _dot(lo, mat)


def _seg_matrix():
    r = lax.broadcasted_iota(jnp.int32, (GM_WIDTH, GM_WIDTH), 0) >> 6
    c = lax.broadcasted_iota(jnp.int32, (GM_WIDTH, GM_WIDTH), 1) >> 6
    return jnp.where(r == c, 1.0 / GM_DIM, 0.0).astype(MXU)


def _spatial_mask():
    i = lax.broadcasted_iota(jnp.int32, (GM_CHUNK, GM_CHUNK), 0) >> CHUNK_SHIFT
    j = lax.broadcasted_iota(jnp.int32, (GM_CHUNK, GM_CHUNK), 1) >> CHUNK_SHIFT
    return (j <= i).astype(F32)


def _head_lane_mask(h, rows):
    lane = lax.broadcasted_iota(jnp.int32, (rows, GM_WIDTH), 1) >> 6
    return lane == h


def _my_place():
    return lax.axis_index("x"), lax.axis_index("y"), lax.axis_index("c")


def _flat(p):
    return 4 * p[0] + 2 * p[1] + p[2]


def _comm_sems(n):
    return [pltpu.SemaphoreType.DMA((7 * n,)), pltpu.SemaphoreType.DMA((7 * n,)), pltpu.SemaphoreType.DMA((n,))]


def _gather_steps(ins, outs, sems):
    send_sems, recv_sems, local_sems = sems
    n = len(ins)
    x, y, c = _my_place()
    me, sibling = (x, y, c), (x, y, 1 - c)
    chips = [(1 - x, y), (x, 1 - y), (1 - x, 1 - y)]

    def copy(a, k, block, to, src=None):
        slot = outs[a].at[_flat(block)]
        return pltpu.make_async_remote_copy(
            src_ref=slot if src is None else src, dst_ref=slot,
            send_sem=send_sems.at[7 * a + k], recv_sem=recv_sems.at[7 * a + k],
            device_id=to, device_id_type=MESH)

    def mine():
        return [pltpu.make_async_copy(ins[a], outs[a].at[_flat(me)], local_sems.at[a]) for a in range(n)]

    def first():
        cps = []
        for a in range(n):
            cps.append(copy(a, 0, me, sibling, src=ins[a]))
            cps += [copy(a, 1 + j, me, (*chip, c), src=ins[a]) for j, chip in enumerate(chips)]
        return cps

    def passed():
        return [copy(a, 4 + j, (*chip, c), sibling) for a in range(n) for j, chip in enumerate(chips)]

    def start():
        for cp in mine() + first():
            cp.start()

    def forward():
        for a in range(n):
            for j, chip in enumerate(chips):
                copy(a, 1 + j, (*chip, c), me).wait_recv()
                copy(a, 4 + j, (*chip, c), sibling).start()

    def finish():
        for a in range(n):
            copy(a, 0, sibling, me).wait_recv()
            for j, chip in enumerate(chips):
                copy(a, 4 + j, (*chip, 1 - c), me).wait_recv()
        for cp in first() + passed():
            cp.wait_send()
        for cp in mine():
            cp.wait()

    return start, forward, finish


def _scatter_steps(ins, outs, sems, slots):
    send_sems, recv_sems, local_sems = sems
    n = len(ins)
    flips = [(fx, fy, fc) for fx in (0, 1) for fy in (0, 1) for fc in (0, 1)][1:]
    me = _my_place()

    def peer(f):
        return tuple(1 - v if b else v for v, b in zip(me, f))

    def copy(a, k, arriving=False):
        p = peer(flips[k])
        return pltpu.make_async_remote_copy(
            src_ref=ins[a].at[slots[a](_flat(p))], dst_ref=outs[a].at[_flat(p if arriving else me)],
            send_sem=send_sems.at[7 * a + k], recv_sem=recv_sems.at[7 * a + k],
            device_id=p, device_id_type=MESH)

    def mine():
        return [pltpu.make_async_copy(ins[a].at[slots[a](_flat(me))], outs[a].at[_flat(me)], local_sems.at[a])
                for a in range(n)]

    def start():
        for cp in mine() + [copy(a, k) for a in range(n) for k in range(7)]:
            cp.start()

    def finish():
        for a in range(n):
            for k in range(7):
                copy(a, k, arriving=True).wait_recv()
        for a in range(n):
            for k in range(7):
                copy(a, k).wait_send()
        for cp in mine():
            cp.wait()

    return start, finish


def _plain_slot(j):
    return (j,)


def _scatter_out_shape(arr, slot):
    return _sds((N_DEV,) + arr.shape[len(slot(0)):], arr.dtype)


def _exchange(gathered, scattered, name):
    ng, ns = len(gathered), len(scattered)
    slots = [slot for _, slot in scattered]

    def body(*refs):
        g_in, s_in = refs[:ng], refs[ng:ng + ns]
        g_out, s_out = refs[ng + ns:2 * ng + ns], refs[2 * ng + ns:2 * (ng + ns)]
        sems = refs[2 * (ng + ns):]
        g_start, g_forward, g_finish = _gather_steps(g_in, g_out, sems[:3])
        s_start, s_finish = _scatter_steps(s_in, s_out, sems[3:], slots)
        g_start()
        s_start()
        g_forward()
        g_finish()
        s_finish()

    any_spec = pl.BlockSpec(memory_space=pl.ANY)
    outs = pl.pallas_call(
        body, name=name,
        in_specs=[any_spec] * (ng + ns), out_specs=[any_spec] * (ng + ns),
        out_shape=[_sds((N_DEV,) + a.shape, a.dtype) for a in gathered]
        + [_scatter_out_shape(a, slot) for a, slot in scattered],
        scratch_shapes=_comm_sems(max(ng, 1)) + _comm_sems(max(ns, 1)),
    )(*gathered, *[a for a, _ in scattered])
    return outs[:ng], outs[ng:]


def _ada_fwd(c_all, w_ada):
    def body(c_ref, w_ref, part_ref, act_ref):
        cv = c_ref[...]
        act = cv * jax.nn.sigmoid(cv)
        act_ref[...] = act
        part_ref[...] = _dot(act.astype(MXU), w_ref[...].astype(MXU))

    cols = w_ada.shape[1]
    return _call(body, name="ada_fwd", grid=(1,),
                 in_specs=[_full(c_all.shape), _full(w_ada.shape)],
                 out_specs=[_full((N_DEV, cols)), _full(c_all.shape)],
                 out_shape=[_sds((N_DEV, cols), F32), _sds(c_all.shape, F32)])(c_all, w_ada)


def _mix_in_fwd(x, ada_raw, ada_b, g_pre, w1, g_q, g_kv, wq, wkv, cos_t, sin_t, tm):
    s = x.shape[0]

    def body(x_ref, ar_ref, ab_ref, g_ref, w1_ref, gq_ref, gkv_ref, wq_ref, wkv_ref, cos_ref, sin_ref,
             h1_ref, z_ref, qp_ref, kp_ref, vp_ref, cqn_ref, ckvn_ref):
        sh = _row(ar_ref, 0) + _row(ab_ref, 0)
        sc = _row(ar_ref, 1) + _row(ab_ref, 1)
        xn, _ = _rms(x_ref[...])
        hb = ((xn * g_ref[...]) * (1.0 + sc) + sh).astype(MXU)
        h1_ref[...] = hb
        z = _dot_nt(hb, w1_ref[...])
        z_ref[...] = z
        cos, sin = cos_ref[...], sin_ref[...]
        cqn = (_rms(z[:, :Q_LORA])[0] * gq_ref[...]).astype(MXU)
        ckvn = (_rms(z[:, Q_LORA:Q_LORA + KV_LORA])[0] * gkv_ref[...]).astype(MXU)
        cqn_ref[...] = cqn
        ckvn_ref[...] = ckvn
        q = _dot_nt(cqn, wq_ref[...])
        kv = _dot(ckvn, wkv_ref[...])
        k_rope = _rope(z[:, Q_LORA + KV_LORA:Q_LORA + KV_LORA + HEAD_PAD], cos, sin)
        for h in range(HEADS):
            blk = slice(h * HEAD_PAD, (h + 1) * HEAD_PAD)
            qp_ref[:, blk] = _rope(q[:, blk], cos, sin).astype(MXU)
            kp_ref[:, blk] = (kv[:, blk] + k_rope).astype(MXU)
        v_lane = lax.broadcasted_iota(jnp.int32, (tm, HEADS * HEAD_PAD), 1) & (HEAD_PAD - 1)
        vp_ref[...] = jnp.where(v_lane == NOPE, 1.0, kv[:, HEADS * HEAD_PAD:]).astype(MXU)

    hp = HEADS * HEAD_PAD
    return _call(
        body, name="mix_in_fwd", grid=(s // tm,), sem=("parallel",),
        in_specs=[_rows(tm, D_MODEL), _full(ada_raw.shape), _full(ada_b.shape), _full(g_pre.shape), _full(w1.shape),
                  _full(g_q.shape), _full(g_kv.shape), _full(wq.shape), _full(wkv.shape),
                  _rows(tm, HEAD_PAD), _rows(tm, HEAD_PAD)],
        out_specs=[_rows(tm, D_MODEL), _rows(tm, Z_COLS), _rows(tm, hp), _rows(tm, hp), _rows(tm, hp),
                   _rows(tm, Q_LORA), _rows(tm, KV_LORA)],
        out_shape=[_sds((s, D_MODEL), MXU), _sds((s, Z_COLS), F32), _sds((s, hp), MXU), _sds((s, hp), MXU),
                   _sds((s, hp), MXU), _sds((s, Q_LORA), MXU), _sds((s, KV_LORA), MXU)],
    )(x, ada_raw, ada_b, g_pre, w1, g_q, g_kv, wq, wkv, cos_t, sin_t)


def _gm_norm(zv, seg):
    gv = _gelu(zv)
    cen = gv - _split_dot(gv, seg)
    rstd = lax.rsqrt(_split_dot(cen * cen, seg) + EPS)
    return gv, cen * rstd, rstd


def _gm_mix(wm, vb, rows):
    out = jnp.zeros((rows, GM_WIDTH), F32)
    for h in range(HEADS):
        out = out + jnp.where(_head_lane_mask(h, rows), _dot(wm[h], vb), 0.0)
    return out


def _gmlp_fwd(z, ln_g, ln_b, w_sp, bias_exp, tm):
    s = z.shape[0]
    nblk = tm // GM_CHUNK

    def body(zu_ref, zv_ref, lg_ref, lb_ref, w_ref, be_ref, sgu_ref):
        seg = _seg_matrix()
        mask = _spatial_mask()
        wm = [(w_ref[h] * mask).astype(MXU) for h in range(HEADS)]
        gu = _gelu(zu_ref[...])
        _, vhat, _ = _gm_norm(zv_ref[...], seg)
        vln = (vhat * lg_ref[...] + lb_ref[...]).astype(MXU)
        for n in range(nblk):
            rows = slice(n * GM_CHUNK, (n + 1) * GM_CHUNK)
            mixed = _gm_mix(wm, vln[rows], GM_CHUNK) + be_ref[...]
            sgu_ref[rows, :] = (gu[rows] * mixed).astype(MXU)

    return _call(
        body, name="gmlp_fwd", grid=(s // tm,), sem=("parallel",),
        in_specs=[_rows(tm, GM_WIDTH, 1), _rows(tm, GM_WIDTH, 2), _full(ln_g.shape), _full(ln_b.shape),
                  _full(w_sp.shape), _full(bias_exp.shape)],
        out_specs=_rows(tm, GM_WIDTH), out_shape=_sds((s, GM_WIDTH), MXU),
    )(z, z, ln_g, ln_b, w_sp, bias_exp)


def _diag_mask(t):
    qc = lax.broadcasted_iota(jnp.int32, (t, t), 0) >> CHUNK_SHIFT
    kc = lax.broadcasted_iota(jnp.int32, (t, t), 1) >> CHUNK_SHIFT
    return kc <= qc


NEG_BIG = -1e30
ATTN_HEADS_PER_STEP = 2


def _attn_fwd(qp, kp, vp, tq, gathered):
    s = qp.shape[0]
    nq = s // tq
    hb = ATTN_HEADS_PER_STEP
    groups = HEADS // hb
    width = hb * HEAD_PAD
    ng = len(gathered)

    def body(q_ref, k_ref, v_ref, *rest):
        g_in, (o_ref, lse_ref), g_out = rest[:ng], rest[ng:ng + 2], rest[ng + 2:2 * ng + 2]
        m_sc, acc_sc = rest[2 * ng + 2:2 * ng + 4]
        g_start, g_forward, g_finish = _gather_steps(g_in, g_out, rest[2 * ng + 4:])
        g, i = pl.program_id(0), pl.program_id(1)
        pl.when((g == 0) & (i == 0))(g_start)
        pl.when((g == groups - 1) & (i == 0))(g_forward)
        m_sc[...] = jnp.full(m_sc.shape, NEG_BIG, F32)
        acc_sc[...] = jnp.zeros(acc_sc.shape, F32)

        def tile(j, masked):
            rows = pl.ds(pl.multiple_of(j * tq, tq), tq)
            for hh in range(hb):
                lanes = slice(hh * HEAD_PAD, (hh + 1) * HEAD_PAD)
                sc = _dot_nt(q_ref[:, lanes], k_ref[rows, lanes])
                if masked:
                    sc = jnp.where(_diag_mask(tq), sc, NEG_BIG)
                blocks = [sc[:, b * 128:(b + 1) * 128] for b in range(tq // 128)]
                m_prev = m_sc[hh]
                m_tile = jnp.max(functools.reduce(jnp.maximum, blocks), axis=-1, keepdims=True)
                m_new = jnp.maximum(m_prev, m_tile)
                alpha = jnp.exp2((m_prev - m_new) * SCALE_LOG2E)
                p = jnp.concatenate([jnp.exp2((b - m_new) * SCALE_LOG2E) for b in blocks], axis=1).astype(MXU)
                acc_sc[hh] = alpha * acc_sc[hh] + _dot(p, v_ref[rows, lanes])
                m_sc[hh] = m_new

        def off_diagonal_pair(p, carry):
            tile(2 * p, False)
            tile(2 * p + 1, False)
            return carry

        lax.fori_loop(0, i // 2, off_diagonal_pair, 0)

        @pl.when(i % 2 == 1)
        def _():
            tile(i - 1, False)

        tile(i, True)
        for hh in range(hb):
            lanes = slice(hh * HEAD_PAD, (hh + 1) * HEAD_PAD)
            acc = acc_sc[hh]
            denom = acc[:, NOPE:NOPE + 1]
            o_ref[:, lanes] = (acc / denom).astype(MXU)
            lse_ref[hh] = m_sc[hh][:, :1] * SCALE_LOG2E + jnp.log(denom) * LOG2E
        pl.when((g == groups - 1) & (i == nq - 1))(g_finish)

    q_spec = pl.BlockSpec((tq, width), lambda g, i: (i, g))
    kv_spec = pl.BlockSpec((s, width), lambda g, i: (0, g))
    any_spec = pl.BlockSpec(memory_space=pl.ANY)
    outs = _call(
        body, name="attn_fwd", grid=(groups, nq), sem=("arbitrary", "arbitrary"),
        in_specs=[q_spec, kv_spec, kv_spec] + [any_spec] * ng,
        out_specs=[q_spec, pl.BlockSpec((hb, tq, 1), lambda g, i: (g, i, 0))] + [any_spec] * ng,
        out_shape=[_sds(qp.shape, MXU), _sds((HEADS, s, 1), F32)]
        + [_sds((N_DEV,) + a.shape, a.dtype) for a in gathered],
        scratch=[pltpu.VMEM((hb, tq, HEAD_PAD), F32), pltpu.VMEM((hb, tq, HEAD_PAD), F32)] + _comm_sems(ng),
    )(qp, kp, vp, *gathered)
    return outs[0], outs[1], outs[2:]


def _out_proj_fwd(o_pad, sgu, wo, x, ada_raw, ada_b, g_post_mix, g_pre_ffn, tm):
    s = x.shape[0]
    hp = HEADS * HEAD_PAD

    def body(o_ref, sgu_ref, wo_ref, x_ref, ar_ref, ab_ref, gpm_ref, gpf_ref, m_ref, x2_ref, h2_ref):
        gt1 = _row(ar_ref, 2) + _row(ab_ref, 2)
        sh2 = _row(ar_ref, 3) + _row(ab_ref, 3)
        sc2 = _row(ar_ref, 4) + _row(ab_ref, 4)
        m = _dot(o_ref[...], wo_ref[pl.ds(0, hp), :]) + _dot(sgu_ref[...], wo_ref[pl.ds(hp, GM_WIDTH), :])
        m_ref[...] = m
        x2 = x_ref[...] + gt1 * (_rms(m)[0] * gpm_ref[...])
        x2_ref[...] = x2
        h2_ref[...] = ((_rms(x2)[0] * gpf_ref[...]) * (1.0 + sc2) + sh2).astype(MXU)

    return _call(
        body, name="out_proj_fwd", grid=(s // tm,), sem=("parallel",),
        in_specs=[_rows(tm, hp), _rows(tm, GM_WIDTH), _full(wo.shape), _rows(tm, D_MODEL), _full(ada_raw.shape),
                  _full(ada_b.shape), _full(g_post_mix.shape), _full(g_pre_ffn.shape)],
        out_specs=[_rows(tm, D_MODEL)] * 3,
        out_shape=[_sds((s, D_MODEL), F32), _sds((s, D_MODEL), F32), _sds((s, D_MODEL), MXU)],
    )(o_pad, sgu, wo, x, ada_raw, ada_b, g_post_mix, g_pre_ffn)


def _conv(u, halo, cw_ref, cb_ref):
    ext = jnp.concatenate([halo, u], axis=0)
    m1, m2 = pltpu.roll(ext, 1, 0)[8:], pltpu.roll(ext, 2, 0)[8:]
    return cb_ref[0] + ((m2 * cw_ref[0, pl.ds(0, 1), :] + m1 * cw_ref[0, pl.ds(1, 1), :]) + u * cw_ref[0, pl.ds(2, 1), :])


ROW_SUB = 256


def _sub_blocks(tm):
    return [slice(r, r + ROW_SUB) for r in range(0, tm, ROW_SUB)]


def _ffn_up_fwd(h2, w_up, conv_w, conv_b, tm):
    s = h2.shape[0]
    half = N_DEV // 2

    def body(h_ref, wa_ref, wb_ref, cwa_ref, cwb_ref, cba_ref, cbb_ref,
             ua_ref, ub_ref, ya_ref, yb_ref, act_ref, halo_a, halo_b):
        i = pl.program_id(1)

        @pl.when(i == 0)
        def _():
            halo_a[...] = jnp.zeros(halo_a.shape, F32)
            halo_b[...] = jnp.zeros(halo_b.shape, F32)

        ha, hb = halo_a[...], halo_b[...]
        for rows in _sub_blocks(tm):
            h = h_ref[rows, :]
            ua = _dot_nt(h, wa_ref[0])
            ub = _dot_nt(h, wb_ref[0])
            ua_ref[0, rows, :] = ua
            ub_ref[0, rows, :] = ub
            ya = _conv(ua, ha, cwa_ref, cba_ref)
            yb = _conv(ub, hb, cwb_ref, cbb_ref)
            ya_ref[0, rows, :] = ya
            yb_ref[0, rows, :] = yb
            ha, hb = ua[ROW_SUB - 8:], ub[ROW_SUB - 8:]
            act_ref[0, rows, :] = ((ya * jax.nn.sigmoid(ya)) * yb).astype(MXU)
        halo_a[...] = ha
        halo_b[...] = hb

    def blk(shape, off):
        return pl.BlockSpec(shape, lambda j, i: (j + off, 0, 0))

    def tok(off=0):
        return pl.BlockSpec((1, tm, FF_BLK), lambda j, i: (j + off, i, 0))

    return _call(
        body, name="ffn_up_fwd", grid=(half, s // tm), sem=("parallel", "arbitrary"),
        in_specs=[pl.BlockSpec((tm, D_MODEL), lambda j, i: (i, 0)),
                  blk((1, FF_BLK, D_MODEL), 0), blk((1, FF_BLK, D_MODEL), half),
                  blk((1, 3, FF_BLK), 0), blk((1, 3, FF_BLK), half), blk((1, 1, FF_BLK), 0), blk((1, 1, FF_BLK), half)],
        out_specs=[tok()] * 5,
        out_shape=[_sds((half, s, FF_BLK), F32)] * 4 + [_sds((half, s, FF_BLK), MXU)],
        scratch=[pltpu.VMEM((8, FF_BLK), F32), pltpu.VMEM((8, FF_BLK), F32)],
    )(h2, w_up, w_up, conv_w, conv_w, conv_b, conv_b)


def _ffn_down_fwd(act, wd, x2, target, ada_raw, ada_b, g_post_ffn, tm):
    s = x2.shape[0]
    half = N_DEV // 2

    def body(act_ref, wd_ref, x2_ref, t_ref, ar_ref, ab_ref, g_ref, dout_ref, df_ref, loss_ref, dgt_ref, dg_ref):
        i = pl.program_id(0)

        @pl.when(i == 0)
        def _():
            loss_ref[...] = jnp.zeros(loss_ref.shape, F32)
            dgt_ref[...] = jnp.zeros(dgt_ref.shape, F32)
            dg_ref[...] = jnp.zeros(dg_ref.shape, F32)

        gt2 = _row(ar_ref, 5) + _row(ab_ref, 5)
        g = g_ref[...]
        for rows in _sub_blocks(tm):
            f = _dot(act_ref[0, rows, :], wd_ref[0])
            for j in range(1, half):
                f = f + _dot(act_ref[j, rows, :], wd_ref[j])
            fhat, rf = _rms(f)
            fn = fhat * g
            err = (x2_ref[rows, :] + gt2 * fn) - t_ref[rows, :]
            loss_ref[...] += 0.5 * jnp.sum(jnp.mean(err * err, axis=-1, keepdims=True))
            d_out = err * (1.0 / D_MODEL)
            dout_ref[rows, :] = d_out
            dgt_ref[...] += jnp.sum(d_out * fn, axis=0, keepdims=True)
            d_fn = d_out * gt2
            dg_ref[...] += jnp.sum(d_fn * fhat, axis=0, keepdims=True)
            df_ref[rows, :] = _rms_bwd(d_fn * g, fhat, rf).astype(MXU)

    vec = pl.BlockSpec((1, D_MODEL), lambda i: (0, 0))
    return _call(
        body, name="ffn_down_fwd", grid=(s // tm,), sem=("arbitrary",),
        in_specs=[pl.BlockSpec((half, tm, FF_BLK), lambda i: (0, i, 0)), _full(wd.shape), _rows(tm, D_MODEL),
                  _rows(tm, D_MODEL), _full(ada_raw.shape), _full(ada_b.shape), _full(g_post_ffn.shape)],
        out_specs=[_rows(tm, D_MODEL), _rows(tm, D_MODEL), pl.BlockSpec((1, 128), lambda i: (0, 0)), vec, vec],
        out_shape=[_sds((s, D_MODEL), F32), _sds((s, D_MODEL), MXU), _sds((1, 128), F32),
                   _sds((1, D_MODEL), F32), _sds((1, D_MODEL), F32)],
    )(act, wd, x2, target, ada_raw, ada_b, g_post_ffn)


def _ffn_down_bwd(d_f, wd, up_a, up_b, y_a, y_b, conv_w, tm):
    s = d_f.shape[0]
    half = N_DEV // 2
    nt = s // tm

    def body(df_ref, wd_ref, ua_ref, ub_ref, ya_ref, yb_ref, cwa_ref, cwb_ref,
             dup_ref, dcw_ref, dcb_ref, next_a, next_b):
        i = pl.program_id(1)

        @pl.when(i == 0)
        def _():
            next_a[...] = jnp.zeros(next_a.shape, F32)
            next_b[...] = jnp.zeros(next_b.shape, F32)
            dcw_ref[...] = jnp.zeros(dcw_ref.shape, F32)
            dcb_ref[...] = jnp.zeros(dcb_ref.shape, F32)

        def conv_bwd(d_y, u, nxt, cw_ref, part, rows):
            ext = jnp.concatenate([d_y, nxt], axis=0)
            p1 = pltpu.roll(ext, ROW_SUB + 7, 0)[:ROW_SUB]
            p2 = pltpu.roll(ext, ROW_SUB + 6, 0)[:ROW_SUB]
            d_u = (d_y * cw_ref[0, pl.ds(2, 1), :] + p1 * cw_ref[0, pl.ds(1, 1), :]) + p2 * cw_ref[0, pl.ds(0, 1), :]
            dup_ref[0, part, rows, :] = d_u.astype(MXU)
            dcb_ref[0, part] += jnp.sum(d_y, axis=0, keepdims=True)
            dcw_ref[0, part, pl.ds(0, 1), :] += jnp.sum(p2 * u, axis=0, keepdims=True)
            dcw_ref[0, part, pl.ds(1, 1), :] += jnp.sum(p1 * u, axis=0, keepdims=True)
            dcw_ref[0, part, pl.ds(2, 1), :] += jnp.sum(d_y * u, axis=0, keepdims=True)
            return d_y[:8]

        nxa, nxb = next_a[...], next_b[...]
        for rows in reversed(_sub_blocks(tm)):
            d_act = _dot_nt(df_ref[rows, :], wd_ref[0])
            ya, yb = ya_ref[0, rows, :], yb_ref[0, rows, :]
            sig = jax.nn.sigmoid(ya)
            d_ya = d_act * yb * (sig * (1.0 + ya * (1.0 - sig)))
            d_yb = d_act * (ya * sig)
            nxa = conv_bwd(d_ya, ua_ref[0, rows, :], nxa, cwa_ref, 0, rows)
            nxb = conv_bwd(d_yb, ub_ref[0, rows, :], nxb, cwb_ref, 1, rows)
        next_a[...] = nxa
        next_b[...] = nxb

    def rev(i):
        return nt - 1 - i

    def blk(shape, off):
        return pl.BlockSpec(shape, lambda j, i: (j + off, 0, 0))

    tok = pl.BlockSpec((1, tm, FF_BLK), lambda j, i: (j, rev(i), 0))
    acc3 = pl.BlockSpec((1, 2, 3, FF_BLK), lambda j, i: (j, 0, 0, 0))
    acc1 = pl.BlockSpec((1, 2, 1, FF_BLK), lambda j, i: (j, 0, 0, 0))
    return _call(
        body, name="ffn_down_bwd", grid=(half, nt), sem=("parallel", "arbitrary"),
        in_specs=[pl.BlockSpec((tm, D_MODEL), lambda j, i: (rev(i), 0)), blk((1, FF_BLK, D_MODEL), 0),
                  tok, tok, tok, tok, blk((1, 3, FF_BLK), 0), blk((1, 3, FF_BLK), half)],
        out_specs=[pl.BlockSpec((1, 2, tm, FF_BLK), lambda j, i: (j, 0, rev(i), 0)), acc3, acc1],
        out_shape=[_sds((half, 2, s, FF_BLK), MXU), _sds((half, 2, 3, FF_BLK), F32), _sds((half, 2, 1, FF_BLK), F32)],
        scratch=[pltpu.VMEM((8, FF_BLK), F32), pltpu.VMEM((8, FF_BLK), F32)],
    )(d_f, wd, up_a, up_b, y_a, y_b, conv_w, conv_w)


def _ffn_up_bwd(d_up, w_up, x2, m, d_out, ada_raw, ada_b, g_pre_ffn, g_post_mix, tm):
    s = x2.shape[0]
    half = N_DEV // 2

    def body(dup_ref, w_ref, x2_ref, m_ref, dout_ref, ar_ref, ab_ref, gpf_ref, gpm_ref,
             dx_ref, dm_ref, dsh_ref, dsc_ref, dgpf_ref, dgt1_ref, dgpm_ref):
        i = pl.program_id(0)

        @pl.when(i == 0)
        def _():
            for r in (dsh_ref, dsc_ref, dgpf_ref, dgt1_ref, dgpm_ref):
                r[...] = jnp.zeros(r.shape, F32)

        gt1 = _row(ar_ref, 2) + _row(ab_ref, 2)
        sc2 = _row(ar_ref, 4) + _row(ab_ref, 4)
        gpf, gpm = gpf_ref[...], gpm_ref[...]
        d_h2 = _dot(dup_ref[0, 0], w_ref[0])
        for j in range(1, half):
            d_h2 = d_h2 + _dot(dup_ref[j, 0], w_ref[j])
        for j in range(half):
            d_h2 = d_h2 + _dot(dup_ref[j, 1], w_ref[half + j])
        x2n, r2 = _rms(x2_ref[...])
        dsh_ref[...] += jnp.sum(d_h2, axis=0, keepdims=True)
        dsc_ref[...] += jnp.sum(d_h2 * (x2n * gpf), axis=0, keepdims=True)
        d_mod = d_h2 * (1.0 + sc2)
        dgpf_ref[...] += jnp.sum(d_mod * x2n, axis=0, keepdims=True)
        d_x2 = dout_ref[...] + _rms_bwd(d_mod * gpf, x2n, r2)
        dx_ref[...] = d_x2
        mhat, rm = _rms(m_ref[...])
        dgt1_ref[...] += jnp.sum(d_x2 * (mhat * gpm), axis=0, keepdims=True)
        d_mn = d_x2 * gt1
        dgpm_ref[...] += jnp.sum(d_mn * mhat, axis=0, keepdims=True)
        dm_ref[...] = _rms_bwd(d_mn * gpm, mhat, rm).astype(MXU)

    vec = pl.BlockSpec((1, D_MODEL), lambda i: (0, 0))
    tok = pl.BlockSpec((half, 2, tm, FF_BLK), lambda i: (0, 0, i, 0))
    return _call(
        body, name="ffn_up_bwd", grid=(s // tm,), sem=("arbitrary",),
        in_specs=[tok, _full(w_up.shape), _rows(tm, D_MODEL), _rows(tm, D_MODEL), _rows(tm, D_MODEL),
                  _full(ada_raw.shape), _full(ada_b.shape), _full(g_pre_ffn.shape), _full(g_post_mix.shape)],
        out_specs=[_rows(tm, D_MODEL), _rows(tm, D_MODEL), vec, vec, vec, vec, vec],
        out_shape=[_sds((s, D_MODEL), F32), _sds((s, D_MODEL), MXU)] + [_sds((1, D_MODEL), F32)] * 5,
    )(d_up, w_up, x2, m, d_out, ada_raw, ada_b, g_pre_ffn, g_post_mix)


def _out_proj_bwd(d_m, wo, o_pad, tm):
    s = d_m.shape[0]
    hp = HEADS * HEAD_PAD

    def body(dm_ref, wo_ref, o_ref, do_ref, dsgu_ref, delta_ref):
        d_cat = _dot_nt(dm_ref[...], wo_ref[...])
        d_o = d_cat[:, :hp]
        do_ref[...] = d_o.astype(MXU)
        dsgu_ref[...] = d_cat[:, hp:]
        prod = d_o * o_ref[...].astype(F32)
        for h in range(HEADS):
            delta_ref[h] = jnp.sum(prod[:, h * HEAD_PAD:(h + 1) * HEAD_PAD], axis=-1, keepdims=True)

    return _call(
        body, name="out_proj_bwd", grid=(s // tm,), sem=("parallel",),
        in_specs=[_rows(tm, D_MODEL), _full(wo.shape), _rows(tm, hp)],
        out_specs=[_rows(tm, hp), _rows(tm, GM_WIDTH), pl.BlockSpec((HEADS, tm, 1), lambda i: (0, i, 0))],
        out_shape=[_sds((s, hp), MXU), _sds((s, GM_WIDTH), F32), _sds((HEADS, s, 1), F32)],
    )(d_m, wo, o_pad)


def _attn_bwd(qp, kp, vp, d_o, lse, delta, tq, scattered):
    s = qp.shape[0]
    nq = s // tq
    hb = ATTN_HEADS_PER_STEP
    groups = HEADS // hb
    width = hb * HEAD_PAD
    ns = len(scattered)
    slots = [slot for _, slot in scattered]

    def body(q_ref, k_ref, v_ref, do_ref, lse_ref, dl_ref, *rest):
        s_in, (dq_ref, dk_ref, dv_ref), s_out = rest[:ns], rest[ns:ns + 3], rest[ns + 3:2 * ns + 3]
        dk_sc, dv_sc = rest[2 * ns + 3:2 * ns + 5]
        s_start, s_finish = _scatter_steps(s_in, s_out, rest[2 * ns + 5:], slots)
        g, j = pl.program_id(0), pl.program_id(1)
        pl.when((g == 0) & (j == 0))(s_start)

        @pl.when(j == 0)
        def _():
            dq_ref[...] = jnp.zeros(dq_ref.shape, F32)

        dk_sc[...] = jnp.zeros(dk_sc.shape, F32)
        dv_sc[...] = jnp.zeros(dv_sc.shape, F32)

        def tile(i, masked):
            rows = pl.ds(pl.multiple_of(i * tq, tq), tq)
            for hh in range(hb):
                lanes = slice(hh * HEAD_PAD, (hh + 1) * HEAD_PAD)
                q, do, k = q_ref[rows, lanes], do_ref[rows, lanes], k_ref[:, lanes]
                sc = _dot_nt(q, k)
                if masked:
                    sc = jnp.where(_diag_mask(tq), sc, NEG_BIG)
                p = jnp.exp2(sc * SCALE_LOG2E - lse_ref[hh, rows, :])
                dv_sc[hh] += _dot_tn(p.astype(MXU), do)
                dp = _dot_nt(do, v_ref[:, lanes])
                ds = (p * (dp - dl_ref[hh, rows, :])).astype(MXU)
                dk_sc[hh] += _dot_tn(ds, q)
                dq_ref[rows, lanes] += _dot(ds, k) * ATTN_SCALE

        def off_diagonal_pair(p, carry):
            tile(j + 1 + 2 * p, False)
            tile(j + 2 + 2 * p, False)
            return carry

        below = nq - 1 - j
        tile(j, True)
        lax.fori_loop(0, below // 2, off_diagonal_pair, 0)

        @pl.when(below % 2 == 1)
        def _():
            tile(nq - 1, False)
        for hh in range(hb):
            lanes = slice(hh * HEAD_PAD, (hh + 1) * HEAD_PAD)
            dk_ref[:, lanes] = dk_sc[hh] * ATTN_SCALE
            dv_ref[:, lanes] = dv_sc[hh]
        pl.when((g == groups - 1) & (j == nq - 1))(s_finish)

    seq_spec = pl.BlockSpec((s, width), lambda g, j: (0, g))
    kv_spec = pl.BlockSpec((tq, width), lambda g, j: (j, g))
    col_spec = pl.BlockSpec((hb, s, 1), lambda g, j: (g, 0, 0))
    any_spec = pl.BlockSpec(memory_space=pl.ANY)
    outs = _call(
        body, name="attn_bwd", grid=(groups, nq), sem=("arbitrary", "arbitrary"),
        in_specs=[seq_spec, kv_spec, kv_spec, seq_spec, col_spec, col_spec] + [any_spec] * ns,
        out_specs=[seq_spec, kv_spec, kv_spec] + [any_spec] * ns,
        out_shape=[_sds(qp.shape, F32), _sds(qp.shape, F32), _sds(qp.shape, F32)]
        + [_scatter_out_shape(a, slot) for a, slot in scattered],
        scratch=[pltpu.VMEM((hb, tq, HEAD_PAD), F32), pltpu.VMEM((hb, tq, HEAD_PAD), F32)] + _comm_sems(ns),
    )(qp, kp, vp, d_o, lse, delta, *[a for a, _ in scattered])
    return outs[0], outs[1], outs[2], outs[3:]


def _gmlp_bwd(z, d_sgu, ln_g, ln_b, w_sp, bias_exp, tm):
    s = z.shape[0]
    nblk = tm // GM_CHUNK

    def body(zu_ref, zv_ref, dsgu_ref, lg_ref, lb_ref, w_ref, be_ref,
             dguv_ref, dws_ref, dbs_ref, dlg_ref, dlb_ref, dbe_sc, dvln_sc, dlg_sc, dlb_sc):
        i = pl.program_id(0)

        @pl.when(i == 0)
        def _():
            for r in (dws_ref, dlg_sc, dlb_sc, dbe_sc):
                r[...] = jnp.zeros(r.shape, F32)

        seg = _seg_matrix()
        mask = _spatial_mask()
        wm = [(w_ref[h] * mask).astype(MXU) for h in range(HEADS)]
        zu, zv = zu_ref[...], zv_ref[...]
        gu = _gelu(zu)
        _, vhat, rstd = _gm_norm(zv, seg)
        lg = lg_ref[...]
        vln = (vhat * lg + lb_ref[...]).astype(MXU)
        d_sgu = dsgu_ref[...]
        for n in range(nblk):
            rows = slice(n * GM_CHUNK, (n + 1) * GM_CHUNK)
            vb = vln[rows]
            mixed = _gm_mix(wm, vb, GM_CHUNK) + be_ref[...]
            d_mixed = d_sgu[rows] * gu[rows]
            dguv_ref[rows, pl.ds(0, GM_WIDTH)] = ((d_sgu[rows] * mixed) * _gelu_grad(zu[rows])).astype(MXU)
            dbe_sc[...] += d_mixed
            dmb = d_mixed.astype(MXU)
            d_vln = jnp.zeros((GM_CHUNK, GM_WIDTH), F32)
            for h in range(HEADS):
                hm = _head_lane_mask(h, GM_CHUNK)
                dws_ref[h] += _dot_nt(jnp.where(hm, dmb, jnp.zeros_like(dmb)), vb)
                d_vln = d_vln + jnp.where(hm, _dot_tn(wm[h], dmb), 0.0)
            dvln_sc[rows, :] = d_vln
        d_vln = dvln_sc[...]
        dlg_sc[...] += jnp.sum(d_vln * vhat, axis=0, keepdims=True)
        dlb_sc[...] += jnp.sum(d_vln, axis=0, keepdims=True)
        d_vhat = d_vln * lg
        d_gv = rstd * ((d_vhat - _split_dot(d_vhat, seg)) - vhat * _split_dot(d_vhat * vhat, seg))
        dguv_ref[:, pl.ds(GM_WIDTH, GM_WIDTH)] = (d_gv * _gelu_grad(zv)).astype(MXU)

        @pl.when(i == pl.num_programs(0) - 1)
        def _():
            for h in range(HEADS):
                dws_ref[h] = dws_ref[h] * mask
            hrow = lax.broadcasted_iota(jnp.int32, (HEADS, GM_WIDTH), 0)
            hlane = lax.broadcasted_iota(jnp.int32, (HEADS, GM_WIDTH), 1) >> 6
            ind = jnp.where(hrow == hlane, 1.0, 0.0).astype(MXU)
            acc = dbe_sc[...]
            hi = acc.astype(MXU)
            lo = (acc - hi.astype(F32)).astype(MXU)
            dbs_ref[...] = _dot_nt(ind, hi) + _dot_nt(ind, lo)
            pick = (lax.broadcasted_iota(jnp.int32, (GM_WIDTH, GM_DIM), 0) & (GM_DIM - 1)
                    == lax.broadcasted_iota(jnp.int32, (GM_WIDTH, GM_DIM), 1))
            pick = jnp.where(pick, 1.0, 0.0).astype(MXU)
            for src, dst in ((dlg_sc, dlg_ref), (dlb_sc, dlb_ref)):
                spread = jnp.where(hrow == hlane, jnp.broadcast_to(src[...], (HEADS, GM_WIDTH)), 0.0)
                dst[...] = _split_dot3(spread, pick)

    return _call(
        body, name="gmlp_bwd", grid=(s // tm,), sem=("arbitrary",),
        in_specs=[_rows(tm, GM_WIDTH, 1), _rows(tm, GM_WIDTH, 2), _rows(tm, GM_WIDTH), _full(ln_g.shape),
                  _full(ln_b.shape), _full(w_sp.shape), _full(bias_exp.shape)],
        out_specs=[_rows(tm, 2 * GM_WIDTH), _full(w_sp.shape), _full((HEADS, GM_CHUNK)), _full((HEADS, GM_DIM)),
                   _full((HEADS, GM_DIM))],
        out_shape=[_sds((s, 2 * GM_WIDTH), MXU), _sds(w_sp.shape, F32), _sds((HEADS, GM_CHUNK), F32),
                   _sds((HEADS, GM_DIM), F32), _sds((HEADS, GM_DIM), F32)],
        scratch=[pltpu.VMEM((GM_CHUNK, GM_WIDTH), F32), pltpu.VMEM((tm, GM_WIDTH), F32),
                 pltpu.VMEM((1, GM_WIDTH), F32), pltpu.VMEM((1, GM_WIDTH), F32)],
    )(z, z, d_sgu, ln_g, ln_b, w_sp, bias_exp)


def _mix_in_bwd(dq, dk, dv, z, d_guv, x, d_x_part, ada_raw, ada_b, g_pre, g_q, g_kv, w1t, wqt, wkv,
                cos_t, sin_t, tm, gathered):
    s = x.shape[0]
    nt = s // tm
    hp = HEADS * HEAD_PAD
    za = Q_LORA + KV_LORA + HEAD_PAD
    ng = len(gathered)

    def body(dq_ref, dk_ref, dv_ref, z_ref, dguv_ref, x_ref, dxp_ref, ar_ref, ab_ref, g_ref, gq_ref, gkv_ref,
             w1_ref, wq_ref, wkv_ref, cos_ref, sin_ref, *rest):
        g_in, g_out = rest[:ng], rest[ng + 9:2 * ng + 9]
        gx_ref, dza_ref, dqp_ref, dkvp_ref, dsh_ref, dsc_ref, dg_ref, dgq_ref, dgkv_ref = rest[ng:ng + 9]
        g_start, g_forward, g_finish = _gather_steps(g_in, g_out, rest[2 * ng + 9:])
        i = pl.program_id(0)
        pl.when(i == 0)(g_start)
        pl.when(i == (3 * nt) // 4)(g_forward)

        @pl.when(i == 0)
        def _():
            for r in (dsh_ref, dsc_ref, dg_ref, dgq_ref, dgkv_ref):
                r[...] = jnp.zeros(r.shape, F32)

        cos, sin = cos_ref[...], sin_ref[...]
        d_krot = jnp.zeros((tm, HEAD_PAD), F32)
        for h in range(HEADS):
            blk = slice(h * HEAD_PAD, (h + 1) * HEAD_PAD)
            dqp_ref[:, blk] = _rope_transposed(dq_ref[:, blk], cos, sin).astype(MXU)
            dk_h = dk_ref[:, blk]
            d_krot = d_krot + dk_h
            dkvp_ref[:, blk] = dk_h.astype(MXU)
        dkvp_ref[:, pl.ds(hp, hp)] = dv_ref[...].astype(MXU)
        lane = lax.broadcasted_iota(jnp.int32, (tm, HEAD_PAD), 1)
        d_kr = jnp.where((lane >= NOPE) & (lane < NOPE + ROPE), _rope_transposed(d_krot, cos, sin), 0.0)
        d_cqn = _dot(dqp_ref[...], wq_ref[...])
        d_ckvn = _dot_nt(dkvp_ref[...], wkv_ref[...])
        zt = z_ref[...]
        gq, gkv = gq_ref[...], gkv_ref[...]
        cq_hat, rq = _rms(zt[:, :Q_LORA])
        ckv_hat, rkv = _rms(zt[:, Q_LORA:Q_LORA + KV_LORA])
        dgq_ref[...] += jnp.sum(d_cqn * cq_hat, axis=0, keepdims=True)
        dgkv_ref[...] += jnp.sum(d_ckvn * ckv_hat, axis=0, keepdims=True)
        d_cq = _rms_bwd(d_cqn * gq, cq_hat, rq)
        d_ckv = _rms_bwd(d_ckvn * gkv, ckv_hat, rkv)
        d_za = jnp.concatenate([d_cq, d_ckv, d_kr], axis=1).astype(MXU)
        dza_ref[...] = d_za
        d_h1 = _dot(d_za, w1_ref[pl.ds(0, za), :]) + _dot(dguv_ref[...], w1_ref[pl.ds(za, 2 * GM_WIDTH), :])
        sc1 = _row(ar_ref, 1) + _row(ab_ref, 1)
        g = g_ref[...]
        xn, r1 = _rms(x_ref[...])
        dsh_ref[...] += jnp.sum(d_h1, axis=0, keepdims=True)
        dsc_ref[...] += jnp.sum(d_h1 * (xn * g), axis=0, keepdims=True)
        d_mod = d_h1 * (1.0 + sc1)
        dg_ref[...] += jnp.sum(d_mod * xn, axis=0, keepdims=True)
        gx_ref[...] = dxp_ref[...] + _rms_bwd(d_mod * g, xn, r1)
        pl.when(i == nt - 1)(g_finish)

    vec = pl.BlockSpec((1, D_MODEL), lambda i: (0, 0))
    any_spec = pl.BlockSpec(memory_space=pl.ANY)
    outs = _call(
        body, name="mix_in_bwd", grid=(nt,), sem=("arbitrary",),
        in_specs=[_rows(tm, hp), _rows(tm, hp), _rows(tm, hp), _rows(tm, za), _rows(tm, 2 * GM_WIDTH),
                  _rows(tm, D_MODEL), _rows(tm, D_MODEL), _full(ada_raw.shape), _full(ada_b.shape), _full(g_pre.shape),
                  _full(g_q.shape), _full(g_kv.shape), _full(w1t.shape), _full(wqt.shape),
                  _full(wkv.shape), _rows(tm, HEAD_PAD), _rows(tm, HEAD_PAD)] + [any_spec] * ng,
        out_specs=[_rows(tm, D_MODEL), _rows(tm, za), _rows(tm, hp), _rows(tm, 2 * hp), vec, vec, vec,
                   _full(g_q.shape), _full(g_kv.shape)] + [any_spec] * ng,
        out_shape=[_sds((s, D_MODEL), F32), _sds((s, za), MXU), _sds((s, hp), MXU), _sds((s, 2 * hp), MXU),
                   _sds((1, D_MODEL), F32), _sds((1, D_MODEL), F32), _sds((1, D_MODEL), F32),
                   _sds(g_q.shape, F32), _sds(g_kv.shape, F32)]
        + [_sds((N_DEV,) + a.shape, a.dtype) for a in gathered],
        scratch=_comm_sems(ng),
    )(dq, dk, dv, z, d_guv, x, d_x_part, ada_raw, ada_b, g_pre, g_q, g_kv, w1t, wqt, wkv, cos_t, sin_t, *gathered)
    return outs[:9], outs[9:]


def _tn_matmul(a, b, name, ts):
    ga, s, m = a.shape
    gb, _, n = b.shape
    g = max(ga, gb)
    tn = n if n <= 1024 else 1024
    steps = s // ts

    def body(a_ref, b_ref, o_ref, acc):
        k = pl.program_id(2)

        @pl.when(k == 0)
        def _():
            acc[...] = jnp.zeros(acc.shape, F32)

        acc[...] += _dot_tn(a_ref[0], b_ref[0])

        @pl.when(k == steps - 1)
        def _():
            o_ref[0] = acc[...].astype(MXU)

    return _call(
        body, name=name, grid=(g, n // tn, steps), sem=("parallel", "parallel", "arbitrary"),
        in_specs=[pl.BlockSpec((1, ts, m), lambda gi, ni, k: (gi if ga > 1 else 0, k, 0)),
                  pl.BlockSpec((1, ts, tn), lambda gi, ni, k: (gi if gb > 1 else 0, k, ni))],
        out_specs=pl.BlockSpec((1, m, tn), lambda gi, ni, k: (gi, 0, ni)),
        out_shape=_sds((g, m, n), MXU),
        scratch=[pltpu.VMEM((m, tn), F32)],
    )(a, b)


def _adamw(w, g, m, v):
    m2 = ADAM_B1 * m + (1.0 - ADAM_B1) * g
    v2 = ADAM_B2 * v + (1.0 - ADAM_B2) * (g * g)
    m_hat = m2 / (1.0 - ADAM_B1 ** ADAM_STEP)
    v_hat = v2 / (1.0 - ADAM_B2 ** ADAM_STEP)
    delta = -ADAM_LR * (m_hat / (jnp.sqrt(v_hat) + ADAM_EPS) + ADAM_WD * w)
    return delta, m2, v2


def _adam_reduce(recv, w, m, v, name):
    r, c = w.shape
    tr = r if r <= 512 else max(t for t in range(16, 513, 16) if r % t == 0)

    def body(p_ref, w_ref, m_ref, v_ref, g_ref, d_ref, mo_ref, vo_ref):
        g = p_ref[0].astype(F32)
        for j in range(1, N_DEV):
            g = g + p_ref[j].astype(F32)
        g_ref[...] = g
        d_ref[...], mo_ref[...], vo_ref[...] = _adamw(w_ref[...], g, m_ref[...], v_ref[...])

    blk = pl.BlockSpec((tr, c), lambda i: (i, 0))
    return _call(
        body, name=name, grid=(r // tr,), sem=("parallel",),
        in_specs=[pl.BlockSpec((N_DEV, tr, c), lambda i: (0, i, 0)), blk, blk, blk],
        out_specs=[blk] * 4, out_shape=[_sds((r, c), F32)] * 4,
    )(recv, w, m, v)


def _adam_w_ada(c_act_t, d_ada_cols, w, m, v):
    r, c = w.shape
    tr = 256

    def body(ct_ref, da_ref, w_ref, m_ref, v_ref, g_ref, d_ref, mo_ref, vo_ref):
        g = ct_ref[:, pl.ds(0, 1)] * da_ref[pl.ds(0, 1), :]
        for b in range(1, N_DEV):
            g = g + ct_ref[:, pl.ds(b, 1)] * da_ref[pl.ds(b, 1), :]
        g_ref[...] = g
        d_ref[...], mo_ref[...], vo_ref[...] = _adamw(w_ref[...], g, m_ref[...], v_ref[...])

    blk = pl.BlockSpec((tr, c), lambda i: (i, 0))
    return _call(
        body, name="adam_w_ada", grid=(r // tr,), sem=("parallel",),
        in_specs=[pl.BlockSpec((tr, N_DEV), lambda i: (i, 0)), _full(d_ada_cols.shape), blk, blk, blk],
        out_specs=[blk] * 4, out_shape=[_sds((r, c), F32)] * 4,
    )(c_act_t, d_ada_cols, w, m, v)


VEC_ROWS = D_MODEL // 128
PK_ADA = 0
PK_GAIN = PK_ADA + 6 * VEC_ROWS
PK_GQ = PK_GAIN + 4 * VEC_ROWS
PK_GKV = PK_GQ + Q_LORA // 128
PK_LOSS = PK_GKV + KV_LORA // 128
PK_LNG = 88
PK_LNB = PK_LNG + HEADS
PK_BS = PK_LNB + HEADS
PK_CB = PK_BS + HEADS
CB_ROWS = 6
PK_WS = PK_CB + N_DEV * CB_ROWS
PK_ROWS = PK_WS + HEADS * GM_CHUNK
assert PK_LOSS < PK_LNG and PK_ROWS % 8 == 0
LATE_GAIN = 2 * VEC_ROWS
LATE_GQ = 3 * VEC_ROWS
LATE_GKV = LATE_GQ + Q_LORA // 128
LATE_ROWS = 32


def _cb_chunks():
    return [(k, k * 128, min(128, FF_BLK - k * 128)) for k in range(CB_ROWS)]


def _put_rows(out_ref, row0, ref, width):
    for k in range(width // 128):
        out_ref[pl.ds(row0 + k, 1), :] = ref[:, pl.ds(k * 128, 128)]


def _pack_small(ada_rows, gains, loss_part, d_ln_g, d_ln_b, d_bs, d_cb, d_ws):
    half = N_DEV // 2

    def body(*refs):
        vec_refs = refs[:7]
        loss_ref, lng_ref, lnb_ref, bs_ref, cb_ref, ws_ref, out_ref = refs[7:]
        out_ref[pl.ds(0, PK_WS), :] = jnp.zeros((PK_WS, 128), F32)
        for n, ref in enumerate(vec_refs[:4]):
            _put_rows(out_ref, PK_ADA + (2 + n) * VEC_ROWS, ref, D_MODEL)
        for n, ref in enumerate(vec_refs[4:]):
            _put_rows(out_ref, PK_GAIN + (1 + n) * VEC_ROWS, ref, D_MODEL)
        _put_rows(out_ref, PK_LOSS, loss_ref, 128)
        out_ref[pl.ds(PK_LNG, HEADS), pl.ds(0, GM_DIM)] = lng_ref[...]
        out_ref[pl.ds(PK_LNB, HEADS), pl.ds(0, GM_DIM)] = lnb_ref[...]
        out_ref[pl.ds(PK_BS, HEADS), :] = bs_ref[...]
        for j in range(N_DEV):
            for k, lane, width in _cb_chunks():
                out_ref[pl.ds(PK_CB + j * CB_ROWS + k, 1), pl.ds(0, width)] = cb_ref[j % half, j // half, :, pl.ds(lane, width)]
        for h in range(HEADS):
            out_ref[pl.ds(PK_WS + h * GM_CHUNK, GM_CHUNK), :] = ws_ref[h]

    ins = list(ada_rows) + list(gains) + [loss_part, d_ln_g, d_ln_b, d_bs, d_cb, d_ws]
    return _call(body, name="pack_small", grid=(1,), in_specs=[_full(a.shape) for a in ins],
                 out_specs=_full((PK_ROWS, 128)), out_shape=_sds((PK_ROWS, 128), F32))(*ins)


def _pack_late(d_sh1, d_sc1, d_g_pre_mix, d_g_q, d_g_kv):
    def body(sh_ref, sc_ref, g_ref, gq_ref, gkv_ref, out_ref):
        out_ref[...] = jnp.zeros((LATE_ROWS, 128), F32)
        _put_rows(out_ref, 0, sh_ref, D_MODEL)
        _put_rows(out_ref, VEC_ROWS, sc_ref, D_MODEL)
        _put_rows(out_ref, LATE_GAIN, g_ref, D_MODEL)
        _put_rows(out_ref, LATE_GQ, gq_ref, Q_LORA)
        _put_rows(out_ref, LATE_GKV, gkv_ref, KV_LORA)

    ins = [d_sh1, d_sc1, d_g_pre_mix, d_g_q, d_g_kv]
    return _call(body, name="pack_late", grid=(1,), in_specs=[_full(a.shape) for a in ins],
                 out_specs=_full((LATE_ROWS, 128)), out_shape=_sds((LATE_ROWS, 128), F32))(*ins)


def _adam_small(gathered, late, params):
    n_par = len(params)

    def body(p_ref, late_ref, *refs):
        ins = [refs[3 * n:3 * n + 3] for n in range(n_par)]
        outs = [refs[3 * n_par + 4 * n:3 * n_par + 4 * n + 4] for n in range(n_par)]
        loss_ref, dada_ref = refs[7 * n_par:]

        def total(rows, lanes=slice(None), src=p_ref):
            g = src[0, rows, lanes]
            for j in range(1, N_DEV):
                g = g + src[j, rows, lanes]
            return g

        def apply(n, g, idx):
            w_ref, m_ref, v_ref = ins[n]
            d, m2, v2 = _adamw(w_ref[idx], g, m_ref[idx], v_ref[idx])
            for ref, val in zip(outs[n], (g, d, m2, v2)):
                ref[idx] = val

        def vector(n, src, row0, width, lane0=0):
            for k in range(width // 128):
                apply(n, total(pl.ds(row0 + k, 1), src=src), (slice(None), pl.ds(lane0 + k * 128, 128)))

        vector(0, late_ref, 0, 2 * D_MODEL)
        vector(0, p_ref, PK_ADA + 2 * VEC_ROWS, 4 * D_MODEL, lane0=2 * D_MODEL)
        vector(1, late_ref, LATE_GAIN, D_MODEL)
        for n in range(1, 4):
            vector(1 + n, p_ref, PK_GAIN + n * VEC_ROWS, D_MODEL)
        vector(5, late_ref, LATE_GQ, Q_LORA)
        vector(6, late_ref, LATE_GKV, KV_LORA)
        apply(7, total(pl.ds(PK_LNG, HEADS), pl.ds(0, GM_DIM)), (0,))
        apply(8, total(pl.ds(PK_LNB, HEADS), pl.ds(0, GM_DIM)), (0,))
        for h in range(HEADS):
            apply(9, total(pl.ds(PK_WS + h * GM_CHUNK, GM_CHUNK)), (0, h))
        apply(10, total(pl.ds(PK_BS, HEADS)), (0,))
        for j in range(N_DEV):
            for k, lane, width in _cb_chunks():
                apply(11, total(pl.ds(PK_CB + j * CB_ROWS + k, 1), pl.ds(0, width)), (pl.ds(j, 1), pl.ds(lane, width)))
        loss_ref[...] = total(pl.ds(PK_LOSS, 1))
        dada_ref[:, pl.ds(0, 2 * VEC_ROWS), :] = late_ref[:, pl.ds(0, 2 * VEC_ROWS), :]
        dada_ref[:, pl.ds(2 * VEC_ROWS, 4 * VEC_ROWS), :] = p_ref[:, pl.ds(PK_ADA + 2 * VEC_ROWS, 4 * VEC_ROWS), :]

    flat = [a for triple in params for a in triple]
    out_shape = [_sds(w.shape, F32) for w, _, _ in params for _ in range(4)]
    out_shape += [_sds((1, 128), F32), _sds((N_DEV, 6 * VEC_ROWS, 128), F32)]
    outs = _call(body, name="adam_small", grid=(1,),
                 in_specs=[_full(gathered.shape), _full(late.shape)] + [_full(a.shape) for a in flat],
                 out_specs=[_full(o.shape) for o in out_shape], out_shape=out_shape)(gathered, late, *flat)
    return [tuple(outs[4 * n:4 * n + 4]) for n in range(n_par)], outs[-2], outs[-1]


def _rope_tables(s):
    pos = jnp.arange(s, dtype=F32)
    inv = ROPE_THETA ** (-jnp.arange(0, ROPE, 2, dtype=F32) / ROPE)
    ang = pos[:, None] * inv[None, :]
    cos, sin = jnp.cos(ang), jnp.sin(ang)
    ones, zeros = jnp.ones((s, NOPE), F32), jnp.zeros((s, NOPE), F32)
    cos_t = jnp.concatenate([ones, cos, cos, ones[:, :HEAD_PAD - NOPE - ROPE]], axis=1)
    sin_t = jnp.concatenate([zeros, sin, sin, zeros[:, :HEAD_PAD - NOPE - ROPE]], axis=1)
    return cos_t, sin_t


def kernel(x, c, w_ada, b_ada, g_pre_mix, g_post_mix, w_in, g_q, w_uq, g_kv, w_ukv, gm_ln_g, gm_ln_b, w_spatial, b_spatial, w_out, g_pre_ffn, g_post_ffn, w_up, conv_w, conv_b, w_down, loss_target, m_w_ada, m_b_ada, m_g_pre_mix, m_g_post_mix, m_w_in, m_g_q, m_w_uq, m_g_kv, m_w_ukv, m_gm_ln_g, m_gm_ln_b, m_w_spatial, m_b_spatial, m_w_out, m_g_pre_ffn, m_g_post_ffn, m_w_up, m_conv_w, m_conv_b, m_w_down, v_w_ada, v_b_ada, v_g_pre_mix, v_g_post_mix, v_w_in, v_g_q, v_w_uq, v_g_kv, v_w_ukv, v_gm_ln_g, v_gm_ln_b, v_w_spatial, v_b_spatial, v_w_out, v_g_pre_ffn, v_g_post_ffn, v_w_up, v_conv_w, v_conv_b, v_w_down):
    s = x.shape[1]
    tm = min(256, s)
    tf = min(2 * ROW_SUB, s)
    tq = min(512, s)
    ts = min(2048, s)
    hp = HEADS * HEAD_PAD
    half = N_DEV // 2
    my_slot = 4 * lax.axis_index("x") + 2 * lax.axis_index("y") + lax.axis_index("c")
    x2d, target = x[0], loss_target[0]

    def t_(a):
        return jnp.swapaxes(a[0], 0, 1)

    w_in_t, m_in_t, v_in_t = t_(w_in), t_(m_w_in), t_(v_w_in)
    w_uq_t, m_uq_t, v_uq_t = t_(w_uq), t_(m_w_uq), t_(v_w_uq)
    w_up_t, m_up_t, v_up_t = t_(w_up), t_(m_w_up), t_(v_w_up)
    (g_c, g_in_t, g_uq_t, g_ukv, g_cw), _ = _exchange(
        [c, w_in_t.astype(MXU), w_uq_t.astype(MXU), w_ukv[0].astype(MXU), conv_w[0]], [], "gather_mixer_weights")

    w_in_f = g_in_t.reshape(-1, D_MODEL)
    o1, o2, o3 = Q_LORA, Q_LORA + KV_LORA, Q_LORA + KV_LORA + ROPE
    w1t = jnp.concatenate([w_in_f[:o2], jnp.zeros((NOPE, D_MODEL), MXU), w_in_f[o2:o3],
                           jnp.zeros((HEAD_PAD - NOPE - ROPE, D_MODEL), MXU), w_in_f[o3:]], axis=0)
    wqt = jnp.pad(g_uq_t, ((0, 0), (0, HEAD_PAD - NOPE - ROPE), (0, 0))).reshape(hp, Q_LORA)
    w_ukv_f = jnp.transpose(g_ukv, (1, 0, 2)).reshape(KV_LORA, HEADS, 2 * NOPE)
    pad_head = ((0, 0), (0, 0), (0, HEAD_PAD - NOPE))
    wkv = jnp.concatenate([jnp.pad(w_ukv_f[:, :, :NOPE], pad_head).reshape(KV_LORA, hp),
                           jnp.pad(w_ukv_f[:, :, NOPE:], pad_head).reshape(KV_LORA, hp)], axis=1)
    cb8 = conv_b.reshape(N_DEV, 1, FF_BLK)
    bias_exp = jnp.repeat(b_spatial[0].T, GM_DIM, axis=1)
    ln_g, ln_b = gm_ln_g.reshape(1, GM_WIDTH), gm_ln_b.reshape(1, GM_WIDTH)
    w_sp = w_spatial[0]
    cos_t, sin_t = _rope_tables(s)

    ada_part, c_act = _ada_fwd(g_c.reshape(N_DEV, D_MODEL), w_ada[0])
    _, (ada_recv,) = _exchange([], [(ada_part.reshape(N_DEV, 1, -1), _plain_slot)], "ada_rows")
    ada_raw = ada_recv.reshape(6, D_MODEL)
    ada_b = b_ada.reshape(6, D_MODEL)

    h1, z, qp, kp, vp, cqn, ckvn = _mix_in_fwd(x2d, ada_raw, ada_b, g_pre_mix, w1t, g_q, g_kv, wqt, wkv, cos_t, sin_t, tm)
    sgu = _gmlp_fwd(z, ln_g, ln_b, w_sp, bias_exp, tm)
    o_pad, lse, (g_out, g_up, g_down) = _attn_fwd(
        qp, kp, vp, tq, [w_out[0].astype(MXU), w_up_t.astype(MXU), w_down[0].astype(MXU)])
    w_out_f = g_out.reshape(2 * GM_WIDTH, D_MODEL)
    wo_attn = jnp.pad(w_out_f[:GM_WIDTH].reshape(HEADS, NOPE, D_MODEL), ((0, 0), (0, HEAD_PAD - NOPE), (0, 0)))
    wo = jnp.concatenate([wo_attn.reshape(hp, D_MODEL), w_out_f[GM_WIDTH:]], axis=0)
    wd = g_down.reshape(half, FF_BLK, D_MODEL)
    m_mix, x2, h2 = _out_proj_fwd(o_pad, sgu, wo, x2d, ada_raw, ada_b, g_post_mix, g_pre_ffn, tm)
    up_a, up_b, y_a, y_b, act = _ffn_up_fwd(h2, g_up, g_cw, cb8, tf)
    d_out, d_f, loss_part, d_gt2, d_g_post_ffn = _ffn_down_fwd(act, wd, x2, target, ada_raw, ada_b, g_post_ffn, tf)

    d_up, d_cw, d_cb = _ffn_down_bwd(d_f, wd, up_a, up_b, y_a, y_b, g_cw, tf)
    d_x2, d_m, d_sh2, d_sc2, d_g_pre_ffn, d_gt1, d_g_post_mix = _ffn_up_bwd(
        d_up, g_up, x2, m_mix, d_out, ada_raw, ada_b, g_pre_ffn, g_post_mix, tm)
    p_down = _tn_matmul(act, d_f[None], "dw_down", ts).reshape(N_DEV, -1, D_MODEL)
    p_up = _tn_matmul(d_up.reshape(N_DEV, s, FF_BLK), h2[None], "dw_up", ts).reshape(half, 2, FF_BLK, D_MODEL)
    d_m3 = d_m[None]
    dwo_attn = _tn_matmul(o_pad[None], d_m3, "dw_out_attn", ts)[0].reshape(HEADS, HEAD_PAD, D_MODEL)[:, :NOPE]
    dwo_sgu = _tn_matmul(sgu[None], d_m3, "dw_out_sgu", ts)[0]
    p_out = jnp.concatenate([dwo_attn.reshape(GM_WIDTH, D_MODEL), dwo_sgu], axis=0).reshape(N_DEV, -1, D_MODEL)
    d_o, d_sgu, delta = _out_proj_bwd(d_m, wo, o_pad, tm)
    def ffn_slot(j):
        return (j % half, j // half)

    dq, dk, dv, (r_out, r_up, r_down, r_cw) = _attn_bwd(
        qp, kp, vp, d_o, lse, delta, tq,
        [(p_out, _plain_slot), (p_up, ffn_slot), (p_down, _plain_slot), (d_cw, ffn_slot)])
    d_guv, d_ws, d_bs, d_ln_g, d_ln_b = _gmlp_bwd(z, d_sgu, ln_g, ln_b, w_sp, bias_exp, tm)
    za = Q_LORA + KV_LORA + HEAD_PAD
    packed = _pack_small([d_gt1, d_sh2, d_sc2, d_gt2], [d_g_post_mix, d_g_pre_ffn, d_g_post_ffn], loss_part,
                         d_ln_g, d_ln_b, d_bs, d_cb, d_ws)
    (grad_x, d_za, d_qp, d_kvp, d_sh1, d_sc1, d_g_pre_mix, d_g_q, d_g_kv), (g_small,) = _mix_in_bwd(
        dq, dk, dv, z, d_guv, x2d, d_x2, ada_raw, ada_b, g_pre_mix, g_q, g_kv, w1t, wqt, wkv, cos_t, sin_t, tm,
        [packed])
    h1_3 = h1[None]
    dw1a = _tn_matmul(d_za[None], h1_3, "dw_in_a", ts)[0]
    dw1b = _tn_matmul(d_guv[None], h1_3, "dw_in_b", ts)[0]
    d_w_in_t = jnp.concatenate([dw1a[:o2], dw1a[o2 + NOPE:o2 + NOPE + ROPE], dw1b], axis=0)
    p_in = d_w_in_t.reshape(N_DEV, -1, D_MODEL)
    p_uq = _tn_matmul(d_qp[None], cqn[None], "dw_uq", ts)[0].reshape(HEADS, HEAD_PAD, Q_LORA)[:, :NOPE + ROPE]
    dwkv = _tn_matmul(ckvn[None], d_kvp[None], "dw_ukv", ts)[0]
    dwk = dwkv[:, :hp].reshape(KV_LORA, HEADS, HEAD_PAD)[:, :, :NOPE]
    dwv = dwkv[:, hp:].reshape(KV_LORA, HEADS, HEAD_PAD)[:, :, :NOPE]
    p_ukv = jnp.transpose(jnp.concatenate([dwk, dwv], axis=2), (1, 0, 2))

    (g_late,), (r_in, r_uq, r_ukv) = _exchange(
        [_pack_late(d_sh1, d_sc1, d_g_pre_mix, d_g_q, d_g_kv)],
        [(p_in, _plain_slot), (p_uq, _plain_slot), (p_ukv, _plain_slot)], "final_exchange")
    small_params = [(b_ada, m_b_ada, v_b_ada), (g_pre_mix, m_g_pre_mix, v_g_pre_mix),
                    (g_post_mix, m_g_post_mix, v_g_post_mix), (g_pre_ffn, m_g_pre_ffn, v_g_pre_ffn),
                    (g_post_ffn, m_g_post_ffn, v_g_post_ffn), (g_q, m_g_q, v_g_q), (g_kv, m_g_kv, v_g_kv),
                    (gm_ln_g, m_gm_ln_g, v_gm_ln_g), (gm_ln_b, m_gm_ln_b, v_gm_ln_b),
                    (w_spatial, m_w_spatial, v_w_spatial), (b_spatial, m_b_spatial, v_b_spatial),
                    tuple(a.reshape(N_DEV, FF_BLK) for a in (conv_b, m_conv_b, v_conv_b))]
    small_out, loss_row, d_ada_all = _adam_small(g_small, g_late, small_params)
    small_out[11] = tuple(o.reshape(conv_b.shape) for o in small_out[11])
    loss = loss_row[0, 0]

    def big(recv, w, m, v, name):
        g, d, m2, v2 = _adam_reduce(recv, w[0], m[0], v[0], name)
        return g[None], d[None], m2[None], v2[None]

    def big_t(recv, w_t, m_t, v_t, name):
        return tuple(jnp.swapaxes(o, 0, 1)[None] for o in _adam_reduce(recv, w_t, m_t, v_t, name))

    a_in = big_t(r_in, w_in_t, m_in_t, v_in_t, "adam_w_in")
    a_uq = big_t(r_uq, w_uq_t, m_uq_t, v_uq_t, "adam_w_uq")
    a_ukv = big(r_ukv, w_ukv, m_w_ukv, v_w_ukv, "adam_w_ukv")
    a_out = big(r_out, w_out, m_w_out, v_w_out, "adam_w_out")
    a_up = big_t(r_up, w_up_t, m_up_t, v_up_t, "adam_w_up")
    a_down = big(r_down, w_down, m_w_down, v_w_down, "adam_w_down")
    ada_cols = w_ada.shape[2]
    d_ada_cols = lax.dynamic_slice(d_ada_all.reshape(N_DEV, 6 * D_MODEL), (0, my_slot * ada_cols), (N_DEV, ada_cols))
    a_ada = tuple(t[None] for t in _adam_w_ada(c_act.T, d_ada_cols, w_ada[0], m_w_ada[0], v_w_ada[0]))
    a_cw = big(r_cw, conv_w, m_conv_w, v_conv_w, "adam_conv_w")

    def small(k):
        return small_out[k]

    per_weight = [a_ada, small(0), small(1), small(2), a_in, small(5), a_uq, small(6), a_ukv, small(7), small(8),
                  small(9), small(10), a_out, small(3), small(4), a_up, a_cw, small(11), a_down]
    outs = [loss, grad_x[None]]
    for k in range(4):
        outs += [t[k] for t in per_weight]
    return tuple(outs)
```

```python
import functools

import jax
import jax.numpy as jnp
from jax import lax
from jax.experimental import pallas as pl
from jax.experimental.pallas import tpu as pltpu

F32 = jnp.float32
MXU = jnp.bfloat16

N_DEV = 8
D_MODEL = 1024
HEADS = 8
HEAD_PAD = 128
NOPE = 64
ROPE = 32
Q_LORA = 256
KV_LORA = 128
GM_WIDTH = 512
GM_DIM = 64
GM_CHUNK = 128
CHUNK_SHIFT = 6
ROPE_THETA = 10000.0
ATTN_SCALE = (NOPE + ROPE) ** -0.5
LOG2E = 1.4426950408889634
SCALE_LOG2E = ATTN_SCALE * LOG2E
Z_COLS = 1536
FF_BLK = 704
EPS = 1e-6
ADAM_LR = 0.001
ADAM_B1 = 0.9
ADAM_B2 = 0.999
ADAM_EPS = 1e-08
ADAM_WD = 0.01
ADAM_STEP = 10
VMEM_LIMIT = 56 * 1024 * 1024
MESH = pl.DeviceIdType.MESH


def _dot(a, b):
    return jnp.dot(a, b, preferred_element_type=F32)


def _dot_nt(a, b):
    return lax.dot_general(a, b, (((1,), (1,)), ((), ())), preferred_element_type=F32)


def _dot_tn(a, b):
    return lax.dot_general(a, b, (((0,), (0,)), ((), ())), preferred_element_type=F32)


def _call(body, *, name, grid, in_specs, out_specs, out_shape, scratch=(), sem=None):
    params = pltpu.CompilerParams(dimension_semantics=sem, vmem_limit_bytes=VMEM_LIMIT)
    return pl.pallas_call(body, name=name, grid=grid, in_specs=in_specs, out_specs=out_specs,
                          out_shape=out_shape, scratch_shapes=list(scratch), compiler_params=params)


def _full(shape):
    n = len(shape)
    return pl.BlockSpec(shape, lambda *_: (0,) * n)


def _rows(tm, cols, col_block=0):
    return pl.BlockSpec((tm, cols), lambda i: (i, col_block))


def _sds(shape, dtype):
    return jax.ShapeDtypeStruct(shape, dtype)


def _row(ref, k):
    return ref[pl.ds(k, 1), :]


def _rms(x):
    r = lax.rsqrt(jnp.mean(x * x, axis=-1, keepdims=True) + EPS)
    return x * r, r


def _rms_bwd(d_hat, hat, r):
    return r * (d_hat - hat * jnp.mean(d_hat * hat, axis=-1, keepdims=True))


def _rope_partner(t):
    lane = lax.broadcasted_iota(jnp.int32, t.shape, 1)
    swapped = jnp.where(lane < NOPE + ROPE // 2, -pltpu.roll(t, HEAD_PAD - ROPE // 2, 1), pltpu.roll(t, ROPE // 2, 1))
    return jnp.where((lane >= NOPE) & (lane < NOPE + ROPE), swapped, 0.0)


def _rope(t, cos, sin):
    return t * cos + _rope_partner(t) * sin


def _rope_transposed(g, cos, sin):
    return g * cos - _rope_partner(g * sin)


def _gelu(x):
    return x * (0.5 * (1.0 + jnp.tanh(0.7978845608028654 * (x + 0.044715 * (x * x * x)))))


def _gelu_grad(x):
    t = jnp.tanh(0.7978845608028654 * (x + 0.044715 * (x * x * x)))
    return 0.5 * (1.0 + t) + 0.5 * x * (1.0 - t * t) * (0.7978845608028654 * (1.0 + 3.0 * 0.044715 * (x * x)))


def _split_dot(x, mat):
    hi = x.astype(MXU)
    lo = (x - hi.astype(F32)).astype(MXU)
    return _dot(hi, mat) + _dot(lo, mat)


def _split_dot3(x, mat):
    hi = x.astype(MXU)
    r1 = x - hi.astype(F32)
    mid = r1.astype(MXU)
    lo = (r1 - mid.astype(F32)).astype(MXU)
    return (_dot(hi, mat) + _dot(mid, mat)) + _dot(lo, mat)


def _seg_matrix():
    r = lax.broadcasted_iota(jnp.int32, (GM_WIDTH, GM_WIDTH), 0) >> 6
    c = lax.broadcasted_iota(jnp.int32, (GM_WIDTH, GM_WIDTH), 1) >> 6
    return jnp.where(r == c, 1.0 / GM_DIM, 0.0).astype(MXU)


def _spatial_mask():
    i = lax.broadcasted_iota(jnp.int32, (GM_CHUNK, GM_CHUNK), 0) >> CHUNK_SHIFT
    j = lax.broadcasted_iota(jnp.int32, (GM_CHUNK, GM_CHUNK), 1) >> CHUNK_SHIFT
    return (j <= i).astype(F32)


def _head_lane_mask(h, rows):
    lane = lax.broadcasted_iota(jnp.int32, (rows, GM_WIDTH), 1) >> 6
    return lane == h


def _my_place():
    return lax.axis_index("x"), lax.axis_index("y"), lax.axis_index("c")


def _flat(p):
    return 4 * p[0] + 2 * p[1] + p[2]


def _comm_sems(n):
    return [pltpu.SemaphoreType.DMA((7 * n,)), pltpu.SemaphoreType.DMA((7 * n,)), pltpu.SemaphoreType.DMA((n,))]


def _gather_steps(ins, outs, sems):
    send_sems, recv_sems, local_sems = sems
    n = len(ins)
    x, y, c = _my_place()
    me, sibling = (x, y, c), (x, y, 1 - c)
    chips = [(1 - x, y), (x, 1 - y), (1 - x, 1 - y)]

    def copy(a, k, block, to, src=None):
        slot = outs[a].at[_flat(block)]
        return pltpu.make_async_remote_copy(
            src_ref=slot if src is None else src, dst_ref=slot,
            send_sem=send_sems.at[7 * a + k], recv_sem=recv_sems.at[7 * a + k],
            device_id=to, device_id_type=MESH)

    def mine():
        return [pltpu.make_async_copy(ins[a], outs[a].at[_flat(me)], local_sems.at[a]) for a in range(n)]

    def first():
        cps = []
        for a in range(n):
            cps.append(copy(a, 0, me, sibling, src=ins[a]))
            cps += [copy(a, 1 + j, me, (*chip, c), src=ins[a]) for j, chip in enumerate(chips)]
        return cps

    def passed():
        return [copy(a, 4 + j, (*chip, c), sibling) for a in range(n) for j, chip in enumerate(chips)]

    def start():
        for cp in mine() + first():
            cp.start()

    def forward():
        for a in range(n):
            for j, chip in enumerate(chips):
                copy(a, 1 + j, (*chip, c), me).wait_recv()
                copy(a, 4 + j, (*chip, c), sibling).start()

    def finish():
        for a in range(n):
            copy(a, 0, sibling, me).wait_recv()
            for j, chip in enumerate(chips):
                copy(a, 4 + j, (*chip, 1 - c), me).wait_recv()
        for cp in first() + passed():
            cp.wait_send()
        for cp in mine():
            cp.wait()

    return start, forward, finish


def _scatter_steps(ins, outs, sems, slots):
    send_sems, recv_sems, local_sems = sems
    n = len(ins)
    flips = [(fx, fy, fc) for fx in (0, 1) for fy in (0, 1) for fc in (0, 1)][1:]
    me = _my_place()

    def peer(f):
        return tuple(1 - v if b else v for v, b in zip(me, f))

    def copy(a, k, arriving=False):
        p = peer(flips[k])
        return pltpu.make_async_remote_copy(
            src_ref=ins[a].at[slots[a](_flat(p))], dst_ref=outs[a].at[_flat(p if arriving else me)],
            send_sem=send_sems.at[7 * a + k], recv_sem=recv_sems.at[7 * a + k],
            device_id=p, device_id_type=MESH)

    def mine():
        return [pltpu.make_async_copy(ins[a].at[slots[a](_flat(me))], outs[a].at[_flat(me)], local_sems.at[a])
                for a in range(n)]

    def start():
        for cp in mine() + [copy(a, k) for a in range(n) for k in range(7)]:
            cp.start()

    def finish():
        for a in range(n):
            for k in range(7):
                copy(a, k, arriving=True).wait_recv()
        for a in range(n):
            for k in range(7):
                copy(a, k).wait_send()
        for cp in mine():
            cp.wait()

    return start, finish


def _plain_slot(j):
    return (j,)


def _scatter_out_shape(arr, slot):
    return _sds((N_DEV,) + arr.shape[len(slot(0)):], arr.dtype)


def _exchange(gathered, scattered, name):
    ng, ns = len(gathered), len(scattered)
    slots = [slot for _, slot in scattered]

    def body(*refs):
        g_in, s_in = refs[:ng], refs[ng:ng + ns]
        g_out, s_out = refs[ng + ns:2 * ng + ns], refs[2 * ng + ns:2 * (ng + ns)]
        sems = refs[2 * (ng + ns):]
        g_start, g_forward, g_finish = _gather_steps(g_in, g_out, sems[:3])
        s_start, s_finish = _scatter_steps(s_in, s_out, sems[3:], slots)
        g_start()
        s_start()
        g_forward()
        g_finish()
        s_finish()

    any_spec = pl.BlockSpec(memory_space=pl.ANY)
    outs = pl.pallas_call(
        body, name=name,
        in_specs=[any_spec] * (ng + ns), out_specs=[any_spec] * (ng + ns),
        out_shape=[_sds((N_DEV,) + a.shape, a.dtype) for a in gathered]
        + [_scatter_out_shape(a, slot) for a, slot in scattered],
        scratch_shapes=_comm_sems(max(ng, 1)) + _comm_sems(max(ns, 1)),
    )(*gathered, *[a for a, _ in scattered])
    return outs[:ng], outs[ng:]


def _ada_fwd(c_all, w_ada):
    def body(c_ref, w_ref, part_ref, act_ref):
        cv = c_ref[...]
        act = cv * jax.nn.sigmoid(cv)
        act_ref[...] = act
        part_ref[...] = _dot(act.astype(MXU), w_ref[...].astype(MXU))

    cols = w_ada.shape[1]
    return _call(body, name="ada_fwd", grid=(1,),
                 in_specs=[_full(c_all.shape), _full(w_ada.shape)],
                 out_specs=[_full((N_DEV, cols)), _full(c_all.shape)],
                 out_shape=[_sds((N_DEV, cols), F32), _sds(c_all.shape, F32)])(c_all, w_ada)


def _mix_in_fwd(x, ada_raw, ada_b, g_pre, w1, g_q, g_kv, wq, wkv, cos_t, sin_t, tm):
    s = x.shape[0]

    def body(x_ref, ar_ref, ab_ref, g_ref, w1_ref, gq_ref, gkv_ref, wq_ref, wkv_ref, cos_ref, sin_ref,
             h1_ref, z_ref, qp_ref, kp_ref, vp_ref, cqn_ref, ckvn_ref):
        sh = _row(ar_ref, 0) + _row(ab_ref, 0)
        sc = _row(ar_ref, 1) + _row(ab_ref, 1)
        xn, _ = _rms(x_ref[...])
        hb = ((xn * g_ref[...]) * (1.0 + sc) + sh).astype(MXU)
        h1_ref[...] = hb
        z = _dot_nt(hb, w1_ref[...])
        z_ref[...] = z
        cos, sin = cos_ref[...], sin_ref[...]
        cqn = (_rms(z[:, :Q_LORA])[0] * gq_ref[...]).astype(MXU)
        ckvn = (_rms(z[:, Q_LORA:Q_LORA + KV_LORA])[0] * gkv_ref[...]).astype(MXU)
        cqn_ref[...] = cqn
        ckvn_ref[...] = ckvn
        q = _dot_nt(cqn, wq_ref[...])
        kv = _dot(ckvn, wkv_ref[...])
        k_rope = _rope(z[:, Q_LORA + KV_LORA:Q_LORA + KV_LORA + HEAD_PAD], cos, sin)
        for h in range(HEADS):
            blk = slice(h * HEAD_PAD, (h + 1) * HEAD_PAD)
            qp_ref[:, blk] = _rope(q[:, blk], cos, sin).astype(MXU)
            kp_ref[:, blk] = (kv[:, blk] + k_rope).astype(MXU)
        v_lane = lax.broadcasted_iota(jnp.int32, (tm, HEADS * HEAD_PAD), 1) & (HEAD_PAD - 1)
        vp_ref[...] = jnp.where(v_lane == NOPE, 1.0, kv[:, HEADS * HEAD_PAD:]).astype(MXU)

    hp = HEADS * HEAD_PAD
    return _call(
        body, name="mix_in_fwd", grid=(s // tm,), sem=("parallel",),
        in_specs=[_rows(tm, D_MODEL), _full(ada_raw.shape), _full(ada_b.shape), _full(g_pre.shape), _full(w1.shape),
                  _full(g_q.shape), _full(g_kv.shape), _full(wq.shape), _full(wkv.shape),
                  _rows(tm, HEAD_PAD), _rows(tm, HEAD_PAD)],
        out_specs=[_rows(tm, D_MODEL), _rows(tm, Z_COLS), _rows(tm, hp), _rows(tm, hp), _rows(tm, hp),
                   _rows(tm, Q_LORA), _rows(tm, KV_LORA)],
        out_shape=[_sds((s, D_MODEL), MXU), _sds((s, Z_COLS), F32), _sds((s, hp), MXU), _sds((s, hp), MXU),
                   _sds((s, hp), MXU), _sds((s, Q_LORA), MXU), _sds((s, KV_LORA), MXU)],
    )(x, ada_raw, ada_b, g_pre, w1, g_q, g_kv, wq, wkv, cos_t, sin_t)


def _gm_norm(zv, seg):
    gv = _gelu(zv)
    cen = gv - _split_dot(gv, seg)
    rstd = lax.rsqrt(_split_dot(cen * cen, seg) + EPS)
    return gv, cen * rstd, rstd


def _gm_mix(wm, vb, rows):
    out = jnp.zeros((rows, GM_WIDTH), F32)
    for h in range(HEADS):
        out = out + jnp.where(_head_lane_mask(h, rows), _dot(wm[h], vb), 0.0)
    return out


def _gmlp_fwd(z, ln_g, ln_b, w_sp, bias_exp, tm):
    s = z.shape[0]
    nblk = tm // GM_CHUNK

    def body(zu_ref, zv_ref, lg_ref, lb_ref, w_ref, be_ref, sgu_ref):
        seg = _seg_matrix()
        mask = _spatial_mask()
        wm = [(w_ref[h] * mask).astype(MXU) for h in range(HEADS)]
        gu = _gelu(zu_ref[...])
        _, vhat, _ = _gm_norm(zv_ref[...], seg)
        vln = (vhat * lg_ref[...] + lb_ref[...]).astype(MXU)
        for n in range(nblk):
            rows = slice(n * GM_CHUNK, (n + 1) * GM_CHUNK)
            mixed = _gm_mix(wm, vln[rows], GM_CHUNK) + be_ref[...]
            sgu_ref[rows, :] = (gu[rows] * mixed).astype(MXU)

    return _call(
        body, name="gmlp_fwd", grid=(s // tm,), sem=("parallel",),
        in_specs=[_rows(tm, GM_WIDTH, 1), _rows(tm, GM_WIDTH, 2), _full(ln_g.shape), _full(ln_b.shape),
                  _full(w_sp.shape), _full(bias_exp.shape)],
        out_specs=_rows(tm, GM_WIDTH), out_shape=_sds((s, GM_WIDTH), MXU),
    )(z, z, ln_g, ln_b, w_sp, bias_exp)


def _diag_mask(t):
    qc = lax.broadcasted_iota(jnp.int32, (t, t), 0) >> CHUNK_SHIFT
    kc = lax.broadcasted_iota(jnp.int32, (t, t), 1) >> CHUNK_SHIFT
    return kc <= qc


NEG_BIG = -1e30
ATTN_HEADS_PER_STEP = 2


def _attn_fwd(qp, kp, vp, tq, gathered):
    s = qp.shape[0]
    nq = s // tq
    hb = ATTN_HEADS_PER_STEP
    groups = HEADS // hb
    width = hb * HEAD_PAD
    ng = len(gathered)

    def body(q_ref, k_ref, v_ref, *rest):
        g_in, (o_ref, lse_ref), g_out = rest[:ng], rest[ng:ng + 2], rest[ng + 2:2 * ng + 2]
        m_sc, acc_sc = rest[2 * ng + 2:2 * ng + 4]
        g_start, g_forward, g_finish = _gather_steps(g_in, g_out, rest[2 * ng + 4:])
        g, i = pl.program_id(0), pl.program_id(1)
        pl.when((g == 0) & (i == 0))(g_start)
        pl.when((g == groups - 1) & (i == 0))(g_forward)
        m_sc[...] = jnp.full(m_sc.shape, NEG_BIG, F32)
        acc_sc[...] = jnp.zeros(acc_sc.shape, F32)

        def tile(j, masked):
            rows = pl.ds(pl.multiple_of(j * tq, tq), tq)
            for hh in range(hb):
                lanes = slice(hh * HEAD_PAD, (hh + 1) * HEAD_PAD)
                sc = _dot_nt(q_ref[:, lanes], k_ref[rows, lanes])
                if masked:
                    sc = jnp.where(_diag_mask(tq), sc, NEG_BIG)
                blocks = [sc[:, b * 128:(b + 1) * 128] for b in range(tq // 128)]
                m_prev = m_sc[hh]
                m_tile = jnp.max(functools.reduce(jnp.maximum, blocks), axis=-1, keepdims=True)
                m_new = jnp.maximum(m_prev, m_tile)
                alpha = jnp.exp2((m_prev - m_new) * SCALE_LOG2E)
                p = jnp.concatenate([jnp.exp2((b - m_new) * SCALE_LOG2E) for b in blocks], axis=1).astype(MXU)
                acc_sc[hh] = alpha * acc_sc[hh] + _dot(p, v_ref[rows, lanes])
                m_sc[hh] = m_new

        def off_diagonal_pair(p, carry):
            tile(2 * p, False)
            tile(2 * p + 1, False)
            return carry

        lax.fori_loop(0, i // 2, off_diagonal_pair, 0)

        @pl.when(i % 2 == 1)
        def _():
            tile(i - 1, False)

        tile(i, True)
        for hh in range(hb):
            lanes = slice(hh * HEAD_PAD, (hh + 1) * HEAD_PAD)
            acc = acc_sc[hh]
            denom = acc[:, NOPE:NOPE + 1]
            o_ref[:, lanes] = (acc / denom).astype(MXU)
            lse_ref[hh] = m_sc[hh][:, :1] * SCALE_LOG2E + jnp.log(denom) * LOG2E
        pl.when((g == groups - 1) & (i == nq - 1))(g_finish)

    q_spec = pl.BlockSpec((tq, width), lambda g, i: (i, g))
    kv_spec = pl.BlockSpec((s, width), lambda g, i: (0, g))
    any_spec = pl.BlockSpec(memory_space=pl.ANY)
    outs = _call(
        body, name="attn_fwd", grid=(groups, nq), sem=("arbitrary", "arbitrary"),
        in_specs=[q_spec, kv_spec, kv_spec] + [any_spec] * ng,
        out_specs=[q_spec, pl.BlockSpec((hb, tq, 1), lambda g, i: (g, i, 0))] + [any_spec] * ng,
        out_shape=[_sds(qp.shape, MXU), _sds((HEADS, s, 1), F32)]
        + [_sds((N_DEV,) + a.shape, a.dtype) for a in gathered],
        scratch=[pltpu.VMEM((hb, tq, HEAD_PAD), F32), pltpu.VMEM((hb, tq, HEAD_PAD), F32)] + _comm_sems(ng),
    )(qp, kp, vp, *gathered)
    return outs[0], outs[1], outs[2:]


def _out_proj_fwd(o_pad, sgu, wo, x, ada_raw, ada_b, g_post_mix, g_pre_ffn, tm):
    s = x.shape[0]
    hp = HEADS * HEAD_PAD

    def body(o_ref, sgu_ref, wo_ref, x_ref, ar_ref, ab_ref, gpm_ref, gpf_ref, m_ref, x2_ref, h2_ref):
        gt1 = _row(ar_ref, 2) + _row(ab_ref, 2)
        sh2 = _row(ar_ref, 3) + _row(ab_ref, 3)
        sc2 = _row(ar_ref, 4) + _row(ab_ref, 4)
        m = _dot(o_ref[...], wo_ref[pl.ds(0, hp), :]) + _dot(sgu_ref[...], wo_ref[pl.ds(hp, GM_WIDTH), :])
        m_ref[...] = m
        x2 = x_ref[...] + gt1 * (_rms(m)[0] * gpm_ref[...])
        x2_ref[...] = x2
        h2_ref[...] = ((_rms(x2)[0] * gpf_ref[...]) * (1.0 + sc2) + sh2).astype(MXU)

    return _call(
        body, name="out_proj_fwd", grid=(s // tm,), sem=("parallel",),
        in_specs=[_rows(tm, hp), _rows(tm, GM_WIDTH), _full(wo.shape), _rows(tm, D_MODEL), _full(ada_raw.shape),
                  _full(ada_b.shape), _full(g_post_mix.shape), _full(g_pre_ffn.shape)],
        out_specs=[_rows(tm, D_MODEL)] * 3,
        out_shape=[_sds((s, D_MODEL), F32), _sds((s, D_MODEL), F32), _sds((s, D_MODEL), MXU)],
    )(o_pad, sgu, wo, x, ada_raw, ada_b, g_post_mix, g_pre_ffn)


def _conv(u, halo, cw_ref, cb_ref):
    ext = jnp.concatenate([halo, u], axis=0)
    m1, m2 = pltpu.roll(ext, 1, 0)[8:], pltpu.roll(ext, 2, 0)[8:]
    return cb_ref[0] + ((m2 * cw_ref[0, pl.ds(0, 1), :] + m1 * cw_ref[0, pl.ds(1, 1), :]) + u * cw_ref[0, pl.ds(2, 1), :])


ROW_SUB = 256


def _sub_blocks(tm):
    return [slice(r, r + ROW_SUB) for r in range(0, tm, ROW_SUB)]


def _ffn_up_fwd(h2, w_up, conv_w, conv_b, tm):
    s = h2.shape[0]
    half = N_DEV // 2

    def body(h_ref, wa_ref, wb_ref, cwa_ref, cwb_ref, cba_ref, cbb_ref,
             ua_ref, ub_ref, ya_ref, yb_ref, act_ref, halo_a, halo_b):
        i = pl.program_id(1)

        @pl.when(i == 0)
        def _():
            halo_a[...] = jnp.zeros(halo_a.shape, F32)
            halo_b[...] = jnp.zeros(halo_b.shape, F32)

        ha, hb = halo_a[...], halo_b[...]
        for rows in _sub_blocks(tm):
            h = h_ref[rows, :]
            ua = _dot_nt(h, wa_ref[0])
            ub = _dot_nt(h, wb_ref[0])
            ua_ref[0, rows, :] = ua
            ub_ref[0, rows, :] = ub
            ya = _conv(ua, ha, cwa_ref, cba_ref)
            yb = _conv(ub, hb, cwb_ref, cbb_ref)
            ya_ref[0, rows, :] = ya
            yb_ref[0, rows, :] = yb
            ha, hb = ua[ROW_SUB - 8:], ub[ROW_SUB - 8:]
            act_ref[0, rows, :] = ((ya * jax.nn.sigmoid(ya)) * yb).astype(MXU)
        halo_a[...] = ha
        halo_b[...] = hb

    def blk(shape, off):
        return pl.BlockSpec(shape, lambda j, i: (j + off, 0, 0))

    def tok(off=0):
        return pl.BlockSpec((1, tm, FF_BLK), lambda j, i: (j + off, i, 0))

    return _call(
        body, name="ffn_up_fwd", grid=(half, s // tm), sem=("parallel", "arbitrary"),
        in_specs=[pl.BlockSpec((tm, D_MODEL), lambda j, i: (i, 0)),
                  blk((1, FF_BLK, D_MODEL), 0), blk((1, FF_BLK, D_MODEL), half),
                  blk((1, 3, FF_BLK), 0), blk((1, 3, FF_BLK), half), blk((1, 1, FF_BLK), 0), blk((1, 1, FF_BLK), half)],
        out_specs=[tok()] * 5,
        out_shape=[_sds((half, s, FF_BLK), F32)] * 4 + [_sds((half, s, FF_BLK), MXU)],
        scratch=[pltpu.VMEM((8, FF_BLK), F32), pltpu.VMEM((8, FF_BLK), F32)],
    )(h2, w_up, w_up, conv_w, conv_w, conv_b, conv_b)


def _ffn_down_fwd(act, wd, x2, target, ada_raw, ada_b, g_post_ffn, tm):
    s = x2.shape[0]
    half = N_DEV // 2

    def body(act_ref, wd_ref, x2_ref, t_ref, ar_ref, ab_ref, g_ref, dout_ref, df_ref, loss_ref, dgt_ref, dg_ref):
        i = pl.program_id(0)

        @pl.when(i == 0)
        def _():
            loss_ref[...] = jnp.zeros(loss_ref.shape, F32)
            dgt_ref[...] = jnp.zeros(dgt_ref.shape, F32)
            dg_ref[...] = jnp.zeros(dg_ref.shape, F32)

        gt2 = _row(ar_ref, 5) + _row(ab_ref, 5)
        g = g_ref[...]
        for rows in _sub_blocks(tm):
            f = _dot(act_ref[0, rows, :], wd_ref[0])
            for j in range(1, half):
                f = f + _dot(act_ref[j, rows, :], wd_ref[j])
            fhat, rf = _rms(f)
            fn = fhat * g
            err = (x2_ref[rows, :] + gt2 * fn) - t_ref[rows, :]
            loss_ref[...] += 0.5 * jnp.sum(jnp.mean(err * err, axis=-1, keepdims=True))
            d_out = err * (1.0 / D_MODEL)
            dout_ref[rows, :] = d_out
            dgt_ref[...] += jnp.sum(d_out * fn, axis=0, keepdims=True)
            d_fn = d_out * gt2
            dg_ref[...] += jnp.sum(d_fn * fhat, axis=0, keepdims=True)
            df_ref[rows, :] = _rms_bwd(d_fn * g, fhat, rf).astype(MXU)

    vec = pl.BlockSpec((1, D_MODEL), lambda i: (0, 0))
    return _call(
        body, name="ffn_down_fwd", grid=(s // tm,), sem=("arbitrary",),
        in_specs=[pl.BlockSpec((half, tm, FF_BLK), lambda i: (0, i, 0)), _full(wd.shape), _rows(tm, D_MODEL),
                  _rows(tm, D_MODEL), _full(ada_raw.shape), _full(ada_b.shape), _full(g_post_ffn.shape)],
        out_specs=[_rows(tm, D_MODEL), _rows(tm, D_MODEL), pl.BlockSpec((1, 128), lambda i: (0, 0)), vec, vec],
        out_shape=[_sds((s, D_MODEL), F32), _sds((s, D_MODEL), MXU), _sds((1, 128), F32),
                   _sds((1, D_MODEL), F32), _sds((1, D_MODEL), F32)],
    )(act, wd, x2, target, ada_raw, ada_b, g_post_ffn)


def _ffn_down_bwd(d_f, wd, up_a, up_b, y_a, y_b, conv_w, tm):
    s = d_f.shape[0]
    half = N_DEV // 2
    nt = s // tm

    def body(df_ref, wd_ref, ua_ref, ub_ref, ya_ref, yb_ref, cwa_ref, cwb_ref,
             dup_ref, dcw_ref, dcb_ref, next_a, next_b):
        i = pl.program_id(1)

        @pl.when(i == 0)
        def _():
            next_a[...] = jnp.zeros(next_a.shape, F32)
            next_b[...] = jnp.zeros(next_b.shape, F32)
            dcw_ref[...] = jnp.zeros(dcw_ref.shape, F32)
            dcb_ref[...] = jnp.zeros(dcb_ref.shape, F32)

        def conv_bwd(d_y, u, nxt, cw_ref, part, rows):
            ext = jnp.concatenate([d_y, nxt], axis=0)
            p1 = pltpu.roll(ext, ROW_SUB + 7, 0)[:ROW_SUB]
            p2 = pltpu.roll(ext, ROW_SUB + 6, 0)[:ROW_SUB]
            d_u = (d_y * cw_ref[0, pl.ds(2, 1), :] + p1 * cw_ref[0, pl.ds(1, 1), :]) + p2 * cw_ref[0, pl.ds(0, 1), :]
            dup_ref[0, part, rows, :] = d_u.astype(MXU)
            dcb_ref[0, part] += jnp.sum(d_y, axis=0, keepdims=True)
            dcw_ref[0, part, pl.ds(0, 1), :] += jnp.sum(p2 * u, axis=0, keepdims=True)
            dcw_ref[0, part, pl.ds(1, 1), :] += jnp.sum(p1 * u, axis=0, keepdims=True)
            dcw_ref[0, part, pl.ds(2, 1), :] += jnp.sum(d_y * u, axis=0, keepdims=True)
            return d_y[:8]

        nxa, nxb = next_a[...], next_b[...]
        for rows in reversed(_sub_blocks(tm)):
            d_act = _dot_nt(df_ref[rows, :], wd_ref[0])
            ya, yb = ya_ref[0, rows, :], yb_ref[0, rows, :]
            sig = jax.nn.sigmoid(ya)
            d_ya = d_act * yb * (sig * (1.0 + ya * (1.0 - sig)))
            d_yb = d_act * (ya * sig)
            nxa = conv_bwd(d_ya, ua_ref[0, rows, :], nxa, cwa_ref, 0, rows)
            nxb = conv_bwd(d_yb, ub_ref[0, rows, :], nxb, cwb_ref, 1, rows)
        next_a[...] = nxa
        next_b[...] = nxb

    def rev(i):
        return nt - 1 - i

    def blk(shape, off):
        return pl.BlockSpec(shape, lambda j, i: (j + off, 0, 0))

    tok = pl.BlockSpec((1, tm, FF_BLK), lambda j, i: (j, rev(i), 0))
    acc3 = pl.BlockSpec((1, 2, 3, FF_BLK), lambda j, i: (j, 0, 0, 0))
    acc1 = pl.BlockSpec((1, 2, 1, FF_BLK), lambda j, i: (j, 0, 0, 0))
    return _call(
        body, name="ffn_down_bwd", grid=(half, nt), sem=("parallel", "arbitrary"),
        in_specs=[pl.BlockSpec((tm, D_MODEL), lambda j, i: (rev(i), 0)), blk((1, FF_BLK, D_MODEL), 0),
                  tok, tok, tok, tok, blk((1, 3, FF_BLK), 0), blk((1, 3, FF_BLK), half)],
        out_specs=[pl.BlockSpec((1, 2, tm, FF_BLK), lambda j, i: (j, 0, rev(i), 0)), acc3, acc1],
        out_shape=[_sds((half, 2, s, FF_BLK), MXU), _sds((half, 2, 3, FF_BLK), F32), _sds((half, 2, 1, FF_BLK), F32)],
        scratch=[pltpu.VMEM((8, FF_BLK), F32), pltpu.VMEM((8, FF_BLK), F32)],
    )(d_f, wd, up_a, up_b, y_a, y_b, conv_w, conv_w)


def _ffn_up_bwd(d_up, w_up, x2, m, d_out, ada_raw, ada_b, g_pre_ffn, g_post_mix, tm):
    s = x2.shape[0]
    half = N_DEV // 2

    def body(dup_ref, w_ref, x2_ref, m_ref, dout_ref, ar_ref, ab_ref, gpf_ref, gpm_ref,
             dx_ref, dm_ref, dsh_ref, dsc_ref, dgpf_ref, dgt1_ref, dgpm_ref):
        i = pl.program_id(0)

        @pl.when(i == 0)
        def _():
            for r in (dsh_ref, dsc_ref, dgpf_ref, dgt1_ref, dgpm_ref):
                r[...] = jnp.zeros(r.shape, F32)

        gt1 = _row(ar_ref, 2) + _row(ab_ref, 2)
        sc2 = _row(ar_ref, 4) + _row(ab_ref, 4)
        gpf, gpm = gpf_ref[...], gpm_ref[...]
        d_h2 = _dot(dup_ref[0, 0], w_ref[0])
        for j in range(1, half):
            d_h2 = d_h2 + _dot(dup_ref[j, 0], w_ref[j])
        for j in range(half):
            d_h2 = d_h2 + _dot(dup_ref[j, 1], w_ref[half + j])
        x2n, r2 = _rms(x2_ref[...])
        dsh_ref[...] += jnp.sum(d_h2, axis=0, keepdims=True)
        dsc_ref[...] += jnp.sum(d_h2 * (x2n * gpf), axis=0, keepdims=True)
        d_mod = d_h2 * (1.0 + sc2)
        dgpf_ref[...] += jnp.sum(d_mod * x2n, axis=0, keepdims=True)
        d_x2 = dout_ref[...] + _rms_bwd(d_mod * gpf, x2n, r2)
        dx_ref[...] = d_x2
        mhat, rm = _rms(m_ref[...])
        dgt1_ref[...] += jnp.sum(d_x2 * (mhat * gpm), axis=0, keepdims=True)
        d_mn = d_x2 * gt1
        dgpm_ref[...] += jnp.sum(d_mn * mhat, axis=0, keepdims=True)
        dm_ref[...] = _rms_bwd(d_mn * gpm, mhat, rm).astype(MXU)

    vec = pl.BlockSpec((1, D_MODEL), lambda i: (0, 0))
    tok = pl.BlockSpec((half, 2, tm, FF_BLK), lambda i: (0, 0, i, 0))
    return _call(
        body, name="ffn_up_bwd", grid=(s // tm,), sem=("arbitrary",),
        in_specs=[tok, _full(w_up.shape), _rows(tm, D_MODEL), _rows(tm, D_MODEL), _rows(tm, D_MODEL),
                  _full(ada_raw.shape), _full(ada_b.shape), _full(g_pre_ffn.shape), _full(g_post_mix.shape)],
        out_specs=[_rows(tm, D_MODEL), _rows(tm, D_MODEL), vec, vec, vec, vec, vec],
        out_shape=[_sds((s, D_MODEL), F32), _sds((s, D_MODEL), MXU)] + [_sds((1, D_MODEL), F32)] * 5,
    )(d_up, w_up, x2, m, d_out, ada_raw, ada_b, g_pre_ffn, g_post_mix)


def _out_proj_bwd(d_m, wo, o_pad, tm):
    s = d_m.shape[0]
    hp = HEADS * HEAD_PAD

    def body(dm_ref, wo_ref, o_ref, do_ref, dsgu_ref, delta_ref):
        d_cat = _dot_nt(dm_ref[...], wo_ref[...])
        d_o = d_cat[:, :hp]
        do_ref[...] = d_o.astype(MXU)
        dsgu_ref[...] = d_cat[:, hp:]
        prod = d_o * o_ref[...].astype(F32)
        for h in range(HEADS):
            delta_ref[h] = jnp.sum(prod[:, h * HEAD_PAD:(h + 1) * HEAD_PAD], axis=-1, keepdims=True)

    return _call(
        body, name="out_proj_bwd", grid=(s // tm,), sem=("parallel",),
        in_specs=[_rows(tm, D_MODEL), _full(wo.shape), _rows(tm, hp)],
        out_specs=[_rows(tm, hp), _rows(tm, GM_WIDTH), pl.BlockSpec((HEADS, tm, 1), lambda i: (0, i, 0))],
        out_shape=[_sds((s, hp), MXU), _sds((s, GM_WIDTH), F32), _sds((HEADS, s, 1), F32)],
    )(d_m, wo, o_pad)


def _attn_bwd(qp, kp, vp, d_o, lse, delta, tq, scattered, gathered):
    s = qp.shape[0]
    nq = s // tq
    hb = ATTN_HEADS_PER_STEP
    groups = HEADS // hb
    width = hb * HEAD_PAD
    ns, ng = len(scattered), len(gathered)
    nc = ns + ng
    slots = [slot for _, slot in scattered]

    def body(q_ref, k_ref, v_ref, do_ref, lse_ref, dl_ref, *rest):
        c_in, (dq_ref, dk_ref, dv_ref), c_out = rest[:nc], rest[nc:nc + 3], rest[nc + 3:2 * nc + 3]
        dk_sc, dv_sc = rest[2 * nc + 3:2 * nc + 5]
        sems = rest[2 * nc + 5:]
        s_start, s_finish = _scatter_steps(c_in[:ns], c_out[:ns], sems[:3], slots)
        g_start, g_forward, g_finish = _gather_steps(c_in[ns:], c_out[ns:], sems[3:])
        g, j = pl.program_id(0), pl.program_id(1)

        @pl.when((g == 0) & (j == 0))
        def _():
            s_start()
            g_start()

        pl.when((g == groups - 1) & (j == 0))(g_forward)

        @pl.when(j == 0)
        def _():
            dq_ref[...] = jnp.zeros(dq_ref.shape, F32)

        dk_sc[...] = jnp.zeros(dk_sc.shape, F32)
        dv_sc[...] = jnp.zeros(dv_sc.shape, F32)

        def tile(i, masked):
            rows = pl.ds(pl.multiple_of(i * tq, tq), tq)
            for hh in range(hb):
                lanes = slice(hh * HEAD_PAD, (hh + 1) * HEAD_PAD)
                q, do, k = q_ref[rows, lanes], do_ref[rows, lanes], k_ref[:, lanes]
                sc = _dot_nt(q, k)
                if masked:
                    sc = jnp.where(_diag_mask(tq), sc, NEG_BIG)
                p = jnp.exp2(sc * SCALE_LOG2E - lse_ref[hh, rows, :])
                dv_sc[hh] += _dot_tn(p.astype(MXU), do)
                dp = _dot_nt(do, v_ref[:, lanes])
                ds = (p * (dp - dl_ref[hh, rows, :])).astype(MXU)
                dk_sc[hh] += _dot_tn(ds, q)
                dq_ref[rows, lanes] += _dot(ds, k) * ATTN_SCALE

        def off_diagonal_pair(p, carry):
            tile(j + 1 + 2 * p, False)
            tile(j + 2 + 2 * p, False)
            return carry

        below = nq - 1 - j
        tile(j, True)
        lax.fori_loop(0, below // 2, off_diagonal_pair, 0)

        @pl.when(below % 2 == 1)
        def _():
            tile(nq - 1, False)
        for hh in range(hb):
            lanes = slice(hh * HEAD_PAD, (hh + 1) * HEAD_PAD)
            dk_ref[:, lanes] = dk_sc[hh] * ATTN_SCALE
            dv_ref[:, lanes] = dv_sc[hh]
        @pl.when((g == groups - 1) & (j == nq - 1))
        def _():
            g_finish()
            s_finish()

    seq_spec = pl.BlockSpec((s, width), lambda g, j: (0, g))
    kv_spec = pl.BlockSpec((tq, width), lambda g, j: (j, g))
    col_spec = pl.BlockSpec((hb, s, 1), lambda g, j: (g, 0, 0))
    any_spec = pl.BlockSpec(memory_space=pl.ANY)
    outs = _call(
        body, name="attn_bwd", grid=(groups, nq), sem=("arbitrary", "arbitrary"),
        in_specs=[seq_spec, kv_spec, kv_spec, seq_spec, col_spec, col_spec] + [any_spec] * nc,
        out_specs=[seq_spec, kv_spec, kv_spec] + [any_spec] * nc,
        out_shape=[_sds(qp.shape, F32), _sds(qp.shape, F32), _sds(qp.shape, F32)]
        + [_scatter_out_shape(a, slot) for a, slot in scattered]
        + [_sds((N_DEV,) + a.shape, a.dtype) for a in gathered],
        scratch=[pltpu.VMEM((hb, tq, HEAD_PAD), F32), pltpu.VMEM((hb, tq, HEAD_PAD), F32)]
        + _comm_sems(ns) + _comm_sems(ng),
    )(qp, kp, vp, d_o, lse, delta, *[a for a, _ in scattered], *gathered)
    return outs[0], outs[1], outs[2], outs[3:3 + ns], outs[3 + ns:]


def _gmlp_bwd(z, d_sgu, ln_g, ln_b, w_sp, bias_exp, tm):
    s = z.shape[0]
    nblk = tm // GM_CHUNK

    def body(zu_ref, zv_ref, dsgu_ref, lg_ref, lb_ref, w_ref, be_ref,
             dguv_ref, dws_ref, dbs_ref, dlg_ref, dlb_ref, dbe_sc, dvln_sc, dlg_sc, dlb_sc):
        i = pl.program_id(0)

        @pl.when(i == 0)
        def _():
            for r in (dws_ref, dlg_sc, dlb_sc, dbe_sc):
                r[...] = jnp.zeros(r.shape, F32)

        seg = _seg_matrix()
        mask = _spatial_mask()
        wm = [(w_ref[h] * mask).astype(MXU) for h in range(HEADS)]
        zu, zv = zu_ref[...], zv_ref[...]
        gu = _gelu(zu)
        _, vhat, rstd = _gm_norm(zv, seg)
        lg = lg_ref[...]
        vln = (vhat * lg + lb_ref[...]).astype(MXU)
        d_sgu = dsgu_ref[...]
        for n in range(nblk):
            rows = slice(n * GM_CHUNK, (n + 1) * GM_CHUNK)
            vb = vln[rows]
            mixed = _gm_mix(wm, vb, GM_CHUNK) + be_ref[...]
            d_mixed = d_sgu[rows] * gu[rows]
            dguv_ref[rows, pl.ds(0, GM_WIDTH)] = ((d_sgu[rows] * mixed) * _gelu_grad(zu[rows])).astype(MXU)
            dbe_sc[...] += d_mixed
            dmb = d_mixed.astype(MXU)
            d_vln = jnp.zeros((GM_CHUNK, GM_WIDTH), F32)
            for h in range(HEADS):
                hm = _head_lane_mask(h, GM_CHUNK)
                dws_ref[h] += _dot_nt(jnp.where(hm, dmb, jnp.zeros_like(dmb)), vb)
                d_vln = d_vln + jnp.where(hm, _dot_tn(wm[h], dmb), 0.0)
            dvln_sc[rows, :] = d_vln
        d_vln = dvln_sc[...]
        dlg_sc[...] += jnp.sum(d_vln * vhat, axis=0, keepdims=True)
        dlb_sc[...] += jnp.sum(d_vln, axis=0, keepdims=True)
        d_vhat = d_vln * lg
        d_gv = rstd * ((d_vhat - _split_dot(d_vhat, seg)) - vhat * _split_dot(d_vhat * vhat, seg))
        dguv_ref[:, pl.ds(GM_WIDTH, GM_WIDTH)] = (d_gv * _gelu_grad(zv)).astype(MXU)

        @pl.when(i == pl.num_programs(0) - 1)
        def _():
            for h in range(HEADS):
                dws_ref[h] = dws_ref[h] * mask
            hrow = lax.broadcasted_iota(jnp.int32, (HEADS, GM_WIDTH), 0)
            hlane = lax.broadcasted_iota(jnp.int32, (HEADS, GM_WIDTH), 1) >> 6
            ind = jnp.where(hrow == hlane, 1.0, 0.0).astype(MXU)
            acc = dbe_sc[...]
            hi = acc.astype(MXU)
            lo = (acc - hi.astype(F32)).astype(MXU)
            dbs_ref[...] = _dot_nt(ind, hi) + _dot_nt(ind, lo)
            pick = (lax.broadcasted_iota(jnp.int32, (GM_WIDTH, GM_DIM), 0) & (GM_DIM - 1)
                    == lax.broadcasted_iota(jnp.int32, (GM_WIDTH, GM_DIM), 1))
            pick = jnp.where(pick, 1.0, 0.0).astype(MXU)
            for src, dst in ((dlg_sc, dlg_ref), (dlb_sc, dlb_ref)):
                spread = jnp.where(hrow == hlane, jnp.broadcast_to(src[...], (HEADS, GM_WIDTH)), 0.0)
                dst[...] = _split_dot3(spread, pick)

    return _call(
        body, name="gmlp_bwd", grid=(s // tm,), sem=("arbitrary",),
        in_specs=[_rows(tm, GM_WIDTH, 1), _rows(tm, GM_WIDTH, 2), _rows(tm, GM_WIDTH), _full(ln_g.shape),
                  _full(ln_b.shape), _full(w_sp.shape), _full(bias_exp.shape)],
        out_specs=[_rows(tm, 2 * GM_WIDTH), _full(w_sp.shape), _full((HEADS, GM_CHUNK)), _full((HEADS, GM_DIM)),
                   _full((HEADS, GM_DIM))],
        out_shape=[_sds((s, 2 * GM_WIDTH), MXU), _sds(w_sp.shape, F32), _sds((HEADS, GM_CHUNK), F32),
                   _sds((HEADS, GM_DIM), F32), _sds((HEADS, GM_DIM), F32)],
        scratch=[pltpu.VMEM((GM_CHUNK, GM_WIDTH), F32), pltpu.VMEM((tm, GM_WIDTH), F32),
                 pltpu.VMEM((1, GM_WIDTH), F32), pltpu.VMEM((1, GM_WIDTH), F32)],
    )(z, z, d_sgu, ln_g, ln_b, w_sp, bias_exp)


def _mix_in_bwd(dq, dk, dv, z, d_guv, x, d_x_part, ada_raw, ada_b, g_pre, g_q, g_kv, w1t, wqt, wkv,
                cos_t, sin_t, tm):
    s = x.shape[0]
    hp = HEADS * HEAD_PAD
    za = Q_LORA + KV_LORA + HEAD_PAD

    def body(dq_ref, dk_ref, dv_ref, z_ref, dguv_ref, x_ref, dxp_ref, ar_ref, ab_ref, g_ref, gq_ref, gkv_ref,
             w1_ref, wq_ref, wkv_ref, cos_ref, sin_ref,
             gx_ref, dza_ref, dqp_ref, dkvp_ref, dsh_ref, dsc_ref, dg_ref, dgq_ref, dgkv_ref):
        i = pl.program_id(0)

        @pl.when(i == 0)
        def _():
            for r in (dsh_ref, dsc_ref, dg_ref, dgq_ref, dgkv_ref):
                r[...] = jnp.zeros(r.shape, F32)

        cos, sin = cos_ref[...], sin_ref[...]
        d_krot = jnp.zeros((tm, HEAD_PAD), F32)
        for h in range(HEADS):
            blk = slice(h * HEAD_PAD, (h + 1) * HEAD_PAD)
            dqp_ref[:, blk] = _rope_transposed(dq_ref[:, blk], cos, sin).astype(MXU)
            dk_h = dk_ref[:, blk]
            d_krot = d_krot + dk_h
            dkvp_ref[:, blk] = dk_h.astype(MXU)
        dkvp_ref[:, pl.ds(hp, hp)] = dv_ref[...].astype(MXU)
        lane = lax.broadcasted_iota(jnp.int32, (tm, HEAD_PAD), 1)
        d_kr = jnp.where((lane >= NOPE) & (lane < NOPE + ROPE), _rope_transposed(d_krot, cos, sin), 0.0)
        d_cqn = _dot(dqp_ref[...], wq_ref[...])
        d_ckvn = _dot_nt(dkvp_ref[...], wkv_ref[...])
        zt = z_ref[...]
        gq, gkv = gq_ref[...], gkv_ref[...]
        cq_hat, rq = _rms(zt[:, :Q_LORA])
        ckv_hat, rkv = _rms(zt[:, Q_LORA:Q_LORA + KV_LORA])
        dgq_ref[...] += jnp.sum(d_cqn * cq_hat, axis=0, keepdims=True)
        dgkv_ref[...] += jnp.sum(d_ckvn * ckv_hat, axis=0, keepdims=True)
        d_cq = _rms_bwd(d_cqn * gq, cq_hat, rq)
        d_ckv = _rms_bwd(d_ckvn * gkv, ckv_hat, rkv)
        d_za = jnp.concatenate([d_cq, d_ckv, d_kr], axis=1).astype(MXU)
        dza_ref[...] = d_za
        d_h1 = _dot(d_za, w1_ref[pl.ds(0, za), :]) + _dot(dguv_ref[...], w1_ref[pl.ds(za, 2 * GM_WIDTH), :])
        sc1 = _row(ar_ref, 1) + _row(ab_ref, 1)
        g = g_ref[...]
        xn, r1 = _rms(x_ref[...])
        dsh_ref[...] += jnp.sum(d_h1, axis=0, keepdims=True)
        dsc_ref[...] += jnp.sum(d_h1 * (xn * g), axis=0, keepdims=True)
        d_mod = d_h1 * (1.0 + sc1)
        dg_ref[...] += jnp.sum(d_mod * xn, axis=0, keepdims=True)
        gx_ref[...] = dxp_ref[...] + _rms_bwd(d_mod * g, xn, r1)

    vec = pl.BlockSpec((1, D_MODEL), lambda i: (0, 0))
    return _call(
        body, name="mix_in_bwd", grid=(s // tm,), sem=("arbitrary",),
        in_specs=[_rows(tm, hp), _rows(tm, hp), _rows(tm, hp), _rows(tm, za), _rows(tm, 2 * GM_WIDTH),
                  _rows(tm, D_MODEL), _rows(tm, D_MODEL), _full(ada_raw.shape), _full(ada_b.shape), _full(g_pre.shape),
                  _full(g_q.shape), _full(g_kv.shape), _full(w1t.shape), _full(wqt.shape),
                  _full(wkv.shape), _rows(tm, HEAD_PAD), _rows(tm, HEAD_PAD)],
        out_specs=[_rows(tm, D_MODEL), _rows(tm, za), _rows(tm, hp), _rows(tm, 2 * hp), vec, vec, vec,
                   _full(g_q.shape), _full(g_kv.shape)],
        out_shape=[_sds((s, D_MODEL), F32), _sds((s, za), MXU), _sds((s, hp), MXU), _sds((s, 2 * hp), MXU),
                   _sds((1, D_MODEL), F32), _sds((1, D_MODEL), F32), _sds((1, D_MODEL), F32),
                   _sds(g_q.shape, F32), _sds(g_kv.shape, F32)],
    )(dq, dk, dv, z, d_guv, x, d_x_part, ada_raw, ada_b, g_pre, g_q, g_kv, w1t, wqt, wkv, cos_t, sin_t)


def _tn_matmul(a, b, name, ts):
    ga, s, m = a.shape
    gb, _, n = b.shape
    g = max(ga, gb)
    tn = n if n <= 1024 else 1024
    steps = s // ts

    def body(a_ref, b_ref, o_ref, acc):
        k = pl.program_id(2)

        @pl.when(k == 0)
        def _():
            acc[...] = jnp.zeros(acc.shape, F32)

        acc[...] += _dot_tn(a_ref[0], b_ref[0])

        @pl.when(k == steps - 1)
        def _():
            o_ref[0] = acc[...].astype(MXU)

    return _call(
        body, name=name, grid=(g, n // tn, steps), sem=("parallel", "parallel", "arbitrary"),
        in_specs=[pl.BlockSpec((1, ts, m), lambda gi, ni, k: (gi if ga > 1 else 0, k, 0)),
                  pl.BlockSpec((1, ts, tn), lambda gi, ni, k: (gi if gb > 1 else 0, k, ni))],
        out_specs=pl.BlockSpec((1, m, tn), lambda gi, ni, k: (gi, 0, ni)),
        out_shape=_sds((g, m, n), MXU),
        scratch=[pltpu.VMEM((m, tn), F32)],
    )(a, b)


def _adamw(w, g, m, v):
    m2 = ADAM_B1 * m + (1.0 - ADAM_B1) * g
    v2 = ADAM_B2 * v + (1.0 - ADAM_B2) * (g * g)
    m_hat = m2 / (1.0 - ADAM_B1 ** ADAM_STEP)
    v_hat = v2 / (1.0 - ADAM_B2 ** ADAM_STEP)
    delta = -ADAM_LR * (m_hat / (jnp.sqrt(v_hat) + ADAM_EPS) + ADAM_WD * w)
    return delta, m2, v2


def _adam_reduce(recv, w, m, v, name):
    r, c = w.shape
    tr = r if r <= 512 else max(t for t in range(16, 513, 16) if r % t == 0)

    def body(p_ref, w_ref, m_ref, v_ref, g_ref, d_ref, mo_ref, vo_ref):
        g = p_ref[0].astype(F32)
        for j in range(1, N_DEV):
            g = g + p_ref[j].astype(F32)
        g_ref[...] = g
        d_ref[...], mo_ref[...], vo_ref[...] = _adamw(w_ref[...], g, m_ref[...], v_ref[...])

    blk = pl.BlockSpec((tr, c), lambda i: (i, 0))
    return _call(
        body, name=name, grid=(r // tr,), sem=("parallel",),
        in_specs=[pl.BlockSpec((N_DEV, tr, c), lambda i: (0, i, 0)), blk, blk, blk],
        out_specs=[blk] * 4, out_shape=[_sds((r, c), F32)] * 4,
    )(recv, w, m, v)


def _adam_w_ada(c_act_t, d_ada_cols, w, m, v):
    r, c = w.shape
    tr = 256

    def body(ct_ref, da_ref, w_ref, m_ref, v_ref, g_ref, d_ref, mo_ref, vo_ref):
        g = ct_ref[:, pl.ds(0, 1)] * da_ref[pl.ds(0, 1), :]
        for b in range(1, N_DEV):
            g = g + ct_ref[:, pl.ds(b, 1)] * da_ref[pl.ds(b, 1), :]
        g_ref[...] = g
        d_ref[...], mo_ref[...], vo_ref[...] = _adamw(w_ref[...], g, m_ref[...], v_ref[...])

    blk = pl.BlockSpec((tr, c), lambda i: (i, 0))
    return _call(
        body, name="adam_w_ada", grid=(r // tr,), sem=("parallel",),
        in_specs=[pl.BlockSpec((tr, N_DEV), lambda i: (i, 0)), _full(d_ada_cols.shape), blk, blk, blk],
        out_specs=[blk] * 4, out_shape=[_sds((r, c), F32)] * 4,
    )(c_act_t, d_ada_cols, w, m, v)


VEC_ROWS = D_MODEL // 128
PK_ADA = 0
PK_GAIN = PK_ADA + 6 * VEC_ROWS
PK_GQ = PK_GAIN + 4 * VEC_ROWS
PK_GKV = PK_GQ + Q_LORA // 128
PK_LOSS = PK_GKV + KV_LORA // 128
PK_LNG = 88
PK_LNB = PK_LNG + HEADS
PK_BS = PK_LNB + HEADS
PK_CB = PK_BS + HEADS
CB_ROWS = 6
PK_WS = PK_CB + N_DEV * CB_ROWS
PK_ROWS = PK_WS + HEADS * GM_CHUNK
assert PK_LOSS < PK_LNG and PK_ROWS % 8 == 0
LATE_GAIN = 2 * VEC_ROWS
LATE_GQ = 3 * VEC_ROWS
LATE_GKV = LATE_GQ + Q_LORA // 128
LATE_ROWS = 32


def _cb_chunks():
    return [(k, k * 128, min(128, FF_BLK - k * 128)) for k in range(CB_ROWS)]


def _put_rows(out_ref, row0, ref, width):
    for k in range(width // 128):
        out_ref[pl.ds(row0 + k, 1), :] = ref[:, pl.ds(k * 128, 128)]


def _pack_small(ada_rows, gains, loss_part, d_ln_g, d_ln_b, d_bs, d_cb, d_ws):
    half = N_DEV // 2

    def body(*refs):
        vec_refs = refs[:7]
        loss_ref, lng_ref, lnb_ref, bs_ref, cb_ref, ws_ref, out_ref = refs[7:]
        out_ref[pl.ds(0, PK_WS), :] = jnp.zeros((PK_WS, 128), F32)
        for n, ref in enumerate(vec_refs[:4]):
            _put_rows(out_ref, PK_ADA + (2 + n) * VEC_ROWS, ref, D_MODEL)
        for n, ref in enumerate(vec_refs[4:]):
            _put_rows(out_ref, PK_GAIN + (1 + n) * VEC_ROWS, ref, D_MODEL)
        _put_rows(out_ref, PK_LOSS, loss_ref, 128)
        out_ref[pl.ds(PK_LNG, HEADS), pl.ds(0, GM_DIM)] = lng_ref[...]
        out_ref[pl.ds(PK_LNB, HEADS), pl.ds(0, GM_DIM)] = lnb_ref[...]
        out_ref[pl.ds(PK_BS, HEADS), :] = bs_ref[...]
        for j in range(N_DEV):
            for k, lane, width in _cb_chunks():
                out_ref[pl.ds(PK_CB + j * CB_ROWS + k, 1), pl.ds(0, width)] = cb_ref[j % half, j // half, :, pl.ds(lane, width)]
        for h in range(HEADS):
            out_ref[pl.ds(PK_WS + h * GM_CHUNK, GM_CHUNK), :] = ws_ref[h]

    ins = list(ada_rows) + list(gains) + [loss_part, d_ln_g, d_ln_b, d_bs, d_cb, d_ws]
    return _call(body, name="pack_small", grid=(1,), in_specs=[_full(a.shape) for a in ins],
                 out_specs=_full((PK_ROWS, 128)), out_shape=_sds((PK_ROWS, 128), F32))(*ins)


def _pack_late(d_sh1, d_sc1, d_g_pre_mix, d_g_q, d_g_kv):
    def body(sh_ref, sc_ref, g_ref, gq_ref, gkv_ref, out_ref):
        out_ref[...] = jnp.zeros((LATE_ROWS, 128), F32)
        _put_rows(out_ref, 0, sh_ref, D_MODEL)
        _put_rows(out_ref, VEC_ROWS, sc_ref, D_MODEL)
        _put_rows(out_ref, LATE_GAIN, g_ref, D_MODEL)
        _put_rows(out_ref, LATE_GQ, gq_ref, Q_LORA)
        _put_rows(out_ref, LATE_GKV, gkv_ref, KV_LORA)

    ins = [d_sh1, d_sc1, d_g_pre_mix, d_g_q, d_g_kv]
    return _call(body, name="pack_late", grid=(1,), in_specs=[_full(a.shape) for a in ins],
                 out_specs=_full((LATE_ROWS, 128)), out_shape=_sds((LATE_ROWS, 128), F32))(*ins)


def _adam_small(gathered, late, params):
    n_par = len(params)

    def body(p_ref, late_ref, *refs):
        ins = [refs[3 * n:3 * n + 3] for n in range(n_par)]
        outs = [refs[3 * n_par + 4 * n:3 * n_par + 4 * n + 4] for n in range(n_par)]
        loss_ref, dada_ref = refs[7 * n_par:]

        def total(rows, lanes=slice(None), src=p_ref):
            g = src[0, rows, lanes]
            for j in range(1, N_DEV):
                g = g + src[j, rows, lanes]
            return g

        def apply(n, g, idx):
            w_ref, m_ref, v_ref = ins[n]
            d, m2, v2 = _adamw(w_ref[idx], g, m_ref[idx], v_ref[idx])
            for ref, val in zip(outs[n], (g, d, m2, v2)):
                ref[idx] = val

        def vector(n, src, row0, width, lane0=0):
            for k in range(width // 128):
                apply(n, total(pl.ds(row0 + k, 1), src=src), (slice(None), pl.ds(lane0 + k * 128, 128)))

        vector(0, late_ref, 0, 2 * D_MODEL)
        vector(0, p_ref, PK_ADA + 2 * VEC_ROWS, 4 * D_MODEL, lane0=2 * D_MODEL)
        vector(1, late_ref, LATE_GAIN, D_MODEL)
        for n in range(1, 4):
            vector(1 + n, p_ref, PK_GAIN + n * VEC_ROWS, D_MODEL)
        vector(5, late_ref, LATE_GQ, Q_LORA)
        vector(6, late_ref, LATE_GKV, KV_LORA)
        apply(7, total(pl.ds(PK_LNG, HEADS), pl.ds(0, GM_DIM)), (0,))
        apply(8, total(pl.ds(PK_LNB, HEADS), pl.ds(0, GM_DIM)), (0,))
        for h in range(HEADS):
            apply(9, total(pl.ds(PK_WS + h * GM_CHUNK, GM_CHUNK)), (0, h))
        apply(10, total(pl.ds(PK_BS, HEADS)), (0,))
        for j in range(N_DEV):
            for k, lane, width in _cb_chunks():
                apply(11, total(pl.ds(PK_CB + j * CB_ROWS + k, 1), pl.ds(0, width)), (pl.ds(j, 1), pl.ds(lane, width)))
        loss_ref[...] = total(pl.ds(PK_LOSS, 1))
        dada_ref[:, pl.ds(0, 2 * VEC_ROWS), :] = late_ref[:, pl.ds(0, 2 * VEC_ROWS), :]
        dada_ref[:, pl.ds(2 * VEC_ROWS, 4 * VEC_ROWS), :] = p_ref[:, pl.ds(PK_ADA + 2 * VEC_ROWS, 4 * VEC_ROWS), :]

    flat = [a for triple in params for a in triple]
    out_shape = [_sds(w.shape, F32) for w, _, _ in params for _ in range(4)]
    out_shape += [_sds((1, 128), F32), _sds((N_DEV, 6 * VEC_ROWS, 128), F32)]
    outs = _call(body, name="adam_small", grid=(1,),
                 in_specs=[_full(gathered.shape), _full(late.shape)] + [_full(a.shape) for a in flat],
                 out_specs=[_full(o.shape) for o in out_shape], out_shape=out_shape)(gathered, late, *flat)
    return [tuple(outs[4 * n:4 * n + 4]) for n in range(n_par)], outs[-2], outs[-1]


def _rope_tables(s):
    pos = jnp.arange(s, dtype=F32)
    inv = ROPE_THETA ** (-jnp.arange(0, ROPE, 2, dtype=F32) / ROPE)
    ang = pos[:, None] * inv[None, :]
    cos, sin = jnp.cos(ang), jnp.sin(ang)
    ones, zeros = jnp.ones((s, NOPE), F32), jnp.zeros((s, NOPE), F32)
    cos_t = jnp.concatenate([ones, cos, cos, ones[:, :HEAD_PAD - NOPE - ROPE]], axis=1)
    sin_t = jnp.concatenate([zeros, sin, sin, zeros[:, :HEAD_PAD - NOPE - ROPE]], axis=1)
    return cos_t, sin_t


def kernel(x, c, w_ada, b_ada, g_pre_mix, g_post_mix, w_in, g_q, w_uq, g_kv, w_ukv, gm_ln_g, gm_ln_b, w_spatial, b_spatial, w_out, g_pre_ffn, g_post_ffn, w_up, conv_w, conv_b, w_down, loss_target, m_w_ada, m_b_ada, m_g_pre_mix, m_g_post_mix, m_w_in, m_g_q, m_w_uq, m_g_kv, m_w_ukv, m_gm_ln_g, m_gm_ln_b, m_w_spatial, m_b_spatial, m_w_out, m_g_pre_ffn, m_g_post_ffn, m_w_up, m_conv_w, m_conv_b, m_w_down, v_w_ada, v_b_ada, v_g_pre_mix, v_g_post_mix, v_w_in, v_g_q, v_w_uq, v_g_kv, v_w_ukv, v_gm_ln_g, v_gm_ln_b, v_w_spatial, v_b_spatial, v_w_out, v_g_pre_ffn, v_g_post_ffn, v_w_up, v_conv_w, v_conv_b, v_w_down):
    s = x.shape[1]
    tm = min(256, s)
    tf = min(2 * ROW_SUB, s)
    tq = min(512, s)
    ts = min(2048, s)
    hp = HEADS * HEAD_PAD
    half = N_DEV // 2
    my_slot = 4 * lax.axis_index("x") + 2 * lax.axis_index("y") + lax.axis_index("c")
    x2d, target = x[0], loss_target[0]

    def t_(a):
        return jnp.swapaxes(a[0], 0, 1)

    w_in_t, m_in_t, v_in_t = t_(w_in), t_(m_w_in), t_(v_w_in)
    w_uq_t, m_uq_t, v_uq_t = t_(w_uq), t_(m_w_uq), t_(v_w_uq)
    w_up_t, m_up_t, v_up_t = t_(w_up), t_(m_w_up), t_(v_w_up)
    (g_c, g_in_t, g_uq_t, g_ukv, g_cw), _ = _exchange(
        [c, w_in_t.astype(MXU), w_uq_t.astype(MXU), w_ukv[0].astype(MXU), conv_w[0]], [], "gather_mixer_weights")

    w_in_f = g_in_t.reshape(-1, D_MODEL)
    o1, o2, o3 = Q_LORA, Q_LORA + KV_LORA, Q_LORA + KV_LORA + ROPE
    w1t = jnp.concatenate([w_in_f[:o2], jnp.zeros((NOPE, D_MODEL), MXU), w_in_f[o2:o3],
                           jnp.zeros((HEAD_PAD - NOPE - ROPE, D_MODEL), MXU), w_in_f[o3:]], axis=0)
    wqt = jnp.pad(g_uq_t, ((0, 0), (0, HEAD_PAD - NOPE - ROPE), (0, 0))).reshape(hp, Q_LORA)
    w_ukv_f = jnp.transpose(g_ukv, (1, 0, 2)).reshape(KV_LORA, HEADS, 2 * NOPE)
    pad_head = ((0, 0), (0, 0), (0, HEAD_PAD - NOPE))
    wkv = jnp.concatenate([jnp.pad(w_ukv_f[:, :, :NOPE], pad_head).reshape(KV_LORA, hp),
                           jnp.pad(w_ukv_f[:, :, NOPE:], pad_head).reshape(KV_LORA, hp)], axis=1)
    cb8 = conv_b.reshape(N_DEV, 1, FF_BLK)
    bias_exp = jnp.repeat(b_spatial[0].T, GM_DIM, axis=1)
    ln_g, ln_b = gm_ln_g.reshape(1, GM_WIDTH), gm_ln_b.reshape(1, GM_WIDTH)
    w_sp = w_spatial[0]
    cos_t, sin_t = _rope_tables(s)

    ada_part, c_act = _ada_fwd(g_c.reshape(N_DEV, D_MODEL), w_ada[0])
    _, (ada_recv,) = _exchange([], [(ada_part.reshape(N_DEV, 1, -1), _plain_slot)], "ada_rows")
    ada_raw = ada_recv.reshape(6, D_MODEL)
    ada_b = b_ada.reshape(6, D_MODEL)

    h1, z, qp, kp, vp, cqn, ckvn = _mix_in_fwd(x2d, ada_raw, ada_b, g_pre_mix, w1t, g_q, g_kv, wqt, wkv, cos_t, sin_t, tm)
    sgu = _gmlp_fwd(z, ln_g, ln_b, w_sp, bias_exp, tm)
    o_pad, lse, (g_out, g_up, g_down) = _attn_fwd(
        qp, kp, vp, tq, [w_out[0].astype(MXU), w_up_t.astype(MXU), w_down[0].astype(MXU)])
    w_out_f = g_out.reshape(2 * GM_WIDTH, D_MODEL)
    wo_attn = jnp.pad(w_out_f[:GM_WIDTH].reshape(HEADS, NOPE, D_MODEL), ((0, 0), (0, HEAD_PAD - NOPE), (0, 0)))
    wo = jnp.concatenate([wo_attn.reshape(hp, D_MODEL), w_out_f[GM_WIDTH:]], axis=0)
    wd = g_down.reshape(half, FF_BLK, D_MODEL)
    m_mix, x2, h2 = _out_proj_fwd(o_pad, sgu, wo, x2d, ada_raw, ada_b, g_post_mix, g_pre_ffn, tm)
    up_a, up_b, y_a, y_b, act = _ffn_up_fwd(h2, g_up, g_cw, cb8, tf)
    d_out, d_f, loss_part, d_gt2, d_g_post_ffn = _ffn_down_fwd(act, wd, x2, target, ada_raw, ada_b, g_post_ffn, tf)

    d_up, d_cw, d_cb = _ffn_down_bwd(d_f, wd, up_a, up_b, y_a, y_b, g_cw, tf)
    d_x2, d_m, d_sh2, d_sc2, d_g_pre_ffn, d_gt1, d_g_post_mix = _ffn_up_bwd(
        d_up, g_up, x2, m_mix, d_out, ada_raw, ada_b, g_pre_ffn, g_post_mix, tm)
    p_down = _tn_matmul(act, d_f[None], "dw_down", ts).reshape(N_DEV, -1, D_MODEL)
    p_up = _tn_matmul(d_up.reshape(N_DEV, s, FF_BLK), h2[None], "dw_up", ts).reshape(half, 2, FF_BLK, D_MODEL)
    d_m3 = d_m[None]
    dwo_attn = _tn_matmul(o_pad[None], d_m3, "dw_out_attn", ts)[0].reshape(HEADS, HEAD_PAD, D_MODEL)[:, :NOPE]
    dwo_sgu = _tn_matmul(sgu[None], d_m3, "dw_out_sgu", ts)[0]
    p_out = jnp.concatenate([dwo_attn.reshape(GM_WIDTH, D_MODEL), dwo_sgu], axis=0).reshape(N_DEV, -1, D_MODEL)
    d_o, d_sgu, delta = _out_proj_bwd(d_m, wo, o_pad, tm)
    d_guv, d_ws, d_bs, d_ln_g, d_ln_b = _gmlp_bwd(z, d_sgu, ln_g, ln_b, w_sp, bias_exp, tm)
    packed = _pack_small([d_gt1, d_sh2, d_sc2, d_gt2], [d_g_post_mix, d_g_pre_ffn, d_g_post_ffn], loss_part,
                         d_ln_g, d_ln_b, d_bs, d_cb, d_ws)

    def ffn_slot(j):
        return (j % half, j // half)

    dq, dk, dv, (r_out, r_up, r_down, r_cw), (g_small,) = _attn_bwd(
        qp, kp, vp, d_o, lse, delta, tq,
        [(p_out, _plain_slot), (p_up, ffn_slot), (p_down, _plain_slot), (d_cw, ffn_slot)], [packed])
    grad_x, d_za, d_qp, d_kvp, d_sh1, d_sc1, d_g_pre_mix, d_g_q, d_g_kv = _mix_in_bwd(
        dq, dk, dv, z, d_guv, x2d, d_x2, ada_raw, ada_b, g_pre_mix, g_q, g_kv, w1t, wqt, wkv, cos_t, sin_t, tm)
    h1_3 = h1[None]
    dw1a = _tn_matmul(d_za[None], h1_3, "dw_in_a", ts)[0]
    dw1b = _tn_matmul(d_guv[None], h1_3, "dw_in_b", ts)[0]
    d_w_in_t = jnp.concatenate([dw1a[:o2], dw1a[o2 + NOPE:o2 + NOPE + ROPE], dw1b], axis=0)
    p_in = d_w_in_t.reshape(N_DEV, -1, D_MODEL)
    p_uq = _tn_matmul(d_qp[None], cqn[None], "dw_uq", ts)[0].reshape(HEADS, HEAD_PAD, Q_LORA)[:, :NOPE + ROPE]
    dwkv = _tn_matmul(ckvn[None], d_kvp[None], "dw_ukv", ts)[0]
    dwk = dwkv[:, :hp].reshape(KV_LORA, HEADS, HEAD_PAD)[:, :, :NOPE]
    dwv = dwkv[:, hp:].reshape(KV_LORA, HEADS, HEAD_PAD)[:, :, :NOPE]
    p_ukv = jnp.transpose(jnp.concatenate([dwk, dwv], axis=2), (1, 0, 2))

    (g_late,), (r_in, r_uq, r_ukv) = _exchange(
        [_pack_late(d_sh1, d_sc1, d_g_pre_mix, d_g_q, d_g_kv)],
        [(p_in, _plain_slot), (p_uq, _plain_slot), (p_ukv, _plain_slot)], "final_exchange")
    small_params = [(b_ada, m_b_ada, v_b_ada), (g_pre_mix, m_g_pre_mix, v_g_pre_mix),
                    (g_post_mix, m_g_post_mix, v_g_post_mix), (g_pre_ffn, m_g_pre_ffn, v_g_pre_ffn),
                    (g_post_ffn, m_g_post_ffn, v_g_post_ffn), (g_q, m_g_q, v_g_q), (g_kv, m_g_kv, v_g_kv),
                    (gm_ln_g, m_gm_ln_g, v_gm_ln_g), (gm_ln_b, m_gm_ln_b, v_gm_ln_b),
                    (w_spatial, m_w_spatial, v_w_spatial), (b_spatial, m_b_spatial, v_b_spatial),
                    tuple(a.reshape(N_DEV, FF_BLK) for a in (conv_b, m_conv_b, v_conv_b))]
    small_out, loss_row, d_ada_all = _adam_small(g_small, g_late, small_params)
    small_out[11] = tuple(o.reshape(conv_b.shape) for o in small_out[11])
    loss = loss_row[0, 0]

    def big(recv, w, m, v, name):
        g, d, m2, v2 = _adam_reduce(recv, w[0], m[0], v[0], name)
        return g[None], d[None], m2[None], v2[None]

    def big_t(recv, w_t, m_t, v_t, name):
        return tuple(jnp.swapaxes(o, 0, 1)[None] for o in _adam_reduce(recv, w_t, m_t, v_t, name))

    a_in = big_t(r_in, w_in_t, m_in_t, v_in_t, "adam_w_in")
    a_uq = big_t(r_uq, w_uq_t, m_uq_t, v_uq_t, "adam_w_uq")
    a_ukv = big(r_ukv, w_ukv, m_w_ukv, v_w_ukv, "adam_w_ukv")
    a_out = big(r_out, w_out, m_w_out, v_w_out, "adam_w_out")
    a_up = big_t(r_up, w_up_t, m_up_t, v_up_t, "adam_w_up")
    a_down = big(r_down, w_down, m_w_down, v_w_down, "adam_w_down")
    ada_cols = w_ada.shape[2]
    d_ada_cols = lax.dynamic_slice(d_ada_all.reshape(N_DEV, 6 * D_MODEL), (0, my_slot * ada_cols), (N_DEV, ada_cols))
    a_ada = tuple(t[None] for t in _adam_w_ada(c_act.T, d_ada_cols, w_ada[0], m_w_ada[0], v_w_ada[0]))
    a_cw = big(r_cw, conv_w, m_conv_w, v_conv_w, "adam_conv_w")

    def small(k):
        return small_out[k]

    per_weight = [a_ada, small(0), small(1), small(2), a_in, small(5), a_uq, small(6), a_ukv, small(7), small(8),
                  small(9), small(10), a_out, small(3), small(4), a_up, a_cw, small(11), a_down]
    outs = [loss, grad_x[None]]
    for k in range(4):
        outs += [t[k] for t in per_weight]
    return tuple(outs)
```

```python
import functools

import jax
import jax.numpy as jnp
from jax import lax
from jax.experimental import pallas as pl
from jax.experimental.pallas import tpu as pltpu

F32 = jnp.float32
MXU = jnp.bfloat16

N_DEV = 8
D_MODEL = 1024
HEADS = 8
HEAD_PAD = 128
NOPE = 64
ROPE = 32
Q_LORA = 256
KV_LORA = 128
GM_WIDTH = 512
GM_DIM = 64
GM_CHUNK = 128
CHUNK_SHIFT = 6
ROPE_THETA = 10000.0
ATTN_SCALE = (NOPE + ROPE) ** -0.5
LOG2E = 1.4426950408889634
SCALE_LOG2E = ATTN_SCALE * LOG2E
Z_COLS = 1536
FF_BLK = 704
EPS = 1e-6
ADAM_LR = 0.001
ADAM_B1 = 0.9
ADAM_B2 = 0.999
ADAM_EPS = 1e-08
ADAM_WD = 0.01
ADAM_STEP = 10
VMEM_LIMIT = 56 * 1024 * 1024
MESH = pl.DeviceIdType.MESH


def _dot(a, b):
    return jnp.dot(a, b, preferred_element_type=F32)


def _dot_nt(a, b):
    return lax.dot_general(a, b, (((1,), (1,)), ((), ())), preferred_element_type=F32)


def _dot_tn(a, b):
    return lax.dot_general(a, b, (((0,), (0,)), ((), ())), preferred_element_type=F32)


def _call(body, *, name, grid, in_specs, out_specs, out_shape, scratch=(), sem=None):
    params = pltpu.CompilerParams(dimension_semantics=sem, vmem_limit_bytes=VMEM_LIMIT)
    return pl.pallas_call(body, name=name, grid=grid, in_specs=in_specs, out_specs=out_specs,
                          out_shape=out_shape, scratch_shapes=list(scratch), compiler_params=params)


def _full(shape):
    n = len(shape)
    return pl.BlockSpec(shape, lambda *_: (0,) * n)


def _rows(tm, cols, col_block=0):
    return pl.BlockSpec((tm, cols), lambda i: (i, col_block))


def _sds(shape, dtype):
    return jax.ShapeDtypeStruct(shape, dtype)


def _row(ref, k):
    return ref[pl.ds(k, 1), :]


def _rms(x):
    r = lax.rsqrt(jnp.mean(x * x, axis=-1, keepdims=True) + EPS)
    return x * r, r


def _rms_bwd(d_hat, hat, r):
    return r * (d_hat - hat * jnp.mean(d_hat * hat, axis=-1, keepdims=True))


def _rope_partner(t):
    lane = lax.broadcasted_iota(jnp.int32, t.shape, 1)
    swapped = jnp.where(lane < NOPE + ROPE // 2, -pltpu.roll(t, HEAD_PAD - ROPE // 2, 1), pltpu.roll(t, ROPE // 2, 1))
    return jnp.where((lane >= NOPE) & (lane < NOPE + ROPE), swapped, 0.0)


def _rope(t, cos, sin):
    return t * cos + _rope_partner(t) * sin


def _rope_transposed(g, cos, sin):
    return g * cos - _rope_partner(g * sin)


def _gelu(x):
    return x * (0.5 * (1.0 + jnp.tanh(0.7978845608028654 * (x + 0.044715 * (x * x * x)))))


def _gelu_grad(x):
    t = jnp.tanh(0.7978845608028654 * (x + 0.044715 * (x * x * x)))
    return 0.5 * (1.0 + t) + 0.5 * x * (1.0 - t * t) * (0.7978845608028654 * (1.0 + 3.0 * 0.044715 * (x * x)))


def _split_dot(x, mat):
    hi = x.astype(MXU)
    lo = (x - hi.astype(F32)).astype(MXU)
    return _dot(hi, mat) + _dot(lo, mat)


def _split_dot3(x, mat):
    hi = x.astype(MXU)
    r1 = x - hi.astype(F32)
    mid = r1.astype(MXU)
    lo = (r1 - mid.astype(F32)).astype(MXU)
    return (_dot(hi, mat) + _dot(mid, mat)) + _dot(lo, mat)


def _seg_matrix():
    r = lax.broadcasted_iota(jnp.int32, (GM_WIDTH, GM_WIDTH), 0) >> 6
    c = lax.broadcasted_iota(jnp.int32, (GM_WIDTH, GM_WIDTH), 1) >> 6
    return jnp.where(r == c, 1.0 / GM_DIM, 0.0).astype(MXU)


def _spatial_mask():
    i = lax.broadcasted_iota(jnp.int32, (GM_CHUNK, GM_CHUNK), 0) >> CHUNK_SHIFT
    j = lax.broadcasted_iota(jnp.int32, (GM_CHUNK, GM_CHUNK), 1) >> CHUNK_SHIFT
    return (j <= i).astype(F32)


def _head_lane_mask(h, rows):
    lane = lax.broadcasted_iota(jnp.int32, (rows, GM_WIDTH), 1) >> 6
    return lane == h


def _my_place():
    return lax.axis_index("x"), lax.axis_index("y"), lax.axis_index("c")


def _flat(p):
    return 4 * p[0] + 2 * p[1] + p[2]


def _comm_sems(n):
    return [pltpu.SemaphoreType.DMA((7 * n,)), pltpu.SemaphoreType.DMA((7 * n,)), pltpu.SemaphoreType.DMA((n,))]


def _gather_steps(ins, outs, sems):
    send_sems, recv_sems, local_sems = sems
    n = len(ins)
    x, y, c = _my_place()
    me, sibling = (x, y, c), (x, y, 1 - c)
    chips = [(1 - x, y), (x, 1 - y), (1 - x, 1 - y)]

    def copy(a, k, block, to, src=None):
        slot = outs[a].at[_flat(block)]
        return pltpu.make_async_remote_copy(
            src_ref=slot if src is None else src, dst_ref=slot,
            send_sem=send_sems.at[7 * a + k], recv_sem=recv_sems.at[7 * a + k],
            device_id=to, device_id_type=MESH)

    def mine():
        return [pltpu.make_async_copy(ins[a], outs[a].at[_flat(me)], local_sems.at[a]) for a in range(n)]

    def first():
        cps = []
        for a in range(n):
            cps.append(copy(a, 0, me, sibling, src=ins[a]))
            cps += [copy(a, 1 + j, me, (*chip, c), src=ins[a]) for j, chip in enumerate(chips)]
        return cps

    def passed():
        return [copy(a, 4 + j, (*chip, c), sibling) for a in range(n) for j, chip in enumerate(chips)]

    def start():
        for cp in mine() + first():
            cp.start()

    def forward():
        for a in range(n):
            for j, chip in enumerate(chips):
                copy(a, 1 + j, (*chip, c), me).wait_recv()
                copy(a, 4 + j, (*chip, c), sibling).start()

    def finish():
        for a in range(n):
            copy(a, 0, sibling, me).wait_recv()
            for j, chip in enumerate(chips):
                copy(a, 4 + j, (*chip, 1 - c), me).wait_recv()
        for cp in first() + passed():
            cp.wait_send()
        for cp in mine():
            cp.wait()

    return start, forward, finish


def _scatter_steps(ins, outs, sems, slots):
    send_sems, recv_sems, local_sems = sems
    n = len(ins)
    flips = [(fx, fy, fc) for fx in (0, 1) for fy in (0, 1) for fc in (0, 1)][1:]
    me = _my_place()

    def peer(f):
        return tuple(1 - v if b else v for v, b in zip(me, f))

    def copy(a, k, arriving=False):
        p = peer(flips[k])
        return pltpu.make_async_remote_copy(
            src_ref=ins[a].at[slots[a](_flat(p))], dst_ref=outs[a].at[_flat(p if arriving else me)],
            send_sem=send_sems.at[7 * a + k], recv_sem=recv_sems.at[7 * a + k],
            device_id=p, device_id_type=MESH)

    def mine():
        return [pltpu.make_async_copy(ins[a].at[slots[a](_flat(me))], outs[a].at[_flat(me)], local_sems.at[a])
                for a in range(n)]

    def start():
        for cp in mine() + [copy(a, k) for a in range(n) for k in range(7)]:
            cp.start()

    def finish():
        for a in range(n):
            for k in range(7):
                copy(a, k, arriving=True).wait_recv()
        for a in range(n):
            for k in range(7):
                copy(a, k).wait_send()
        for cp in mine():
            cp.wait()

    return start, finish


def _plain_slot(j):
    return (j,)


def _scatter_out_shape(arr, slot):
    return _sds((N_DEV,) + arr.shape[len(slot(0)):], arr.dtype)


def _exchange(gathered, scattered, name):
    ng, ns = len(gathered), len(scattered)
    slots = [slot for _, slot in scattered]

    def body(*refs):
        g_in, s_in = refs[:ng], refs[ng:ng + ns]
        g_out, s_out = refs[ng + ns:2 * ng + ns], refs[2 * ng + ns:2 * (ng + ns)]
        sems = refs[2 * (ng + ns):]
        g_start, g_forward, g_finish = _gather_steps(g_in, g_out, sems[:3])
        s_start, s_finish = _scatter_steps(s_in, s_out, sems[3:], slots)
        g_start()
        s_start()
        g_forward()
        g_finish()
        s_finish()

    any_spec = pl.BlockSpec(memory_space=pl.ANY)
    outs = pl.pallas_call(
        body, name=name,
        in_specs=[any_spec] * (ng + ns), out_specs=[any_spec] * (ng + ns),
        out_shape=[_sds((N_DEV,) + a.shape, a.dtype) for a in gathered]
        + [_scatter_out_shape(a, slot) for a, slot in scattered],
        scratch_shapes=_comm_sems(max(ng, 1)) + _comm_sems(max(ns, 1)),
    )(*gathered, *[a for a, _ in scattered])
    return outs[:ng], outs[ng:]


def _ada_fwd(c_all, w_ada):
    def body(c_ref, w_ref, part_ref, act_ref):
        cv = c_ref[...]
        act = cv * jax.nn.sigmoid(cv)
        act_ref[...] = act
        part_ref[...] = _dot(act.astype(MXU), w_ref[...].astype(MXU))

    cols = w_ada.shape[1]
    return _call(body, name="ada_fwd", grid=(1,),
                 in_specs=[_full(c_all.shape), _full(w_ada.shape)],
                 out_specs=[_full((N_DEV, cols)), _full(c_all.shape)],
                 out_shape=[_sds((N_DEV, cols), F32), _sds(c_all.shape, F32)])(c_all, w_ada)


def _mix_in_fwd(x, ada_raw, ada_b, g_pre, w1, g_q, g_kv, wq, wkv, cos_t, sin_t, tm):
    s = x.shape[0]

    def body(x_ref, ar_ref, ab_ref, g_ref, w1_ref, gq_ref, gkv_ref, wq_ref, wkv_ref, cos_ref, sin_ref,
             h1_ref, z_ref, qp_ref, kp_ref, vp_ref, cqn_ref, ckvn_ref):
        sh = _row(ar_ref, 0) + _row(ab_ref, 0)
        sc = _row(ar_ref, 1) + _row(ab_ref, 1)
        xn, _ = _rms(x_ref[...])
        hb = ((xn * g_ref[...]) * (1.0 + sc) + sh).astype(MXU)
        h1_ref[...] = hb
        z = _dot_nt(hb, w1_ref[...])
        z_ref[...] = z
        cos, sin = cos_ref[...], sin_ref[...]
        cqn = (_rms(z[:, :Q_LORA])[0] * gq_ref[...]).astype(MXU)
        ckvn = (_rms(z[:, Q_LORA:Q_LORA + KV_LORA])[0] * gkv_ref[...]).astype(MXU)
        cqn_ref[...] = cqn
        ckvn_ref[...] = ckvn
        q = _dot_nt(cqn, wq_ref[...])
        kv = _dot(ckvn, wkv_ref[...])
        k_rope = _rope(z[:, Q_LORA + KV_LORA:Q_LORA + KV_LORA + HEAD_PAD], cos, sin)
        for h in range(HEADS):
            blk = slice(h * HEAD_PAD, (h + 1) * HEAD_PAD)
            qp_ref[:, blk] = _rope(q[:, blk], cos, sin).astype(MXU)
            kp_ref[:, blk] = (kv[:, blk] + k_rope).astype(MXU)
        v_lane = lax.broadcasted_iota(jnp.int32, (tm, HEADS * HEAD_PAD), 1) & (HEAD_PAD - 1)
        vp_ref[...] = jnp.where(v_lane == NOPE, 1.0, kv[:, HEADS * HEAD_PAD:]).astype(MXU)

    hp = HEADS * HEAD_PAD
    return _call(
        body, name="mix_in_fwd", grid=(s // tm,), sem=("parallel",),
        in_specs=[_rows(tm, D_MODEL), _full(ada_raw.shape), _full(ada_b.shape), _full(g_pre.shape), _full(w1.shape),
                  _full(g_q.shape), _full(g_kv.shape), _full(wq.shape), _full(wkv.shape),
                  _rows(tm, HEAD_PAD), _rows(tm, HEAD_PAD)],
        out_specs=[_rows(tm, D_MODEL), _rows(tm, Z_COLS), _rows(tm, hp), _rows(tm, hp), _rows(tm, hp),
                   _rows(tm, Q_LORA), _rows(tm, KV_LORA)],
        out_shape=[_sds((s, D_MODEL), MXU), _sds((s, Z_COLS), F32), _sds((s, hp), MXU), _sds((s, hp), MXU),
                   _sds((s, hp), MXU), _sds((s, Q_LORA), MXU), _sds((s, KV_LORA), MXU)],
    )(x, ada_raw, ada_b, g_pre, w1, g_q, g_kv, wq, wkv, cos_t, sin_t)


def _gm_norm(zv, seg):
    gv = _gelu(zv)
    cen = gv - _split_dot(gv, seg)
    rstd = lax.rsqrt(_split_dot(cen * cen, seg) + EPS)
    return gv, cen * rstd, rstd


def _gm_mix(wm, vb, rows):
    out = jnp.zeros((rows, GM_WIDTH), F32)
    for h in range(HEADS):
        out = out + jnp.where(_head_lane_mask(h, rows), _dot(wm[h], vb), 0.0)
    return out


def _gmlp_fwd(z, ln_g, ln_b, w_sp, bias_exp, tm):
    s = z.shape[0]
    nblk = tm // GM_CHUNK

    def body(zu_ref, zv_ref, lg_ref, lb_ref, w_ref, be_ref, sgu_ref):
        seg = _seg_matrix()
        mask = _spatial_mask()
        wm = [(w_ref[h] * mask).astype(MXU) for h in range(HEADS)]
        gu = _gelu(zu_ref[...])
        _, vhat, _ = _gm_norm(zv_ref[...], seg)
        vln = (vhat * lg_ref[...] + lb_ref[...]).astype(MXU)
        for n in range(nblk):
            rows = slice(n * GM_CHUNK, (n + 1) * GM_CHUNK)
            mixed = _gm_mix(wm, vln[rows], GM_CHUNK) + be_ref[...]
            sgu_ref[rows, :] = (gu[rows] * mixed).astype(MXU)

    return _call(
        body, name="gmlp_fwd", grid=(s // tm,), sem=("parallel",),
        in_specs=[_rows(tm, GM_WIDTH, 1), _rows(tm, GM_WIDTH, 2), _full(ln_g.shape), _full(ln_b.shape),
                  _full(w_sp.shape), _full(bias_exp.shape)],
        out_specs=_rows(tm, GM_WIDTH), out_shape=_sds((s, GM_WIDTH), MXU),
    )(z, z, ln_g, ln_b, w_sp, bias_exp)


def _diag_mask(t):
    qc = lax.broadcasted_iota(jnp.int32, (t, t), 0) >> CHUNK_SHIFT
    kc = lax.broadcasted_iota(jnp.int32, (t, t), 1) >> CHUNK_SHIFT
    return kc <= qc


NEG_BIG = -1e30
ATTN_HEADS_PER_STEP = 2


def _attn_fwd(qp, kp, vp, tq, gathered):
    s = qp.shape[0]
    nq = s // tq
    hb = ATTN_HEADS_PER_STEP
    groups = HEADS // hb
    width = hb * HEAD_PAD
    ng = len(gathered)

    def body(q_ref, k_ref, v_ref, *rest):
        g_in, (o_ref, lse_ref), g_out = rest[:ng], rest[ng:ng + 2], rest[ng + 2:2 * ng + 2]
        m_sc, acc_sc = rest[2 * ng + 2:2 * ng + 4]
        g_start, g_forward, g_finish = _gather_steps(g_in, g_out, rest[2 * ng + 4:])
        g, i = pl.program_id(0), pl.program_id(1)
        pl.when((g == 0) & (i == 0))(g_start)
        pl.when((g == groups - 1) & (i == 0))(g_forward)
        m_sc[...] = jnp.full(m_sc.shape, NEG_BIG, F32)
        acc_sc[...] = jnp.zeros(acc_sc.shape, F32)

        def tile(j, masked):
            rows = pl.ds(pl.multiple_of(j * tq, tq), tq)
            for hh in range(hb):
                lanes = slice(hh * HEAD_PAD, (hh + 1) * HEAD_PAD)
                sc = _dot_nt(q_ref[:, lanes], k_ref[rows, lanes])
                if masked:
                    sc = jnp.where(_diag_mask(tq), sc, NEG_BIG)
                blocks = [sc[:, b * 128:(b + 1) * 128] for b in range(tq // 128)]
                m_prev = m_sc[hh]
                m_tile = jnp.max(functools.reduce(jnp.maximum, blocks), axis=-1, keepdims=True)
                m_new = jnp.maximum(m_prev, m_tile)
                alpha = jnp.exp2((m_prev - m_new) * SCALE_LOG2E)
                p = jnp.concatenate([jnp.exp2((b - m_new) * SCALE_LOG2E) for b in blocks], axis=1).astype(MXU)
                acc_sc[hh] = alpha * acc_sc[hh] + _dot(p, v_ref[rows, lanes])
                m_sc[hh] = m_new

        def off_diagonal_pair(p, carry):
            tile(2 * p, False)
            tile(2 * p + 1, False)
            return carry

        lax.fori_loop(0, i // 2, off_diagonal_pair, 0)

        @pl.when(i % 2 == 1)
        def _():
            tile(i - 1, False)

        tile(i, True)
        for hh in range(hb):
            lanes = slice(hh * HEAD_PAD, (hh + 1) * HEAD_PAD)
            acc = acc_sc[hh]
            denom = acc[:, NOPE:NOPE + 1]
            o_ref[:, lanes] = (acc / denom).astype(MXU)
            lse_ref[hh] = m_sc[hh][:, :1] * SCALE_LOG2E + jnp.log(denom) * LOG2E
        pl.when((g == groups - 1) & (i == nq - 1))(g_finish)

    q_spec = pl.BlockSpec((tq, width), lambda g, i: (i, g))
    kv_spec = pl.BlockSpec((s, width), lambda g, i: (0, g))
    any_spec = pl.BlockSpec(memory_space=pl.ANY)
    outs = _call(
        body, name="attn_fwd", grid=(groups, nq), sem=("arbitrary", "arbitrary"),
        in_specs=[q_spec, kv_spec, kv_spec] + [any_spec] * ng,
        out_specs=[q_spec, pl.BlockSpec((hb, tq, 1), lambda g, i: (g, i, 0))] + [any_spec] * ng,
        out_shape=[_sds(qp.shape, MXU), _sds((HEADS, s, 1), F32)]
        + [_sds((N_DEV,) + a.shape, a.dtype) for a in gathered],
        scratch=[pltpu.VMEM((hb, tq, HEAD_PAD), F32), pltpu.VMEM((hb, tq, HEAD_PAD), F32)] + _comm_sems(ng),
    )(qp, kp, vp, *gathered)
    return outs[0], outs[1], outs[2:]


def _out_proj_fwd(o_pad, sgu, wo, x, ada_raw, ada_b, g_post_mix, g_pre_ffn, tm):
    s = x.shape[0]
    hp = HEADS * HEAD_PAD

    def body(o_ref, sgu_ref, wo_ref, x_ref, ar_ref, ab_ref, gpm_ref, gpf_ref, m_ref, x2_ref, h2_ref):
        gt1 = _row(ar_ref, 2) + _row(ab_ref, 2)
        sh2 = _row(ar_ref, 3) + _row(ab_ref, 3)
        sc2 = _row(ar_ref, 4) + _row(ab_ref, 4)
        m = _dot(o_ref[...], wo_ref[pl.ds(0, hp), :]) + _dot(sgu_ref[...], wo_ref[pl.ds(hp, GM_WIDTH), :])
        m_ref[...] = m
        x2 = x_ref[...] + gt1 * (_rms(m)[0] * gpm_ref[...])
        x2_ref[...] = x2
        h2_ref[...] = ((_rms(x2)[0] * gpf_ref[...]) * (1.0 + sc2) + sh2).astype(MXU)

    return _call(
        body, name="out_proj_fwd", grid=(s // tm,), sem=("parallel",),
        in_specs=[_rows(tm, hp), _rows(tm, GM_WIDTH), _full(wo.shape), _rows(tm, D_MODEL), _full(ada_raw.shape),
                  _full(ada_b.shape), _full(g_post_mix.shape), _full(g_pre_ffn.shape)],
        out_specs=[_rows(tm, D_MODEL)] * 3,
        out_shape=[_sds((s, D_MODEL), F32), _sds((s, D_MODEL), F32), _sds((s, D_MODEL), MXU)],
    )(o_pad, sgu, wo, x, ada_raw, ada_b, g_post_mix, g_pre_ffn)


def _conv(u, halo, cw_ref, cb_ref):
    ext = jnp.concatenate([halo, u], axis=0)
    m1, m2 = pltpu.roll(ext, 1, 0)[8:], pltpu.roll(ext, 2, 0)[8:]
    return cb_ref[0] + ((m2 * cw_ref[0, pl.ds(0, 1), :] + m1 * cw_ref[0, pl.ds(1, 1), :]) + u * cw_ref[0, pl.ds(2, 1), :])


ROW_SUB = 256


def _sub_blocks(tm):
    return [slice(r, r + ROW_SUB) for r in range(0, tm, ROW_SUB)]


def _ffn_up_fwd(h2, w_up, conv_w, conv_b, tm):
    s = h2.shape[0]
    half = N_DEV // 2

    def body(h_ref, wa_ref, wb_ref, cwa_ref, cwb_ref, cba_ref, cbb_ref,
             ua_ref, ub_ref, ya_ref, yb_ref, act_ref, halo_a, halo_b):
        i = pl.program_id(1)

        @pl.when(i == 0)
        def _():
            halo_a[...] = jnp.zeros(halo_a.shape, F32)
            halo_b[...] = jnp.zeros(halo_b.shape, F32)

        ha, hb = halo_a[...], halo_b[...]
        for rows in _sub_blocks(tm):
            h = h_ref[rows, :]
            ua = _dot_nt(h, wa_ref[0])
            ub = _dot_nt(h, wb_ref[0])
            ua_ref[0, rows, :] = ua
            ub_ref[0, rows, :] = ub
            ya = _conv(ua, ha, cwa_ref, cba_ref)
            yb = _conv(ub, hb, cwb_ref, cbb_ref)
            ya_ref[0, rows, :] = ya
            yb_ref[0, rows, :] = yb
            ha, hb = ua[ROW_SUB - 8:], ub[ROW_SUB - 8:]
            act_ref[0, rows, :] = ((ya * jax.nn.sigmoid(ya)) * yb).astype(MXU)
        halo_a[...] = ha
        halo_b[...] = hb

    def blk(shape, off):
        return pl.BlockSpec(shape, lambda j, i: (j + off, 0, 0))

    def tok(off=0):
        return pl.BlockSpec((1, tm, FF_BLK), lambda j, i: (j + off, i, 0))

    return _call(
        body, name="ffn_up_fwd", grid=(half, s // tm), sem=("parallel", "arbitrary"),
        in_specs=[pl.BlockSpec((tm, D_MODEL), lambda j, i: (i, 0)),
                  blk((1, FF_BLK, D_MODEL), 0), blk((1, FF_BLK, D_MODEL), half),
                  blk((1, 3, FF_BLK), 0), blk((1, 3, FF_BLK), half), blk((1, 1, FF_BLK), 0), blk((1, 1, FF_BLK), half)],
        out_specs=[tok()] * 5,
        out_shape=[_sds((half, s, FF_BLK), F32)] * 4 + [_sds((half, s, FF_BLK), MXU)],
        scratch=[pltpu.VMEM((8, FF_BLK), F32), pltpu.VMEM((8, FF_BLK), F32)],
    )(h2, w_up, w_up, conv_w, conv_w, conv_b, conv_b)


def _ffn_down_fwd(act, wd, x2, target, ada_raw, ada_b, g_post_ffn, tm):
    s = x2.shape[0]
    half = N_DEV // 2

    def body(act_ref, wd_ref, x2_ref, t_ref, ar_ref, ab_ref, g_ref, dout_ref, df_ref, loss_ref, dgt_ref, dg_ref):
        i = pl.program_id(0)

        @pl.when(i == 0)
        def _():
            loss_ref[...] = jnp.zeros(loss_ref.shape, F32)
            dgt_ref[...] = jnp.zeros(dgt_ref.shape, F32)
            dg_ref[...] = jnp.zeros(dg_ref.shape, F32)

        gt2 = _row(ar_ref, 5) + _row(ab_ref, 5)
        g = g_ref[...]
        for rows in _sub_blocks(tm):
            f = _dot(act_ref[0, rows, :], wd_ref[0])
            for j in range(1, half):
                f = f + _dot(act_ref[j, rows, :], wd_ref[j])
            fhat, rf = _rms(f)
            fn = fhat * g
            err = (x2_ref[rows, :] + gt2 * fn) - t_ref[rows, :]
            loss_ref[...] += 0.5 * jnp.sum(jnp.mean(err * err, axis=-1, keepdims=True))
            d_out = err * (1.0 / D_MODEL)
            dout_ref[rows, :] = d_out
            dgt_ref[...] += jnp.sum(d_out * fn, axis=0, keepdims=True)
            d_fn = d_out * gt2
            dg_ref[...] += jnp.sum(d_fn * fhat, axis=0, keepdims=True)
            df_ref[rows, :] = _rms_bwd(d_fn * g, fhat, rf).astype(MXU)

    vec = pl.BlockSpec((1, D_MODEL), lambda i: (0, 0))
    return _call(
        body, name="ffn_down_fwd", grid=(s // tm,), sem=("arbitrary",),
        in_specs=[pl.BlockSpec((half, tm, FF_BLK), lambda i: (0, i, 0)), _full(wd.shape), _rows(tm, D_MODEL),
                  _rows(tm, D_MODEL), _full(ada_raw.shape), _full(ada_b.shape), _full(g_post_ffn.shape)],
        out_specs=[_rows(tm, D_MODEL), _rows(tm, D_MODEL), pl.BlockSpec((1, 128), lambda i: (0, 0)), vec, vec],
        out_shape=[_sds((s, D_MODEL), F32), _sds((s, D_MODEL), MXU), _sds((1, 128), F32),
                   _sds((1, D_MODEL), F32), _sds((1, D_MODEL), F32)],
    )(act, wd, x2, target, ada_raw, ada_b, g_post_ffn)


def _ffn_down_bwd(d_f, wd, up_a, up_b, y_a, y_b, conv_w, act, h2, tm):
    s = d_f.shape[0]
    half = N_DEV // 2
    nt = s // tm

    def body(df_ref, wd_ref, ua_ref, ub_ref, ya_ref, yb_ref, cwa_ref, cwb_ref, act_ref, h2_ref,
             dup_ref, dcw_ref, dcb_ref, pd_ref, pu_ref, next_a, next_b, acc_d, acc_a, acc_b):
        i = pl.program_id(1)

        @pl.when(i == 0)
        def _():
            next_a[...] = jnp.zeros(next_a.shape, F32)
            next_b[...] = jnp.zeros(next_b.shape, F32)
            dcw_ref[...] = jnp.zeros(dcw_ref.shape, F32)
            dcb_ref[...] = jnp.zeros(dcb_ref.shape, F32)
            for acc in (acc_d, acc_a, acc_b):
                acc[...] = jnp.zeros(acc.shape, F32)

        def conv_bwd(d_y, u, nxt, cw_ref, part, rows):
            ext = jnp.concatenate([d_y, nxt], axis=0)
            p1 = pltpu.roll(ext, ROW_SUB + 7, 0)[:ROW_SUB]
            p2 = pltpu.roll(ext, ROW_SUB + 6, 0)[:ROW_SUB]
            d_u = (d_y * cw_ref[0, pl.ds(2, 1), :] + p1 * cw_ref[0, pl.ds(1, 1), :]) + p2 * cw_ref[0, pl.ds(0, 1), :]
            dup_ref[0, part, rows, :] = d_u.astype(MXU)
            dcb_ref[0, part] += jnp.sum(d_y, axis=0, keepdims=True)
            dcw_ref[0, part, pl.ds(0, 1), :] += jnp.sum(p2 * u, axis=0, keepdims=True)
            dcw_ref[0, part, pl.ds(1, 1), :] += jnp.sum(p1 * u, axis=0, keepdims=True)
            dcw_ref[0, part, pl.ds(2, 1), :] += jnp.sum(d_y * u, axis=0, keepdims=True)
            return d_y[:8]

        nxa, nxb = next_a[...], next_b[...]
        for rows in reversed(_sub_blocks(tm)):
            d_act = _dot_nt(df_ref[rows, :], wd_ref[0])
            ya, yb = ya_ref[0, rows, :], yb_ref[0, rows, :]
            sig = jax.nn.sigmoid(ya)
            d_ya = d_act * yb * (sig * (1.0 + ya * (1.0 - sig)))
            d_yb = d_act * (ya * sig)
            nxa = conv_bwd(d_ya, ua_ref[0, rows, :], nxa, cwa_ref, 0, rows)
            nxb = conv_bwd(d_yb, ub_ref[0, rows, :], nxb, cwb_ref, 1, rows)
        next_a[...] = nxa
        next_b[...] = nxb
        acc_d[...] += _dot_tn(act_ref[0], df_ref[...])
        acc_a[...] += _dot_tn(dup_ref[0, 0], h2_ref[...])
        acc_b[...] += _dot_tn(dup_ref[0, 1], h2_ref[...])

        @pl.when(i == nt - 1)
        def _():
            pd_ref[0] = acc_d[...].astype(MXU)
            pu_ref[0, 0] = acc_a[...].astype(MXU)
            pu_ref[0, 1] = acc_b[...].astype(MXU)

    def rev(i):
        return nt - 1 - i

    def blk(shape, off):
        return pl.BlockSpec(shape, lambda j, i: (j + off, 0, 0))

    tok = pl.BlockSpec((1, tm, FF_BLK), lambda j, i: (j, rev(i), 0))
    acc3 = pl.BlockSpec((1, 2, 3, FF_BLK), lambda j, i: (j, 0, 0, 0))
    acc1 = pl.BlockSpec((1, 2, 1, FF_BLK), lambda j, i: (j, 0, 0, 0))
    return _call(
        body, name="ffn_down_bwd", grid=(half, nt), sem=("parallel", "arbitrary"),
        in_specs=[pl.BlockSpec((tm, D_MODEL), lambda j, i: (rev(i), 0)), blk((1, FF_BLK, D_MODEL), 0),
                  tok, tok, tok, tok, blk((1, 3, FF_BLK), 0), blk((1, 3, FF_BLK), half),
                  tok, pl.BlockSpec((tm, D_MODEL), lambda j, i: (rev(i), 0))],
        out_specs=[pl.BlockSpec((1, 2, tm, FF_BLK), lambda j, i: (j, 0, rev(i), 0)), acc3, acc1,
                   pl.BlockSpec((1, FF_BLK, D_MODEL), lambda j, i: (j, 0, 0)),
                   pl.BlockSpec((1, 2, FF_BLK, D_MODEL), lambda j, i: (j, 0, 0, 0))],
        out_shape=[_sds((half, 2, s, FF_BLK), MXU), _sds((half, 2, 3, FF_BLK), F32), _sds((half, 2, 1, FF_BLK), F32),
                   _sds((half, FF_BLK, D_MODEL), MXU), _sds((half, 2, FF_BLK, D_MODEL), MXU)],
        scratch=[pltpu.VMEM((8, FF_BLK), F32), pltpu.VMEM((8, FF_BLK), F32)]
        + [pltpu.VMEM((FF_BLK, D_MODEL), F32)] * 3,
    )(d_f, wd, up_a, up_b, y_a, y_b, conv_w, conv_w, act, h2)


def _ffn_up_bwd(d_up, w_up, x2, m, d_out, ada_raw, ada_b, g_pre_ffn, g_post_mix, tm):
    s = x2.shape[0]
    half = N_DEV // 2

    def body(dup_ref, w_ref, x2_ref, m_ref, dout_ref, ar_ref, ab_ref, gpf_ref, gpm_ref,
             dx_ref, dm_ref, dsh_ref, dsc_ref, dgpf_ref, dgt1_ref, dgpm_ref):
        i = pl.program_id(0)

        @pl.when(i == 0)
        def _():
            for r in (dsh_ref, dsc_ref, dgpf_ref, dgt1_ref, dgpm_ref):
                r[...] = jnp.zeros(r.shape, F32)

        gt1 = _row(ar_ref, 2) + _row(ab_ref, 2)
        sc2 = _row(ar_ref, 4) + _row(ab_ref, 4)
        gpf, gpm = gpf_ref[...], gpm_ref[...]
        d_h2 = _dot(dup_ref[0, 0], w_ref[0])
        for j in range(1, half):
            d_h2 = d_h2 + _dot(dup_ref[j, 0], w_ref[j])
        for j in range(half):
            d_h2 = d_h2 + _dot(dup_ref[j, 1], w_ref[half + j])
        x2n, r2 = _rms(x2_ref[...])
        dsh_ref[...] += jnp.sum(d_h2, axis=0, keepdims=True)
        dsc_ref[...] += jnp.sum(d_h2 * (x2n * gpf), axis=0, keepdims=True)
        d_mod = d_h2 * (1.0 + sc2)
        dgpf_ref[...] += jnp.sum(d_mod * x2n, axis=0, keepdims=True)
        d_x2 = dout_ref[...] + _rms_bwd(d_mod * gpf, x2n, r2)
        dx_ref[...] = d_x2
        mhat, rm = _rms(m_ref[...])
        dgt1_ref[...] += jnp.sum(d_x2 * (mhat * gpm), axis=0, keepdims=True)
        d_mn = d_x2 * gt1
        dgpm_ref[...] += jnp.sum(d_mn * mhat, axis=0, keepdims=True)
        dm_ref[...] = _rms_bwd(d_mn * gpm, mhat, rm).astype(MXU)

    vec = pl.BlockSpec((1, D_MODEL), lambda i: (0, 0))
    tok = pl.BlockSpec((half, 2, tm, FF_BLK), lambda i: (0, 0, i, 0))
    return _call(
        body, name="ffn_up_bwd", grid=(s // tm,), sem=("arbitrary",),
        in_specs=[tok, _full(w_up.shape), _rows(tm, D_MODEL), _rows(tm, D_MODEL), _rows(tm, D_MODEL),
                  _full(ada_raw.shape), _full(ada_b.shape), _full(g_pre_ffn.shape), _full(g_post_mix.shape)],
        out_specs=[_rows(tm, D_MODEL), _rows(tm, D_MODEL), vec, vec, vec, vec, vec],
        out_shape=[_sds((s, D_MODEL), F32), _sds((s, D_MODEL), MXU)] + [_sds((1, D_MODEL), F32)] * 5,
    )(d_up, w_up, x2, m, d_out, ada_raw, ada_b, g_pre_ffn, g_post_mix)


def _out_proj_bwd(d_m, wo, o_pad, tm):
    s = d_m.shape[0]
    hp = HEADS * HEAD_PAD

    def body(dm_ref, wo_ref, o_ref, do_ref, dsgu_ref, delta_ref):
        d_cat = _dot_nt(dm_ref[...], wo_ref[...])
        d_o = d_cat[:, :hp]
        do_ref[...] = d_o.astype(MXU)
        dsgu_ref[...] = d_cat[:, hp:]
        prod = d_o * o_ref[...].astype(F32)
        for h in range(HEADS):
            delta_ref[h] = jnp.sum(prod[:, h * HEAD_PAD:(h + 1) * HEAD_PAD], axis=-1, keepdims=True)

    return _call(
        body, name="out_proj_bwd", grid=(s // tm,), sem=("parallel",),
        in_specs=[_rows(tm, D_MODEL), _full(wo.shape), _rows(tm, hp)],
        out_specs=[_rows(tm, hp), _rows(tm, GM_WIDTH), pl.BlockSpec((HEADS, tm, 1), lambda i: (0, i, 0))],
        out_shape=[_sds((s, hp), MXU), _sds((s, GM_WIDTH), F32), _sds((HEADS, s, 1), F32)],
    )(d_m, wo, o_pad)


def _attn_bwd(qp, kp, vp, d_o, lse, delta, tq, scattered, gathered):
    s = qp.shape[0]
    nq = s // tq
    hb = ATTN_HEADS_PER_STEP
    groups = HEADS // hb
    width = hb * HEAD_PAD
    ns, ng = len(scattered), len(gathered)
    nc = ns + ng
    slots = [slot for _, slot in scattered]

    def body(q_ref, k_ref, v_ref, do_ref, lse_ref, dl_ref, *rest):
        c_in, (dq_ref, dk_ref, dv_ref), c_out = rest[:nc], rest[nc:nc + 3], rest[nc + 3:2 * nc + 3]
        dk_sc, dv_sc = rest[2 * nc + 3:2 * nc + 5]
        sems = rest[2 * nc + 5:]
        s_start, s_finish = _scatter_steps(c_in[:ns], c_out[:ns], sems[:3], slots)
        g_start, g_forward, g_finish = _gather_steps(c_in[ns:], c_out[ns:], sems[3:])
        g, j = pl.program_id(0), pl.program_id(1)

        @pl.when((g == 0) & (j == 0))
        def _():
            s_start()
            g_start()

        pl.when((g == groups - 1) & (j == 0))(g_forward)

        @pl.when(j == 0)
        def _():
            dq_ref[...] = jnp.zeros(dq_ref.shape, F32)

        dk_sc[...] = jnp.zeros(dk_sc.shape, F32)
        dv_sc[...] = jnp.zeros(dv_sc.shape, F32)

        def tile(i, masked):
            rows = pl.ds(pl.multiple_of(i * tq, tq), tq)
            for hh in range(hb):
                lanes = slice(hh * HEAD_PAD, (hh + 1) * HEAD_PAD)
                q, do, k = q_ref[rows, lanes], do_ref[rows, lanes], k_ref[:, lanes]
                sc = _dot_nt(q, k)
                if masked:
                    sc = jnp.where(_diag_mask(tq), sc, NEG_BIG)
                p = jnp.exp2(sc * SCALE_LOG2E - lse_ref[hh, rows, :])
                dv_sc[hh] += _dot_tn(p.astype(MXU), do)
                dp = _dot_nt(do, v_ref[:, lanes])
                ds = (p * (dp - dl_ref[hh, rows, :])).astype(MXU)
                dk_sc[hh] += _dot_tn(ds, q)
                dq_ref[rows, lanes] += _dot(ds, k) * ATTN_SCALE

        def off_diagonal_pair(p, carry):
            tile(j + 1 + 2 * p, False)
            tile(j + 2 + 2 * p, False)
            return carry

        below = nq - 1 - j
        tile(j, True)
        lax.fori_loop(0, below // 2, off_diagonal_pair, 0)

        @pl.when(below % 2 == 1)
        def _():
            tile(nq - 1, False)
        for hh in range(hb):
            lanes = slice(hh * HEAD_PAD, (hh + 1) * HEAD_PAD)
            dk_ref[:, lanes] = dk_sc[hh] * ATTN_SCALE
            dv_ref[:, lanes] = dv_sc[hh]
        @pl.when((g == groups - 1) & (j == nq - 1))
        def _():
            g_finish()
            s_finish()

    seq_spec = pl.BlockSpec((s, width), lambda g, j: (0, g))
    kv_spec = pl.BlockSpec((tq, width), lambda g, j: (j, g))
    col_spec = pl.BlockSpec((hb, s, 1), lambda g, j: (g, 0, 0))
    any_spec = pl.BlockSpec(memory_space=pl.ANY)
    outs = _call(
        body, name="attn_bwd", grid=(groups, nq), sem=("arbitrary", "arbitrary"),
        in_specs=[seq_spec, kv_spec, kv_spec, seq_spec, col_spec, col_spec] + [any_spec] * nc,
        out_specs=[seq_spec, kv_spec, kv_spec] + [any_spec] * nc,
        out_shape=[_sds(qp.shape, F32), _sds(qp.shape, F32), _sds(qp.shape, F32)]
        + [_scatter_out_shape(a, slot) for a, slot in scattered]
        + [_sds((N_DEV,) + a.shape, a.dtype) for a in gathered],
        scratch=[pltpu.VMEM((hb, tq, HEAD_PAD), F32), pltpu.VMEM((hb, tq, HEAD_PAD), F32)]
        + _comm_sems(ns) + _comm_sems(ng),
    )(qp, kp, vp, d_o, lse, delta, *[a for a, _ in scattered], *gathered)
    return outs[0], outs[1], outs[2], outs[3:3 + ns], outs[3 + ns:]


def _gmlp_bwd(z, d_sgu, ln_g, ln_b, w_sp, bias_exp, tm):
    s = z.shape[0]
    nblk = tm // GM_CHUNK

    def body(zu_ref, zv_ref, dsgu_ref, lg_ref, lb_ref, w_ref, be_ref,
             dguv_ref, dws_ref, dbs_ref, dlg_ref, dlb_ref, dbe_sc, dvln_sc, dlg_sc, dlb_sc):
        i = pl.program_id(0)

        @pl.when(i == 0)
        def _():
            for r in (dws_ref, dlg_sc, dlb_sc, dbe_sc):
                r[...] = jnp.zeros(r.shape, F32)

        seg = _seg_matrix()
        mask = _spatial_mask()
        wm = [(w_ref[h] * mask).astype(MXU) for h in range(HEADS)]
        zu, zv = zu_ref[...], zv_ref[...]
        gu = _gelu(zu)
        _, vhat, rstd = _gm_norm(zv, seg)
        lg = lg_ref[...]
        vln = (vhat * lg + lb_ref[...]).astype(MXU)
        d_sgu = dsgu_ref[...]
        for n in range(nblk):
            rows = slice(n * GM_CHUNK, (n + 1) * GM_CHUNK)
            vb = vln[rows]
            mixed = _gm_mix(wm, vb, GM_CHUNK) + be_ref[...]
            d_mixed = d_sgu[rows] * gu[rows]
            dguv_ref[rows, pl.ds(0, GM_WIDTH)] = ((d_sgu[rows] * mixed) * _gelu_grad(zu[rows])).astype(MXU)
            dbe_sc[...] += d_mixed
            dmb = d_mixed.astype(MXU)
            d_vln = jnp.zeros((GM_CHUNK, GM_WIDTH), F32)
            for h in range(HEADS):
                hm = _head_lane_mask(h, GM_CHUNK)
                dws_ref[h] += _dot_nt(jnp.where(hm, dmb, jnp.zeros_like(dmb)), vb)
                d_vln = d_vln + jnp.where(hm, _dot_tn(wm[h], dmb), 0.0)
            dvln_sc[rows, :] = d_vln
        d_vln = dvln_sc[...]
        dlg_sc[...] += jnp.sum(d_vln * vhat, axis=0, keepdims=True)
        dlb_sc[...] += jnp.sum(d_vln, axis=0, keepdims=True)
        d_vhat = d_vln * lg
        d_gv = rstd * ((d_vhat - _split_dot(d_vhat, seg)) - vhat * _split_dot(d_vhat * vhat, seg))
        dguv_ref[:, pl.ds(GM_WIDTH, GM_WIDTH)] = (d_gv * _gelu_grad(zv)).astype(MXU)

        @pl.when(i == pl.num_programs(0) - 1)
        def _():
            for h in range(HEADS):
                dws_ref[h] = dws_ref[h] * mask
            hrow = lax.broadcasted_iota(jnp.int32, (HEADS, GM_WIDTH), 0)
            hlane = lax.broadcasted_iota(jnp.int32, (HEADS, GM_WIDTH), 1) >> 6
            ind = jnp.where(hrow == hlane, 1.0, 0.0).astype(MXU)
            acc = dbe_sc[...]
            hi = acc.astype(MXU)
            lo = (acc - hi.astype(F32)).astype(MXU)
            dbs_ref[...] = _dot_nt(ind, hi) + _dot_nt(ind, lo)
            pick = (lax.broadcasted_iota(jnp.int32, (GM_WIDTH, GM_DIM), 0) & (GM_DIM - 1)
                    == lax.broadcasted_iota(jnp.int32, (GM_WIDTH, GM_DIM), 1))
            pick = jnp.where(pick, 1.0, 0.0).astype(MXU)
            for src, dst in ((dlg_sc, dlg_ref), (dlb_sc, dlb_ref)):
                spread = jnp.where(hrow == hlane, jnp.broadcast_to(src[...], (HEADS, GM_WIDTH)), 0.0)
                dst[...] = _split_dot3(spread, pick)

    return _call(
        body, name="gmlp_bwd", grid=(s // tm,), sem=("arbitrary",),
        in_specs=[_rows(tm, GM_WIDTH, 1), _rows(tm, GM_WIDTH, 2), _rows(tm, GM_WIDTH), _full(ln_g.shape),
                  _full(ln_b.shape), _full(w_sp.shape), _full(bias_exp.shape)],
        out_specs=[_rows(tm, 2 * GM_WIDTH), _full(w_sp.shape), _full((HEADS, GM_CHUNK)), _full((HEADS, GM_DIM)),
                   _full((HEADS, GM_DIM))],
        out_shape=[_sds((s, 2 * GM_WIDTH), MXU), _sds(w_sp.shape, F32), _sds((HEADS, GM_CHUNK), F32),
                   _sds((HEADS, GM_DIM), F32), _sds((HEADS, GM_DIM), F32)],
        scratch=[pltpu.VMEM((GM_CHUNK, GM_WIDTH), F32), pltpu.VMEM((tm, GM_WIDTH), F32),
                 pltpu.VMEM((1, GM_WIDTH), F32), pltpu.VMEM((1, GM_WIDTH), F32)],
    )(z, z, d_sgu, ln_g, ln_b, w_sp, bias_exp)


def _mix_in_bwd(dq, dk, dv, z, d_guv, x, d_x_part, ada_raw, ada_b, g_pre, g_q, g_kv, w1t, wqt, wkv,
                cos_t, sin_t, tm):
    s = x.shape[0]
    hp = HEADS * HEAD_PAD
    za = Q_LORA + KV_LORA + HEAD_PAD

    def body(dq_ref, dk_ref, dv_ref, z_ref, dguv_ref, x_ref, dxp_ref, ar_ref, ab_ref, g_ref, gq_ref, gkv_ref,
             w1_ref, wq_ref, wkv_ref, cos_ref, sin_ref,
             gx_ref, dza_ref, dqp_ref, dkvp_ref, dsh_ref, dsc_ref, dg_ref, dgq_ref, dgkv_ref):
        i = pl.program_id(0)

        @pl.when(i == 0)
        def _():
            for r in (dsh_ref, dsc_ref, dg_ref, dgq_ref, dgkv_ref):
                r[...] = jnp.zeros(r.shape, F32)

        cos, sin = cos_ref[...], sin_ref[...]
        d_krot = jnp.zeros((tm, HEAD_PAD), F32)
        for h in range(HEADS):
            blk = slice(h * HEAD_PAD, (h + 1) * HEAD_PAD)
            dqp_ref[:, blk] = _rope_transposed(dq_ref[:, blk], cos, sin).astype(MXU)
            dk_h = dk_ref[:, blk]
            d_krot = d_krot + dk_h
            dkvp_ref[:, blk] = dk_h.astype(MXU)
        dkvp_ref[:, pl.ds(hp, hp)] = dv_ref[...].astype(MXU)
        lane = lax.broadcasted_iota(jnp.int32, (tm, HEAD_PAD), 1)
        d_kr = jnp.where((lane >= NOPE) & (lane < NOPE + ROPE), _rope_transposed(d_krot, cos, sin), 0.0)
        d_cqn = _dot(dqp_ref[...], wq_ref[...])
        d_ckvn = _dot_nt(dkvp_ref[...], wkv_ref[...])
        zt = z_ref[...]
        gq, gkv = gq_ref[...], gkv_ref[...]
        cq_hat, rq = _rms(zt[:, :Q_LORA])
        ckv_hat, rkv = _rms(zt[:, Q_LORA:Q_LORA + KV_LORA])
        dgq_ref[...] += jnp.sum(d_cqn * cq_hat, axis=0, keepdims=True)
        dgkv_ref[...] += jnp.sum(d_ckvn * ckv_hat, axis=0, keepdims=True)
        d_cq = _rms_bwd(d_cqn * gq, cq_hat, rq)
        d_ckv = _rms_bwd(d_ckvn * gkv, ckv_hat, rkv)
        d_za = jnp.concatenate([d_cq, d_ckv, d_kr], axis=1).astype(MXU)
        dza_ref[...] = d_za
        d_h1 = _dot(d_za, w1_ref[pl.ds(0, za), :]) + _dot(dguv_ref[...], w1_ref[pl.ds(za, 2 * GM_WIDTH), :])
        sc1 = _row(ar_ref, 1) + _row(ab_ref, 1)
        g = g_ref[...]
        xn, r1 = _rms(x_ref[...])
        dsh_ref[...] += jnp.sum(d_h1, axis=0, keepdims=True)
        dsc_ref[...] += jnp.sum(d_h1 * (xn * g), axis=0, keepdims=True)
        d_mod = d_h1 * (1.0 + sc1)
        dg_ref[...] += jnp.sum(d_mod * xn, axis=0, keepdims=True)
        gx_ref[...] = dxp_ref[...] + _rms_bwd(d_mod * g, xn, r1)

    vec = pl.BlockSpec((1, D_MODEL), lambda i: (0, 0))
    return _call(
        body, name="mix_in_bwd", grid=(s // tm,), sem=("arbitrary",),
        in_specs=[_rows(tm, hp), _rows(tm, hp), _rows(tm, hp), _rows(tm, za), _rows(tm, 2 * GM_WIDTH),
                  _rows(tm, D_MODEL), _rows(tm, D_MODEL), _full(ada_raw.shape), _full(ada_b.shape), _full(g_pre.shape),
                  _full(g_q.shape), _full(g_kv.shape), _full(w1t.shape), _full(wqt.shape),
                  _full(wkv.shape), _rows(tm, HEAD_PAD), _rows(tm, HEAD_PAD)],
        out_specs=[_rows(tm, D_MODEL), _rows(tm, za), _rows(tm, hp), _rows(tm, 2 * hp), vec, vec, vec,
                   _full(g_q.shape), _full(g_kv.shape)],
        out_shape=[_sds((s, D_MODEL), F32), _sds((s, za), MXU), _sds((s, hp), MXU), _sds((s, 2 * hp), MXU),
                   _sds((1, D_MODEL), F32), _sds((1, D_MODEL), F32), _sds((1, D_MODEL), F32),
                   _sds(g_q.shape, F32), _sds(g_kv.shape, F32)],
    )(dq, dk, dv, z, d_guv, x, d_x_part, ada_raw, ada_b, g_pre, g_q, g_kv, w1t, wqt, wkv, cos_t, sin_t)


def _tn_matmul(a, b, name, ts):
    ga, s, m = a.shape
    gb, _, n = b.shape
    g = max(ga, gb)
    tn = n if n <= 1024 else 1024
    steps = s // ts

    def body(a_ref, b_ref, o_ref, acc):
        k = pl.program_id(2)

        @pl.when(k == 0)
        def _():
            acc[...] = jnp.zeros(acc.shape, F32)

        acc[...] += _dot_tn(a_ref[0], b_ref[0])

        @pl.when(k == steps - 1)
        def _():
            o_ref[0] = acc[...].astype(MXU)

    return _call(
        body, name=name, grid=(g, n // tn, steps), sem=("parallel", "parallel", "arbitrary"),
        in_specs=[pl.BlockSpec((1, ts, m), lambda gi, ni, k: (gi if ga > 1 else 0, k, 0)),
                  pl.BlockSpec((1, ts, tn), lambda gi, ni, k: (gi if gb > 1 else 0, k, ni))],
        out_specs=pl.BlockSpec((1, m, tn), lambda gi, ni, k: (gi, 0, ni)),
        out_shape=_sds((g, m, n), MXU),
        scratch=[pltpu.VMEM((m, tn), F32)],
    )(a, b)


def _adamw(w, g, m, v):
    m2 = ADAM_B1 * m + (1.0 - ADAM_B1) * g
    v2 = ADAM_B2 * v + (1.0 - ADAM_B2) * (g * g)
    m_hat = m2 / (1.0 - ADAM_B1 ** ADAM_STEP)
    v_hat = v2 / (1.0 - ADAM_B2 ** ADAM_STEP)
    delta = -ADAM_LR * (m_hat / (jnp.sqrt(v_hat) + ADAM_EPS) + ADAM_WD * w)
    return delta, m2, v2


def _adam_reduce(recv, w, m, v, name):
    r, c = w.shape
    tr = r if r <= 512 else max(t for t in range(16, 513, 16) if r % t == 0)

    def body(p_ref, w_ref, m_ref, v_ref, g_ref, d_ref, mo_ref, vo_ref):
        g = p_ref[0].astype(F32)
        for j in range(1, N_DEV):
            g = g + p_ref[j].astype(F32)
        g_ref[...] = g
        d_ref[...], mo_ref[...], vo_ref[...] = _adamw(w_ref[...], g, m_ref[...], v_ref[...])

    blk = pl.BlockSpec((tr, c), lambda i: (i, 0))
    return _call(
        body, name=name, grid=(r // tr,), sem=("parallel",),
        in_specs=[pl.BlockSpec((N_DEV, tr, c), lambda i: (0, i, 0)), blk, blk, blk],
        out_specs=[blk] * 4, out_shape=[_sds((r, c), F32)] * 4,
    )(recv, w, m, v)


def _adam_w_ada(c_act_t, d_ada_cols, w, m, v):
    r, c = w.shape
    tr = 256

    def body(ct_ref, da_ref, w_ref, m_ref, v_ref, g_ref, d_ref, mo_ref, vo_ref):
        g = ct_ref[:, pl.ds(0, 1)] * da_ref[pl.ds(0, 1), :]
        for b in range(1, N_DEV):
            g = g + ct_ref[:, pl.ds(b, 1)] * da_ref[pl.ds(b, 1), :]
        g_ref[...] = g
        d_ref[...], mo_ref[...], vo_ref[...] = _adamw(w_ref[...], g, m_ref[...], v_ref[...])

    blk = pl.BlockSpec((tr, c), lambda i: (i, 0))
    return _call(
        body, name="adam_w_ada", grid=(r // tr,), sem=("parallel",),
        in_specs=[pl.BlockSpec((tr, N_DEV), lambda i: (i, 0)), _full(d_ada_cols.shape), blk, blk, blk],
        out_specs=[blk] * 4, out_shape=[_sds((r, c), F32)] * 4,
    )(c_act_t, d_ada_cols, w, m, v)


VEC_ROWS = D_MODEL // 128
PK_ADA = 0
PK_GAIN = PK_ADA + 6 * VEC_ROWS
PK_GQ = PK_GAIN + 4 * VEC_ROWS
PK_GKV = PK_GQ + Q_LORA // 128
PK_LOSS = PK_GKV + KV_LORA // 128
PK_LNG = 88
PK_LNB = PK_LNG + HEADS
PK_BS = PK_LNB + HEADS
PK_CB = PK_BS + HEADS
CB_ROWS = 6
PK_WS = PK_CB + N_DEV * CB_ROWS
PK_ROWS = PK_WS + HEADS * GM_CHUNK
assert PK_LOSS < PK_LNG and PK_ROWS % 8 == 0
LATE_GAIN = 2 * VEC_ROWS
LATE_GQ = 3 * VEC_ROWS
LATE_GKV = LATE_GQ + Q_LORA // 128
LATE_ROWS = 32


def _cb_chunks():
    return [(k, k * 128, min(128, FF_BLK - k * 128)) for k in range(CB_ROWS)]


def _put_rows(out_ref, row0, ref, width):
    for k in range(width // 128):
        out_ref[pl.ds(row0 + k, 1), :] = ref[:, pl.ds(k * 128, 128)]


def _pack_small(ada_rows, gains, loss_part, d_ln_g, d_ln_b, d_bs, d_cb, d_ws):
    half = N_DEV // 2

    def body(*refs):
        vec_refs = refs[:7]
        loss_ref, lng_ref, lnb_ref, bs_ref, cb_ref, ws_ref, out_ref = refs[7:]
        out_ref[pl.ds(0, PK_WS), :] = jnp.zeros((PK_WS, 128), F32)
        for n, ref in enumerate(vec_refs[:4]):
            _put_rows(out_ref, PK_ADA + (2 + n) * VEC_ROWS, ref, D_MODEL)
        for n, ref in enumerate(vec_refs[4:]):
            _put_rows(out_ref, PK_GAIN + (1 + n) * VEC_ROWS, ref, D_MODEL)
        _put_rows(out_ref, PK_LOSS, loss_ref, 128)
        out_ref[pl.ds(PK_LNG, HEADS), pl.ds(0, GM_DIM)] = lng_ref[...]
        out_ref[pl.ds(PK_LNB, HEADS), pl.ds(0, GM_DIM)] = lnb_ref[...]
        out_ref[pl.ds(PK_BS, HEADS), :] = bs_ref[...]
        for j in range(N_DEV):
            for k, lane, width in _cb_chunks():
                out_ref[pl.ds(PK_CB + j * CB_ROWS + k, 1), pl.ds(0, width)] = cb_ref[j % half, j // half, :, pl.ds(lane, width)]
        for h in range(HEADS):
            out_ref[pl.ds(PK_WS + h * GM_CHUNK, GM_CHUNK), :] = ws_ref[h]

    ins = list(ada_rows) + list(gains) + [loss_part, d_ln_g, d_ln_b, d_bs, d_cb, d_ws]
    return _call(body, name="pack_small", grid=(1,), in_specs=[_full(a.shape) for a in ins],
                 out_specs=_full((PK_ROWS, 128)), out_shape=_sds((PK_ROWS, 128), F32))(*ins)


def _pack_late(d_sh1, d_sc1, d_g_pre_mix, d_g_q, d_g_kv):
    def body(sh_ref, sc_ref, g_ref, gq_ref, gkv_ref, out_ref):
        out_ref[...] = jnp.zeros((LATE_ROWS, 128), F32)
        _put_rows(out_ref, 0, sh_ref, D_MODEL)
        _put_rows(out_ref, VEC_ROWS, sc_ref, D_MODEL)
        _put_rows(out_ref, LATE_GAIN, g_ref, D_MODEL)
        _put_rows(out_ref, LATE_GQ, gq_ref, Q_LORA)
        _put_rows(out_ref, LATE_GKV, gkv_ref, KV_LORA)

    ins = [d_sh1, d_sc1, d_g_pre_mix, d_g_q, d_g_kv]
    return _call(body, name="pack_late", grid=(1,), in_specs=[_full(a.shape) for a in ins],
                 out_specs=_full((LATE_ROWS, 128)), out_shape=_sds((LATE_ROWS, 128), F32))(*ins)


def _adam_small(gathered, late, params):
    n_par = len(params)

    def body(p_ref, late_ref, *refs):
        ins = [refs[3 * n:3 * n + 3] for n in range(n_par)]
        outs = [refs[3 * n_par + 4 * n:3 * n_par + 4 * n + 4] for n in range(n_par)]
        loss_ref, dada_ref = refs[7 * n_par:]

        def total(rows, lanes=slice(None), src=p_ref):
            g = src[0, rows, lanes]
            for j in range(1, N_DEV):
                g = g + src[j, rows, lanes]
            return g

        def apply(n, g, idx):
            w_ref, m_ref, v_ref = ins[n]
            d, m2, v2 = _adamw(w_ref[idx], g, m_ref[idx], v_ref[idx])
            for ref, val in zip(outs[n], (g, d, m2, v2)):
                ref[idx] = val

        def vector(n, src, row0, width, lane0=0):
            for k in range(width // 128):
                apply(n, total(pl.ds(row0 + k, 1), src=src), (slice(None), pl.ds(lane0 + k * 128, 128)))

        vector(0, late_ref, 0, 2 * D_MODEL)
        vector(0, p_ref, PK_ADA + 2 * VEC_ROWS, 4 * D_MODEL, lane0=2 * D_MODEL)
        vector(1, late_ref, LATE_GAIN, D_MODEL)
        for n in range(1, 4):
            vector(1 + n, p_ref, PK_GAIN + n * VEC_ROWS, D_MODEL)
        vector(5, late_ref, LATE_GQ, Q_LORA)
        vector(6, late_ref, LATE_GKV, KV_LORA)
        apply(7, total(pl.ds(PK_LNG, HEADS), pl.ds(0, GM_DIM)), (0,))
        apply(8, total(pl.ds(PK_LNB, HEADS), pl.ds(0, GM_DIM)), (0,))
        for h in range(HEADS):
            apply(9, total(pl.ds(PK_WS + h * GM_CHUNK, GM_CHUNK)), (0, h))
        apply(10, total(pl.ds(PK_BS, HEADS)), (0,))
        for j in range(N_DEV):
            for k, lane, width in _cb_chunks():
                apply(11, total(pl.ds(PK_CB + j * CB_ROWS + k, 1), pl.ds(0, width)), (pl.ds(j, 1), pl.ds(lane, width)))
        loss_ref[...] = total(pl.ds(PK_LOSS, 1))
        dada_ref[:, pl.ds(0, 2 * VEC_ROWS), :] = late_ref[:, pl.ds(0, 2 * VEC_ROWS), :]
        dada_ref[:, pl.ds(2 * VEC_ROWS, 4 * VEC_ROWS), :] = p_ref[:, pl.ds(PK_ADA + 2 * VEC_ROWS, 4 * VEC_ROWS), :]

    flat = [a for triple in params for a in triple]
    out_shape = [_sds(w.shape, F32) for w, _, _ in params for _ in range(4)]
    out_shape += [_sds((1, 128), F32), _sds((N_DEV, 6 * VEC_ROWS, 128), F32)]
    outs = _call(body, name="adam_small", grid=(1,),
                 in_specs=[_full(gathered.shape), _full(late.shape)] + [_full(a.shape) for a in flat],
                 out_specs=[_full(o.shape) for o in out_shape], out_shape=out_shape)(gathered, late, *flat)
    return [tuple(outs[4 * n:4 * n + 4]) for n in range(n_par)], outs[-2], outs[-1]


def _rope_tables(s):
    pos = jnp.arange(s, dtype=F32)
    inv = ROPE_THETA ** (-jnp.arange(0, ROPE, 2, dtype=F32) / ROPE)
    ang = pos[:, None] * inv[None, :]
    cos, sin = jnp.cos(ang), jnp.sin(ang)
    ones, zeros = jnp.ones((s, NOPE), F32), jnp.zeros((s, NOPE), F32)
    cos_t = jnp.concatenate([ones, cos, cos, ones[:, :HEAD_PAD - NOPE - ROPE]], axis=1)
    sin_t = jnp.concatenate([zeros, sin, sin, zeros[:, :HEAD_PAD - NOPE - ROPE]], axis=1)
    return cos_t, sin_t


def kernel(x, c, w_ada, b_ada, g_pre_mix, g_post_mix, w_in, g_q, w_uq, g_kv, w_ukv, gm_ln_g, gm_ln_b, w_spatial, b_spatial, w_out, g_pre_ffn, g_post_ffn, w_up, conv_w, conv_b, w_down, loss_target, m_w_ada, m_b_ada, m_g_pre_mix, m_g_post_mix, m_w_in, m_g_q, m_w_uq, m_g_kv, m_w_ukv, m_gm_ln_g, m_gm_ln_b, m_w_spatial, m_b_spatial, m_w_out, m_g_pre_ffn, m_g_post_ffn, m_w_up, m_conv_w, m_conv_b, m_w_down, v_w_ada, v_b_ada, v_g_pre_mix, v_g_post_mix, v_w_in, v_g_q, v_w_uq, v_g_kv, v_w_ukv, v_gm_ln_g, v_gm_ln_b, v_w_spatial, v_b_spatial, v_w_out, v_g_pre_ffn, v_g_post_ffn, v_w_up, v_conv_w, v_conv_b, v_w_down):
    s = x.shape[1]
    tm = min(256, s)
    tf = min(2 * ROW_SUB, s)
    tq = min(512, s)
    ts = min(2048, s)
    hp = HEADS * HEAD_PAD
    half = N_DEV // 2
    my_slot = 4 * lax.axis_index("x") + 2 * lax.axis_index("y") + lax.axis_index("c")
    x2d, target = x[0], loss_target[0]

    def t_(a):
        return jnp.swapaxes(a[0], 0, 1)

    w_in_t, m_in_t, v_in_t = t_(w_in), t_(m_w_in), t_(v_w_in)
    w_uq_t, m_uq_t, v_uq_t = t_(w_uq), t_(m_w_uq), t_(v_w_uq)
    w_up_t, m_up_t, v_up_t = t_(w_up), t_(m_w_up), t_(v_w_up)
    (g_c, g_in_t, g_uq_t, g_ukv, g_cw), _ = _exchange(
        [c, w_in_t.astype(MXU), w_uq_t.astype(MXU), w_ukv[0].astype(MXU), conv_w[0]], [], "gather_mixer_weights")

    w_in_f = g_in_t.reshape(-1, D_MODEL)
    o1, o2, o3 = Q_LORA, Q_LORA + KV_LORA, Q_LORA + KV_LORA + ROPE
    w1t = jnp.concatenate([w_in_f[:o2], jnp.zeros((NOPE, D_MODEL), MXU), w_in_f[o2:o3],
                           jnp.zeros((HEAD_PAD - NOPE - ROPE, D_MODEL), MXU), w_in_f[o3:]], axis=0)
    wqt = jnp.pad(g_uq_t, ((0, 0), (0, HEAD_PAD - NOPE - ROPE), (0, 0))).reshape(hp, Q_LORA)
    w_ukv_f = jnp.transpose(g_ukv, (1, 0, 2)).reshape(KV_LORA, HEADS, 2 * NOPE)
    pad_head = ((0, 0), (0, 0), (0, HEAD_PAD - NOPE))
    wkv = jnp.concatenate([jnp.pad(w_ukv_f[:, :, :NOPE], pad_head).reshape(KV_LORA, hp),
                           jnp.pad(w_ukv_f[:, :, NOPE:], pad_head).reshape(KV_LORA, hp)], axis=1)
    cb8 = conv_b.reshape(N_DEV, 1, FF_BLK)
    bias_exp = jnp.repeat(b_spatial[0].T, GM_DIM, axis=1)
    ln_g, ln_b = gm_ln_g.reshape(1, GM_WIDTH), gm_ln_b.reshape(1, GM_WIDTH)
    w_sp = w_spatial[0]
    cos_t, sin_t = _rope_tables(s)

    ada_part, c_act = _ada_fwd(g_c.reshape(N_DEV, D_MODEL), w_ada[0])
    _, (ada_recv,) = _exchange([], [(ada_part.reshape(N_DEV, 1, -1), _plain_slot)], "ada_rows")
    ada_raw = ada_recv.reshape(6, D_MODEL)
    ada_b = b_ada.reshape(6, D_MODEL)

    h1, z, qp, kp, vp, cqn, ckvn = _mix_in_fwd(x2d, ada_raw, ada_b, g_pre_mix, w1t, g_q, g_kv, wqt, wkv, cos_t, sin_t, tm)
    sgu = _gmlp_fwd(z, ln_g, ln_b, w_sp, bias_exp, tm)
    o_pad, lse, (g_out, g_up, g_down) = _attn_fwd(
        qp, kp, vp, tq, [w_out[0].astype(MXU), w_up_t.astype(MXU), w_down[0].astype(MXU)])
    w_out_f = g_out.reshape(2 * GM_WIDTH, D_MODEL)
    wo_attn = jnp.pad(w_out_f[:GM_WIDTH].reshape(HEADS, NOPE, D_MODEL), ((0, 0), (0, HEAD_PAD - NOPE), (0, 0)))
    wo = jnp.concatenate([wo_attn.reshape(hp, D_MODEL), w_out_f[GM_WIDTH:]], axis=0)
    wd = g_down.reshape(half, FF_BLK, D_MODEL)
    m_mix, x2, h2 = _out_proj_fwd(o_pad, sgu, wo, x2d, ada_raw, ada_b, g_post_mix, g_pre_ffn, tm)
    up_a, up_b, y_a, y_b, act = _ffn_up_fwd(h2, g_up, g_cw, cb8, tf)
    d_out, d_f, loss_part, d_gt2, d_g_post_ffn = _ffn_down_fwd(act, wd, x2, target, ada_raw, ada_b, g_post_ffn, tf)

    d_up, d_cw, d_cb, p_down, p_up = _ffn_down_bwd(d_f, wd, up_a, up_b, y_a, y_b, g_cw, act, h2, tf)
    p_down = p_down.reshape(N_DEV, -1, D_MODEL)
    d_x2, d_m, d_sh2, d_sc2, d_g_pre_ffn, d_gt1, d_g_post_mix = _ffn_up_bwd(
        d_up, g_up, x2, m_mix, d_out, ada_raw, ada_b, g_pre_ffn, g_post_mix, tm)
    d_m3 = d_m[None]
    dwo_attn = _tn_matmul(o_pad[None], d_m3, "dw_out_attn", ts)[0].reshape(HEADS, HEAD_PAD, D_MODEL)[:, :NOPE]
    dwo_sgu = _tn_matmul(sgu[None], d_m3, "dw_out_sgu", ts)[0]
    p_out = jnp.concatenate([dwo_attn.reshape(GM_WIDTH, D_MODEL), dwo_sgu], axis=0).reshape(N_DEV, -1, D_MODEL)
    d_o, d_sgu, delta = _out_proj_bwd(d_m, wo, o_pad, tm)
    d_guv, d_ws, d_bs, d_ln_g, d_ln_b = _gmlp_bwd(z, d_sgu, ln_g, ln_b, w_sp, bias_exp, tm)
    packed = _pack_small([d_gt1, d_sh2, d_sc2, d_gt2], [d_g_post_mix, d_g_pre_ffn, d_g_post_ffn], loss_part,
                         d_ln_g, d_ln_b, d_bs, d_cb, d_ws)

    def ffn_slot(j):
        return (j % half, j // half)

    dq, dk, dv, (r_out, r_up, r_down, r_cw), (g_small,) = _attn_bwd(
        qp, kp, vp, d_o, lse, delta, tq,
        [(p_out, _plain_slot), (p_up, ffn_slot), (p_down, _plain_slot), (d_cw, ffn_slot)], [packed])
    grad_x, d_za, d_qp, d_kvp, d_sh1, d_sc1, d_g_pre_mix, d_g_q, d_g_kv = _mix_in_bwd(
        dq, dk, dv, z, d_guv, x2d, d_x2, ada_raw, ada_b, g_pre_mix, g_q, g_kv, w1t, wqt, wkv, cos_t, sin_t, tm)
    h1_3 = h1[None]
    dw1a = _tn_matmul(d_za[None], h1_3, "dw_in_a", ts)[0]
    dw1b = _tn_matmul(d_guv[None], h1_3, "dw_in_b", ts)[0]
    d_w_in_t = jnp.concatenate([dw1a[:o2], dw1a[o2 + NOPE:o2 + NOPE + ROPE], dw1b], axis=0)
    p_in = d_w_in_t.reshape(N_DEV, -1, D_MODEL)
    p_uq = _tn_matmul(d_qp[None], cqn[None], "dw_uq", ts)[0].reshape(HEADS, HEAD_PAD, Q_LORA)[:, :NOPE + ROPE]
    dwkv = _tn_matmul(ckvn[None], d_kvp[None], "dw_ukv", ts)[0]
    dwk = dwkv[:, :hp].reshape(KV_LORA, HEADS, HEAD_PAD)[:, :, :NOPE]
    dwv = dwkv[:, hp:].reshape(KV_LORA, HEADS, HEAD_PAD)[:, :, :NOPE]
    p_ukv = jnp.transpose(jnp.concatenate([dwk, dwv], axis=2), (1, 0, 2))

    (g_late,), (r_in, r_uq, r_ukv) = _exchange(
        [_pack_late(d_sh1, d_sc1, d_g_pre_mix, d_g_q, d_g_kv)],
        [(p_in, _plain_slot), (p_uq, _plain_slot), (p_ukv, _plain_slot)], "final_exchange")
    small_params = [(b_ada, m_b_ada, v_b_ada), (g_pre_mix, m_g_pre_mix, v_g_pre_mix),
                    (g_post_mix, m_g_post_mix, v_g_post_mix), (g_pre_ffn, m_g_pre_ffn, v_g_pre_ffn),
                    (g_post_ffn, m_g_post_ffn, v_g_post_ffn), (g_q, m_g_q, v_g_q), (g_kv, m_g_kv, v_g_kv),
                    (gm_ln_g, m_gm_ln_g, v_gm_ln_g), (gm_ln_b, m_gm_ln_b, v_gm_ln_b),
                    (w_spatial, m_w_spatial, v_w_spatial), (b_spatial, m_b_spatial, v_b_spatial),
                    tuple(a.reshape(N_DEV, FF_BLK) for a in (conv_b, m_conv_b, v_conv_b))]
    small_out, loss_row, d_ada_all = _adam_small(g_small, g_late, small_params)
    small_out[11] = tuple(o.reshape(conv_b.shape) for o in small_out[11])
    loss = loss_row[0, 0]

    def big(recv, w, m, v, name):
        g, d, m2, v2 = _adam_reduce(recv, w[0], m[0], v[0], name)
        return g[None], d[None], m2[None], v2[None]

    def big_t(recv, w_t, m_t, v_t, name):
        return tuple(jnp.swapaxes(o, 0, 1)[None] for o in _adam_reduce(recv, w_t, m_t, v_t, name))

    a_in = big_t(r_in, w_in_t, m_in_t, v_in_t, "adam_w_in")
    a_uq = big_t(r_uq, w_uq_t, m_uq_t, v_uq_t, "adam_w_uq")
    a_ukv = big(r_ukv, w_ukv, m_w_ukv, v_w_ukv, "adam_w_ukv")
    a_out = big(r_out, w_out, m_w_out, v_w_out, "adam_w_out")
    a_up = big_t(r_up, w_up_t, m_up_t, v_up_t, "adam_w_up")
    a_down = big(r_down, w_down, m_w_down, v_w_down, "adam_w_down")
    ada_cols = w_ada.shape[2]
    d_ada_cols = lax.dynamic_slice(d_ada_all.reshape(N_DEV, 6 * D_MODEL), (0, my_slot * ada_cols), (N_DEV, ada_cols))
    a_ada = tuple(t[None] for t in _adam_w_ada(c_act.T, d_ada_cols, w_ada[0], m_w_ada[0], v_w_ada[0]))
    a_cw = big(r_cw, conv_w, m_conv_w, v_conv_w, "adam_conv_w")

    def small(k):
        return small_out[k]

    per_weight = [a_ada, small(0), small(1), small(2), a_in, small(5), a_uq, small(6), a_ukv, small(7), small(8),
                  small(9), small(10), a_out, small(3), small(4), a_up, a_cw, small(11), a_down]
    outs = [loss, grad_x[None]]
    for k in range(4):
        outs += [t[k] for t in per_weight]
    return tuple(outs)
```

```python
import functools

import jax
import jax.numpy as jnp
from jax import lax
from jax.experimental import pallas as pl
from jax.experimental.pallas import tpu as pltpu

F32 = jnp.float32
MXU = jnp.bfloat16

N_DEV = 8
D_MODEL = 1024
HEADS = 8
HEAD_PAD = 128
NOPE = 64
ROPE = 32
Q_LORA = 256
KV_LORA = 128
GM_WIDTH = 512
GM_DIM = 64
GM_CHUNK = 128
CHUNK_SHIFT = 6
ROPE_THETA = 10000.0
ATTN_SCALE = (NOPE + ROPE) ** -0.5
LOG2E = 1.4426950408889634
SCALE_LOG2E = ATTN_SCALE * LOG2E
Z_COLS = 1536
FF_BLK = 704
EPS = 1e-6
ADAM_LR = 0.001
ADAM_B1 = 0.9
ADAM_B2 = 0.999
ADAM_EPS = 1e-08
ADAM_WD = 0.01
ADAM_STEP = 10
VMEM_LIMIT = 56 * 1024 * 1024
MESH = pl.DeviceIdType.MESH


def _dot(a, b):
    return jnp.dot(a, b, preferred_element_type=F32)


def _dot_nt(a, b):
    return lax.dot_general(a, b, (((1,), (1,)), ((), ())), preferred_element_type=F32)


def _dot_tn(a, b):
    return lax.dot_general(a, b, (((0,), (0,)), ((), ())), preferred_element_type=F32)


def _call(body, *, name, grid, in_specs, out_specs, out_shape, scratch=(), sem=None):
    params = pltpu.CompilerParams(dimension_semantics=sem, vmem_limit_bytes=VMEM_LIMIT)
    return pl.pallas_call(body, name=name, grid=grid, in_specs=in_specs, out_specs=out_specs,
                          out_shape=out_shape, scratch_shapes=list(scratch), compiler_params=params)


def _full(shape):
    n = len(shape)
    return pl.BlockSpec(shape, lambda *_: (0,) * n)


def _rows(tm, cols, col_block=0):
    return pl.BlockSpec((tm, cols), lambda i: (i, col_block))


def _sds(shape, dtype):
    return jax.ShapeDtypeStruct(shape, dtype)


def _row(ref, k):
    return ref[pl.ds(k, 1), :]


def _rms(x):
    r = lax.rsqrt(jnp.mean(x * x, axis=-1, keepdims=True) + EPS)
    return x * r, r


def _rms_bwd(d_hat, hat, r):
    return r * (d_hat - hat * jnp.mean(d_hat * hat, axis=-1, keepdims=True))


def _rope_partner(t):
    lane = lax.broadcasted_iota(jnp.int32, t.shape, 1)
    swapped = jnp.where(lane < NOPE + ROPE // 2, -pltpu.roll(t, HEAD_PAD - ROPE // 2, 1), pltpu.roll(t, ROPE // 2, 1))
    return jnp.where((lane >= NOPE) & (lane < NOPE + ROPE), swapped, 0.0)


def _rope(t, cos, sin):
    return t * cos + _rope_partner(t) * sin


def _rope_transposed(g, cos, sin):
    return g * cos - _rope_partner(g * sin)


def _gelu(x):
    return x * (0.5 * (1.0 + jnp.tanh(0.7978845608028654 * (x + 0.044715 * (x * x * x)))))


def _gelu_grad(x):
    t = jnp.tanh(0.7978845608028654 * (x + 0.044715 * (x * x * x)))
    return 0.5 * (1.0 + t) + 0.5 * x * (1.0 - t * t) * (0.7978845608028654 * (1.0 + 3.0 * 0.044715 * (x * x)))


def _split_dot(x, mat):
    hi = x.astype(MXU)
    lo = (x - hi.astype(F32)).astype(MXU)
    return _dot(hi, mat) + _dot(lo, mat)


def _split_dot3(x, mat):
    hi = x.astype(MXU)
    r1 = x - hi.astype(F32)
    mid = r1.astype(MXU)
    lo = (r1 - mid.astype(F32)).astype(MXU)
    return (_dot(hi, mat) + _dot(mid, mat)) + _dot(lo, mat)


def _seg_matrix():
    r = lax.broadcasted_iota(jnp.int32, (GM_WIDTH, GM_WIDTH), 0) >> 6
    c = lax.broadcasted_iota(jnp.int32, (GM_WIDTH, GM_WIDTH), 1) >> 6
    return jnp.where(r == c, 1.0 / GM_DIM, 0.0).astype(MXU)


def _spatial_mask():
    i = lax.broadcasted_iota(jnp.int32, (GM_CHUNK, GM_CHUNK), 0) >> CHUNK_SHIFT
    j = lax.broadcasted_iota(jnp.int32, (GM_CHUNK, GM_CHUNK), 1) >> CHUNK_SHIFT
    return (j <= i).astype(F32)


def _head_lane_mask(h, rows):
    lane = lax.broadcasted_iota(jnp.int32, (rows, GM_WIDTH), 1) >> 6
    return lane == h


def _my_place():
    return lax.axis_index("x"), lax.axis_index("y"), lax.axis_index("c")


def _flat(p):
    return 4 * p[0] + 2 * p[1] + p[2]


def _comm_sems(n):
    return [pltpu.SemaphoreType.DMA((7 * n,)), pltpu.SemaphoreType.DMA((7 * n,)), pltpu.SemaphoreType.DMA((n,))]


def _gather_steps(ins, outs, sems):
    send_sems, recv_sems, local_sems = sems
    n = len(ins)
    x, y, c = _my_place()
    me, sibling = (x, y, c), (x, y, 1 - c)
    chips = [(1 - x, y), (x, 1 - y), (1 - x, 1 - y)]

    def copy(a, k, block, to, src=None):
        slot = outs[a].at[_flat(block)]
        return pltpu.make_async_remote_copy(
            src_ref=slot if src is None else src, dst_ref=slot,
            send_sem=send_sems.at[7 * a + k], recv_sem=recv_sems.at[7 * a + k],
            device_id=to, device_id_type=MESH)

    def mine():
        return [pltpu.make_async_copy(ins[a], outs[a].at[_flat(me)], local_sems.at[a]) for a in range(n)]

    def first():
        cps = []
        for a in range(n):
            cps.append(copy(a, 0, me, sibling, src=ins[a]))
            cps += [copy(a, 1 + j, me, (*chip, c), src=ins[a]) for j, chip in enumerate(chips)]
        return cps

    def passed():
        return [copy(a, 4 + j, (*chip, c), sibling) for a in range(n) for j, chip in enumerate(chips)]

    def start():
        for cp in mine() + first():
            cp.start()

    def forward():
        for a in range(n):
            for j, chip in enumerate(chips):
                copy(a, 1 + j, (*chip, c), me).wait_recv()
                copy(a, 4 + j, (*chip, c), sibling).start()

    def finish():
        for a in range(n):
            copy(a, 0, sibling, me).wait_recv()
            for j, chip in enumerate(chips):
                copy(a, 4 + j, (*chip, 1 - c), me).wait_recv()
        for cp in first() + passed():
            cp.wait_send()
        for cp in mine():
            cp.wait()

    return start, forward, finish


def _scatter_steps(ins, outs, sems, slots):
    send_sems, recv_sems, local_sems = sems
    n = len(ins)
    flips = [(fx, fy, fc) for fx in (0, 1) for fy in (0, 1) for fc in (0, 1)][1:]
    me = _my_place()

    def peer(f):
        return tuple(1 - v if b else v for v, b in zip(me, f))

    def copy(a, k, arriving=False):
        p = peer(flips[k])
        return pltpu.make_async_remote_copy(
            src_ref=ins[a].at[slots[a](_flat(p))], dst_ref=outs[a].at[_flat(p if arriving else me)],
            send_sem=send_sems.at[7 * a + k], recv_sem=recv_sems.at[7 * a + k],
            device_id=p, device_id_type=MESH)

    def mine():
        return [pltpu.make_async_copy(ins[a].at[slots[a](_flat(me))], outs[a].at[_flat(me)], local_sems.at[a])
                for a in range(n)]

    def start():
        for cp in mine() + [copy(a, k) for a in range(n) for k in range(7)]:
            cp.start()

    def finish():
        for a in range(n):
            for k in range(7):
                copy(a, k, arriving=True).wait_recv()
        for a in range(n):
            for k in range(7):
                copy(a, k).wait_send()
        for cp in mine():
            cp.wait()

    return start, finish


def _plain_slot(j):
    return (j,)


def _scatter_out_shape(arr, slot):
    return _sds((N_DEV,) + arr.shape[len(slot(0)):], arr.dtype)


def _exchange(gathered, scattered, name):
    ng, ns = len(gathered), len(scattered)
    slots = [slot for _, slot in scattered]

    def body(*refs):
        g_in, s_in = refs[:ng], refs[ng:ng + ns]
        g_out, s_out = refs[ng + ns:2 * ng + ns], refs[2 * ng + ns:2 * (ng + ns)]
        sems = refs[2 * (ng + ns):]
        g_start, g_forward, g_finish = _gather_steps(g_in, g_out, sems[:3])
        s_start, s_finish = _scatter_steps(s_in, s_out, sems[3:], slots)
        g_start()
        s_start()
        g_forward()
        g_finish()
        s_finish()

    any_spec = pl.BlockSpec(memory_space=pl.ANY)
    outs = pl.pallas_call(
        body, name=name,
        in_specs=[any_spec] * (ng + ns), out_specs=[any_spec] * (ng + ns),
        out_shape=[_sds((N_DEV,) + a.shape, a.dtype) for a in gathered]
        + [_scatter_out_shape(a, slot) for a, slot in scattered],
        scratch_shapes=_comm_sems(max(ng, 1)) + _comm_sems(max(ns, 1)),
    )(*gathered, *[a for a, _ in scattered])
    return outs[:ng], outs[ng:]


def _ada_fwd(c_all, w_ada):
    def body(c_ref, w_ref, part_ref, act_ref):
        cv = c_ref[...]
        act = cv * jax.nn.sigmoid(cv)
        act_ref[...] = act
        part_ref[...] = _dot(act.astype(MXU), w_ref[...].astype(MXU))

    cols = w_ada.shape[1]
    return _call(body, name="ada_fwd", grid=(1,),
                 in_specs=[_full(c_all.shape), _full(w_ada.shape)],
                 out_specs=[_full((N_DEV, cols)), _full(c_all.shape)],
                 out_shape=[_sds((N_DEV, cols), F32), _sds(c_all.shape, F32)])(c_all, w_ada)


def _mix_in_fwd(x, ada_raw, ada_b, g_pre, w1, g_q, g_kv, wq, wkv, cos_t, sin_t, tm):
    s = x.shape[0]

    def body(x_ref, ar_ref, ab_ref, g_ref, w1_ref, gq_ref, gkv_ref, wq_ref, wkv_ref, cos_ref, sin_ref,
             h1_ref, z_ref, qp_ref, kp_ref, vp_ref, cqn_ref, ckvn_ref):
        sh = _row(ar_ref, 0) + _row(ab_ref, 0)
        sc = _row(ar_ref, 1) + _row(ab_ref, 1)
        xn, _ = _rms(x_ref[...])
        hb = ((xn * g_ref[...]) * (1.0 + sc) + sh).astype(MXU)
        h1_ref[...] = hb
        z = _dot_nt(hb, w1_ref[...])
        z_ref[...] = z
        cos, sin = cos_ref[...], sin_ref[...]
        cqn = (_rms(z[:, :Q_LORA])[0] * gq_ref[...]).astype(MXU)
        ckvn = (_rms(z[:, Q_LORA:Q_LORA + KV_LORA])[0] * gkv_ref[...]).astype(MXU)
        cqn_ref[...] = cqn
        ckvn_ref[...] = ckvn
        q = _dot_nt(cqn, wq_ref[...])
        kv = _dot(ckvn, wkv_ref[...])
        k_rope = _rope(z[:, Q_LORA + KV_LORA:Q_LORA + KV_LORA + HEAD_PAD], cos, sin)
        for h in range(HEADS):
            blk = slice(h * HEAD_PAD, (h + 1) * HEAD_PAD)
            qp_ref[:, blk] = _rope(q[:, blk], cos, sin).astype(MXU)
            kp_ref[:, blk] = (kv[:, blk] + k_rope).astype(MXU)
        v_lane = lax.broadcasted_iota(jnp.int32, (tm, HEADS * HEAD_PAD), 1) & (HEAD_PAD - 1)
        vp_ref[...] = jnp.where(v_lane == NOPE, 1.0, kv[:, HEADS * HEAD_PAD:]).astype(MXU)

    hp = HEADS * HEAD_PAD
    return _call(
        body, name="mix_in_fwd", grid=(s // tm,), sem=("parallel",),
        in_specs=[_rows(tm, D_MODEL), _full(ada_raw.shape), _full(ada_b.shape), _full(g_pre.shape), _full(w1.shape),
                  _full(g_q.shape), _full(g_kv.shape), _full(wq.shape), _full(wkv.shape),
                  _rows(tm, HEAD_PAD), _rows(tm, HEAD_PAD)],
        out_specs=[_rows(tm, D_MODEL), _rows(tm, Z_COLS), _rows(tm, hp), _rows(tm, hp), _rows(tm, hp),
                   _rows(tm, Q_LORA), _rows(tm, KV_LORA)],
        out_shape=[_sds((s, D_MODEL), MXU), _sds((s, Z_COLS), F32), _sds((s, hp), MXU), _sds((s, hp), MXU),
                   _sds((s, hp), MXU), _sds((s, Q_LORA), MXU), _sds((s, KV_LORA), MXU)],
    )(x, ada_raw, ada_b, g_pre, w1, g_q, g_kv, wq, wkv, cos_t, sin_t)


def _gm_norm(zv, seg):
    gv = _gelu(zv)
    cen = gv - _split_dot(gv, seg)
    rstd = lax.rsqrt(_split_dot(cen * cen, seg) + EPS)
    return gv, cen * rstd, rstd


def _gm_mix(wm, vb, rows):
    out = jnp.zeros((rows, GM_WIDTH), F32)
    for h in range(HEADS):
        out = out + jnp.where(_head_lane_mask(h, rows), _dot(wm[h], vb), 0.0)
    return out


def _gmlp_fwd(z, ln_g, ln_b, w_sp, bias_exp, tm):
    s = z.shape[0]
    nblk = tm // GM_CHUNK

    def body(zu_ref, zv_ref, lg_ref, lb_ref, w_ref, be_ref, sgu_ref):
        seg = _seg_matrix()
        mask = _spatial_mask()
        wm = [(w_ref[h] * mask).astype(MXU) for h in range(HEADS)]
        gu = _gelu(zu_ref[...])
        _, vhat, _ = _gm_norm(zv_ref[...], seg)
        vln = (vhat * lg_ref[...] + lb_ref[...]).astype(MXU)
        for n in range(nblk):
            rows = slice(n * GM_CHUNK, (n + 1) * GM_CHUNK)
            mixed = _gm_mix(wm, vln[rows], GM_CHUNK) + be_ref[...]
            sgu_ref[rows, :] = (gu[rows] * mixed).astype(MXU)

    return _call(
        body, name="gmlp_fwd", grid=(s // tm,), sem=("parallel",),
        in_specs=[_rows(tm, GM_WIDTH, 1), _rows(tm, GM_WIDTH, 2), _full(ln_g.shape), _full(ln_b.shape),
                  _full(w_sp.shape), _full(bias_exp.shape)],
        out_specs=_rows(tm, GM_WIDTH), out_shape=_sds((s, GM_WIDTH), MXU),
    )(z, z, ln_g, ln_b, w_sp, bias_exp)


def _chunk_mask(n_q, n_k, q_off):
    qc = (q_off + lax.broadcasted_iota(jnp.int32, (n_q, n_k), 0)) >> CHUNK_SHIFT
    kc = lax.broadcasted_iota(jnp.int32, (n_q, n_k), 1) >> CHUNK_SHIFT
    return kc <= qc


NEG_BIG = -1e30
ATTN_HEADS_PER_STEP = 2


def _attn_fwd(qp, kp, vp, tq, gathered):
    s = qp.shape[0]
    nq = s // tq
    hb = ATTN_HEADS_PER_STEP
    groups = HEADS // hb
    width = hb * HEAD_PAD
    half = tq // 2
    ng = len(gathered)

    def body(q_ref, k_ref, v_ref, *rest):
        g_in, (o_ref, lse_ref), g_out = rest[:ng], rest[ng:ng + 2], rest[ng + 2:2 * ng + 2]
        m_sc, acc_sc = rest[2 * ng + 2:2 * ng + 4]
        g_start, g_forward, g_finish = _gather_steps(g_in, g_out, rest[2 * ng + 4:])
        g, i = pl.program_id(0), pl.program_id(1)
        pl.when((g == 0) & (i == 0))(g_start)
        pl.when((g == groups - 1) & (i == 0))(g_forward)
        m_sc[...] = jnp.full(m_sc.shape, NEG_BIG, F32)
        acc_sc[...] = jnp.zeros(acc_sc.shape, F32)

        def tile(k0, n_k=tq, q0=0, n_q=tq, q_off=None):
            k_rows, q_rows = pl.ds(k0, n_k), slice(q0, q0 + n_q)
            for hh in range(hb):
                lanes = slice(hh * HEAD_PAD, (hh + 1) * HEAD_PAD)
                sc = _dot_nt(q_ref[q_rows, lanes], k_ref[k_rows, lanes])
                if q_off is not None:
                    sc = jnp.where(_chunk_mask(n_q, n_k, q_off), sc, NEG_BIG)
                blocks = [sc[:, b * 128:(b + 1) * 128] for b in range(n_k // 128)]
                m_prev = m_sc[hh, q_rows]
                m_tile = jnp.max(functools.reduce(jnp.maximum, blocks), axis=-1, keepdims=True)
                m_new = jnp.maximum(m_prev, m_tile)
                alpha = jnp.exp2((m_prev - m_new) * SCALE_LOG2E)
                p = jnp.concatenate([jnp.exp2((b - m_new) * SCALE_LOG2E) for b in blocks], axis=1).astype(MXU)
                acc_sc[hh, q_rows] = alpha * acc_sc[hh, q_rows] + _dot(p, v_ref[k_rows, lanes])
                m_sc[hh, q_rows] = m_new

        def key_tile(j):
            return pl.multiple_of(j * tq, tq)

        def off_diagonal_pair(p, carry):
            tile(key_tile(2 * p))
            tile(key_tile(2 * p + 1))
            return carry

        lax.fori_loop(0, i // 2, off_diagonal_pair, 0)

        @pl.when(i % 2 == 1)
        def _():
            tile(key_tile(i - 1))

        tile(key_tile(i), half, 0, half, 0)
        tile(key_tile(i), tq, half, half, half)
        for hh in range(hb):
            lanes = slice(hh * HEAD_PAD, (hh + 1) * HEAD_PAD)
            acc = acc_sc[hh]
            denom = acc[:, NOPE:NOPE + 1]
            o_ref[:, lanes] = (acc / denom).astype(MXU)
            lse_ref[hh] = m_sc[hh][:, :1] * SCALE_LOG2E + jnp.log(denom) * LOG2E
        pl.when((g == groups - 1) & (i == nq - 1))(g_finish)

    q_spec = pl.BlockSpec((tq, width), lambda g, i: (i, g))
    kv_spec = pl.BlockSpec((s, width), lambda g, i: (0, g))
    any_spec = pl.BlockSpec(memory_space=pl.ANY)
    outs = _call(
        body, name="attn_fwd", grid=(groups, nq), sem=("arbitrary", "arbitrary"),
        in_specs=[q_spec, kv_spec, kv_spec] + [any_spec] * ng,
        out_specs=[q_spec, pl.BlockSpec((hb, tq, 1), lambda g, i: (g, i, 0))] + [any_spec] * ng,
        out_shape=[_sds(qp.shape, MXU), _sds((HEADS, s, 1), F32)]
        + [_sds((N_DEV,) + a.shape, a.dtype) for a in gathered],
        scratch=[pltpu.VMEM((hb, tq, HEAD_PAD), F32), pltpu.VMEM((hb, tq, HEAD_PAD), F32)] + _comm_sems(ng),
    )(qp, kp, vp, *gathered)
    return outs[0], outs[1], outs[2:]


def _out_proj_fwd(o_pad, sgu, wo, x, ada_raw, ada_b, g_post_mix, g_pre_ffn, tm):
    s = x.shape[0]
    hp = HEADS * HEAD_PAD

    def body(o_ref, sgu_ref, wo_ref, x_ref, ar_ref, ab_ref, gpm_ref, gpf_ref, m_ref, x2_ref, h2_ref):
        gt1 = _row(ar_ref, 2) + _row(ab_ref, 2)
        sh2 = _row(ar_ref, 3) + _row(ab_ref, 3)
        sc2 = _row(ar_ref, 4) + _row(ab_ref, 4)
        m = _dot(o_ref[...], wo_ref[pl.ds(0, hp), :]) + _dot(sgu_ref[...], wo_ref[pl.ds(hp, GM_WIDTH), :])
        m_ref[...] = m
        x2 = x_ref[...] + gt1 * (_rms(m)[0] * gpm_ref[...])
        x2_ref[...] = x2
        h2_ref[...] = ((_rms(x2)[0] * gpf_ref[...]) * (1.0 + sc2) + sh2).astype(MXU)

    return _call(
        body, name="out_proj_fwd", grid=(s // tm,), sem=("parallel",),
        in_specs=[_rows(tm, hp), _rows(tm, GM_WIDTH), _full(wo.shape), _rows(tm, D_MODEL), _full(ada_raw.shape),
                  _full(ada_b.shape), _full(g_post_mix.shape), _full(g_pre_ffn.shape)],
        out_specs=[_rows(tm, D_MODEL)] * 3,
        out_shape=[_sds((s, D_MODEL), F32), _sds((s, D_MODEL), F32), _sds((s, D_MODEL), MXU)],
    )(o_pad, sgu, wo, x, ada_raw, ada_b, g_post_mix, g_pre_ffn)


def _conv(u, halo, cw_ref, cb_ref):
    ext = jnp.concatenate([halo, u], axis=0)
    m1, m2 = pltpu.roll(ext, 1, 0)[8:], pltpu.roll(ext, 2, 0)[8:]
    return cb_ref[0] + ((m2 * cw_ref[0, pl.ds(0, 1), :] + m1 * cw_ref[0, pl.ds(1, 1), :]) + u * cw_ref[0, pl.ds(2, 1), :])


ROW_SUB = 256


def _sub_blocks(tm):
    return [slice(r, r + ROW_SUB) for r in range(0, tm, ROW_SUB)]


def _ffn_up_fwd(h2, w_up, conv_w, conv_b, tm):
    s = h2.shape[0]
    half = N_DEV // 2

    def body(h_ref, wa_ref, wb_ref, cwa_ref, cwb_ref, cba_ref, cbb_ref,
             ua_ref, ub_ref, ya_ref, yb_ref, act_ref, halo_a, halo_b):
        i = pl.program_id(1)

        @pl.when(i == 0)
        def _():
            halo_a[...] = jnp.zeros(halo_a.shape, F32)
            halo_b[...] = jnp.zeros(halo_b.shape, F32)

        ha, hb = halo_a[...], halo_b[...]
        for rows in _sub_blocks(tm):
            h = h_ref[rows, :]
            ua = _dot_nt(h, wa_ref[0])
            ub = _dot_nt(h, wb_ref[0])
            ua_ref[0, rows, :] = ua
            ub_ref[0, rows, :] = ub
            ya = _conv(ua, ha, cwa_ref, cba_ref)
            yb = _conv(ub, hb, cwb_ref, cbb_ref)
            ya_ref[0, rows, :] = ya
            yb_ref[0, rows, :] = yb
            ha, hb = ua[ROW_SUB - 8:], ub[ROW_SUB - 8:]
            act_ref[0, rows, :] = ((ya * jax.nn.sigmoid(ya)) * yb).astype(MXU)
        halo_a[...] = ha
        halo_b[...] = hb

    def blk(shape, off):
        return pl.BlockSpec(shape, lambda j, i: (j + off, 0, 0))

    def tok(off=0):
        return pl.BlockSpec((1, tm, FF_BLK), lambda j, i: (j + off, i, 0))

    return _call(
        body, name="ffn_up_fwd", grid=(half, s // tm), sem=("parallel", "arbitrary"),
        in_specs=[pl.BlockSpec((tm, D_MODEL), lambda j, i: (i, 0)),
                  blk((1, FF_BLK, D_MODEL), 0), blk((1, FF_BLK, D_MODEL), half),
                  blk((1, 3, FF_BLK), 0), blk((1, 3, FF_BLK), half), blk((1, 1, FF_BLK), 0), blk((1, 1, FF_BLK), half)],
        out_specs=[tok()] * 5,
        out_shape=[_sds((half, s, FF_BLK), F32)] * 4 + [_sds((half, s, FF_BLK), MXU)],
        scratch=[pltpu.VMEM((8, FF_BLK), F32), pltpu.VMEM((8, FF_BLK), F32)],
    )(h2, w_up, w_up, conv_w, conv_w, conv_b, conv_b)


def _ffn_down_fwd(act, wd, x2, target, ada_raw, ada_b, g_post_ffn, tm):
    s = x2.shape[0]
    half = N_DEV // 2

    def body(act_ref, wd_ref, x2_ref, t_ref, ar_ref, ab_ref, g_ref, dout_ref, df_ref, loss_ref, dgt_ref, dg_ref):
        i = pl.program_id(0)

        @pl.when(i == 0)
        def _():
            loss_ref[...] = jnp.zeros(loss_ref.shape, F32)
            dgt_ref[...] = jnp.zeros(dgt_ref.shape, F32)
            dg_ref[...] = jnp.zeros(dg_ref.shape, F32)

        gt2 = _row(ar_ref, 5) + _row(ab_ref, 5)
        g = g_ref[...]
        for rows in _sub_blocks(tm):
            f = _dot(act_ref[0, rows, :], wd_ref[0])
            for j in range(1, half):
                f = f + _dot(act_ref[j, rows, :], wd_ref[j])
            fhat, rf = _rms(f)
            fn = fhat * g
            err = (x2_ref[rows, :] + gt2 * fn) - t_ref[rows, :]
            loss_ref[...] += 0.5 * jnp.sum(jnp.mean(err * err, axis=-1, keepdims=True))
            d_out = err * (1.0 / D_MODEL)
            dout_ref[rows, :] = d_out
            dgt_ref[...] += jnp.sum(d_out * fn, axis=0, keepdims=True)
            d_fn = d_out * gt2
            dg_ref[...] += jnp.sum(d_fn * fhat, axis=0, keepdims=True)
            df_ref[rows, :] = _rms_bwd(d_fn * g, fhat, rf).astype(MXU)

    vec = pl.BlockSpec((1, D_MODEL), lambda i: (0, 0))
    return _call(
        body, name="ffn_down_fwd", grid=(s // tm,), sem=("arbitrary",),
        in_specs=[pl.BlockSpec((half, tm, FF_BLK), lambda i: (0, i, 0)), _full(wd.shape), _rows(tm, D_MODEL),
                  _rows(tm, D_MODEL), _full(ada_raw.shape), _full(ada_b.shape), _full(g_post_ffn.shape)],
        out_specs=[_rows(tm, D_MODEL), _rows(tm, D_MODEL), pl.BlockSpec((1, 128), lambda i: (0, 0)), vec, vec],
        out_shape=[_sds((s, D_MODEL), F32), _sds((s, D_MODEL), MXU), _sds((1, 128), F32),
                   _sds((1, D_MODEL), F32), _sds((1, D_MODEL), F32)],
    )(act, wd, x2, target, ada_raw, ada_b, g_post_ffn)


def _ffn_down_bwd(d_f, wd, up_a, up_b, y_a, y_b, conv_w, act, h2, tm):
    s = d_f.shape[0]
    half = N_DEV // 2
    nt = s // tm

    def body(df_ref, wd_ref, ua_ref, ub_ref, ya_ref, yb_ref, cwa_ref, cwb_ref, act_ref, h2_ref,
             dup_ref, dcw_ref, dcb_ref, pd_ref, pu_ref, next_a, next_b, acc_d, acc_a, acc_b):
        i = pl.program_id(1)

        @pl.when(i == 0)
        def _():
            next_a[...] = jnp.zeros(next_a.shape, F32)
            next_b[...] = jnp.zeros(next_b.shape, F32)
            dcw_ref[...] = jnp.zeros(dcw_ref.shape, F32)
            dcb_ref[...] = jnp.zeros(dcb_ref.shape, F32)
            for acc in (acc_d, acc_a, acc_b):
                acc[...] = jnp.zeros(acc.shape, F32)

        def conv_bwd(d_y, u, nxt, cw_ref, part, rows):
            ext = jnp.concatenate([d_y, nxt], axis=0)
            p1 = pltpu.roll(ext, ROW_SUB + 7, 0)[:ROW_SUB]
            p2 = pltpu.roll(ext, ROW_SUB + 6, 0)[:ROW_SUB]
            d_u = (d_y * cw_ref[0, pl.ds(2, 1), :] + p1 * cw_ref[0, pl.ds(1, 1), :]) + p2 * cw_ref[0, pl.ds(0, 1), :]
            dup_ref[0, part, rows, :] = d_u.astype(MXU)
            dcb_ref[0, part] += jnp.sum(d_y, axis=0, keepdims=True)
            dcw_ref[0, part, pl.ds(0, 1), :] += jnp.sum(p2 * u, axis=0, keepdims=True)
            dcw_ref[0, part, pl.ds(1, 1), :] += jnp.sum(p1 * u, axis=0, keepdims=True)
            dcw_ref[0, part, pl.ds(2, 1), :] += jnp.sum(d_y * u, axis=0, keepdims=True)
            return d_y[:8]

        nxa, nxb = next_a[...], next_b[...]
        for rows in reversed(_sub_blocks(tm)):
            d_act = _dot_nt(df_ref[rows, :], wd_ref[0])
            ya, yb = ya_ref[0, rows, :], yb_ref[0, rows, :]
            sig = jax.nn.sigmoid(ya)
            d_ya = d_act * yb * (sig * (1.0 + ya * (1.0 - sig)))
            d_yb = d_act * (ya * sig)
            nxa = conv_bwd(d_ya, ua_ref[0, rows, :], nxa, cwa_ref, 0, rows)
            nxb = conv_bwd(d_yb, ub_ref[0, rows, :], nxb, cwb_ref, 1, rows)
        next_a[...] = nxa
        next_b[...] = nxb
        acc_d[...] += _dot_tn(act_ref[0], df_ref[...])
        acc_a[...] += _dot_tn(dup_ref[0, 0], h2_ref[...])
        acc_b[...] += _dot_tn(dup_ref[0, 1], h2_ref[...])

        @pl.when(i == nt - 1)
        def _():
            pd_ref[0] = acc_d[...].astype(MXU)
            pu_ref[0, 0] = acc_a[...].astype(MXU)
            pu_ref[0, 1] = acc_b[...].astype(MXU)

    def rev(i):
        return nt - 1 - i

    def blk(shape, off):
        return pl.BlockSpec(shape, lambda j, i: (j + off, 0, 0))

    tok = pl.BlockSpec((1, tm, FF_BLK), lambda j, i: (j, rev(i), 0))
    acc3 = pl.BlockSpec((1, 2, 3, FF_BLK), lambda j, i: (j, 0, 0, 0))
    acc1 = pl.BlockSpec((1, 2, 1, FF_BLK), lambda j, i: (j, 0, 0, 0))
    return _call(
        body, name="ffn_down_bwd", grid=(half, nt), sem=("parallel", "arbitrary"),
        in_specs=[pl.BlockSpec((tm, D_MODEL), lambda j, i: (rev(i), 0)), blk((1, FF_BLK, D_MODEL), 0),
                  tok, tok, tok, tok, blk((1, 3, FF_BLK), 0), blk((1, 3, FF_BLK), half),
                  tok, pl.BlockSpec((tm, D_MODEL), lambda j, i: (rev(i), 0))],
        out_specs=[pl.BlockSpec((1, 2, tm, FF_BLK), lambda j, i: (j, 0, rev(i), 0)), acc3, acc1,
                   pl.BlockSpec((1, FF_BLK, D_MODEL), lambda j, i: (j, 0, 0)),
                   pl.BlockSpec((1, 2, FF_BLK, D_MODEL), lambda j, i: (j, 0, 0, 0))],
        out_shape=[_sds((half, 2, s, FF_BLK), MXU), _sds((half, 2, 3, FF_BLK), F32), _sds((half, 2, 1, FF_BLK), F32),
                   _sds((half, FF_BLK, D_MODEL), MXU), _sds((half, 2, FF_BLK, D_MODEL), MXU)],
        scratch=[pltpu.VMEM((8, FF_BLK), F32), pltpu.VMEM((8, FF_BLK), F32)]
        + [pltpu.VMEM((FF_BLK, D_MODEL), F32)] * 3,
    )(d_f, wd, up_a, up_b, y_a, y_b, conv_w, conv_w, act, h2)


def _ffn_up_bwd(d_up, w_up, x2, m, d_out, ada_raw, ada_b, g_pre_ffn, g_post_mix, tm):
    s = x2.shape[0]
    half = N_DEV // 2

    def body(dup_ref, w_ref, x2_ref, m_ref, dout_ref, ar_ref, ab_ref, gpf_ref, gpm_ref,
             dx_ref, dm_ref, dsh_ref, dsc_ref, dgpf_ref, dgt1_ref, dgpm_ref):
        i = pl.program_id(0)

        @pl.when(i == 0)
        def _():
            for r in (dsh_ref, dsc_ref, dgpf_ref, dgt1_ref, dgpm_ref):
                r[...] = jnp.zeros(r.shape, F32)

        gt1 = _row(ar_ref, 2) + _row(ab_ref, 2)
        sc2 = _row(ar_ref, 4) + _row(ab_ref, 4)
        gpf, gpm = gpf_ref[...], gpm_ref[...]
        d_h2 = _dot(dup_ref[0, 0], w_ref[0])
        for j in range(1, half):
            d_h2 = d_h2 + _dot(dup_ref[j, 0], w_ref[j])
        for j in range(half):
            d_h2 = d_h2 + _dot(dup_ref[j, 1], w_ref[half + j])
        x2n, r2 = _rms(x2_ref[...])
        dsh_ref[...] += jnp.sum(d_h2, axis=0, keepdims=True)
        dsc_ref[...] += jnp.sum(d_h2 * (x2n * gpf), axis=0, keepdims=True)
        d_mod = d_h2 * (1.0 + sc2)
        dgpf_ref[...] += jnp.sum(d_mod * x2n, axis=0, keepdims=True)
        d_x2 = dout_ref[...] + _rms_bwd(d_mod * gpf, x2n, r2)
        dx_ref[...] = d_x2
        mhat, rm = _rms(m_ref[...])
        dgt1_ref[...] += jnp.sum(d_x2 * (mhat * gpm), axis=0, keepdims=True)
        d_mn = d_x2 * gt1
        dgpm_ref[...] += jnp.sum(d_mn * mhat, axis=0, keepdims=True)
        dm_ref[...] = _rms_bwd(d_mn * gpm, mhat, rm).astype(MXU)

    vec = pl.BlockSpec((1, D_MODEL), lambda i: (0, 0))
    tok = pl.BlockSpec((half, 2, tm, FF_BLK), lambda i: (0, 0, i, 0))
    return _call(
        body, name="ffn_up_bwd", grid=(s // tm,), sem=("arbitrary",),
        in_specs=[tok, _full(w_up.shape), _rows(tm, D_MODEL), _rows(tm, D_MODEL), _rows(tm, D_MODEL),
                  _full(ada_raw.shape), _full(ada_b.shape), _full(g_pre_ffn.shape), _full(g_post_mix.shape)],
        out_specs=[_rows(tm, D_MODEL), _rows(tm, D_MODEL), vec, vec, vec, vec, vec],
        out_shape=[_sds((s, D_MODEL), F32), _sds((s, D_MODEL), MXU)] + [_sds((1, D_MODEL), F32)] * 5,
    )(d_up, w_up, x2, m, d_out, ada_raw, ada_b, g_pre_ffn, g_post_mix)


def _out_proj_bwd(d_m, wo, o_pad, tm):
    s = d_m.shape[0]
    hp = HEADS * HEAD_PAD

    def body(dm_ref, wo_ref, o_ref, do_ref, dsgu_ref, delta_ref):
        d_cat = _dot_nt(dm_ref[...], wo_ref[...])
        d_o = d_cat[:, :hp]
        do_ref[...] = d_o.astype(MXU)
        dsgu_ref[...] = d_cat[:, hp:]
        prod = d_o * o_ref[...].astype(F32)
        for h in range(HEADS):
            delta_ref[h] = jnp.sum(prod[:, h * HEAD_PAD:(h + 1) * HEAD_PAD], axis=-1, keepdims=True)

    return _call(
        body, name="out_proj_bwd", grid=(s // tm,), sem=("parallel",),
        in_specs=[_rows(tm, D_MODEL), _full(wo.shape), _rows(tm, hp)],
        out_specs=[_rows(tm, hp), _rows(tm, GM_WIDTH), pl.BlockSpec((HEADS, tm, 1), lambda i: (0, i, 0))],
        out_shape=[_sds((s, hp), MXU), _sds((s, GM_WIDTH), F32), _sds((HEADS, s, 1), F32)],
    )(d_m, wo, o_pad)


def _attn_bwd(qp, kp, vp, d_o, lse, delta, tq, scattered, gathered):
    s = qp.shape[0]
    nq = s // tq
    hb = ATTN_HEADS_PER_STEP
    groups = HEADS // hb
    width = hb * HEAD_PAD
    half = tq // 2
    ns, ng = len(scattered), len(gathered)
    nc = ns + ng
    slots = [slot for _, slot in scattered]

    def body(q_ref, k_ref, v_ref, do_ref, lse_ref, dl_ref, *rest):
        c_in, (dq_ref, dk_ref, dv_ref), c_out = rest[:nc], rest[nc:nc + 3], rest[nc + 3:2 * nc + 3]
        dk_sc, dv_sc = rest[2 * nc + 3:2 * nc + 5]
        sems = rest[2 * nc + 5:]
        s_start, s_finish = _scatter_steps(c_in[:ns], c_out[:ns], sems[:3], slots)
        g_start, g_forward, g_finish = _gather_steps(c_in[ns:], c_out[ns:], sems[3:])
        g, j = pl.program_id(0), pl.program_id(1)

        @pl.when((g == 0) & (j == 0))
        def _():
            s_start()
            g_start()

        pl.when((g == groups - 1) & (j == 0))(g_forward)

        @pl.when(j == 0)
        def _():
            dq_ref[...] = jnp.zeros(dq_ref.shape, F32)

        dk_sc[...] = jnp.zeros(dk_sc.shape, F32)
        dv_sc[...] = jnp.zeros(dv_sc.shape, F32)

        def tile(q0, n_q=tq, n_k=tq, q_off=None):
            rows, keys = pl.ds(q0, n_q), slice(0, n_k)
            for hh in range(hb):
                lanes = slice(hh * HEAD_PAD, (hh + 1) * HEAD_PAD)
                q, do, k = q_ref[rows, lanes], do_ref[rows, lanes], k_ref[keys, lanes]
                sc = _dot_nt(q, k)
                if q_off is not None:
                    sc = jnp.where(_chunk_mask(n_q, n_k, q_off), sc, NEG_BIG)
                p = jnp.exp2(sc * SCALE_LOG2E - lse_ref[hh, rows, :])
                dv_sc[hh, keys] += _dot_tn(p.astype(MXU), do)
                dp = _dot_nt(do, v_ref[keys, lanes])
                ds = (p * (dp - dl_ref[hh, rows, :])).astype(MXU)
                dk_sc[hh, keys] += _dot_tn(ds, q)
                dq_ref[rows, lanes] += _dot(ds, k) * ATTN_SCALE

        def query_tile(i):
            return pl.multiple_of(i * tq, tq)

        def off_diagonal_pair(p, carry):
            tile(query_tile(j + 1 + 2 * p))
            tile(query_tile(j + 2 + 2 * p))
            return carry

        below = nq - 1 - j
        tile(query_tile(j), half, half, 0)
        tile(pl.multiple_of(j * tq + half, half), half, tq, half)
        lax.fori_loop(0, below // 2, off_diagonal_pair, 0)

        @pl.when(below % 2 == 1)
        def _():
            tile(query_tile(nq - 1))
        for hh in range(hb):
            lanes = slice(hh * HEAD_PAD, (hh + 1) * HEAD_PAD)
            dk_ref[:, lanes] = dk_sc[hh] * ATTN_SCALE
            dv_ref[:, lanes] = dv_sc[hh]
        @pl.when((g == groups - 1) & (j == nq - 1))
        def _():
            g_finish()
            s_finish()

    seq_spec = pl.BlockSpec((s, width), lambda g, j: (0, g))
    kv_spec = pl.BlockSpec((tq, width), lambda g, j: (j, g))
    col_spec = pl.BlockSpec((hb, s, 1), lambda g, j: (g, 0, 0))
    any_spec = pl.BlockSpec(memory_space=pl.ANY)
    outs = _call(
        body, name="attn_bwd", grid=(groups, nq), sem=("arbitrary", "arbitrary"),
        in_specs=[seq_spec, kv_spec, kv_spec, seq_spec, col_spec, col_spec] + [any_spec] * nc,
        out_specs=[seq_spec, kv_spec, kv_spec] + [any_spec] * nc,
        out_shape=[_sds(qp.shape, F32), _sds(qp.shape, F32), _sds(qp.shape, F32)]
        + [_scatter_out_shape(a, slot) for a, slot in scattered]
        + [_sds((N_DEV,) + a.shape, a.dtype) for a in gathered],
        scratch=[pltpu.VMEM((hb, tq, HEAD_PAD), F32), pltpu.VMEM((hb, tq, HEAD_PAD), F32)]
        + _comm_sems(ns) + _comm_sems(ng),
    )(qp, kp, vp, d_o, lse, delta, *[a for a, _ in scattered], *gathered)
    return outs[0], outs[1], outs[2], outs[3:3 + ns], outs[3 + ns:]


def _gmlp_bwd(z, d_sgu, ln_g, ln_b, w_sp, bias_exp, tm):
    s = z.shape[0]
    nblk = tm // GM_CHUNK

    def body(zu_ref, zv_ref, dsgu_ref, lg_ref, lb_ref, w_ref, be_ref,
             dguv_ref, dws_ref, dbs_ref, dlg_ref, dlb_ref, dbe_sc, dvln_sc, dlg_sc, dlb_sc):
        i = pl.program_id(0)

        @pl.when(i == 0)
        def _():
            for r in (dws_ref, dlg_sc, dlb_sc, dbe_sc):
                r[...] = jnp.zeros(r.shape, F32)

        seg = _seg_matrix()
        mask = _spatial_mask()
        wm = [(w_ref[h] * mask).astype(MXU) for h in range(HEADS)]
        zu, zv = zu_ref[...], zv_ref[...]
        gu = _gelu(zu)
        _, vhat, rstd = _gm_norm(zv, seg)
        lg = lg_ref[...]
        vln = (vhat * lg + lb_ref[...]).astype(MXU)
        d_sgu = dsgu_ref[...]
        for n in range(nblk):
            rows = slice(n * GM_CHUNK, (n + 1) * GM_CHUNK)
            vb = vln[rows]
            mixed = _gm_mix(wm, vb, GM_CHUNK) + be_ref[...]
            d_mixed = d_sgu[rows] * gu[rows]
            dguv_ref[rows, pl.ds(0, GM_WIDTH)] = ((d_sgu[rows] * mixed) * _gelu_grad(zu[rows])).astype(MXU)
            dbe_sc[...] += d_mixed
            dmb = d_mixed.astype(MXU)
            d_vln = jnp.zeros((GM_CHUNK, GM_WIDTH), F32)
            for h in range(HEADS):
                hm = _head_lane_mask(h, GM_CHUNK)
                dws_ref[h] += _dot_nt(jnp.where(hm, dmb, jnp.zeros_like(dmb)), vb)
                d_vln = d_vln + jnp.where(hm, _dot_tn(wm[h], dmb), 0.0)
            dvln_sc[rows, :] = d_vln
        d_vln = dvln_sc[...]
        dlg_sc[...] += jnp.sum(d_vln * vhat, axis=0, keepdims=True)
        dlb_sc[...] += jnp.sum(d_vln, axis=0, keepdims=True)
        d_vhat = d_vln * lg
        d_gv = rstd * ((d_vhat - _split_dot(d_vhat, seg)) - vhat * _split_dot(d_vhat * vhat, seg))
        dguv_ref[:, pl.ds(GM_WIDTH, GM_WIDTH)] = (d_gv * _gelu_grad(zv)).astype(MXU)

        @pl.when(i == pl.num_programs(0) - 1)
        def _():
            for h in range(HEADS):
                dws_ref[h] = dws_ref[h] * mask
            hrow = lax.broadcasted_iota(jnp.int32, (HEADS, GM_WIDTH), 0)
            hlane = lax.broadcasted_iota(jnp.int32, (HEADS, GM_WIDTH), 1) >> 6
            ind = jnp.where(hrow == hlane, 1.0, 0.0).astype(MXU)
            acc = dbe_sc[...]
            hi = acc.astype(MXU)
            lo = (acc - hi.astype(F32)).astype(MXU)
            dbs_ref[...] = _dot_nt(ind, hi) + _dot_nt(ind, lo)
            pick = (lax.broadcasted_iota(jnp.int32, (GM_WIDTH, GM_DIM), 0) & (GM_DIM - 1)
                    == lax.broadcasted_iota(jnp.int32, (GM_WIDTH, GM_DIM), 1))
            pick = jnp.where(pick, 1.0, 0.0).astype(MXU)
            for src, dst in ((dlg_sc, dlg_ref), (dlb_sc, dlb_ref)):
                spread = jnp.where(hrow == hlane, jnp.broadcast_to(src[...], (HEADS, GM_WIDTH)), 0.0)
                dst[...] = _split_dot3(spread, pick)

    return _call(
        body, name="gmlp_bwd", grid=(s // tm,), sem=("arbitrary",),
        in_specs=[_rows(tm, GM_WIDTH, 1), _rows(tm, GM_WIDTH, 2), _rows(tm, GM_WIDTH), _full(ln_g.shape),
                  _full(ln_b.shape), _full(w_sp.shape), _full(bias_exp.shape)],
        out_specs=[_rows(tm, 2 * GM_WIDTH), _full(w_sp.shape), _full((HEADS, GM_CHUNK)), _full((HEADS, GM_DIM)),
                   _full((HEADS, GM_DIM))],
        out_shape=[_sds((s, 2 * GM_WIDTH), MXU), _sds(w_sp.shape, F32), _sds((HEADS, GM_CHUNK), F32),
                   _sds((HEADS, GM_DIM), F32), _sds((HEADS, GM_DIM), F32)],
        scratch=[pltpu.VMEM((GM_CHUNK, GM_WIDTH), F32), pltpu.VMEM((tm, GM_WIDTH), F32),
                 pltpu.VMEM((1, GM_WIDTH), F32), pltpu.VMEM((1, GM_WIDTH), F32)],
    )(z, z, d_sgu, ln_g, ln_b, w_sp, bias_exp)


def _mix_in_bwd(dq, dk, dv, z, d_guv, x, d_x_part, ada_raw, ada_b, g_pre, g_q, g_kv, w1t, wqt, wkv,
                cos_t, sin_t, tm):
    s = x.shape[0]
    hp = HEADS * HEAD_PAD
    za = Q_LORA + KV_LORA + HEAD_PAD

    def body(dq_ref, dk_ref, dv_ref, z_ref, dguv_ref, x_ref, dxp_ref, ar_ref, ab_ref, g_ref, gq_ref, gkv_ref,
             w1_ref, wq_ref, wkv_ref, cos_ref, sin_ref,
             gx_ref, dza_ref, dqp_ref, dkvp_ref, dsh_ref, dsc_ref, dg_ref, dgq_ref, dgkv_ref):
        i = pl.program_id(0)

        @pl.when(i == 0)
        def _():
            for r in (dsh_ref, dsc_ref, dg_ref, dgq_ref, dgkv_ref):
                r[...] = jnp.zeros(r.shape, F32)

        cos, sin = cos_ref[...], sin_ref[...]
        d_krot = jnp.zeros((tm, HEAD_PAD), F32)
        for h in range(HEADS):
            blk = slice(h * HEAD_PAD, (h + 1) * HEAD_PAD)
            dqp_ref[:, blk] = _rope_transposed(dq_ref[:, blk], cos, sin).astype(MXU)
            dk_h = dk_ref[:, blk]
            d_krot = d_krot + dk_h
            dkvp_ref[:, blk] = dk_h.astype(MXU)
        dkvp_ref[:, pl.ds(hp, hp)] = dv_ref[...].astype(MXU)
        lane = lax.broadcasted_iota(jnp.int32, (tm, HEAD_PAD), 1)
        d_kr = jnp.where((lane >= NOPE) & (lane < NOPE + ROPE), _rope_transposed(d_krot, cos, sin), 0.0)
        d_cqn = _dot(dqp_ref[...], wq_ref[...])
        d_ckvn = _dot_nt(dkvp_ref[...], wkv_ref[...])
        zt = z_ref[...]
        gq, gkv = gq_ref[...], gkv_ref[...]
        cq_hat, rq = _rms(zt[:, :Q_LORA])
        ckv_hat, rkv = _rms(zt[:, Q_LORA:Q_LORA + KV_LORA])
        dgq_ref[...] += jnp.sum(d_cqn * cq_hat, axis=0, keepdims=True)
        dgkv_ref[...] += jnp.sum(d_ckvn * ckv_hat, axis=0, keepdims=True)
        d_cq = _rms_bwd(d_cqn * gq, cq_hat, rq)
        d_ckv = _rms_bwd(d_ckvn * gkv, ckv_hat, rkv)
        d_za = jnp.concatenate([d_cq, d_ckv, d_kr], axis=1).astype(MXU)
        dza_ref[...] = d_za
        d_h1 = _dot(d_za, w1_ref[pl.ds(0, za), :]) + _dot(dguv_ref[...], w1_ref[pl.ds(za, 2 * GM_WIDTH), :])
        sc1 = _row(ar_ref, 1) + _row(ab_ref, 1)
        g = g_ref[...]
        xn, r1 = _rms(x_ref[...])
        dsh_ref[...] += jnp.sum(d_h1, axis=0, keepdims=True)
        dsc_ref[...] += jnp.sum(d_h1 * (xn * g), axis=0, keepdims=True)
        d_mod = d_h1 * (1.0 + sc1)
        dg_ref[...] += jnp.sum(d_mod * xn, axis=0, keepdims=True)
        gx_ref[...] = dxp_ref[...] + _rms_bwd(d_mod * g, xn, r1)

    vec = pl.BlockSpec((1, D_MODEL), lambda i: (0, 0))
    return _call(
        body, name="mix_in_bwd", grid=(s // tm,), sem=("arbitrary",),
        in_specs=[_rows(tm, hp), _rows(tm, hp), _rows(tm, hp), _rows(tm, za), _rows(tm, 2 * GM_WIDTH),
                  _rows(tm, D_MODEL), _rows(tm, D_MODEL), _full(ada_raw.shape), _full(ada_b.shape), _full(g_pre.shape),
                  _full(g_q.shape), _full(g_kv.shape), _full(w1t.shape), _full(wqt.shape),
                  _full(wkv.shape), _rows(tm, HEAD_PAD), _rows(tm, HEAD_PAD)],
        out_specs=[_rows(tm, D_MODEL), _rows(tm, za), _rows(tm, hp), _rows(tm, 2 * hp), vec, vec, vec,
                   _full(g_q.shape), _full(g_kv.shape)],
        out_shape=[_sds((s, D_MODEL), F32), _sds((s, za), MXU), _sds((s, hp), MXU), _sds((s, 2 * hp), MXU),
                   _sds((1, D_MODEL), F32), _sds((1, D_MODEL), F32), _sds((1, D_MODEL), F32),
                   _sds(g_q.shape, F32), _sds(g_kv.shape, F32)],
    )(dq, dk, dv, z, d_guv, x, d_x_part, ada_raw, ada_b, g_pre, g_q, g_kv, w1t, wqt, wkv, cos_t, sin_t)


def _tn_matmul(a, b, name, ts):
    ga, s, m = a.shape
    gb, _, n = b.shape
    g = max(ga, gb)
    tn = n if n <= 1024 else 1024
    steps = s // ts

    def body(a_ref, b_ref, o_ref, acc):
        k = pl.program_id(2)

        @pl.when(k == 0)
        def _():
            acc[...] = jnp.zeros(acc.shape, F32)

        acc[...] += _dot_tn(a_ref[0], b_ref[0])

        @pl.when(k == steps - 1)
        def _():
            o_ref[0] = acc[...].astype(MXU)

    return _call(
        body, name=name, grid=(g, n // tn, steps), sem=("parallel", "parallel", "arbitrary"),
        in_specs=[pl.BlockSpec((1, ts, m), lambda gi, ni, k: (gi if ga > 1 else 0, k, 0)),
                  pl.BlockSpec((1, ts, tn), lambda gi, ni, k: (gi if gb > 1 else 0, k, ni))],
        out_specs=pl.BlockSpec((1, m, tn), lambda gi, ni, k: (gi, 0, ni)),
        out_shape=_sds((g, m, n), MXU),
        scratch=[pltpu.VMEM((m, tn), F32)],
    )(a, b)


def _adamw(w, g, m, v):
    m2 = ADAM_B1 * m + (1.0 - ADAM_B1) * g
    v2 = ADAM_B2 * v + (1.0 - ADAM_B2) * (g * g)
    m_hat = m2 / (1.0 - ADAM_B1 ** ADAM_STEP)
    v_hat = v2 / (1.0 - ADAM_B2 ** ADAM_STEP)
    delta = -ADAM_LR * (m_hat / (jnp.sqrt(v_hat) + ADAM_EPS) + ADAM_WD * w)
    return delta, m2, v2


def _adam_reduce(recv, w, m, v, name):
    r, c = w.shape
    tr = r if r <= 512 else max(t for t in range(16, 513, 16) if r % t == 0)

    def body(p_ref, w_ref, m_ref, v_ref, g_ref, d_ref, mo_ref, vo_ref):
        g = p_ref[0].astype(F32)
        for j in range(1, N_DEV):
            g = g + p_ref[j].astype(F32)
        g_ref[...] = g
        d_ref[...], mo_ref[...], vo_ref[...] = _adamw(w_ref[...], g, m_ref[...], v_ref[...])

    blk = pl.BlockSpec((tr, c), lambda i: (i, 0))
    return _call(
        body, name=name, grid=(r // tr,), sem=("parallel",),
        in_specs=[pl.BlockSpec((N_DEV, tr, c), lambda i: (0, i, 0)), blk, blk, blk],
        out_specs=[blk] * 4, out_shape=[_sds((r, c), F32)] * 4,
    )(recv, w, m, v)


def _adam_w_ada(c_act_t, d_ada_cols, w, m, v):
    r, c = w.shape
    tr = 256

    def body(ct_ref, da_ref, w_ref, m_ref, v_ref, g_ref, d_ref, mo_ref, vo_ref):
        g = ct_ref[:, pl.ds(0, 1)] * da_ref[pl.ds(0, 1), :]
        for b in range(1, N_DEV):
            g = g + ct_ref[:, pl.ds(b, 1)] * da_ref[pl.ds(b, 1), :]
        g_ref[...] = g
        d_ref[...], mo_ref[...], vo_ref[...] = _adamw(w_ref[...], g, m_ref[...], v_ref[...])

    blk = pl.BlockSpec((tr, c), lambda i: (i, 0))
    return _call(
        body, name="adam_w_ada", grid=(r // tr,), sem=("parallel",),
        in_specs=[pl.BlockSpec((tr, N_DEV), lambda i: (i, 0)), _full(d_ada_cols.shape), blk, blk, blk],
        out_specs=[blk] * 4, out_shape=[_sds((r, c), F32)] * 4,
    )(c_act_t, d_ada_cols, w, m, v)


VEC_ROWS = D_MODEL // 128
PK_ADA = 0
PK_GAIN = PK_ADA + 6 * VEC_ROWS
PK_GQ = PK_GAIN + 4 * VEC_ROWS
PK_GKV = PK_GQ + Q_LORA // 128
PK_LOSS = PK_GKV + KV_LORA // 128
PK_LNG = 88
PK_LNB = PK_LNG + HEADS
PK_BS = PK_LNB + HEADS
PK_CB = PK_BS + HEADS
CB_ROWS = 6
PK_WS = PK_CB + N_DEV * CB_ROWS
PK_ROWS = PK_WS + HEADS * GM_CHUNK
assert PK_LOSS < PK_LNG and PK_ROWS % 8 == 0
LATE_GAIN = 2 * VEC_ROWS
LATE_GQ = 3 * VEC_ROWS
LATE_GKV = LATE_GQ + Q_LORA // 128
LATE_ROWS = 32


def _cb_chunks():
    return [(k, k * 128, min(128, FF_BLK - k * 128)) for k in range(CB_ROWS)]


def _put_rows(out_ref, row0, ref, width):
    for k in range(width // 128):
        out_ref[pl.ds(row0 + k, 1), :] = ref[:, pl.ds(k * 128, 128)]


def _pack_small(ada_rows, gains, loss_part, d_ln_g, d_ln_b, d_bs, d_cb, d_ws):
    half = N_DEV // 2

    def body(*refs):
        vec_refs = refs[:7]
        loss_ref, lng_ref, lnb_ref, bs_ref, cb_ref, ws_ref, out_ref = refs[7:]
        out_ref[pl.ds(0, PK_WS), :] = jnp.zeros((PK_WS, 128), F32)
        for n, ref in enumerate(vec_refs[:4]):
            _put_rows(out_ref, PK_ADA + (2 + n) * VEC_ROWS, ref, D_MODEL)
        for n, ref in enumerate(vec_refs[4:]):
            _put_rows(out_ref, PK_GAIN + (1 + n) * VEC_ROWS, ref, D_MODEL)
        _put_rows(out_ref, PK_LOSS, loss_ref, 128)
        out_ref[pl.ds(PK_LNG, HEADS), pl.ds(0, GM_DIM)] = lng_ref[...]
        out_ref[pl.ds(PK_LNB, HEADS), pl.ds(0, GM_DIM)] = lnb_ref[...]
        out_ref[pl.ds(PK_BS, HEADS), :] = bs_ref[...]
        for j in range(N_DEV):
            for k, lane, width in _cb_chunks():
                out_ref[pl.ds(PK_CB + j * CB_ROWS + k, 1), pl.ds(0, width)] = cb_ref[j % half, j // half, :, pl.ds(lane, width)]
        for h in range(HEADS):
            out_ref[pl.ds(PK_WS + h * GM_CHUNK, GM_CHUNK), :] = ws_ref[h]

    ins = list(ada_rows) + list(gains) + [loss_part, d_ln_g, d_ln_b, d_bs, d_cb, d_ws]
    return _call(body, name="pack_small", grid=(1,), in_specs=[_full(a.shape) for a in ins],
                 out_specs=_full((PK_ROWS, 128)), out_shape=_sds((PK_ROWS, 128), F32))(*ins)


def _pack_late(d_sh1, d_sc1, d_g_pre_mix, d_g_q, d_g_kv):
    def body(sh_ref, sc_ref, g_ref, gq_ref, gkv_ref, out_ref):
        out_ref[...] = jnp.zeros((LATE_ROWS, 128), F32)
        _put_rows(out_ref, 0, sh_ref, D_MODEL)
        _put_rows(out_ref, VEC_ROWS, sc_ref, D_MODEL)
        _put_rows(out_ref, LATE_GAIN, g_ref, D_MODEL)
        _put_rows(out_ref, LATE_GQ, gq_ref, Q_LORA)
        _put_rows(out_ref, LATE_GKV, gkv_ref, KV_LORA)

    ins = [d_sh1, d_sc1, d_g_pre_mix, d_g_q, d_g_kv]
    return _call(body, name="pack_late", grid=(1,), in_specs=[_full(a.shape) for a in ins],
                 out_specs=_full((LATE_ROWS, 128)), out_shape=_sds((LATE_ROWS, 128), F32))(*ins)


def _adam_small(gathered, late, params):
    n_par = len(params)

    def body(p_ref, late_ref, *refs):
        ins = [refs[3 * n:3 * n + 3] for n in range(n_par)]
        outs = [refs[3 * n_par + 4 * n:3 * n_par + 4 * n + 4] for n in range(n_par)]
        loss_ref, dada_ref = refs[7 * n_par:]

        def total(rows, lanes=slice(None), src=p_ref):
            g = src[0, rows, lanes]
            for j in range(1, N_DEV):
                g = g + src[j, rows, lanes]
            return g

        def apply(n, g, idx):
            w_ref, m_ref, v_ref = ins[n]
            d, m2, v2 = _adamw(w_ref[idx], g, m_ref[idx], v_ref[idx])
            for ref, val in zip(outs[n], (g, d, m2, v2)):
                ref[idx] = val

        def vector(n, src, row0, width, lane0=0):
            for k in range(width // 128):
                apply(n, total(pl.ds(row0 + k, 1), src=src), (slice(None), pl.ds(lane0 + k * 128, 128)))

        vector(0, late_ref, 0, 2 * D_MODEL)
        vector(0, p_ref, PK_ADA + 2 * VEC_ROWS, 4 * D_MODEL, lane0=2 * D_MODEL)
        vector(1, late_ref, LATE_GAIN, D_MODEL)
        for n in range(1, 4):
            vector(1 + n, p_ref, PK_GAIN + n * VEC_ROWS, D_MODEL)
        vector(5, late_ref, LATE_GQ, Q_LORA)
        vector(6, late_ref, LATE_GKV, KV_LORA)
        apply(7, total(pl.ds(PK_LNG, HEADS), pl.ds(0, GM_DIM)), (0,))
        apply(8, total(pl.ds(PK_LNB, HEADS), pl.ds(0, GM_DIM)), (0,))
        for h in range(HEADS):
            apply(9, total(pl.ds(PK_WS + h * GM_CHUNK, GM_CHUNK)), (0, h))
        apply(10, total(pl.ds(PK_BS, HEADS)), (0,))
        for j in range(N_DEV):
            for k, lane, width in _cb_chunks():
                apply(11, total(pl.ds(PK_CB + j * CB_ROWS + k, 1), pl.ds(0, width)), (pl.ds(j, 1), pl.ds(lane, width)))
        loss_ref[...] = total(pl.ds(PK_LOSS, 1))
        dada_ref[:, pl.ds(0, 2 * VEC_ROWS), :] = late_ref[:, pl.ds(0, 2 * VEC_ROWS), :]
        dada_ref[:, pl.ds(2 * VEC_ROWS, 4 * VEC_ROWS), :] = p_ref[:, pl.ds(PK_ADA + 2 * VEC_ROWS, 4 * VEC_ROWS), :]

    flat = [a for triple in params for a in triple]
    out_shape = [_sds(w.shape, F32) for w, _, _ in params for _ in range(4)]
    out_shape += [_sds((1, 128), F32), _sds((N_DEV, 6 * VEC_ROWS, 128), F32)]
    outs = _call(body, name="adam_small", grid=(1,),
                 in_specs=[_full(gathered.shape), _full(late.shape)] + [_full(a.shape) for a in flat],
                 out_specs=[_full(o.shape) for o in out_shape], out_shape=out_shape)(gathered, late, *flat)
    return [tuple(outs[4 * n:4 * n + 4]) for n in range(n_par)], outs[-2], outs[-1]


def _rope_tables(s):
    pos = jnp.arange(s, dtype=F32)
    inv = ROPE_THETA ** (-jnp.arange(0, ROPE, 2, dtype=F32) / ROPE)
    lane_inv = jnp.concatenate([jnp.zeros((NOPE,), F32), inv, inv, jnp.zeros((HEAD_PAD - NOPE - ROPE,), F32)])
    ang = pos[:, None] * lane_inv[None, :]
    return jnp.cos(ang), jnp.sin(ang)


def kernel(x, c, w_ada, b_ada, g_pre_mix, g_post_mix, w_in, g_q, w_uq, g_kv, w_ukv, gm_ln_g, gm_ln_b, w_spatial, b_spatial, w_out, g_pre_ffn, g_post_ffn, w_up, conv_w, conv_b, w_down, loss_target, m_w_ada, m_b_ada, m_g_pre_mix, m_g_post_mix, m_w_in, m_g_q, m_w_uq, m_g_kv, m_w_ukv, m_gm_ln_g, m_gm_ln_b, m_w_spatial, m_b_spatial, m_w_out, m_g_pre_ffn, m_g_post_ffn, m_w_up, m_conv_w, m_conv_b, m_w_down, v_w_ada, v_b_ada, v_g_pre_mix, v_g_post_mix, v_w_in, v_g_q, v_w_uq, v_g_kv, v_w_ukv, v_gm_ln_g, v_gm_ln_b, v_w_spatial, v_b_spatial, v_w_out, v_g_pre_ffn, v_g_post_ffn, v_w_up, v_conv_w, v_conv_b, v_w_down):
    s = x.shape[1]
    tm = min(256, s)
    tf = min(2 * ROW_SUB, s)
    tq = min(512, s)
    ts = min(2048, s)
    hp = HEADS * HEAD_PAD
    half = N_DEV // 2
    my_slot = 4 * lax.axis_index("x") + 2 * lax.axis_index("y") + lax.axis_index("c")
    x2d, target = x[0], loss_target[0]

    def t_(a):
        return jnp.swapaxes(a[0], 0, 1)

    w_in_t, m_in_t, v_in_t = t_(w_in), t_(m_w_in), t_(v_w_in)
    w_uq_t, m_uq_t, v_uq_t = t_(w_uq), t_(m_w_uq), t_(v_w_uq)
    w_up_t, m_up_t, v_up_t = t_(w_up), t_(m_w_up), t_(v_w_up)
    (g_c, g_in_t, g_uq_t, g_ukv, g_cw), _ = _exchange(
        [c, w_in_t.astype(MXU), w_uq_t.astype(MXU), w_ukv[0].astype(MXU), conv_w[0]], [], "gather_mixer_weights")

    w_in_f = g_in_t.reshape(-1, D_MODEL)
    o1, o2, o3 = Q_LORA, Q_LORA + KV_LORA, Q_LORA + KV_LORA + ROPE
    w1t = jnp.concatenate([w_in_f[:o2], jnp.zeros((NOPE, D_MODEL), MXU), w_in_f[o2:o3],
                           jnp.zeros((HEAD_PAD - NOPE - ROPE, D_MODEL), MXU), w_in_f[o3:]], axis=0)
    wqt = jnp.pad(g_uq_t, ((0, 0), (0, HEAD_PAD - NOPE - ROPE), (0, 0))).reshape(hp, Q_LORA)
    w_ukv_f = jnp.transpose(g_ukv, (1, 0, 2)).reshape(KV_LORA, HEADS, 2 * NOPE)
    pad_head = ((0, 0), (0, 0), (0, HEAD_PAD - NOPE))
    wkv = jnp.concatenate([jnp.pad(w_ukv_f[:, :, :NOPE], pad_head).reshape(KV_LORA, hp),
                           jnp.pad(w_ukv_f[:, :, NOPE:], pad_head).reshape(KV_LORA, hp)], axis=1)
    cb8 = conv_b.reshape(N_DEV, 1, FF_BLK)
    bias_exp = jnp.repeat(b_spatial[0].T, GM_DIM, axis=1)
    ln_g, ln_b = gm_ln_g.reshape(1, GM_WIDTH), gm_ln_b.reshape(1, GM_WIDTH)
    w_sp = w_spatial[0]
    cos_t, sin_t = _rope_tables(s)

    ada_part, c_act = _ada_fwd(g_c.reshape(N_DEV, D_MODEL), w_ada[0])
    _, (ada_recv,) = _exchange([], [(ada_part.reshape(N_DEV, 1, -1), _plain_slot)], "ada_rows")
    ada_raw = ada_recv.reshape(6, D_MODEL)
    ada_b = b_ada.reshape(6, D_MODEL)

    h1, z, qp, kp, vp, cqn, ckvn = _mix_in_fwd(x2d, ada_raw, ada_b, g_pre_mix, w1t, g_q, g_kv, wqt, wkv, cos_t, sin_t, tm)
    sgu = _gmlp_fwd(z, ln_g, ln_b, w_sp, bias_exp, tm)
    o_pad, lse, (g_out, g_up, g_down) = _attn_fwd(
        qp, kp, vp, tq, [w_out[0].astype(MXU), w_up_t.astype(MXU), w_down[0].astype(MXU)])
    w_out_f = g_out.reshape(2 * GM_WIDTH, D_MODEL)
    wo_attn = jnp.pad(w_out_f[:GM_WIDTH].reshape(HEADS, NOPE, D_MODEL), ((0, 0), (0, HEAD_PAD - NOPE), (0, 0)))
    wo = jnp.concatenate([wo_attn.reshape(hp, D_MODEL), w_out_f[GM_WIDTH:]], axis=0)
    wd = g_down.reshape(half, FF_BLK, D_MODEL)
    m_mix, x2, h2 = _out_proj_fwd(o_pad, sgu, wo, x2d, ada_raw, ada_b, g_post_mix, g_pre_ffn, tm)
    up_a, up_b, y_a, y_b, act = _ffn_up_fwd(h2, g_up, g_cw, cb8, tf)
    d_out, d_f, loss_part, d_gt2, d_g_post_ffn = _ffn_down_fwd(act, wd, x2, target, ada_raw, ada_b, g_post_ffn, tf)

    d_up, d_cw, d_cb, p_down, p_up = _ffn_down_bwd(d_f, wd, up_a, up_b, y_a, y_b, g_cw, act, h2, tf)
    p_down = p_down.reshape(N_DEV, -1, D_MODEL)
    d_x2, d_m, d_sh2, d_sc2, d_g_pre_ffn, d_gt1, d_g_post_mix = _ffn_up_bwd(
        d_up, g_up, x2, m_mix, d_out, ada_raw, ada_b, g_pre_ffn, g_post_mix, tm)
    d_m3 = d_m[None]
    dwo_attn = _tn_matmul(o_pad[None], d_m3, "dw_out_attn", ts)[0].reshape(HEADS, HEAD_PAD, D_MODEL)[:, :NOPE]
    dwo_sgu = _tn_matmul(sgu[None], d_m3, "dw_out_sgu", ts)[0]
    p_out = jnp.concatenate([dwo_attn.reshape(GM_WIDTH, D_MODEL), dwo_sgu], axis=0).reshape(N_DEV, -1, D_MODEL)
    d_o, d_sgu, delta = _out_proj_bwd(d_m, wo, o_pad, tm)
    d_guv, d_ws, d_bs, d_ln_g, d_ln_b = _gmlp_bwd(z, d_sgu, ln_g, ln_b, w_sp, bias_exp, tm)
    packed = _pack_small([d_gt1, d_sh2, d_sc2, d_gt2], [d_g_post_mix, d_g_pre_ffn, d_g_post_ffn], loss_part,
                         d_ln_g, d_ln_b, d_bs, d_cb, d_ws)

    def ffn_slot(j):
        return (j % half, j // half)

    dq, dk, dv, (r_out, r_up, r_down, r_cw), (g_small,) = _attn_bwd(
        qp, kp, vp, d_o, lse, delta, tq,
        [(p_out, _plain_slot), (p_up, ffn_slot), (p_down, _plain_slot), (d_cw, ffn_slot)], [packed])
    grad_x, d_za, d_qp, d_kvp, d_sh1, d_sc1, d_g_pre_mix, d_g_q, d_g_kv = _mix_in_bwd(
        dq, dk, dv, z, d_guv, x2d, d_x2, ada_raw, ada_b, g_pre_mix, g_q, g_kv, w1t, wqt, wkv, cos_t, sin_t, tm)
    h1_3 = h1[None]
    dw1a = _tn_matmul(d_za[None], h1_3, "dw_in_a", ts)[0]
    dw1b = _tn_matmul(d_guv[None], h1_3, "dw_in_b", ts)[0]
    d_w_in_t = jnp.concatenate([dw1a[:o2], dw1a[o2 + NOPE:o2 + NOPE + ROPE], dw1b], axis=0)
    p_in = d_w_in_t.reshape(N_DEV, -1, D_MODEL)
    p_uq = _tn_matmul(d_qp[None], cqn[None], "dw_uq", ts)[0].reshape(HEADS, HEAD_PAD, Q_LORA)[:, :NOPE + ROPE]
    dwkv = _tn_matmul(ckvn[None], d_kvp[None], "dw_ukv", ts)[0]
    dwk = dwkv[:, :hp].reshape(KV_LORA, HEADS, HEAD_PAD)[:, :, :NOPE]
    dwv = dwkv[:, hp:].reshape(KV_LORA, HEADS, HEAD_PAD)[:, :, :NOPE]
    p_ukv = jnp.transpose(jnp.concatenate([dwk, dwv], axis=2), (1, 0, 2))

    (g_late,), (r_in, r_uq, r_ukv) = _exchange(
        [_pack_late(d_sh1, d_sc1, d_g_pre_mix, d_g_q, d_g_kv)],
        [(p_in, _plain_slot), (p_uq, _plain_slot), (p_ukv, _plain_slot)], "final_exchange")
    small_params = [(b_ada, m_b_ada, v_b_ada), (g_pre_mix, m_g_pre_mix, v_g_pre_mix),
                    (g_post_mix, m_g_post_mix, v_g_post_mix), (g_pre_ffn, m_g_pre_ffn, v_g_pre_ffn),
                    (g_post_ffn, m_g_post_ffn, v_g_post_ffn), (g_q, m_g_q, v_g_q), (g_kv, m_g_kv, v_g_kv),
                    (gm_ln_g, m_gm_ln_g, v_gm_ln_g), (gm_ln_b, m_gm_ln_b, v_gm_ln_b),
                    (w_spatial, m_w_spatial, v_w_spatial), (b_spatial, m_b_spatial, v_b_spatial),
                    tuple(a.reshape(N_DEV, FF_BLK) for a in (conv_b, m_conv_b, v_conv_b))]
    small_out, loss_row, d_ada_all = _adam_small(g_small, g_late, small_params)
    small_out[11] = tuple(o.reshape(conv_b.shape) for o in small_out[11])
    loss = loss_row[0, 0]

    def big(recv, w, m, v, name):
        g, d, m2, v2 = _adam_reduce(recv, w[0], m[0], v[0], name)
        return g[None], d[None], m2[None], v2[None]

    def big_t(recv, w_t, m_t, v_t, name):
        return tuple(jnp.swapaxes(o, 0, 1)[None] for o in _adam_reduce(recv, w_t, m_t, v_t, name))

    a_in = big_t(r_in, w_in_t, m_in_t, v_in_t, "adam_w_in")
    a_uq = big_t(r_uq, w_uq_t, m_uq_t, v_uq_t, "adam_w_uq")
    a_ukv = big(r_ukv, w_ukv, m_w_ukv, v_w_ukv, "adam_w_ukv")
    a_out = big(r_out, w_out, m_w_out, v_w_out, "adam_w_out")
    a_up = big_t(r_up, w_up_t, m_up_t, v_up_t, "adam_w_up")
    a_down = big(r_down, w_down, m_w_down, v_w_down, "adam_w_down")
    ada_cols = w_ada.shape[2]
    d_ada_cols = lax.dynamic_slice(d_ada_all.reshape(N_DEV, 6 * D_MODEL), (0, my_slot * ada_cols), (N_DEV, ada_cols))
    a_ada = tuple(t[None] for t in _adam_w_ada(c_act.T, d_ada_cols, w_ada[0], m_w_ada[0], v_w_ada[0]))
    a_cw = big(r_cw, conv_w, m_conv_w, v_conv_w, "adam_conv_w")

    def small(k):
        return small_out[k]

    per_weight = [a_ada, small(0), small(1), small(2), a_in, small(5), a_uq, small(6), a_ukv, small(7), small(8),
                  small(9), small(10), a_out, small(3), small(4), a_up, a_cw, small(11), a_down]
    outs = [loss, grad_x[None]]
    for k in range(4):
        outs += [t[k] for t in per_weight]
    return tuple(outs)
```

```python
import functools

import jax
import jax.numpy as jnp
from jax import lax
from jax.experimental import pallas as pl
from jax.experimental.pallas import tpu as pltpu

F32 = jnp.float32
MXU = jnp.bfloat16

N_DEV = 8
D_MODEL = 1024
HEADS = 8
HEAD_PAD = 128
NOPE = 64
ROPE = 32
Q_LORA = 256
KV_LORA = 128
GM_WIDTH = 512
GM_DIM = 64
GM_CHUNK = 128
CHUNK_SHIFT = 6
ROPE_THETA = 10000.0
ATTN_SCALE = (NOPE + ROPE) ** -0.5
LOG2E = 1.4426950408889634
SCALE_LOG2E = ATTN_SCALE * LOG2E
Z_COLS = 1536
FF_BLK = 704
EPS = 1e-6
ADAM_LR = 0.001
ADAM_B1 = 0.9
ADAM_B2 = 0.999
ADAM_EPS = 1e-08
ADAM_WD = 0.01
ADAM_STEP = 10
VMEM_LIMIT = 56 * 1024 * 1024
MESH = pl.DeviceIdType.MESH


def _dot(a, b):
    return jnp.dot(a, b, preferred_element_type=F32)


def _dot_nt(a, b):
    return lax.dot_general(a, b, (((1,), (1,)), ((), ())), preferred_element_type=F32)


def _dot_tn(a, b):
    return lax.dot_general(a, b, (((0,), (0,)), ((), ())), preferred_element_type=F32)


def _call(body, *, name, grid, in_specs, out_specs, out_shape, scratch=(), sem=None):
    params = pltpu.CompilerParams(dimension_semantics=sem, vmem_limit_bytes=VMEM_LIMIT)
    return pl.pallas_call(body, name=name, grid=grid, in_specs=in_specs, out_specs=out_specs,
                          out_shape=out_shape, scratch_shapes=list(scratch), compiler_params=params)


def _full(shape):
    n = len(shape)
    return pl.BlockSpec(shape, lambda *_: (0,) * n)


def _rows(tm, cols, col_block=0):
    return pl.BlockSpec((tm, cols), lambda i: (i, col_block))


def _sds(shape, dtype):
    return jax.ShapeDtypeStruct(shape, dtype)


def _row(ref, k):
    return ref[pl.ds(k, 1), :]


def _rms(x):
    r = lax.rsqrt(jnp.mean(x * x, axis=-1, keepdims=True) + EPS)
    return x * r, r


def _rms_bwd(d_hat, hat, r):
    return r * (d_hat - hat * jnp.mean(d_hat * hat, axis=-1, keepdims=True))


def _rope_partner(t):
    lane = lax.broadcasted_iota(jnp.int32, t.shape, 1)
    swapped = jnp.where(lane < NOPE + ROPE // 2, -pltpu.roll(t, HEAD_PAD - ROPE // 2, 1), pltpu.roll(t, ROPE // 2, 1))
    return jnp.where((lane >= NOPE) & (lane < NOPE + ROPE), swapped, 0.0)


def _rope(t, cos, sin):
    return t * cos + _rope_partner(t) * sin


def _rope_transposed(g, cos, sin):
    return g * cos - _rope_partner(g * sin)


def _gelu(x):
    return x * (0.5 * (1.0 + jnp.tanh(0.7978845608028654 * (x + 0.044715 * (x * x * x)))))


def _gelu_grad(x):
    t = jnp.tanh(0.7978845608028654 * (x + 0.044715 * (x * x * x)))
    return 0.5 * (1.0 + t) + 0.5 * x * (1.0 - t * t) * (0.7978845608028654 * (1.0 + 3.0 * 0.044715 * (x * x)))


def _split_dot(x, mat):
    hi = x.astype(MXU)
    lo = (x - hi.astype(F32)).astype(MXU)
    return _dot(hi, mat) + _dot(lo, mat)


def _split_dot3(x, mat):
    hi = x.astype(MXU)
    r1 = x - hi.astype(F32)
    mid = r1.astype(MXU)
    lo = (r1 - mid.astype(F32)).astype(MXU)
    return (_dot(hi, mat) + _dot(mid, mat)) + _dot(lo, mat)


def _seg_matrix():
    r = lax.broadcasted_iota(jnp.int32, (GM_WIDTH, GM_WIDTH), 0) >> 6
    c = lax.broadcasted_iota(jnp.int32, (GM_WIDTH, GM_WIDTH), 1) >> 6
    return jnp.where(r == c, 1.0 / GM_DIM, 0.0).astype(MXU)


def _spatial_mask():
    i = lax.broadcasted_iota(jnp.int32, (GM_CHUNK, GM_CHUNK), 0) >> CHUNK_SHIFT
    j = lax.broadcasted_iota(jnp.int32, (GM_CHUNK, GM_CHUNK), 1) >> CHUNK_SHIFT
    return (j <= i).astype(F32)


def _my_place():
    return lax.axis_index("x"), lax.axis_index("y"), lax.axis_index("c")


def _flat(p):
    return 4 * p[0] + 2 * p[1] + p[2]


def _comm_sems(n):
    return [pltpu.SemaphoreType.DMA((7 * n,)), pltpu.SemaphoreType.DMA((7 * n,)), pltpu.SemaphoreType.DMA((n,))]


def _gather_steps(ins, outs, sems):
    send_sems, recv_sems, local_sems = sems
    n = len(ins)
    x, y, c = _my_place()
    me, sibling = (x, y, c), (x, y, 1 - c)
    chips = [(1 - x, y), (x, 1 - y), (1 - x, 1 - y)]

    def copy(a, k, block, to, src=None):
        slot = outs[a].at[_flat(block)]
        return pltpu.make_async_remote_copy(
            src_ref=slot if src is None else src, dst_ref=slot,
            send_sem=send_sems.at[7 * a + k], recv_sem=recv_sems.at[7 * a + k],
            device_id=to, device_id_type=MESH)

    def mine():
        return [pltpu.make_async_copy(ins[a], outs[a].at[_flat(me)], local_sems.at[a]) for a in range(n)]

    def first():
        cps = []
        for a in range(n):
            cps.append(copy(a, 0, me, sibling, src=ins[a]))
            cps += [copy(a, 1 + j, me, (*chip, c), src=ins[a]) for j, chip in enumerate(chips)]
        return cps

    def passed():
        return [copy(a, 4 + j, (*chip, c), sibling) for a in range(n) for j, chip in enumerate(chips)]

    def start():
        for cp in mine() + first():
            cp.start()

    def forward():
        for a in range(n):
            for j, chip in enumerate(chips):
                copy(a, 1 + j, (*chip, c), me).wait_recv()
                copy(a, 4 + j, (*chip, c), sibling).start()

    def finish():
        for a in range(n):
            copy(a, 0, sibling, me).wait_recv()
            for j, chip in enumerate(chips):
                copy(a, 4 + j, (*chip, 1 - c), me).wait_recv()
        for cp in first() + passed():
            cp.wait_send()
        for cp in mine():
            cp.wait()

    return start, forward, finish


def _scatter_steps(ins, outs, sems, slots):
    send_sems, recv_sems, local_sems = sems
    n = len(ins)
    flips = [(fx, fy, fc) for fx in (0, 1) for fy in (0, 1) for fc in (0, 1)][1:]
    me = _my_place()

    def peer(f):
        return tuple(1 - v if b else v for v, b in zip(me, f))

    def copy(a, k, arriving=False):
        p = peer(flips[k])
        return pltpu.make_async_remote_copy(
            src_ref=ins[a].at[slots[a](_flat(p))], dst_ref=outs[a].at[_flat(p if arriving else me)],
            send_sem=send_sems.at[7 * a + k], recv_sem=recv_sems.at[7 * a + k],
            device_id=p, device_id_type=MESH)

    def mine():
        return [pltpu.make_async_copy(ins[a].at[slots[a](_flat(me))], outs[a].at[_flat(me)], local_sems.at[a])
                for a in range(n)]

    def start():
        for cp in mine() + [copy(a, k) for a in range(n) for k in range(7)]:
            cp.start()

    def finish():
        for a in range(n):
            for k in range(7):
                copy(a, k, arriving=True).wait_recv()
        for a in range(n):
            for k in range(7):
                copy(a, k).wait_send()
        for cp in mine():
            cp.wait()

    return start, finish


def _plain_slot(j):
    return (j,)


def _scatter_out_shape(arr, slot):
    return _sds((N_DEV,) + arr.shape[len(slot(0)):], arr.dtype)


def _exchange(gathered, scattered, name):
    ng, ns = len(gathered), len(scattered)
    slots = [slot for _, slot in scattered]

    def body(*refs):
        g_in, s_in = refs[:ng], refs[ng:ng + ns]
        g_out, s_out = refs[ng + ns:2 * ng + ns], refs[2 * ng + ns:2 * (ng + ns)]
        sems = refs[2 * (ng + ns):]
        g_start, g_forward, g_finish = _gather_steps(g_in, g_out, sems[:3])
        s_start, s_finish = _scatter_steps(s_in, s_out, sems[3:], slots)
        g_start()
        s_start()
        g_forward()
        g_finish()
        s_finish()

    any_spec = pl.BlockSpec(memory_space=pl.ANY)
    outs = pl.pallas_call(
        body, name=name,
        in_specs=[any_spec] * (ng + ns), out_specs=[any_spec] * (ng + ns),
        out_shape=[_sds((N_DEV,) + a.shape, a.dtype) for a in gathered]
        + [_scatter_out_shape(a, slot) for a, slot in scattered],
        scratch_shapes=_comm_sems(max(ng, 1)) + _comm_sems(max(ns, 1)),
    )(*gathered, *[a for a, _ in scattered])
    return outs[:ng], outs[ng:]


def _ada_fwd(c_all, w_ada):
    def body(c_ref, w_ref, part_ref, act_ref):
        cv = c_ref[...]
        act = cv * jax.nn.sigmoid(cv)
        act_ref[...] = act
        part_ref[...] = _dot(act.astype(MXU), w_ref[...].astype(MXU))

    cols = w_ada.shape[1]
    return _call(body, name="ada_fwd", grid=(1,),
                 in_specs=[_full(c_all.shape), _full(w_ada.shape)],
                 out_specs=[_full((N_DEV, cols)), _full(c_all.shape)],
                 out_shape=[_sds((N_DEV, cols), F32), _sds(c_all.shape, F32)])(c_all, w_ada)


def _mix_in_fwd(x, ada_raw, ada_b, g_pre, w1, g_q, g_kv, wq, wkv, cos_t, sin_t, tm):
    s = x.shape[0]

    def body(x_ref, ar_ref, ab_ref, g_ref, w1_ref, gq_ref, gkv_ref, wq_ref, wkv_ref, cos_ref, sin_ref,
             h1_ref, z_ref, qp_ref, kp_ref, vp_ref, cqn_ref, ckvn_ref):
        sh = _row(ar_ref, 0) + _row(ab_ref, 0)
        sc = _row(ar_ref, 1) + _row(ab_ref, 1)
        xn, _ = _rms(x_ref[...])
        hb = ((xn * g_ref[...]) * (1.0 + sc) + sh).astype(MXU)
        h1_ref[...] = hb
        z = _dot_nt(hb, w1_ref[...])
        z_ref[...] = z
        cos, sin = cos_ref[...], sin_ref[...]
        cqn = (_rms(z[:, :Q_LORA])[0] * gq_ref[...]).astype(MXU)
        ckvn = (_rms(z[:, Q_LORA:Q_LORA + KV_LORA])[0] * gkv_ref[...]).astype(MXU)
        cqn_ref[...] = cqn
        ckvn_ref[...] = ckvn
        q = _dot_nt(cqn, wq_ref[...])
        kv = _dot(ckvn, wkv_ref[...])
        k_rope = _rope(z[:, Q_LORA + KV_LORA:Q_LORA + KV_LORA + HEAD_PAD], cos, sin)
        for h in range(HEADS):
            blk = slice(h * HEAD_PAD, (h + 1) * HEAD_PAD)
            qp_ref[:, blk] = _rope(q[:, blk], cos, sin).astype(MXU)
            kp_ref[:, blk] = (kv[:, blk] + k_rope).astype(MXU)
        v_lane = lax.broadcasted_iota(jnp.int32, (tm, HEADS * HEAD_PAD), 1) & (HEAD_PAD - 1)
        vp_ref[...] = jnp.where(v_lane == NOPE, 1.0, kv[:, HEADS * HEAD_PAD:]).astype(MXU)

    hp = HEADS * HEAD_PAD
    return _call(
        body, name="mix_in_fwd", grid=(s // tm,), sem=("parallel",),
        in_specs=[_rows(tm, D_MODEL), _full(ada_raw.shape), _full(ada_b.shape), _full(g_pre.shape), _full(w1.shape),
                  _full(g_q.shape), _full(g_kv.shape), _full(wq.shape), _full(wkv.shape),
                  _rows(tm, HEAD_PAD), _rows(tm, HEAD_PAD)],
        out_specs=[_rows(tm, D_MODEL), _rows(tm, Z_COLS), _rows(tm, hp), _rows(tm, hp), _rows(tm, hp),
                   _rows(tm, Q_LORA), _rows(tm, KV_LORA)],
        out_shape=[_sds((s, D_MODEL), MXU), _sds((s, Z_COLS), F32), _sds((s, hp), MXU), _sds((s, hp), MXU),
                   _sds((s, hp), MXU), _sds((s, Q_LORA), MXU), _sds((s, KV_LORA), MXU)],
    )(x, ada_raw, ada_b, g_pre, w1, g_q, g_kv, wq, wkv, cos_t, sin_t)


def _gm_norm(zv, seg):
    gv = _gelu(zv)
    cen = gv - _split_dot(gv, seg)
    rstd = lax.rsqrt(_split_dot(cen * cen, seg) + EPS)
    return gv, cen * rstd, rstd


def _gm_pairs(rows):
    first = lax.broadcasted_iota(jnp.int32, (rows, 2 * GM_DIM), 1) < GM_DIM
    return [(slice(p * 2 * GM_DIM, (p + 1) * 2 * GM_DIM), first) for p in range(HEADS // 2)]


def _gm_mix(wm, vb, rows, transposed=False):
    dot = _dot_tn if transposed else _dot
    return jnp.concatenate([jnp.where(first, dot(wm[2 * p], vb[:, lanes]), dot(wm[2 * p + 1], vb[:, lanes]))
                            for p, (lanes, first) in enumerate(_gm_pairs(rows))], axis=1)


def _gmlp_fwd(z, ln_g, ln_b, w_sp, bias_exp, tm):
    s = z.shape[0]
    nblk = tm // GM_CHUNK

    def body(zu_ref, zv_ref, lg_ref, lb_ref, w_ref, be_ref, sgu_ref):
        seg = _seg_matrix()
        mask = _spatial_mask()
        wm = [(w_ref[h] * mask).astype(MXU) for h in range(HEADS)]
        gu = _gelu(zu_ref[...])
        _, vhat, _ = _gm_norm(zv_ref[...], seg)
        vln = (vhat * lg_ref[...] + lb_ref[...]).astype(MXU)
        for n in range(nblk):
            rows = slice(n * GM_CHUNK, (n + 1) * GM_CHUNK)
            mixed = _gm_mix(wm, vln[rows], GM_CHUNK) + be_ref[...]
            sgu_ref[rows, :] = (gu[rows] * mixed).astype(MXU)

    return _call(
        body, name="gmlp_fwd", grid=(s // tm,), sem=("parallel",),
        in_specs=[_rows(tm, GM_WIDTH, 1), _rows(tm, GM_WIDTH, 2), _full(ln_g.shape), _full(ln_b.shape),
                  _full(w_sp.shape), _full(bias_exp.shape)],
        out_specs=_rows(tm, GM_WIDTH), out_shape=_sds((s, GM_WIDTH), MXU),
    )(z, z, ln_g, ln_b, w_sp, bias_exp)


def _chunk_mask(n_q, n_k, q_off):
    qc = (q_off + lax.broadcasted_iota(jnp.int32, (n_q, n_k), 0)) >> CHUNK_SHIFT
    kc = lax.broadcasted_iota(jnp.int32, (n_q, n_k), 1) >> CHUNK_SHIFT
    return kc <= qc


NEG_BIG = -1e30
ATTN_HEADS_PER_STEP = 2


def _attn_fwd(qp, kp, vp, tq, gathered):
    s = qp.shape[0]
    nq = s // tq
    hb = ATTN_HEADS_PER_STEP
    groups = HEADS // hb
    width = hb * HEAD_PAD
    ng = len(gathered)

    def body(q_ref, k_ref, v_ref, *rest):
        g_in, (o_ref, lse_ref), g_out = rest[:ng], rest[ng:ng + 2], rest[ng + 2:2 * ng + 2]
        m_sc, acc_sc = rest[2 * ng + 2:2 * ng + 4]
        g_start, g_forward, g_finish = _gather_steps(g_in, g_out, rest[2 * ng + 4:])
        g, i = pl.program_id(0), pl.program_id(1)
        pl.when((g == 0) & (i == 0))(g_start)
        pl.when((g == groups - 1) & (i == 0))(g_forward)
        m_sc[...] = jnp.full(m_sc.shape, NEG_BIG, F32)
        acc_sc[...] = jnp.zeros(acc_sc.shape, F32)

        def tile(j, masked):
            rows = pl.ds(pl.multiple_of(j * tq, tq), tq)
            for hh in range(hb):
                lanes = slice(hh * HEAD_PAD, (hh + 1) * HEAD_PAD)
                sc = _dot_nt(q_ref[:, lanes], k_ref[rows, lanes])
                if masked:
                    sc = jnp.where(_chunk_mask(tq, tq, 0), sc, NEG_BIG)
                blocks = [sc[:, b * 128:(b + 1) * 128] for b in range(tq // 128)]
                m_prev = m_sc[hh]
                m_tile = jnp.max(functools.reduce(jnp.maximum, blocks), axis=-1, keepdims=True)
                m_new = jnp.maximum(m_prev, m_tile)
                alpha = jnp.exp2((m_prev - m_new) * SCALE_LOG2E)
                p = jnp.concatenate([jnp.exp2((b - m_new) * SCALE_LOG2E) for b in blocks], axis=1).astype(MXU)
                acc_sc[hh] = alpha * acc_sc[hh] + _dot(p, v_ref[rows, lanes])
                m_sc[hh] = m_new

        def off_diagonal_pair(p, carry):
            tile(2 * p, False)
            tile(2 * p + 1, False)
            return carry

        lax.fori_loop(0, i // 2, off_diagonal_pair, 0)

        @pl.when(i % 2 == 1)
        def _():
            tile(i - 1, False)

        tile(i, True)
        for hh in range(hb):
            lanes = slice(hh * HEAD_PAD, (hh + 1) * HEAD_PAD)
            acc = acc_sc[hh]
            denom = acc[:, NOPE:NOPE + 1]
            o_ref[:, lanes] = (acc / denom).astype(MXU)
            lse_ref[hh] = m_sc[hh][:, :1] * SCALE_LOG2E + jnp.log(denom) * LOG2E
        pl.when((g == groups - 1) & (i == nq - 1))(g_finish)

    q_spec = pl.BlockSpec((tq, width), lambda g, i: (i, g))
    kv_spec = pl.BlockSpec((s, width), lambda g, i: (0, g))
    any_spec = pl.BlockSpec(memory_space=pl.ANY)
    outs = _call(
        body, name="attn_fwd", grid=(groups, nq), sem=("arbitrary", "arbitrary"),
        in_specs=[q_spec, kv_spec, kv_spec] + [any_spec] * ng,
        out_specs=[q_spec, pl.BlockSpec((hb, tq, 1), lambda g, i: (g, i, 0))] + [any_spec] * ng,
        out_shape=[_sds(qp.shape, MXU), _sds((HEADS, s, 1), F32)]
        + [_sds((N_DEV,) + a.shape, a.dtype) for a in gathered],
        scratch=[pltpu.VMEM((hb, tq, HEAD_PAD), F32), pltpu.VMEM((hb, tq, HEAD_PAD), F32)] + _comm_sems(ng),
    )(qp, kp, vp, *gathered)
    return outs[0], outs[1], outs[2:]


def _out_proj_fwd(o_pad, sgu, wo, x, ada_raw, ada_b, g_post_mix, g_pre_ffn, tm):
    s = x.shape[0]
    hp = HEADS * HEAD_PAD

    def body(o_ref, sgu_ref, wo_ref, x_ref, ar_ref, ab_ref, gpm_ref, gpf_ref, m_ref, x2_ref, h2_ref):
        gt1 = _row(ar_ref, 2) + _row(ab_ref, 2)
        sh2 = _row(ar_ref, 3) + _row(ab_ref, 3)
        sc2 = _row(ar_ref, 4) + _row(ab_ref, 4)
        m = _dot(o_ref[...], wo_ref[pl.ds(0, hp), :]) + _dot(sgu_ref[...], wo_ref[pl.ds(hp, GM_WIDTH), :])
        m_ref[...] = m
        x2 = x_ref[...] + gt1 * (_rms(m)[0] * gpm_ref[...])
        x2_ref[...] = x2
        h2_ref[...] = ((_rms(x2)[0] * gpf_ref[...]) * (1.0 + sc2) + sh2).astype(MXU)

    return _call(
        body, name="out_proj_fwd", grid=(s // tm,), sem=("parallel",),
        in_specs=[_rows(tm, hp), _rows(tm, GM_WIDTH), _full(wo.shape), _rows(tm, D_MODEL), _full(ada_raw.shape),
                  _full(ada_b.shape), _full(g_post_mix.shape), _full(g_pre_ffn.shape)],
        out_specs=[_rows(tm, D_MODEL)] * 3,
        out_shape=[_sds((s, D_MODEL), F32), _sds((s, D_MODEL), F32), _sds((s, D_MODEL), MXU)],
    )(o_pad, sgu, wo, x, ada_raw, ada_b, g_post_mix, g_pre_ffn)


def _conv(u, halo, cw_ref, cb_ref):
    ext = jnp.concatenate([halo, u], axis=0)
    m1, m2 = pltpu.roll(ext, 1, 0)[8:], pltpu.roll(ext, 2, 0)[8:]
    return cb_ref[0] + ((m2 * cw_ref[0, pl.ds(0, 1), :] + m1 * cw_ref[0, pl.ds(1, 1), :]) + u * cw_ref[0, pl.ds(2, 1), :])


ROW_SUB = 256


def _sub_blocks(tm):
    return [slice(r, r + ROW_SUB) for r in range(0, tm, ROW_SUB)]


def _ffn_up_fwd(h2, w_up, conv_w, conv_b, tm):
    s = h2.shape[0]
    half = N_DEV // 2

    def body(h_ref, wa_ref, wb_ref, cwa_ref, cwb_ref, cba_ref, cbb_ref,
             ua_ref, ub_ref, ya_ref, yb_ref, act_ref, halo_a, halo_b):
        i = pl.program_id(1)

        @pl.when(i == 0)
        def _():
            halo_a[...] = jnp.zeros(halo_a.shape, F32)
            halo_b[...] = jnp.zeros(halo_b.shape, F32)

        ha, hb = halo_a[...], halo_b[...]
        for rows in _sub_blocks(tm):
            h = h_ref[rows, :]
            ua = _dot(h, wa_ref[0])
            ub = _dot(h, wb_ref[0])
            ua_ref[0, rows, :] = ua
            ub_ref[0, rows, :] = ub
            ya = _conv(ua, ha, cwa_ref, cba_ref)
            yb = _conv(ub, hb, cwb_ref, cbb_ref)
            ya_ref[0, rows, :] = ya
            yb_ref[0, rows, :] = yb
            ha, hb = ua[ROW_SUB - 8:], ub[ROW_SUB - 8:]
            act_ref[0, rows, :] = ((ya * jax.nn.sigmoid(ya)) * yb).astype(MXU)
        halo_a[...] = ha
        halo_b[...] = hb

    def blk(shape, off):
        return pl.BlockSpec(shape, lambda j, i: (j + off, 0, 0))

    def tok(off=0):
        return pl.BlockSpec((1, tm, FF_BLK), lambda j, i: (j + off, i, 0))

    return _call(
        body, name="ffn_up_fwd", grid=(half, s // tm), sem=("parallel", "arbitrary"),
        in_specs=[pl.BlockSpec((tm, D_MODEL), lambda j, i: (i, 0)),
                  blk((1, D_MODEL, FF_BLK), 0), blk((1, D_MODEL, FF_BLK), half),
                  blk((1, 3, FF_BLK), 0), blk((1, 3, FF_BLK), half), blk((1, 1, FF_BLK), 0), blk((1, 1, FF_BLK), half)],
        out_specs=[tok()] * 5,
        out_shape=[_sds((half, s, FF_BLK), F32)] * 4 + [_sds((half, s, FF_BLK), MXU)],
        scratch=[pltpu.VMEM((8, FF_BLK), F32), pltpu.VMEM((8, FF_BLK), F32)],
    )(h2, w_up, w_up, conv_w, conv_w, conv_b, conv_b)


def _ffn_down_fwd(act, wd, x2, target, ada_raw, ada_b, g_post_ffn, tm):
    s = x2.shape[0]
    half = N_DEV // 2

    def body(act_ref, wd_ref, x2_ref, t_ref, ar_ref, ab_ref, g_ref, dout_ref, df_ref, loss_ref, dgt_ref, dg_ref):
        i = pl.program_id(0)

        @pl.when(i == 0)
        def _():
            loss_ref[...] = jnp.zeros(loss_ref.shape, F32)
            dgt_ref[...] = jnp.zeros(dgt_ref.shape, F32)
            dg_ref[...] = jnp.zeros(dg_ref.shape, F32)

        gt2 = _row(ar_ref, 5) + _row(ab_ref, 5)
        g = g_ref[...]
        for rows in _sub_blocks(tm):
            f = _dot(act_ref[0, rows, :], wd_ref[0])
            for j in range(1, half):
                f = f + _dot(act_ref[j, rows, :], wd_ref[j])
            fhat, rf = _rms(f)
            fn = fhat * g
            err = (x2_ref[rows, :] + gt2 * fn) - t_ref[rows, :]
            loss_ref[...] += 0.5 * jnp.sum(jnp.mean(err * err, axis=-1, keepdims=True))
            d_out = err * (1.0 / D_MODEL)
            dout_ref[rows, :] = d_out
            dgt_ref[...] += jnp.sum(d_out * fn, axis=0, keepdims=True)
            d_fn = d_out * gt2
            dg_ref[...] += jnp.sum(d_fn * fhat, axis=0, keepdims=True)
            df_ref[rows, :] = _rms_bwd(d_fn * g, fhat, rf).astype(MXU)

    vec = pl.BlockSpec((1, D_MODEL), lambda i: (0, 0))
    return _call(
        body, name="ffn_down_fwd", grid=(s // tm,), sem=("arbitrary",),
        in_specs=[pl.BlockSpec((half, tm, FF_BLK), lambda i: (0, i, 0)), _full(wd.shape), _rows(tm, D_MODEL),
                  _rows(tm, D_MODEL), _full(ada_raw.shape), _full(ada_b.shape), _full(g_post_ffn.shape)],
        out_specs=[_rows(tm, D_MODEL), _rows(tm, D_MODEL), pl.BlockSpec((1, 128), lambda i: (0, 0)), vec, vec],
        out_shape=[_sds((s, D_MODEL), F32), _sds((s, D_MODEL), MXU), _sds((1, 128), F32),
                   _sds((1, D_MODEL), F32), _sds((1, D_MODEL), F32)],
    )(act, wd, x2, target, ada_raw, ada_b, g_post_ffn)


def _ffn_down_bwd(d_f, wd, up_a, up_b, y_a, y_b, conv_w, act, h2, tm):
    s = d_f.shape[0]
    half = N_DEV // 2
    nt = s // tm

    def body(df_ref, wd_ref, ua_ref, ub_ref, ya_ref, yb_ref, cwa_ref, cwb_ref, act_ref, h2_ref,
             dup_ref, dcw_ref, dcb_ref, pd_ref, pu_ref, next_a, next_b, acc_d, acc_a, acc_b):
        i = pl.program_id(1)

        @pl.when(i == 0)
        def _():
            next_a[...] = jnp.zeros(next_a.shape, F32)
            next_b[...] = jnp.zeros(next_b.shape, F32)
            dcw_ref[...] = jnp.zeros(dcw_ref.shape, F32)
            dcb_ref[...] = jnp.zeros(dcb_ref.shape, F32)
            for acc in (acc_d, acc_a, acc_b):
                acc[...] = jnp.zeros(acc.shape, F32)

        def conv_bwd(d_y, u, nxt, cw_ref, part, rows):
            ext = jnp.concatenate([d_y, nxt], axis=0)
            p1 = pltpu.roll(ext, ROW_SUB + 7, 0)[:ROW_SUB]
            p2 = pltpu.roll(ext, ROW_SUB + 6, 0)[:ROW_SUB]
            d_u = (d_y * cw_ref[0, pl.ds(2, 1), :] + p1 * cw_ref[0, pl.ds(1, 1), :]) + p2 * cw_ref[0, pl.ds(0, 1), :]
            dup_ref[0, part, rows, :] = d_u.astype(MXU)
            dcb_ref[0, part] += jnp.sum(d_y, axis=0, keepdims=True)
            dcw_ref[0, part, pl.ds(0, 1), :] += jnp.sum(p2 * u, axis=0, keepdims=True)
            dcw_ref[0, part, pl.ds(1, 1), :] += jnp.sum(p1 * u, axis=0, keepdims=True)
            dcw_ref[0, part, pl.ds(2, 1), :] += jnp.sum(d_y * u, axis=0, keepdims=True)
            return d_y[:8]

        nxa, nxb = next_a[...], next_b[...]
        for rows in reversed(_sub_blocks(tm)):
            d_act = _dot_nt(df_ref[rows, :], wd_ref[0])
            ya, yb = ya_ref[0, rows, :], yb_ref[0, rows, :]
            sig = jax.nn.sigmoid(ya)
            d_ya = d_act * yb * (sig * (1.0 + ya * (1.0 - sig)))
            d_yb = d_act * (ya * sig)
            nxa = conv_bwd(d_ya, ua_ref[0, rows, :], nxa, cwa_ref, 0, rows)
            nxb = conv_bwd(d_yb, ub_ref[0, rows, :], nxb, cwb_ref, 1, rows)
        next_a[...] = nxa
        next_b[...] = nxb
        acc_d[...] += _dot_tn(act_ref[0], df_ref[...])
        acc_a[...] += _dot_tn(dup_ref[0, 0], h2_ref[...])
        acc_b[...] += _dot_tn(dup_ref[0, 1], h2_ref[...])

        @pl.when(i == nt - 1)
        def _():
            pd_ref[0] = acc_d[...].astype(MXU)
            pu_ref[0, 0] = acc_a[...].astype(MXU)
            pu_ref[0, 1] = acc_b[...].astype(MXU)

    def rev(i):
        return nt - 1 - i

    def blk(shape, off):
        return pl.BlockSpec(shape, lambda j, i: (j + off, 0, 0))

    tok = pl.BlockSpec((1, tm, FF_BLK), lambda j, i: (j, rev(i), 0))
    acc3 = pl.BlockSpec((1, 2, 3, FF_BLK), lambda j, i: (j, 0, 0, 0))
    acc1 = pl.BlockSpec((1, 2, 1, FF_BLK), lambda j, i: (j, 0, 0, 0))
    return _call(
        body, name="ffn_down_bwd", grid=(half, nt), sem=("parallel", "arbitrary"),
        in_specs=[pl.BlockSpec((tm, D_MODEL), lambda j, i: (rev(i), 0)), blk((1, FF_BLK, D_MODEL), 0),
                  tok, tok, tok, tok, blk((1, 3, FF_BLK), 0), blk((1, 3, FF_BLK), half),
                  tok, pl.BlockSpec((tm, D_MODEL), lambda j, i: (rev(i), 0))],
        out_specs=[pl.BlockSpec((1, 2, tm, FF_BLK), lambda j, i: (j, 0, rev(i), 0)), acc3, acc1,
                   pl.BlockSpec((1, FF_BLK, D_MODEL), lambda j, i: (j, 0, 0)),
                   pl.BlockSpec((1, 2, FF_BLK, D_MODEL), lambda j, i: (j, 0, 0, 0))],
        out_shape=[_sds((half, 2, s, FF_BLK), MXU), _sds((half, 2, 3, FF_BLK), F32), _sds((half, 2, 1, FF_BLK), F32),
                   _sds((half, FF_BLK, D_MODEL), MXU), _sds((half, 2, FF_BLK, D_MODEL), MXU)],
        scratch=[pltpu.VMEM((8, FF_BLK), F32), pltpu.VMEM((8, FF_BLK), F32)]
        + [pltpu.VMEM((FF_BLK, D_MODEL), F32)] * 3,
    )(d_f, wd, up_a, up_b, y_a, y_b, conv_w, conv_w, act, h2)


def _ffn_up_bwd(d_up, w_up, x2, m, d_out, ada_raw, ada_b, g_pre_ffn, g_post_mix, tm):
    s = x2.shape[0]
    half = N_DEV // 2

    def body(dup_ref, w_ref, x2_ref, m_ref, dout_ref, ar_ref, ab_ref, gpf_ref, gpm_ref,
             dx_ref, dm_ref, dsh_ref, dsc_ref, dgpf_ref, dgt1_ref, dgpm_ref):
        i = pl.program_id(0)

        @pl.when(i == 0)
        def _():
            for r in (dsh_ref, dsc_ref, dgpf_ref, dgt1_ref, dgpm_ref):
                r[...] = jnp.zeros(r.shape, F32)

        gt1 = _row(ar_ref, 2) + _row(ab_ref, 2)
        sc2 = _row(ar_ref, 4) + _row(ab_ref, 4)
        gpf, gpm = gpf_ref[...], gpm_ref[...]
        d_h2 = _dot(dup_ref[0, 0], w_ref[0])
        for j in range(1, half):
            d_h2 = d_h2 + _dot(dup_ref[j, 0], w_ref[j])
        for j in range(half):
            d_h2 = d_h2 + _dot(dup_ref[j, 1], w_ref[half + j])
        x2n, r2 = _rms(x2_ref[...])
        dsh_ref[...] += jnp.sum(d_h2, axis=0, keepdims=True)
        dsc_ref[...] += jnp.sum(d_h2 * (x2n * gpf), axis=0, keepdims=True)
        d_mod = d_h2 * (1.0 + sc2)
        dgpf_ref[...] += jnp.sum(d_mod * x2n, axis=0, keepdims=True)
        d_x2 = dout_ref[...] + _rms_bwd(d_mod * gpf, x2n, r2)
        dx_ref[...] = d_x2
        mhat, rm = _rms(m_ref[...])
        dgt1_ref[...] += jnp.sum(d_x2 * (mhat * gpm), axis=0, keepdims=True)
        d_mn = d_x2 * gt1
        dgpm_ref[...] += jnp.sum(d_mn * mhat, axis=0, keepdims=True)
        dm_ref[...] = _rms_bwd(d_mn * gpm, mhat, rm).astype(MXU)

    vec = pl.BlockSpec((1, D_MODEL), lambda i: (0, 0))
    tok = pl.BlockSpec((half, 2, tm, FF_BLK), lambda i: (0, 0, i, 0))
    return _call(
        body, name="ffn_up_bwd", grid=(s // tm,), sem=("arbitrary",),
        in_specs=[tok, _full(w_up.shape), _rows(tm, D_MODEL), _rows(tm, D_MODEL), _rows(tm, D_MODEL),
                  _full(ada_raw.shape), _full(ada_b.shape), _full(g_pre_ffn.shape), _full(g_post_mix.shape)],
        out_specs=[_rows(tm, D_MODEL), _rows(tm, D_MODEL), vec, vec, vec, vec, vec],
        out_shape=[_sds((s, D_MODEL), F32), _sds((s, D_MODEL), MXU)] + [_sds((1, D_MODEL), F32)] * 5,
    )(d_up, w_up, x2, m, d_out, ada_raw, ada_b, g_pre_ffn, g_post_mix)


def _out_proj_bwd(d_m, wo, o_pad, tm):
    s = d_m.shape[0]
    hp = HEADS * HEAD_PAD

    def body(dm_ref, wo_ref, o_ref, do_ref, dsgu_ref, delta_ref):
        d_cat = _dot_nt(dm_ref[...], wo_ref[...])
        d_o = d_cat[:, :hp]
        do_ref[...] = d_o.astype(MXU)
        dsgu_ref[...] = d_cat[:, hp:]
        prod = d_o * o_ref[...].astype(F32)
        for h in range(HEADS):
            delta_ref[h] = jnp.sum(prod[:, h * HEAD_PAD:(h + 1) * HEAD_PAD], axis=-1, keepdims=True)

    return _call(
        body, name="out_proj_bwd", grid=(s // tm,), sem=("parallel",),
        in_specs=[_rows(tm, D_MODEL), _full(wo.shape), _rows(tm, hp)],
        out_specs=[_rows(tm, hp), _rows(tm, GM_WIDTH), pl.BlockSpec((HEADS, tm, 1), lambda i: (0, i, 0))],
        out_shape=[_sds((s, hp), MXU), _sds((s, GM_WIDTH), F32), _sds((HEADS, s, 1), F32)],
    )(d_m, wo, o_pad)


def _attn_bwd(qp, kp, vp, d_o, lse, delta, tq, scattered, gathered):
    s = qp.shape[0]
    nq = s // tq
    hb = ATTN_HEADS_PER_STEP
    groups = HEADS // hb
    width = hb * HEAD_PAD
    ns, ng = len(scattered), len(gathered)
    nc = ns + ng
    slots = [slot for _, slot in scattered]

    def body(q_ref, k_ref, v_ref, do_ref, lse_ref, dl_ref, *rest):
        c_in, (dq_ref, dk_ref, dv_ref), c_out = rest[:nc], rest[nc:nc + 3], rest[nc + 3:2 * nc + 3]
        dk_sc, dv_sc = rest[2 * nc + 3:2 * nc + 5]
        sems = rest[2 * nc + 5:]
        s_start, s_finish = _scatter_steps(c_in[:ns], c_out[:ns], sems[:3], slots)
        g_start, g_forward, g_finish = _gather_steps(c_in[ns:], c_out[ns:], sems[3:])
        g, j = pl.program_id(0), pl.program_id(1)

        @pl.when((g == 0) & (j == 0))
        def _():
            s_start()
            g_start()

        pl.when((g == groups - 1) & (j == 0))(g_forward)

        @pl.when(j == 0)
        def _():
            dq_ref[...] = jnp.zeros(dq_ref.shape, F32)

        dk_sc[...] = jnp.zeros(dk_sc.shape, F32)
        dv_sc[...] = jnp.zeros(dv_sc.shape, F32)

        def tile(i, masked):
            rows = pl.ds(pl.multiple_of(i * tq, tq), tq)
            for hh in range(hb):
                lanes = slice(hh * HEAD_PAD, (hh + 1) * HEAD_PAD)
                q, do, k = q_ref[rows, lanes], do_ref[rows, lanes], k_ref[:, lanes]
                sc = _dot_nt(q, k)
                if masked:
                    sc = jnp.where(_chunk_mask(tq, tq, 0), sc, NEG_BIG)
                p = jnp.exp2(sc * SCALE_LOG2E - lse_ref[hh, rows, :])
                dv_sc[hh] += _dot_tn(p.astype(MXU), do)
                dp = _dot_nt(do, v_ref[:, lanes])
                ds = (p * (dp - dl_ref[hh, rows, :])).astype(MXU)
                dk_sc[hh] += _dot_tn(ds, q)
                dq_ref[rows, lanes] += _dot(ds, k) * ATTN_SCALE

        def off_diagonal_pair(p, carry):
            tile(j + 1 + 2 * p, False)
            tile(j + 2 + 2 * p, False)
            return carry

        below = nq - 1 - j
        tile(j, True)
        lax.fori_loop(0, below // 2, off_diagonal_pair, 0)

        @pl.when(below % 2 == 1)
        def _():
            tile(nq - 1, False)
        for hh in range(hb):
            lanes = slice(hh * HEAD_PAD, (hh + 1) * HEAD_PAD)
            dk_ref[:, lanes] = dk_sc[hh] * ATTN_SCALE
            dv_ref[:, lanes] = dv_sc[hh]
        @pl.when((g == groups - 1) & (j == nq - 1))
        def _():
            g_finish()
            s_finish()

    seq_spec = pl.BlockSpec((s, width), lambda g, j: (0, g))
    kv_spec = pl.BlockSpec((tq, width), lambda g, j: (j, g))
    col_spec = pl.BlockSpec((hb, s, 1), lambda g, j: (g, 0, 0))
    any_spec = pl.BlockSpec(memory_space=pl.ANY)
    outs = _call(
        body, name="attn_bwd", grid=(groups, nq), sem=("arbitrary", "arbitrary"),
        in_specs=[seq_spec, kv_spec, kv_spec, seq_spec, col_spec, col_spec] + [any_spec] * nc,
        out_specs=[seq_spec, kv_spec, kv_spec] + [any_spec] * nc,
        out_shape=[_sds(qp.shape, F32), _sds(qp.shape, F32), _sds(qp.shape, F32)]
        + [_scatter_out_shape(a, slot) for a, slot in scattered]
        + [_sds((N_DEV,) + a.shape, a.dtype) for a in gathered],
        scratch=[pltpu.VMEM((hb, tq, HEAD_PAD), F32), pltpu.VMEM((hb, tq, HEAD_PAD), F32)]
        + _comm_sems(ns) + _comm_sems(ng),
    )(qp, kp, vp, d_o, lse, delta, *[a for a, _ in scattered], *gathered)
    return outs[0], outs[1], outs[2], outs[3:3 + ns], outs[3 + ns:]


def _gmlp_bwd(z, d_sgu, ln_g, ln_b, w_sp, bias_exp, tm):
    s = z.shape[0]
    nblk = tm // GM_CHUNK

    def body(zu_ref, zv_ref, dsgu_ref, lg_ref, lb_ref, w_ref, be_ref,
             dguv_ref, dws_ref, dbs_ref, dlg_ref, dlb_ref, dbe_sc, dvln_sc, dlg_sc, dlb_sc):
        i = pl.program_id(0)

        @pl.when(i == 0)
        def _():
            for r in (dws_ref, dlg_sc, dlb_sc, dbe_sc):
                r[...] = jnp.zeros(r.shape, F32)

        seg = _seg_matrix()
        mask = _spatial_mask()
        wm = [(w_ref[h] * mask).astype(MXU) for h in range(HEADS)]
        zu, zv = zu_ref[...], zv_ref[...]
        gu = _gelu(zu)
        _, vhat, rstd = _gm_norm(zv, seg)
        lg = lg_ref[...]
        vln = (vhat * lg + lb_ref[...]).astype(MXU)
        d_sgu = dsgu_ref[...]
        for n in range(nblk):
            rows = slice(n * GM_CHUNK, (n + 1) * GM_CHUNK)
            vb = vln[rows]
            mixed = _gm_mix(wm, vb, GM_CHUNK) + be_ref[...]
            d_mixed = d_sgu[rows] * gu[rows]
            dguv_ref[rows, pl.ds(0, GM_WIDTH)] = ((d_sgu[rows] * mixed) * _gelu_grad(zu[rows])).astype(MXU)
            dbe_sc[...] += d_mixed
            dmb = d_mixed.astype(MXU)
            for p, (lanes, first) in enumerate(_gm_pairs(GM_CHUNK)):
                dm_pair, v_pair = dmb[:, lanes], vb[:, lanes]
                zero = jnp.zeros_like(dm_pair)
                dws_ref[2 * p] += _dot_nt(jnp.where(first, dm_pair, zero), v_pair)
                dws_ref[2 * p + 1] += _dot_nt(jnp.where(first, zero, dm_pair), v_pair)
            dvln_sc[rows, :] = _gm_mix(wm, dmb, GM_CHUNK, transposed=True)
        d_vln = dvln_sc[...]
        dlg_sc[...] += jnp.sum(d_vln * vhat, axis=0, keepdims=True)
        dlb_sc[...] += jnp.sum(d_vln, axis=0, keepdims=True)
        d_vhat = d_vln * lg
        d_gv = rstd * ((d_vhat - _split_dot(d_vhat, seg)) - vhat * _split_dot(d_vhat * vhat, seg))
        dguv_ref[:, pl.ds(GM_WIDTH, GM_WIDTH)] = (d_gv * _gelu_grad(zv)).astype(MXU)

        @pl.when(i == pl.num_programs(0) - 1)
        def _():
            for h in range(HEADS):
                dws_ref[h] = dws_ref[h] * mask
            hrow = lax.broadcasted_iota(jnp.int32, (HEADS, GM_WIDTH), 0)
            hlane = lax.broadcasted_iota(jnp.int32, (HEADS, GM_WIDTH), 1) >> 6
            ind = jnp.where(hrow == hlane, 1.0, 0.0).astype(MXU)
            acc = dbe_sc[...]
            hi = acc.astype(MXU)
            lo = (acc - hi.astype(F32)).astype(MXU)
            dbs_ref[...] = _dot_nt(ind, hi) + _dot_nt(ind, lo)
            pick = (lax.broadcasted_iota(jnp.int32, (GM_WIDTH, GM_DIM), 0) & (GM_DIM - 1)
                    == lax.broadcasted_iota(jnp.int32, (GM_WIDTH, GM_DIM), 1))
            pick = jnp.where(pick, 1.0, 0.0).astype(MXU)
            for src, dst in ((dlg_sc, dlg_ref), (dlb_sc, dlb_ref)):
                spread = jnp.where(hrow == hlane, jnp.broadcast_to(src[...], (HEADS, GM_WIDTH)), 0.0)
                dst[...] = _split_dot3(spread, pick)

    return _call(
        body, name="gmlp_bwd", grid=(s // tm,), sem=("arbitrary",),
        in_specs=[_rows(tm, GM_WIDTH, 1), _rows(tm, GM_WIDTH, 2), _rows(tm, GM_WIDTH), _full(ln_g.shape),
                  _full(ln_b.shape), _full(w_sp.shape), _full(bias_exp.shape)],
        out_specs=[_rows(tm, 2 * GM_WIDTH), _full(w_sp.shape), _full((HEADS, GM_CHUNK)), _full((HEADS, GM_DIM)),
                   _full((HEADS, GM_DIM))],
        out_shape=[_sds((s, 2 * GM_WIDTH), MXU), _sds(w_sp.shape, F32), _sds((HEADS, GM_CHUNK), F32),
                   _sds((HEADS, GM_DIM), F32), _sds((HEADS, GM_DIM), F32)],
        scratch=[pltpu.VMEM((GM_CHUNK, GM_WIDTH), F32), pltpu.VMEM((tm, GM_WIDTH), F32),
                 pltpu.VMEM((1, GM_WIDTH), F32), pltpu.VMEM((1, GM_WIDTH), F32)],
    )(z, z, d_sgu, ln_g, ln_b, w_sp, bias_exp)


def _mix_in_bwd(dq, dk, dv, z, d_guv, x, d_x_part, ada_raw, ada_b, g_pre, g_q, g_kv, w1t, wqt, wkv,
                cos_t, sin_t, tm):
    s = x.shape[0]
    hp = HEADS * HEAD_PAD
    za = Q_LORA + KV_LORA + HEAD_PAD

    def body(dq_ref, dk_ref, dv_ref, z_ref, dguv_ref, x_ref, dxp_ref, ar_ref, ab_ref, g_ref, gq_ref, gkv_ref,
             w1_ref, wq_ref, wkv_ref, cos_ref, sin_ref,
             gx_ref, dza_ref, dqp_ref, dkvp_ref, dsh_ref, dsc_ref, dg_ref, dgq_ref, dgkv_ref):
        i = pl.program_id(0)

        @pl.when(i == 0)
        def _():
            for r in (dsh_ref, dsc_ref, dg_ref, dgq_ref, dgkv_ref):
                r[...] = jnp.zeros(r.shape, F32)

        cos, sin = cos_ref[...], sin_ref[...]
        d_krot = jnp.zeros((tm, HEAD_PAD), F32)
        for h in range(HEADS):
            blk = slice(h * HEAD_PAD, (h + 1) * HEAD_PAD)
            dqp_ref[:, blk] = _rope_transposed(dq_ref[:, blk], cos, sin).astype(MXU)
            dk_h = dk_ref[:, blk]
            d_krot = d_krot + dk_h
            dkvp_ref[:, blk] = dk_h.astype(MXU)
        dkvp_ref[:, pl.ds(hp, hp)] = dv_ref[...].astype(MXU)
        lane = lax.broadcasted_iota(jnp.int32, (tm, HEAD_PAD), 1)
        d_kr = jnp.where((lane >= NOPE) & (lane < NOPE + ROPE), _rope_transposed(d_krot, cos, sin), 0.0)
        d_cqn = _dot(dqp_ref[...], wq_ref[...])
        d_ckvn = _dot_nt(dkvp_ref[...], wkv_ref[...])
        zt = z_ref[...]
        gq, gkv = gq_ref[...], gkv_ref[...]
        cq_hat, rq = _rms(zt[:, :Q_LORA])
        ckv_hat, rkv = _rms(zt[:, Q_LORA:Q_LORA + KV_LORA])
        dgq_ref[...] += jnp.sum(d_cqn * cq_hat, axis=0, keepdims=True)
        dgkv_ref[...] += jnp.sum(d_ckvn * ckv_hat, axis=0, keepdims=True)
        d_cq = _rms_bwd(d_cqn * gq, cq_hat, rq)
        d_ckv = _rms_bwd(d_ckvn * gkv, ckv_hat, rkv)
        d_za = jnp.concatenate([d_cq, d_ckv, d_kr], axis=1).astype(MXU)
        dza_ref[...] = d_za
        d_h1 = _dot(d_za, w1_ref[pl.ds(0, za), :]) + _dot(dguv_ref[...], w1_ref[pl.ds(za, 2 * GM_WIDTH), :])
        sc1 = _row(ar_ref, 1) + _row(ab_ref, 1)
        g = g_ref[...]
        xn, r1 = _rms(x_ref[...])
        dsh_ref[...] += jnp.sum(d_h1, axis=0, keepdims=True)
        dsc_ref[...] += jnp.sum(d_h1 * (xn * g), axis=0, keepdims=True)
        d_mod = d_h1 * (1.0 + sc1)
        dg_ref[...] += jnp.sum(d_mod * xn, axis=0, keepdims=True)
        gx_ref[...] = dxp_ref[...] + _rms_bwd(d_mod * g, xn, r1)

    vec = pl.BlockSpec((1, D_MODEL), lambda i: (0, 0))
    return _call(
        body, name="mix_in_bwd", grid=(s // tm,), sem=("arbitrary",),
        in_specs=[_rows(tm, hp), _rows(tm, hp), _rows(tm, hp), _rows(tm, za), _rows(tm, 2 * GM_WIDTH),
                  _rows(tm, D_MODEL), _rows(tm, D_MODEL), _full(ada_raw.shape), _full(ada_b.shape), _full(g_pre.shape),
                  _full(g_q.shape), _full(g_kv.shape), _full(w1t.shape), _full(wqt.shape),
                  _full(wkv.shape), _rows(tm, HEAD_PAD), _rows(tm, HEAD_PAD)],
        out_specs=[_rows(tm, D_MODEL), _rows(tm, za), _rows(tm, hp), _rows(tm, 2 * hp), vec, vec, vec,
                   _full(g_q.shape), _full(g_kv.shape)],
        out_shape=[_sds((s, D_MODEL), F32), _sds((s, za), MXU), _sds((s, hp), MXU), _sds((s, 2 * hp), MXU),
                   _sds((1, D_MODEL), F32), _sds((1, D_MODEL), F32), _sds((1, D_MODEL), F32),
                   _sds(g_q.shape, F32), _sds(g_kv.shape, F32)],
    )(dq, dk, dv, z, d_guv, x, d_x_part, ada_raw, ada_b, g_pre, g_q, g_kv, w1t, wqt, wkv, cos_t, sin_t)


def _tn_matmul(a, b, name, ts):
    ga, s, m = a.shape
    gb, _, n = b.shape
    g = max(ga, gb)
    tn = n if n <= 1024 else 1024
    steps = s // ts

    def body(a_ref, b_ref, o_ref, acc):
        k = pl.program_id(2)

        @pl.when(k == 0)
        def _():
            acc[...] = jnp.zeros(acc.shape, F32)

        acc[...] += _dot_tn(a_ref[0], b_ref[0])

        @pl.when(k == steps - 1)
        def _():
            o_ref[0] = acc[...].astype(MXU)

    return _call(
        body, name=name, grid=(g, n // tn, steps), sem=("parallel", "parallel", "arbitrary"),
        in_specs=[pl.BlockSpec((1, ts, m), lambda gi, ni, k: (gi if ga > 1 else 0, k, 0)),
                  pl.BlockSpec((1, ts, tn), lambda gi, ni, k: (gi if gb > 1 else 0, k, ni))],
        out_specs=pl.BlockSpec((1, m, tn), lambda gi, ni, k: (gi, 0, ni)),
        out_shape=_sds((g, m, n), MXU),
        scratch=[pltpu.VMEM((m, tn), F32)],
    )(a, b)


def _adamw(w, g, m, v):
    m2 = ADAM_B1 * m + (1.0 - ADAM_B1) * g
    v2 = ADAM_B2 * v + (1.0 - ADAM_B2) * (g * g)
    m_hat = m2 / (1.0 - ADAM_B1 ** ADAM_STEP)
    v_hat = v2 / (1.0 - ADAM_B2 ** ADAM_STEP)
    delta = -ADAM_LR * (m_hat / (jnp.sqrt(v_hat) + ADAM_EPS) + ADAM_WD * w)
    return delta, m2, v2


def _adam_reduce(recv, w, m, v, name):
    r, c = w.shape
    tr = r if r <= 512 else max(t for t in range(16, 513, 16) if r % t == 0)

    def body(p_ref, w_ref, m_ref, v_ref, g_ref, d_ref, mo_ref, vo_ref):
        g = p_ref[0].astype(F32)
        for j in range(1, N_DEV):
            g = g + p_ref[j].astype(F32)
        g_ref[...] = g
        d_ref[...], mo_ref[...], vo_ref[...] = _adamw(w_ref[...], g, m_ref[...], v_ref[...])

    blk = pl.BlockSpec((tr, c), lambda i: (i, 0))
    return _call(
        body, name=name, grid=(r // tr,), sem=("parallel",),
        in_specs=[pl.BlockSpec((N_DEV, tr, c), lambda i: (0, i, 0)), blk, blk, blk],
        out_specs=[blk] * 4, out_shape=[_sds((r, c), F32)] * 4,
    )(recv, w, m, v)


def _adam_w_ada(c_act_t, d_ada_cols, w, m, v):
    r, c = w.shape
    tr = 256

    def body(ct_ref, da_ref, w_ref, m_ref, v_ref, g_ref, d_ref, mo_ref, vo_ref):
        g = ct_ref[:, pl.ds(0, 1)] * da_ref[pl.ds(0, 1), :]
        for b in range(1, N_DEV):
            g = g + ct_ref[:, pl.ds(b, 1)] * da_ref[pl.ds(b, 1), :]
        g_ref[...] = g
        d_ref[...], mo_ref[...], vo_ref[...] = _adamw(w_ref[...], g, m_ref[...], v_ref[...])

    blk = pl.BlockSpec((tr, c), lambda i: (i, 0))
    return _call(
        body, name="adam_w_ada", grid=(r // tr,), sem=("parallel",),
        in_specs=[pl.BlockSpec((tr, N_DEV), lambda i: (i, 0)), _full(d_ada_cols.shape), blk, blk, blk],
        out_specs=[blk] * 4, out_shape=[_sds((r, c), F32)] * 4,
    )(c_act_t, d_ada_cols, w, m, v)


VEC_ROWS = D_MODEL // 128
PK_ADA = 0
PK_GAIN = PK_ADA + 6 * VEC_ROWS
PK_GQ = PK_GAIN + 4 * VEC_ROWS
PK_GKV = PK_GQ + Q_LORA // 128
PK_LOSS = PK_GKV + KV_LORA // 128
PK_LNG = 88
PK_LNB = PK_LNG + HEADS
PK_BS = PK_LNB + HEADS
PK_CB = PK_BS + HEADS
CB_ROWS = 6
PK_WS = PK_CB + N_DEV * CB_ROWS
PK_ROWS = PK_WS + HEADS * GM_CHUNK
assert PK_LOSS < PK_LNG and PK_ROWS % 8 == 0
LATE_GAIN = 2 * VEC_ROWS
LATE_GQ = 3 * VEC_ROWS
LATE_GKV = LATE_GQ + Q_LORA // 128
LATE_ROWS = 32


def _cb_chunks():
    return [(k, k * 128, min(128, FF_BLK - k * 128)) for k in range(CB_ROWS)]


def _put_rows(out_ref, row0, ref, width):
    for k in range(width // 128):
        out_ref[pl.ds(row0 + k, 1), :] = ref[:, pl.ds(k * 128, 128)]


def _pack_small(ada_rows, gains, loss_part, d_ln_g, d_ln_b, d_bs, d_cb, d_ws):
    half = N_DEV // 2

    def body(*refs):
        vec_refs = refs[:7]
        loss_ref, lng_ref, lnb_ref, bs_ref, cb_ref, ws_ref, out_ref = refs[7:]
        out_ref[pl.ds(0, PK_WS), :] = jnp.zeros((PK_WS, 128), F32)
        for n, ref in enumerate(vec_refs[:4]):
            _put_rows(out_ref, PK_ADA + (2 + n) * VEC_ROWS, ref, D_MODEL)
        for n, ref in enumerate(vec_refs[4:]):
            _put_rows(out_ref, PK_GAIN + (1 + n) * VEC_ROWS, ref, D_MODEL)
        _put_rows(out_ref, PK_LOSS, loss_ref, 128)
        out_ref[pl.ds(PK_LNG, HEADS), pl.ds(0, GM_DIM)] = lng_ref[...]
        out_ref[pl.ds(PK_LNB, HEADS), pl.ds(0, GM_DIM)] = lnb_ref[...]
        out_ref[pl.ds(PK_BS, HEADS), :] = bs_ref[...]
        for j in range(N_DEV):
            for k, lane, width in _cb_chunks():
                out_ref[pl.ds(PK_CB + j * CB_ROWS + k, 1), pl.ds(0, width)] = cb_ref[j % half, j // half, :, pl.ds(lane, width)]
        for h in range(HEADS):
            out_ref[pl.ds(PK_WS + h * GM_CHUNK, GM_CHUNK), :] = ws_ref[h]

    ins = list(ada_rows) + list(gains) + [loss_part, d_ln_g, d_ln_b, d_bs, d_cb, d_ws]
    return _call(body, name="pack_small", grid=(1,), in_specs=[_full(a.shape) for a in ins],
                 out_specs=_full((PK_ROWS, 128)), out_shape=_sds((PK_ROWS, 128), F32))(*ins)


def _pack_late(d_sh1, d_sc1, d_g_pre_mix, d_g_q, d_g_kv):
    def body(sh_ref, sc_ref, g_ref, gq_ref, gkv_ref, out_ref):
        out_ref[...] = jnp.zeros((LATE_ROWS, 128), F32)
        _put_rows(out_ref, 0, sh_ref, D_MODEL)
        _put_rows(out_ref, VEC_ROWS, sc_ref, D_MODEL)
        _put_rows(out_ref, LATE_GAIN, g_ref, D_MODEL)
        _put_rows(out_ref, LATE_GQ, gq_ref, Q_LORA)
        _put_rows(out_ref, LATE_GKV, gkv_ref, KV_LORA)

    ins = [d_sh1, d_sc1, d_g_pre_mix, d_g_q, d_g_kv]
    return _call(body, name="pack_late", grid=(1,), in_specs=[_full(a.shape) for a in ins],
                 out_specs=_full((LATE_ROWS, 128)), out_shape=_sds((LATE_ROWS, 128), F32))(*ins)


def _adam_small(gathered, late, params):
    n_par = len(params)

    def body(p_ref, late_ref, *refs):
        ins = [refs[3 * n:3 * n + 3] for n in range(n_par)]
        outs = [refs[3 * n_par + 4 * n:3 * n_par + 4 * n + 4] for n in range(n_par)]
        loss_ref, dada_ref = refs[7 * n_par:]

        def total(rows, lanes=slice(None), src=p_ref):
            g = src[0, rows, lanes]
            for j in range(1, N_DEV):
                g = g + src[j, rows, lanes]
            return g

        def apply(n, g, idx):
            w_ref, m_ref, v_ref = ins[n]
            d, m2, v2 = _adamw(w_ref[idx], g, m_ref[idx], v_ref[idx])
            for ref, val in zip(outs[n], (g, d, m2, v2)):
                ref[idx] = val

        def vector(n, src, row0, width, lane0=0):
            for k in range(width // 128):
                apply(n, total(pl.ds(row0 + k, 1), src=src), (slice(None), pl.ds(lane0 + k * 128, 128)))

        vector(0, late_ref, 0, 2 * D_MODEL)
        vector(0, p_ref, PK_ADA + 2 * VEC_ROWS, 4 * D_MODEL, lane0=2 * D_MODEL)
        vector(1, late_ref, LATE_GAIN, D_MODEL)
        for n in range(1, 4):
            vector(1 + n, p_ref, PK_GAIN + n * VEC_ROWS, D_MODEL)
        vector(5, late_ref, LATE_GQ, Q_LORA)
        vector(6, late_ref, LATE_GKV, KV_LORA)
        apply(7, total(pl.ds(PK_LNG, HEADS), pl.ds(0, GM_DIM)), (0,))
        apply(8, total(pl.ds(PK_LNB, HEADS), pl.ds(0, GM_DIM)), (0,))
        for h in range(HEADS):
            apply(9, total(pl.ds(PK_WS + h * GM_CHUNK, GM_CHUNK)), (0, h))
        apply(10, total(pl.ds(PK_BS, HEADS)), (0,))
        for j in range(N_DEV):
            for k, lane, width in _cb_chunks():
                apply(11, total(pl.ds(PK_CB + j * CB_ROWS + k, 1), pl.ds(0, width)), (pl.ds(j, 1), pl.ds(lane, width)))
        loss_ref[...] = total(pl.ds(PK_LOSS, 1))
        dada_ref[:, pl.ds(0, 2 * VEC_ROWS), :] = late_ref[:, pl.ds(0, 2 * VEC_ROWS), :]
        dada_ref[:, pl.ds(2 * VEC_ROWS, 4 * VEC_ROWS), :] = p_ref[:, pl.ds(PK_ADA + 2 * VEC_ROWS, 4 * VEC_ROWS), :]

    flat = [a for triple in params for a in triple]
    out_shape = [_sds(w.shape, F32) for w, _, _ in params for _ in range(4)]
    out_shape += [_sds((1, 128), F32), _sds((N_DEV, 6 * VEC_ROWS, 128), F32)]
    outs = _call(body, name="adam_small", grid=(1,),
                 in_specs=[_full(gathered.shape), _full(late.shape)] + [_full(a.shape) for a in flat],
                 out_specs=[_full(o.shape) for o in out_shape], out_shape=out_shape)(gathered, late, *flat)
    return [tuple(outs[4 * n:4 * n + 4]) for n in range(n_par)], outs[-2], outs[-1]


def _rope_tables(s):
    pos = jnp.arange(s, dtype=F32)
    inv = ROPE_THETA ** (-jnp.arange(0, ROPE, 2, dtype=F32) / ROPE)
    lane_inv = jnp.concatenate([jnp.zeros((NOPE,), F32), inv, inv, jnp.zeros((HEAD_PAD - NOPE - ROPE,), F32)])
    ang = pos[:, None] * lane_inv[None, :]
    return jnp.cos(ang), jnp.sin(ang)


def kernel(x, c, w_ada, b_ada, g_pre_mix, g_post_mix, w_in, g_q, w_uq, g_kv, w_ukv, gm_ln_g, gm_ln_b, w_spatial, b_spatial, w_out, g_pre_ffn, g_post_ffn, w_up, conv_w, conv_b, w_down, loss_target, m_w_ada, m_b_ada, m_g_pre_mix, m_g_post_mix, m_w_in, m_g_q, m_w_uq, m_g_kv, m_w_ukv, m_gm_ln_g, m_gm_ln_b, m_w_spatial, m_b_spatial, m_w_out, m_g_pre_ffn, m_g_post_ffn, m_w_up, m_conv_w, m_conv_b, m_w_down, v_w_ada, v_b_ada, v_g_pre_mix, v_g_post_mix, v_w_in, v_g_q, v_w_uq, v_g_kv, v_w_ukv, v_gm_ln_g, v_gm_ln_b, v_w_spatial, v_b_spatial, v_w_out, v_g_pre_ffn, v_g_post_ffn, v_w_up, v_conv_w, v_conv_b, v_w_down):
    s = x.shape[1]
    tm = min(256, s)
    tf = min(2 * ROW_SUB, s)
    tq = min(512, s)
    ts = min(2048, s)
    hp = HEADS * HEAD_PAD
    half = N_DEV // 2
    my_slot = 4 * lax.axis_index("x") + 2 * lax.axis_index("y") + lax.axis_index("c")
    x2d, target = x[0], loss_target[0]

    def t_(a):
        return jnp.swapaxes(a[0], 0, 1)

    w_in_t, m_in_t, v_in_t = t_(w_in), t_(m_w_in), t_(v_w_in)
    w_uq_t, m_uq_t, v_uq_t = t_(w_uq), t_(m_w_uq), t_(v_w_uq)
    w_up_t, m_up_t, v_up_t = t_(w_up), t_(m_w_up), t_(v_w_up)
    (g_c, g_in_t, g_uq_t, g_ukv, g_cw), _ = _exchange(
        [c, w_in_t.astype(MXU), w_uq_t.astype(MXU), w_ukv[0].astype(MXU), conv_w[0]], [], "gather_mixer_weights")

    w_in_f = g_in_t.reshape(-1, D_MODEL)
    o1, o2, o3 = Q_LORA, Q_LORA + KV_LORA, Q_LORA + KV_LORA + ROPE
    w1t = jnp.concatenate([w_in_f[:o2], jnp.zeros((NOPE, D_MODEL), MXU), w_in_f[o2:o3],
                           jnp.zeros((HEAD_PAD - NOPE - ROPE, D_MODEL), MXU), w_in_f[o3:]], axis=0)
    wqt = jnp.pad(g_uq_t, ((0, 0), (0, HEAD_PAD - NOPE - ROPE), (0, 0))).reshape(hp, Q_LORA)
    w_ukv_f = jnp.transpose(g_ukv, (1, 0, 2)).reshape(KV_LORA, HEADS, 2 * NOPE)
    pad_head = ((0, 0), (0, 0), (0, HEAD_PAD - NOPE))
    wkv = jnp.concatenate([jnp.pad(w_ukv_f[:, :, :NOPE], pad_head).reshape(KV_LORA, hp),
                           jnp.pad(w_ukv_f[:, :, NOPE:], pad_head).reshape(KV_LORA, hp)], axis=1)
    cb8 = conv_b.reshape(N_DEV, 1, FF_BLK)
    bias_exp = jnp.repeat(b_spatial[0].T, GM_DIM, axis=1)
    ln_g, ln_b = gm_ln_g.reshape(1, GM_WIDTH), gm_ln_b.reshape(1, GM_WIDTH)
    w_sp = w_spatial[0]
    cos_t, sin_t = _rope_tables(s)

    ada_part, c_act = _ada_fwd(g_c.reshape(N_DEV, D_MODEL), w_ada[0])
    _, (ada_recv,) = _exchange([], [(ada_part.reshape(N_DEV, 1, -1), _plain_slot)], "ada_rows")
    ada_raw = ada_recv.reshape(6, D_MODEL)
    ada_b = b_ada.reshape(6, D_MODEL)

    h1, z, qp, kp, vp, cqn, ckvn = _mix_in_fwd(x2d, ada_raw, ada_b, g_pre_mix, w1t, g_q, g_kv, wqt, wkv, cos_t, sin_t, tm)
    sgu = _gmlp_fwd(z, ln_g, ln_b, w_sp, bias_exp, tm)
    o_pad, lse, (g_out, g_up, g_down) = _attn_fwd(
        qp, kp, vp, tq, [w_out[0].astype(MXU), w_up_t.astype(MXU), w_down[0].astype(MXU)])
    w_out_f = g_out.reshape(2 * GM_WIDTH, D_MODEL)
    wo_attn = jnp.pad(w_out_f[:GM_WIDTH].reshape(HEADS, NOPE, D_MODEL), ((0, 0), (0, HEAD_PAD - NOPE), (0, 0)))
    wo = jnp.concatenate([wo_attn.reshape(hp, D_MODEL), w_out_f[GM_WIDTH:]], axis=0)
    wd = g_down.reshape(half, FF_BLK, D_MODEL)
    m_mix, x2, h2 = _out_proj_fwd(o_pad, sgu, wo, x2d, ada_raw, ada_b, g_post_mix, g_pre_ffn, tm)
    up_a, up_b, y_a, y_b, act = _ffn_up_fwd(h2, jnp.swapaxes(g_up, 1, 2), g_cw, cb8, tf)
    d_out, d_f, loss_part, d_gt2, d_g_post_ffn = _ffn_down_fwd(act, wd, x2, target, ada_raw, ada_b, g_post_ffn, tf)

    d_up, d_cw, d_cb, p_down, p_up = _ffn_down_bwd(d_f, wd, up_a, up_b, y_a, y_b, g_cw, act, h2, tf)
    p_down = p_down.reshape(N_DEV, -1, D_MODEL)
    d_x2, d_m, d_sh2, d_sc2, d_g_pre_ffn, d_gt1, d_g_post_mix = _ffn_up_bwd(
        d_up, g_up, x2, m_mix, d_out, ada_raw, ada_b, g_pre_ffn, g_post_mix, tm)
    d_m3 = d_m[None]
    dwo_attn = _tn_matmul(o_pad[None], d_m3, "dw_out_attn", ts)[0].reshape(HEADS, HEAD_PAD, D_MODEL)[:, :NOPE]
    dwo_sgu = _tn_matmul(sgu[None], d_m3, "dw_out_sgu", ts)[0]
    p_out = jnp.concatenate([dwo_attn.reshape(GM_WIDTH, D_MODEL), dwo_sgu], axis=0).reshape(N_DEV, -1, D_MODEL)
    d_o, d_sgu, delta = _out_proj_bwd(d_m, wo, o_pad, tm)
    d_guv, d_ws, d_bs, d_ln_g, d_ln_b = _gmlp_bwd(z, d_sgu, ln_g, ln_b, w_sp, bias_exp, tm)
    packed = _pack_small([d_gt1, d_sh2, d_sc2, d_gt2], [d_g_post_mix, d_g_pre_ffn, d_g_post_ffn], loss_part,
                         d_ln_g, d_ln_b, d_bs, d_cb, d_ws)

    def ffn_slot(j):
        return (j % half, j // half)

    dq, dk, dv, (r_out, r_up, r_down, r_cw), (g_small,) = _attn_bwd(
        qp, kp, vp, d_o, lse, delta, tq,
        [(p_out, _plain_slot), (p_up, ffn_slot), (p_down, _plain_slot), (d_cw, ffn_slot)], [packed])
    grad_x, d_za, d_qp, d_kvp, d_sh1, d_sc1, d_g_pre_mix, d_g_q, d_g_kv = _mix_in_bwd(
        dq, dk, dv, z, d_guv, x2d, d_x2, ada_raw, ada_b, g_pre_mix, g_q, g_kv, w1t, wqt, wkv, cos_t, sin_t, tm)
    h1_3 = h1[None]
    dw1a = _tn_matmul(d_za[None], h1_3, "dw_in_a", ts)[0]
    dw1b = _tn_matmul(d_guv[None], h1_3, "dw_in_b", ts)[0]
    d_w_in_t = jnp.concatenate([dw1a[:o2], dw1a[o2 + NOPE:o2 + NOPE + ROPE], dw1b], axis=0)
    p_in = d_w_in_t.reshape(N_DEV, -1, D_MODEL)
    p_uq = _tn_matmul(d_qp[None], cqn[None], "dw_uq", ts)[0].reshape(HEADS, HEAD_PAD, Q_LORA)[:, :NOPE + ROPE]
    dwkv = _tn_matmul(ckvn[None], d_kvp[None], "dw_ukv", ts)[0]
    dwk = dwkv[:, :hp].reshape(KV_LORA, HEADS, HEAD_PAD)[:, :, :NOPE]
    dwv = dwkv[:, hp:].reshape(KV_LORA, HEADS, HEAD_PAD)[:, :, :NOPE]
    p_ukv = jnp.transpose(jnp.concatenate([dwk, dwv], axis=2), (1, 0, 2))

    (g_late,), (r_in, r_uq, r_ukv) = _exchange(
        [_pack_late(d_sh1, d_sc1, d_g_pre_mix, d_g_q, d_g_kv)],
        [(p_in, _plain_slot), (p_uq, _plain_slot), (p_ukv, _plain_slot)], "final_exchange")
    small_params = [(b_ada, m_b_ada, v_b_ada), (g_pre_mix, m_g_pre_mix, v_g_pre_mix),
                    (g_post_mix, m_g_post_mix, v_g_post_mix), (g_pre_ffn, m_g_pre_ffn, v_g_pre_ffn),
                    (g_post_ffn, m_g_post_ffn, v_g_post_ffn), (g_q, m_g_q, v_g_q), (g_kv, m_g_kv, v_g_kv),
                    (gm_ln_g, m_gm_ln_g, v_gm_ln_g), (gm_ln_b, m_gm_ln_b, v_gm_ln_b),
                    (w_spatial, m_w_spatial, v_w_spatial), (b_spatial, m_b_spatial, v_b_spatial),
                    tuple(a.reshape(N_DEV, FF_BLK) for a in (conv_b, m_conv_b, v_conv_b))]
    small_out, loss_row, d_ada_all = _adam_small(g_small, g_late, small_params)
    small_out[11] = tuple(o.reshape(conv_b.shape) for o in small_out[11])
    loss = loss_row[0, 0]

    def big(recv, w, m, v, name):
        g, d, m2, v2 = _adam_reduce(recv, w[0], m[0], v[0], name)
        return g[None], d[None], m2[None], v2[None]

    def big_t(recv, w_t, m_t, v_t, name):
        return tuple(jnp.swapaxes(o, 0, 1)[None] for o in _adam_reduce(recv, w_t, m_t, v_t, name))

    a_in = big_t(r_in, w_in_t, m_in_t, v_in_t, "adam_w_in")
    a_uq = big_t(r_uq, w_uq_t, m_uq_t, v_uq_t, "adam_w_uq")
    a_ukv = big(r_ukv, w_ukv, m_w_ukv, v_w_ukv, "adam_w_ukv")
    a_out = big(r_out, w_out, m_w_out, v_w_out, "adam_w_out")
    a_up = big_t(r_up, w_up_t, m_up_t, v_up_t, "adam_w_up")
    a_down = big(r_down, w_down, m_w_down, v_w_down, "adam_w_down")
    ada_cols = w_ada.shape[2]
    d_ada_cols = lax.dynamic_slice(d_ada_all.reshape(N_DEV, 6 * D_MODEL), (0, my_slot * ada_cols), (N_DEV, ada_cols))
    a_ada = tuple(t[None] for t in _adam_w_ada(c_act.T, d_ada_cols, w_ada[0], m_w_ada[0], v_w_ada[0]))
    a_cw = big(r_cw, conv_w, m_conv_w, v_conv_w, "adam_conv_w")

    def small(k):
        return small_out[k]

    per_weight = [a_ada, small(0), small(1), small(2), a_in, small(5), a_uq, small(6), a_ukv, small(7), small(8),
                  small(9), small(10), a_out, small(3), small(4), a_up, a_cw, small(11), a_down]
    outs = [loss, grad_x[None]]
    for k in range(4):
        outs += [t[k] for t in per_weight]
    return tuple(outs)
```

```python
import functools

import jax
import jax.numpy as jnp
from jax import lax
from jax.experimental import pallas as pl
from jax.experimental.pallas import tpu as pltpu

F32 = jnp.float32
MXU = jnp.bfloat16

N_DEV = 8
D_MODEL = 1024
HEADS = 8
HEAD_PAD = 128
NOPE = 64
ROPE = 32
Q_LORA = 256
KV_LORA = 128
GM_WIDTH = 512
GM_DIM = 64
GM_CHUNK = 128
CHUNK_SHIFT = 6
ROPE_THETA = 10000.0
ATTN_SCALE = (NOPE + ROPE) ** -0.5
LOG2E = 1.4426950408889634
SCALE_LOG2E = ATTN_SCALE * LOG2E
Z_COLS = 1536
FF_BLK = 704
EPS = 1e-6
ADAM_LR = 0.001
ADAM_B1 = 0.9
ADAM_B2 = 0.999
ADAM_EPS = 1e-08
ADAM_WD = 0.01
ADAM_STEP = 10
VMEM_LIMIT = 56 * 1024 * 1024
MESH = pl.DeviceIdType.MESH


def _dot(a, b):
    return jnp.dot(a, b, preferred_element_type=F32)


def _dot_nt(a, b):
    return lax.dot_general(a, b, (((1,), (1,)), ((), ())), preferred_element_type=F32)


def _dot_tn(a, b):
    return lax.dot_general(a, b, (((0,), (0,)), ((), ())), preferred_element_type=F32)


def _call(body, *, name, grid, in_specs, out_specs, out_shape, scratch=(), sem=None):
    params = pltpu.CompilerParams(dimension_semantics=sem, vmem_limit_bytes=VMEM_LIMIT)
    return pl.pallas_call(body, name=name, grid=grid, in_specs=in_specs, out_specs=out_specs,
                          out_shape=out_shape, scratch_shapes=list(scratch), compiler_params=params)


def _full(shape):
    n = len(shape)
    return pl.BlockSpec(shape, lambda *_: (0,) * n)


def _rows(tm, cols, col_block=0):
    return pl.BlockSpec((tm, cols), lambda i: (i, col_block))


def _sds(shape, dtype):
    return jax.ShapeDtypeStruct(shape, dtype)


def _row(ref, k):
    return ref[pl.ds(k, 1), :]


def _rms(x):
    r = lax.rsqrt(jnp.mean(x * x, axis=-1, keepdims=True) + EPS)
    return x * r, r


def _rms_bwd(d_hat, hat, r):
    return r * (d_hat - hat * jnp.mean(d_hat * hat, axis=-1, keepdims=True))


def _rope_partner(t):
    lane = lax.broadcasted_iota(jnp.int32, t.shape, 1)
    swapped = jnp.where(lane < NOPE + ROPE // 2, -pltpu.roll(t, HEAD_PAD - ROPE // 2, 1), pltpu.roll(t, ROPE // 2, 1))
    return jnp.where((lane >= NOPE) & (lane < NOPE + ROPE), swapped, 0.0)


def _rope(t, cos, sin):
    return t * cos + _rope_partner(t) * sin


def _rope_transposed(g, cos, sin):
    return g * cos - _rope_partner(g * sin)


def _gelu(x):
    return x * (0.5 * (1.0 + jnp.tanh(0.7978845608028654 * (x + 0.044715 * (x * x * x)))))


def _gelu_grad(x):
    t = jnp.tanh(0.7978845608028654 * (x + 0.044715 * (x * x * x)))
    return 0.5 * (1.0 + t) + 0.5 * x * (1.0 - t * t) * (0.7978845608028654 * (1.0 + 3.0 * 0.044715 * (x * x)))


def _split_dot(x, mat):
    hi = x.astype(MXU)
    lo = (x - hi.astype(F32)).astype(MXU)
    return _dot(hi, mat) + _dot(lo, mat)


def _split_dot3(x, mat):
    hi = x.astype(MXU)
    r1 = x - hi.astype(F32)
    mid = r1.astype(MXU)
    lo = (r1 - mid.astype(F32)).astype(MXU)
    return (_dot(hi, mat) + _dot(mid, mat)) + _dot(lo, mat)


def _seg_matrix():
    r = lax.broadcasted_iota(jnp.int32, (GM_WIDTH, GM_WIDTH), 0) >> 6
    c = lax.broadcasted_iota(jnp.int32, (GM_WIDTH, GM_WIDTH), 1) >> 6
    return jnp.where(r == c, 1.0 / GM_DIM, 0.0).astype(MXU)


def _spatial_mask():
    i = lax.broadcasted_iota(jnp.int32, (GM_CHUNK, GM_CHUNK), 0) >> CHUNK_SHIFT
    j = lax.broadcasted_iota(jnp.int32, (GM_CHUNK, GM_CHUNK), 1) >> CHUNK_SHIFT
    return (j <= i).astype(F32)


def _my_place():
    return lax.axis_index("x"), lax.axis_index("y"), lax.axis_index("c")


def _flat(p):
    return 4 * p[0] + 2 * p[1] + p[2]


def _comm_sems(n):
    return [pltpu.SemaphoreType.DMA((7 * n,)), pltpu.SemaphoreType.DMA((7 * n,)), pltpu.SemaphoreType.DMA((n,))]


def _gather_steps(ins, outs, sems):
    send_sems, recv_sems, local_sems = sems
    n = len(ins)
    x, y, c = _my_place()
    me, sibling = (x, y, c), (x, y, 1 - c)
    chips = [(1 - x, y), (x, 1 - y), (1 - x, 1 - y)]

    def copy(a, k, block, to, src=None):
        slot = outs[a].at[_flat(block)]
        return pltpu.make_async_remote_copy(
            src_ref=slot if src is None else src, dst_ref=slot,
            send_sem=send_sems.at[7 * a + k], recv_sem=recv_sems.at[7 * a + k],
            device_id=to, device_id_type=MESH)

    def mine():
        return [pltpu.make_async_copy(ins[a], outs[a].at[_flat(me)], local_sems.at[a]) for a in range(n)]

    def first():
        cps = []
        for a in range(n):
            cps.append(copy(a, 0, me, sibling, src=ins[a]))
            cps += [copy(a, 1 + j, me, (*chip, c), src=ins[a]) for j, chip in enumerate(chips)]
        return cps

    def passed():
        return [copy(a, 4 + j, (*chip, c), sibling) for a in range(n) for j, chip in enumerate(chips)]

    def start():
        for cp in mine() + first():
            cp.start()

    def forward():
        for a in range(n):
            for j, chip in enumerate(chips):
                copy(a, 1 + j, (*chip, c), me).wait_recv()
                copy(a, 4 + j, (*chip, c), sibling).start()

    def finish():
        for a in range(n):
            copy(a, 0, sibling, me).wait_recv()
            for j, chip in enumerate(chips):
                copy(a, 4 + j, (*chip, 1 - c), me).wait_recv()
        for cp in first() + passed():
            cp.wait_send()
        for cp in mine():
            cp.wait()

    return start, forward, finish


def _scatter_steps(ins, outs, sems, slots):
    send_sems, recv_sems, local_sems = sems
    n = len(ins)
    flips = [(fx, fy, fc) for fx in (0, 1) for fy in (0, 1) for fc in (0, 1)][1:]
    me = _my_place()

    def peer(f):
        return tuple(1 - v if b else v for v, b in zip(me, f))

    def copy(a, k, arriving=False):
        p = peer(flips[k])
        return pltpu.make_async_remote_copy(
            src_ref=ins[a].at[slots[a](_flat(p))], dst_ref=outs[a].at[_flat(p if arriving else me)],
            send_sem=send_sems.at[7 * a + k], recv_sem=recv_sems.at[7 * a + k],
            device_id=p, device_id_type=MESH)

    def mine():
        return [pltpu.make_async_copy(ins[a].at[slots[a](_flat(me))], outs[a].at[_flat(me)], local_sems.at[a])
                for a in range(n)]

    def start():
        for cp in mine() + [copy(a, k) for a in range(n) for k in range(7)]:
            cp.start()

    def finish():
        for a in range(n):
            for k in range(7):
                copy(a, k, arriving=True).wait_recv()
        for a in range(n):
            for k in range(7):
                copy(a, k).wait_send()
        for cp in mine():
            cp.wait()

    return start, finish


def _plain_slot(j):
    return (j,)


def _scatter_out_shape(arr, slot):
    return _sds((N_DEV,) + arr.shape[len(slot(0)):], arr.dtype)


def _exchange(gathered, scattered, name):
    ng, ns = len(gathered), len(scattered)
    slots = [slot for _, slot in scattered]

    def body(*refs):
        g_in, s_in = refs[:ng], refs[ng:ng + ns]
        g_out, s_out = refs[ng + ns:2 * ng + ns], refs[2 * ng + ns:2 * (ng + ns)]
        sems = refs[2 * (ng + ns):]
        g_start, g_forward, g_finish = _gather_steps(g_in, g_out, sems[:3])
        s_start, s_finish = _scatter_steps(s_in, s_out, sems[3:], slots)
        g_start()
        s_start()
        g_forward()
        g_finish()
        s_finish()

    any_spec = pl.BlockSpec(memory_space=pl.ANY)
    outs = pl.pallas_call(
        body, name=name,
        in_specs=[any_spec] * (ng + ns), out_specs=[any_spec] * (ng + ns),
        out_shape=[_sds((N_DEV,) + a.shape, a.dtype) for a in gathered]
        + [_scatter_out_shape(a, slot) for a, slot in scattered],
        scratch_shapes=_comm_sems(max(ng, 1)) + _comm_sems(max(ns, 1)),
    )(*gathered, *[a for a, _ in scattered])
    return outs[:ng], outs[ng:]


def _ada_fwd(c_all, w_ada):
    def body(c_ref, w_ref, part_ref, act_ref):
        cv = c_ref[...]
        act = cv * jax.nn.sigmoid(cv)
        act_ref[...] = act
        part_ref[...] = _dot(act.astype(MXU), w_ref[...].astype(MXU))

    cols = w_ada.shape[1]
    return _call(body, name="ada_fwd", grid=(1,),
                 in_specs=[_full(c_all.shape), _full(w_ada.shape)],
                 out_specs=[_full((N_DEV, cols)), _full(c_all.shape)],
                 out_shape=[_sds((N_DEV, cols), F32), _sds(c_all.shape, F32)])(c_all, w_ada)


def _mix_in_fwd(x, ada_raw, ada_b, g_pre, w1, g_q, g_kv, wq, wkv, cos_t, sin_t, tm):
    s = x.shape[0]

    def body(x_ref, ar_ref, ab_ref, g_ref, w1_ref, gq_ref, gkv_ref, wq_ref, wkv_ref, cos_ref, sin_ref,
             h1_ref, z_ref, qp_ref, kp_ref, vp_ref, cqn_ref, ckvn_ref):
        sh = _row(ar_ref, 0) + _row(ab_ref, 0)
        sc = _row(ar_ref, 1) + _row(ab_ref, 1)
        xn, _ = _rms(x_ref[...])
        hb = ((xn * g_ref[...]) * (1.0 + sc) + sh).astype(MXU)
        h1_ref[...] = hb
        z = _dot_nt(hb, w1_ref[...])
        z_ref[...] = z
        cos, sin = cos_ref[...], sin_ref[...]
        cqn = (_rms(z[:, :Q_LORA])[0] * gq_ref[...]).astype(MXU)
        ckvn = (_rms(z[:, Q_LORA:Q_LORA + KV_LORA])[0] * gkv_ref[...]).astype(MXU)
        cqn_ref[...] = cqn
        ckvn_ref[...] = ckvn
        q = _dot_nt(cqn, wq_ref[...])
        kv = _dot(ckvn, wkv_ref[...])
        k_rope = _rope(z[:, Q_LORA + KV_LORA:Q_LORA + KV_LORA + HEAD_PAD], cos, sin)
        for h in range(HEADS):
            blk = slice(h * HEAD_PAD, (h + 1) * HEAD_PAD)
            qp_ref[:, blk] = _rope(q[:, blk], cos, sin).astype(MXU)
            kp_ref[:, blk] = (kv[:, blk] + k_rope).astype(MXU)
        v_lane = lax.broadcasted_iota(jnp.int32, (tm, HEADS * HEAD_PAD), 1) & (HEAD_PAD - 1)
        vp_ref[...] = jnp.where(v_lane == NOPE, 1.0, kv[:, HEADS * HEAD_PAD:]).astype(MXU)

    hp = HEADS * HEAD_PAD
    return _call(
        body, name="mix_in_fwd", grid=(s // tm,), sem=("parallel",),
        in_specs=[_rows(tm, D_MODEL), _full(ada_raw.shape), _full(ada_b.shape), _full(g_pre.shape), _full(w1.shape),
                  _full(g_q.shape), _full(g_kv.shape), _full(wq.shape), _full(wkv.shape),
                  _rows(tm, HEAD_PAD), _rows(tm, HEAD_PAD)],
        out_specs=[_rows(tm, D_MODEL), _rows(tm, Z_COLS), _rows(tm, hp), _rows(tm, hp), _rows(tm, hp),
                   _rows(tm, Q_LORA), _rows(tm, KV_LORA)],
        out_shape=[_sds((s, D_MODEL), MXU), _sds((s, Z_COLS), F32), _sds((s, hp), MXU), _sds((s, hp), MXU),
                   _sds((s, hp), MXU), _sds((s, Q_LORA), MXU), _sds((s, KV_LORA), MXU)],
    )(x, ada_raw, ada_b, g_pre, w1, g_q, g_kv, wq, wkv, cos_t, sin_t)


def _gm_norm(zv, seg):
    gv = _gelu(zv)
    cen = gv - _split_dot(gv, seg)
    rstd = lax.rsqrt(_split_dot(cen * cen, seg) + EPS)
    return gv, cen * rstd, rstd


def _gm_pairs(rows):
    first = lax.broadcasted_iota(jnp.int32, (rows, 2 * GM_DIM), 1) < GM_DIM
    return [(slice(p * 2 * GM_DIM, (p + 1) * 2 * GM_DIM), first) for p in range(HEADS // 2)]


def _gm_mix(wm, vb, rows, transposed=False):
    dot = _dot_tn if transposed else _dot
    return jnp.concatenate([jnp.where(first, dot(wm[2 * p], vb[:, lanes]), dot(wm[2 * p + 1], vb[:, lanes]))
                            for p, (lanes, first) in enumerate(_gm_pairs(rows))], axis=1)


def _gmlp_fwd(z, ln_g, ln_b, w_sp, bias_exp, tm):
    s = z.shape[0]
    nblk = tm // GM_CHUNK

    def body(zu_ref, zv_ref, lg_ref, lb_ref, w_ref, be_ref, sgu_ref):
        seg = _seg_matrix()
        mask = _spatial_mask()
        wm = [(w_ref[h] * mask).astype(MXU) for h in range(HEADS)]
        gu = _gelu(zu_ref[...])
        _, vhat, _ = _gm_norm(zv_ref[...], seg)
        vln = (vhat * lg_ref[...] + lb_ref[...]).astype(MXU)
        for n in range(nblk):
            rows = slice(n * GM_CHUNK, (n + 1) * GM_CHUNK)
            mixed = _gm_mix(wm, vln[rows], GM_CHUNK) + be_ref[...]
            sgu_ref[rows, :] = (gu[rows] * mixed).astype(MXU)

    return _call(
        body, name="gmlp_fwd", grid=(s // tm,), sem=("parallel",),
        in_specs=[_rows(tm, GM_WIDTH, 1), _rows(tm, GM_WIDTH, 2), _full(ln_g.shape), _full(ln_b.shape),
                  _full(w_sp.shape), _full(bias_exp.shape)],
        out_specs=_rows(tm, GM_WIDTH), out_shape=_sds((s, GM_WIDTH), MXU),
    )(z, z, ln_g, ln_b, w_sp, bias_exp)


def _chunk_mask(n_q, n_k, q_off):
    qc = (q_off + lax.broadcasted_iota(jnp.int32, (n_q, n_k), 0)) >> CHUNK_SHIFT
    kc = lax.broadcasted_iota(jnp.int32, (n_q, n_k), 1) >> CHUNK_SHIFT
    return kc <= qc


NEG_BIG = -1e30
ATTN_HEADS_PER_STEP = 2


def _attn_fwd(qp, kp, vp, tq, gathered):
    s = qp.shape[0]
    nq = s // tq
    hb = ATTN_HEADS_PER_STEP
    groups = HEADS // hb
    width = hb * HEAD_PAD
    ng = len(gathered)

    def body(q_ref, k_ref, v_ref, *rest):
        g_in, (o_ref, lse_ref), g_out = rest[:ng], rest[ng:ng + 2], rest[ng + 2:2 * ng + 2]
        m_sc, acc_sc = rest[2 * ng + 2:2 * ng + 4]
        g_start, g_forward, g_finish = _gather_steps(g_in, g_out, rest[2 * ng + 4:])
        g, i = pl.program_id(0), pl.program_id(1)
        pl.when((g == 0) & (i == 0))(g_start)
        pl.when((g == groups - 1) & (i == 0))(g_forward)
        m_sc[...] = jnp.full(m_sc.shape, NEG_BIG, F32)
        acc_sc[...] = jnp.zeros(acc_sc.shape, F32)

        def tile(j, masked):
            rows = pl.ds(pl.multiple_of(j * tq, tq), tq)
            for hh in range(hb):
                lanes = slice(hh * HEAD_PAD, (hh + 1) * HEAD_PAD)
                sc = _dot_nt(q_ref[:, lanes], k_ref[rows, lanes])
                if masked:
                    sc = jnp.where(_chunk_mask(tq, tq, 0), sc, NEG_BIG)
                blocks = [sc[:, b * 128:(b + 1) * 128] for b in range(tq // 128)]
                m_prev = m_sc[hh]
                m_tile = jnp.max(functools.reduce(jnp.maximum, blocks), axis=-1, keepdims=True)
                m_new = jnp.maximum(m_prev, m_tile)
                alpha = jnp.exp2((m_prev - m_new) * SCALE_LOG2E)
                p = jnp.concatenate([jnp.exp2((b - m_new) * SCALE_LOG2E) for b in blocks], axis=1).astype(MXU)
                acc_sc[hh] = alpha * acc_sc[hh] + _dot(p, v_ref[rows, lanes])
                m_sc[hh] = m_new

        def off_diagonal_pair(p, carry):
            tile(2 * p, False)
            tile(2 * p + 1, False)
            return carry

        lax.fori_loop(0, i // 2, off_diagonal_pair, 0)

        @pl.when(i % 2 == 1)
        def _():
            tile(i - 1, False)

        tile(i, True)
        for hh in range(hb):
            lanes = slice(hh * HEAD_PAD, (hh + 1) * HEAD_PAD)
            acc = acc_sc[hh]
            denom = acc[:, NOPE:NOPE + 1]
            o_ref[:, lanes] = (acc / denom).astype(MXU)
            lse_ref[hh] = m_sc[hh][:, :1] * SCALE_LOG2E + jnp.log(denom) * LOG2E
        pl.when((g == groups - 1) & (i == nq - 1))(g_finish)

    q_spec = pl.BlockSpec((tq, width), lambda g, i: (i, g))
    kv_spec = pl.BlockSpec((s, width), lambda g, i: (0, g))
    any_spec = pl.BlockSpec(memory_space=pl.ANY)
    outs = _call(
        body, name="attn_fwd", grid=(groups, nq), sem=("arbitrary", "arbitrary"),
        in_specs=[q_spec, kv_spec, kv_spec] + [any_spec] * ng,
        out_specs=[q_spec, pl.BlockSpec((hb, tq, 1), lambda g, i: (g, i, 0))] + [any_spec] * ng,
        out_shape=[_sds(qp.shape, MXU), _sds((HEADS, s, 1), F32)]
        + [_sds((N_DEV,) + a.shape, a.dtype) for a in gathered],
        scratch=[pltpu.VMEM((hb, tq, HEAD_PAD), F32), pltpu.VMEM((hb, tq, HEAD_PAD), F32)] + _comm_sems(ng),
    )(qp, kp, vp, *gathered)
    return outs[0], outs[1], outs[2:]


def _out_proj_fwd(o_pad, sgu, wo, x, ada_raw, ada_b, g_post_mix, g_pre_ffn, tm):
    s = x.shape[0]
    hp = HEADS * HEAD_PAD

    def body(o_ref, sgu_ref, wo_ref, x_ref, ar_ref, ab_ref, gpm_ref, gpf_ref, m_ref, x2_ref, h2_ref):
        gt1 = _row(ar_ref, 2) + _row(ab_ref, 2)
        sh2 = _row(ar_ref, 3) + _row(ab_ref, 3)
        sc2 = _row(ar_ref, 4) + _row(ab_ref, 4)
        m = _dot(o_ref[...], wo_ref[pl.ds(0, hp), :]) + _dot(sgu_ref[...], wo_ref[pl.ds(hp, GM_WIDTH), :])
        m_ref[...] = m
        x2 = x_ref[...] + gt1 * (_rms(m)[0] * gpm_ref[...])
        x2_ref[...] = x2
        h2_ref[...] = ((_rms(x2)[0] * gpf_ref[...]) * (1.0 + sc2) + sh2).astype(MXU)

    return _call(
        body, name="out_proj_fwd", grid=(s // tm,), sem=("parallel",),
        in_specs=[_rows(tm, hp), _rows(tm, GM_WIDTH), _full(wo.shape), _rows(tm, D_MODEL), _full(ada_raw.shape),
                  _full(ada_b.shape), _full(g_post_mix.shape), _full(g_pre_ffn.shape)],
        out_specs=[_rows(tm, D_MODEL)] * 3,
        out_shape=[_sds((s, D_MODEL), F32), _sds((s, D_MODEL), F32), _sds((s, D_MODEL), MXU)],
    )(o_pad, sgu, wo, x, ada_raw, ada_b, g_post_mix, g_pre_ffn)


def _conv(u, halo, cw_ref, cb_ref):
    ext = jnp.concatenate([halo, u], axis=0)
    m1, m2 = pltpu.roll(ext, 1, 0)[8:], pltpu.roll(ext, 2, 0)[8:]
    return cb_ref[0] + ((m2 * cw_ref[0, pl.ds(0, 1), :] + m1 * cw_ref[0, pl.ds(1, 1), :]) + u * cw_ref[0, pl.ds(2, 1), :])


ROW_SUB = 256


def _sub_blocks(tm):
    return [slice(r, r + ROW_SUB) for r in range(0, tm, ROW_SUB)]


def _ffn_up_fwd(h2, w_up, conv_w, conv_b, tm):
    s = h2.shape[0]
    half = N_DEV // 2

    def body(h_ref, wa_ref, wb_ref, cwa_ref, cwb_ref, cba_ref, cbb_ref,
             ua_ref, ub_ref, ya_ref, yb_ref, act_ref, halo_a, halo_b, wa_t, wb_t):
        i = pl.program_id(1)

        @pl.when(i == 0)
        def _():
            halo_a[...] = jnp.zeros(halo_a.shape, F32)
            halo_b[...] = jnp.zeros(halo_b.shape, F32)
            wa_t[...] = wa_ref[0].T
            wb_t[...] = wb_ref[0].T

        ha, hb = halo_a[...], halo_b[...]
        for rows in _sub_blocks(tm):
            h = h_ref[rows, :]
            ua = _dot(h, wa_t[...])
            ub = _dot(h, wb_t[...])
            ua_ref[0, rows, :] = ua
            ub_ref[0, rows, :] = ub
            ya = _conv(ua, ha, cwa_ref, cba_ref)
            yb = _conv(ub, hb, cwb_ref, cbb_ref)
            ya_ref[0, rows, :] = ya
            yb_ref[0, rows, :] = yb
            ha, hb = ua[ROW_SUB - 8:], ub[ROW_SUB - 8:]
            act_ref[0, rows, :] = ((ya * jax.nn.sigmoid(ya)) * yb).astype(MXU)
        halo_a[...] = ha
        halo_b[...] = hb

    def blk(shape, off):
        return pl.BlockSpec(shape, lambda j, i: (j + off, 0, 0))

    def tok(off=0):
        return pl.BlockSpec((1, tm, FF_BLK), lambda j, i: (j + off, i, 0))

    return _call(
        body, name="ffn_up_fwd", grid=(half, s // tm), sem=("parallel", "arbitrary"),
        in_specs=[pl.BlockSpec((tm, D_MODEL), lambda j, i: (i, 0)),
                  blk((1, FF_BLK, D_MODEL), 0), blk((1, FF_BLK, D_MODEL), half),
                  blk((1, 3, FF_BLK), 0), blk((1, 3, FF_BLK), half), blk((1, 1, FF_BLK), 0), blk((1, 1, FF_BLK), half)],
        out_specs=[tok()] * 5,
        out_shape=[_sds((half, s, FF_BLK), F32)] * 4 + [_sds((half, s, FF_BLK), MXU)],
        scratch=[pltpu.VMEM((8, FF_BLK), F32), pltpu.VMEM((8, FF_BLK), F32),
                 pltpu.VMEM((D_MODEL, FF_BLK), MXU), pltpu.VMEM((D_MODEL, FF_BLK), MXU)],
    )(h2, w_up, w_up, conv_w, conv_w, conv_b, conv_b)


def _ffn_down_fwd(act, wd, x2, target, ada_raw, ada_b, g_post_ffn, tm):
    s = x2.shape[0]
    half = N_DEV // 2

    def body(act_ref, wd_ref, x2_ref, t_ref, ar_ref, ab_ref, g_ref, dout_ref, df_ref, loss_ref, dgt_ref, dg_ref):
        i = pl.program_id(0)

        @pl.when(i == 0)
        def _():
            loss_ref[...] = jnp.zeros(loss_ref.shape, F32)
            dgt_ref[...] = jnp.zeros(dgt_ref.shape, F32)
            dg_ref[...] = jnp.zeros(dg_ref.shape, F32)

        gt2 = _row(ar_ref, 5) + _row(ab_ref, 5)
        g = g_ref[...]
        for rows in _sub_blocks(tm):
            f = _dot(act_ref[0, rows, :], wd_ref[0])
            for j in range(1, half):
                f = f + _dot(act_ref[j, rows, :], wd_ref[j])
            fhat, rf = _rms(f)
            fn = fhat * g
            err = (x2_ref[rows, :] + gt2 * fn) - t_ref[rows, :]
            loss_ref[...] += 0.5 * jnp.sum(jnp.mean(err * err, axis=-1, keepdims=True))
            d_out = err * (1.0 / D_MODEL)
            dout_ref[rows, :] = d_out
            dgt_ref[...] += jnp.sum(d_out * fn, axis=0, keepdims=True)
            d_fn = d_out * gt2
            dg_ref[...] += jnp.sum(d_fn * fhat, axis=0, keepdims=True)
            df_ref[rows, :] = _rms_bwd(d_fn * g, fhat, rf).astype(MXU)

    vec = pl.BlockSpec((1, D_MODEL), lambda i: (0, 0))
    return _call(
        body, name="ffn_down_fwd", grid=(s // tm,), sem=("arbitrary",),
        in_specs=[pl.BlockSpec((half, tm, FF_BLK), lambda i: (0, i, 0)), _full(wd.shape), _rows(tm, D_MODEL),
                  _rows(tm, D_MODEL), _full(ada_raw.shape), _full(ada_b.shape), _full(g_post_ffn.shape)],
        out_specs=[_rows(tm, D_MODEL), _rows(tm, D_MODEL), pl.BlockSpec((1, 128), lambda i: (0, 0)), vec, vec],
        out_shape=[_sds((s, D_MODEL), F32), _sds((s, D_MODEL), MXU), _sds((1, 128), F32),
                   _sds((1, D_MODEL), F32), _sds((1, D_MODEL), F32)],
    )(act, wd, x2, target, ada_raw, ada_b, g_post_ffn)


def _ffn_down_bwd(d_f, wd, up_a, up_b, y_a, y_b, conv_w, act, h2, tm):
    s = d_f.shape[0]
    half = N_DEV // 2
    nt = s // tm

    def body(df_ref, wd_ref, ua_ref, ub_ref, ya_ref, yb_ref, cwa_ref, cwb_ref, act_ref, h2_ref,
             dup_ref, dcw_ref, dcb_ref, pd_ref, pu_ref, next_a, next_b, acc_d, acc_a, acc_b):
        i = pl.program_id(1)

        @pl.when(i == 0)
        def _():
            next_a[...] = jnp.zeros(next_a.shape, F32)
            next_b[...] = jnp.zeros(next_b.shape, F32)
            dcw_ref[...] = jnp.zeros(dcw_ref.shape, F32)
            dcb_ref[...] = jnp.zeros(dcb_ref.shape, F32)
            for acc in (acc_d, acc_a, acc_b):
                acc[...] = jnp.zeros(acc.shape, F32)

        def conv_bwd(d_y, u, nxt, cw_ref, part, rows):
            ext = jnp.concatenate([d_y, nxt], axis=0)
            p1 = pltpu.roll(ext, ROW_SUB + 7, 0)[:ROW_SUB]
            p2 = pltpu.roll(ext, ROW_SUB + 6, 0)[:ROW_SUB]
            d_u = (d_y * cw_ref[0, pl.ds(2, 1), :] + p1 * cw_ref[0, pl.ds(1, 1), :]) + p2 * cw_ref[0, pl.ds(0, 1), :]
            dup_ref[0, part, rows, :] = d_u.astype(MXU)
            dcb_ref[0, part] += jnp.sum(d_y, axis=0, keepdims=True)
            dcw_ref[0, part, pl.ds(0, 1), :] += jnp.sum(p2 * u, axis=0, keepdims=True)
            dcw_ref[0, part, pl.ds(1, 1), :] += jnp.sum(p1 * u, axis=0, keepdims=True)
            dcw_ref[0, part, pl.ds(2, 1), :] += jnp.sum(d_y * u, axis=0, keepdims=True)
            return d_y[:8]

        nxa, nxb = next_a[...], next_b[...]
        for rows in reversed(_sub_blocks(tm)):
            d_act = _dot_nt(df_ref[rows, :], wd_ref[0])
            ya, yb = ya_ref[0, rows, :], yb_ref[0, rows, :]
            sig = jax.nn.sigmoid(ya)
            d_ya = d_act * yb * (sig * (1.0 + ya * (1.0 - sig)))
            d_yb = d_act * (ya * sig)
            nxa = conv_bwd(d_ya, ua_ref[0, rows, :], nxa, cwa_ref, 0, rows)
            nxb = conv_bwd(d_yb, ub_ref[0, rows, :], nxb, cwb_ref, 1, rows)
        next_a[...] = nxa
        next_b[...] = nxb
        acc_d[...] += _dot_tn(act_ref[0], df_ref[...])
        acc_a[...] += _dot_tn(dup_ref[0, 0], h2_ref[...])
        acc_b[...] += _dot_tn(dup_ref[0, 1], h2_ref[...])

        @pl.when(i == nt - 1)
        def _():
            pd_ref[0] = acc_d[...].astype(MXU)
            pu_ref[0, 0] = acc_a[...].astype(MXU)
            pu_ref[0, 1] = acc_b[...].astype(MXU)

    def rev(i):
        return nt - 1 - i

    def blk(shape, off):
        return pl.BlockSpec(shape, lambda j, i: (j + off, 0, 0))

    tok = pl.BlockSpec((1, tm, FF_BLK), lambda j, i: (j, rev(i), 0))
    acc3 = pl.BlockSpec((1, 2, 3, FF_BLK), lambda j, i: (j, 0, 0, 0))
    acc1 = pl.BlockSpec((1, 2, 1, FF_BLK), lambda j, i: (j, 0, 0, 0))
    return _call(
        body, name="ffn_down_bwd", grid=(half, nt), sem=("parallel", "arbitrary"),
        in_specs=[pl.BlockSpec((tm, D_MODEL), lambda j, i: (rev(i), 0)), blk((1, FF_BLK, D_MODEL), 0),
                  tok, tok, tok, tok, blk((1, 3, FF_BLK), 0), blk((1, 3, FF_BLK), half),
                  tok, pl.BlockSpec((tm, D_MODEL), lambda j, i: (rev(i), 0))],
        out_specs=[pl.BlockSpec((1, 2, tm, FF_BLK), lambda j, i: (j, 0, rev(i), 0)), acc3, acc1,
                   pl.BlockSpec((1, FF_BLK, D_MODEL), lambda j, i: (j, 0, 0)),
                   pl.BlockSpec((1, 2, FF_BLK, D_MODEL), lambda j, i: (j, 0, 0, 0))],
        out_shape=[_sds((half, 2, s, FF_BLK), MXU), _sds((half, 2, 3, FF_BLK), F32), _sds((half, 2, 1, FF_BLK), F32),
                   _sds((half, FF_BLK, D_MODEL), MXU), _sds((half, 2, FF_BLK, D_MODEL), MXU)],
        scratch=[pltpu.VMEM((8, FF_BLK), F32), pltpu.VMEM((8, FF_BLK), F32)]
        + [pltpu.VMEM((FF_BLK, D_MODEL), F32)] * 3,
    )(d_f, wd, up_a, up_b, y_a, y_b, conv_w, conv_w, act, h2)


def _ffn_up_bwd(d_up, w_up, x2, m, d_out, ada_raw, ada_b, g_pre_ffn, g_post_mix, tm):
    s = x2.shape[0]
    half = N_DEV // 2

    def body(dup_ref, w_ref, x2_ref, m_ref, dout_ref, ar_ref, ab_ref, gpf_ref, gpm_ref,
             dx_ref, dm_ref, dsh_ref, dsc_ref, dgpf_ref, dgt1_ref, dgpm_ref):
        i = pl.program_id(0)

        @pl.when(i == 0)
        def _():
            for r in (dsh_ref, dsc_ref, dgpf_ref, dgt1_ref, dgpm_ref):
                r[...] = jnp.zeros(r.shape, F32)

        gt1 = _row(ar_ref, 2) + _row(ab_ref, 2)
        sc2 = _row(ar_ref, 4) + _row(ab_ref, 4)
        gpf, gpm = gpf_ref[...], gpm_ref[...]
        d_h2 = _dot(dup_ref[0, 0], w_ref[0])
        for j in range(1, half):
            d_h2 = d_h2 + _dot(dup_ref[j, 0], w_ref[j])
        for j in range(half):
            d_h2 = d_h2 + _dot(dup_ref[j, 1], w_ref[half + j])
        x2n, r2 = _rms(x2_ref[...])
        dsh_ref[...] += jnp.sum(d_h2, axis=0, keepdims=True)
        dsc_ref[...] += jnp.sum(d_h2 * (x2n * gpf), axis=0, keepdims=True)
        d_mod = d_h2 * (1.0 + sc2)
        dgpf_ref[...] += jnp.sum(d_mod * x2n, axis=0, keepdims=True)
        d_x2 = dout_ref[...] + _rms_bwd(d_mod * gpf, x2n, r2)
        dx_ref[...] = d_x2
        mhat, rm = _rms(m_ref[...])
        dgt1_ref[...] += jnp.sum(d_x2 * (mhat * gpm), axis=0, keepdims=True)
        d_mn = d_x2 * gt1
        dgpm_ref[...] += jnp.sum(d_mn * mhat, axis=0, keepdims=True)
        dm_ref[...] = _rms_bwd(d_mn * gpm, mhat, rm).astype(MXU)

    vec = pl.BlockSpec((1, D_MODEL), lambda i: (0, 0))
    tok = pl.BlockSpec((half, 2, tm, FF_BLK), lambda i: (0, 0, i, 0))
    return _call(
        body, name="ffn_up_bwd", grid=(s // tm,), sem=("arbitrary",),
        in_specs=[tok, _full(w_up.shape), _rows(tm, D_MODEL), _rows(tm, D_MODEL), _rows(tm, D_MODEL),
                  _full(ada_raw.shape), _full(ada_b.shape), _full(g_pre_ffn.shape), _full(g_post_mix.shape)],
        out_specs=[_rows(tm, D_MODEL), _rows(tm, D_MODEL), vec, vec, vec, vec, vec],
        out_shape=[_sds((s, D_MODEL), F32), _sds((s, D_MODEL), MXU)] + [_sds((1, D_MODEL), F32)] * 5,
    )(d_up, w_up, x2, m, d_out, ada_raw, ada_b, g_pre_ffn, g_post_mix)


def _out_proj_bwd(d_m, wo, o_pad, tm):
    s = d_m.shape[0]
    hp = HEADS * HEAD_PAD

    def body(dm_ref, wo_ref, o_ref, do_ref, dsgu_ref, delta_ref):
        d_cat = _dot_nt(dm_ref[...], wo_ref[...])
        d_o = d_cat[:, :hp]
        do_ref[...] = d_o.astype(MXU)
        dsgu_ref[...] = d_cat[:, hp:]
        prod = d_o * o_ref[...].astype(F32)
        for h in range(HEADS):
            delta_ref[h] = jnp.sum(prod[:, h * HEAD_PAD:(h + 1) * HEAD_PAD], axis=-1, keepdims=True)

    return _call(
        body, name="out_proj_bwd", grid=(s // tm,), sem=("parallel",),
        in_specs=[_rows(tm, D_MODEL), _full(wo.shape), _rows(tm, hp)],
        out_specs=[_rows(tm, hp), _rows(tm, GM_WIDTH), pl.BlockSpec((HEADS, tm, 1), lambda i: (0, i, 0))],
        out_shape=[_sds((s, hp), MXU), _sds((s, GM_WIDTH), F32), _sds((HEADS, s, 1), F32)],
    )(d_m, wo, o_pad)


def _attn_bwd(qp, kp, vp, d_o, lse, delta, tq, scattered, gathered):
    s = qp.shape[0]
    nq = s // tq
    hb = ATTN_HEADS_PER_STEP
    groups = HEADS // hb
    width = hb * HEAD_PAD
    ns, ng = len(scattered), len(gathered)
    nc = ns + ng
    slots = [slot for _, slot in scattered]

    def body(q_ref, k_ref, v_ref, do_ref, lse_ref, dl_ref, *rest):
        c_in, (dq_ref, dk_ref, dv_ref), c_out = rest[:nc], rest[nc:nc + 3], rest[nc + 3:2 * nc + 3]
        dk_sc, dv_sc = rest[2 * nc + 3:2 * nc + 5]
        sems = rest[2 * nc + 5:]
        s_start, s_finish = _scatter_steps(c_in[:ns], c_out[:ns], sems[:3], slots)
        g_start, g_forward, g_finish = _gather_steps(c_in[ns:], c_out[ns:], sems[3:])
        g, j = pl.program_id(0), pl.program_id(1)

        @pl.when((g == 0) & (j == 0))
        def _():
            s_start()
            g_start()

        pl.when((g == groups - 1) & (j == 0))(g_forward)

        @pl.when(j == 0)
        def _():
            dq_ref[...] = jnp.zeros(dq_ref.shape, F32)

        dk_sc[...] = jnp.zeros(dk_sc.shape, F32)
        dv_sc[...] = jnp.zeros(dv_sc.shape, F32)

        def tile(i, masked):
            rows = pl.ds(pl.multiple_of(i * tq, tq), tq)
            for hh in range(hb):
                lanes = slice(hh * HEAD_PAD, (hh + 1) * HEAD_PAD)
                q, do, k = q_ref[rows, lanes], do_ref[rows, lanes], k_ref[:, lanes]
                sc = _dot_nt(q, k)
                if masked:
                    sc = jnp.where(_chunk_mask(tq, tq, 0), sc, NEG_BIG)
                p = jnp.exp2(sc * SCALE_LOG2E - lse_ref[hh, rows, :])
                dv_sc[hh] += _dot_tn(p.astype(MXU), do)
                dp = _dot_nt(do, v_ref[:, lanes])
                ds = (p * (dp - dl_ref[hh, rows, :])).astype(MXU)
                dk_sc[hh] += _dot_tn(ds, q)
                dq_ref[rows, lanes] += _dot(ds, k) * ATTN_SCALE

        def off_diagonal_pair(p, carry):
            tile(j + 1 + 2 * p, False)
            tile(j + 2 + 2 * p, False)
            return carry

        below = nq - 1 - j
        tile(j, True)
        lax.fori_loop(0, below // 2, off_diagonal_pair, 0)

        @pl.when(below % 2 == 1)
        def _():
            tile(nq - 1, False)
        for hh in range(hb):
            lanes = slice(hh * HEAD_PAD, (hh + 1) * HEAD_PAD)
            dk_ref[:, lanes] = dk_sc[hh] * ATTN_SCALE
            dv_ref[:, lanes] = dv_sc[hh]
        @pl.when((g == groups - 1) & (j == nq - 1))
        def _():
            g_finish()
            s_finish()

    seq_spec = pl.BlockSpec((s, width), lambda g, j: (0, g))
    kv_spec = pl.BlockSpec((tq, width), lambda g, j: (j, g))
    col_spec = pl.BlockSpec((hb, s, 1), lambda g, j: (g, 0, 0))
    any_spec = pl.BlockSpec(memory_space=pl.ANY)
    outs = _call(
        body, name="attn_bwd", grid=(groups, nq), sem=("arbitrary", "arbitrary"),
        in_specs=[seq_spec, kv_spec, kv_spec, seq_spec, col_spec, col_spec] + [any_spec] * nc,
        out_specs=[seq_spec, kv_spec, kv_spec] + [any_spec] * nc,
        out_shape=[_sds(qp.shape, F32), _sds(qp.shape, F32), _sds(qp.shape, F32)]
        + [_scatter_out_shape(a, slot) for a, slot in scattered]
        + [_sds((N_DEV,) + a.shape, a.dtype) for a in gathered],
        scratch=[pltpu.VMEM((hb, tq, HEAD_PAD), F32), pltpu.VMEM((hb, tq, HEAD_PAD), F32)]
        + _comm_sems(ns) + _comm_sems(ng),
    )(qp, kp, vp, d_o, lse, delta, *[a for a, _ in scattered], *gathered)
    return outs[0], outs[1], outs[2], outs[3:3 + ns], outs[3 + ns:]


def _gmlp_bwd(z, d_sgu, ln_g, ln_b, w_sp, bias_exp, tm):
    s = z.shape[0]
    nblk = tm // GM_CHUNK

    def body(zu_ref, zv_ref, dsgu_ref, lg_ref, lb_ref, w_ref, be_ref,
             dguv_ref, dws_ref, dbs_ref, dlg_ref, dlb_ref, dbe_sc, dvln_sc, dlg_sc, dlb_sc):
        i = pl.program_id(0)

        @pl.when(i == 0)
        def _():
            for r in (dws_ref, dlg_sc, dlb_sc, dbe_sc):
                r[...] = jnp.zeros(r.shape, F32)

        seg = _seg_matrix()
        mask = _spatial_mask()
        wm = [(w_ref[h] * mask).astype(MXU) for h in range(HEADS)]
        zu, zv = zu_ref[...], zv_ref[...]
        gu = _gelu(zu)
        _, vhat, rstd = _gm_norm(zv, seg)
        lg = lg_ref[...]
        vln = (vhat * lg + lb_ref[...]).astype(MXU)
        d_sgu = dsgu_ref[...]
        for n in range(nblk):
            rows = slice(n * GM_CHUNK, (n + 1) * GM_CHUNK)
            vb = vln[rows]
            mixed = _gm_mix(wm, vb, GM_CHUNK) + be_ref[...]
            d_mixed = d_sgu[rows] * gu[rows]
            dguv_ref[rows, pl.ds(0, GM_WIDTH)] = ((d_sgu[rows] * mixed) * _gelu_grad(zu[rows])).astype(MXU)
            dbe_sc[...] += d_mixed
            dmb = d_mixed.astype(MXU)
            for p, (lanes, first) in enumerate(_gm_pairs(GM_CHUNK)):
                dm_pair, v_pair = dmb[:, lanes], vb[:, lanes]
                zero = jnp.zeros_like(dm_pair)
                dws_ref[2 * p] += _dot_nt(jnp.where(first, dm_pair, zero), v_pair)
                dws_ref[2 * p + 1] += _dot_nt(jnp.where(first, zero, dm_pair), v_pair)
            dvln_sc[rows, :] = _gm_mix(wm, dmb, GM_CHUNK, transposed=True)
        d_vln = dvln_sc[...]
        dlg_sc[...] += jnp.sum(d_vln * vhat, axis=0, keepdims=True)
        dlb_sc[...] += jnp.sum(d_vln, axis=0, keepdims=True)
        d_vhat = d_vln * lg
        d_gv = rstd * ((d_vhat - _split_dot(d_vhat, seg)) - vhat * _split_dot(d_vhat * vhat, seg))
        dguv_ref[:, pl.ds(GM_WIDTH, GM_WIDTH)] = (d_gv * _gelu_grad(zv)).astype(MXU)

        @pl.when(i == pl.num_programs(0) - 1)
        def _():
            for h in range(HEADS):
                dws_ref[h] = dws_ref[h] * mask
            hrow = lax.broadcasted_iota(jnp.int32, (HEADS, GM_WIDTH), 0)
            hlane = lax.broadcasted_iota(jnp.int32, (HEADS, GM_WIDTH), 1) >> 6
            ind = jnp.where(hrow == hlane, 1.0, 0.0).astype(MXU)
            acc = dbe_sc[...]
            hi = acc.astype(MXU)
            lo = (acc - hi.astype(F32)).astype(MXU)
            dbs_ref[...] = _dot_nt(ind, hi) + _dot_nt(ind, lo)
            pick = (lax.broadcasted_iota(jnp.int32, (GM_WIDTH, GM_DIM), 0) & (GM_DIM - 1)
                    == lax.broadcasted_iota(jnp.int32, (GM_WIDTH, GM_DIM), 1))
            pick = jnp.where(pick, 1.0, 0.0).astype(MXU)
            for src, dst in ((dlg_sc, dlg_ref), (dlb_sc, dlb_ref)):
                spread = jnp.where(hrow == hlane, jnp.broadcast_to(src[...], (HEADS, GM_WIDTH)), 0.0)
                dst[...] = _split_dot3(spread, pick)

    return _call(
        body, name="gmlp_bwd", grid=(s // tm,), sem=("arbitrary",),
        in_specs=[_rows(tm, GM_WIDTH, 1), _rows(tm, GM_WIDTH, 2), _rows(tm, GM_WIDTH), _full(ln_g.shape),
                  _full(ln_b.shape), _full(w_sp.shape), _full(bias_exp.shape)],
        out_specs=[_rows(tm, 2 * GM_WIDTH), _full(w_sp.shape), _full((HEADS, GM_CHUNK)), _full((HEADS, GM_DIM)),
                   _full((HEADS, GM_DIM))],
        out_shape=[_sds((s, 2 * GM_WIDTH), MXU), _sds(w_sp.shape, F32), _sds((HEADS, GM_CHUNK), F32),
                   _sds((HEADS, GM_DIM), F32), _sds((HEADS, GM_DIM), F32)],
        scratch=[pltpu.VMEM((GM_CHUNK, GM_WIDTH), F32), pltpu.VMEM((tm, GM_WIDTH), F32),
                 pltpu.VMEM((1, GM_WIDTH), F32), pltpu.VMEM((1, GM_WIDTH), F32)],
    )(z, z, d_sgu, ln_g, ln_b, w_sp, bias_exp)


def _mix_in_bwd(dq, dk, dv, z, d_guv, x, d_x_part, ada_raw, ada_b, g_pre, g_q, g_kv, w1t, wqt, wkv,
                cos_t, sin_t, tm):
    s = x.shape[0]
    hp = HEADS * HEAD_PAD
    za = Q_LORA + KV_LORA + HEAD_PAD

    def body(dq_ref, dk_ref, dv_ref, z_ref, dguv_ref, x_ref, dxp_ref, ar_ref, ab_ref, g_ref, gq_ref, gkv_ref,
             w1_ref, wq_ref, wkv_ref, cos_ref, sin_ref,
             gx_ref, dza_ref, dqp_ref, dkvp_ref, dsh_ref, dsc_ref, dg_ref, dgq_ref, dgkv_ref):
        i = pl.program_id(0)

        @pl.when(i == 0)
        def _():
            for r in (dsh_ref, dsc_ref, dg_ref, dgq_ref, dgkv_ref):
                r[...] = jnp.zeros(r.shape, F32)

        cos, sin = cos_ref[...], sin_ref[...]
        d_krot = jnp.zeros((tm, HEAD_PAD), F32)
        for h in range(HEADS):
            blk = slice(h * HEAD_PAD, (h + 1) * HEAD_PAD)
            dqp_ref[:, blk] = _rope_transposed(dq_ref[:, blk], cos, sin).astype(MXU)
            dk_h = dk_ref[:, blk]
            d_krot = d_krot + dk_h
            dkvp_ref[:, blk] = dk_h.astype(MXU)
        dkvp_ref[:, pl.ds(hp, hp)] = dv_ref[...].astype(MXU)
        lane = lax.broadcasted_iota(jnp.int32, (tm, HEAD_PAD), 1)
        d_kr = jnp.where((lane >= NOPE) & (lane < NOPE + ROPE), _rope_transposed(d_krot, cos, sin), 0.0)
        d_cqn = _dot(dqp_ref[...], wq_ref[...])
        d_ckvn = _dot_nt(dkvp_ref[...], wkv_ref[...])
        zt = z_ref[...]
        gq, gkv = gq_ref[...], gkv_ref[...]
        cq_hat, rq = _rms(zt[:, :Q_LORA])
        ckv_hat, rkv = _rms(zt[:, Q_LORA:Q_LORA + KV_LORA])
        dgq_ref[...] += jnp.sum(d_cqn * cq_hat, axis=0, keepdims=True)
        dgkv_ref[...] += jnp.sum(d_ckvn * ckv_hat, axis=0, keepdims=True)
        d_cq = _rms_bwd(d_cqn * gq, cq_hat, rq)
        d_ckv = _rms_bwd(d_ckvn * gkv, ckv_hat, rkv)
        d_za = jnp.concatenate([d_cq, d_ckv, d_kr], axis=1).astype(MXU)
        dza_ref[...] = d_za
        d_h1 = _dot(d_za, w1_ref[pl.ds(0, za), :]) + _dot(dguv_ref[...], w1_ref[pl.ds(za, 2 * GM_WIDTH), :])
        sc1 = _row(ar_ref, 1) + _row(ab_ref, 1)
        g = g_ref[...]
        xn, r1 = _rms(x_ref[...])
        dsh_ref[...] += jnp.sum(d_h1, axis=0, keepdims=True)
        dsc_ref[...] += jnp.sum(d_h1 * (xn * g), axis=0, keepdims=True)
        d_mod = d_h1 * (1.0 + sc1)
        dg_ref[...] += jnp.sum(d_mod * xn, axis=0, keepdims=True)
        gx_ref[...] = dxp_ref[...] + _rms_bwd(d_mod * g, xn, r1)

    vec = pl.BlockSpec((1, D_MODEL), lambda i: (0, 0))
    return _call(
        body, name="mix_in_bwd", grid=(s // tm,), sem=("arbitrary",),
        in_specs=[_rows(tm, hp), _rows(tm, hp), _rows(tm, hp), _rows(tm, za), _rows(tm, 2 * GM_WIDTH),
                  _rows(tm, D_MODEL), _rows(tm, D_MODEL), _full(ada_raw.shape), _full(ada_b.shape), _full(g_pre.shape),
                  _full(g_q.shape), _full(g_kv.shape), _full(w1t.shape), _full(wqt.shape),
                  _full(wkv.shape), _rows(tm, HEAD_PAD), _rows(tm, HEAD_PAD)],
        out_specs=[_rows(tm, D_MODEL), _rows(tm, za), _rows(tm, hp), _rows(tm, 2 * hp), vec, vec, vec,
                   _full(g_q.shape), _full(g_kv.shape)],
        out_shape=[_sds((s, D_MODEL), F32), _sds((s, za), MXU), _sds((s, hp), MXU), _sds((s, 2 * hp), MXU),
                   _sds((1, D_MODEL), F32), _sds((1, D_MODEL), F32), _sds((1, D_MODEL), F32),
                   _sds(g_q.shape, F32), _sds(g_kv.shape, F32)],
    )(dq, dk, dv, z, d_guv, x, d_x_part, ada_raw, ada_b, g_pre, g_q, g_kv, w1t, wqt, wkv, cos_t, sin_t)


def _tn_matmul(a, b, name, ts):
    ga, s, m = a.shape
    gb, _, n = b.shape
    g = max(ga, gb)
    tn = n if n <= 1024 else 1024
    steps = s // ts

    def body(a_ref, b_ref, o_ref, acc):
        k = pl.program_id(2)

        @pl.when(k == 0)
        def _():
            acc[...] = jnp.zeros(acc.shape, F32)

        acc[...] += _dot_tn(a_ref[0], b_ref[0])

        @pl.when(k == steps - 1)
        def _():
            o_ref[0] = acc[...].astype(MXU)

    return _call(
        body, name=name, grid=(g, n // tn, steps), sem=("parallel", "parallel", "arbitrary"),
        in_specs=[pl.BlockSpec((1, ts, m), lambda gi, ni, k: (gi if ga > 1 else 0, k, 0)),
                  pl.BlockSpec((1, ts, tn), lambda gi, ni, k: (gi if gb > 1 else 0, k, ni))],
        out_specs=pl.BlockSpec((1, m, tn), lambda gi, ni, k: (gi, 0, ni)),
        out_shape=_sds((g, m, n), MXU),
        scratch=[pltpu.VMEM((m, tn), F32)],
    )(a, b)


def _adamw(w, g, m, v):
    m2 = ADAM_B1 * m + (1.0 - ADAM_B1) * g
    v2 = ADAM_B2 * v + (1.0 - ADAM_B2) * (g * g)
    m_hat = m2 / (1.0 - ADAM_B1 ** ADAM_STEP)
    v_hat = v2 / (1.0 - ADAM_B2 ** ADAM_STEP)
    delta = -ADAM_LR * (m_hat / (jnp.sqrt(v_hat) + ADAM_EPS) + ADAM_WD * w)
    return delta, m2, v2


def _adam_reduce(recv, w, m, v, name):
    r, c = w.shape
    tr = r if r <= 512 else max(t for t in range(16, 513, 16) if r % t == 0)

    def body(p_ref, w_ref, m_ref, v_ref, g_ref, d_ref, mo_ref, vo_ref):
        g = p_ref[0].astype(F32)
        for j in range(1, N_DEV):
            g = g + p_ref[j].astype(F32)
        g_ref[...] = g
        d_ref[...], mo_ref[...], vo_ref[...] = _adamw(w_ref[...], g, m_ref[...], v_ref[...])

    blk = pl.BlockSpec((tr, c), lambda i: (i, 0))
    return _call(
        body, name=name, grid=(r // tr,), sem=("parallel",),
        in_specs=[pl.BlockSpec((N_DEV, tr, c), lambda i: (0, i, 0)), blk, blk, blk],
        out_specs=[blk] * 4, out_shape=[_sds((r, c), F32)] * 4,
    )(recv, w, m, v)


def _adam_w_ada(c_act_t, d_ada_cols, w, m, v):
    r, c = w.shape
    tr = 256

    def body(ct_ref, da_ref, w_ref, m_ref, v_ref, g_ref, d_ref, mo_ref, vo_ref):
        g = ct_ref[:, pl.ds(0, 1)] * da_ref[pl.ds(0, 1), :]
        for b in range(1, N_DEV):
            g = g + ct_ref[:, pl.ds(b, 1)] * da_ref[pl.ds(b, 1), :]
        g_ref[...] = g
        d_ref[...], mo_ref[...], vo_ref[...] = _adamw(w_ref[...], g, m_ref[...], v_ref[...])

    blk = pl.BlockSpec((tr, c), lambda i: (i, 0))
    return _call(
        body, name="adam_w_ada", grid=(r // tr,), sem=("parallel",),
        in_specs=[pl.BlockSpec((tr, N_DEV), lambda i: (i, 0)), _full(d_ada_cols.shape), blk, blk, blk],
        out_specs=[blk] * 4, out_shape=[_sds((r, c), F32)] * 4,
    )(c_act_t, d_ada_cols, w, m, v)


VEC_ROWS = D_MODEL // 128
PK_ADA = 0
PK_GAIN = PK_ADA + 6 * VEC_ROWS
PK_GQ = PK_GAIN + 4 * VEC_ROWS
PK_GKV = PK_GQ + Q_LORA // 128
PK_LOSS = PK_GKV + KV_LORA // 128
PK_LNG = 88
PK_LNB = PK_LNG + HEADS
PK_BS = PK_LNB + HEADS
PK_CB = PK_BS + HEADS
CB_ROWS = 6
PK_WS = PK_CB + N_DEV * CB_ROWS
PK_ROWS = PK_WS + HEADS * GM_CHUNK
assert PK_LOSS < PK_LNG and PK_ROWS % 8 == 0
LATE_GAIN = 2 * VEC_ROWS
LATE_GQ = 3 * VEC_ROWS
LATE_GKV = LATE_GQ + Q_LORA // 128
LATE_ROWS = 32


def _cb_chunks():
    return [(k, k * 128, min(128, FF_BLK - k * 128)) for k in range(CB_ROWS)]


def _put_rows(out_ref, row0, ref, width):
    for k in range(width // 128):
        out_ref[pl.ds(row0 + k, 1), :] = ref[:, pl.ds(k * 128, 128)]


def _pack_small(ada_rows, gains, loss_part, d_ln_g, d_ln_b, d_bs, d_cb, d_ws):
    half = N_DEV // 2

    def body(*refs):
        vec_refs = refs[:7]
        loss_ref, lng_ref, lnb_ref, bs_ref, cb_ref, ws_ref, out_ref = refs[7:]
        out_ref[pl.ds(0, PK_WS), :] = jnp.zeros((PK_WS, 128), F32)
        for n, ref in enumerate(vec_refs[:4]):
            _put_rows(out_ref, PK_ADA + (2 + n) * VEC_ROWS, ref, D_MODEL)
        for n, ref in enumerate(vec_refs[4:]):
            _put_rows(out_ref, PK_GAIN + (1 + n) * VEC_ROWS, ref, D_MODEL)
        _put_rows(out_ref, PK_LOSS, loss_ref, 128)
        out_ref[pl.ds(PK_LNG, HEADS), pl.ds(0, GM_DIM)] = lng_ref[...]
        out_ref[pl.ds(PK_LNB, HEADS), pl.ds(0, GM_DIM)] = lnb_ref[...]
        out_ref[pl.ds(PK_BS, HEADS), :] = bs_ref[...]
        for j in range(N_DEV):
            for k, lane, width in _cb_chunks():
                out_ref[pl.ds(PK_CB + j * CB_ROWS + k, 1), pl.ds(0, width)] = cb_ref[j % half, j // half, :, pl.ds(lane, width)]
        for h in range(HEADS):
            out_ref[pl.ds(PK_WS + h * GM_CHUNK, GM_CHUNK), :] = ws_ref[h]

    ins = list(ada_rows) + list(gains) + [loss_part, d_ln_g, d_ln_b, d_bs, d_cb, d_ws]
    return _call(body, name="pack_small", grid=(1,), in_specs=[_full(a.shape) for a in ins],
                 out_specs=_full((PK_ROWS, 128)), out_shape=_sds((PK_ROWS, 128), F32))(*ins)


def _pack_late(d_sh1, d_sc1, d_g_pre_mix, d_g_q, d_g_kv):
    def body(sh_ref, sc_ref, g_ref, gq_ref, gkv_ref, out_ref):
        out_ref[...] = jnp.zeros((LATE_ROWS, 128), F32)
        _put_rows(out_ref, 0, sh_ref, D_MODEL)
        _put_rows(out_ref, VEC_ROWS, sc_ref, D_MODEL)
        _put_rows(out_ref, LATE_GAIN, g_ref, D_MODEL)
        _put_rows(out_ref, LATE_GQ, gq_ref, Q_LORA)
        _put_rows(out_ref, LATE_GKV, gkv_ref, KV_LORA)

    ins = [d_sh1, d_sc1, d_g_pre_mix, d_g_q, d_g_kv]
    return _call(body, name="pack_late", grid=(1,), in_specs=[_full(a.shape) for a in ins],
                 out_specs=_full((LATE_ROWS, 128)), out_shape=_sds((LATE_ROWS, 128), F32))(*ins)


def _adam_small(gathered, late, params):
    n_par = len(params)

    def body(p_ref, late_ref, *refs):
        ins = [refs[3 * n:3 * n + 3] for n in range(n_par)]
        outs = [refs[3 * n_par + 4 * n:3 * n_par + 4 * n + 4] for n in range(n_par)]
        loss_ref, dada_ref = refs[7 * n_par:]

        def total(rows, lanes=slice(None), src=p_ref):
            g = src[0, rows, lanes]
            for j in range(1, N_DEV):
                g = g + src[j, rows, lanes]
            return g

        def apply(n, g, idx):
            w_ref, m_ref, v_ref = ins[n]
            d, m2, v2 = _adamw(w_ref[idx], g, m_ref[idx], v_ref[idx])
            for ref, val in zip(outs[n], (g, d, m2, v2)):
                ref[idx] = val

        def vector(n, src, row0, width, lane0=0):
            for k in range(width // 128):
                apply(n, total(pl.ds(row0 + k, 1), src=src), (slice(None), pl.ds(lane0 + k * 128, 128)))

        vector(0, late_ref, 0, 2 * D_MODEL)
        vector(0, p_ref, PK_ADA + 2 * VEC_ROWS, 4 * D_MODEL, lane0=2 * D_MODEL)
        vector(1, late_ref, LATE_GAIN, D_MODEL)
        for n in range(1, 4):
            vector(1 + n, p_ref, PK_GAIN + n * VEC_ROWS, D_MODEL)
        vector(5, late_ref, LATE_GQ, Q_LORA)
        vector(6, late_ref, LATE_GKV, KV_LORA)
        apply(7, total(pl.ds(PK_LNG, HEADS), pl.ds(0, GM_DIM)), (0,))
        apply(8, total(pl.ds(PK_LNB, HEADS), pl.ds(0, GM_DIM)), (0,))
        for h in range(HEADS):
            apply(9, total(pl.ds(PK_WS + h * GM_CHUNK, GM_CHUNK)), (0, h))
        apply(10, total(pl.ds(PK_BS, HEADS)), (0,))
        for j in range(N_DEV):
            for k, lane, width in _cb_chunks():
                apply(11, total(pl.ds(PK_CB + j * CB_ROWS + k, 1), pl.ds(0, width)), (pl.ds(j, 1), pl.ds(lane, width)))
        loss_ref[...] = total(pl.ds(PK_LOSS, 1))
        dada_ref[:, pl.ds(0, 2 * VEC_ROWS), :] = late_ref[:, pl.ds(0, 2 * VEC_ROWS), :]
        dada_ref[:, pl.ds(2 * VEC_ROWS, 4 * VEC_ROWS), :] = p_ref[:, pl.ds(PK_ADA + 2 * VEC_ROWS, 4 * VEC_ROWS), :]

    flat = [a for triple in params for a in triple]
    out_shape = [_sds(w.shape, F32) for w, _, _ in params for _ in range(4)]
    out_shape += [_sds((1, 128), F32), _sds((N_DEV, 6 * VEC_ROWS, 128), F32)]
    outs = _call(body, name="adam_small", grid=(1,),
                 in_specs=[_full(gathered.shape), _full(late.shape)] + [_full(a.shape) for a in flat],
                 out_specs=[_full(o.shape) for o in out_shape], out_shape=out_shape)(gathered, late, *flat)
    return [tuple(outs[4 * n:4 * n + 4]) for n in range(n_par)], outs[-2], outs[-1]


def _rope_tables(s):
    pos = jnp.arange(s, dtype=F32)
    inv = ROPE_THETA ** (-jnp.arange(0, ROPE, 2, dtype=F32) / ROPE)
    lane_inv = jnp.concatenate([jnp.zeros((NOPE,), F32), inv, inv, jnp.zeros((HEAD_PAD - NOPE - ROPE,), F32)])
    ang = pos[:, None] * lane_inv[None, :]
    return jnp.cos(ang), jnp.sin(ang)


def kernel(x, c, w_ada, b_ada, g_pre_mix, g_post_mix, w_in, g_q, w_uq, g_kv, w_ukv, gm_ln_g, gm_ln_b, w_spatial, b_spatial, w_out, g_pre_ffn, g_post_ffn, w_up, conv_w, conv_b, w_down, loss_target, m_w_ada, m_b_ada, m_g_pre_mix, m_g_post_mix, m_w_in, m_g_q, m_w_uq, m_g_kv, m_w_ukv, m_gm_ln_g, m_gm_ln_b, m_w_spatial, m_b_spatial, m_w_out, m_g_pre_ffn, m_g_post_ffn, m_w_up, m_conv_w, m_conv_b, m_w_down, v_w_ada, v_b_ada, v_g_pre_mix, v_g_post_mix, v_w_in, v_g_q, v_w_uq, v_g_kv, v_w_ukv, v_gm_ln_g, v_gm_ln_b, v_w_spatial, v_b_spatial, v_w_out, v_g_pre_ffn, v_g_post_ffn, v_w_up, v_conv_w, v_conv_b, v_w_down):
    s = x.shape[1]
    tm = min(256, s)
    tf = min(2 * ROW_SUB, s)
    tq = min(512, s)
    ts = min(2048, s)
    hp = HEADS * HEAD_PAD
    half = N_DEV // 2
    my_slot = 4 * lax.axis_index("x") + 2 * lax.axis_index("y") + lax.axis_index("c")
    x2d, target = x[0], loss_target[0]

    def t_(a):
        return jnp.swapaxes(a[0], 0, 1)

    w_in_t, m_in_t, v_in_t = t_(w_in), t_(m_w_in), t_(v_w_in)
    w_uq_t, m_uq_t, v_uq_t = t_(w_uq), t_(m_w_uq), t_(v_w_uq)
    w_up_t, m_up_t, v_up_t = t_(w_up), t_(m_w_up), t_(v_w_up)
    (g_c, g_in_t, g_uq_t, g_ukv, g_cw), _ = _exchange(
        [c, w_in_t.astype(MXU), w_uq_t.astype(MXU), w_ukv[0].astype(MXU), conv_w[0]], [], "gather_mixer_weights")

    w_in_f = g_in_t.reshape(-1, D_MODEL)
    o1, o2, o3 = Q_LORA, Q_LORA + KV_LORA, Q_LORA + KV_LORA + ROPE
    w1t = jnp.concatenate([w_in_f[:o2], jnp.zeros((NOPE, D_MODEL), MXU), w_in_f[o2:o3],
                           jnp.zeros((HEAD_PAD - NOPE - ROPE, D_MODEL), MXU), w_in_f[o3:]], axis=0)
    wqt = jnp.pad(g_uq_t, ((0, 0), (0, HEAD_PAD - NOPE - ROPE), (0, 0))).reshape(hp, Q_LORA)
    w_ukv_f = jnp.transpose(g_ukv, (1, 0, 2)).reshape(KV_LORA, HEADS, 2 * NOPE)
    pad_head = ((0, 0), (0, 0), (0, HEAD_PAD - NOPE))
    wkv = jnp.concatenate([jnp.pad(w_ukv_f[:, :, :NOPE], pad_head).reshape(KV_LORA, hp),
                           jnp.pad(w_ukv_f[:, :, NOPE:], pad_head).reshape(KV_LORA, hp)], axis=1)
    cb8 = conv_b.reshape(N_DEV, 1, FF_BLK)
    bias_exp = jnp.repeat(b_spatial[0].T, GM_DIM, axis=1)
    ln_g, ln_b = gm_ln_g.reshape(1, GM_WIDTH), gm_ln_b.reshape(1, GM_WIDTH)
    w_sp = w_spatial[0]
    cos_t, sin_t = _rope_tables(s)

    ada_part, c_act = _ada_fwd(g_c.reshape(N_DEV, D_MODEL), w_ada[0])
    _, (ada_recv,) = _exchange([], [(ada_part.reshape(N_DEV, 1, -1), _plain_slot)], "ada_rows")
    ada_raw = ada_recv.reshape(6, D_MODEL)
    ada_b = b_ada.reshape(6, D_MODEL)

    h1, z, qp, kp, vp, cqn, ckvn = _mix_in_fwd(x2d, ada_raw, ada_b, g_pre_mix, w1t, g_q, g_kv, wqt, wkv, cos_t, sin_t, tm)
    sgu = _gmlp_fwd(z, ln_g, ln_b, w_sp, bias_exp, tm)
    o_pad, lse, (g_out, g_up, g_down) = _attn_fwd(
        qp, kp, vp, tq, [w_out[0].astype(MXU), w_up_t.astype(MXU), w_down[0].astype(MXU)])
    w_out_f = g_out.reshape(2 * GM_WIDTH, D_MODEL)
    wo_attn = jnp.pad(w_out_f[:GM_WIDTH].reshape(HEADS, NOPE, D_MODEL), ((0, 0), (0, HEAD_PAD - NOPE), (0, 0)))
    wo = jnp.concatenate([wo_attn.reshape(hp, D_MODEL), w_out_f[GM_WIDTH:]], axis=0)
    wd = g_down.reshape(half, FF_BLK, D_MODEL)
    m_mix, x2, h2 = _out_proj_fwd(o_pad, sgu, wo, x2d, ada_raw, ada_b, g_post_mix, g_pre_ffn, tm)
    up_a, up_b, y_a, y_b, act = _ffn_up_fwd(h2, g_up, g_cw, cb8, tf)
    d_out, d_f, loss_part, d_gt2, d_g_post_ffn = _ffn_down_fwd(act, wd, x2, target, ada_raw, ada_b, g_post_ffn, tf)

    d_up, d_cw, d_cb, p_down, p_up = _ffn_down_bwd(d_f, wd, up_a, up_b, y_a, y_b, g_cw, act, h2, tf)
    p_down = p_down.reshape(N_DEV, -1, D_MODEL)
    d_x2, d_m, d_sh2, d_sc2, d_g_pre_ffn, d_gt1, d_g_post_mix = _ffn_up_bwd(
        d_up, g_up, x2, m_mix, d_out, ada_raw, ada_b, g_pre_ffn, g_post_mix, tm)
    d_m3 = d_m[None]
    dwo_attn = _tn_matmul(o_pad[None], d_m3, "dw_out_attn", ts)[0].reshape(HEADS, HEAD_PAD, D_MODEL)[:, :NOPE]
    dwo_sgu = _tn_matmul(sgu[None], d_m3, "dw_out_sgu", ts)[0]
    p_out = jnp.concatenate([dwo_attn.reshape(GM_WIDTH, D_MODEL), dwo_sgu], axis=0).reshape(N_DEV, -1, D_MODEL)
    d_o, d_sgu, delta = _out_proj_bwd(d_m, wo, o_pad, tm)
    d_guv, d_ws, d_bs, d_ln_g, d_ln_b = _gmlp_bwd(z, d_sgu, ln_g, ln_b, w_sp, bias_exp, tm)
    packed = _pack_small([d_gt1, d_sh2, d_sc2, d_gt2], [d_g_post_mix, d_g_pre_ffn, d_g_post_ffn], loss_part,
                         d_ln_g, d_ln_b, d_bs, d_cb, d_ws)

    def ffn_slot(j):
        return (j % half, j // half)

    dq, dk, dv, (r_out, r_up, r_down, r_cw), (g_small,) = _attn_bwd(
        qp, kp, vp, d_o, lse, delta, tq,
        [(p_out, _plain_slot), (p_up, ffn_slot), (p_down, _plain_slot), (d_cw, ffn_slot)], [packed])
    grad_x, d_za, d_qp, d_kvp, d_sh1, d_sc1, d_g_pre_mix, d_g_q, d_g_kv = _mix_in_bwd(
        dq, dk, dv, z, d_guv, x2d, d_x2, ada_raw, ada_b, g_pre_mix, g_q, g_kv, w1t, wqt, wkv, cos_t, sin_t, tm)
    h1_3 = h1[None]
    dw1a = _tn_matmul(d_za[None], h1_3, "dw_in_a", ts)[0]
    dw1b = _tn_matmul(d_guv[None], h1_3, "dw_in_b", ts)[0]
    d_w_in_t = jnp.concatenate([dw1a[:o2], dw1a[o2 + NOPE:o2 + NOPE + ROPE], dw1b], axis=0)
    p_in = d_w_in_t.reshape(N_DEV, -1, D_MODEL)
    p_uq = _tn_matmul(d_qp[None], cqn[None], "dw_uq", ts)[0].reshape(HEADS, HEAD_PAD, Q_LORA)[:, :NOPE + ROPE]
    dwkv = _tn_matmul(ckvn[None], d_kvp[None], "dw_ukv", ts)[0]
    dwk = dwkv[:, :hp].reshape(KV_LORA, HEADS, HEAD_PAD)[:, :, :NOPE]
    dwv = dwkv[:, hp:].reshape(KV_LORA, HEADS, HEAD_PAD)[:, :, :NOPE]
    p_ukv = jnp.transpose(jnp.concatenate([dwk, dwv], axis=2), (1, 0, 2))

    (g_late,), (r_in, r_uq, r_ukv) = _exchange(
        [_pack_late(d_sh1, d_sc1, d_g_pre_mix, d_g_q, d_g_kv)],
        [(p_in, _plain_slot), (p_uq, _plain_slot), (p_ukv, _plain_slot)], "final_exchange")
    small_params = [(b_ada, m_b_ada, v_b_ada), (g_pre_mix, m_g_pre_mix, v_g_pre_mix),
                    (g_post_mix, m_g_post_mix, v_g_post_mix), (g_pre_ffn, m_g_pre_ffn, v_g_pre_ffn),
                    (g_post_ffn, m_g_post_ffn, v_g_post_ffn), (g_q, m_g_q, v_g_q), (g_kv, m_g_kv, v_g_kv),
                    (gm_ln_g, m_gm_ln_g, v_gm_ln_g), (gm_ln_b, m_gm_ln_b, v_gm_ln_b),
                    (w_spatial, m_w_spatial, v_w_spatial), (b_spatial, m_b_spatial, v_b_spatial),
                    tuple(a.reshape(N_DEV, FF_BLK) for a in (conv_b, m_conv_b, v_conv_b))]
    small_out, loss_row, d_ada_all = _adam_small(g_small, g_late, small_params)
    small_out[11] = tuple(o.reshape(conv_b.shape) for o in small_out[11])
    loss = loss_row[0, 0]

    def big(recv, w, m, v, name):
        g, d, m2, v2 = _adam_reduce(recv, w[0], m[0], v[0], name)
        return g[None], d[None], m2[None], v2[None]

    def big_t(recv, w_t, m_t, v_t, name):
        return tuple(jnp.swapaxes(o, 0, 1)[None] for o in _adam_reduce(recv, w_t, m_t, v_t, name))

    a_in = big_t(r_in, w_in_t, m_in_t, v_in_t, "adam_w_in")
    a_uq = big_t(r_uq, w_uq_t, m_uq_t, v_uq_t, "adam_w_uq")
    a_ukv = big(r_ukv, w_ukv, m_w_ukv, v_w_ukv, "adam_w_ukv")
    a_out = big(r_out, w_out, m_w_out, v_w_out, "adam_w_out")
    a_up = big_t(r_up, w_up_t, m_up_t, v_up_t, "adam_w_up")
    a_down = big(r_down, w_down, m_w_down, v_w_down, "adam_w_down")
    ada_cols = w_ada.shape[2]
    d_ada_cols = lax.dynamic_slice(d_ada_all.reshape(N_DEV, 6 * D_MODEL), (0, my_slot * ada_cols), (N_DEV, ada_cols))
    a_ada = tuple(t[None] for t in _adam_w_ada(c_act.T, d_ada_cols, w_ada[0], m_w_ada[0], v_w_ada[0]))
    a_cw = big(r_cw, conv_w, m_conv_w, v_conv_w, "adam_conv_w")

    def small(k):
        return small_out[k]

    per_weight = [a_ada, small(0), small(1), small(2), a_in, small(5), a_uq, small(6), a_ukv, small(7), small(8),
                  small(9), small(10), a_out, small(3), small(4), a_up, a_cw, small(11), a_down]
    outs = [loss, grad_x[None]]
    for k in range(4):
        outs += [t[k] for t in per_weight]
    return tuple(outs)
```

```python
import functools

import jax
import jax.numpy as jnp
from jax import lax
from jax.experimental import pallas as pl
from jax.experimental.pallas import tpu as pltpu

F32 = jnp.float32
MXU = jnp.bfloat16

N_DEV = 8
D_MODEL = 1024
HEADS = 8
HEAD_PAD = 128
NOPE = 64
ROPE = 32
Q_LORA = 256
KV_LORA = 128
GM_WIDTH = 512
GM_DIM = 64
GM_CHUNK = 128
CHUNK_SHIFT = 6
ROPE_THETA = 10000.0
ATTN_SCALE = (NOPE + ROPE) ** -0.5
LOG2E = 1.4426950408889634
SCALE_LOG2E = ATTN_SCALE * LOG2E
Z_COLS = 1536
FF_BLK = 704
EPS = 1e-6
ADAM_LR = 0.001
ADAM_B1 = 0.9
ADAM_B2 = 0.999
ADAM_EPS = 1e-08
ADAM_WD = 0.01
ADAM_STEP = 10
VMEM_LIMIT = 56 * 1024 * 1024
MESH = pl.DeviceIdType.MESH


def _dot(a, b):
    return jnp.dot(a, b, preferred_element_type=F32)


def _dot_nt(a, b):
    return lax.dot_general(a, b, (((1,), (1,)), ((), ())), preferred_element_type=F32)


def _dot_tn(a, b):
    return lax.dot_general(a, b, (((0,), (0,)), ((), ())), preferred_element_type=F32)


def _call(body, *, name, grid, in_specs, out_specs, out_shape, scratch=(), sem=None):
    params = pltpu.CompilerParams(dimension_semantics=sem, vmem_limit_bytes=VMEM_LIMIT)
    return pl.pallas_call(body, name=name, grid=grid, in_specs=in_specs, out_specs=out_specs,
                          out_shape=out_shape, scratch_shapes=list(scratch), compiler_params=params)


def _full(shape):
    n = len(shape)
    return pl.BlockSpec(shape, lambda *_: (0,) * n)


def _rows(tm, cols, col_block=0):
    return pl.BlockSpec((tm, cols), lambda i: (i, col_block))


def _sds(shape, dtype):
    return jax.ShapeDtypeStruct(shape, dtype)


def _row(ref, k):
    return ref[pl.ds(k, 1), :]


def _rms(x):
    r = lax.rsqrt(jnp.mean(x * x, axis=-1, keepdims=True) + EPS)
    return x * r, r


def _rms_bwd(d_hat, hat, r):
    return r * (d_hat - hat * jnp.mean(d_hat * hat, axis=-1, keepdims=True))


def _rope_partner(t):
    lane = lax.broadcasted_iota(jnp.int32, t.shape, 1)
    swapped = jnp.where(lane < NOPE + ROPE // 2, -pltpu.roll(t, HEAD_PAD - ROPE // 2, 1), pltpu.roll(t, ROPE // 2, 1))
    return jnp.where((lane >= NOPE) & (lane < NOPE + ROPE), swapped, 0.0)


def _rope(t, cos, sin):
    return t * cos + _rope_partner(t) * sin


def _rope_transposed(g, cos, sin):
    return g * cos - _rope_partner(g * sin)


def _gelu(x):
    return x * (0.5 * (1.0 + jnp.tanh(0.7978845608028654 * (x + 0.044715 * (x * x * x)))))


def _gelu_grad(x):
    t = jnp.tanh(0.7978845608028654 * (x + 0.044715 * (x * x * x)))
    return 0.5 * (1.0 + t) + 0.5 * x * (1.0 - t * t) * (0.7978845608028654 * (1.0 + 3.0 * 0.044715 * (x * x)))


def _split_dot(x, mat):
    hi = x.astype(MXU)
    lo = (x - hi.astype(F32)).astype(MXU)
    return _dot(hi, mat) + _dot(lo, mat)


def _split_dot3(x, mat):
    hi = x.astype(MXU)
    r1 = x - hi.astype(F32)
    mid = r1.astype(MXU)
    lo = (r1 - mid.astype(F32)).astype(MXU)
    return (_dot(hi, mat) + _dot(mid, mat)) + _dot(lo, mat)


def _seg_matrix():
    r = lax.broadcasted_iota(jnp.int32, (GM_WIDTH, GM_WIDTH), 0) >> 6
    c = lax.broadcasted_iota(jnp.int32, (GM_WIDTH, GM_WIDTH), 1) >> 6
    return jnp.where(r == c, 1.0 / GM_DIM, 0.0).astype(MXU)


def _spatial_mask():
    i = lax.broadcasted_iota(jnp.int32, (GM_CHUNK, GM_CHUNK), 0) >> CHUNK_SHIFT
    j = lax.broadcasted_iota(jnp.int32, (GM_CHUNK, GM_CHUNK), 1) >> CHUNK_SHIFT
    return (j <= i).astype(F32)


def _my_place():
    return lax.axis_index("x"), lax.axis_index("y"), lax.axis_index("c")


def _flat(p):
    return 4 * p[0] + 2 * p[1] + p[2]


def _comm_sems(n):
    return [pltpu.SemaphoreType.DMA((7 * n,)), pltpu.SemaphoreType.DMA((7 * n,)), pltpu.SemaphoreType.DMA((n,))]


def _gather_steps(ins, outs, sems):
    send_sems, recv_sems, local_sems = sems
    n = len(ins)
    x, y, c = _my_place()
    me, sibling = (x, y, c), (x, y, 1 - c)
    chips = [(1 - x, y), (x, 1 - y), (1 - x, 1 - y)]

    def copy(a, k, block, to, src=None):
        slot = outs[a].at[_flat(block)]
        return pltpu.make_async_remote_copy(
            src_ref=slot if src is None else src, dst_ref=slot,
            send_sem=send_sems.at[7 * a + k], recv_sem=recv_sems.at[7 * a + k],
            device_id=to, device_id_type=MESH)

    def mine():
        return [pltpu.make_async_copy(ins[a], outs[a].at[_flat(me)], local_sems.at[a]) for a in range(n)]

    def first():
        cps = []
        for a in range(n):
            cps.append(copy(a, 0, me, sibling, src=ins[a]))
            cps += [copy(a, 1 + j, me, (*chip, c), src=ins[a]) for j, chip in enumerate(chips)]
        return cps

    def passed():
        return [copy(a, 4 + j, (*chip, c), sibling) for a in range(n) for j, chip in enumerate(chips)]

    def start():
        for cp in mine() + first():
            cp.start()

    def forward():
        for a in range(n):
            for j, chip in enumerate(chips):
                copy(a, 1 + j, (*chip, c), me).wait_recv()
                copy(a, 4 + j, (*chip, c), sibling).start()

    def finish():
        for a in range(n):
            copy(a, 0, sibling, me).wait_recv()
            for j, chip in enumerate(chips):
                copy(a, 4 + j, (*chip, 1 - c), me).wait_recv()
        for cp in first() + passed():
            cp.wait_send()
        for cp in mine():
            cp.wait()

    return start, forward, finish


def _scatter_steps(ins, outs, sems, slots):
    send_sems, recv_sems, local_sems = sems
    n = len(ins)
    flips = [(fx, fy, fc) for fx in (0, 1) for fy in (0, 1) for fc in (0, 1)][1:]
    me = _my_place()

    def peer(f):
        return tuple(1 - v if b else v for v, b in zip(me, f))

    def copy(a, k, arriving=False):
        p = peer(flips[k])
        return pltpu.make_async_remote_copy(
            src_ref=ins[a].at[slots[a](_flat(p))], dst_ref=outs[a].at[_flat(p if arriving else me)],
            send_sem=send_sems.at[7 * a + k], recv_sem=recv_sems.at[7 * a + k],
            device_id=p, device_id_type=MESH)

    def mine():
        return [pltpu.make_async_copy(ins[a].at[slots[a](_flat(me))], outs[a].at[_flat(me)], local_sems.at[a])
                for a in range(n)]

    def start():
        for cp in mine() + [copy(a, k) for a in range(n) for k in range(7)]:
            cp.start()

    def finish():
        for a in range(n):
            for k in range(7):
                copy(a, k, arriving=True).wait_recv()
        for a in range(n):
            for k in range(7):
                copy(a, k).wait_send()
        for cp in mine():
            cp.wait()

    return start, finish


def _plain_slot(j):
    return (j,)


def _scatter_out_shape(arr, slot):
    return _sds((N_DEV,) + arr.shape[len(slot(0)):], arr.dtype)


def _exchange(gathered, scattered, name):
    ng, ns = len(gathered), len(scattered)
    slots = [slot for _, slot in scattered]

    def body(*refs):
        g_in, s_in = refs[:ng], refs[ng:ng + ns]
        g_out, s_out = refs[ng + ns:2 * ng + ns], refs[2 * ng + ns:2 * (ng + ns)]
        sems = refs[2 * (ng + ns):]
        g_start, g_forward, g_finish = _gather_steps(g_in, g_out, sems[:3])
        s_start, s_finish = _scatter_steps(s_in, s_out, sems[3:], slots)
        g_start()
        s_start()
        g_forward()
        g_finish()
        s_finish()

    any_spec = pl.BlockSpec(memory_space=pl.ANY)
    outs = pl.pallas_call(
        body, name=name,
        in_specs=[any_spec] * (ng + ns), out_specs=[any_spec] * (ng + ns),
        out_shape=[_sds((N_DEV,) + a.shape, a.dtype) for a in gathered]
        + [_scatter_out_shape(a, slot) for a, slot in scattered],
        scratch_shapes=_comm_sems(max(ng, 1)) + _comm_sems(max(ns, 1)),
    )(*gathered, *[a for a, _ in scattered])
    return outs[:ng], outs[ng:]


def _ada_fwd(c_all, w_ada):
    def body(c_ref, w_ref, part_ref, act_ref):
        cv = c_ref[...]
        act = cv * jax.nn.sigmoid(cv)
        act_ref[...] = act
        part_ref[...] = _dot(act.astype(MXU), w_ref[...].astype(MXU))

    cols = w_ada.shape[1]
    return _call(body, name="ada_fwd", grid=(1,),
                 in_specs=[_full(c_all.shape), _full(w_ada.shape)],
                 out_specs=[_full((N_DEV, cols)), _full(c_all.shape)],
                 out_shape=[_sds((N_DEV, cols), F32), _sds(c_all.shape, F32)])(c_all, w_ada)


def _mix_in_fwd(x, ada_raw, ada_b, g_pre, w1, g_q, g_kv, wq, wkv, cos_t, sin_t, tm):
    s = x.shape[0]

    def body(x_ref, ar_ref, ab_ref, g_ref, w1_ref, gq_ref, gkv_ref, wq_ref, wkv_ref, cos_ref, sin_ref,
             h1_ref, z_ref, qp_ref, kp_ref, vp_ref, cqn_ref, ckvn_ref):
        sh = _row(ar_ref, 0) + _row(ab_ref, 0)
        sc = _row(ar_ref, 1) + _row(ab_ref, 1)
        xn, _ = _rms(x_ref[...])
        hb = ((xn * g_ref[...]) * (1.0 + sc) + sh).astype(MXU)
        h1_ref[...] = hb
        z = _dot_nt(hb, w1_ref[...])
        z_ref[...] = z
        cos, sin = cos_ref[...], sin_ref[...]
        cqn = (_rms(z[:, :Q_LORA])[0] * gq_ref[...]).astype(MXU)
        ckvn = (_rms(z[:, Q_LORA:Q_LORA + KV_LORA])[0] * gkv_ref[...]).astype(MXU)
        cqn_ref[...] = cqn
        ckvn_ref[...] = ckvn
        q = _dot_nt(cqn, wq_ref[...])
        kv = _dot(ckvn, wkv_ref[...])
        k_rope = _rope(z[:, Q_LORA + KV_LORA:Q_LORA + KV_LORA + HEAD_PAD], cos, sin)
        for h in range(HEADS):
            blk = slice(h * HEAD_PAD, (h + 1) * HEAD_PAD)
            qp_ref[:, blk] = _rope(q[:, blk], cos, sin).astype(MXU)
            kp_ref[:, blk] = (kv[:, blk] + k_rope).astype(MXU)
        v_lane = lax.broadcasted_iota(jnp.int32, (tm, HEADS * HEAD_PAD), 1) & (HEAD_PAD - 1)
        vp_ref[...] = jnp.where(v_lane == NOPE, 1.0, kv[:, HEADS * HEAD_PAD:]).astype(MXU)

    hp = HEADS * HEAD_PAD
    return _call(
        body, name="mix_in_fwd", grid=(s // tm,), sem=("parallel",),
        in_specs=[_rows(tm, D_MODEL), _full(ada_raw.shape), _full(ada_b.shape), _full(g_pre.shape), _full(w1.shape),
                  _full(g_q.shape), _full(g_kv.shape), _full(wq.shape), _full(wkv.shape),
                  _rows(tm, HEAD_PAD), _rows(tm, HEAD_PAD)],
        out_specs=[_rows(tm, D_MODEL), _rows(tm, Z_COLS), _rows(tm, hp), _rows(tm, hp), _rows(tm, hp),
                   _rows(tm, Q_LORA), _rows(tm, KV_LORA)],
        out_shape=[_sds((s, D_MODEL), MXU), _sds((s, Z_COLS), F32), _sds((s, hp), MXU), _sds((s, hp), MXU),
                   _sds((s, hp), MXU), _sds((s, Q_LORA), MXU), _sds((s, KV_LORA), MXU)],
    )(x, ada_raw, ada_b, g_pre, w1, g_q, g_kv, wq, wkv, cos_t, sin_t)


def _gm_norm(zv, seg):
    gv = _gelu(zv)
    cen = gv - _split_dot(gv, seg)
    rstd = lax.rsqrt(_split_dot(cen * cen, seg) + EPS)
    return gv, cen * rstd, rstd


def _gm_pairs(rows):
    first = lax.broadcasted_iota(jnp.int32, (rows, 2 * GM_DIM), 1) < GM_DIM
    return [(slice(p * 2 * GM_DIM, (p + 1) * 2 * GM_DIM), first) for p in range(HEADS // 2)]


def _gm_mix(wm, vb, rows, transposed=False):
    dot = _dot_tn if transposed else _dot
    return jnp.concatenate([jnp.where(first, dot(wm[2 * p], vb[:, lanes]), dot(wm[2 * p + 1], vb[:, lanes]))
                            for p, (lanes, first) in enumerate(_gm_pairs(rows))], axis=1)


def _gmlp_fwd(z, ln_g, ln_b, w_sp, bias_exp, tm):
    s = z.shape[0]
    nblk = tm // GM_CHUNK

    def body(zu_ref, zv_ref, lg_ref, lb_ref, w_ref, be_ref, sgu_ref):
        seg = _seg_matrix()
        mask = _spatial_mask()
        wm = [(w_ref[h] * mask).astype(MXU) for h in range(HEADS)]
        gu = _gelu(zu_ref[...])
        _, vhat, _ = _gm_norm(zv_ref[...], seg)
        vln = (vhat * lg_ref[...] + lb_ref[...]).astype(MXU)
        for n in range(nblk):
            rows = slice(n * GM_CHUNK, (n + 1) * GM_CHUNK)
            mixed = _gm_mix(wm, vln[rows], GM_CHUNK) + be_ref[...]
            sgu_ref[rows, :] = (gu[rows] * mixed).astype(MXU)

    return _call(
        body, name="gmlp_fwd", grid=(s // tm,), sem=("parallel",),
        in_specs=[_rows(tm, GM_WIDTH, 1), _rows(tm, GM_WIDTH, 2), _full(ln_g.shape), _full(ln_b.shape),
                  _full(w_sp.shape), _full(bias_exp.shape)],
        out_specs=_rows(tm, GM_WIDTH), out_shape=_sds((s, GM_WIDTH), MXU),
    )(z, z, ln_g, ln_b, w_sp, bias_exp)


def _chunk_mask(n_q, n_k, q_off):
    qc = (q_off + lax.broadcasted_iota(jnp.int32, (n_q, n_k), 0)) >> CHUNK_SHIFT
    kc = lax.broadcasted_iota(jnp.int32, (n_q, n_k), 1) >> CHUNK_SHIFT
    return kc <= qc


NEG_BIG = -1e30
ATTN_HEADS_PER_STEP = 2


def _attn_fwd(qp, kp, vp, tq, gathered):
    s = qp.shape[0]
    nq = s // tq
    hb = ATTN_HEADS_PER_STEP
    groups = HEADS // hb
    width = hb * HEAD_PAD
    ng = len(gathered)

    def body(q_ref, k_ref, v_ref, *rest):
        g_in, (o_ref, lse_ref), g_out = rest[:ng], rest[ng:ng + 2], rest[ng + 2:2 * ng + 2]
        m_sc, acc_sc = rest[2 * ng + 2:2 * ng + 4]
        g_start, g_forward, g_finish = _gather_steps(g_in, g_out, rest[2 * ng + 4:])
        g, i = pl.program_id(0), pl.program_id(1)
        pl.when((g == 0) & (i == 0))(g_start)
        pl.when((g == groups - 1) & (i == 0))(g_forward)
        m_sc[...] = jnp.full(m_sc.shape, NEG_BIG, F32)
        acc_sc[...] = jnp.zeros(acc_sc.shape, F32)

        def tile(j, masked):
            rows = pl.ds(pl.multiple_of(j * tq, tq), tq)
            for hh in range(hb):
                lanes = slice(hh * HEAD_PAD, (hh + 1) * HEAD_PAD)
                sc = _dot_nt(q_ref[:, lanes], k_ref[rows, lanes])
                if masked:
                    sc = jnp.where(_chunk_mask(tq, tq, 0), sc, NEG_BIG)
                blocks = [sc[:, b * 128:(b + 1) * 128] for b in range(tq // 128)]
                m_prev = m_sc[hh]
                m_tile = jnp.max(functools.reduce(jnp.maximum, blocks), axis=-1, keepdims=True)
                m_new = jnp.maximum(m_prev, m_tile)
                alpha = jnp.exp2((m_prev - m_new) * SCALE_LOG2E)
                p = jnp.concatenate([jnp.exp2((b - m_new) * SCALE_LOG2E) for b in blocks], axis=1).astype(MXU)
                acc_sc[hh] = alpha * acc_sc[hh] + _dot(p, v_ref[rows, lanes])
                m_sc[hh] = m_new

        def off_diagonal_pair(p, carry):
            tile(2 * p, False)
            tile(2 * p + 1, False)
            return carry

        lax.fori_loop(0, i // 2, off_diagonal_pair, 0)

        @pl.when(i % 2 == 1)
        def _():
            tile(i - 1, False)

        tile(i, True)
        for hh in range(hb):
            lanes = slice(hh * HEAD_PAD, (hh + 1) * HEAD_PAD)
            acc = acc_sc[hh]
            denom = acc[:, NOPE:NOPE + 1]
            o_ref[:, lanes] = (acc / denom).astype(MXU)
            lse_ref[hh] = m_sc[hh][:, :1] * SCALE_LOG2E + jnp.log(denom) * LOG2E
        pl.when((g == groups - 1) & (i == nq - 1))(g_finish)

    q_spec = pl.BlockSpec((tq, width), lambda g, i: (i, g))
    kv_spec = pl.BlockSpec((s, width), lambda g, i: (0, g))
    any_spec = pl.BlockSpec(memory_space=pl.ANY)
    outs = _call(
        body, name="attn_fwd", grid=(groups, nq), sem=("arbitrary", "arbitrary"),
        in_specs=[q_spec, kv_spec, kv_spec] + [any_spec] * ng,
        out_specs=[q_spec, pl.BlockSpec((hb, tq, 1), lambda g, i: (g, i, 0))] + [any_spec] * ng,
        out_shape=[_sds(qp.shape, MXU), _sds((HEADS, s, 1), F32)]
        + [_sds((N_DEV,) + a.shape, a.dtype) for a in gathered],
        scratch=[pltpu.VMEM((hb, tq, HEAD_PAD), F32), pltpu.VMEM((hb, tq, HEAD_PAD), F32)] + _comm_sems(ng),
    )(qp, kp, vp, *gathered)
    return outs[0], outs[1], outs[2:]


def _out_proj_fwd(o_pad, sgu, wo, x, ada_raw, ada_b, g_post_mix, g_pre_ffn, tm):
    s = x.shape[0]
    hp = HEADS * HEAD_PAD

    def body(o_ref, sgu_ref, wo_ref, x_ref, ar_ref, ab_ref, gpm_ref, gpf_ref, m_ref, x2_ref, h2_ref):
        gt1 = _row(ar_ref, 2) + _row(ab_ref, 2)
        sh2 = _row(ar_ref, 3) + _row(ab_ref, 3)
        sc2 = _row(ar_ref, 4) + _row(ab_ref, 4)
        m = _dot(o_ref[...], wo_ref[pl.ds(0, hp), :]) + _dot(sgu_ref[...], wo_ref[pl.ds(hp, GM_WIDTH), :])
        m_ref[...] = m
        x2 = x_ref[...] + gt1 * (_rms(m)[0] * gpm_ref[...])
        x2_ref[...] = x2
        h2_ref[...] = ((_rms(x2)[0] * gpf_ref[...]) * (1.0 + sc2) + sh2).astype(MXU)

    return _call(
        body, name="out_proj_fwd", grid=(s // tm,), sem=("parallel",),
        in_specs=[_rows(tm, hp), _rows(tm, GM_WIDTH), _full(wo.shape), _rows(tm, D_MODEL), _full(ada_raw.shape),
                  _full(ada_b.shape), _full(g_post_mix.shape), _full(g_pre_ffn.shape)],
        out_specs=[_rows(tm, D_MODEL)] * 3,
        out_shape=[_sds((s, D_MODEL), F32), _sds((s, D_MODEL), F32), _sds((s, D_MODEL), MXU)],
    )(o_pad, sgu, wo, x, ada_raw, ada_b, g_post_mix, g_pre_ffn)


def _conv(u, halo, cw_ref, cb_ref):
    ext = jnp.concatenate([halo, u], axis=0)
    m1, m2 = pltpu.roll(ext, 1, 0)[8:], pltpu.roll(ext, 2, 0)[8:]
    return cb_ref[0] + ((m2 * cw_ref[0, pl.ds(0, 1), :] + m1 * cw_ref[0, pl.ds(1, 1), :]) + u * cw_ref[0, pl.ds(2, 1), :])


ROW_SUB = 256


def _sub_blocks(tm):
    return [slice(r, r + ROW_SUB) for r in range(0, tm, ROW_SUB)]


def _ffn_up_fwd(h2, w_up, conv_w, conv_b, tm):
    s = h2.shape[0]
    half = N_DEV // 2

    def body(h_ref, wa_ref, wb_ref, cwa_ref, cwb_ref, cba_ref, cbb_ref,
             ua_ref, ub_ref, ya_ref, yb_ref, act_ref, halo_a, halo_b, wa_t, wb_t):
        i = pl.program_id(1)

        @pl.when(i == 0)
        def _():
            halo_a[...] = jnp.zeros(halo_a.shape, F32)
            halo_b[...] = jnp.zeros(halo_b.shape, F32)
            wa_t[...] = wa_ref[0].T
            wb_t[...] = wb_ref[0].T

        ha, hb = halo_a[...], halo_b[...]
        for rows in _sub_blocks(tm):
            h = h_ref[rows, :]
            ua = _dot(h, wa_t[...])
            ub = _dot(h, wb_t[...])
            ua_ref[0, rows, :] = ua
            ub_ref[0, rows, :] = ub
            ya = _conv(ua, ha, cwa_ref, cba_ref)
            yb = _conv(ub, hb, cwb_ref, cbb_ref)
            ya_ref[0, rows, :] = ya
            yb_ref[0, rows, :] = yb
            ha, hb = ua[ROW_SUB - 8:], ub[ROW_SUB - 8:]
            act_ref[0, rows, :] = ((ya * jax.nn.sigmoid(ya)) * yb).astype(MXU)
        halo_a[...] = ha
        halo_b[...] = hb

    def blk(shape, off):
        return pl.BlockSpec(shape, lambda j, i: (j + off, 0, 0))

    def tok(off=0):
        return pl.BlockSpec((1, tm, FF_BLK), lambda j, i: (j + off, i, 0))

    return _call(
        body, name="ffn_up_fwd", grid=(half, s // tm), sem=("parallel", "arbitrary"),
        in_specs=[pl.BlockSpec((tm, D_MODEL), lambda j, i: (i, 0)),
                  blk((1, FF_BLK, D_MODEL), 0), blk((1, FF_BLK, D_MODEL), half),
                  blk((1, 3, FF_BLK), 0), blk((1, 3, FF_BLK), half), blk((1, 1, FF_BLK), 0), blk((1, 1, FF_BLK), half)],
        out_specs=[tok()] * 5,
        out_shape=[_sds((half, s, FF_BLK), F32)] * 4 + [_sds((half, s, FF_BLK), MXU)],
        scratch=[pltpu.VMEM((8, FF_BLK), F32), pltpu.VMEM((8, FF_BLK), F32),
                 pltpu.VMEM((D_MODEL, FF_BLK), MXU), pltpu.VMEM((D_MODEL, FF_BLK), MXU)],
    )(h2, w_up, w_up, conv_w, conv_w, conv_b, conv_b)


def _ffn_down_fwd(act, wd, x2, target, ada_raw, ada_b, g_post_ffn, tm):
    s = x2.shape[0]
    half = N_DEV // 2

    def body(act_ref, wd_ref, x2_ref, t_ref, ar_ref, ab_ref, g_ref, dout_ref, df_ref, loss_ref, dgt_ref, dg_ref):
        i = pl.program_id(0)

        @pl.when(i == 0)
        def _():
            loss_ref[...] = jnp.zeros(loss_ref.shape, F32)
            dgt_ref[...] = jnp.zeros(dgt_ref.shape, F32)
            dg_ref[...] = jnp.zeros(dg_ref.shape, F32)

        gt2 = _row(ar_ref, 5) + _row(ab_ref, 5)
        g = g_ref[...]
        for rows in _sub_blocks(tm):
            f = _dot(act_ref[0, rows, :], wd_ref[0])
            for j in range(1, half):
                f = f + _dot(act_ref[j, rows, :], wd_ref[j])
            fhat, rf = _rms(f)
            fn = fhat * g
            err = (x2_ref[rows, :] + gt2 * fn) - t_ref[rows, :]
            loss_ref[...] += 0.5 * jnp.sum(jnp.mean(err * err, axis=-1, keepdims=True))
            d_out = err * (1.0 / D_MODEL)
            dout_ref[rows, :] = d_out
            dgt_ref[...] += jnp.sum(d_out * fn, axis=0, keepdims=True)
            d_fn = d_out * gt2
            dg_ref[...] += jnp.sum(d_fn * fhat, axis=0, keepdims=True)
            df_ref[rows, :] = _rms_bwd(d_fn * g, fhat, rf).astype(MXU)

    vec = pl.BlockSpec((1, D_MODEL), lambda i: (0, 0))
    return _call(
        body, name="ffn_down_fwd", grid=(s // tm,), sem=("arbitrary",),
        in_specs=[pl.BlockSpec((half, tm, FF_BLK), lambda i: (0, i, 0)), _full(wd.shape), _rows(tm, D_MODEL),
                  _rows(tm, D_MODEL), _full(ada_raw.shape), _full(ada_b.shape), _full(g_post_ffn.shape)],
        out_specs=[_rows(tm, D_MODEL), _rows(tm, D_MODEL), pl.BlockSpec((1, 128), lambda i: (0, 0)), vec, vec],
        out_shape=[_sds((s, D_MODEL), F32), _sds((s, D_MODEL), MXU), _sds((1, 128), F32),
                   _sds((1, D_MODEL), F32), _sds((1, D_MODEL), F32)],
    )(act, wd, x2, target, ada_raw, ada_b, g_post_ffn)


def _ffn_down_bwd(d_f, wd, up_a, up_b, y_a, y_b, conv_w, act, h2, tm):
    s = d_f.shape[0]
    half = N_DEV // 2
    nt = s // tm

    def body(df_ref, wd_ref, ua_ref, ub_ref, ya_ref, yb_ref, cwa_ref, cwb_ref, act_ref, h2_ref,
             dup_ref, dcw_ref, dcb_ref, pd_ref, pu_ref, next_a, next_b, acc_d, acc_a, acc_b):
        i = pl.program_id(1)

        @pl.when(i == 0)
        def _():
            next_a[...] = jnp.zeros(next_a.shape, F32)
            next_b[...] = jnp.zeros(next_b.shape, F32)
            dcw_ref[...] = jnp.zeros(dcw_ref.shape, F32)
            dcb_ref[...] = jnp.zeros(dcb_ref.shape, F32)
            for acc in (acc_d, acc_a, acc_b):
                acc[...] = jnp.zeros(acc.shape, F32)

        def conv_bwd(d_y, u, nxt, cw_ref, part, rows):
            ext = jnp.concatenate([d_y, nxt], axis=0)
            p1 = pltpu.roll(ext, ROW_SUB + 7, 0)[:ROW_SUB]
            p2 = pltpu.roll(ext, ROW_SUB + 6, 0)[:ROW_SUB]
            d_u = (d_y * cw_ref[0, pl.ds(2, 1), :] + p1 * cw_ref[0, pl.ds(1, 1), :]) + p2 * cw_ref[0, pl.ds(0, 1), :]
            dup_ref[0, part, rows, :] = d_u.astype(MXU)
            dcb_ref[0, part] += jnp.sum(d_y, axis=0, keepdims=True)
            dcw_ref[0, part, pl.ds(0, 1), :] += jnp.sum(p2 * u, axis=0, keepdims=True)
            dcw_ref[0, part, pl.ds(1, 1), :] += jnp.sum(p1 * u, axis=0, keepdims=True)
            dcw_ref[0, part, pl.ds(2, 1), :] += jnp.sum(d_y * u, axis=0, keepdims=True)
            return d_y[:8]

        nxa, nxb = next_a[...], next_b[...]
        for rows in reversed(_sub_blocks(tm)):
            d_act = _dot_nt(df_ref[rows, :], wd_ref[0])
            ya, yb = ya_ref[0, rows, :], yb_ref[0, rows, :]
            sig = jax.nn.sigmoid(ya)
            d_ya = d_act * yb * (sig * (1.0 + ya * (1.0 - sig)))
            d_yb = d_act * (ya * sig)
            nxa = conv_bwd(d_ya, ua_ref[0, rows, :], nxa, cwa_ref, 0, rows)
            nxb = conv_bwd(d_yb, ub_ref[0, rows, :], nxb, cwb_ref, 1, rows)
        next_a[...] = nxa
        next_b[...] = nxb
        acc_d[...] += _dot_tn(act_ref[0], df_ref[...])
        acc_a[...] += _dot_tn(dup_ref[0, 0], h2_ref[...])
        acc_b[...] += _dot_tn(dup_ref[0, 1], h2_ref[...])

        @pl.when(i == nt - 1)
        def _():
            pd_ref[0] = acc_d[...].astype(MXU)
            pu_ref[0, 0] = acc_a[...].astype(MXU)
            pu_ref[0, 1] = acc_b[...].astype(MXU)

    def rev(i):
        return nt - 1 - i

    def blk(shape, off):
        return pl.BlockSpec(shape, lambda j, i: (j + off, 0, 0))

    tok = pl.BlockSpec((1, tm, FF_BLK), lambda j, i: (j, rev(i), 0))
    acc3 = pl.BlockSpec((1, 2, 3, FF_BLK), lambda j, i: (j, 0, 0, 0))
    acc1 = pl.BlockSpec((1, 2, 1, FF_BLK), lambda j, i: (j, 0, 0, 0))
    return _call(
        body, name="ffn_down_bwd", grid=(half, nt), sem=("parallel", "arbitrary"),
        in_specs=[pl.BlockSpec((tm, D_MODEL), lambda j, i: (rev(i), 0)), blk((1, FF_BLK, D_MODEL), 0),
                  tok, tok, tok, tok, blk((1, 3, FF_BLK), 0), blk((1, 3, FF_BLK), half),
                  tok, pl.BlockSpec((tm, D_MODEL), lambda j, i: (rev(i), 0))],
        out_specs=[pl.BlockSpec((1, 2, tm, FF_BLK), lambda j, i: (j, 0, rev(i), 0)), acc3, acc1,
                   pl.BlockSpec((1, FF_BLK, D_MODEL), lambda j, i: (j, 0, 0)),
                   pl.BlockSpec((1, 2, FF_BLK, D_MODEL), lambda j, i: (j, 0, 0, 0))],
        out_shape=[_sds((half, 2, s, FF_BLK), MXU), _sds((half, 2, 3, FF_BLK), F32), _sds((half, 2, 1, FF_BLK), F32),
                   _sds((half, FF_BLK, D_MODEL), MXU), _sds((half, 2, FF_BLK, D_MODEL), MXU)],
        scratch=[pltpu.VMEM((8, FF_BLK), F32), pltpu.VMEM((8, FF_BLK), F32)]
        + [pltpu.VMEM((FF_BLK, D_MODEL), F32)] * 3,
    )(d_f, wd, up_a, up_b, y_a, y_b, conv_w, conv_w, act, h2)


def _ffn_up_bwd(d_up, w_up, x2, m, d_out, ada_raw, ada_b, g_pre_ffn, g_post_mix, tm):
    s = x2.shape[0]
    half = N_DEV // 2

    def body(dup_ref, w_ref, x2_ref, m_ref, dout_ref, ar_ref, ab_ref, gpf_ref, gpm_ref,
             dx_ref, dm_ref, dsh_ref, dsc_ref, dgpf_ref, dgt1_ref, dgpm_ref):
        i = pl.program_id(0)

        @pl.when(i == 0)
        def _():
            for r in (dsh_ref, dsc_ref, dgpf_ref, dgt1_ref, dgpm_ref):
                r[...] = jnp.zeros(r.shape, F32)

        gt1 = _row(ar_ref, 2) + _row(ab_ref, 2)
        sc2 = _row(ar_ref, 4) + _row(ab_ref, 4)
        gpf, gpm = gpf_ref[...], gpm_ref[...]
        d_h2 = _dot(dup_ref[0, 0], w_ref[0])
        for j in range(1, half):
            d_h2 = d_h2 + _dot(dup_ref[j, 0], w_ref[j])
        for j in range(half):
            d_h2 = d_h2 + _dot(dup_ref[j, 1], w_ref[half + j])
        x2n, r2 = _rms(x2_ref[...])
        dsh_ref[...] += jnp.sum(d_h2, axis=0, keepdims=True)
        dsc_ref[...] += jnp.sum(d_h2 * (x2n * gpf), axis=0, keepdims=True)
        d_mod = d_h2 * (1.0 + sc2)
        dgpf_ref[...] += jnp.sum(d_mod * x2n, axis=0, keepdims=True)
        d_x2 = dout_ref[...] + _rms_bwd(d_mod * gpf, x2n, r2)
        dx_ref[...] = d_x2
        mhat, rm = _rms(m_ref[...])
        dgt1_ref[...] += jnp.sum(d_x2 * (mhat * gpm), axis=0, keepdims=True)
        d_mn = d_x2 * gt1
        dgpm_ref[...] += jnp.sum(d_mn * mhat, axis=0, keepdims=True)
        dm_ref[...] = _rms_bwd(d_mn * gpm, mhat, rm).astype(MXU)

    vec = pl.BlockSpec((1, D_MODEL), lambda i: (0, 0))
    tok = pl.BlockSpec((half, 2, tm, FF_BLK), lambda i: (0, 0, i, 0))
    return _call(
        body, name="ffn_up_bwd", grid=(s // tm,), sem=("arbitrary",),
        in_specs=[tok, _full(w_up.shape), _rows(tm, D_MODEL), _rows(tm, D_MODEL), _rows(tm, D_MODEL),
                  _full(ada_raw.shape), _full(ada_b.shape), _full(g_pre_ffn.shape), _full(g_post_mix.shape)],
        out_specs=[_rows(tm, D_MODEL), _rows(tm, D_MODEL), vec, vec, vec, vec, vec],
        out_shape=[_sds((s, D_MODEL), F32), _sds((s, D_MODEL), MXU)] + [_sds((1, D_MODEL), F32)] * 5,
    )(d_up, w_up, x2, m, d_out, ada_raw, ada_b, g_pre_ffn, g_post_mix)


def _out_proj_bwd(d_m, wo, o_pad, tm):
    s = d_m.shape[0]
    hp = HEADS * HEAD_PAD

    def body(dm_ref, wo_ref, o_ref, do_ref, dsgu_ref, delta_ref):
        d_cat = _dot_nt(dm_ref[...], wo_ref[...])
        d_o = d_cat[:, :hp]
        do_ref[...] = d_o.astype(MXU)
        dsgu_ref[...] = d_cat[:, hp:]
        prod = d_o * o_ref[...].astype(F32)
        for h in range(HEADS):
            delta_ref[h] = jnp.sum(prod[:, h * HEAD_PAD:(h + 1) * HEAD_PAD], axis=-1, keepdims=True)

    return _call(
        body, name="out_proj_bwd", grid=(s // tm,), sem=("parallel",),
        in_specs=[_rows(tm, D_MODEL), _full(wo.shape), _rows(tm, hp)],
        out_specs=[_rows(tm, hp), _rows(tm, GM_WIDTH), pl.BlockSpec((HEADS, tm, 1), lambda i: (0, i, 0))],
        out_shape=[_sds((s, hp), MXU), _sds((s, GM_WIDTH), F32), _sds((HEADS, s, 1), F32)],
    )(d_m, wo, o_pad)


def _attn_bwd(qp, kp, vp, d_o, lse, delta, tq, scattered, gathered):
    s = qp.shape[0]
    nq = s // tq
    hb = ATTN_HEADS_PER_STEP
    groups = HEADS // hb
    width = hb * HEAD_PAD
    ns, ng = len(scattered), len(gathered)
    nc = ns + ng
    slots = [slot for _, slot in scattered]

    def body(q_ref, k_ref, v_ref, do_ref, lse_ref, dl_ref, *rest):
        c_in, (dq_ref, dk_ref, dv_ref), c_out = rest[:nc], rest[nc:nc + 3], rest[nc + 3:2 * nc + 3]
        dk_sc, dv_sc = rest[2 * nc + 3:2 * nc + 5]
        sems = rest[2 * nc + 5:]
        s_start, s_finish = _scatter_steps(c_in[:ns], c_out[:ns], sems[:3], slots)
        g_start, g_forward, g_finish = _gather_steps(c_in[ns:], c_out[ns:], sems[3:])
        g, j = pl.program_id(0), pl.program_id(1)

        @pl.when((g == 0) & (j == 0))
        def _():
            s_start()
            g_start()

        pl.when((g == groups - 1) & (j == 0))(g_forward)

        @pl.when(j == 0)
        def _():
            dq_ref[...] = jnp.zeros(dq_ref.shape, F32)

        dk_sc[...] = jnp.zeros(dk_sc.shape, F32)
        dv_sc[...] = jnp.zeros(dv_sc.shape, F32)

        def tile(i, masked):
            rows = pl.ds(pl.multiple_of(i * tq, tq), tq)
            for hh in range(hb):
                lanes = slice(hh * HEAD_PAD, (hh + 1) * HEAD_PAD)
                q, do, k = q_ref[rows, lanes], do_ref[rows, lanes], k_ref[:, lanes]
                sc = _dot_nt(q, k)
                if masked:
                    sc = jnp.where(_chunk_mask(tq, tq, 0), sc, NEG_BIG)
                p = jnp.exp2(sc * SCALE_LOG2E - lse_ref[hh, rows, :])
                dv_sc[hh] += _dot_tn(p.astype(MXU), do)
                dp = _dot_nt(do, v_ref[:, lanes])
                ds = (p * (dp - dl_ref[hh, rows, :])).astype(MXU)
                dk_sc[hh] += _dot_tn(ds, q)
                dq_ref[rows, lanes] += _dot(ds, k) * ATTN_SCALE

        def off_diagonal_pair(p, carry):
            tile(j + 1 + 2 * p, False)
            tile(j + 2 + 2 * p, False)
            return carry

        below = nq - 1 - j
        tile(j, True)
        lax.fori_loop(0, below // 2, off_diagonal_pair, 0)

        @pl.when(below % 2 == 1)
        def _():
            tile(nq - 1, False)
        for hh in range(hb):
            lanes = slice(hh * HEAD_PAD, (hh + 1) * HEAD_PAD)
            dk_ref[:, lanes] = dk_sc[hh] * ATTN_SCALE
            dv_ref[:, lanes] = dv_sc[hh]
        @pl.when((g == groups - 1) & (j == nq - 1))
        def _():
            g_finish()
            s_finish()

    seq_spec = pl.BlockSpec((s, width), lambda g, j: (0, g))
    kv_spec = pl.BlockSpec((tq, width), lambda g, j: (j, g))
    col_spec = pl.BlockSpec((hb, s, 1), lambda g, j: (g, 0, 0))
    any_spec = pl.BlockSpec(memory_space=pl.ANY)
    outs = _call(
        body, name="attn_bwd", grid=(groups, nq), sem=("arbitrary", "arbitrary"),
        in_specs=[seq_spec, kv_spec, kv_spec, seq_spec, col_spec, col_spec] + [any_spec] * nc,
        out_specs=[seq_spec, kv_spec, kv_spec] + [any_spec] * nc,
        out_shape=[_sds(qp.shape, F32), _sds(qp.shape, F32), _sds(qp.shape, F32)]
        + [_scatter_out_shape(a, slot) for a, slot in scattered]
        + [_sds((N_DEV,) + a.shape, a.dtype) for a in gathered],
        scratch=[pltpu.VMEM((hb, tq, HEAD_PAD), F32), pltpu.VMEM((hb, tq, HEAD_PAD), F32)]
        + _comm_sems(ns) + _comm_sems(ng),
    )(qp, kp, vp, d_o, lse, delta, *[a for a, _ in scattered], *gathered)
    return outs[0], outs[1], outs[2], outs[3:3 + ns], outs[3 + ns:]


def _gmlp_bwd(z, d_sgu, ln_g, ln_b, w_sp, bias_exp, tm):
    s = z.shape[0]
    nblk = tm // GM_CHUNK

    def body(zu_ref, zv_ref, dsgu_ref, lg_ref, lb_ref, w_ref, be_ref,
             dguv_ref, dws_ref, dbs_ref, dlg_ref, dlb_ref, dbe_sc, dvln_sc, dlg_sc, dlb_sc):
        i = pl.program_id(0)

        @pl.when(i == 0)
        def _():
            for r in (dws_ref, dlg_sc, dlb_sc, dbe_sc):
                r[...] = jnp.zeros(r.shape, F32)

        seg = _seg_matrix()
        mask = _spatial_mask()
        wm = [(w_ref[h] * mask).astype(MXU) for h in range(HEADS)]
        zu, zv = zu_ref[...], zv_ref[...]
        gu = _gelu(zu)
        _, vhat, rstd = _gm_norm(zv, seg)
        lg = lg_ref[...]
        vln = (vhat * lg + lb_ref[...]).astype(MXU)
        d_sgu = dsgu_ref[...]
        for n in range(nblk):
            rows = slice(n * GM_CHUNK, (n + 1) * GM_CHUNK)
            vb = vln[rows]
            mixed = _gm_mix(wm, vb, GM_CHUNK) + be_ref[...]
            d_mixed = d_sgu[rows] * gu[rows]
            dguv_ref[rows, pl.ds(0, GM_WIDTH)] = ((d_sgu[rows] * mixed) * _gelu_grad(zu[rows])).astype(MXU)
            dbe_sc[...] += d_mixed
            dmb = d_mixed.astype(MXU)
            for p, (lanes, first) in enumerate(_gm_pairs(GM_CHUNK)):
                dm_pair, v_pair = dmb[:, lanes], vb[:, lanes]
                zero = jnp.zeros_like(dm_pair)
                dws_ref[2 * p] += _dot_nt(jnp.where(first, dm_pair, zero), v_pair)
                dws_ref[2 * p + 1] += _dot_nt(jnp.where(first, zero, dm_pair), v_pair)
            dvln_sc[rows, :] = _gm_mix(wm, dmb, GM_CHUNK, transposed=True)
        d_vln = dvln_sc[...]
        dlg_sc[...] += jnp.sum(d_vln * vhat, axis=0, keepdims=True)
        dlb_sc[...] += jnp.sum(d_vln, axis=0, keepdims=True)
        d_vhat = d_vln * lg
        d_gv = rstd * ((d_vhat - _split_dot(d_vhat, seg)) - vhat * _split_dot(d_vhat * vhat, seg))
        dguv_ref[:, pl.ds(GM_WIDTH, GM_WIDTH)] = (d_gv * _gelu_grad(zv)).astype(MXU)

        @pl.when(i == pl.num_programs(0) - 1)
        def _():
            for h in range(HEADS):
                dws_ref[h] = dws_ref[h] * mask
            hrow = lax.broadcasted_iota(jnp.int32, (HEADS, GM_WIDTH), 0)
            hlane = lax.broadcasted_iota(jnp.int32, (HEADS, GM_WIDTH), 1) >> 6
            ind = jnp.where(hrow == hlane, 1.0, 0.0).astype(MXU)
            acc = dbe_sc[...]
            hi = acc.astype(MXU)
            lo = (acc - hi.astype(F32)).astype(MXU)
            dbs_ref[...] = _dot_nt(ind, hi) + _dot_nt(ind, lo)
            pick = (lax.broadcasted_iota(jnp.int32, (GM_WIDTH, GM_DIM), 0) & (GM_DIM - 1)
                    == lax.broadcasted_iota(jnp.int32, (GM_WIDTH, GM_DIM), 1))
            pick = jnp.where(pick, 1.0, 0.0).astype(MXU)
            for src, dst in ((dlg_sc, dlg_ref), (dlb_sc, dlb_ref)):
                spread = jnp.where(hrow == hlane, jnp.broadcast_to(src[...], (HEADS, GM_WIDTH)), 0.0)
                dst[...] = _split_dot3(spread, pick)

    return _call(
        body, name="gmlp_bwd", grid=(s // tm,), sem=("arbitrary",),
        in_specs=[_rows(tm, GM_WIDTH, 1), _rows(tm, GM_WIDTH, 2), _rows(tm, GM_WIDTH), _full(ln_g.shape),
                  _full(ln_b.shape), _full(w_sp.shape), _full(bias_exp.shape)],
        out_specs=[_rows(tm, 2 * GM_WIDTH), _full(w_sp.shape), _full((HEADS, GM_CHUNK)), _full((HEADS, GM_DIM)),
                   _full((HEADS, GM_DIM))],
        out_shape=[_sds((s, 2 * GM_WIDTH), MXU), _sds(w_sp.shape, F32), _sds((HEADS, GM_CHUNK), F32),
                   _sds((HEADS, GM_DIM), F32), _sds((HEADS, GM_DIM), F32)],
        scratch=[pltpu.VMEM((GM_CHUNK, GM_WIDTH), F32), pltpu.VMEM((tm, GM_WIDTH), F32),
                 pltpu.VMEM((1, GM_WIDTH), F32), pltpu.VMEM((1, GM_WIDTH), F32)],
    )(z, z, d_sgu, ln_g, ln_b, w_sp, bias_exp)


def _mix_in_bwd(dq, dk, dv, z, d_guv, x, d_x_part, ada_raw, ada_b, g_pre, g_q, g_kv, w1t, wqt, wkv,
                cos_t, sin_t, tm):
    s = x.shape[0]
    hp = HEADS * HEAD_PAD
    za = Q_LORA + KV_LORA + HEAD_PAD

    def body(dq_ref, dk_ref, dv_ref, z_ref, dguv_ref, x_ref, dxp_ref, ar_ref, ab_ref, g_ref, gq_ref, gkv_ref,
             w1_ref, wq_ref, wkv_ref, cos_ref, sin_ref,
             gx_ref, dza_ref, dqp_ref, dkvp_ref, dsh_ref, dsc_ref, dg_ref, dgq_ref, dgkv_ref):
        i = pl.program_id(0)

        @pl.when(i == 0)
        def _():
            for r in (dsh_ref, dsc_ref, dg_ref, dgq_ref, dgkv_ref):
                r[...] = jnp.zeros(r.shape, F32)

        cos, sin = cos_ref[...], sin_ref[...]
        d_krot = jnp.zeros((tm, HEAD_PAD), F32)
        for h in range(HEADS):
            blk = slice(h * HEAD_PAD, (h + 1) * HEAD_PAD)
            dqp_ref[:, blk] = _rope_transposed(dq_ref[:, blk], cos, sin).astype(MXU)
            dk_h = dk_ref[:, blk]
            d_krot = d_krot + dk_h
            dkvp_ref[:, blk] = dk_h.astype(MXU)
        dkvp_ref[:, pl.ds(hp, hp)] = dv_ref[...].astype(MXU)
        lane = lax.broadcasted_iota(jnp.int32, (tm, HEAD_PAD), 1)
        d_kr = jnp.where((lane >= NOPE) & (lane < NOPE + ROPE), _rope_transposed(d_krot, cos, sin), 0.0)
        d_cqn = _dot(dqp_ref[...], wq_ref[...])
        d_ckvn = _dot_nt(dkvp_ref[...], wkv_ref[...])
        zt = z_ref[...]
        gq, gkv = gq_ref[...], gkv_ref[...]
        cq_hat, rq = _rms(zt[:, :Q_LORA])
        ckv_hat, rkv = _rms(zt[:, Q_LORA:Q_LORA + KV_LORA])
        dgq_ref[...] += jnp.sum(d_cqn * cq_hat, axis=0, keepdims=True)
        dgkv_ref[...] += jnp.sum(d_ckvn * ckv_hat, axis=0, keepdims=True)
        d_cq = _rms_bwd(d_cqn * gq, cq_hat, rq)
        d_ckv = _rms_bwd(d_ckvn * gkv, ckv_hat, rkv)
        d_za = jnp.concatenate([d_cq, d_ckv, d_kr], axis=1).astype(MXU)
        dza_ref[...] = d_za
        d_h1 = _dot(d_za, w1_ref[pl.ds(0, za), :]) + _dot(dguv_ref[...], w1_ref[pl.ds(za, 2 * GM_WIDTH), :])
        sc1 = _row(ar_ref, 1) + _row(ab_ref, 1)
        g = g_ref[...]
        xn, r1 = _rms(x_ref[...])
        dsh_ref[...] += jnp.sum(d_h1, axis=0, keepdims=True)
        dsc_ref[...] += jnp.sum(d_h1 * (xn * g), axis=0, keepdims=True)
        d_mod = d_h1 * (1.0 + sc1)
        dg_ref[...] += jnp.sum(d_mod * xn, axis=0, keepdims=True)
        gx_ref[...] = dxp_ref[...] + _rms_bwd(d_mod * g, xn, r1)

    vec = pl.BlockSpec((1, D_MODEL), lambda i: (0, 0))
    return _call(
        body, name="mix_in_bwd", grid=(s // tm,), sem=("arbitrary",),
        in_specs=[_rows(tm, hp), _rows(tm, hp), _rows(tm, hp), _rows(tm, za), _rows(tm, 2 * GM_WIDTH),
                  _rows(tm, D_MODEL), _rows(tm, D_MODEL), _full(ada_raw.shape), _full(ada_b.shape), _full(g_pre.shape),
                  _full(g_q.shape), _full(g_kv.shape), _full(w1t.shape), _full(wqt.shape),
                  _full(wkv.shape), _rows(tm, HEAD_PAD), _rows(tm, HEAD_PAD)],
        out_specs=[_rows(tm, D_MODEL), _rows(tm, za), _rows(tm, hp), _rows(tm, 2 * hp), vec, vec, vec,
                   _full(g_q.shape), _full(g_kv.shape)],
        out_shape=[_sds((s, D_MODEL), F32), _sds((s, za), MXU), _sds((s, hp), MXU), _sds((s, 2 * hp), MXU),
                   _sds((1, D_MODEL), F32), _sds((1, D_MODEL), F32), _sds((1, D_MODEL), F32),
                   _sds(g_q.shape, F32), _sds(g_kv.shape, F32)],
    )(dq, dk, dv, z, d_guv, x, d_x_part, ada_raw, ada_b, g_pre, g_q, g_kv, w1t, wqt, wkv, cos_t, sin_t)


def _tn_matmuls(arrays, pairs, name, ts):
    s = arrays[0].shape[0]
    steps = s // ts
    n_in, n_out = len(arrays), len(pairs)
    shapes = [(arrays[ia].shape[1], arrays[ib].shape[1]) for ia, ib in pairs]

    def body(*refs):
        ins, outs, accs = refs[:n_in], refs[n_in:n_in + n_out], refs[n_in + n_out:]
        k = pl.program_id(0)

        @pl.when(k == 0)
        def _():
            for acc in accs:
                acc[...] = jnp.zeros(acc.shape, F32)

        for (ia, ib), acc in zip(pairs, accs):
            acc[...] += _dot_tn(ins[ia][...], ins[ib][...])

        @pl.when(k == steps - 1)
        def _():
            for out, acc in zip(outs, accs):
                out[...] = acc[...].astype(MXU)

    return _call(
        body, name=name, grid=(steps,), sem=("arbitrary",),
        in_specs=[_rows(ts, a.shape[1]) for a in arrays],
        out_specs=[_full(shape) for shape in shapes],
        out_shape=[_sds(shape, MXU) for shape in shapes],
        scratch=[pltpu.VMEM(shape, F32) for shape in shapes],
    )(*arrays)


def _adamw(w, g, m, v):
    m2 = ADAM_B1 * m + (1.0 - ADAM_B1) * g
    v2 = ADAM_B2 * v + (1.0 - ADAM_B2) * (g * g)
    m_hat = m2 / (1.0 - ADAM_B1 ** ADAM_STEP)
    v_hat = v2 / (1.0 - ADAM_B2 ** ADAM_STEP)
    delta = -ADAM_LR * (m_hat / (jnp.sqrt(v_hat) + ADAM_EPS) + ADAM_WD * w)
    return delta, m2, v2


def _adam_reduce(recv, w, m, v, name):
    r, c = w.shape
    tr = r if r <= 512 else max(t for t in range(16, 513, 16) if r % t == 0)

    def body(p_ref, w_ref, m_ref, v_ref, g_ref, d_ref, mo_ref, vo_ref):
        g = p_ref[0].astype(F32)
        for j in range(1, N_DEV):
            g = g + p_ref[j].astype(F32)
        g_ref[...] = g
        d_ref[...], mo_ref[...], vo_ref[...] = _adamw(w_ref[...], g, m_ref[...], v_ref[...])

    blk = pl.BlockSpec((tr, c), lambda i: (i, 0))
    return _call(
        body, name=name, grid=(r // tr,), sem=("parallel",),
        in_specs=[pl.BlockSpec((N_DEV, tr, c), lambda i: (0, i, 0)), blk, blk, blk],
        out_specs=[blk] * 4, out_shape=[_sds((r, c), F32)] * 4,
    )(recv, w, m, v)


def _adam_w_ada(c_act_t, d_ada_cols, w, m, v):
    r, c = w.shape
    tr = 256

    def body(ct_ref, da_ref, w_ref, m_ref, v_ref, g_ref, d_ref, mo_ref, vo_ref):
        g = _dot(ct_ref[...], da_ref[...])
        g_ref[...] = g
        d_ref[...], mo_ref[...], vo_ref[...] = _adamw(w_ref[...], g, m_ref[...], v_ref[...])

    blk = pl.BlockSpec((tr, c), lambda i: (i, 0))
    return _call(
        body, name="adam_w_ada", grid=(r // tr,), sem=("parallel",),
        in_specs=[pl.BlockSpec((tr, c_act_t.shape[1]), lambda i: (i, 0)), _full(d_ada_cols.shape), blk, blk, blk],
        out_specs=[blk] * 4, out_shape=[_sds((r, c), F32)] * 4,
    )(c_act_t, d_ada_cols, w, m, v)


VEC_ROWS = D_MODEL // 128
PK_ADA = 0
PK_GAIN = PK_ADA + 6 * VEC_ROWS
PK_GQ = PK_GAIN + 4 * VEC_ROWS
PK_GKV = PK_GQ + Q_LORA // 128
PK_LOSS = PK_GKV + KV_LORA // 128
PK_LNG = 88
PK_LNB = PK_LNG + HEADS
PK_BS = PK_LNB + HEADS
PK_CB = PK_BS + HEADS
CB_ROWS = 6
PK_WS = PK_CB + N_DEV * CB_ROWS
PK_ROWS = PK_WS + HEADS * GM_CHUNK
assert PK_LOSS < PK_LNG and PK_ROWS % 8 == 0
LATE_GAIN = 2 * VEC_ROWS
LATE_GQ = 3 * VEC_ROWS
LATE_GKV = LATE_GQ + Q_LORA // 128
LATE_ROWS = 32


def _cb_chunks():
    return [(k, k * 128, min(128, FF_BLK - k * 128)) for k in range(CB_ROWS)]


def _put_rows(out_ref, row0, ref, width):
    for k in range(width // 128):
        out_ref[pl.ds(row0 + k, 1), :] = ref[:, pl.ds(k * 128, 128)]


def _pack_small(ada_rows, gains, loss_part, d_ln_g, d_ln_b, d_bs, d_cb, d_ws):
    half = N_DEV // 2

    def body(*refs):
        vec_refs = refs[:7]
        loss_ref, lng_ref, lnb_ref, bs_ref, cb_ref, ws_ref, out_ref = refs[7:]
        out_ref[pl.ds(0, PK_WS), :] = jnp.zeros((PK_WS, 128), F32)
        for n, ref in enumerate(vec_refs[:4]):
            _put_rows(out_ref, PK_ADA + (2 + n) * VEC_ROWS, ref, D_MODEL)
        for n, ref in enumerate(vec_refs[4:]):
            _put_rows(out_ref, PK_GAIN + (1 + n) * VEC_ROWS, ref, D_MODEL)
        _put_rows(out_ref, PK_LOSS, loss_ref, 128)
        out_ref[pl.ds(PK_LNG, HEADS), pl.ds(0, GM_DIM)] = lng_ref[...]
        out_ref[pl.ds(PK_LNB, HEADS), pl.ds(0, GM_DIM)] = lnb_ref[...]
        out_ref[pl.ds(PK_BS, HEADS), :] = bs_ref[...]
        for j in range(N_DEV):
            for k, lane, width in _cb_chunks():
                out_ref[pl.ds(PK_CB + j * CB_ROWS + k, 1), pl.ds(0, width)] = cb_ref[j % half, j // half, :, pl.ds(lane, width)]
        for h in range(HEADS):
            out_ref[pl.ds(PK_WS + h * GM_CHUNK, GM_CHUNK), :] = ws_ref[h]

    ins = list(ada_rows) + list(gains) + [loss_part, d_ln_g, d_ln_b, d_bs, d_cb, d_ws]
    return _call(body, name="pack_small", grid=(1,), in_specs=[_full(a.shape) for a in ins],
                 out_specs=_full((PK_ROWS, 128)), out_shape=_sds((PK_ROWS, 128), F32))(*ins)


def _pack_late(d_sh1, d_sc1, d_g_pre_mix, d_g_q, d_g_kv):
    def body(sh_ref, sc_ref, g_ref, gq_ref, gkv_ref, out_ref):
        out_ref[...] = jnp.zeros((LATE_ROWS, 128), F32)
        _put_rows(out_ref, 0, sh_ref, D_MODEL)
        _put_rows(out_ref, VEC_ROWS, sc_ref, D_MODEL)
        _put_rows(out_ref, LATE_GAIN, g_ref, D_MODEL)
        _put_rows(out_ref, LATE_GQ, gq_ref, Q_LORA)
        _put_rows(out_ref, LATE_GKV, gkv_ref, KV_LORA)

    ins = [d_sh1, d_sc1, d_g_pre_mix, d_g_q, d_g_kv]
    return _call(body, name="pack_late", grid=(1,), in_specs=[_full(a.shape) for a in ins],
                 out_specs=_full((LATE_ROWS, 128)), out_shape=_sds((LATE_ROWS, 128), F32))(*ins)


def _adam_small(gathered, late, params):
    n_par = len(params)

    def body(p_ref, late_ref, *refs):
        ins = [refs[3 * n:3 * n + 3] for n in range(n_par)]
        outs = [refs[3 * n_par + 4 * n:3 * n_par + 4 * n + 4] for n in range(n_par)]
        loss_ref, dada_ref = refs[7 * n_par:]

        def total(rows, lanes=slice(None), src=p_ref):
            g = src[0, rows, lanes]
            for j in range(1, N_DEV):
                g = g + src[j, rows, lanes]
            return g

        def apply(n, g, idx):
            w_ref, m_ref, v_ref = ins[n]
            d, m2, v2 = _adamw(w_ref[idx], g, m_ref[idx], v_ref[idx])
            for ref, val in zip(outs[n], (g, d, m2, v2)):
                ref[idx] = val

        def vector(n, src, row0, width, lane0=0):
            for k in range(width // 128):
                apply(n, total(pl.ds(row0 + k, 1), src=src), (slice(None), pl.ds(lane0 + k * 128, 128)))

        vector(0, late_ref, 0, 2 * D_MODEL)
        vector(0, p_ref, PK_ADA + 2 * VEC_ROWS, 4 * D_MODEL, lane0=2 * D_MODEL)
        vector(1, late_ref, LATE_GAIN, D_MODEL)
        for n in range(1, 4):
            vector(1 + n, p_ref, PK_GAIN + n * VEC_ROWS, D_MODEL)
        vector(5, late_ref, LATE_GQ, Q_LORA)
        vector(6, late_ref, LATE_GKV, KV_LORA)
        apply(7, total(pl.ds(PK_LNG, HEADS), pl.ds(0, GM_DIM)), (0,))
        apply(8, total(pl.ds(PK_LNB, HEADS), pl.ds(0, GM_DIM)), (0,))
        for h in range(HEADS):
            apply(9, total(pl.ds(PK_WS + h * GM_CHUNK, GM_CHUNK)), (0, h))
        apply(10, total(pl.ds(PK_BS, HEADS)), (0,))
        for j in range(N_DEV):
            for k, lane, width in _cb_chunks():
                apply(11, total(pl.ds(PK_CB + j * CB_ROWS + k, 1), pl.ds(0, width)), (pl.ds(j, 1), pl.ds(lane, width)))
        loss_ref[...] = total(pl.ds(PK_LOSS, 1))
        dada_ref[:, pl.ds(0, 2 * VEC_ROWS), :] = late_ref[:, pl.ds(0, 2 * VEC_ROWS), :]
        dada_ref[:, pl.ds(2 * VEC_ROWS, 4 * VEC_ROWS), :] = p_ref[:, pl.ds(PK_ADA + 2 * VEC_ROWS, 4 * VEC_ROWS), :]

    flat = [a for triple in params for a in triple]
    out_shape = [_sds(w.shape, F32) for w, _, _ in params for _ in range(4)]
    out_shape += [_sds((1, 128), F32), _sds((N_DEV, 6 * VEC_ROWS, 128), F32)]
    outs = _call(body, name="adam_small", grid=(1,),
                 in_specs=[_full(gathered.shape), _full(late.shape)] + [_full(a.shape) for a in flat],
                 out_specs=[_full(o.shape) for o in out_shape], out_shape=out_shape)(gathered, late, *flat)
    return [tuple(outs[4 * n:4 * n + 4]) for n in range(n_par)], outs[-2], outs[-1]


def _rope_tables(s):
    pos = jnp.arange(s, dtype=F32)
    inv = ROPE_THETA ** (-jnp.arange(0, ROPE, 2, dtype=F32) / ROPE)
    lane_inv = jnp.concatenate([jnp.zeros((NOPE,), F32), inv, inv, jnp.zeros((HEAD_PAD - NOPE - ROPE,), F32)])
    ang = pos[:, None] * lane_inv[None, :]
    return jnp.cos(ang), jnp.sin(ang)


def kernel(x, c, w_ada, b_ada, g_pre_mix, g_post_mix, w_in, g_q, w_uq, g_kv, w_ukv, gm_ln_g, gm_ln_b, w_spatial, b_spatial, w_out, g_pre_ffn, g_post_ffn, w_up, conv_w, conv_b, w_down, loss_target, m_w_ada, m_b_ada, m_g_pre_mix, m_g_post_mix, m_w_in, m_g_q, m_w_uq, m_g_kv, m_w_ukv, m_gm_ln_g, m_gm_ln_b, m_w_spatial, m_b_spatial, m_w_out, m_g_pre_ffn, m_g_post_ffn, m_w_up, m_conv_w, m_conv_b, m_w_down, v_w_ada, v_b_ada, v_g_pre_mix, v_g_post_mix, v_w_in, v_g_q, v_w_uq, v_g_kv, v_w_ukv, v_gm_ln_g, v_gm_ln_b, v_w_spatial, v_b_spatial, v_w_out, v_g_pre_ffn, v_g_post_ffn, v_w_up, v_conv_w, v_conv_b, v_w_down):
    s = x.shape[1]
    tm = min(256, s)
    tf = min(2 * ROW_SUB, s)
    tq = min(512, s)
    ts = min(2048, s)
    hp = HEADS * HEAD_PAD
    half = N_DEV // 2
    my_slot = 4 * lax.axis_index("x") + 2 * lax.axis_index("y") + lax.axis_index("c")
    x2d, target = x[0], loss_target[0]

    def t_(a):
        return jnp.swapaxes(a[0], 0, 1)

    w_in_t, m_in_t, v_in_t = t_(w_in), t_(m_w_in), t_(v_w_in)
    w_uq_t, m_uq_t, v_uq_t = t_(w_uq), t_(m_w_uq), t_(v_w_uq)
    w_up_t, m_up_t, v_up_t = t_(w_up), t_(m_w_up), t_(v_w_up)
    (g_c, g_in_t, g_uq_t, g_ukv, g_cw), _ = _exchange(
        [c, w_in_t.astype(MXU), w_uq_t.astype(MXU), w_ukv[0].astype(MXU), conv_w[0]], [], "gather_mixer_weights")

    w_in_f = g_in_t.reshape(-1, D_MODEL)
    o1, o2, o3 = Q_LORA, Q_LORA + KV_LORA, Q_LORA + KV_LORA + ROPE
    w1t = jnp.concatenate([w_in_f[:o2], jnp.zeros((NOPE, D_MODEL), MXU), w_in_f[o2:o3],
                           jnp.zeros((HEAD_PAD - NOPE - ROPE, D_MODEL), MXU), w_in_f[o3:]], axis=0)
    wqt = jnp.pad(g_uq_t, ((0, 0), (0, HEAD_PAD - NOPE - ROPE), (0, 0))).reshape(hp, Q_LORA)
    w_ukv_f = jnp.transpose(g_ukv, (1, 0, 2)).reshape(KV_LORA, HEADS, 2 * NOPE)
    pad_head = ((0, 0), (0, 0), (0, HEAD_PAD - NOPE))
    wkv = jnp.concatenate([jnp.pad(w_ukv_f[:, :, :NOPE], pad_head).reshape(KV_LORA, hp),
                           jnp.pad(w_ukv_f[:, :, NOPE:], pad_head).reshape(KV_LORA, hp)], axis=1)
    cb8 = conv_b.reshape(N_DEV, 1, FF_BLK)
    bias_exp = jnp.repeat(b_spatial[0].T, GM_DIM, axis=1)
    ln_g, ln_b = gm_ln_g.reshape(1, GM_WIDTH), gm_ln_b.reshape(1, GM_WIDTH)
    w_sp = w_spatial[0]
    cos_t, sin_t = _rope_tables(s)

    ada_part, c_act = _ada_fwd(g_c.reshape(N_DEV, D_MODEL), w_ada[0])
    _, (ada_recv,) = _exchange([], [(ada_part.reshape(N_DEV, 1, -1), _plain_slot)], "ada_rows")
    ada_raw = ada_recv.reshape(6, D_MODEL)
    ada_b = b_ada.reshape(6, D_MODEL)

    h1, z, qp, kp, vp, cqn, ckvn = _mix_in_fwd(x2d, ada_raw, ada_b, g_pre_mix, w1t, g_q, g_kv, wqt, wkv, cos_t, sin_t, tm)
    sgu = _gmlp_fwd(z, ln_g, ln_b, w_sp, bias_exp, tm)
    o_pad, lse, (g_out, g_up, g_down) = _attn_fwd(
        qp, kp, vp, tq, [w_out[0].astype(MXU), w_up_t.astype(MXU), w_down[0].astype(MXU)])
    w_out_f = g_out.reshape(2 * GM_WIDTH, D_MODEL)
    wo_attn = jnp.pad(w_out_f[:GM_WIDTH].reshape(HEADS, NOPE, D_MODEL), ((0, 0), (0, HEAD_PAD - NOPE), (0, 0)))
    wo = jnp.concatenate([wo_attn.reshape(hp, D_MODEL), w_out_f[GM_WIDTH:]], axis=0)
    wd = g_down.reshape(half, FF_BLK, D_MODEL)
    m_mix, x2, h2 = _out_proj_fwd(o_pad, sgu, wo, x2d, ada_raw, ada_b, g_post_mix, g_pre_ffn, tm)
    up_a, up_b, y_a, y_b, act = _ffn_up_fwd(h2, g_up, g_cw, cb8, tf)
    d_out, d_f, loss_part, d_gt2, d_g_post_ffn = _ffn_down_fwd(act, wd, x2, target, ada_raw, ada_b, g_post_ffn, tf)

    d_up, d_cw, d_cb, p_down, p_up = _ffn_down_bwd(d_f, wd, up_a, up_b, y_a, y_b, g_cw, act, h2, tf)
    p_down = p_down.reshape(N_DEV, -1, D_MODEL)
    d_x2, d_m, d_sh2, d_sc2, d_g_pre_ffn, d_gt1, d_g_post_mix = _ffn_up_bwd(
        d_up, g_up, x2, m_mix, d_out, ada_raw, ada_b, g_pre_ffn, g_post_mix, tm)
    dwo_attn, dwo_sgu = _tn_matmuls([o_pad, sgu, d_m], [(0, 2), (1, 2)], "dw_out", ts)
    dwo_attn = dwo_attn.reshape(HEADS, HEAD_PAD, D_MODEL)[:, :NOPE]
    p_out = jnp.concatenate([dwo_attn.reshape(GM_WIDTH, D_MODEL), dwo_sgu], axis=0).reshape(N_DEV, -1, D_MODEL)
    d_o, d_sgu, delta = _out_proj_bwd(d_m, wo, o_pad, tm)
    d_guv, d_ws, d_bs, d_ln_g, d_ln_b = _gmlp_bwd(z, d_sgu, ln_g, ln_b, w_sp, bias_exp, tm)
    packed = _pack_small([d_gt1, d_sh2, d_sc2, d_gt2], [d_g_post_mix, d_g_pre_ffn, d_g_post_ffn], loss_part,
                         d_ln_g, d_ln_b, d_bs, d_cb, d_ws)

    def ffn_slot(j):
        return (j % half, j // half)

    dq, dk, dv, (r_out, r_up, r_down, r_cw), (g_small,) = _attn_bwd(
        qp, kp, vp, d_o, lse, delta, tq,
        [(p_out, _plain_slot), (p_up, ffn_slot), (p_down, _plain_slot), (d_cw, ffn_slot)], [packed])
    grad_x, d_za, d_qp, d_kvp, d_sh1, d_sc1, d_g_pre_mix, d_g_q, d_g_kv = _mix_in_bwd(
        dq, dk, dv, z, d_guv, x2d, d_x2, ada_raw, ada_b, g_pre_mix, g_q, g_kv, w1t, wqt, wkv, cos_t, sin_t, tm)
    dw1a, dw1b, dwq, dwkv = _tn_matmuls([d_za, d_guv, h1, d_qp, cqn, ckvn, d_kvp],
                                        [(0, 2), (1, 2), (3, 4), (5, 6)], "dw_mixer", ts // 2)
    d_w_in_t = jnp.concatenate([dw1a[:o2], dw1a[o2 + NOPE:o2 + NOPE + ROPE], dw1b], axis=0)
    p_in = d_w_in_t.reshape(N_DEV, -1, D_MODEL)
    p_uq = dwq.reshape(HEADS, HEAD_PAD, Q_LORA)[:, :NOPE + ROPE]
    dwk = dwkv[:, :hp].reshape(KV_LORA, HEADS, HEAD_PAD)[:, :, :NOPE]
    dwv = dwkv[:, hp:].reshape(KV_LORA, HEADS, HEAD_PAD)[:, :, :NOPE]
    p_ukv = jnp.transpose(jnp.concatenate([dwk, dwv], axis=2), (1, 0, 2))

    (g_late,), (r_in, r_uq, r_ukv) = _exchange(
        [_pack_late(d_sh1, d_sc1, d_g_pre_mix, d_g_q, d_g_kv)],
        [(p_in, _plain_slot), (p_uq, _plain_slot), (p_ukv, _plain_slot)], "final_exchange")
    small_params = [(b_ada, m_b_ada, v_b_ada), (g_pre_mix, m_g_pre_mix, v_g_pre_mix),
                    (g_post_mix, m_g_post_mix, v_g_post_mix), (g_pre_ffn, m_g_pre_ffn, v_g_pre_ffn),
                    (g_post_ffn, m_g_post_ffn, v_g_post_ffn), (g_q, m_g_q, v_g_q), (g_kv, m_g_kv, v_g_kv),
                    (gm_ln_g, m_gm_ln_g, v_gm_ln_g), (gm_ln_b, m_gm_ln_b, v_gm_ln_b),
                    (w_spatial, m_w_spatial, v_w_spatial), (b_spatial, m_b_spatial, v_b_spatial),
                    tuple(a.reshape(N_DEV, FF_BLK) for a in (conv_b, m_conv_b, v_conv_b))]
    small_out, loss_row, d_ada_all = _adam_small(g_small, g_late, small_params)
    small_out[11] = tuple(o.reshape(conv_b.shape) for o in small_out[11])
    loss = loss_row[0, 0]

    def big(recv, w, m, v, name):
        g, d, m2, v2 = _adam_reduce(recv, w[0], m[0], v[0], name)
        return g[None], d[None], m2[None], v2[None]

    def big_t(recv, w_t, m_t, v_t, name):
        return tuple(jnp.swapaxes(o, 0, 1)[None] for o in _adam_reduce(recv, w_t, m_t, v_t, name))

    a_in = big_t(r_in, w_in_t, m_in_t, v_in_t, "adam_w_in")
    a_uq = big_t(r_uq, w_uq_t, m_uq_t, v_uq_t, "adam_w_uq")
    a_ukv = big(r_ukv, w_ukv, m_w_ukv, v_w_ukv, "adam_w_ukv")
    a_out = big(r_out, w_out, m_w_out, v_w_out, "adam_w_out")
    a_up = big_t(r_up, w_up_t, m_up_t, v_up_t, "adam_w_up")
    a_down = big(r_down, w_down, m_w_down, v_w_down, "adam_w_down")
    ada_cols = w_ada.shape[2]
    d_ada_cols = lax.dynamic_slice(d_ada_all.reshape(N_DEV, 6 * D_MODEL), (0, my_slot * ada_cols), (N_DEV, ada_cols))
    pad_seq = 128 - N_DEV
    a_ada = tuple(t[None] for t in _adam_w_ada(jnp.pad(c_act.T, ((0, 0), (0, pad_seq))).astype(MXU),
                                               jnp.pad(d_ada_cols, ((0, pad_seq), (0, 0))).astype(MXU),
                                               w_ada[0], m_w_ada[0], v_w_ada[0]))
    a_cw = big(r_cw, conv_w, m_conv_w, v_conv_w, "adam_conv_w")

    def small(k):
        return small_out[k]

    per_weight = [a_ada, small(0), small(1), small(2), a_in, small(5), a_uq, small(6), a_ukv, small(7), small(8),
                  small(9), small(10), a_out, small(3), small(4), a_up, a_cw, small(11), a_down]
    outs = [loss, grad_x[None]]
    for k in range(4):
        outs += [t[k] for t in per_weight]
    return tuple(outs)
```

```python
import functools

import jax
import jax.numpy as jnp
from jax import lax
from jax.experimental import pallas as pl
from jax.experimental.pallas import tpu as pltpu

F32 = jnp.float32
MXU = jnp.bfloat16

N_DEV = 8
D_MODEL = 1024
HEADS = 8
HEAD_PAD = 128
NOPE = 64
ROPE = 32
Q_LORA = 256
KV_LORA = 128
GM_WIDTH = 512
GM_DIM = 64
GM_CHUNK = 128
CHUNK_SHIFT = 6
ROPE_THETA = 10000.0
ATTN_SCALE = (NOPE + ROPE) ** -0.5
LOG2E = 1.4426950408889634
SCALE_LOG2E = ATTN_SCALE * LOG2E
Z_COLS = 1536
FF_BLK = 704
EPS = 1e-6
ADAM_LR = 0.001
ADAM_B1 = 0.9
ADAM_B2 = 0.999
ADAM_EPS = 1e-08
ADAM_WD = 0.01
ADAM_STEP = 10
VMEM_LIMIT = 56 * 1024 * 1024
MESH = pl.DeviceIdType.MESH


def _dot(a, b):
    return jnp.dot(a, b, preferred_element_type=F32)


def _dot_nt(a, b):
    return lax.dot_general(a, b, (((1,), (1,)), ((), ())), preferred_element_type=F32)


def _dot_tn(a, b):
    return lax.dot_general(a, b, (((0,), (0,)), ((), ())), preferred_element_type=F32)


def _call(body, *, name, grid, in_specs, out_specs, out_shape, scratch=(), sem=None):
    params = pltpu.CompilerParams(dimension_semantics=sem, vmem_limit_bytes=VMEM_LIMIT)
    return pl.pallas_call(body, name=name, grid=grid, in_specs=in_specs, out_specs=out_specs,
                          out_shape=out_shape, scratch_shapes=list(scratch), compiler_params=params)


def _full(shape):
    n = len(shape)
    return pl.BlockSpec(shape, lambda *_: (0,) * n)


def _rows(tm, cols, col_block=0):
    return pl.BlockSpec((tm, cols), lambda i: (i, col_block))


def _sds(shape, dtype):
    return jax.ShapeDtypeStruct(shape, dtype)


def _row(ref, k):
    return ref[pl.ds(k, 1), :]


def _rms(x):
    r = lax.rsqrt(jnp.mean(x * x, axis=-1, keepdims=True) + EPS)
    return x * r, r


def _rms_bwd(d_hat, hat, r):
    return r * (d_hat - hat * jnp.mean(d_hat * hat, axis=-1, keepdims=True))


def _rope_partner(t):
    lane = lax.broadcasted_iota(jnp.int32, t.shape, 1)
    swapped = jnp.where(lane < NOPE + ROPE // 2, -pltpu.roll(t, HEAD_PAD - ROPE // 2, 1), pltpu.roll(t, ROPE // 2, 1))
    return jnp.where((lane >= NOPE) & (lane < NOPE + ROPE), swapped, 0.0)


def _rope(t, cos, sin):
    return t * cos + _rope_partner(t) * sin


def _rope_transposed(g, cos, sin):
    return g * cos - _rope_partner(g * sin)


def _gelu(x):
    return x * (0.5 * (1.0 + jnp.tanh(0.7978845608028654 * (x + 0.044715 * (x * x * x)))))


def _gelu_grad(x):
    t = jnp.tanh(0.7978845608028654 * (x + 0.044715 * (x * x * x)))
    return 0.5 * (1.0 + t) + 0.5 * x * (1.0 - t * t) * (0.7978845608028654 * (1.0 + 3.0 * 0.044715 * (x * x)))


def _split_dot(x, mat):
    hi = x.astype(MXU)
    lo = (x - hi.astype(F32)).astype(MXU)
    return _dot(hi, mat) + _dot(lo, mat)


def _split_dot3(x, mat):
    hi = x.astype(MXU)
    r1 = x - hi.astype(F32)
    mid = r1.astype(MXU)
    lo = (r1 - mid.astype(F32)).astype(MXU)
    return (_dot(hi, mat) + _dot(mid, mat)) + _dot(lo, mat)


def _seg_matrix():
    r = lax.broadcasted_iota(jnp.int32, (GM_WIDTH, GM_WIDTH), 0) >> 6
    c = lax.broadcasted_iota(jnp.int32, (GM_WIDTH, GM_WIDTH), 1) >> 6
    return jnp.where(r == c, 1.0 / GM_DIM, 0.0).astype(MXU)


def _spatial_mask():
    i = lax.broadcasted_iota(jnp.int32, (GM_CHUNK, GM_CHUNK), 0) >> CHUNK_SHIFT
    j = lax.broadcasted_iota(jnp.int32, (GM_CHUNK, GM_CHUNK), 1) >> CHUNK_SHIFT
    return (j <= i).astype(F32)


def _my_place():
    return lax.axis_index("x"), lax.axis_index("y"), lax.axis_index("c")


def _flat(p):
    return 4 * p[0] + 2 * p[1] + p[2]


def _comm_sems(n):
    return [pltpu.SemaphoreType.DMA((7 * n,)), pltpu.SemaphoreType.DMA((7 * n,)), pltpu.SemaphoreType.DMA((n,))]


def _gather_steps(ins, outs, sems):
    send_sems, recv_sems, local_sems = sems
    n = len(ins)
    x, y, c = _my_place()
    me, sibling = (x, y, c), (x, y, 1 - c)
    chips = [(1 - x, y), (x, 1 - y), (1 - x, 1 - y)]

    def copy(a, k, block, to, src=None):
        slot = outs[a].at[_flat(block)]
        return pltpu.make_async_remote_copy(
            src_ref=slot if src is None else src, dst_ref=slot,
            send_sem=send_sems.at[7 * a + k], recv_sem=recv_sems.at[7 * a + k],
            device_id=to, device_id_type=MESH)

    def mine():
        return [pltpu.make_async_copy(ins[a], outs[a].at[_flat(me)], local_sems.at[a]) for a in range(n)]

    def first():
        cps = []
        for a in range(n):
            cps.append(copy(a, 0, me, sibling, src=ins[a]))
            cps += [copy(a, 1 + j, me, (*chip, c), src=ins[a]) for j, chip in enumerate(chips)]
        return cps

    def passed():
        return [copy(a, 4 + j, (*chip, c), sibling) for a in range(n) for j, chip in enumerate(chips)]

    def start():
        for cp in mine() + first():
            cp.start()

    def forward():
        for a in range(n):
            for j, chip in enumerate(chips):
                copy(a, 1 + j, (*chip, c), me).wait_recv()
                copy(a, 4 + j, (*chip, c), sibling).start()

    def finish():
        for a in range(n):
            copy(a, 0, sibling, me).wait_recv()
            for j, chip in enumerate(chips):
                copy(a, 4 + j, (*chip, 1 - c), me).wait_recv()
        for cp in first() + passed():
            cp.wait_send()
        for cp in mine():
            cp.wait()

    return start, forward, finish


def _scatter_steps(ins, outs, sems, slots):
    send_sems, recv_sems, local_sems = sems
    n = len(ins)
    flips = [(fx, fy, fc) for fx in (0, 1) for fy in (0, 1) for fc in (0, 1)][1:]
    me = _my_place()

    def peer(f):
        return tuple(1 - v if b else v for v, b in zip(me, f))

    def copy(a, k, arriving=False):
        p = peer(flips[k])
        return pltpu.make_async_remote_copy(
            src_ref=ins[a].at[slots[a](_flat(p))], dst_ref=outs[a].at[_flat(p if arriving else me)],
            send_sem=send_sems.at[7 * a + k], recv_sem=recv_sems.at[7 * a + k],
            device_id=p, device_id_type=MESH)

    def mine():
        return [pltpu.make_async_copy(ins[a].at[slots[a](_flat(me))], outs[a].at[_flat(me)], local_sems.at[a])
                for a in range(n)]

    def start():
        for cp in mine() + [copy(a, k) for a in range(n) for k in range(7)]:
            cp.start()

    def finish():
        for a in range(n):
            for k in range(7):
                copy(a, k, arriving=True).wait_recv()
        for a in range(n):
            for k in range(7):
                copy(a, k).wait_send()
        for cp in mine():
            cp.wait()

    return start, finish


def _plain_slot(j):
    return (j,)


def _scatter_out_shape(arr, slot):
    return _sds((N_DEV,) + arr.shape[len(slot(0)):], arr.dtype)


def _exchange(gathered, scattered, name):
    ng, ns = len(gathered), len(scattered)
    slots = [slot for _, slot in scattered]

    def body(*refs):
        g_in, s_in = refs[:ng], refs[ng:ng + ns]
        g_out, s_out = refs[ng + ns:2 * ng + ns], refs[2 * ng + ns:2 * (ng + ns)]
        sems = refs[2 * (ng + ns):]
        g_start, g_forward, g_finish = _gather_steps(g_in, g_out, sems[:3])
        s_start, s_finish = _scatter_steps(s_in, s_out, sems[3:], slots)
        g_start()
        s_start()
        g_forward()
        g_finish()
        s_finish()

    any_spec = pl.BlockSpec(memory_space=pl.ANY)
    outs = pl.pallas_call(
        body, name=name,
        in_specs=[any_spec] * (ng + ns), out_specs=[any_spec] * (ng + ns),
        out_shape=[_sds((N_DEV,) + a.shape, a.dtype) for a in gathered]
        + [_scatter_out_shape(a, slot) for a, slot in scattered],
        scratch_shapes=_comm_sems(max(ng, 1)) + _comm_sems(max(ns, 1)),
    )(*gathered, *[a for a, _ in scattered])
    return outs[:ng], outs[ng:]


def _ada_fwd(c_all, w_ada):
    def body(c_ref, w_ref, part_ref, act_ref):
        cv = c_ref[...]
        act = cv * jax.nn.sigmoid(cv)
        act_ref[...] = act
        part_ref[...] = _dot(act.astype(MXU), w_ref[...].astype(MXU))

    cols = w_ada.shape[1]
    return _call(body, name="ada_fwd", grid=(1,),
                 in_specs=[_full(c_all.shape), _full(w_ada.shape)],
                 out_specs=[_full((N_DEV, cols)), _full(c_all.shape)],
                 out_shape=[_sds((N_DEV, cols), F32), _sds(c_all.shape, F32)])(c_all, w_ada)


def _mix_in_fwd(x, ada_raw, ada_b, g_pre, w1, g_q, g_kv, wq, wkv, cos_t, sin_t, tm):
    s = x.shape[0]

    def body(x_ref, ar_ref, ab_ref, g_ref, w1_ref, gq_ref, gkv_ref, wq_ref, wkv_ref, cos_ref, sin_ref,
             h1_ref, z_ref, qp_ref, kp_ref, vp_ref, cqn_ref, ckvn_ref):
        sh = _row(ar_ref, 0) + _row(ab_ref, 0)
        sc = _row(ar_ref, 1) + _row(ab_ref, 1)
        xn, _ = _rms(x_ref[...])
        hb = ((xn * g_ref[...]) * (1.0 + sc) + sh).astype(MXU)
        h1_ref[...] = hb
        z = _dot_nt(hb, w1_ref[...])
        z_ref[...] = z
        cos, sin = cos_ref[...], sin_ref[...]
        cqn = (_rms(z[:, :Q_LORA])[0] * gq_ref[...]).astype(MXU)
        ckvn = (_rms(z[:, Q_LORA:Q_LORA + KV_LORA])[0] * gkv_ref[...]).astype(MXU)
        cqn_ref[...] = cqn
        ckvn_ref[...] = ckvn
        q = _dot_nt(cqn, wq_ref[...])
        kv = _dot(ckvn, wkv_ref[...])
        k_rope = _rope(z[:, Q_LORA + KV_LORA:Q_LORA + KV_LORA + HEAD_PAD], cos, sin)
        for h in range(HEADS):
            blk = slice(h * HEAD_PAD, (h + 1) * HEAD_PAD)
            qp_ref[:, blk] = _rope(q[:, blk], cos, sin).astype(MXU)
            kp_ref[:, blk] = (kv[:, blk] + k_rope).astype(MXU)
        v_lane = lax.broadcasted_iota(jnp.int32, (tm, HEADS * HEAD_PAD), 1) & (HEAD_PAD - 1)
        vp_ref[...] = jnp.where(v_lane == NOPE, 1.0, kv[:, HEADS * HEAD_PAD:]).astype(MXU)

    hp = HEADS * HEAD_PAD
    return _call(
        body, name="mix_in_fwd", grid=(s // tm,), sem=("parallel",),
        in_specs=[_rows(tm, D_MODEL), _full(ada_raw.shape), _full(ada_b.shape), _full(g_pre.shape), _full(w1.shape),
                  _full(g_q.shape), _full(g_kv.shape), _full(wq.shape), _full(wkv.shape),
                  _rows(tm, HEAD_PAD), _rows(tm, HEAD_PAD)],
        out_specs=[_rows(tm, D_MODEL), _rows(tm, Z_COLS), _rows(tm, hp), _rows(tm, hp), _rows(tm, hp),
                   _rows(tm, Q_LORA), _rows(tm, KV_LORA)],
        out_shape=[_sds((s, D_MODEL), MXU), _sds((s, Z_COLS), F32), _sds((s, hp), MXU), _sds((s, hp), MXU),
                   _sds((s, hp), MXU), _sds((s, Q_LORA), MXU), _sds((s, KV_LORA), MXU)],
    )(x, ada_raw, ada_b, g_pre, w1, g_q, g_kv, wq, wkv, cos_t, sin_t)


def _gm_norm(zv, seg):
    gv = _gelu(zv)
    cen = gv - _split_dot(gv, seg)
    rstd = lax.rsqrt(_split_dot(cen * cen, seg) + EPS)
    return gv, cen * rstd, rstd


def _gm_pairs(rows):
    first = lax.broadcasted_iota(jnp.int32, (rows, 2 * GM_DIM), 1) < GM_DIM
    return [(slice(p * 2 * GM_DIM, (p + 1) * 2 * GM_DIM), first) for p in range(HEADS // 2)]


def _gm_mix(wm, vb, rows, transposed=False):
    dot = _dot_tn if transposed else _dot
    return jnp.concatenate([jnp.where(first, dot(wm[2 * p], vb[:, lanes]), dot(wm[2 * p + 1], vb[:, lanes]))
                            for p, (lanes, first) in enumerate(_gm_pairs(rows))], axis=1)


def _gmlp_fwd(z, ln_g, ln_b, w_sp, bias_exp, tm):
    s = z.shape[0]
    nblk = tm // GM_CHUNK

    def body(zu_ref, zv_ref, lg_ref, lb_ref, w_ref, be_ref, sgu_ref):
        seg = _seg_matrix()
        mask = _spatial_mask()
        wm = [(w_ref[h] * mask).astype(MXU) for h in range(HEADS)]
        gu = _gelu(zu_ref[...])
        _, vhat, _ = _gm_norm(zv_ref[...], seg)
        vln = (vhat * lg_ref[...] + lb_ref[...]).astype(MXU)
        for n in range(nblk):
            rows = slice(n * GM_CHUNK, (n + 1) * GM_CHUNK)
            mixed = _gm_mix(wm, vln[rows], GM_CHUNK) + be_ref[...]
            sgu_ref[rows, :] = (gu[rows] * mixed).astype(MXU)

    return _call(
        body, name="gmlp_fwd", grid=(s // tm,), sem=("parallel",),
        in_specs=[_rows(tm, GM_WIDTH, 1), _rows(tm, GM_WIDTH, 2), _full(ln_g.shape), _full(ln_b.shape),
                  _full(w_sp.shape), _full(bias_exp.shape)],
        out_specs=_rows(tm, GM_WIDTH), out_shape=_sds((s, GM_WIDTH), MXU),
    )(z, z, ln_g, ln_b, w_sp, bias_exp)


def _chunk_mask(n_q, n_k, q_off):
    qc = (q_off + lax.broadcasted_iota(jnp.int32, (n_q, n_k), 0)) >> CHUNK_SHIFT
    kc = lax.broadcasted_iota(jnp.int32, (n_q, n_k), 1) >> CHUNK_SHIFT
    return kc <= qc


NEG_BIG = -1e30
ATTN_HEADS_PER_STEP = 2


def _attn_fwd(qp, kp, vp, tq, gathered):
    s = qp.shape[0]
    nq = s // tq
    hb = ATTN_HEADS_PER_STEP
    groups = HEADS // hb
    width = hb * HEAD_PAD
    ng = len(gathered)

    def body(q_ref, k_ref, v_ref, *rest):
        g_in, (o_ref, lse_ref), g_out = rest[:ng], rest[ng:ng + 2], rest[ng + 2:2 * ng + 2]
        m_sc, acc_sc = rest[2 * ng + 2:2 * ng + 4]
        g_start, g_forward, g_finish = _gather_steps(g_in, g_out, rest[2 * ng + 4:])
        g, i = pl.program_id(0), pl.program_id(1)
        pl.when((g == 0) & (i == 0))(g_start)
        pl.when((g == groups - 1) & (i == 0))(g_forward)
        m_sc[...] = jnp.full(m_sc.shape, NEG_BIG, F32)
        acc_sc[...] = jnp.zeros(acc_sc.shape, F32)

        def tile(j, masked):
            rows = pl.ds(pl.multiple_of(j * tq, tq), tq)
            for hh in range(hb):
                lanes = slice(hh * HEAD_PAD, (hh + 1) * HEAD_PAD)
                sc = _dot_nt(q_ref[:, lanes], k_ref[rows, lanes])
                if masked:
                    sc = jnp.where(_chunk_mask(tq, tq, 0), sc, NEG_BIG)
                blocks = [sc[:, b * 128:(b + 1) * 128] for b in range(tq // 128)]
                m_prev = m_sc[hh]
                m_tile = jnp.max(functools.reduce(jnp.maximum, blocks), axis=-1, keepdims=True)
                m_new = jnp.maximum(m_prev, m_tile)
                alpha = jnp.exp2((m_prev - m_new) * SCALE_LOG2E)
                p = jnp.concatenate([jnp.exp2((b - m_new) * SCALE_LOG2E) for b in blocks], axis=1).astype(MXU)
                acc_sc[hh] = alpha * acc_sc[hh] + _dot(p, v_ref[rows, lanes])
                m_sc[hh] = m_new

        def off_diagonal_pair(p, carry):
            tile(2 * p, False)
            tile(2 * p + 1, False)
            return carry

        lax.fori_loop(0, i // 2, off_diagonal_pair, 0)

        @pl.when(i % 2 == 1)
        def _():
            tile(i - 1, False)

        tile(i, True)
        for hh in range(hb):
            lanes = slice(hh * HEAD_PAD, (hh + 1) * HEAD_PAD)
            acc = acc_sc[hh]
            denom = acc[:, NOPE:NOPE + 1]
            o_ref[:, lanes] = (acc / denom).astype(MXU)
            lse_ref[hh] = m_sc[hh][:, :1] * SCALE_LOG2E + jnp.log(denom) * LOG2E
        pl.when((g == groups - 1) & (i == nq - 1))(g_finish)

    q_spec = pl.BlockSpec((tq, width), lambda g, i: (i, g))
    kv_spec = pl.BlockSpec((s, width), lambda g, i: (0, g))
    any_spec = pl.BlockSpec(memory_space=pl.ANY)
    outs = _call(
        body, name="attn_fwd", grid=(groups, nq), sem=("arbitrary", "arbitrary"),
        in_specs=[q_spec, kv_spec, kv_spec] + [any_spec] * ng,
        out_specs=[q_spec, pl.BlockSpec((hb, tq, 1), lambda g, i: (g, i, 0))] + [any_spec] * ng,
        out_shape=[_sds(qp.shape, MXU), _sds((HEADS, s, 1), F32)]
        + [_sds((N_DEV,) + a.shape, a.dtype) for a in gathered],
        scratch=[pltpu.VMEM((hb, tq, HEAD_PAD), F32), pltpu.VMEM((hb, tq, HEAD_PAD), F32)] + _comm_sems(ng),
    )(qp, kp, vp, *gathered)
    return outs[0], outs[1], outs[2:]


def _out_proj_fwd(o_pad, sgu, wo, x, ada_raw, ada_b, g_post_mix, g_pre_ffn, tm):
    s = x.shape[0]
    hp = HEADS * HEAD_PAD

    def body(o_ref, sgu_ref, wo_ref, x_ref, ar_ref, ab_ref, gpm_ref, gpf_ref, m_ref, x2_ref, h2_ref):
        gt1 = _row(ar_ref, 2) + _row(ab_ref, 2)
        sh2 = _row(ar_ref, 3) + _row(ab_ref, 3)
        sc2 = _row(ar_ref, 4) + _row(ab_ref, 4)
        m = _dot(o_ref[...], wo_ref[pl.ds(0, hp), :]) + _dot(sgu_ref[...], wo_ref[pl.ds(hp, GM_WIDTH), :])
        m_ref[...] = m
        x2 = x_ref[...] + gt1 * (_rms(m)[0] * gpm_ref[...])
        x2_ref[...] = x2
        h2_ref[...] = ((_rms(x2)[0] * gpf_ref[...]) * (1.0 + sc2) + sh2).astype(MXU)

    return _call(
        body, name="out_proj_fwd", grid=(s // tm,), sem=("parallel",),
        in_specs=[_rows(tm, hp), _rows(tm, GM_WIDTH), _full(wo.shape), _rows(tm, D_MODEL), _full(ada_raw.shape),
                  _full(ada_b.shape), _full(g_post_mix.shape), _full(g_pre_ffn.shape)],
        out_specs=[_rows(tm, D_MODEL)] * 3,
        out_shape=[_sds((s, D_MODEL), F32), _sds((s, D_MODEL), F32), _sds((s, D_MODEL), MXU)],
    )(o_pad, sgu, wo, x, ada_raw, ada_b, g_post_mix, g_pre_ffn)


def _conv(u, halo, cw_ref, cb_ref):
    ext = jnp.concatenate([halo, u], axis=0)
    m1, m2 = pltpu.roll(ext, 1, 0)[8:], pltpu.roll(ext, 2, 0)[8:]
    return cb_ref[0] + ((m2 * cw_ref[0, pl.ds(0, 1), :] + m1 * cw_ref[0, pl.ds(1, 1), :]) + u * cw_ref[0, pl.ds(2, 1), :])


ROW_SUB = 256


def _sub_blocks(tm):
    return [slice(r, r + ROW_SUB) for r in range(0, tm, ROW_SUB)]


def _ffn_up_fwd(h2, w_up, conv_w, conv_b, tm):
    s = h2.shape[0]
    half = N_DEV // 2

    def body(h_ref, wa_ref, wb_ref, cwa_ref, cwb_ref, cba_ref, cbb_ref,
             ua_ref, ub_ref, ya_ref, yb_ref, act_ref, halo_a, halo_b, wa_t, wb_t):
        i = pl.program_id(1)

        @pl.when(i == 0)
        def _():
            halo_a[...] = jnp.zeros(halo_a.shape, F32)
            halo_b[...] = jnp.zeros(halo_b.shape, F32)
            wa_t[...] = wa_ref[0].T
            wb_t[...] = wb_ref[0].T

        ha, hb = halo_a[...], halo_b[...]
        for rows in _sub_blocks(tm):
            h = h_ref[rows, :]
            ua = _dot(h, wa_t[...])
            ub = _dot(h, wb_t[...])
            ua_ref[0, rows, :] = ua
            ub_ref[0, rows, :] = ub
            ya = _conv(ua, ha, cwa_ref, cba_ref)
            yb = _conv(ub, hb, cwb_ref, cbb_ref)
            ya_ref[0, rows, :] = ya
            yb_ref[0, rows, :] = yb
            ha, hb = ua[ROW_SUB - 8:], ub[ROW_SUB - 8:]
            act_ref[0, rows, :] = ((ya * jax.nn.sigmoid(ya)) * yb).astype(MXU)
        halo_a[...] = ha
        halo_b[...] = hb

    def blk(shape, off):
        return pl.BlockSpec(shape, lambda j, i: (j + off, 0, 0))

    def tok(off=0):
        return pl.BlockSpec((1, tm, FF_BLK), lambda j, i: (j + off, i, 0))

    return _call(
        body, name="ffn_up_fwd", grid=(half, s // tm), sem=("parallel", "arbitrary"),
        in_specs=[pl.BlockSpec((tm, D_MODEL), lambda j, i: (i, 0)),
                  blk((1, FF_BLK, D_MODEL), 0), blk((1, FF_BLK, D_MODEL), half),
                  blk((1, 3, FF_BLK), 0), blk((1, 3, FF_BLK), half), blk((1, 1, FF_BLK), 0), blk((1, 1, FF_BLK), half)],
        out_specs=[tok()] * 5,
        out_shape=[_sds((half, s, FF_BLK), F32)] * 4 + [_sds((half, s, FF_BLK), MXU)],
        scratch=[pltpu.VMEM((8, FF_BLK), F32), pltpu.VMEM((8, FF_BLK), F32),
                 pltpu.VMEM((D_MODEL, FF_BLK), MXU), pltpu.VMEM((D_MODEL, FF_BLK), MXU)],
    )(h2, w_up, w_up, conv_w, conv_w, conv_b, conv_b)


def _ffn_down_fwd(act, wd, x2, target, ada_raw, ada_b, g_post_ffn, tm):
    s = x2.shape[0]
    half = N_DEV // 2

    def body(act_ref, wd_ref, x2_ref, t_ref, ar_ref, ab_ref, g_ref, dout_ref, df_ref, loss_ref, dgt_ref, dg_ref):
        i = pl.program_id(0)

        @pl.when(i == 0)
        def _():
            loss_ref[...] = jnp.zeros(loss_ref.shape, F32)
            dgt_ref[...] = jnp.zeros(dgt_ref.shape, F32)
            dg_ref[...] = jnp.zeros(dg_ref.shape, F32)

        gt2 = _row(ar_ref, 5) + _row(ab_ref, 5)
        g = g_ref[...]
        for rows in _sub_blocks(tm):
            f = _dot(act_ref[0, rows, :], wd_ref[0])
            for j in range(1, half):
                f = f + _dot(act_ref[j, rows, :], wd_ref[j])
            fhat, rf = _rms(f)
            fn = fhat * g
            err = (x2_ref[rows, :] + gt2 * fn) - t_ref[rows, :]
            loss_ref[...] += 0.5 * jnp.sum(jnp.mean(err * err, axis=-1, keepdims=True))
            d_out = err * (1.0 / D_MODEL)
            dout_ref[rows, :] = d_out
            dgt_ref[...] += jnp.sum(d_out * fn, axis=0, keepdims=True)
            d_fn = d_out * gt2
            dg_ref[...] += jnp.sum(d_fn * fhat, axis=0, keepdims=True)
            df_ref[rows, :] = _rms_bwd(d_fn * g, fhat, rf).astype(MXU)

    vec = pl.BlockSpec((1, D_MODEL), lambda i: (0, 0))
    return _call(
        body, name="ffn_down_fwd", grid=(s // tm,), sem=("arbitrary",),
        in_specs=[pl.BlockSpec((half, tm, FF_BLK), lambda i: (0, i, 0)), _full(wd.shape), _rows(tm, D_MODEL),
                  _rows(tm, D_MODEL), _full(ada_raw.shape), _full(ada_b.shape), _full(g_post_ffn.shape)],
        out_specs=[_rows(tm, D_MODEL), _rows(tm, D_MODEL), pl.BlockSpec((1, 128), lambda i: (0, 0)), vec, vec],
        out_shape=[_sds((s, D_MODEL), F32), _sds((s, D_MODEL), MXU), _sds((1, 128), F32),
                   _sds((1, D_MODEL), F32), _sds((1, D_MODEL), F32)],
    )(act, wd, x2, target, ada_raw, ada_b, g_post_ffn)


def _ffn_down_bwd(d_f, wd, up_a, up_b, y_a, y_b, conv_w, act, h2, tm):
    s = d_f.shape[0]
    half = N_DEV // 2
    nt = s // tm

    def body(df_ref, wd_ref, ua_ref, ub_ref, ya_ref, yb_ref, cwa_ref, cwb_ref, act_ref, h2_ref,
             dup_ref, dcw_ref, dcb_ref, pd_ref, pu_ref, next_a, next_b, acc_d, acc_a, acc_b):
        i = pl.program_id(1)

        @pl.when(i == 0)
        def _():
            next_a[...] = jnp.zeros(next_a.shape, F32)
            next_b[...] = jnp.zeros(next_b.shape, F32)
            dcw_ref[...] = jnp.zeros(dcw_ref.shape, F32)
            dcb_ref[...] = jnp.zeros(dcb_ref.shape, F32)
            for acc in (acc_d, acc_a, acc_b):
                acc[...] = jnp.zeros(acc.shape, F32)

        def conv_bwd(d_y, u, nxt, cw_ref, part, rows):
            ext = jnp.concatenate([d_y, nxt], axis=0)
            p1 = pltpu.roll(ext, ROW_SUB + 7, 0)[:ROW_SUB]
            p2 = pltpu.roll(ext, ROW_SUB + 6, 0)[:ROW_SUB]
            d_u = (d_y * cw_ref[0, pl.ds(2, 1), :] + p1 * cw_ref[0, pl.ds(1, 1), :]) + p2 * cw_ref[0, pl.ds(0, 1), :]
            dup_ref[0, part, rows, :] = d_u.astype(MXU)
            dcb_ref[0, part] += jnp.sum(d_y, axis=0, keepdims=True)
            dcw_ref[0, part, pl.ds(0, 1), :] += jnp.sum(p2 * u, axis=0, keepdims=True)
            dcw_ref[0, part, pl.ds(1, 1), :] += jnp.sum(p1 * u, axis=0, keepdims=True)
            dcw_ref[0, part, pl.ds(2, 1), :] += jnp.sum(d_y * u, axis=0, keepdims=True)
            return d_y[:8]

        nxa, nxb = next_a[...], next_b[...]
        for rows in reversed(_sub_blocks(tm)):
            d_act = _dot_nt(df_ref[rows, :], wd_ref[0])
            ya, yb = ya_ref[0, rows, :], yb_ref[0, rows, :]
            sig = jax.nn.sigmoid(ya)
            d_ya = d_act * yb * (sig * (1.0 + ya * (1.0 - sig)))
            d_yb = d_act * (ya * sig)
            nxa = conv_bwd(d_ya, ua_ref[0, rows, :], nxa, cwa_ref, 0, rows)
            nxb = conv_bwd(d_yb, ub_ref[0, rows, :], nxb, cwb_ref, 1, rows)
        next_a[...] = nxa
        next_b[...] = nxb
        acc_d[...] += _dot_tn(act_ref[0], df_ref[...])
        acc_a[...] += _dot_tn(dup_ref[0, 0], h2_ref[...])
        acc_b[...] += _dot_tn(dup_ref[0, 1], h2_ref[...])

        @pl.when(i == nt - 1)
        def _():
            pd_ref[0] = acc_d[...].astype(MXU)
            pu_ref[0, 0] = acc_a[...].astype(MXU)
            pu_ref[0, 1] = acc_b[...].astype(MXU)

    def rev(i):
        return nt - 1 - i

    def blk(shape, off):
        return pl.BlockSpec(shape, lambda j, i: (j + off, 0, 0))

    tok = pl.BlockSpec((1, tm, FF_BLK), lambda j, i: (j, rev(i), 0))
    acc3 = pl.BlockSpec((1, 2, 3, FF_BLK), lambda j, i: (j, 0, 0, 0))
    acc1 = pl.BlockSpec((1, 2, 1, FF_BLK), lambda j, i: (j, 0, 0, 0))
    return _call(
        body, name="ffn_down_bwd", grid=(half, nt), sem=("parallel", "arbitrary"),
        in_specs=[pl.BlockSpec((tm, D_MODEL), lambda j, i: (rev(i), 0)), blk((1, FF_BLK, D_MODEL), 0),
                  tok, tok, tok, tok, blk((1, 3, FF_BLK), 0), blk((1, 3, FF_BLK), half),
                  tok, pl.BlockSpec((tm, D_MODEL), lambda j, i: (rev(i), 0))],
        out_specs=[pl.BlockSpec((1, 2, tm, FF_BLK), lambda j, i: (j, 0, rev(i), 0)), acc3, acc1,
                   pl.BlockSpec((1, FF_BLK, D_MODEL), lambda j, i: (j, 0, 0)),
                   pl.BlockSpec((1, 2, FF_BLK, D_MODEL), lambda j, i: (j, 0, 0, 0))],
        out_shape=[_sds((half, 2, s, FF_BLK), MXU), _sds((half, 2, 3, FF_BLK), F32), _sds((half, 2, 1, FF_BLK), F32),
                   _sds((half, FF_BLK, D_MODEL), MXU), _sds((half, 2, FF_BLK, D_MODEL), MXU)],
        scratch=[pltpu.VMEM((8, FF_BLK), F32), pltpu.VMEM((8, FF_BLK), F32)]
        + [pltpu.VMEM((FF_BLK, D_MODEL), F32)] * 3,
    )(d_f, wd, up_a, up_b, y_a, y_b, conv_w, conv_w, act, h2)


def _ffn_up_bwd(d_up, w_up, x2, m, d_out, ada_raw, ada_b, g_pre_ffn, g_post_mix, tm):
    s = x2.shape[0]
    half = N_DEV // 2

    def body(dup_ref, w_ref, x2_ref, m_ref, dout_ref, ar_ref, ab_ref, gpf_ref, gpm_ref,
             dx_ref, dm_ref, dsh_ref, dsc_ref, dgpf_ref, dgt1_ref, dgpm_ref):
        i = pl.program_id(0)

        @pl.when(i == 0)
        def _():
            for r in (dsh_ref, dsc_ref, dgpf_ref, dgt1_ref, dgpm_ref):
                r[...] = jnp.zeros(r.shape, F32)

        gt1 = _row(ar_ref, 2) + _row(ab_ref, 2)
        sc2 = _row(ar_ref, 4) + _row(ab_ref, 4)
        gpf, gpm = gpf_ref[...], gpm_ref[...]
        d_h2 = _dot(dup_ref[0, 0], w_ref[0])
        for j in range(1, half):
            d_h2 = d_h2 + _dot(dup_ref[j, 0], w_ref[j])
        for j in range(half):
            d_h2 = d_h2 + _dot(dup_ref[j, 1], w_ref[half + j])
        x2n, r2 = _rms(x2_ref[...])
        dsh_ref[...] += jnp.sum(d_h2, axis=0, keepdims=True)
        dsc_ref[...] += jnp.sum(d_h2 * (x2n * gpf), axis=0, keepdims=True)
        d_mod = d_h2 * (1.0 + sc2)
        dgpf_ref[...] += jnp.sum(d_mod * x2n, axis=0, keepdims=True)
        d_x2 = dout_ref[...] + _rms_bwd(d_mod * gpf, x2n, r2)
        dx_ref[...] = d_x2
        mhat, rm = _rms(m_ref[...])
        dgt1_ref[...] += jnp.sum(d_x2 * (mhat * gpm), axis=0, keepdims=True)
        d_mn = d_x2 * gt1
        dgpm_ref[...] += jnp.sum(d_mn * mhat, axis=0, keepdims=True)
        dm_ref[...] = _rms_bwd(d_mn * gpm, mhat, rm).astype(MXU)

    vec = pl.BlockSpec((1, D_MODEL), lambda i: (0, 0))
    tok = pl.BlockSpec((half, 2, tm, FF_BLK), lambda i: (0, 0, i, 0))
    return _call(
        body, name="ffn_up_bwd", grid=(s // tm,), sem=("arbitrary",),
        in_specs=[tok, _full(w_up.shape), _rows(tm, D_MODEL), _rows(tm, D_MODEL), _rows(tm, D_MODEL),
                  _full(ada_raw.shape), _full(ada_b.shape), _full(g_pre_ffn.shape), _full(g_post_mix.shape)],
        out_specs=[_rows(tm, D_MODEL), _rows(tm, D_MODEL), vec, vec, vec, vec, vec],
        out_shape=[_sds((s, D_MODEL), F32), _sds((s, D_MODEL), MXU)] + [_sds((1, D_MODEL), F32)] * 5,
    )(d_up, w_up, x2, m, d_out, ada_raw, ada_b, g_pre_ffn, g_post_mix)


def _out_proj_bwd(d_m, wo, o_pad, tm):
    s = d_m.shape[0]
    hp = HEADS * HEAD_PAD

    def body(dm_ref, wo_ref, o_ref, do_ref, dsgu_ref, delta_ref):
        d_cat = _dot_nt(dm_ref[...], wo_ref[...])
        d_o = d_cat[:, :hp]
        do_ref[...] = d_o.astype(MXU)
        dsgu_ref[...] = d_cat[:, hp:]
        prod = d_o * o_ref[...].astype(F32)
        for h in range(HEADS):
            delta_ref[h] = jnp.sum(prod[:, h * HEAD_PAD:(h + 1) * HEAD_PAD], axis=-1, keepdims=True)

    return _call(
        body, name="out_proj_bwd", grid=(s // tm,), sem=("parallel",),
        in_specs=[_rows(tm, D_MODEL), _full(wo.shape), _rows(tm, hp)],
        out_specs=[_rows(tm, hp), _rows(tm, GM_WIDTH), pl.BlockSpec((HEADS, tm, 1), lambda i: (0, i, 0))],
        out_shape=[_sds((s, hp), MXU), _sds((s, GM_WIDTH), F32), _sds((HEADS, s, 1), F32)],
    )(d_m, wo, o_pad)


def _attn_bwd(qp, kp, vp, d_o, lse, delta, tq, scattered, gathered):
    s = qp.shape[0]
    nq = s // tq
    hb = ATTN_HEADS_PER_STEP
    groups = HEADS // hb
    width = hb * HEAD_PAD
    ns, ng = len(scattered), len(gathered)
    nc = ns + ng
    slots = [slot for _, slot in scattered]

    def body(q_ref, k_ref, v_ref, do_ref, lse_ref, dl_ref, *rest):
        c_in, (dq_ref, dk_ref, dv_ref), c_out = rest[:nc], rest[nc:nc + 3], rest[nc + 3:2 * nc + 3]
        dk_sc, dv_sc = rest[2 * nc + 3:2 * nc + 5]
        sems = rest[2 * nc + 5:]
        s_start, s_finish = _scatter_steps(c_in[:ns], c_out[:ns], sems[:3], slots)
        g_start, g_forward, g_finish = _gather_steps(c_in[ns:], c_out[ns:], sems[3:])
        g, j = pl.program_id(0), pl.program_id(1)

        @pl.when((g == 0) & (j == 0))
        def _():
            s_start()
            g_start()

        pl.when((g == groups - 1) & (j == 0))(g_forward)

        @pl.when(j == 0)
        def _():
            dq_ref[...] = jnp.zeros(dq_ref.shape, F32)

        dk_sc[...] = jnp.zeros(dk_sc.shape, F32)
        dv_sc[...] = jnp.zeros(dv_sc.shape, F32)

        def tile(i, masked):
            rows = pl.ds(pl.multiple_of(i * tq, tq), tq)
            for hh in range(hb):
                lanes = slice(hh * HEAD_PAD, (hh + 1) * HEAD_PAD)
                q, do, k = q_ref[rows, lanes], do_ref[rows, lanes], k_ref[:, lanes]
                sc = _dot_nt(q, k)
                if masked:
                    sc = jnp.where(_chunk_mask(tq, tq, 0), sc, NEG_BIG)
                p = jnp.exp2(sc * SCALE_LOG2E - lse_ref[hh, rows, :])
                dv_sc[hh] += _dot_tn(p.astype(MXU), do)
                dp = _dot_nt(do, v_ref[:, lanes])
                ds = (p * (dp - dl_ref[hh, rows, :])).astype(MXU)
                dk_sc[hh] += _dot_tn(ds, q)
                dq_ref[rows, lanes] += _dot(ds, k) * ATTN_SCALE

        def off_diagonal_pair(p, carry):
            tile(j + 1 + 2 * p, False)
            tile(j + 2 + 2 * p, False)
            return carry

        below = nq - 1 - j
        tile(j, True)
        lax.fori_loop(0, below // 2, off_diagonal_pair, 0)

        @pl.when(below % 2 == 1)
        def _():
            tile(nq - 1, False)
        for hh in range(hb):
            lanes = slice(hh * HEAD_PAD, (hh + 1) * HEAD_PAD)
            dk_ref[:, lanes] = dk_sc[hh] * ATTN_SCALE
            dv_ref[:, lanes] = dv_sc[hh]
        @pl.when((g == groups - 1) & (j == nq - 1))
        def _():
            g_finish()
            s_finish()

    seq_spec = pl.BlockSpec((s, width), lambda g, j: (0, g))
    kv_spec = pl.BlockSpec((tq, width), lambda g, j: (j, g))
    col_spec = pl.BlockSpec((hb, s, 1), lambda g, j: (g, 0, 0))
    any_spec = pl.BlockSpec(memory_space=pl.ANY)
    outs = _call(
        body, name="attn_bwd", grid=(groups, nq), sem=("arbitrary", "arbitrary"),
        in_specs=[seq_spec, kv_spec, kv_spec, seq_spec, col_spec, col_spec] + [any_spec] * nc,
        out_specs=[seq_spec, kv_spec, kv_spec] + [any_spec] * nc,
        out_shape=[_sds(qp.shape, F32), _sds(qp.shape, F32), _sds(qp.shape, F32)]
        + [_scatter_out_shape(a, slot) for a, slot in scattered]
        + [_sds((N_DEV,) + a.shape, a.dtype) for a in gathered],
        scratch=[pltpu.VMEM((hb, tq, HEAD_PAD), F32), pltpu.VMEM((hb, tq, HEAD_PAD), F32)]
        + _comm_sems(ns) + _comm_sems(ng),
    )(qp, kp, vp, d_o, lse, delta, *[a for a, _ in scattered], *gathered)
    return outs[0], outs[1], outs[2], outs[3:3 + ns], outs[3 + ns:]


def _gmlp_bwd(z, d_sgu, ln_g, ln_b, w_sp, bias_exp, tm):
    s = z.shape[0]
    nblk = tm // GM_CHUNK

    def body(zu_ref, zv_ref, dsgu_ref, lg_ref, lb_ref, w_ref, be_ref,
             dguv_ref, dws_ref, dbs_ref, dlg_ref, dlb_ref, dbe_sc, dvln_sc, dlg_sc, dlb_sc):
        i = pl.program_id(0)

        @pl.when(i == 0)
        def _():
            for r in (dws_ref, dlg_sc, dlb_sc, dbe_sc):
                r[...] = jnp.zeros(r.shape, F32)

        seg = _seg_matrix()
        mask = _spatial_mask()
        wm = [(w_ref[h] * mask).astype(MXU) for h in range(HEADS)]
        zu, zv = zu_ref[...], zv_ref[...]
        gu = _gelu(zu)
        _, vhat, rstd = _gm_norm(zv, seg)
        lg = lg_ref[...]
        vln = (vhat * lg + lb_ref[...]).astype(MXU)
        d_sgu = dsgu_ref[...]
        for n in range(nblk):
            rows = slice(n * GM_CHUNK, (n + 1) * GM_CHUNK)
            vb = vln[rows]
            mixed = _gm_mix(wm, vb, GM_CHUNK) + be_ref[...]
            d_mixed = d_sgu[rows] * gu[rows]
            dguv_ref[rows, pl.ds(0, GM_WIDTH)] = ((d_sgu[rows] * mixed) * _gelu_grad(zu[rows])).astype(MXU)
            dbe_sc[...] += d_mixed
            dmb = d_mixed.astype(MXU)
            for p, (lanes, first) in enumerate(_gm_pairs(GM_CHUNK)):
                dm_pair, v_pair = dmb[:, lanes], vb[:, lanes]
                zero = jnp.zeros_like(dm_pair)
                dws_ref[2 * p] += _dot_nt(jnp.where(first, dm_pair, zero), v_pair)
                dws_ref[2 * p + 1] += _dot_nt(jnp.where(first, zero, dm_pair), v_pair)
            dvln_sc[rows, :] = _gm_mix(wm, dmb, GM_CHUNK, transposed=True)
        d_vln = dvln_sc[...]
        dlg_sc[...] += jnp.sum(d_vln * vhat, axis=0, keepdims=True)
        dlb_sc[...] += jnp.sum(d_vln, axis=0, keepdims=True)
        d_vhat = d_vln * lg
        d_gv = rstd * ((d_vhat - _split_dot(d_vhat, seg)) - vhat * _split_dot(d_vhat * vhat, seg))
        dguv_ref[:, pl.ds(GM_WIDTH, GM_WIDTH)] = (d_gv * _gelu_grad(zv)).astype(MXU)

        @pl.when(i == pl.num_programs(0) - 1)
        def _():
            for h in range(HEADS):
                dws_ref[h] = dws_ref[h] * mask
            hrow = lax.broadcasted_iota(jnp.int32, (HEADS, GM_WIDTH), 0)
            hlane = lax.broadcasted_iota(jnp.int32, (HEADS, GM_WIDTH), 1) >> 6
            ind = jnp.where(hrow == hlane, 1.0, 0.0).astype(MXU)
            acc = dbe_sc[...]
            hi = acc.astype(MXU)
            lo = (acc - hi.astype(F32)).astype(MXU)
            dbs_ref[...] = _dot_nt(ind, hi) + _dot_nt(ind, lo)
            pick = (lax.broadcasted_iota(jnp.int32, (GM_WIDTH, GM_DIM), 0) & (GM_DIM - 1)
                    == lax.broadcasted_iota(jnp.int32, (GM_WIDTH, GM_DIM), 1))
            pick = jnp.where(pick, 1.0, 0.0).astype(MXU)
            for src, dst in ((dlg_sc, dlg_ref), (dlb_sc, dlb_ref)):
                spread = jnp.where(hrow == hlane, jnp.broadcast_to(src[...], (HEADS, GM_WIDTH)), 0.0)
                dst[...] = _split_dot3(spread, pick)

    return _call(
        body, name="gmlp_bwd", grid=(s // tm,), sem=("arbitrary",),
        in_specs=[_rows(tm, GM_WIDTH, 1), _rows(tm, GM_WIDTH, 2), _rows(tm, GM_WIDTH), _full(ln_g.shape),
                  _full(ln_b.shape), _full(w_sp.shape), _full(bias_exp.shape)],
        out_specs=[_rows(tm, 2 * GM_WIDTH), _full(w_sp.shape), _full((HEADS, GM_CHUNK)), _full((HEADS, GM_DIM)),
                   _full((HEADS, GM_DIM))],
        out_shape=[_sds((s, 2 * GM_WIDTH), MXU), _sds(w_sp.shape, F32), _sds((HEADS, GM_CHUNK), F32),
                   _sds((HEADS, GM_DIM), F32), _sds((HEADS, GM_DIM), F32)],
        scratch=[pltpu.VMEM((GM_CHUNK, GM_WIDTH), F32), pltpu.VMEM((tm, GM_WIDTH), F32),
                 pltpu.VMEM((1, GM_WIDTH), F32), pltpu.VMEM((1, GM_WIDTH), F32)],
    )(z, z, d_sgu, ln_g, ln_b, w_sp, bias_exp)


def _mix_in_bwd(dq, dk, dv, z, d_guv, x, d_x_part, ada_raw, ada_b, g_pre, g_q, g_kv, w1t, wqt, wkv,
                cos_t, sin_t, tm):
    s = x.shape[0]
    hp = HEADS * HEAD_PAD
    za = Q_LORA + KV_LORA + HEAD_PAD

    def body(dq_ref, dk_ref, dv_ref, z_ref, dguv_ref, x_ref, dxp_ref, ar_ref, ab_ref, g_ref, gq_ref, gkv_ref,
             w1_ref, wq_ref, wkv_ref, cos_ref, sin_ref,
             gx_ref, dza_ref, dqp_ref, dkvp_ref, dsh_ref, dsc_ref, dg_ref, dgq_ref, dgkv_ref):
        i = pl.program_id(0)

        @pl.when(i == 0)
        def _():
            for r in (dsh_ref, dsc_ref, dg_ref, dgq_ref, dgkv_ref):
                r[...] = jnp.zeros(r.shape, F32)

        cos, sin = cos_ref[...], sin_ref[...]
        d_krot = jnp.zeros((tm, HEAD_PAD), F32)
        for h in range(HEADS):
            blk = slice(h * HEAD_PAD, (h + 1) * HEAD_PAD)
            dqp_ref[:, blk] = _rope_transposed(dq_ref[:, blk], cos, sin).astype(MXU)
            dk_h = dk_ref[:, blk]
            d_krot = d_krot + dk_h
            dkvp_ref[:, blk] = dk_h.astype(MXU)
        dkvp_ref[:, pl.ds(hp, hp)] = dv_ref[...].astype(MXU)
        lane = lax.broadcasted_iota(jnp.int32, (tm, HEAD_PAD), 1)
        d_kr = jnp.where((lane >= NOPE) & (lane < NOPE + ROPE), _rope_transposed(d_krot, cos, sin), 0.0)
        d_cqn = _dot(dqp_ref[...], wq_ref[...])
        d_ckvn = _dot_nt(dkvp_ref[...], wkv_ref[...])
        zt = z_ref[...]
        gq, gkv = gq_ref[...], gkv_ref[...]
        cq_hat, rq = _rms(zt[:, :Q_LORA])
        ckv_hat, rkv = _rms(zt[:, Q_LORA:Q_LORA + KV_LORA])
        dgq_ref[...] += jnp.sum(d_cqn * cq_hat, axis=0, keepdims=True)
        dgkv_ref[...] += jnp.sum(d_ckvn * ckv_hat, axis=0, keepdims=True)
        d_cq = _rms_bwd(d_cqn * gq, cq_hat, rq)
        d_ckv = _rms_bwd(d_ckvn * gkv, ckv_hat, rkv)
        d_za = jnp.concatenate([d_cq, d_ckv, d_kr], axis=1).astype(MXU)
        dza_ref[...] = d_za
        d_h1 = _dot(d_za, w1_ref[pl.ds(0, za), :]) + _dot(dguv_ref[...], w1_ref[pl.ds(za, 2 * GM_WIDTH), :])
        sc1 = _row(ar_ref, 1) + _row(ab_ref, 1)
        g = g_ref[...]
        xn, r1 = _rms(x_ref[...])
        dsh_ref[...] += jnp.sum(d_h1, axis=0, keepdims=True)
        dsc_ref[...] += jnp.sum(d_h1 * (xn * g), axis=0, keepdims=True)
        d_mod = d_h1 * (1.0 + sc1)
        dg_ref[...] += jnp.sum(d_mod * xn, axis=0, keepdims=True)
        gx_ref[...] = dxp_ref[...] + _rms_bwd(d_mod * g, xn, r1)

    vec = pl.BlockSpec((1, D_MODEL), lambda i: (0, 0))
    return _call(
        body, name="mix_in_bwd", grid=(s // tm,), sem=("arbitrary",),
        in_specs=[_rows(tm, hp), _rows(tm, hp), _rows(tm, hp), _rows(tm, za), _rows(tm, 2 * GM_WIDTH),
                  _rows(tm, D_MODEL), _rows(tm, D_MODEL), _full(ada_raw.shape), _full(ada_b.shape), _full(g_pre.shape),
                  _full(g_q.shape), _full(g_kv.shape), _full(w1t.shape), _full(wqt.shape),
                  _full(wkv.shape), _rows(tm, HEAD_PAD), _rows(tm, HEAD_PAD)],
        out_specs=[_rows(tm, D_MODEL), _rows(tm, za), _rows(tm, hp), _rows(tm, 2 * hp), vec, vec, vec,
                   _full(g_q.shape), _full(g_kv.shape)],
        out_shape=[_sds((s, D_MODEL), F32), _sds((s, za), MXU), _sds((s, hp), MXU), _sds((s, 2 * hp), MXU),
                   _sds((1, D_MODEL), F32), _sds((1, D_MODEL), F32), _sds((1, D_MODEL), F32),
                   _sds(g_q.shape, F32), _sds(g_kv.shape, F32)],
    )(dq, dk, dv, z, d_guv, x, d_x_part, ada_raw, ada_b, g_pre, g_q, g_kv, w1t, wqt, wkv, cos_t, sin_t)


def _tn_matmuls(arrays, pairs, name, ts):
    s = arrays[0].shape[0]
    steps = s // ts
    n_in, n_out = len(arrays), len(pairs)
    shapes = [(arrays[ia].shape[1], arrays[ib].shape[1]) for ia, ib in pairs]

    def body(*refs):
        ins, outs, accs = refs[:n_in], refs[n_in:n_in + n_out], refs[n_in + n_out:]
        k = pl.program_id(0)

        @pl.when(k == 0)
        def _():
            for acc in accs:
                acc[...] = jnp.zeros(acc.shape, F32)

        for (ia, ib), acc in zip(pairs, accs):
            acc[...] += _dot_tn(ins[ia][...], ins[ib][...])

        @pl.when(k == steps - 1)
        def _():
            for out, acc in zip(outs, accs):
                out[...] = acc[...].astype(MXU)

    return _call(
        body, name=name, grid=(steps,), sem=("arbitrary",),
        in_specs=[_rows(ts, a.shape[1]) for a in arrays],
        out_specs=[_full(shape) for shape in shapes],
        out_shape=[_sds(shape, MXU) for shape in shapes],
        scratch=[pltpu.VMEM(shape, F32) for shape in shapes],
    )(*arrays)


def _adamw(w, g, m, v):
    m2 = ADAM_B1 * m + (1.0 - ADAM_B1) * g
    v2 = ADAM_B2 * v + (1.0 - ADAM_B2) * (g * g)
    m_hat = m2 / (1.0 - ADAM_B1 ** ADAM_STEP)
    v_hat = v2 / (1.0 - ADAM_B2 ** ADAM_STEP)
    delta = -ADAM_LR * (m_hat / (jnp.sqrt(v_hat) + ADAM_EPS) + ADAM_WD * w)
    return delta, m2, v2


def _adam_reduce(recv, w, m, v, name):
    r, c = w.shape
    tr = r if r <= 512 else max(t for t in range(16, 513, 16) if r % t == 0)

    def body(p_ref, w_ref, m_ref, v_ref, g_ref, d_ref, mo_ref, vo_ref):
        g = p_ref[0].astype(F32)
        for j in range(1, N_DEV):
            g = g + p_ref[j].astype(F32)
        g_ref[...] = g
        d_ref[...], mo_ref[...], vo_ref[...] = _adamw(w_ref[...], g, m_ref[...], v_ref[...])

    blk = pl.BlockSpec((tr, c), lambda i: (i, 0))
    return _call(
        body, name=name, grid=(r // tr,), sem=("parallel",),
        in_specs=[pl.BlockSpec((N_DEV, tr, c), lambda i: (0, i, 0)), blk, blk, blk],
        out_specs=[blk] * 4, out_shape=[_sds((r, c), F32)] * 4,
    )(recv, w, m, v)


def _adam_w_ada(c_act_t, d_ada_cols, w, m, v):
    r, c = w.shape
    tr = 256

    def body(ct_ref, da_ref, w_ref, m_ref, v_ref, g_ref, d_ref, mo_ref, vo_ref):
        g = _dot(ct_ref[...], da_ref[...])
        g_ref[...] = g
        d_ref[...], mo_ref[...], vo_ref[...] = _adamw(w_ref[...], g, m_ref[...], v_ref[...])

    blk = pl.BlockSpec((tr, c), lambda i: (i, 0))
    return _call(
        body, name="adam_w_ada", grid=(r // tr,), sem=("parallel",),
        in_specs=[pl.BlockSpec((tr, c_act_t.shape[1]), lambda i: (i, 0)), _full(d_ada_cols.shape), blk, blk, blk],
        out_specs=[blk] * 4, out_shape=[_sds((r, c), F32)] * 4,
    )(c_act_t, d_ada_cols, w, m, v)


VEC_ROWS = D_MODEL // 128
PK_ADA = 0
PK_GAIN = PK_ADA + 6 * VEC_ROWS
PK_GQ = PK_GAIN + 4 * VEC_ROWS
PK_GKV = PK_GQ + Q_LORA // 128
PK_LOSS = PK_GKV + KV_LORA // 128
PK_LNG = 88
PK_LNB = PK_LNG + HEADS
PK_BS = PK_LNB + HEADS
PK_CB = PK_BS + HEADS
CB_ROWS = 6
PK_WS = PK_CB + N_DEV * CB_ROWS
PK_ROWS = PK_WS + HEADS * GM_CHUNK
assert PK_LOSS < PK_LNG and PK_ROWS % 8 == 0
LATE_GAIN = 2 * VEC_ROWS
LATE_GQ = 3 * VEC_ROWS
LATE_GKV = LATE_GQ + Q_LORA // 128
LATE_ROWS = 32


def _cb_chunks():
    return [(k, k * 128, min(128, FF_BLK - k * 128)) for k in range(CB_ROWS)]


def _put_rows(out_ref, row0, ref, width):
    for k in range(width // 128):
        out_ref[pl.ds(row0 + k, 1), :] = ref[:, pl.ds(k * 128, 128)]


def _pack_small(ada_rows, gains, loss_part, d_ln_g, d_ln_b, d_bs, d_cb, d_ws):
    half = N_DEV // 2

    def body(*refs):
        vec_refs = refs[:7]
        loss_ref, lng_ref, lnb_ref, bs_ref, cb_ref, ws_ref, out_ref = refs[7:]
        out_ref[pl.ds(0, PK_WS), :] = jnp.zeros((PK_WS, 128), F32)
        for n, ref in enumerate(vec_refs[:4]):
            _put_rows(out_ref, PK_ADA + (2 + n) * VEC_ROWS, ref, D_MODEL)
        for n, ref in enumerate(vec_refs[4:]):
            _put_rows(out_ref, PK_GAIN + (1 + n) * VEC_ROWS, ref, D_MODEL)
        _put_rows(out_ref, PK_LOSS, loss_ref, 128)
        out_ref[pl.ds(PK_LNG, HEADS), pl.ds(0, GM_DIM)] = lng_ref[...]
        out_ref[pl.ds(PK_LNB, HEADS), pl.ds(0, GM_DIM)] = lnb_ref[...]
        out_ref[pl.ds(PK_BS, HEADS), :] = bs_ref[...]
        for j in range(N_DEV):
            for k, lane, width in _cb_chunks():
                out_ref[pl.ds(PK_CB + j * CB_ROWS + k, 1), pl.ds(0, width)] = cb_ref[j % half, j // half, :, pl.ds(lane, width)]
        for h in range(HEADS):
            out_ref[pl.ds(PK_WS + h * GM_CHUNK, GM_CHUNK), :] = ws_ref[h]

    ins = list(ada_rows) + list(gains) + [loss_part, d_ln_g, d_ln_b, d_bs, d_cb, d_ws]
    return _call(body, name="pack_small", grid=(1,), in_specs=[_full(a.shape) for a in ins],
                 out_specs=_full((PK_ROWS, 128)), out_shape=_sds((PK_ROWS, 128), F32))(*ins)


def _pack_late(d_sh1, d_sc1, d_g_pre_mix, d_g_q, d_g_kv):
    def body(sh_ref, sc_ref, g_ref, gq_ref, gkv_ref, out_ref):
        out_ref[...] = jnp.zeros((LATE_ROWS, 128), F32)
        _put_rows(out_ref, 0, sh_ref, D_MODEL)
        _put_rows(out_ref, VEC_ROWS, sc_ref, D_MODEL)
        _put_rows(out_ref, LATE_GAIN, g_ref, D_MODEL)
        _put_rows(out_ref, LATE_GQ, gq_ref, Q_LORA)
        _put_rows(out_ref, LATE_GKV, gkv_ref, KV_LORA)

    ins = [d_sh1, d_sc1, d_g_pre_mix, d_g_q, d_g_kv]
    return _call(body, name="pack_late", grid=(1,), in_specs=[_full(a.shape) for a in ins],
                 out_specs=_full((LATE_ROWS, 128)), out_shape=_sds((LATE_ROWS, 128), F32))(*ins)


def _adam_small(gathered, late, params):
    n_par = len(params)

    def body(p_ref, late_ref, *refs):
        ins = [refs[3 * n:3 * n + 3] for n in range(n_par)]
        outs = [refs[3 * n_par + 4 * n:3 * n_par + 4 * n + 4] for n in range(n_par)]
        loss_ref, dada_ref = refs[7 * n_par:]

        def total(rows, lanes=slice(None), src=p_ref):
            g = src[0, rows, lanes]
            for j in range(1, N_DEV):
                g = g + src[j, rows, lanes]
            return g

        def apply(n, g, idx):
            w_ref, m_ref, v_ref = ins[n]
            d, m2, v2 = _adamw(w_ref[idx], g, m_ref[idx], v_ref[idx])
            for ref, val in zip(outs[n], (g, d, m2, v2)):
                ref[idx] = val

        def vector(n, src, row0, width, lane0=0):
            for k in range(width // 128):
                apply(n, total(pl.ds(row0 + k, 1), src=src), (slice(None), pl.ds(lane0 + k * 128, 128)))

        vector(0, late_ref, 0, 2 * D_MODEL)
        vector(0, p_ref, PK_ADA + 2 * VEC_ROWS, 4 * D_MODEL, lane0=2 * D_MODEL)
        vector(1, late_ref, LATE_GAIN, D_MODEL)
        for n in range(1, 4):
            vector(1 + n, p_ref, PK_GAIN + n * VEC_ROWS, D_MODEL)
        vector(5, late_ref, LATE_GQ, Q_LORA)
        vector(6, late_ref, LATE_GKV, KV_LORA)
        apply(7, total(pl.ds(PK_LNG, HEADS), pl.ds(0, GM_DIM)), (0,))
        apply(8, total(pl.ds(PK_LNB, HEADS), pl.ds(0, GM_DIM)), (0,))
        for h in range(HEADS):
            apply(9, total(pl.ds(PK_WS + h * GM_CHUNK, GM_CHUNK)), (0, h))
        apply(10, total(pl.ds(PK_BS, HEADS)), (0,))
        for j in range(N_DEV):
            for k, lane, width in _cb_chunks():
                apply(11, total(pl.ds(PK_CB + j * CB_ROWS + k, 1), pl.ds(0, width)), (pl.ds(j, 1), pl.ds(lane, width)))
        loss_ref[...] = total(pl.ds(PK_LOSS, 1))
        dada_ref[:, pl.ds(0, 2 * VEC_ROWS), :] = late_ref[:, pl.ds(0, 2 * VEC_ROWS), :]
        dada_ref[:, pl.ds(2 * VEC_ROWS, 4 * VEC_ROWS), :] = p_ref[:, pl.ds(PK_ADA + 2 * VEC_ROWS, 4 * VEC_ROWS), :]

    flat = [a for triple in params for a in triple]
    out_shape = [_sds(w.shape, F32) for w, _, _ in params for _ in range(4)]
    out_shape += [_sds((1, 128), F32), _sds((N_DEV, 6 * VEC_ROWS, 128), F32)]
    outs = _call(body, name="adam_small", grid=(1,),
                 in_specs=[_full(gathered.shape), _full(late.shape)] + [_full(a.shape) for a in flat],
                 out_specs=[_full(o.shape) for o in out_shape], out_shape=out_shape)(gathered, late, *flat)
    return [tuple(outs[4 * n:4 * n + 4]) for n in range(n_par)], outs[-2], outs[-1]


def _rope_tables(s):
    pos = jnp.arange(s, dtype=F32)
    inv = ROPE_THETA ** (-jnp.arange(0, ROPE, 2, dtype=F32) / ROPE)
    lane_inv = jnp.concatenate([jnp.zeros((NOPE,), F32), inv, inv, jnp.zeros((HEAD_PAD - NOPE - ROPE,), F32)])
    ang = pos[:, None] * lane_inv[None, :]
    return jnp.cos(ang), jnp.sin(ang)


def kernel(x, c, w_ada, b_ada, g_pre_mix, g_post_mix, w_in, g_q, w_uq, g_kv, w_ukv, gm_ln_g, gm_ln_b, w_spatial, b_spatial, w_out, g_pre_ffn, g_post_ffn, w_up, conv_w, conv_b, w_down, loss_target, m_w_ada, m_b_ada, m_g_pre_mix, m_g_post_mix, m_w_in, m_g_q, m_w_uq, m_g_kv, m_w_ukv, m_gm_ln_g, m_gm_ln_b, m_w_spatial, m_b_spatial, m_w_out, m_g_pre_ffn, m_g_post_ffn, m_w_up, m_conv_w, m_conv_b, m_w_down, v_w_ada, v_b_ada, v_g_pre_mix, v_g_post_mix, v_w_in, v_g_q, v_w_uq, v_g_kv, v_w_ukv, v_gm_ln_g, v_gm_ln_b, v_w_spatial, v_b_spatial, v_w_out, v_g_pre_ffn, v_g_post_ffn, v_w_up, v_conv_w, v_conv_b, v_w_down):
    s = x.shape[1]
    tm = min(512, s)
    tf = min(2 * ROW_SUB, s)
    tq = min(512, s)
    ts = min(2048, s)
    hp = HEADS * HEAD_PAD
    half = N_DEV // 2
    my_slot = 4 * lax.axis_index("x") + 2 * lax.axis_index("y") + lax.axis_index("c")
    x2d, target = x[0], loss_target[0]

    def t_(a):
        return jnp.swapaxes(a[0], 0, 1)

    w_in_t, m_in_t, v_in_t = t_(w_in), t_(m_w_in), t_(v_w_in)
    w_uq_t, m_uq_t, v_uq_t = t_(w_uq), t_(m_w_uq), t_(v_w_uq)
    w_up_t, m_up_t, v_up_t = t_(w_up), t_(m_w_up), t_(v_w_up)
    (g_c, g_in_t, g_uq_t, g_ukv, g_cw), _ = _exchange(
        [c, w_in_t.astype(MXU), w_uq_t.astype(MXU), w_ukv[0].astype(MXU), conv_w[0]], [], "gather_mixer_weights")

    w_in_f = g_in_t.reshape(-1, D_MODEL)
    o1, o2, o3 = Q_LORA, Q_LORA + KV_LORA, Q_LORA + KV_LORA + ROPE
    w1t = jnp.concatenate([w_in_f[:o2], jnp.zeros((NOPE, D_MODEL), MXU), w_in_f[o2:o3],
                           jnp.zeros((HEAD_PAD - NOPE - ROPE, D_MODEL), MXU), w_in_f[o3:]], axis=0)
    wqt = jnp.pad(g_uq_t, ((0, 0), (0, HEAD_PAD - NOPE - ROPE), (0, 0))).reshape(hp, Q_LORA)
    w_ukv_f = jnp.transpose(g_ukv, (1, 0, 2)).reshape(KV_LORA, HEADS, 2 * NOPE)
    pad_head = ((0, 0), (0, 0), (0, HEAD_PAD - NOPE))
    wkv = jnp.concatenate([jnp.pad(w_ukv_f[:, :, :NOPE], pad_head).reshape(KV_LORA, hp),
                           jnp.pad(w_ukv_f[:, :, NOPE:], pad_head).reshape(KV_LORA, hp)], axis=1)
    cb8 = conv_b.reshape(N_DEV, 1, FF_BLK)
    bias_exp = jnp.repeat(b_spatial[0].T, GM_DIM, axis=1)
    ln_g, ln_b = gm_ln_g.reshape(1, GM_WIDTH), gm_ln_b.reshape(1, GM_WIDTH)
    w_sp = w_spatial[0]
    cos_t, sin_t = _rope_tables(s)

    ada_part, c_act = _ada_fwd(g_c.reshape(N_DEV, D_MODEL), w_ada[0])
    _, (ada_recv,) = _exchange([], [(ada_part.reshape(N_DEV, 1, -1), _plain_slot)], "ada_rows")
    ada_raw = ada_recv.reshape(6, D_MODEL)
    ada_b = b_ada.reshape(6, D_MODEL)

    h1, z, qp, kp, vp, cqn, ckvn = _mix_in_fwd(x2d, ada_raw, ada_b, g_pre_mix, w1t, g_q, g_kv, wqt, wkv, cos_t, sin_t, tm)
    sgu = _gmlp_fwd(z, ln_g, ln_b, w_sp, bias_exp, tm)
    o_pad, lse, (g_out, g_up, g_down) = _attn_fwd(
        qp, kp, vp, tq, [w_out[0].astype(MXU), w_up_t.astype(MXU), w_down[0].astype(MXU)])
    w_out_f = g_out.reshape(2 * GM_WIDTH, D_MODEL)
    wo_attn = jnp.pad(w_out_f[:GM_WIDTH].reshape(HEADS, NOPE, D_MODEL), ((0, 0), (0, HEAD_PAD - NOPE), (0, 0)))
    wo = jnp.concatenate([wo_attn.reshape(hp, D_MODEL), w_out_f[GM_WIDTH:]], axis=0)
    wd = g_down.reshape(half, FF_BLK, D_MODEL)
    m_mix, x2, h2 = _out_proj_fwd(o_pad, sgu, wo, x2d, ada_raw, ada_b, g_post_mix, g_pre_ffn, tm)
    up_a, up_b, y_a, y_b, act = _ffn_up_fwd(h2, g_up, g_cw, cb8, tf)
    d_out, d_f, loss_part, d_gt2, d_g_post_ffn = _ffn_down_fwd(act, wd, x2, target, ada_raw, ada_b, g_post_ffn, tf)

    d_up, d_cw, d_cb, p_down, p_up = _ffn_down_bwd(d_f, wd, up_a, up_b, y_a, y_b, g_cw, act, h2, tf)
    p_down = p_down.reshape(N_DEV, -1, D_MODEL)
    d_x2, d_m, d_sh2, d_sc2, d_g_pre_ffn, d_gt1, d_g_post_mix = _ffn_up_bwd(
        d_up, g_up, x2, m_mix, d_out, ada_raw, ada_b, g_pre_ffn, g_post_mix, tm)
    dwo_attn, dwo_sgu = _tn_matmuls([o_pad, sgu, d_m], [(0, 2), (1, 2)], "dw_out", ts)
    dwo_attn = dwo_attn.reshape(HEADS, HEAD_PAD, D_MODEL)[:, :NOPE]
    p_out = jnp.concatenate([dwo_attn.reshape(GM_WIDTH, D_MODEL), dwo_sgu], axis=0).reshape(N_DEV, -1, D_MODEL)
    d_o, d_sgu, delta = _out_proj_bwd(d_m, wo, o_pad, tm)
    d_guv, d_ws, d_bs, d_ln_g, d_ln_b = _gmlp_bwd(z, d_sgu, ln_g, ln_b, w_sp, bias_exp, tm)
    packed = _pack_small([d_gt1, d_sh2, d_sc2, d_gt2], [d_g_post_mix, d_g_pre_ffn, d_g_post_ffn], loss_part,
                         d_ln_g, d_ln_b, d_bs, d_cb, d_ws)

    def ffn_slot(j):
        return (j % half, j // half)

    dq, dk, dv, (r_out, r_up, r_down, r_cw), (g_small,) = _attn_bwd(
        qp, kp, vp, d_o, lse, delta, tq,
        [(p_out, _plain_slot), (p_up, ffn_slot), (p_down, _plain_slot), (d_cw, ffn_slot)], [packed])
    grad_x, d_za, d_qp, d_kvp, d_sh1, d_sc1, d_g_pre_mix, d_g_q, d_g_kv = _mix_in_bwd(
        dq, dk, dv, z, d_guv, x2d, d_x2, ada_raw, ada_b, g_pre_mix, g_q, g_kv, w1t, wqt, wkv, cos_t, sin_t,
        min(256, s))
    dw1a, dw1b, dwq, dwkv = _tn_matmuls([d_za, d_guv, h1, d_qp, cqn, ckvn, d_kvp],
                                        [(0, 2), (1, 2), (3, 4), (5, 6)], "dw_mixer", ts // 2)
    d_w_in_t = jnp.concatenate([dw1a[:o2], dw1a[o2 + NOPE:o2 + NOPE + ROPE], dw1b], axis=0)
    p_in = d_w_in_t.reshape(N_DEV, -1, D_MODEL)
    p_uq = dwq.reshape(HEADS, HEAD_PAD, Q_LORA)[:, :NOPE + ROPE]
    dwk = dwkv[:, :hp].reshape(KV_LORA, HEADS, HEAD_PAD)[:, :, :NOPE]
    dwv = dwkv[:, hp:].reshape(KV_LORA, HEADS, HEAD_PAD)[:, :, :NOPE]
    p_ukv = jnp.transpose(jnp.concatenate([dwk, dwv], axis=2), (1, 0, 2))

    (g_late,), (r_in, r_uq, r_ukv) = _exchange(
        [_pack_late(d_sh1, d_sc1, d_g_pre_mix, d_g_q, d_g_kv)],
        [(p_in, _plain_slot), (p_uq, _plain_slot), (p_ukv, _plain_slot)], "final_exchange")
    small_params = [(b_ada, m_b_ada, v_b_ada), (g_pre_mix, m_g_pre_mix, v_g_pre_mix),
                    (g_post_mix, m_g_post_mix, v_g_post_mix), (g_pre_ffn, m_g_pre_ffn, v_g_pre_ffn),
                    (g_post_ffn, m_g_post_ffn, v_g_post_ffn), (g_q, m_g_q, v_g_q), (g_kv, m_g_kv, v_g_kv),
                    (gm_ln_g, m_gm_ln_g, v_gm_ln_g), (gm_ln_b, m_gm_ln_b, v_gm_ln_b),
                    (w_spatial, m_w_spatial, v_w_spatial), (b_spatial, m_b_spatial, v_b_spatial),
                    tuple(a.reshape(N_DEV, FF_BLK) for a in (conv_b, m_conv_b, v_conv_b))]
    small_out, loss_row, d_ada_all = _adam_small(g_small, g_late, small_params)
    small_out[11] = tuple(o.reshape(conv_b.shape) for o in small_out[11])
    loss = loss_row[0, 0]

    def big(recv, w, m, v, name):
        g, d, m2, v2 = _adam_reduce(recv, w[0], m[0], v[0], name)
        return g[None], d[None], m2[None], v2[None]

    def big_t(recv, w_t, m_t, v_t, name):
        return tuple(jnp.swapaxes(o, 0, 1)[None] for o in _adam_reduce(recv, w_t, m_t, v_t, name))

    a_in = big_t(r_in, w_in_t, m_in_t, v_in_t, "adam_w_in")
    a_uq = big_t(r_uq, w_uq_t, m_uq_t, v_uq_t, "adam_w_uq")
    a_ukv = big(r_ukv, w_ukv, m_w_ukv, v_w_ukv, "adam_w_ukv")
    a_out = big(r_out, w_out, m_w_out, v_w_out, "adam_w_out")
    a_up = big_t(r_up, w_up_t, m_up_t, v_up_t, "adam_w_up")
    a_down = big(r_down, w_down, m_w_down, v_w_down, "adam_w_down")
    ada_cols = w_ada.shape[2]
    d_ada_cols = lax.dynamic_slice(d_ada_all.reshape(N_DEV, 6 * D_MODEL), (0, my_slot * ada_cols), (N_DEV, ada_cols))
    pad_seq = 128 - N_DEV
    a_ada = tuple(t[None] for t in _adam_w_ada(jnp.pad(c_act.T, ((0, 0), (0, pad_seq))).astype(MXU),
                                               jnp.pad(d_ada_cols, ((0, pad_seq), (0, 0))).astype(MXU),
                                               w_ada[0], m_w_ada[0], v_w_ada[0]))
    a_cw = big(r_cw, conv_w, m_conv_w, v_conv_w, "adam_conv_w")

    def small(k):
        return small_out[k]

    per_weight = [a_ada, small(0), small(1), small(2), a_in, small(5), a_uq, small(6), a_ukv, small(7), small(8),
                  small(9), small(10), a_out, small(3), small(4), a_up, a_cw, small(11), a_down]
    outs = [loss, grad_x[None]]
    for k in range(4):
        outs += [t[k] for t in per_weight]
    return tuple(outs)
```

```python
import functools

import jax
import jax.numpy as jnp
from jax import lax
from jax.experimental import pallas as pl
from jax.experimental.pallas import tpu as pltpu

F32 = jnp.float32
MXU = jnp.bfloat16

N_DEV = 8
D_MODEL = 1024
HEADS = 8
HEAD_PAD = 128
NOPE = 64
ROPE = 32
Q_LORA = 256
KV_LORA = 128
GM_WIDTH = 512
GM_DIM = 64
GM_CHUNK = 128
CHUNK_SHIFT = 6
ROPE_THETA = 10000.0
ATTN_SCALE = (NOPE + ROPE) ** -0.5
LOG2E = 1.4426950408889634
SCALE_LOG2E = ATTN_SCALE * LOG2E
Z_COLS = 1536
FF_BLK = 704
EPS = 1e-6
ADAM_LR = 0.001
ADAM_B1 = 0.9
ADAM_B2 = 0.999
ADAM_EPS = 1e-08
ADAM_WD = 0.01
ADAM_STEP = 10
VMEM_LIMIT = 56 * 1024 * 1024
MESH = pl.DeviceIdType.MESH


def _dot(a, b):
    return jnp.dot(a, b, preferred_element_type=F32)


def _dot_nt(a, b):
    return lax.dot_general(a, b, (((1,), (1,)), ((), ())), preferred_element_type=F32)


def _dot_tn(a, b):
    return lax.dot_general(a, b, (((0,), (0,)), ((), ())), preferred_element_type=F32)


def _call(body, *, name, grid, in_specs, out_specs, out_shape, scratch=(), sem=None):
    params = pltpu.CompilerParams(dimension_semantics=sem, vmem_limit_bytes=VMEM_LIMIT)
    return pl.pallas_call(body, name=name, grid=grid, in_specs=in_specs, out_specs=out_specs,
                          out_shape=out_shape, scratch_shapes=list(scratch), compiler_params=params)


def _full(shape):
    n = len(shape)
    return pl.BlockSpec(shape, lambda *_: (0,) * n)


def _rows(tm, cols, col_block=0):
    return pl.BlockSpec((tm, cols), lambda i: (i, col_block))


def _sds(shape, dtype):
    return jax.ShapeDtypeStruct(shape, dtype)


def _row(ref, k):
    return ref[pl.ds(k, 1), :]


def _rms(x):
    r = lax.rsqrt(jnp.mean(x * x, axis=-1, keepdims=True) + EPS)
    return x * r, r


def _rms_bwd(d_hat, hat, r):
    return r * (d_hat - hat * jnp.mean(d_hat * hat, axis=-1, keepdims=True))


def _rope_partner(t):
    lane = lax.broadcasted_iota(jnp.int32, t.shape, 1)
    swapped = jnp.where(lane < NOPE + ROPE // 2, -pltpu.roll(t, HEAD_PAD - ROPE // 2, 1), pltpu.roll(t, ROPE // 2, 1))
    return jnp.where((lane >= NOPE) & (lane < NOPE + ROPE), swapped, 0.0)


def _rope(t, cos, sin):
    return t * cos + _rope_partner(t) * sin


def _rope_transposed(g, cos, sin):
    return g * cos - _rope_partner(g * sin)


def _gelu(x):
    return x * (0.5 * (1.0 + jnp.tanh(0.7978845608028654 * (x + 0.044715 * (x * x * x)))))


def _gelu_grad(x):
    t = jnp.tanh(0.7978845608028654 * (x + 0.044715 * (x * x * x)))
    return 0.5 * (1.0 + t) + 0.5 * x * (1.0 - t * t) * (0.7978845608028654 * (1.0 + 3.0 * 0.044715 * (x * x)))


def _split_dot(x, mat):
    hi = x.astype(MXU)
    lo = (x - hi.astype(F32)).astype(MXU)
    return _dot(hi, mat) + _dot(lo, mat)


def _split_dot3(x, mat):
    hi = x.astype(MXU)
    r1 = x - hi.astype(F32)
    mid = r1.astype(MXU)
    lo = (r1 - mid.astype(F32)).astype(MXU)
    return (_dot(hi, mat) + _dot(mid, mat)) + _dot(lo, mat)


def _seg_matrix():
    r = lax.broadcasted_iota(jnp.int32, (GM_WIDTH, GM_WIDTH), 0) >> 6
    c = lax.broadcasted_iota(jnp.int32, (GM_WIDTH, GM_WIDTH), 1) >> 6
    return jnp.where(r == c, 1.0 / GM_DIM, 0.0).astype(MXU)


def _spatial_mask():
    i = lax.broadcasted_iota(jnp.int32, (GM_CHUNK, GM_CHUNK), 0) >> CHUNK_SHIFT
    j = lax.broadcasted_iota(jnp.int32, (GM_CHUNK, GM_CHUNK), 1) >> CHUNK_SHIFT
    return (j <= i).astype(F32)


def _my_place():
    return lax.axis_index("x"), lax.axis_index("y"), lax.axis_index("c")


def _flat(p):
    return 4 * p[0] + 2 * p[1] + p[2]


def _comm_sems(n):
    return [pltpu.SemaphoreType.DMA((7 * n,)), pltpu.SemaphoreType.DMA((7 * n,)), pltpu.SemaphoreType.DMA((n,))]


def _gather_steps(ins, outs, sems):
    send_sems, recv_sems, local_sems = sems
    n = len(ins)
    x, y, c = _my_place()
    me, sibling = (x, y, c), (x, y, 1 - c)
    chips = [(1 - x, y), (x, 1 - y), (1 - x, 1 - y)]

    def copy(a, k, block, to, src=None):
        slot = outs[a].at[_flat(block)]
        return pltpu.make_async_remote_copy(
            src_ref=slot if src is None else src, dst_ref=slot,
            send_sem=send_sems.at[7 * a + k], recv_sem=recv_sems.at[7 * a + k],
            device_id=to, device_id_type=MESH)

    def mine():
        return [pltpu.make_async_copy(ins[a], outs[a].at[_flat(me)], local_sems.at[a]) for a in range(n)]

    def first():
        cps = []
        for a in range(n):
            cps.append(copy(a, 0, me, sibling, src=ins[a]))
            cps += [copy(a, 1 + j, me, (*chip, c), src=ins[a]) for j, chip in enumerate(chips)]
        return cps

    def passed():
        return [copy(a, 4 + j, (*chip, c), sibling) for a in range(n) for j, chip in enumerate(chips)]

    def start():
        for cp in mine() + first():
            cp.start()

    def forward():
        for a in range(n):
            for j, chip in enumerate(chips):
                copy(a, 1 + j, (*chip, c), me).wait_recv()
                copy(a, 4 + j, (*chip, c), sibling).start()

    def finish():
        for a in range(n):
            copy(a, 0, sibling, me).wait_recv()
            for j, chip in enumerate(chips):
                copy(a, 4 + j, (*chip, 1 - c), me).wait_recv()
        for cp in first() + passed():
            cp.wait_send()
        for cp in mine():
            cp.wait()

    return start, forward, finish


def _scatter_steps(ins, outs, sems, slots):
    send_sems, recv_sems, local_sems = sems
    n = len(ins)
    flips = [(fx, fy, fc) for fx in (0, 1) for fy in (0, 1) for fc in (0, 1)][1:]
    me = _my_place()

    def peer(f):
        return tuple(1 - v if b else v for v, b in zip(me, f))

    def copy(a, k, arriving=False):
        p = peer(flips[k])
        return pltpu.make_async_remote_copy(
            src_ref=ins[a].at[slots[a](_flat(p))], dst_ref=outs[a].at[_flat(p if arriving else me)],
            send_sem=send_sems.at[7 * a + k], recv_sem=recv_sems.at[7 * a + k],
            device_id=p, device_id_type=MESH)

    def mine():
        return [pltpu.make_async_copy(ins[a].at[slots[a](_flat(me))], outs[a].at[_flat(me)], local_sems.at[a])
                for a in range(n)]

    def start():
        for cp in mine() + [copy(a, k) for a in range(n) for k in range(7)]:
            cp.start()

    def finish():
        for a in range(n):
            for k in range(7):
                copy(a, k, arriving=True).wait_recv()
        for a in range(n):
            for k in range(7):
                copy(a, k).wait_send()
        for cp in mine():
            cp.wait()

    return start, finish


def _plain_slot(j):
    return (j,)


def _scatter_out_shape(arr, slot):
    return _sds((N_DEV,) + arr.shape[len(slot(0)):], arr.dtype)


def _exchange(gathered, scattered, name):
    ng, ns = len(gathered), len(scattered)
    slots = [slot for _, slot in scattered]

    def body(*refs):
        g_in, s_in = refs[:ng], refs[ng:ng + ns]
        g_out, s_out = refs[ng + ns:2 * ng + ns], refs[2 * ng + ns:2 * (ng + ns)]
        sems = refs[2 * (ng + ns):]
        g_start, g_forward, g_finish = _gather_steps(g_in, g_out, sems[:3])
        s_start, s_finish = _scatter_steps(s_in, s_out, sems[3:], slots)
        g_start()
        s_start()
        g_forward()
        g_finish()
        s_finish()

    any_spec = pl.BlockSpec(memory_space=pl.ANY)
    outs = pl.pallas_call(
        body, name=name,
        in_specs=[any_spec] * (ng + ns), out_specs=[any_spec] * (ng + ns),
        out_shape=[_sds((N_DEV,) + a.shape, a.dtype) for a in gathered]
        + [_scatter_out_shape(a, slot) for a, slot in scattered],
        scratch_shapes=_comm_sems(max(ng, 1)) + _comm_sems(max(ns, 1)),
    )(*gathered, *[a for a, _ in scattered])
    return outs[:ng], outs[ng:]


def _ada_fwd(c_all, w_ada):
    def body(c_ref, w_ref, part_ref, act_ref):
        cv = c_ref[...]
        act = cv * jax.nn.sigmoid(cv)
        act_ref[...] = act
        part_ref[...] = _dot(act.astype(MXU), w_ref[...].astype(MXU))

    cols = w_ada.shape[1]
    return _call(body, name="ada_fwd", grid=(1,),
                 in_specs=[_full(c_all.shape), _full(w_ada.shape)],
                 out_specs=[_full((N_DEV, cols)), _full(c_all.shape)],
                 out_shape=[_sds((N_DEV, cols), F32), _sds(c_all.shape, F32)])(c_all, w_ada)


def _mix_in_fwd(x, ada_raw, ada_b, g_pre, w1, g_q, g_kv, wq, wkv, cos_t, sin_t, tm):
    s = x.shape[0]

    def body(x_ref, ar_ref, ab_ref, g_ref, w1_ref, gq_ref, gkv_ref, wq_ref, wkv_ref, cos_ref, sin_ref,
             h1_ref, z_ref, qp_ref, kp_ref, vp_ref, cqn_ref, ckvn_ref, w1_n, wq_n):
        @pl.when(pl.program_id(0) == 0)
        def _():
            w1_n[...] = w1_ref[...].T
            wq_n[...] = wq_ref[...].T

        sh = _row(ar_ref, 0) + _row(ab_ref, 0)
        sc = _row(ar_ref, 1) + _row(ab_ref, 1)
        xn, _ = _rms(x_ref[...])
        hb = ((xn * g_ref[...]) * (1.0 + sc) + sh).astype(MXU)
        h1_ref[...] = hb
        z = _dot(hb, w1_n[...])
        z_ref[...] = z
        cos, sin = cos_ref[...], sin_ref[...]
        cqn = (_rms(z[:, :Q_LORA])[0] * gq_ref[...]).astype(MXU)
        ckvn = (_rms(z[:, Q_LORA:Q_LORA + KV_LORA])[0] * gkv_ref[...]).astype(MXU)
        cqn_ref[...] = cqn
        ckvn_ref[...] = ckvn
        q = _dot(cqn, wq_n[...])
        kv = _dot(ckvn, wkv_ref[...])
        k_rope = _rope(z[:, Q_LORA + KV_LORA:Q_LORA + KV_LORA + HEAD_PAD], cos, sin)
        for h in range(HEADS):
            blk = slice(h * HEAD_PAD, (h + 1) * HEAD_PAD)
            qp_ref[:, blk] = _rope(q[:, blk], cos, sin).astype(MXU)
            kp_ref[:, blk] = (kv[:, blk] + k_rope).astype(MXU)
        v_lane = lax.broadcasted_iota(jnp.int32, (tm, HEADS * HEAD_PAD), 1) & (HEAD_PAD - 1)
        vp_ref[...] = jnp.where(v_lane == NOPE, 1.0, kv[:, HEADS * HEAD_PAD:]).astype(MXU)

    hp = HEADS * HEAD_PAD
    return _call(
        body, name="mix_in_fwd", grid=(s // tm,), sem=("arbitrary",),
        in_specs=[_rows(tm, D_MODEL), _full(ada_raw.shape), _full(ada_b.shape), _full(g_pre.shape), _full(w1.shape),
                  _full(g_q.shape), _full(g_kv.shape), _full(wq.shape), _full(wkv.shape),
                  _rows(tm, HEAD_PAD), _rows(tm, HEAD_PAD)],
        out_specs=[_rows(tm, D_MODEL), _rows(tm, Z_COLS), _rows(tm, hp), _rows(tm, hp), _rows(tm, hp),
                   _rows(tm, Q_LORA), _rows(tm, KV_LORA)],
        out_shape=[_sds((s, D_MODEL), MXU), _sds((s, Z_COLS), F32), _sds((s, hp), MXU), _sds((s, hp), MXU),
                   _sds((s, hp), MXU), _sds((s, Q_LORA), MXU), _sds((s, KV_LORA), MXU)],
        scratch=[pltpu.VMEM(w1.shape[::-1], MXU), pltpu.VMEM(wq.shape[::-1], MXU)],
    )(x, ada_raw, ada_b, g_pre, w1, g_q, g_kv, wq, wkv, cos_t, sin_t)


def _gm_norm(zv, seg):
    gv = _gelu(zv)
    cen = gv - _split_dot(gv, seg)
    rstd = lax.rsqrt(_split_dot(cen * cen, seg) + EPS)
    return gv, cen * rstd, rstd


def _gm_pairs(rows):
    first = lax.broadcasted_iota(jnp.int32, (rows, 2 * GM_DIM), 1) < GM_DIM
    return [(slice(p * 2 * GM_DIM, (p + 1) * 2 * GM_DIM), first) for p in range(HEADS // 2)]


def _gm_mix(wm, vb, rows, transposed=False):
    dot = _dot_tn if transposed else _dot
    return jnp.concatenate([jnp.where(first, dot(wm[2 * p], vb[:, lanes]), dot(wm[2 * p + 1], vb[:, lanes]))
                            for p, (lanes, first) in enumerate(_gm_pairs(rows))], axis=1)


def _gmlp_fwd(z, ln_g, ln_b, w_sp, bias_exp, tm):
    s = z.shape[0]
    nblk = tm // GM_CHUNK

    def body(zu_ref, zv_ref, lg_ref, lb_ref, w_ref, be_ref, sgu_ref):
        seg = _seg_matrix()
        mask = _spatial_mask()
        wm = [(w_ref[h] * mask).astype(MXU) for h in range(HEADS)]
        gu = _gelu(zu_ref[...])
        _, vhat, _ = _gm_norm(zv_ref[...], seg)
        vln = (vhat * lg_ref[...] + lb_ref[...]).astype(MXU)
        for n in range(nblk):
            rows = slice(n * GM_CHUNK, (n + 1) * GM_CHUNK)
            mixed = _gm_mix(wm, vln[rows], GM_CHUNK) + be_ref[...]
            sgu_ref[rows, :] = (gu[rows] * mixed).astype(MXU)

    return _call(
        body, name="gmlp_fwd", grid=(s // tm,), sem=("parallel",),
        in_specs=[_rows(tm, GM_WIDTH, 1), _rows(tm, GM_WIDTH, 2), _full(ln_g.shape), _full(ln_b.shape),
                  _full(w_sp.shape), _full(bias_exp.shape)],
        out_specs=_rows(tm, GM_WIDTH), out_shape=_sds((s, GM_WIDTH), MXU),
    )(z, z, ln_g, ln_b, w_sp, bias_exp)


def _chunk_mask(n_q, n_k, q_off):
    qc = (q_off + lax.broadcasted_iota(jnp.int32, (n_q, n_k), 0)) >> CHUNK_SHIFT
    kc = lax.broadcasted_iota(jnp.int32, (n_q, n_k), 1) >> CHUNK_SHIFT
    return kc <= qc


NEG_BIG = -1e30
ATTN_HEADS_PER_STEP = 2


def _attn_fwd(qp, kp, vp, tq, gathered):
    s = qp.shape[0]
    nq = s // tq
    hb = ATTN_HEADS_PER_STEP
    groups = HEADS // hb
    width = hb * HEAD_PAD
    ng = len(gathered)

    def body(q_ref, k_ref, v_ref, *rest):
        g_in, (o_ref, lse_ref), g_out = rest[:ng], rest[ng:ng + 2], rest[ng + 2:2 * ng + 2]
        m_sc, acc_sc = rest[2 * ng + 2:2 * ng + 4]
        g_start, g_forward, g_finish = _gather_steps(g_in, g_out, rest[2 * ng + 4:])
        g, i = pl.program_id(0), pl.program_id(1)
        pl.when((g == 0) & (i == 0))(g_start)
        pl.when((g == groups - 1) & (i == 0))(g_forward)
        m_sc[...] = jnp.full(m_sc.shape, NEG_BIG, F32)
        acc_sc[...] = jnp.zeros(acc_sc.shape, F32)

        def tile(j, masked):
            rows = pl.ds(pl.multiple_of(j * tq, tq), tq)
            for hh in range(hb):
                lanes = slice(hh * HEAD_PAD, (hh + 1) * HEAD_PAD)
                sc = _dot_nt(q_ref[:, lanes], k_ref[rows, lanes])
                if masked:
                    sc = jnp.where(_chunk_mask(tq, tq, 0), sc, NEG_BIG)
                blocks = [sc[:, b * 128:(b + 1) * 128] for b in range(tq // 128)]
                m_prev = m_sc[hh]
                m_tile = jnp.max(functools.reduce(jnp.maximum, blocks), axis=-1, keepdims=True)
                m_new = jnp.maximum(m_prev, m_tile)
                alpha = jnp.exp2((m_prev - m_new) * SCALE_LOG2E)
                p = jnp.concatenate([jnp.exp2((b - m_new) * SCALE_LOG2E) for b in blocks], axis=1).astype(MXU)
                acc_sc[hh] = alpha * acc_sc[hh] + _dot(p, v_ref[rows, lanes])
                m_sc[hh] = m_new

        def off_diagonal_pair(p, carry):
            tile(2 * p, False)
            tile(2 * p + 1, False)
            return carry

        lax.fori_loop(0, i // 2, off_diagonal_pair, 0)

        @pl.when(i % 2 == 1)
        def _():
            tile(i - 1, False)

        tile(i, True)
        for hh in range(hb):
            lanes = slice(hh * HEAD_PAD, (hh + 1) * HEAD_PAD)
            acc = acc_sc[hh]
            denom = acc[:, NOPE:NOPE + 1]
            o_ref[:, lanes] = (acc / denom).astype(MXU)
            lse_ref[hh] = m_sc[hh][:, :1] * SCALE_LOG2E + jnp.log(denom) * LOG2E
        pl.when((g == groups - 1) & (i == nq - 1))(g_finish)

    q_spec = pl.BlockSpec((tq, width), lambda g, i: (i, g))
    kv_spec = pl.BlockSpec((s, width), lambda g, i: (0, g))
    any_spec = pl.BlockSpec(memory_space=pl.ANY)
    outs = _call(
        body, name="attn_fwd", grid=(groups, nq), sem=("arbitrary", "arbitrary"),
        in_specs=[q_spec, kv_spec, kv_spec] + [any_spec] * ng,
        out_specs=[q_spec, pl.BlockSpec((hb, tq, 1), lambda g, i: (g, i, 0))] + [any_spec] * ng,
        out_shape=[_sds(qp.shape, MXU), _sds((HEADS, s, 1), F32)]
        + [_sds((N_DEV,) + a.shape, a.dtype) for a in gathered],
        scratch=[pltpu.VMEM((hb, tq, HEAD_PAD), F32), pltpu.VMEM((hb, tq, HEAD_PAD), F32)] + _comm_sems(ng),
    )(qp, kp, vp, *gathered)
    return outs[0], outs[1], outs[2:]


def _out_proj_fwd(o_pad, sgu, wo, x, ada_raw, ada_b, g_post_mix, g_pre_ffn, tm):
    s = x.shape[0]
    hp = HEADS * HEAD_PAD

    def body(o_ref, sgu_ref, wo_ref, x_ref, ar_ref, ab_ref, gpm_ref, gpf_ref, m_ref, x2_ref, h2_ref):
        gt1 = _row(ar_ref, 2) + _row(ab_ref, 2)
        sh2 = _row(ar_ref, 3) + _row(ab_ref, 3)
        sc2 = _row(ar_ref, 4) + _row(ab_ref, 4)
        m = _dot(o_ref[...], wo_ref[pl.ds(0, hp), :]) + _dot(sgu_ref[...], wo_ref[pl.ds(hp, GM_WIDTH), :])
        m_ref[...] = m
        x2 = x_ref[...] + gt1 * (_rms(m)[0] * gpm_ref[...])
        x2_ref[...] = x2
        h2_ref[...] = ((_rms(x2)[0] * gpf_ref[...]) * (1.0 + sc2) + sh2).astype(MXU)

    return _call(
        body, name="out_proj_fwd", grid=(s // tm,), sem=("parallel",),
        in_specs=[_rows(tm, hp), _rows(tm, GM_WIDTH), _full(wo.shape), _rows(tm, D_MODEL), _full(ada_raw.shape),
                  _full(ada_b.shape), _full(g_post_mix.shape), _full(g_pre_ffn.shape)],
        out_specs=[_rows(tm, D_MODEL)] * 3,
        out_shape=[_sds((s, D_MODEL), F32), _sds((s, D_MODEL), F32), _sds((s, D_MODEL), MXU)],
    )(o_pad, sgu, wo, x, ada_raw, ada_b, g_post_mix, g_pre_ffn)


def _conv(u, halo, cw_ref, cb_ref):
    ext = jnp.concatenate([halo, u], axis=0)
    m1, m2 = pltpu.roll(ext, 1, 0)[8:], pltpu.roll(ext, 2, 0)[8:]
    return cb_ref[0] + ((m2 * cw_ref[0, pl.ds(0, 1), :] + m1 * cw_ref[0, pl.ds(1, 1), :]) + u * cw_ref[0, pl.ds(2, 1), :])


ROW_SUB = 256


def _sub_blocks(tm):
    return [slice(r, r + ROW_SUB) for r in range(0, tm, ROW_SUB)]


def _ffn_up_fwd(h2, w_up, conv_w, conv_b, tm):
    s = h2.shape[0]
    half = N_DEV // 2

    def body(h_ref, wa_ref, wb_ref, cwa_ref, cwb_ref, cba_ref, cbb_ref,
             ua_ref, ub_ref, ya_ref, yb_ref, act_ref, halo_a, halo_b, wa_t, wb_t):
        i = pl.program_id(1)

        @pl.when(i == 0)
        def _():
            halo_a[...] = jnp.zeros(halo_a.shape, F32)
            halo_b[...] = jnp.zeros(halo_b.shape, F32)
            wa_t[...] = wa_ref[0].T
            wb_t[...] = wb_ref[0].T

        ha, hb = halo_a[...], halo_b[...]
        for rows in _sub_blocks(tm):
            h = h_ref[rows, :]
            ua = _dot(h, wa_t[...])
            ub = _dot(h, wb_t[...])
            ua_ref[0, rows, :] = ua
            ub_ref[0, rows, :] = ub
            ya = _conv(ua, ha, cwa_ref, cba_ref)
            yb = _conv(ub, hb, cwb_ref, cbb_ref)
            ya_ref[0, rows, :] = ya
            yb_ref[0, rows, :] = yb
            ha, hb = ua[ROW_SUB - 8:], ub[ROW_SUB - 8:]
            act_ref[0, rows, :] = ((ya * jax.nn.sigmoid(ya)) * yb).astype(MXU)
        halo_a[...] = ha
        halo_b[...] = hb

    def blk(shape, off):
        return pl.BlockSpec(shape, lambda j, i: (j + off, 0, 0))

    def tok(off=0):
        return pl.BlockSpec((1, tm, FF_BLK), lambda j, i: (j + off, i, 0))

    return _call(
        body, name="ffn_up_fwd", grid=(half, s // tm), sem=("parallel", "arbitrary"),
        in_specs=[pl.BlockSpec((tm, D_MODEL), lambda j, i: (i, 0)),
                  blk((1, FF_BLK, D_MODEL), 0), blk((1, FF_BLK, D_MODEL), half),
                  blk((1, 3, FF_BLK), 0), blk((1, 3, FF_BLK), half), blk((1, 1, FF_BLK), 0), blk((1, 1, FF_BLK), half)],
        out_specs=[tok()] * 5,
        out_shape=[_sds((half, s, FF_BLK), F32)] * 4 + [_sds((half, s, FF_BLK), MXU)],
        scratch=[pltpu.VMEM((8, FF_BLK), F32), pltpu.VMEM((8, FF_BLK), F32),
                 pltpu.VMEM((D_MODEL, FF_BLK), MXU), pltpu.VMEM((D_MODEL, FF_BLK), MXU)],
    )(h2, w_up, w_up, conv_w, conv_w, conv_b, conv_b)


def _ffn_down_fwd(act, wd, x2, target, ada_raw, ada_b, g_post_ffn, tm):
    s = x2.shape[0]
    half = N_DEV // 2

    def body(act_ref, wd_ref, x2_ref, t_ref, ar_ref, ab_ref, g_ref, dout_ref, df_ref, loss_ref, dgt_ref, dg_ref):
        i = pl.program_id(0)

        @pl.when(i == 0)
        def _():
            loss_ref[...] = jnp.zeros(loss_ref.shape, F32)
            dgt_ref[...] = jnp.zeros(dgt_ref.shape, F32)
            dg_ref[...] = jnp.zeros(dg_ref.shape, F32)

        gt2 = _row(ar_ref, 5) + _row(ab_ref, 5)
        g = g_ref[...]
        for rows in _sub_blocks(tm):
            f = _dot(act_ref[0, rows, :], wd_ref[0])
            for j in range(1, half):
                f = f + _dot(act_ref[j, rows, :], wd_ref[j])
            fhat, rf = _rms(f)
            fn = fhat * g
            err = (x2_ref[rows, :] + gt2 * fn) - t_ref[rows, :]
            loss_ref[...] += 0.5 * jnp.sum(jnp.mean(err * err, axis=-1, keepdims=True))
            d_out = err * (1.0 / D_MODEL)
            dout_ref[rows, :] = d_out
            dgt_ref[...] += jnp.sum(d_out * fn, axis=0, keepdims=True)
            d_fn = d_out * gt2
            dg_ref[...] += jnp.sum(d_fn * fhat, axis=0, keepdims=True)
            df_ref[rows, :] = _rms_bwd(d_fn * g, fhat, rf).astype(MXU)

    vec = pl.BlockSpec((1, D_MODEL), lambda i: (0, 0))
    return _call(
        body, name="ffn_down_fwd", grid=(s // tm,), sem=("arbitrary",),
        in_specs=[pl.BlockSpec((half, tm, FF_BLK), lambda i: (0, i, 0)), _full(wd.shape), _rows(tm, D_MODEL),
                  _rows(tm, D_MODEL), _full(ada_raw.shape), _full(ada_b.shape), _full(g_post_ffn.shape)],
        out_specs=[_rows(tm, D_MODEL), _rows(tm, D_MODEL), pl.BlockSpec((1, 128), lambda i: (0, 0)), vec, vec],
        out_shape=[_sds((s, D_MODEL), F32), _sds((s, D_MODEL), MXU), _sds((1, 128), F32),
                   _sds((1, D_MODEL), F32), _sds((1, D_MODEL), F32)],
    )(act, wd, x2, target, ada_raw, ada_b, g_post_ffn)


def _ffn_down_bwd(d_f, wd, up_a, up_b, y_a, y_b, conv_w, act, h2, tm):
    s = d_f.shape[0]
    half = N_DEV // 2
    nt = s // tm

    def body(df_ref, wd_ref, ua_ref, ub_ref, ya_ref, yb_ref, cwa_ref, cwb_ref, act_ref, h2_ref,
             dup_ref, dcw_ref, dcb_ref, pd_ref, pu_ref, next_a, next_b, acc_d, acc_a, acc_b, wd_t):
        i = pl.program_id(1)

        @pl.when(i == 0)
        def _():
            next_a[...] = jnp.zeros(next_a.shape, F32)
            next_b[...] = jnp.zeros(next_b.shape, F32)
            dcw_ref[...] = jnp.zeros(dcw_ref.shape, F32)
            dcb_ref[...] = jnp.zeros(dcb_ref.shape, F32)
            for acc in (acc_d, acc_a, acc_b):
                acc[...] = jnp.zeros(acc.shape, F32)
            wd_t[...] = wd_ref[0].T

        def conv_bwd(d_y, u, nxt, cw_ref, part, rows):
            ext = jnp.concatenate([d_y, nxt], axis=0)
            p1 = pltpu.roll(ext, ROW_SUB + 7, 0)[:ROW_SUB]
            p2 = pltpu.roll(ext, ROW_SUB + 6, 0)[:ROW_SUB]
            d_u = (d_y * cw_ref[0, pl.ds(2, 1), :] + p1 * cw_ref[0, pl.ds(1, 1), :]) + p2 * cw_ref[0, pl.ds(0, 1), :]
            dup_ref[0, part, rows, :] = d_u.astype(MXU)
            dcb_ref[0, part] += jnp.sum(d_y, axis=0, keepdims=True)
            dcw_ref[0, part, pl.ds(0, 1), :] += jnp.sum(p2 * u, axis=0, keepdims=True)
            dcw_ref[0, part, pl.ds(1, 1), :] += jnp.sum(p1 * u, axis=0, keepdims=True)
            dcw_ref[0, part, pl.ds(2, 1), :] += jnp.sum(d_y * u, axis=0, keepdims=True)
            return d_y[:8]

        nxa, nxb = next_a[...], next_b[...]
        for rows in reversed(_sub_blocks(tm)):
            d_act = _dot(df_ref[rows, :], wd_t[...])
            ya, yb = ya_ref[0, rows, :], yb_ref[0, rows, :]
            sig = jax.nn.sigmoid(ya)
            d_ya = d_act * yb * (sig * (1.0 + ya * (1.0 - sig)))
            d_yb = d_act * (ya * sig)
            nxa = conv_bwd(d_ya, ua_ref[0, rows, :], nxa, cwa_ref, 0, rows)
            nxb = conv_bwd(d_yb, ub_ref[0, rows, :], nxb, cwb_ref, 1, rows)
        next_a[...] = nxa
        next_b[...] = nxb
        acc_d[...] += _dot_tn(act_ref[0], df_ref[...])
        acc_a[...] += _dot_tn(dup_ref[0, 0], h2_ref[...])
        acc_b[...] += _dot_tn(dup_ref[0, 1], h2_ref[...])

        @pl.when(i == nt - 1)
        def _():
            pd_ref[0] = acc_d[...].astype(MXU)
            pu_ref[0, 0] = acc_a[...].astype(MXU)
            pu_ref[0, 1] = acc_b[...].astype(MXU)

    def rev(i):
        return nt - 1 - i

    def blk(shape, off):
        return pl.BlockSpec(shape, lambda j, i: (j + off, 0, 0))

    tok = pl.BlockSpec((1, tm, FF_BLK), lambda j, i: (j, rev(i), 0))
    acc3 = pl.BlockSpec((1, 2, 3, FF_BLK), lambda j, i: (j, 0, 0, 0))
    acc1 = pl.BlockSpec((1, 2, 1, FF_BLK), lambda j, i: (j, 0, 0, 0))
    return _call(
        body, name="ffn_down_bwd", grid=(half, nt), sem=("parallel", "arbitrary"),
        in_specs=[pl.BlockSpec((tm, D_MODEL), lambda j, i: (rev(i), 0)), blk((1, FF_BLK, D_MODEL), 0),
                  tok, tok, tok, tok, blk((1, 3, FF_BLK), 0), blk((1, 3, FF_BLK), half),
                  tok, pl.BlockSpec((tm, D_MODEL), lambda j, i: (rev(i), 0))],
        out_specs=[pl.BlockSpec((1, 2, tm, FF_BLK), lambda j, i: (j, 0, rev(i), 0)), acc3, acc1,
                   pl.BlockSpec((1, FF_BLK, D_MODEL), lambda j, i: (j, 0, 0)),
                   pl.BlockSpec((1, 2, FF_BLK, D_MODEL), lambda j, i: (j, 0, 0, 0))],
        out_shape=[_sds((half, 2, s, FF_BLK), MXU), _sds((half, 2, 3, FF_BLK), F32), _sds((half, 2, 1, FF_BLK), F32),
                   _sds((half, FF_BLK, D_MODEL), MXU), _sds((half, 2, FF_BLK, D_MODEL), MXU)],
        scratch=[pltpu.VMEM((8, FF_BLK), F32), pltpu.VMEM((8, FF_BLK), F32)]
        + [pltpu.VMEM((FF_BLK, D_MODEL), F32)] * 3 + [pltpu.VMEM((D_MODEL, FF_BLK), MXU)],
    )(d_f, wd, up_a, up_b, y_a, y_b, conv_w, conv_w, act, h2)


def _ffn_up_bwd(d_up, w_up, x2, m, d_out, ada_raw, ada_b, g_pre_ffn, g_post_mix, tm):
    s = x2.shape[0]
    half = N_DEV // 2

    def body(dup_ref, w_ref, x2_ref, m_ref, dout_ref, ar_ref, ab_ref, gpf_ref, gpm_ref,
             dx_ref, dm_ref, dsh_ref, dsc_ref, dgpf_ref, dgt1_ref, dgpm_ref):
        i = pl.program_id(0)

        @pl.when(i == 0)
        def _():
            for r in (dsh_ref, dsc_ref, dgpf_ref, dgt1_ref, dgpm_ref):
                r[...] = jnp.zeros(r.shape, F32)

        gt1 = _row(ar_ref, 2) + _row(ab_ref, 2)
        sc2 = _row(ar_ref, 4) + _row(ab_ref, 4)
        gpf, gpm = gpf_ref[...], gpm_ref[...]
        d_h2 = _dot(dup_ref[0, 0], w_ref[0])
        for j in range(1, half):
            d_h2 = d_h2 + _dot(dup_ref[j, 0], w_ref[j])
        for j in range(half):
            d_h2 = d_h2 + _dot(dup_ref[j, 1], w_ref[half + j])
        x2n, r2 = _rms(x2_ref[...])
        dsh_ref[...] += jnp.sum(d_h2, axis=0, keepdims=True)
        dsc_ref[...] += jnp.sum(d_h2 * (x2n * gpf), axis=0, keepdims=True)
        d_mod = d_h2 * (1.0 + sc2)
        dgpf_ref[...] += jnp.sum(d_mod * x2n, axis=0, keepdims=True)
        d_x2 = dout_ref[...] + _rms_bwd(d_mod * gpf, x2n, r2)
        dx_ref[...] = d_x2
        mhat, rm = _rms(m_ref[...])
        dgt1_ref[...] += jnp.sum(d_x2 * (mhat * gpm), axis=0, keepdims=True)
        d_mn = d_x2 * gt1
        dgpm_ref[...] += jnp.sum(d_mn * mhat, axis=0, keepdims=True)
        dm_ref[...] = _rms_bwd(d_mn * gpm, mhat, rm).astype(MXU)

    vec = pl.BlockSpec((1, D_MODEL), lambda i: (0, 0))
    tok = pl.BlockSpec((half, 2, tm, FF_BLK), lambda i: (0, 0, i, 0))
    return _call(
        body, name="ffn_up_bwd", grid=(s // tm,), sem=("arbitrary",),
        in_specs=[tok, _full(w_up.shape), _rows(tm, D_MODEL), _rows(tm, D_MODEL), _rows(tm, D_MODEL),
                  _full(ada_raw.shape), _full(ada_b.shape), _full(g_pre_ffn.shape), _full(g_post_mix.shape)],
        out_specs=[_rows(tm, D_MODEL), _rows(tm, D_MODEL), vec, vec, vec, vec, vec],
        out_shape=[_sds((s, D_MODEL), F32), _sds((s, D_MODEL), MXU)] + [_sds((1, D_MODEL), F32)] * 5,
    )(d_up, w_up, x2, m, d_out, ada_raw, ada_b, g_pre_ffn, g_post_mix)


def _out_proj_bwd(d_m, wo, o_pad, tm):
    s = d_m.shape[0]
    hp = HEADS * HEAD_PAD

    def body(dm_ref, wo_ref, o_ref, do_ref, dsgu_ref, delta_ref, wo_t):
        @pl.when(pl.program_id(0) == 0)
        def _():
            wo_t[...] = wo_ref[...].T

        d_cat = _dot(dm_ref[...], wo_t[...])
        d_o = d_cat[:, :hp]
        do_ref[...] = d_o.astype(MXU)
        dsgu_ref[...] = d_cat[:, hp:]
        prod = d_o * o_ref[...].astype(F32)
        for h in range(HEADS):
            delta_ref[h] = jnp.sum(prod[:, h * HEAD_PAD:(h + 1) * HEAD_PAD], axis=-1, keepdims=True)

    return _call(
        body, name="out_proj_bwd", grid=(s // tm,), sem=("arbitrary",),
        in_specs=[_rows(tm, D_MODEL), _full(wo.shape), _rows(tm, hp)],
        out_specs=[_rows(tm, hp), _rows(tm, GM_WIDTH), pl.BlockSpec((HEADS, tm, 1), lambda i: (0, i, 0))],
        out_shape=[_sds((s, hp), MXU), _sds((s, GM_WIDTH), F32), _sds((HEADS, s, 1), F32)],
        scratch=[pltpu.VMEM(wo.shape[::-1], MXU)],
    )(d_m, wo, o_pad)


def _attn_bwd(qp, kp, vp, d_o, lse, delta, tq, scattered, gathered):
    s = qp.shape[0]
    nq = s // tq
    hb = ATTN_HEADS_PER_STEP
    groups = HEADS // hb
    width = hb * HEAD_PAD
    ns, ng = len(scattered), len(gathered)
    nc = ns + ng
    slots = [slot for _, slot in scattered]

    def body(q_ref, k_ref, v_ref, do_ref, lse_ref, dl_ref, *rest):
        c_in, (dq_ref, dk_ref, dv_ref), c_out = rest[:nc], rest[nc:nc + 3], rest[nc + 3:2 * nc + 3]
        dk_sc, dv_sc = rest[2 * nc + 3:2 * nc + 5]
        sems = rest[2 * nc + 5:]
        s_start, s_finish = _scatter_steps(c_in[:ns], c_out[:ns], sems[:3], slots)
        g_start, g_forward, g_finish = _gather_steps(c_in[ns:], c_out[ns:], sems[3:])
        g, j = pl.program_id(0), pl.program_id(1)

        @pl.when((g == 0) & (j == 0))
        def _():
            s_start()
            g_start()

        pl.when((g == groups - 1) & (j == 0))(g_forward)

        @pl.when(j == 0)
        def _():
            dq_ref[...] = jnp.zeros(dq_ref.shape, F32)

        dk_sc[...] = jnp.zeros(dk_sc.shape, F32)
        dv_sc[...] = jnp.zeros(dv_sc.shape, F32)

        def tile(i, masked):
            rows = pl.ds(pl.multiple_of(i * tq, tq), tq)
            for hh in range(hb):
                lanes = slice(hh * HEAD_PAD, (hh + 1) * HEAD_PAD)
                q, do, k = q_ref[rows, lanes], do_ref[rows, lanes], k_ref[:, lanes]
                sc = _dot_nt(q, k)
                if masked:
                    sc = jnp.where(_chunk_mask(tq, tq, 0), sc, NEG_BIG)
                p = jnp.exp2(sc * SCALE_LOG2E - lse_ref[hh, rows, :])
                dv_sc[hh] += _dot_tn(p.astype(MXU), do)
                dp = _dot_nt(do, v_ref[:, lanes])
                ds = (p * (dp - dl_ref[hh, rows, :])).astype(MXU)
                dk_sc[hh] += _dot_tn(ds, q)
                dq_ref[rows, lanes] += _dot(ds, k) * ATTN_SCALE

        def off_diagonal_pair(p, carry):
            tile(j + 1 + 2 * p, False)
            tile(j + 2 + 2 * p, False)
            return carry

        below = nq - 1 - j
        tile(j, True)
        lax.fori_loop(0, below // 2, off_diagonal_pair, 0)

        @pl.when(below % 2 == 1)
        def _():
            tile(nq - 1, False)
        for hh in range(hb):
            lanes = slice(hh * HEAD_PAD, (hh + 1) * HEAD_PAD)
            dk_ref[:, lanes] = dk_sc[hh] * ATTN_SCALE
            dv_ref[:, lanes] = dv_sc[hh]
        @pl.when((g == groups - 1) & (j == nq - 1))
        def _():
            g_finish()
            s_finish()

    seq_spec = pl.BlockSpec((s, width), lambda g, j: (0, g))
    kv_spec = pl.BlockSpec((tq, width), lambda g, j: (j, g))
    col_spec = pl.BlockSpec((hb, s, 1), lambda g, j: (g, 0, 0))
    any_spec = pl.BlockSpec(memory_space=pl.ANY)
    outs = _call(
        body, name="attn_bwd", grid=(groups, nq), sem=("arbitrary", "arbitrary"),
        in_specs=[seq_spec, kv_spec, kv_spec, seq_spec, col_spec, col_spec] + [any_spec] * nc,
        out_specs=[seq_spec, kv_spec, kv_spec] + [any_spec] * nc,
        out_shape=[_sds(qp.shape, F32), _sds(qp.shape, F32), _sds(qp.shape, F32)]
        + [_scatter_out_shape(a, slot) for a, slot in scattered]
        + [_sds((N_DEV,) + a.shape, a.dtype) for a in gathered],
        scratch=[pltpu.VMEM((hb, tq, HEAD_PAD), F32), pltpu.VMEM((hb, tq, HEAD_PAD), F32)]
        + _comm_sems(ns) + _comm_sems(ng),
    )(qp, kp, vp, d_o, lse, delta, *[a for a, _ in scattered], *gathered)
    return outs[0], outs[1], outs[2], outs[3:3 + ns], outs[3 + ns:]


def _gmlp_bwd(z, d_sgu, ln_g, ln_b, w_sp, bias_exp, tm):
    s = z.shape[0]
    nblk = tm // GM_CHUNK

    def body(zu_ref, zv_ref, dsgu_ref, lg_ref, lb_ref, w_ref, be_ref,
             dguv_ref, dws_ref, dbs_ref, dlg_ref, dlb_ref, dbe_sc, dvln_sc, dlg_sc, dlb_sc):
        i = pl.program_id(0)

        @pl.when(i == 0)
        def _():
            for r in (dws_ref, dlg_sc, dlb_sc, dbe_sc):
                r[...] = jnp.zeros(r.shape, F32)

        seg = _seg_matrix()
        mask = _spatial_mask()
        wm = [(w_ref[h] * mask).astype(MXU) for h in range(HEADS)]
        zu, zv = zu_ref[...], zv_ref[...]
        gu = _gelu(zu)
        _, vhat, rstd = _gm_norm(zv, seg)
        lg = lg_ref[...]
        vln = (vhat * lg + lb_ref[...]).astype(MXU)
        d_sgu = dsgu_ref[...]
        for n in range(nblk):
            rows = slice(n * GM_CHUNK, (n + 1) * GM_CHUNK)
            vb = vln[rows]
            mixed = _gm_mix(wm, vb, GM_CHUNK) + be_ref[...]
            d_mixed = d_sgu[rows] * gu[rows]
            dguv_ref[rows, pl.ds(0, GM_WIDTH)] = ((d_sgu[rows] * mixed) * _gelu_grad(zu[rows])).astype(MXU)
            dbe_sc[...] += d_mixed
            dmb = d_mixed.astype(MXU)
            for p, (lanes, first) in enumerate(_gm_pairs(GM_CHUNK)):
                dm_pair, v_pair = dmb[:, lanes], vb[:, lanes]
                zero = jnp.zeros_like(dm_pair)
                dws_ref[2 * p] += _dot_nt(jnp.where(first, dm_pair, zero), v_pair)
                dws_ref[2 * p + 1] += _dot_nt(jnp.where(first, zero, dm_pair), v_pair)
            dvln_sc[rows, :] = _gm_mix(wm, dmb, GM_CHUNK, transposed=True)
        d_vln = dvln_sc[...]
        dlg_sc[...] += jnp.sum(d_vln * vhat, axis=0, keepdims=True)
        dlb_sc[...] += jnp.sum(d_vln, axis=0, keepdims=True)
        d_vhat = d_vln * lg
        d_gv = rstd * ((d_vhat - _split_dot(d_vhat, seg)) - vhat * _split_dot(d_vhat * vhat, seg))
        dguv_ref[:, pl.ds(GM_WIDTH, GM_WIDTH)] = (d_gv * _gelu_grad(zv)).astype(MXU)

        @pl.when(i == pl.num_programs(0) - 1)
        def _():
            for h in range(HEADS):
                dws_ref[h] = dws_ref[h] * mask
            hrow = lax.broadcasted_iota(jnp.int32, (HEADS, GM_WIDTH), 0)
            hlane = lax.broadcasted_iota(jnp.int32, (HEADS, GM_WIDTH), 1) >> 6
            ind = jnp.where(hrow == hlane, 1.0, 0.0).astype(MXU)
            acc = dbe_sc[...]
            hi = acc.astype(MXU)
            lo = (acc - hi.astype(F32)).astype(MXU)
            dbs_ref[...] = _dot_nt(ind, hi) + _dot_nt(ind, lo)
            pick = (lax.broadcasted_iota(jnp.int32, (GM_WIDTH, GM_DIM), 0) & (GM_DIM - 1)
                    == lax.broadcasted_iota(jnp.int32, (GM_WIDTH, GM_DIM), 1))
            pick = jnp.where(pick, 1.0, 0.0).astype(MXU)
            for src, dst in ((dlg_sc, dlg_ref), (dlb_sc, dlb_ref)):
                spread = jnp.where(hrow == hlane, jnp.broadcast_to(src[...], (HEADS, GM_WIDTH)), 0.0)
                dst[...] = _split_dot3(spread, pick)

    return _call(
        body, name="gmlp_bwd", grid=(s // tm,), sem=("arbitrary",),
        in_specs=[_rows(tm, GM_WIDTH, 1), _rows(tm, GM_WIDTH, 2), _rows(tm, GM_WIDTH), _full(ln_g.shape),
                  _full(ln_b.shape), _full(w_sp.shape), _full(bias_exp.shape)],
        out_specs=[_rows(tm, 2 * GM_WIDTH), _full(w_sp.shape), _full((HEADS, GM_CHUNK)), _full((HEADS, GM_DIM)),
                   _full((HEADS, GM_DIM))],
        out_shape=[_sds((s, 2 * GM_WIDTH), MXU), _sds(w_sp.shape, F32), _sds((HEADS, GM_CHUNK), F32),
                   _sds((HEADS, GM_DIM), F32), _sds((HEADS, GM_DIM), F32)],
        scratch=[pltpu.VMEM((GM_CHUNK, GM_WIDTH), F32), pltpu.VMEM((tm, GM_WIDTH), F32),
                 pltpu.VMEM((1, GM_WIDTH), F32), pltpu.VMEM((1, GM_WIDTH), F32)],
    )(z, z, d_sgu, ln_g, ln_b, w_sp, bias_exp)


def _mix_in_bwd(dq, dk, dv, z, d_guv, x, d_x_part, ada_raw, ada_b, g_pre, g_q, g_kv, w1t, wqt, wkv,
                cos_t, sin_t, tm):
    s = x.shape[0]
    hp = HEADS * HEAD_PAD
    za = Q_LORA + KV_LORA + HEAD_PAD

    def body(dq_ref, dk_ref, dv_ref, z_ref, dguv_ref, x_ref, dxp_ref, ar_ref, ab_ref, g_ref, gq_ref, gkv_ref,
             w1_ref, wq_ref, wkv_ref, cos_ref, sin_ref,
             gx_ref, dza_ref, dqp_ref, dkvp_ref, dsh_ref, dsc_ref, dg_ref, dgq_ref, dgkv_ref):
        i = pl.program_id(0)

        @pl.when(i == 0)
        def _():
            for r in (dsh_ref, dsc_ref, dg_ref, dgq_ref, dgkv_ref):
                r[...] = jnp.zeros(r.shape, F32)

        cos, sin = cos_ref[...], sin_ref[...]
        d_krot = jnp.zeros((tm, HEAD_PAD), F32)
        for h in range(HEADS):
            blk = slice(h * HEAD_PAD, (h + 1) * HEAD_PAD)
            dqp_ref[:, blk] = _rope_transposed(dq_ref[:, blk], cos, sin).astype(MXU)
            dk_h = dk_ref[:, blk]
            d_krot = d_krot + dk_h
            dkvp_ref[:, blk] = dk_h.astype(MXU)
        dkvp_ref[:, pl.ds(hp, hp)] = dv_ref[...].astype(MXU)
        lane = lax.broadcasted_iota(jnp.int32, (tm, HEAD_PAD), 1)
        d_kr = jnp.where((lane >= NOPE) & (lane < NOPE + ROPE), _rope_transposed(d_krot, cos, sin), 0.0)
        d_cqn = _dot(dqp_ref[...], wq_ref[...])
        d_ckvn = _dot_nt(dkvp_ref[...], wkv_ref[...])
        zt = z_ref[...]
        gq, gkv = gq_ref[...], gkv_ref[...]
        cq_hat, rq = _rms(zt[:, :Q_LORA])
        ckv_hat, rkv = _rms(zt[:, Q_LORA:Q_LORA + KV_LORA])
        dgq_ref[...] += jnp.sum(d_cqn * cq_hat, axis=0, keepdims=True)
        dgkv_ref[...] += jnp.sum(d_ckvn * ckv_hat, axis=0, keepdims=True)
        d_cq = _rms_bwd(d_cqn * gq, cq_hat, rq)
        d_ckv = _rms_bwd(d_ckvn * gkv, ckv_hat, rkv)
        d_za = jnp.concatenate([d_cq, d_ckv, d_kr], axis=1).astype(MXU)
        dza_ref[...] = d_za
        d_h1 = _dot(d_za, w1_ref[pl.ds(0, za), :]) + _dot(dguv_ref[...], w1_ref[pl.ds(za, 2 * GM_WIDTH), :])
        sc1 = _row(ar_ref, 1) + _row(ab_ref, 1)
        g = g_ref[...]
        xn, r1 = _rms(x_ref[...])
        dsh_ref[...] += jnp.sum(d_h1, axis=0, keepdims=True)
        dsc_ref[...] += jnp.sum(d_h1 * (xn * g), axis=0, keepdims=True)
        d_mod = d_h1 * (1.0 + sc1)
        dg_ref[...] += jnp.sum(d_mod * xn, axis=0, keepdims=True)
        gx_ref[...] = dxp_ref[...] + _rms_bwd(d_mod * g, xn, r1)

    vec = pl.BlockSpec((1, D_MODEL), lambda i: (0, 0))
    return _call(
        body, name="mix_in_bwd", grid=(s // tm,), sem=("arbitrary",),
        in_specs=[_rows(tm, hp), _rows(tm, hp), _rows(tm, hp), _rows(tm, za), _rows(tm, 2 * GM_WIDTH),
                  _rows(tm, D_MODEL), _rows(tm, D_MODEL), _full(ada_raw.shape), _full(ada_b.shape), _full(g_pre.shape),
                  _full(g_q.shape), _full(g_kv.shape), _full(w1t.shape), _full(wqt.shape),
                  _full(wkv.shape), _rows(tm, HEAD_PAD), _rows(tm, HEAD_PAD)],
        out_specs=[_rows(tm, D_MODEL), _rows(tm, za), _rows(tm, hp), _rows(tm, 2 * hp), vec, vec, vec,
                   _full(g_q.shape), _full(g_kv.shape)],
        out_shape=[_sds((s, D_MODEL), F32), _sds((s, za), MXU), _sds((s, hp), MXU), _sds((s, 2 * hp), MXU),
                   _sds((1, D_MODEL), F32), _sds((1, D_MODEL), F32), _sds((1, D_MODEL), F32),
                   _sds(g_q.shape, F32), _sds(g_kv.shape, F32)],
    )(dq, dk, dv, z, d_guv, x, d_x_part, ada_raw, ada_b, g_pre, g_q, g_kv, w1t, wqt, wkv, cos_t, sin_t)


def _tn_matmuls(arrays, pairs, name, ts):
    s = arrays[0].shape[0]
    steps = s // ts
    n_in, n_out = len(arrays), len(pairs)
    shapes = [(arrays[ia].shape[1], arrays[ib].shape[1]) for ia, ib in pairs]

    def body(*refs):
        ins, outs, accs = refs[:n_in], refs[n_in:n_in + n_out], refs[n_in + n_out:]
        k = pl.program_id(0)

        @pl.when(k == 0)
        def _():
            for acc in accs:
                acc[...] = jnp.zeros(acc.shape, F32)

        for (ia, ib), acc in zip(pairs, accs):
            acc[...] += _dot_tn(ins[ia][...], ins[ib][...])

        @pl.when(k == steps - 1)
        def _():
            for out, acc in zip(outs, accs):
                out[...] = acc[...].astype(MXU)

    return _call(
        body, name=name, grid=(steps,), sem=("arbitrary",),
        in_specs=[_rows(ts, a.shape[1]) for a in arrays],
        out_specs=[_full(shape) for shape in shapes],
        out_shape=[_sds(shape, MXU) for shape in shapes],
        scratch=[pltpu.VMEM(shape, F32) for shape in shapes],
    )(*arrays)


def _adamw(w, g, m, v):
    m2 = ADAM_B1 * m + (1.0 - ADAM_B1) * g
    v2 = ADAM_B2 * v + (1.0 - ADAM_B2) * (g * g)
    m_hat = m2 / (1.0 - ADAM_B1 ** ADAM_STEP)
    v_hat = v2 / (1.0 - ADAM_B2 ** ADAM_STEP)
    delta = -ADAM_LR * (m_hat / (jnp.sqrt(v_hat) + ADAM_EPS) + ADAM_WD * w)
    return delta, m2, v2


def _adam_reduce(recv, w, m, v, name):
    r, c = w.shape
    tr = r if r <= 512 else max(t for t in range(16, 513, 16) if r % t == 0)

    def body(p_ref, w_ref, m_ref, v_ref, g_ref, d_ref, mo_ref, vo_ref):
        g = p_ref[0].astype(F32)
        for j in range(1, N_DEV):
            g = g + p_ref[j].astype(F32)
        g_ref[...] = g
        d_ref[...], mo_ref[...], vo_ref[...] = _adamw(w_ref[...], g, m_ref[...], v_ref[...])

    blk = pl.BlockSpec((tr, c), lambda i: (i, 0))
    return _call(
        body, name=name, grid=(r // tr,), sem=("parallel",),
        in_specs=[pl.BlockSpec((N_DEV, tr, c), lambda i: (0, i, 0)), blk, blk, blk],
        out_specs=[blk] * 4, out_shape=[_sds((r, c), F32)] * 4,
    )(recv, w, m, v)


def _adam_w_ada(c_act_t, d_ada_cols, w, m, v):
    r, c = w.shape
    tr = 256

    def body(ct_ref, da_ref, w_ref, m_ref, v_ref, g_ref, d_ref, mo_ref, vo_ref):
        g = _dot(ct_ref[...], da_ref[...])
        g_ref[...] = g
        d_ref[...], mo_ref[...], vo_ref[...] = _adamw(w_ref[...], g, m_ref[...], v_ref[...])

    blk = pl.BlockSpec((tr, c), lambda i: (i, 0))
    return _call(
        body, name="adam_w_ada", grid=(r // tr,), sem=("parallel",),
        in_specs=[pl.BlockSpec((tr, c_act_t.shape[1]), lambda i: (i, 0)), _full(d_ada_cols.shape), blk, blk, blk],
        out_specs=[blk] * 4, out_shape=[_sds((r, c), F32)] * 4,
    )(c_act_t, d_ada_cols, w, m, v)


VEC_ROWS = D_MODEL // 128
PK_ADA = 0
PK_GAIN = PK_ADA + 6 * VEC_ROWS
PK_GQ = PK_GAIN + 4 * VEC_ROWS
PK_GKV = PK_GQ + Q_LORA // 128
PK_LOSS = PK_GKV + KV_LORA // 128
PK_LNG = 88
PK_LNB = PK_LNG + HEADS
PK_BS = PK_LNB + HEADS
PK_CB = PK_BS + HEADS
CB_ROWS = 6
PK_WS = PK_CB + N_DEV * CB_ROWS
PK_ROWS = PK_WS + HEADS * GM_CHUNK
assert PK_LOSS < PK_LNG and PK_ROWS % 8 == 0
LATE_GAIN = 2 * VEC_ROWS
LATE_GQ = 3 * VEC_ROWS
LATE_GKV = LATE_GQ + Q_LORA // 128
LATE_ROWS = 32


def _cb_chunks():
    return [(k, k * 128, min(128, FF_BLK - k * 128)) for k in range(CB_ROWS)]


def _put_rows(out_ref, row0, ref, width):
    for k in range(width // 128):
        out_ref[pl.ds(row0 + k, 1), :] = ref[:, pl.ds(k * 128, 128)]


def _pack_small(ada_rows, gains, loss_part, d_ln_g, d_ln_b, d_bs, d_cb, d_ws):
    half = N_DEV // 2

    def body(*refs):
        vec_refs = refs[:7]
        loss_ref, lng_ref, lnb_ref, bs_ref, cb_ref, ws_ref, out_ref = refs[7:]
        out_ref[pl.ds(0, PK_WS), :] = jnp.zeros((PK_WS, 128), F32)
        for n, ref in enumerate(vec_refs[:4]):
            _put_rows(out_ref, PK_ADA + (2 + n) * VEC_ROWS, ref, D_MODEL)
        for n, ref in enumerate(vec_refs[4:]):
            _put_rows(out_ref, PK_GAIN + (1 + n) * VEC_ROWS, ref, D_MODEL)
        _put_rows(out_ref, PK_LOSS, loss_ref, 128)
        out_ref[pl.ds(PK_LNG, HEADS), pl.ds(0, GM_DIM)] = lng_ref[...]
        out_ref[pl.ds(PK_LNB, HEADS), pl.ds(0, GM_DIM)] = lnb_ref[...]
        out_ref[pl.ds(PK_BS, HEADS), :] = bs_ref[...]
        for j in range(N_DEV):
            for k, lane, width in _cb_chunks():
                out_ref[pl.ds(PK_CB + j * CB_ROWS + k, 1), pl.ds(0, width)] = cb_ref[j % half, j // half, :, pl.ds(lane, width)]
        for h in range(HEADS):
            out_ref[pl.ds(PK_WS + h * GM_CHUNK, GM_CHUNK), :] = ws_ref[h]

    ins = list(ada_rows) + list(gains) + [loss_part, d_ln_g, d_ln_b, d_bs, d_cb, d_ws]
    return _call(body, name="pack_small", grid=(1,), in_specs=[_full(a.shape) for a in ins],
                 out_specs=_full((PK_ROWS, 128)), out_shape=_sds((PK_ROWS, 128), F32))(*ins)


def _pack_late(d_sh1, d_sc1, d_g_pre_mix, d_g_q, d_g_kv):
    def body(sh_ref, sc_ref, g_ref, gq_ref, gkv_ref, out_ref):
        out_ref[...] = jnp.zeros((LATE_ROWS, 128), F32)
        _put_rows(out_ref, 0, sh_ref, D_MODEL)
        _put_rows(out_ref, VEC_ROWS, sc_ref, D_MODEL)
        _put_rows(out_ref, LATE_GAIN, g_ref, D_MODEL)
        _put_rows(out_ref, LATE_GQ, gq_ref, Q_LORA)
        _put_rows(out_ref, LATE_GKV, gkv_ref, KV_LORA)

    ins = [d_sh1, d_sc1, d_g_pre_mix, d_g_q, d_g_kv]
    return _call(body, name="pack_late", grid=(1,), in_specs=[_full(a.shape) for a in ins],
                 out_specs=_full((LATE_ROWS, 128)), out_shape=_sds((LATE_ROWS, 128), F32))(*ins)


def _adam_small(gathered, late, params):
    n_par = len(params)

    def body(p_ref, late_ref, *refs):
        ins = [refs[3 * n:3 * n + 3] for n in range(n_par)]
        outs = [refs[3 * n_par + 4 * n:3 * n_par + 4 * n + 4] for n in range(n_par)]
        loss_ref, dada_ref = refs[7 * n_par:]

        def total(rows, lanes=slice(None), src=p_ref):
            g = src[0, rows, lanes]
            for j in range(1, N_DEV):
                g = g + src[j, rows, lanes]
            return g

        def apply(n, g, idx):
            w_ref, m_ref, v_ref = ins[n]
            d, m2, v2 = _adamw(w_ref[idx], g, m_ref[idx], v_ref[idx])
            for ref, val in zip(outs[n], (g, d, m2, v2)):
                ref[idx] = val

        def vector(n, src, row0, width, lane0=0):
            for k in range(width // 128):
                apply(n, total(pl.ds(row0 + k, 1), src=src), (slice(None), pl.ds(lane0 + k * 128, 128)))

        vector(0, late_ref, 0, 2 * D_MODEL)
        vector(0, p_ref, PK_ADA + 2 * VEC_ROWS, 4 * D_MODEL, lane0=2 * D_MODEL)
        vector(1, late_ref, LATE_GAIN, D_MODEL)
        for n in range(1, 4):
            vector(1 + n, p_ref, PK_GAIN + n * VEC_ROWS, D_MODEL)
        vector(5, late_ref, LATE_GQ, Q_LORA)
        vector(6, late_ref, LATE_GKV, KV_LORA)
        apply(7, total(pl.ds(PK_LNG, HEADS), pl.ds(0, GM_DIM)), (0,))
        apply(8, total(pl.ds(PK_LNB, HEADS), pl.ds(0, GM_DIM)), (0,))
        for h in range(HEADS):
            apply(9, total(pl.ds(PK_WS + h * GM_CHUNK, GM_CHUNK)), (0, h))
        apply(10, total(pl.ds(PK_BS, HEADS)), (0,))
        for j in range(N_DEV):
            for k, lane, width in _cb_chunks():
                apply(11, total(pl.ds(PK_CB + j * CB_ROWS + k, 1), pl.ds(0, width)), (pl.ds(j, 1), pl.ds(lane, width)))
        loss_ref[...] = total(pl.ds(PK_LOSS, 1))
        dada_ref[:, pl.ds(0, 2 * VEC_ROWS), :] = late_ref[:, pl.ds(0, 2 * VEC_ROWS), :]
        dada_ref[:, pl.ds(2 * VEC_ROWS, 4 * VEC_ROWS), :] = p_ref[:, pl.ds(PK_ADA + 2 * VEC_ROWS, 4 * VEC_ROWS), :]

    flat = [a for triple in params for a in triple]
    out_shape = [_sds(w.shape, F32) for w, _, _ in params for _ in range(4)]
    out_shape += [_sds((1, 128), F32), _sds((N_DEV, 6 * VEC_ROWS, 128), F32)]
    outs = _call(body, name="adam_small", grid=(1,),
                 in_specs=[_full(gathered.shape), _full(late.shape)] + [_full(a.shape) for a in flat],
                 out_specs=[_full(o.shape) for o in out_shape], out_shape=out_shape)(gathered, late, *flat)
    return [tuple(outs[4 * n:4 * n + 4]) for n in range(n_par)], outs[-2], outs[-1]


def _rope_tables(s):
    pos = jnp.arange(s, dtype=F32)
    inv = ROPE_THETA ** (-jnp.arange(0, ROPE, 2, dtype=F32) / ROPE)
    lane_inv = jnp.concatenate([jnp.zeros((NOPE,), F32), inv, inv, jnp.zeros((HEAD_PAD - NOPE - ROPE,), F32)])
    ang = pos[:, None] * lane_inv[None, :]
    return jnp.cos(ang), jnp.sin(ang)


def kernel(x, c, w_ada, b_ada, g_pre_mix, g_post_mix, w_in, g_q, w_uq, g_kv, w_ukv, gm_ln_g, gm_ln_b, w_spatial, b_spatial, w_out, g_pre_ffn, g_post_ffn, w_up, conv_w, conv_b, w_down, loss_target, m_w_ada, m_b_ada, m_g_pre_mix, m_g_post_mix, m_w_in, m_g_q, m_w_uq, m_g_kv, m_w_ukv, m_gm_ln_g, m_gm_ln_b, m_w_spatial, m_b_spatial, m_w_out, m_g_pre_ffn, m_g_post_ffn, m_w_up, m_conv_w, m_conv_b, m_w_down, v_w_ada, v_b_ada, v_g_pre_mix, v_g_post_mix, v_w_in, v_g_q, v_w_uq, v_g_kv, v_w_ukv, v_gm_ln_g, v_gm_ln_b, v_w_spatial, v_b_spatial, v_w_out, v_g_pre_ffn, v_g_post_ffn, v_w_up, v_conv_w, v_conv_b, v_w_down):
    s = x.shape[1]
    tm = min(512, s)
    tf = min(2 * ROW_SUB, s)
    tq = min(512, s)
    ts = min(2048, s)
    hp = HEADS * HEAD_PAD
    half = N_DEV // 2
    my_slot = 4 * lax.axis_index("x") + 2 * lax.axis_index("y") + lax.axis_index("c")
    x2d, target = x[0], loss_target[0]

    def t_(a):
        return jnp.swapaxes(a[0], 0, 1)

    w_in_t, m_in_t, v_in_t = t_(w_in), t_(m_w_in), t_(v_w_in)
    w_uq_t, m_uq_t, v_uq_t = t_(w_uq), t_(m_w_uq), t_(v_w_uq)
    w_up_t, m_up_t, v_up_t = t_(w_up), t_(m_w_up), t_(v_w_up)
    (g_c, g_in_t, g_uq_t, g_ukv, g_cw), _ = _exchange(
        [c, w_in_t.astype(MXU), w_uq_t.astype(MXU), w_ukv[0].astype(MXU), conv_w[0]], [], "gather_mixer_weights")

    w_in_f = g_in_t.reshape(-1, D_MODEL)
    o1, o2, o3 = Q_LORA, Q_LORA + KV_LORA, Q_LORA + KV_LORA + ROPE
    w1t = jnp.concatenate([w_in_f[:o2], jnp.zeros((NOPE, D_MODEL), MXU), w_in_f[o2:o3],
                           jnp.zeros((HEAD_PAD - NOPE - ROPE, D_MODEL), MXU), w_in_f[o3:]], axis=0)
    wqt = jnp.pad(g_uq_t, ((0, 0), (0, HEAD_PAD - NOPE - ROPE), (0, 0))).reshape(hp, Q_LORA)
    w_ukv_f = jnp.transpose(g_ukv, (1, 0, 2)).reshape(KV_LORA, HEADS, 2 * NOPE)
    pad_head = ((0, 0), (0, 0), (0, HEAD_PAD - NOPE))
    wkv = jnp.concatenate([jnp.pad(w_ukv_f[:, :, :NOPE], pad_head).reshape(KV_LORA, hp),
                           jnp.pad(w_ukv_f[:, :, NOPE:], pad_head).reshape(KV_LORA, hp)], axis=1)
    cb8 = conv_b.reshape(N_DEV, 1, FF_BLK)
    bias_exp = jnp.repeat(b_spatial[0].T, GM_DIM, axis=1)
    ln_g, ln_b = gm_ln_g.reshape(1, GM_WIDTH), gm_ln_b.reshape(1, GM_WIDTH)
    w_sp = w_spatial[0]
    cos_t, sin_t = _rope_tables(s)

    ada_part, c_act = _ada_fwd(g_c.reshape(N_DEV, D_MODEL), w_ada[0])
    _, (ada_recv,) = _exchange([], [(ada_part.reshape(N_DEV, 1, -1), _plain_slot)], "ada_rows")
    ada_raw = ada_recv.reshape(6, D_MODEL)
    ada_b = b_ada.reshape(6, D_MODEL)

    h1, z, qp, kp, vp, cqn, ckvn = _mix_in_fwd(x2d, ada_raw, ada_b, g_pre_mix, w1t, g_q, g_kv, wqt, wkv, cos_t, sin_t, tm)
    sgu = _gmlp_fwd(z, ln_g, ln_b, w_sp, bias_exp, tm)
    o_pad, lse, (g_out, g_up, g_down) = _attn_fwd(
        qp, kp, vp, tq, [w_out[0].astype(MXU), w_up_t.astype(MXU), w_down[0].astype(MXU)])
    w_out_f = g_out.reshape(2 * GM_WIDTH, D_MODEL)
    wo_attn = jnp.pad(w_out_f[:GM_WIDTH].reshape(HEADS, NOPE, D_MODEL), ((0, 0), (0, HEAD_PAD - NOPE), (0, 0)))
    wo = jnp.concatenate([wo_attn.reshape(hp, D_MODEL), w_out_f[GM_WIDTH:]], axis=0)
    wd = g_down.reshape(half, FF_BLK, D_MODEL)
    m_mix, x2, h2 = _out_proj_fwd(o_pad, sgu, wo, x2d, ada_raw, ada_b, g_post_mix, g_pre_ffn, tm)
    up_a, up_b, y_a, y_b, act = _ffn_up_fwd(h2, g_up, g_cw, cb8, tf)
    d_out, d_f, loss_part, d_gt2, d_g_post_ffn = _ffn_down_fwd(act, wd, x2, target, ada_raw, ada_b, g_post_ffn, tf)

    d_up, d_cw, d_cb, p_down, p_up = _ffn_down_bwd(d_f, wd, up_a, up_b, y_a, y_b, g_cw, act, h2, tf)
    p_down = p_down.reshape(N_DEV, -1, D_MODEL)
    d_x2, d_m, d_sh2, d_sc2, d_g_pre_ffn, d_gt1, d_g_post_mix = _ffn_up_bwd(
        d_up, g_up, x2, m_mix, d_out, ada_raw, ada_b, g_pre_ffn, g_post_mix, tm)
    dwo_attn, dwo_sgu = _tn_matmuls([o_pad, sgu, d_m], [(0, 2), (1, 2)], "dw_out", ts)
    dwo_attn = dwo_attn.reshape(HEADS, HEAD_PAD, D_MODEL)[:, :NOPE]
    p_out = jnp.concatenate([dwo_attn.reshape(GM_WIDTH, D_MODEL), dwo_sgu], axis=0).reshape(N_DEV, -1, D_MODEL)
    d_o, d_sgu, delta = _out_proj_bwd(d_m, wo, o_pad, tm)
    d_guv, d_ws, d_bs, d_ln_g, d_ln_b = _gmlp_bwd(z, d_sgu, ln_g, ln_b, w_sp, bias_exp, tm)
    packed = _pack_small([d_gt1, d_sh2, d_sc2, d_gt2], [d_g_post_mix, d_g_pre_ffn, d_g_post_ffn], loss_part,
                         d_ln_g, d_ln_b, d_bs, d_cb, d_ws)

    def ffn_slot(j):
        return (j % half, j // half)

    dq, dk, dv, (r_out, r_up, r_down, r_cw), (g_small,) = _attn_bwd(
        qp, kp, vp, d_o, lse, delta, tq,
        [(p_out, _plain_slot), (p_up, ffn_slot), (p_down, _plain_slot), (d_cw, ffn_slot)], [packed])
    grad_x, d_za, d_qp, d_kvp, d_sh1, d_sc1, d_g_pre_mix, d_g_q, d_g_kv = _mix_in_bwd(
        dq, dk, dv, z, d_guv, x2d, d_x2, ada_raw, ada_b, g_pre_mix, g_q, g_kv, w1t, wqt, wkv, cos_t, sin_t,
        min(256, s))
    dw1a, dw1b, dwq, dwkv = _tn_matmuls([d_za, d_guv, h1, d_qp, cqn, ckvn, d_kvp],
                                        [(0, 2), (1, 2), (3, 4), (5, 6)], "dw_mixer", ts // 2)
    d_w_in_t = jnp.concatenate([dw1a[:o2], dw1a[o2 + NOPE:o2 + NOPE + ROPE], dw1b], axis=0)
    p_in = d_w_in_t.reshape(N_DEV, -1, D_MODEL)
    p_uq = dwq.reshape(HEADS, HEAD_PAD, Q_LORA)[:, :NOPE + ROPE]
    dwk = dwkv[:, :hp].reshape(KV_LORA, HEADS, HEAD_PAD)[:, :, :NOPE]
    dwv = dwkv[:, hp:].reshape(KV_LORA, HEADS, HEAD_PAD)[:, :, :NOPE]
    p_ukv = jnp.transpose(jnp.concatenate([dwk, dwv], axis=2), (1, 0, 2))

    (g_late,), (r_in, r_uq, r_ukv) = _exchange(
        [_pack_late(d_sh1, d_sc1, d_g_pre_mix, d_g_q, d_g_kv)],
        [(p_in, _plain_slot), (p_uq, _plain_slot), (p_ukv, _plain_slot)], "final_exchange")
    small_params = [(b_ada, m_b_ada, v_b_ada), (g_pre_mix, m_g_pre_mix, v_g_pre_mix),
                    (g_post_mix, m_g_post_mix, v_g_post_mix), (g_pre_ffn, m_g_pre_ffn, v_g_pre_ffn),
                    (g_post_ffn, m_g_post_ffn, v_g_post_ffn), (g_q, m_g_q, v_g_q), (g_kv, m_g_kv, v_g_kv),
                    (gm_ln_g, m_gm_ln_g, v_gm_ln_g), (gm_ln_b, m_gm_ln_b, v_gm_ln_b),
                    (w_spatial, m_w_spatial, v_w_spatial), (b_spatial, m_b_spatial, v_b_spatial),
                    tuple(a.reshape(N_DEV, FF_BLK) for a in (conv_b, m_conv_b, v_conv_b))]
    small_out, loss_row, d_ada_all = _adam_small(g_small, g_late, small_params)
    small_out[11] = tuple(o.reshape(conv_b.shape) for o in small_out[11])
    loss = loss_row[0, 0]

    def big(recv, w, m, v, name):
        g, d, m2, v2 = _adam_reduce(recv, w[0], m[0], v[0], name)
        return g[None], d[None], m2[None], v2[None]

    def big_t(recv, w_t, m_t, v_t, name):
        return tuple(jnp.swapaxes(o, 0, 1)[None] for o in _adam_reduce(recv, w_t, m_t, v_t, name))

    a_in = big_t(r_in, w_in_t, m_in_t, v_in_t, "adam_w_in")
    a_uq = big_t(r_uq, w_uq_t, m_uq_t, v_uq_t, "adam_w_uq")
    a_ukv = big(r_ukv, w_ukv, m_w_ukv, v_w_ukv, "adam_w_ukv")
    a_out = big(r_out, w_out, m_w_out, v_w_out, "adam_w_out")
    a_up = big_t(r_up, w_up_t, m_up_t, v_up_t, "adam_w_up")
    a_down = big(r_down, w_down, m_w_down, v_w_down, "adam_w_down")
    ada_cols = w_ada.shape[2]
    d_ada_cols = lax.dynamic_slice(d_ada_all.reshape(N_DEV, 6 * D_MODEL), (0, my_slot * ada_cols), (N_DEV, ada_cols))
    pad_seq = 128 - N_DEV
    a_ada = tuple(t[None] for t in _adam_w_ada(jnp.pad(c_act.T, ((0, 0), (0, pad_seq))).astype(MXU),
                                               jnp.pad(d_ada_cols, ((0, pad_seq), (0, 0))).astype(MXU),
                                               w_ada[0], m_w_ada[0], v_w_ada[0]))
    a_cw = big(r_cw, conv_w, m_conv_w, v_conv_w, "adam_conv_w")

    def small(k):
        return small_out[k]

    per_weight = [a_ada, small(0), small(1), small(2), a_in, small(5), a_uq, small(6), a_ukv, small(7), small(8),
                  small(9), small(10), a_out, small(3), small(4), a_up, a_cw, small(11), a_down]
    outs = [loss, grad_x[None]]
    for k in range(4):
        outs += [t[k] for t in per_weight]
    return tuple(outs)
```

```python
import functools

import jax
import jax.numpy as jnp
from jax import lax
from jax.experimental import pallas as pl
from jax.experimental.pallas import tpu as pltpu

F32 = jnp.float32
MXU = jnp.bfloat16

N_DEV = 8
D_MODEL = 1024
HEADS = 8
HEAD_PAD = 128
NOPE = 64
ROPE = 32
Q_LORA = 256
KV_LORA = 128
GM_WIDTH = 512
GM_DIM = 64
GM_CHUNK = 128
CHUNK_SHIFT = 6
ROPE_THETA = 10000.0
ATTN_SCALE = (NOPE + ROPE) ** -0.5
LOG2E = 1.4426950408889634
SCALE_LOG2E = ATTN_SCALE * LOG2E
Z_COLS = 1536
FF_BLK = 704
EPS = 1e-6
ADAM_LR = 0.001
ADAM_B1 = 0.9
ADAM_B2 = 0.999
ADAM_EPS = 1e-08
ADAM_WD = 0.01
ADAM_STEP = 10
VMEM_LIMIT = 56 * 1024 * 1024
MESH = pl.DeviceIdType.MESH


def _dot(a, b):
    return jnp.dot(a, b, preferred_element_type=F32)


def _dot_nt(a, b):
    return lax.dot_general(a, b, (((1,), (1,)), ((), ())), preferred_element_type=F32)


def _dot_tn(a, b):
    return lax.dot_general(a, b, (((0,), (0,)), ((), ())), preferred_element_type=F32)


def _call(body, *, name, grid, in_specs, out_specs, out_shape, scratch=(), sem=None):
    params = pltpu.CompilerParams(dimension_semantics=sem, vmem_limit_bytes=VMEM_LIMIT)
    return pl.pallas_call(body, name=name, grid=grid, in_specs=in_specs, out_specs=out_specs,
                          out_shape=out_shape, scratch_shapes=list(scratch), compiler_params=params)


def _full(shape):
    n = len(shape)
    return pl.BlockSpec(shape, lambda *_: (0,) * n)


def _rows(tm, cols, col_block=0):
    return pl.BlockSpec((tm, cols), lambda i: (i, col_block))


def _sds(shape, dtype):
    return jax.ShapeDtypeStruct(shape, dtype)


def _row(ref, k):
    return ref[pl.ds(k, 1), :]


def _rms(x):
    r = lax.rsqrt(jnp.mean(x * x, axis=-1, keepdims=True) + EPS)
    return x * r, r


def _rms_bwd(d_hat, hat, r):
    return r * (d_hat - hat * jnp.mean(d_hat * hat, axis=-1, keepdims=True))


def _rope_partner(t):
    lane = lax.broadcasted_iota(jnp.int32, t.shape, 1)
    swapped = jnp.where(lane < NOPE + ROPE // 2, -pltpu.roll(t, HEAD_PAD - ROPE // 2, 1), pltpu.roll(t, ROPE // 2, 1))
    return jnp.where((lane >= NOPE) & (lane < NOPE + ROPE), swapped, 0.0)


def _rope(t, cos, sin):
    return t * cos + _rope_partner(t) * sin


def _rope_transposed(g, cos, sin):
    return g * cos - _rope_partner(g * sin)


def _gelu(x):
    return x * (0.5 * (1.0 + jnp.tanh(0.7978845608028654 * (x + 0.044715 * (x * x * x)))))


def _gelu_grad(x):
    t = jnp.tanh(0.7978845608028654 * (x + 0.044715 * (x * x * x)))
    return 0.5 * (1.0 + t) + 0.5 * x * (1.0 - t * t) * (0.7978845608028654 * (1.0 + 3.0 * 0.044715 * (x * x)))


def _split_dot(x, mat):
    hi = x.astype(MXU)
    lo = (x - hi.astype(F32)).astype(MXU)
    return _dot(hi, mat) + _dot(lo, mat)


def _split_dot3(x, mat):
    hi = x.astype(MXU)
    r1 = x - hi.astype(F32)
    mid = r1.astype(MXU)
    lo = (r1 - mid.astype(F32)).astype(MXU)
    return (_dot(hi, mat) + _dot(mid, mat)) + _dot(lo, mat)


def _seg_matrix():
    r = lax.broadcasted_iota(jnp.int32, (GM_WIDTH, GM_WIDTH), 0) >> 6
    c = lax.broadcasted_iota(jnp.int32, (GM_WIDTH, GM_WIDTH), 1) >> 6
    return jnp.where(r == c, 1.0 / GM_DIM, 0.0).astype(MXU)


def _spatial_mask():
    i = lax.broadcasted_iota(jnp.int32, (GM_CHUNK, GM_CHUNK), 0) >> CHUNK_SHIFT
    j = lax.broadcasted_iota(jnp.int32, (GM_CHUNK, GM_CHUNK), 1) >> CHUNK_SHIFT
    return (j <= i).astype(F32)


def _my_place():
    return lax.axis_index("x"), lax.axis_index("y"), lax.axis_index("c")


def _flat(p):
    return 4 * p[0] + 2 * p[1] + p[2]


def _comm_sems(n):
    return [pltpu.SemaphoreType.DMA((7 * n,)), pltpu.SemaphoreType.DMA((7 * n,)), pltpu.SemaphoreType.DMA((n,))]


def _gather_steps(ins, outs, sems):
    send_sems, recv_sems, local_sems = sems
    n = len(ins)
    x, y, c = _my_place()
    me, sibling = (x, y, c), (x, y, 1 - c)
    chips = [(1 - x, y), (x, 1 - y), (1 - x, 1 - y)]

    def copy(a, k, block, to, src=None):
        slot = outs[a].at[_flat(block)]
        return pltpu.make_async_remote_copy(
            src_ref=slot if src is None else src, dst_ref=slot,
            send_sem=send_sems.at[7 * a + k], recv_sem=recv_sems.at[7 * a + k],
            device_id=to, device_id_type=MESH)

    def mine():
        return [pltpu.make_async_copy(ins[a], outs[a].at[_flat(me)], local_sems.at[a]) for a in range(n)]

    def first():
        cps = []
        for a in range(n):
            cps.append(copy(a, 0, me, sibling, src=ins[a]))
            cps += [copy(a, 1 + j, me, (*chip, c), src=ins[a]) for j, chip in enumerate(chips)]
        return cps

    def passed():
        return [copy(a, 4 + j, (*chip, c), sibling) for a in range(n) for j, chip in enumerate(chips)]

    def start():
        for cp in mine() + first():
            cp.start()

    def forward():
        for a in range(n):
            for j, chip in enumerate(chips):
                copy(a, 1 + j, (*chip, c), me).wait_recv()
                copy(a, 4 + j, (*chip, c), sibling).start()

    def finish():
        for a in range(n):
            copy(a, 0, sibling, me).wait_recv()
            for j, chip in enumerate(chips):
                copy(a, 4 + j, (*chip, 1 - c), me).wait_recv()
        for cp in first() + passed():
            cp.wait_send()
        for cp in mine():
            cp.wait()

    return start, forward, finish


def _scatter_steps(ins, outs, sems, slots):
    send_sems, recv_sems, local_sems = sems
    n = len(ins)
    flips = [(fx, fy, fc) for fx in (0, 1) for fy in (0, 1) for fc in (0, 1)][1:]
    me = _my_place()

    def peer(f):
        return tuple(1 - v if b else v for v, b in zip(me, f))

    def copy(a, k, arriving=False):
        p = peer(flips[k])
        return pltpu.make_async_remote_copy(
            src_ref=ins[a].at[slots[a](_flat(p))], dst_ref=outs[a].at[_flat(p if arriving else me)],
            send_sem=send_sems.at[7 * a + k], recv_sem=recv_sems.at[7 * a + k],
            device_id=p, device_id_type=MESH)

    def mine():
        return [pltpu.make_async_copy(ins[a].at[slots[a](_flat(me))], outs[a].at[_flat(me)], local_sems.at[a])
                for a in range(n)]

    def start():
        for cp in mine() + [copy(a, k) for a in range(n) for k in range(7)]:
            cp.start()

    def finish():
        for a in range(n):
            for k in range(7):
                copy(a, k, arriving=True).wait_recv()
        for a in range(n):
            for k in range(7):
                copy(a, k).wait_send()
        for cp in mine():
            cp.wait()

    return start, finish


def _plain_slot(j):
    return (j,)


def _scatter_out_shape(arr, slot):
    return _sds((N_DEV,) + arr.shape[len(slot(0)):], arr.dtype)


def _exchange(gathered, scattered, name):
    ng, ns = len(gathered), len(scattered)
    slots = [slot for _, slot in scattered]

    def body(*refs):
        g_in, s_in = refs[:ng], refs[ng:ng + ns]
        g_out, s_out = refs[ng + ns:2 * ng + ns], refs[2 * ng + ns:2 * (ng + ns)]
        sems = refs[2 * (ng + ns):]
        g_start, g_forward, g_finish = _gather_steps(g_in, g_out, sems[:3])
        s_start, s_finish = _scatter_steps(s_in, s_out, sems[3:], slots)
        g_start()
        s_start()
        g_forward()
        g_finish()
        s_finish()

    any_spec = pl.BlockSpec(memory_space=pl.ANY)
    outs = pl.pallas_call(
        body, name=name,
        in_specs=[any_spec] * (ng + ns), out_specs=[any_spec] * (ng + ns),
        out_shape=[_sds((N_DEV,) + a.shape, a.dtype) for a in gathered]
        + [_scatter_out_shape(a, slot) for a, slot in scattered],
        scratch_shapes=_comm_sems(max(ng, 1)) + _comm_sems(max(ns, 1)),
    )(*gathered, *[a for a, _ in scattered])
    return outs[:ng], outs[ng:]


def _ada_fwd(c_all, w_ada):
    def body(c_ref, w_ref, part_ref, act_ref):
        cv = c_ref[...]
        act = cv * jax.nn.sigmoid(cv)
        act_ref[...] = act
        part_ref[...] = _dot(act.astype(MXU), w_ref[...].astype(MXU))

    cols = w_ada.shape[1]
    return _call(body, name="ada_fwd", grid=(1,),
                 in_specs=[_full(c_all.shape), _full(w_ada.shape)],
                 out_specs=[_full((N_DEV, cols)), _full(c_all.shape)],
                 out_shape=[_sds((N_DEV, cols), F32), _sds(c_all.shape, F32)])(c_all, w_ada)


def _mix_in_fwd(x, ada_raw, ada_b, g_pre, w1, g_q, g_kv, wq, wkv, cos_t, sin_t, tm):
    s = x.shape[0]

    def body(x_ref, ar_ref, ab_ref, g_ref, w1_ref, gq_ref, gkv_ref, wq_ref, wkv_ref, cos_ref, sin_ref,
             h1_ref, z_ref, qp_ref, kp_ref, vp_ref, cqn_ref, ckvn_ref, w1_n, wq_n):
        @pl.when(pl.program_id(0) == 0)
        def _():
            w1_n[...] = w1_ref[...].T
            wq_n[...] = wq_ref[...].T

        sh = _row(ar_ref, 0) + _row(ab_ref, 0)
        sc = _row(ar_ref, 1) + _row(ab_ref, 1)
        xn, _ = _rms(x_ref[...])
        hb = ((xn * g_ref[...]) * (1.0 + sc) + sh).astype(MXU)
        h1_ref[...] = hb
        z = _dot(hb, w1_n[...])
        z_ref[...] = z
        cos, sin = cos_ref[...], sin_ref[...]
        cqn = (_rms(z[:, :Q_LORA])[0] * gq_ref[...]).astype(MXU)
        ckvn = (_rms(z[:, Q_LORA:Q_LORA + KV_LORA])[0] * gkv_ref[...]).astype(MXU)
        cqn_ref[...] = cqn
        ckvn_ref[...] = ckvn
        q = _dot(cqn, wq_n[...])
        kv = _dot(ckvn, wkv_ref[...])
        k_rope = _rope(z[:, Q_LORA + KV_LORA:Q_LORA + KV_LORA + HEAD_PAD], cos, sin)
        for h in range(HEADS):
            blk = slice(h * HEAD_PAD, (h + 1) * HEAD_PAD)
            qp_ref[:, blk] = _rope(q[:, blk], cos, sin).astype(MXU)
            kp_ref[:, blk] = (kv[:, blk] + k_rope).astype(MXU)
        v_lane = lax.broadcasted_iota(jnp.int32, (tm, HEADS * HEAD_PAD), 1) & (HEAD_PAD - 1)
        vp_ref[...] = jnp.where(v_lane == NOPE, 1.0, kv[:, HEADS * HEAD_PAD:]).astype(MXU)

    hp = HEADS * HEAD_PAD
    return _call(
        body, name="mix_in_fwd", grid=(s // tm,), sem=("arbitrary",),
        in_specs=[_rows(tm, D_MODEL), _full(ada_raw.shape), _full(ada_b.shape), _full(g_pre.shape), _full(w1.shape),
                  _full(g_q.shape), _full(g_kv.shape), _full(wq.shape), _full(wkv.shape),
                  _rows(tm, HEAD_PAD), _rows(tm, HEAD_PAD)],
        out_specs=[_rows(tm, D_MODEL), _rows(tm, Z_COLS), _rows(tm, hp), _rows(tm, hp), _rows(tm, hp),
                   _rows(tm, Q_LORA), _rows(tm, KV_LORA)],
        out_shape=[_sds((s, D_MODEL), MXU), _sds((s, Z_COLS), F32), _sds((s, hp), MXU), _sds((s, hp), MXU),
                   _sds((s, hp), MXU), _sds((s, Q_LORA), MXU), _sds((s, KV_LORA), MXU)],
        scratch=[pltpu.VMEM(w1.shape[::-1], MXU), pltpu.VMEM(wq.shape[::-1], MXU)],
    )(x, ada_raw, ada_b, g_pre, w1, g_q, g_kv, wq, wkv, cos_t, sin_t)


def _gm_norm(zv, seg):
    gv = _gelu(zv)
    cen = gv - _split_dot(gv, seg)
    rstd = lax.rsqrt(_split_dot(cen * cen, seg) + EPS)
    return gv, cen * rstd, rstd


def _gm_pairs(rows):
    first = lax.broadcasted_iota(jnp.int32, (rows, 2 * GM_DIM), 1) < GM_DIM
    return [(slice(p * 2 * GM_DIM, (p + 1) * 2 * GM_DIM), first) for p in range(HEADS // 2)]


def _gm_mix(wm, vb, rows, transposed=False):
    dot = _dot_tn if transposed else _dot
    return jnp.concatenate([jnp.where(first, dot(wm[2 * p], vb[:, lanes]), dot(wm[2 * p + 1], vb[:, lanes]))
                            for p, (lanes, first) in enumerate(_gm_pairs(rows))], axis=1)


def _gmlp_fwd(z, ln_g, ln_b, w_sp, bias_exp, tm):
    s = z.shape[0]
    nblk = tm // GM_CHUNK

    def body(zu_ref, zv_ref, lg_ref, lb_ref, w_ref, be_ref, sgu_ref):
        seg = _seg_matrix()
        mask = _spatial_mask()
        wm = [(w_ref[h] * mask).astype(MXU) for h in range(HEADS)]
        gu = _gelu(zu_ref[...])
        _, vhat, _ = _gm_norm(zv_ref[...], seg)
        vln = (vhat * lg_ref[...] + lb_ref[...]).astype(MXU)
        for n in range(nblk):
            rows = slice(n * GM_CHUNK, (n + 1) * GM_CHUNK)
            mixed = _gm_mix(wm, vln[rows], GM_CHUNK) + be_ref[...]
            sgu_ref[rows, :] = (gu[rows] * mixed).astype(MXU)

    return _call(
        body, name="gmlp_fwd", grid=(s // tm,), sem=("parallel",),
        in_specs=[_rows(tm, GM_WIDTH, 1), _rows(tm, GM_WIDTH, 2), _full(ln_g.shape), _full(ln_b.shape),
                  _full(w_sp.shape), _full(bias_exp.shape)],
        out_specs=_rows(tm, GM_WIDTH), out_shape=_sds((s, GM_WIDTH), MXU),
    )(z, z, ln_g, ln_b, w_sp, bias_exp)


def _chunk_mask(n_q, n_k, q_off):
    qc = (q_off + lax.broadcasted_iota(jnp.int32, (n_q, n_k), 0)) >> CHUNK_SHIFT
    kc = lax.broadcasted_iota(jnp.int32, (n_q, n_k), 1) >> CHUNK_SHIFT
    return kc <= qc


NEG_BIG = -1e30
ATTN_HEADS_PER_STEP = 2


def _attn_fwd(qp, kp, vp, tq, gathered):
    s = qp.shape[0]
    nq = s // tq
    hb = ATTN_HEADS_PER_STEP
    groups = HEADS // hb
    width = hb * HEAD_PAD
    ng = len(gathered)

    def body(q_ref, k_ref, v_ref, *rest):
        g_in, (o_ref, lse_ref), g_out = rest[:ng], rest[ng:ng + 2], rest[ng + 2:2 * ng + 2]
        m_sc, acc_sc = rest[2 * ng + 2:2 * ng + 4]
        g_start, g_forward, g_finish = _gather_steps(g_in, g_out, rest[2 * ng + 4:])
        g, i = pl.program_id(0), pl.program_id(1)
        pl.when((g == 0) & (i == 0))(g_start)
        pl.when((g == groups - 1) & (i == 0))(g_forward)
        m_sc[...] = jnp.full(m_sc.shape, NEG_BIG, F32)
        acc_sc[...] = jnp.zeros(acc_sc.shape, F32)

        def tile(j, masked, n_tiles=1):
            n_k = n_tiles * tq
            rows = pl.ds(pl.multiple_of(j * tq, tq), n_k)
            for hh in range(hb):
                lanes = slice(hh * HEAD_PAD, (hh + 1) * HEAD_PAD)
                sc = _dot_nt(q_ref[:, lanes], k_ref[rows, lanes])
                if masked:
                    sc = jnp.where(_chunk_mask(tq, tq, 0), sc, NEG_BIG)
                blocks = [sc[:, b * 128:(b + 1) * 128] for b in range(n_k // 128)]
                m_prev = m_sc[hh]
                m_tile = jnp.max(functools.reduce(jnp.maximum, blocks), axis=-1, keepdims=True)
                m_new = jnp.maximum(m_prev, m_tile)
                alpha = jnp.exp2((m_prev - m_new) * SCALE_LOG2E)
                p = jnp.concatenate([jnp.exp2((b - m_new) * SCALE_LOG2E) for b in blocks], axis=1).astype(MXU)
                acc_sc[hh] = alpha * acc_sc[hh] + _dot(p, v_ref[rows, lanes])
                m_sc[hh] = m_new

        def off_diagonal_pair(p, carry):
            tile(2 * p, False, n_tiles=2)
            return carry

        lax.fori_loop(0, i // 2, off_diagonal_pair, 0)

        @pl.when(i % 2 == 1)
        def _():
            tile(i - 1, False)

        tile(i, True)
        for hh in range(hb):
            lanes = slice(hh * HEAD_PAD, (hh + 1) * HEAD_PAD)
            acc = acc_sc[hh]
            denom = acc[:, NOPE:NOPE + 1]
            o_ref[:, lanes] = (acc / denom).astype(MXU)
            lse_ref[hh] = m_sc[hh][:, :1] * SCALE_LOG2E + jnp.log(denom) * LOG2E
        pl.when((g == groups - 1) & (i == nq - 1))(g_finish)

    q_spec = pl.BlockSpec((tq, width), lambda g, i: (i, g))
    kv_spec = pl.BlockSpec((s, width), lambda g, i: (0, g))
    any_spec = pl.BlockSpec(memory_space=pl.ANY)
    outs = _call(
        body, name="attn_fwd", grid=(groups, nq), sem=("arbitrary", "arbitrary"),
        in_specs=[q_spec, kv_spec, kv_spec] + [any_spec] * ng,
        out_specs=[q_spec, pl.BlockSpec((hb, tq, 1), lambda g, i: (g, i, 0))] + [any_spec] * ng,
        out_shape=[_sds(qp.shape, MXU), _sds((HEADS, s, 1), F32)]
        + [_sds((N_DEV,) + a.shape, a.dtype) for a in gathered],
        scratch=[pltpu.VMEM((hb, tq, HEAD_PAD), F32), pltpu.VMEM((hb, tq, HEAD_PAD), F32)] + _comm_sems(ng),
    )(qp, kp, vp, *gathered)
    return outs[0], outs[1], outs[2:]


def _out_proj_fwd(o_pad, sgu, wo, x, ada_raw, ada_b, g_post_mix, g_pre_ffn, tm):
    s = x.shape[0]
    hp = HEADS * HEAD_PAD

    def body(o_ref, sgu_ref, wo_ref, x_ref, ar_ref, ab_ref, gpm_ref, gpf_ref, m_ref, x2_ref, h2_ref):
        gt1 = _row(ar_ref, 2) + _row(ab_ref, 2)
        sh2 = _row(ar_ref, 3) + _row(ab_ref, 3)
        sc2 = _row(ar_ref, 4) + _row(ab_ref, 4)
        m = _dot(o_ref[...], wo_ref[pl.ds(0, hp), :]) + _dot(sgu_ref[...], wo_ref[pl.ds(hp, GM_WIDTH), :])
        m_ref[...] = m
        x2 = x_ref[...] + gt1 * (_rms(m)[0] * gpm_ref[...])
        x2_ref[...] = x2
        h2_ref[...] = ((_rms(x2)[0] * gpf_ref[...]) * (1.0 + sc2) + sh2).astype(MXU)

    return _call(
        body, name="out_proj_fwd", grid=(s // tm,), sem=("parallel",),
        in_specs=[_rows(tm, hp), _rows(tm, GM_WIDTH), _full(wo.shape), _rows(tm, D_MODEL), _full(ada_raw.shape),
                  _full(ada_b.shape), _full(g_post_mix.shape), _full(g_pre_ffn.shape)],
        out_specs=[_rows(tm, D_MODEL)] * 3,
        out_shape=[_sds((s, D_MODEL), F32), _sds((s, D_MODEL), F32), _sds((s, D_MODEL), MXU)],
    )(o_pad, sgu, wo, x, ada_raw, ada_b, g_post_mix, g_pre_ffn)


def _conv(u, halo, cw_ref, cb_ref):
    ext = jnp.concatenate([halo, u], axis=0)
    m1, m2 = pltpu.roll(ext, 1, 0)[8:], pltpu.roll(ext, 2, 0)[8:]
    return cb_ref[0] + ((m2 * cw_ref[0, pl.ds(0, 1), :] + m1 * cw_ref[0, pl.ds(1, 1), :]) + u * cw_ref[0, pl.ds(2, 1), :])


ROW_SUB = 256


def _sub_blocks(tm):
    return [slice(r, r + ROW_SUB) for r in range(0, tm, ROW_SUB)]


def _ffn_up_fwd(h2, w_up, conv_w, conv_b, tm):
    s = h2.shape[0]
    half = N_DEV // 2

    def body(h_ref, wa_ref, wb_ref, cwa_ref, cwb_ref, cba_ref, cbb_ref,
             ua_ref, ub_ref, ya_ref, yb_ref, act_ref, halo_a, halo_b, wa_t, wb_t):
        i = pl.program_id(1)

        @pl.when(i == 0)
        def _():
            halo_a[...] = jnp.zeros(halo_a.shape, F32)
            halo_b[...] = jnp.zeros(halo_b.shape, F32)
            wa_t[...] = wa_ref[0].T
            wb_t[...] = wb_ref[0].T

        ha, hb = halo_a[...], halo_b[...]
        for rows in _sub_blocks(tm):
            h = h_ref[rows, :]
            ua = _dot(h, wa_t[...])
            ub = _dot(h, wb_t[...])
            ua_ref[0, rows, :] = ua
            ub_ref[0, rows, :] = ub
            ya = _conv(ua, ha, cwa_ref, cba_ref)
            yb = _conv(ub, hb, cwb_ref, cbb_ref)
            ya_ref[0, rows, :] = ya
            yb_ref[0, rows, :] = yb
            ha, hb = ua[ROW_SUB - 8:], ub[ROW_SUB - 8:]
            act_ref[0, rows, :] = ((ya * jax.nn.sigmoid(ya)) * yb).astype(MXU)
        halo_a[...] = ha
        halo_b[...] = hb

    def blk(shape, off):
        return pl.BlockSpec(shape, lambda j, i: (j + off, 0, 0))

    def tok(off=0):
        return pl.BlockSpec((1, tm, FF_BLK), lambda j, i: (j + off, i, 0))

    return _call(
        body, name="ffn_up_fwd", grid=(half, s // tm), sem=("parallel", "arbitrary"),
        in_specs=[pl.BlockSpec((tm, D_MODEL), lambda j, i: (i, 0)),
                  blk((1, FF_BLK, D_MODEL), 0), blk((1, FF_BLK, D_MODEL), half),
                  blk((1, 3, FF_BLK), 0), blk((1, 3, FF_BLK), half), blk((1, 1, FF_BLK), 0), blk((1, 1, FF_BLK), half)],
        out_specs=[tok()] * 5,
        out_shape=[_sds((half, s, FF_BLK), F32)] * 4 + [_sds((half, s, FF_BLK), MXU)],
        scratch=[pltpu.VMEM((8, FF_BLK), F32), pltpu.VMEM((8, FF_BLK), F32),
                 pltpu.VMEM((D_MODEL, FF_BLK), MXU), pltpu.VMEM((D_MODEL, FF_BLK), MXU)],
    )(h2, w_up, w_up, conv_w, conv_w, conv_b, conv_b)


def _ffn_down_fwd(act, wd, x2, target, ada_raw, ada_b, g_post_ffn, tm):
    s = x2.shape[0]
    half = N_DEV // 2

    def body(act_ref, wd_ref, x2_ref, t_ref, ar_ref, ab_ref, g_ref, dout_ref, df_ref, loss_ref, dgt_ref, dg_ref):
        i = pl.program_id(0)

        @pl.when(i == 0)
        def _():
            loss_ref[...] = jnp.zeros(loss_ref.shape, F32)
            dgt_ref[...] = jnp.zeros(dgt_ref.shape, F32)
            dg_ref[...] = jnp.zeros(dg_ref.shape, F32)

        gt2 = _row(ar_ref, 5) + _row(ab_ref, 5)
        g = g_ref[...]
        for rows in _sub_blocks(tm):
            f = _dot(act_ref[0, rows, :], wd_ref[0])
            for j in range(1, half):
                f = f + _dot(act_ref[j, rows, :], wd_ref[j])
            fhat, rf = _rms(f)
            fn = fhat * g
            err = (x2_ref[rows, :] + gt2 * fn) - t_ref[rows, :]
            loss_ref[...] += 0.5 * jnp.sum(jnp.mean(err * err, axis=-1, keepdims=True))
            d_out = err * (1.0 / D_MODEL)
            dout_ref[rows, :] = d_out
            dgt_ref[...] += jnp.sum(d_out * fn, axis=0, keepdims=True)
            d_fn = d_out * gt2
            dg_ref[...] += jnp.sum(d_fn * fhat, axis=0, keepdims=True)
            df_ref[rows, :] = _rms_bwd(d_fn * g, fhat, rf).astype(MXU)

    vec = pl.BlockSpec((1, D_MODEL), lambda i: (0, 0))
    return _call(
        body, name="ffn_down_fwd", grid=(s // tm,), sem=("arbitrary",),
        in_specs=[pl.BlockSpec((half, tm, FF_BLK), lambda i: (0, i, 0)), _full(wd.shape), _rows(tm, D_MODEL),
                  _rows(tm, D_MODEL), _full(ada_raw.shape), _full(ada_b.shape), _full(g_post_ffn.shape)],
        out_specs=[_rows(tm, D_MODEL), _rows(tm, D_MODEL), pl.BlockSpec((1, 128), lambda i: (0, 0)), vec, vec],
        out_shape=[_sds((s, D_MODEL), F32), _sds((s, D_MODEL), MXU), _sds((1, 128), F32),
                   _sds((1, D_MODEL), F32), _sds((1, D_MODEL), F32)],
    )(act, wd, x2, target, ada_raw, ada_b, g_post_ffn)


def _ffn_down_bwd(d_f, wd, up_a, up_b, y_a, y_b, conv_w, act, h2, tm):
    s = d_f.shape[0]
    half = N_DEV // 2
    nt = s // tm

    def body(df_ref, wd_ref, ua_ref, ub_ref, ya_ref, yb_ref, cwa_ref, cwb_ref, act_ref, h2_ref,
             dup_ref, dcw_ref, dcb_ref, pd_ref, pu_ref, next_a, next_b, acc_d, acc_a, acc_b, wd_t):
        i = pl.program_id(1)

        @pl.when(i == 0)
        def _():
            next_a[...] = jnp.zeros(next_a.shape, F32)
            next_b[...] = jnp.zeros(next_b.shape, F32)
            dcw_ref[...] = jnp.zeros(dcw_ref.shape, F32)
            dcb_ref[...] = jnp.zeros(dcb_ref.shape, F32)
            for acc in (acc_d, acc_a, acc_b):
                acc[...] = jnp.zeros(acc.shape, F32)
            wd_t[...] = wd_ref[0].T

        def conv_bwd(d_y, u, nxt, cw_ref, part, rows):
            ext = jnp.concatenate([d_y, nxt], axis=0)
            p1 = pltpu.roll(ext, ROW_SUB + 7, 0)[:ROW_SUB]
            p2 = pltpu.roll(ext, ROW_SUB + 6, 0)[:ROW_SUB]
            d_u = (d_y * cw_ref[0, pl.ds(2, 1), :] + p1 * cw_ref[0, pl.ds(1, 1), :]) + p2 * cw_ref[0, pl.ds(0, 1), :]
            dup_ref[0, part, rows, :] = d_u.astype(MXU)
            dcb_ref[0, part] += jnp.sum(d_y, axis=0, keepdims=True)
            dcw_ref[0, part, pl.ds(0, 1), :] += jnp.sum(p2 * u, axis=0, keepdims=True)
            dcw_ref[0, part, pl.ds(1, 1), :] += jnp.sum(p1 * u, axis=0, keepdims=True)
            dcw_ref[0, part, pl.ds(2, 1), :] += jnp.sum(d_y * u, axis=0, keepdims=True)
            return d_y[:8]

        nxa, nxb = next_a[...], next_b[...]
        for rows in reversed(_sub_blocks(tm)):
            d_act = _dot(df_ref[rows, :], wd_t[...])
            ya, yb = ya_ref[0, rows, :], yb_ref[0, rows, :]
            sig = jax.nn.sigmoid(ya)
            d_ya = d_act * yb * (sig * (1.0 + ya * (1.0 - sig)))
            d_yb = d_act * (ya * sig)
            nxa = conv_bwd(d_ya, ua_ref[0, rows, :], nxa, cwa_ref, 0, rows)
            nxb = conv_bwd(d_yb, ub_ref[0, rows, :], nxb, cwb_ref, 1, rows)
        next_a[...] = nxa
        next_b[...] = nxb
        acc_d[...] += _dot_tn(act_ref[0], df_ref[...])
        acc_a[...] += _dot_tn(dup_ref[0, 0], h2_ref[...])
        acc_b[...] += _dot_tn(dup_ref[0, 1], h2_ref[...])

        @pl.when(i == nt - 1)
        def _():
            pd_ref[0] = acc_d[...].astype(MXU)
            pu_ref[0, 0] = acc_a[...].astype(MXU)
            pu_ref[0, 1] = acc_b[...].astype(MXU)

    def rev(i):
        return nt - 1 - i

    def blk(shape, off):
        return pl.BlockSpec(shape, lambda j, i: (j + off, 0, 0))

    tok = pl.BlockSpec((1, tm, FF_BLK), lambda j, i: (j, rev(i), 0))
    acc3 = pl.BlockSpec((1, 2, 3, FF_BLK), lambda j, i: (j, 0, 0, 0))
    acc1 = pl.BlockSpec((1, 2, 1, FF_BLK), lambda j, i: (j, 0, 0, 0))
    return _call(
        body, name="ffn_down_bwd", grid=(half, nt), sem=("parallel", "arbitrary"),
        in_specs=[pl.BlockSpec((tm, D_MODEL), lambda j, i: (rev(i), 0)), blk((1, FF_BLK, D_MODEL), 0),
                  tok, tok, tok, tok, blk((1, 3, FF_BLK), 0), blk((1, 3, FF_BLK), half),
                  tok, pl.BlockSpec((tm, D_MODEL), lambda j, i: (rev(i), 0))],
        out_specs=[pl.BlockSpec((1, 2, tm, FF_BLK), lambda j, i: (j, 0, rev(i), 0)), acc3, acc1,
                   pl.BlockSpec((1, FF_BLK, D_MODEL), lambda j, i: (j, 0, 0)),
                   pl.BlockSpec((1, 2, FF_BLK, D_MODEL), lambda j, i: (j, 0, 0, 0))],
        out_shape=[_sds((half, 2, s, FF_BLK), MXU), _sds((half, 2, 3, FF_BLK), F32), _sds((half, 2, 1, FF_BLK), F32),
                   _sds((half, FF_BLK, D_MODEL), MXU), _sds((half, 2, FF_BLK, D_MODEL), MXU)],
        scratch=[pltpu.VMEM((8, FF_BLK), F32), pltpu.VMEM((8, FF_BLK), F32)]
        + [pltpu.VMEM((FF_BLK, D_MODEL), F32)] * 3 + [pltpu.VMEM((D_MODEL, FF_BLK), MXU)],
    )(d_f, wd, up_a, up_b, y_a, y_b, conv_w, conv_w, act, h2)


def _ffn_up_bwd(d_up, w_up, x2, m, d_out, ada_raw, ada_b, g_pre_ffn, g_post_mix, tm):
    s = x2.shape[0]
    half = N_DEV // 2

    def body(dup_ref, w_ref, x2_ref, m_ref, dout_ref, ar_ref, ab_ref, gpf_ref, gpm_ref,
             dx_ref, dm_ref, dsh_ref, dsc_ref, dgpf_ref, dgt1_ref, dgpm_ref):
        i = pl.program_id(0)

        @pl.when(i == 0)
        def _():
            for r in (dsh_ref, dsc_ref, dgpf_ref, dgt1_ref, dgpm_ref):
                r[...] = jnp.zeros(r.shape, F32)

        gt1 = _row(ar_ref, 2) + _row(ab_ref, 2)
        sc2 = _row(ar_ref, 4) + _row(ab_ref, 4)
        gpf, gpm = gpf_ref[...], gpm_ref[...]
        d_h2 = _dot(dup_ref[0, 0], w_ref[0])
        for j in range(1, half):
            d_h2 = d_h2 + _dot(dup_ref[j, 0], w_ref[j])
        for j in range(half):
            d_h2 = d_h2 + _dot(dup_ref[j, 1], w_ref[half + j])
        x2n, r2 = _rms(x2_ref[...])
        dsh_ref[...] += jnp.sum(d_h2, axis=0, keepdims=True)
        dsc_ref[...] += jnp.sum(d_h2 * (x2n * gpf), axis=0, keepdims=True)
        d_mod = d_h2 * (1.0 + sc2)
        dgpf_ref[...] += jnp.sum(d_mod * x2n, axis=0, keepdims=True)
        d_x2 = dout_ref[...] + _rms_bwd(d_mod * gpf, x2n, r2)
        dx_ref[...] = d_x2
        mhat, rm = _rms(m_ref[...])
        dgt1_ref[...] += jnp.sum(d_x2 * (mhat * gpm), axis=0, keepdims=True)
        d_mn = d_x2 * gt1
        dgpm_ref[...] += jnp.sum(d_mn * mhat, axis=0, keepdims=True)
        dm_ref[...] = _rms_bwd(d_mn * gpm, mhat, rm).astype(MXU)

    vec = pl.BlockSpec((1, D_MODEL), lambda i: (0, 0))
    tok = pl.BlockSpec((half, 2, tm, FF_BLK), lambda i: (0, 0, i, 0))
    return _call(
        body, name="ffn_up_bwd", grid=(s // tm,), sem=("arbitrary",),
        in_specs=[tok, _full(w_up.shape), _rows(tm, D_MODEL), _rows(tm, D_MODEL), _rows(tm, D_MODEL),
                  _full(ada_raw.shape), _full(ada_b.shape), _full(g_pre_ffn.shape), _full(g_post_mix.shape)],
        out_specs=[_rows(tm, D_MODEL), _rows(tm, D_MODEL), vec, vec, vec, vec, vec],
        out_shape=[_sds((s, D_MODEL), F32), _sds((s, D_MODEL), MXU)] + [_sds((1, D_MODEL), F32)] * 5,
    )(d_up, w_up, x2, m, d_out, ada_raw, ada_b, g_pre_ffn, g_post_mix)


def _out_proj_bwd(d_m, wo, o_pad, tm):
    s = d_m.shape[0]
    hp = HEADS * HEAD_PAD

    def body(dm_ref, wo_ref, o_ref, do_ref, dsgu_ref, delta_ref, wo_t):
        @pl.when(pl.program_id(0) == 0)
        def _():
            wo_t[...] = wo_ref[...].T

        d_cat = _dot(dm_ref[...], wo_t[...])
        d_o = d_cat[:, :hp]
        do_ref[...] = d_o.astype(MXU)
        dsgu_ref[...] = d_cat[:, hp:]
        prod = d_o * o_ref[...].astype(F32)
        for h in range(HEADS):
            delta_ref[h] = jnp.sum(prod[:, h * HEAD_PAD:(h + 1) * HEAD_PAD], axis=-1, keepdims=True)

    return _call(
        body, name="out_proj_bwd", grid=(s // tm,), sem=("arbitrary",),
        in_specs=[_rows(tm, D_MODEL), _full(wo.shape), _rows(tm, hp)],
        out_specs=[_rows(tm, hp), _rows(tm, GM_WIDTH), pl.BlockSpec((HEADS, tm, 1), lambda i: (0, i, 0))],
        out_shape=[_sds((s, hp), MXU), _sds((s, GM_WIDTH), F32), _sds((HEADS, s, 1), F32)],
        scratch=[pltpu.VMEM(wo.shape[::-1], MXU)],
    )(d_m, wo, o_pad)


def _attn_bwd(qp, kp, vp, d_o, lse, delta, tq, scattered, gathered):
    s = qp.shape[0]
    nq = s // tq
    hb = ATTN_HEADS_PER_STEP
    groups = HEADS // hb
    width = hb * HEAD_PAD
    ns, ng = len(scattered), len(gathered)
    nc = ns + ng
    slots = [slot for _, slot in scattered]

    def body(q_ref, k_ref, v_ref, do_ref, lse_ref, dl_ref, *rest):
        c_in, (dq_ref, dk_ref, dv_ref), c_out = rest[:nc], rest[nc:nc + 3], rest[nc + 3:2 * nc + 3]
        dk_sc, dv_sc = rest[2 * nc + 3:2 * nc + 5]
        sems = rest[2 * nc + 5:]
        s_start, s_finish = _scatter_steps(c_in[:ns], c_out[:ns], sems[:3], slots)
        g_start, g_forward, g_finish = _gather_steps(c_in[ns:], c_out[ns:], sems[3:])
        g, j = pl.program_id(0), pl.program_id(1)

        @pl.when((g == 0) & (j == 0))
        def _():
            s_start()
            g_start()

        pl.when((g == groups - 1) & (j == 0))(g_forward)

        @pl.when(j == 0)
        def _():
            dq_ref[...] = jnp.zeros(dq_ref.shape, F32)

        dk_sc[...] = jnp.zeros(dk_sc.shape, F32)
        dv_sc[...] = jnp.zeros(dv_sc.shape, F32)

        def tile(i, masked):
            rows = pl.ds(pl.multiple_of(i * tq, tq), tq)
            for hh in range(hb):
                lanes = slice(hh * HEAD_PAD, (hh + 1) * HEAD_PAD)
                q, do, k = q_ref[rows, lanes], do_ref[rows, lanes], k_ref[:, lanes]
                sc = _dot_nt(q, k)
                if masked:
                    sc = jnp.where(_chunk_mask(tq, tq, 0), sc, NEG_BIG)
                p = jnp.exp2(sc * SCALE_LOG2E - lse_ref[hh, rows, :])
                dv_sc[hh] += _dot_tn(p.astype(MXU), do)
                dp = _dot_nt(do, v_ref[:, lanes])
                ds = (p * (dp - dl_ref[hh, rows, :])).astype(MXU)
                dk_sc[hh] += _dot_tn(ds, q)
                dq_ref[rows, lanes] += _dot(ds, k) * ATTN_SCALE

        def off_diagonal_pair(pair, carry):
            i0 = j + 1 + 2 * pair
            both = pl.ds(pl.multiple_of(i0 * tq, tq), 2 * tq)
            for hh in range(hb):
                lanes = slice(hh * HEAD_PAD, (hh + 1) * HEAD_PAD)
                k, v = k_ref[:, lanes], v_ref[:, lanes]
                ps, dss = [], []
                for t in range(2):
                    rows = pl.ds(pl.multiple_of((i0 + t) * tq, tq), tq)
                    q, do = q_ref[rows, lanes], do_ref[rows, lanes]
                    p = jnp.exp2(_dot_nt(q, k) * SCALE_LOG2E - lse_ref[hh, rows, :])
                    ds = (p * (_dot_nt(do, v) - dl_ref[hh, rows, :])).astype(MXU)
                    dq_ref[rows, lanes] += _dot(ds, k) * ATTN_SCALE
                    ps.append(p.astype(MXU))
                    dss.append(ds)
                dv_sc[hh] += _dot_tn(jnp.concatenate(ps, axis=0), do_ref[both, lanes])
                dk_sc[hh] += _dot_tn(jnp.concatenate(dss, axis=0), q_ref[both, lanes])
            return carry

        below = nq - 1 - j
        tile(j, True)
        lax.fori_loop(0, below // 2, off_diagonal_pair, 0)

        @pl.when(below % 2 == 1)
        def _():
            tile(nq - 1, False)
        for hh in range(hb):
            lanes = slice(hh * HEAD_PAD, (hh + 1) * HEAD_PAD)
            dk_ref[:, lanes] = dk_sc[hh] * ATTN_SCALE
            dv_ref[:, lanes] = dv_sc[hh]
        @pl.when((g == groups - 1) & (j == nq - 1))
        def _():
            g_finish()
            s_finish()

    seq_spec = pl.BlockSpec((s, width), lambda g, j: (0, g))
    kv_spec = pl.BlockSpec((tq, width), lambda g, j: (j, g))
    col_spec = pl.BlockSpec((hb, s, 1), lambda g, j: (g, 0, 0))
    any_spec = pl.BlockSpec(memory_space=pl.ANY)
    outs = _call(
        body, name="attn_bwd", grid=(groups, nq), sem=("arbitrary", "arbitrary"),
        in_specs=[seq_spec, kv_spec, kv_spec, seq_spec, col_spec, col_spec] + [any_spec] * nc,
        out_specs=[seq_spec, kv_spec, kv_spec] + [any_spec] * nc,
        out_shape=[_sds(qp.shape, F32), _sds(qp.shape, F32), _sds(qp.shape, F32)]
        + [_scatter_out_shape(a, slot) for a, slot in scattered]
        + [_sds((N_DEV,) + a.shape, a.dtype) for a in gathered],
        scratch=[pltpu.VMEM((hb, tq, HEAD_PAD), F32), pltpu.VMEM((hb, tq, HEAD_PAD), F32)]
        + _comm_sems(ns) + _comm_sems(ng),
    )(qp, kp, vp, d_o, lse, delta, *[a for a, _ in scattered], *gathered)
    return outs[0], outs[1], outs[2], outs[3:3 + ns], outs[3 + ns:]


def _gmlp_bwd(z, d_sgu, ln_g, ln_b, w_sp, bias_exp, tm):
    s = z.shape[0]
    nblk = tm // GM_CHUNK

    def body(zu_ref, zv_ref, dsgu_ref, lg_ref, lb_ref, w_ref, be_ref,
             dguv_ref, dws_ref, dbs_ref, dlg_ref, dlb_ref, dbe_sc, dvln_sc, dlg_sc, dlb_sc):
        i = pl.program_id(0)

        @pl.when(i == 0)
        def _():
            for r in (dws_ref, dlg_sc, dlb_sc, dbe_sc):
                r[...] = jnp.zeros(r.shape, F32)

        seg = _seg_matrix()
        mask = _spatial_mask()
        wm = [(w_ref[h] * mask).astype(MXU) for h in range(HEADS)]
        zu, zv = zu_ref[...], zv_ref[...]
        gu = _gelu(zu)
        _, vhat, rstd = _gm_norm(zv, seg)
        lg = lg_ref[...]
        vln = (vhat * lg + lb_ref[...]).astype(MXU)
        d_sgu = dsgu_ref[...]
        for n in range(nblk):
            rows = slice(n * GM_CHUNK, (n + 1) * GM_CHUNK)
            vb = vln[rows]
            mixed = _gm_mix(wm, vb, GM_CHUNK) + be_ref[...]
            d_mixed = d_sgu[rows] * gu[rows]
            dguv_ref[rows, pl.ds(0, GM_WIDTH)] = ((d_sgu[rows] * mixed) * _gelu_grad(zu[rows])).astype(MXU)
            dbe_sc[...] += d_mixed
            dmb = d_mixed.astype(MXU)
            for p, (lanes, first) in enumerate(_gm_pairs(GM_CHUNK)):
                dm_pair, v_pair = dmb[:, lanes], vb[:, lanes]
                zero = jnp.zeros_like(dm_pair)
                dws_ref[2 * p] += _dot_nt(jnp.where(first, dm_pair, zero), v_pair)
                dws_ref[2 * p + 1] += _dot_nt(jnp.where(first, zero, dm_pair), v_pair)
            dvln_sc[rows, :] = _gm_mix(wm, dmb, GM_CHUNK, transposed=True)
        d_vln = dvln_sc[...]
        dlg_sc[...] += jnp.sum(d_vln * vhat, axis=0, keepdims=True)
        dlb_sc[...] += jnp.sum(d_vln, axis=0, keepdims=True)
        d_vhat = d_vln * lg
        d_gv = rstd * ((d_vhat - _split_dot(d_vhat, seg)) - vhat * _split_dot(d_vhat * vhat, seg))
        dguv_ref[:, pl.ds(GM_WIDTH, GM_WIDTH)] = (d_gv * _gelu_grad(zv)).astype(MXU)

        @pl.when(i == pl.num_programs(0) - 1)
        def _():
            for h in range(HEADS):
                dws_ref[h] = dws_ref[h] * mask
            hrow = lax.broadcasted_iota(jnp.int32, (HEADS, GM_WIDTH), 0)
            hlane = lax.broadcasted_iota(jnp.int32, (HEADS, GM_WIDTH), 1) >> 6
            ind = jnp.where(hrow == hlane, 1.0, 0.0).astype(MXU)
            acc = dbe_sc[...]
            hi = acc.astype(MXU)
            lo = (acc - hi.astype(F32)).astype(MXU)
            dbs_ref[...] = _dot_nt(ind, hi) + _dot_nt(ind, lo)
            pick = (lax.broadcasted_iota(jnp.int32, (GM_WIDTH, GM_DIM), 0) & (GM_DIM - 1)
                    == lax.broadcasted_iota(jnp.int32, (GM_WIDTH, GM_DIM), 1))
            pick = jnp.where(pick, 1.0, 0.0).astype(MXU)
            for src, dst in ((dlg_sc, dlg_ref), (dlb_sc, dlb_ref)):
                spread = jnp.where(hrow == hlane, jnp.broadcast_to(src[...], (HEADS, GM_WIDTH)), 0.0)
                dst[...] = _split_dot3(spread, pick)

    return _call(
        body, name="gmlp_bwd", grid=(s // tm,), sem=("arbitrary",),
        in_specs=[_rows(tm, GM_WIDTH, 1), _rows(tm, GM_WIDTH, 2), _rows(tm, GM_WIDTH), _full(ln_g.shape),
                  _full(ln_b.shape), _full(w_sp.shape), _full(bias_exp.shape)],
        out_specs=[_rows(tm, 2 * GM_WIDTH), _full(w_sp.shape), _full((HEADS, GM_CHUNK)), _full((HEADS, GM_DIM)),
                   _full((HEADS, GM_DIM))],
        out_shape=[_sds((s, 2 * GM_WIDTH), MXU), _sds(w_sp.shape, F32), _sds((HEADS, GM_CHUNK), F32),
                   _sds((HEADS, GM_DIM), F32), _sds((HEADS, GM_DIM), F32)],
        scratch=[pltpu.VMEM((GM_CHUNK, GM_WIDTH), F32), pltpu.VMEM((tm, GM_WIDTH), F32),
                 pltpu.VMEM((1, GM_WIDTH), F32), pltpu.VMEM((1, GM_WIDTH), F32)],
    )(z, z, d_sgu, ln_g, ln_b, w_sp, bias_exp)


def _mix_in_bwd(dq, dk, dv, z, d_guv, x, d_x_part, ada_raw, ada_b, g_pre, g_q, g_kv, w1t, wqt, wkv,
                cos_t, sin_t, tm):
    s = x.shape[0]
    hp = HEADS * HEAD_PAD
    za = Q_LORA + KV_LORA + HEAD_PAD

    def body(dq_ref, dk_ref, dv_ref, z_ref, dguv_ref, x_ref, dxp_ref, ar_ref, ab_ref, g_ref, gq_ref, gkv_ref,
             w1_ref, wq_ref, wkv_ref, cos_ref, sin_ref,
             gx_ref, dza_ref, dqp_ref, dkvp_ref, dsh_ref, dsc_ref, dg_ref, dgq_ref, dgkv_ref):
        i = pl.program_id(0)

        @pl.when(i == 0)
        def _():
            for r in (dsh_ref, dsc_ref, dg_ref, dgq_ref, dgkv_ref):
                r[...] = jnp.zeros(r.shape, F32)

        cos, sin = cos_ref[...], sin_ref[...]
        d_krot = jnp.zeros((tm, HEAD_PAD), F32)
        for h in range(HEADS):
            blk = slice(h * HEAD_PAD, (h + 1) * HEAD_PAD)
            dqp_ref[:, blk] = _rope_transposed(dq_ref[:, blk], cos, sin).astype(MXU)
            dk_h = dk_ref[:, blk]
            d_krot = d_krot + dk_h
            dkvp_ref[:, blk] = dk_h.astype(MXU)
        dkvp_ref[:, pl.ds(hp, hp)] = dv_ref[...].astype(MXU)
        lane = lax.broadcasted_iota(jnp.int32, (tm, HEAD_PAD), 1)
        d_kr = jnp.where((lane >= NOPE) & (lane < NOPE + ROPE), _rope_transposed(d_krot, cos, sin), 0.0)
        d_cqn = _dot(dqp_ref[...], wq_ref[...])
        d_ckvn = _dot_nt(dkvp_ref[...], wkv_ref[...])
        zt = z_ref[...]
        gq, gkv = gq_ref[...], gkv_ref[...]
        cq_hat, rq = _rms(zt[:, :Q_LORA])
        ckv_hat, rkv = _rms(zt[:, Q_LORA:Q_LORA + KV_LORA])
        dgq_ref[...] += jnp.sum(d_cqn * cq_hat, axis=0, keepdims=True)
        dgkv_ref[...] += jnp.sum(d_ckvn * ckv_hat, axis=0, keepdims=True)
        d_cq = _rms_bwd(d_cqn * gq, cq_hat, rq)
        d_ckv = _rms_bwd(d_ckvn * gkv, ckv_hat, rkv)
        d_za = jnp.concatenate([d_cq, d_ckv, d_kr], axis=1).astype(MXU)
        dza_ref[...] = d_za
        d_h1 = _dot(d_za, w1_ref[pl.ds(0, za), :]) + _dot(dguv_ref[...], w1_ref[pl.ds(za, 2 * GM_WIDTH), :])
        sc1 = _row(ar_ref, 1) + _row(ab_ref, 1)
        g = g_ref[...]
        xn, r1 = _rms(x_ref[...])
        dsh_ref[...] += jnp.sum(d_h1, axis=0, keepdims=True)
        dsc_ref[...] += jnp.sum(d_h1 * (xn * g), axis=0, keepdims=True)
        d_mod = d_h1 * (1.0 + sc1)
        dg_ref[...] += jnp.sum(d_mod * xn, axis=0, keepdims=True)
        gx_ref[...] = dxp_ref[...] + _rms_bwd(d_mod * g, xn, r1)

    vec = pl.BlockSpec((1, D_MODEL), lambda i: (0, 0))
    return _call(
        body, name="mix_in_bwd", grid=(s // tm,), sem=("arbitrary",),
        in_specs=[_rows(tm, hp), _rows(tm, hp), _rows(tm, hp), _rows(tm, za), _rows(tm, 2 * GM_WIDTH),
                  _rows(tm, D_MODEL), _rows(tm, D_MODEL), _full(ada_raw.shape), _full(ada_b.shape), _full(g_pre.shape),
                  _full(g_q.shape), _full(g_kv.shape), _full(w1t.shape), _full(wqt.shape),
                  _full(wkv.shape), _rows(tm, HEAD_PAD), _rows(tm, HEAD_PAD)],
        out_specs=[_rows(tm, D_MODEL), _rows(tm, za), _rows(tm, hp), _rows(tm, 2 * hp), vec, vec, vec,
                   _full(g_q.shape), _full(g_kv.shape)],
        out_shape=[_sds((s, D_MODEL), F32), _sds((s, za), MXU), _sds((s, hp), MXU), _sds((s, 2 * hp), MXU),
                   _sds((1, D_MODEL), F32), _sds((1, D_MODEL), F32), _sds((1, D_MODEL), F32),
                   _sds(g_q.shape, F32), _sds(g_kv.shape, F32)],
    )(dq, dk, dv, z, d_guv, x, d_x_part, ada_raw, ada_b, g_pre, g_q, g_kv, w1t, wqt, wkv, cos_t, sin_t)


def _tn_matmuls(arrays, pairs, name, ts):
    s = arrays[0].shape[0]
    steps = s // ts
    n_in, n_out = len(arrays), len(pairs)
    shapes = [(arrays[ia].shape[1], arrays[ib].shape[1]) for ia, ib in pairs]

    def body(*refs):
        ins, outs, accs = refs[:n_in], refs[n_in:n_in + n_out], refs[n_in + n_out:]
        k = pl.program_id(0)

        @pl.when(k == 0)
        def _():
            for acc in accs:
                acc[...] = jnp.zeros(acc.shape, F32)

        for (ia, ib), acc in zip(pairs, accs):
            acc[...] += _dot_tn(ins[ia][...], ins[ib][...])

        @pl.when(k == steps - 1)
        def _():
            for out, acc in zip(outs, accs):
                out[...] = acc[...].astype(MXU)

    return _call(
        body, name=name, grid=(steps,), sem=("arbitrary",),
        in_specs=[_rows(ts, a.shape[1]) for a in arrays],
        out_specs=[_full(shape) for shape in shapes],
        out_shape=[_sds(shape, MXU) for shape in shapes],
        scratch=[pltpu.VMEM(shape, F32) for shape in shapes],
    )(*arrays)


def _adamw(w, g, m, v):
    m2 = ADAM_B1 * m + (1.0 - ADAM_B1) * g
    v2 = ADAM_B2 * v + (1.0 - ADAM_B2) * (g * g)
    m_hat = m2 / (1.0 - ADAM_B1 ** ADAM_STEP)
    v_hat = v2 / (1.0 - ADAM_B2 ** ADAM_STEP)
    delta = -ADAM_LR * (m_hat / (jnp.sqrt(v_hat) + ADAM_EPS) + ADAM_WD * w)
    return delta, m2, v2


def _adam_reduce(recv, w, m, v, name):
    r, c = w.shape
    tr = r if r <= 512 else max(t for t in range(16, 513, 16) if r % t == 0)

    def body(p_ref, w_ref, m_ref, v_ref, g_ref, d_ref, mo_ref, vo_ref):
        g = p_ref[0].astype(F32)
        for j in range(1, N_DEV):
            g = g + p_ref[j].astype(F32)
        g_ref[...] = g
        d_ref[...], mo_ref[...], vo_ref[...] = _adamw(w_ref[...], g, m_ref[...], v_ref[...])

    blk = pl.BlockSpec((tr, c), lambda i: (i, 0))
    return _call(
        body, name=name, grid=(r // tr,), sem=("parallel",),
        in_specs=[pl.BlockSpec((N_DEV, tr, c), lambda i: (0, i, 0)), blk, blk, blk],
        out_specs=[blk] * 4, out_shape=[_sds((r, c), F32)] * 4,
    )(recv, w, m, v)


def _adam_w_ada(c_act_t, d_ada_cols, w, m, v):
    r, c = w.shape
    tr = 256

    def body(ct_ref, da_ref, w_ref, m_ref, v_ref, g_ref, d_ref, mo_ref, vo_ref):
        g = _dot(ct_ref[...], da_ref[...])
        g_ref[...] = g
        d_ref[...], mo_ref[...], vo_ref[...] = _adamw(w_ref[...], g, m_ref[...], v_ref[...])

    blk = pl.BlockSpec((tr, c), lambda i: (i, 0))
    return _call(
        body, name="adam_w_ada", grid=(r // tr,), sem=("parallel",),
        in_specs=[pl.BlockSpec((tr, c_act_t.shape[1]), lambda i: (i, 0)), _full(d_ada_cols.shape), blk, blk, blk],
        out_specs=[blk] * 4, out_shape=[_sds((r, c), F32)] * 4,
    )(c_act_t, d_ada_cols, w, m, v)


VEC_ROWS = D_MODEL // 128
PK_ADA = 0
PK_GAIN = PK_ADA + 6 * VEC_ROWS
PK_GQ = PK_GAIN + 4 * VEC_ROWS
PK_GKV = PK_GQ + Q_LORA // 128
PK_LOSS = PK_GKV + KV_LORA // 128
PK_LNG = 88
PK_LNB = PK_LNG + HEADS
PK_BS = PK_LNB + HEADS
PK_CB = PK_BS + HEADS
CB_ROWS = 6
PK_WS = PK_CB + N_DEV * CB_ROWS
PK_ROWS = PK_WS + HEADS * GM_CHUNK
assert PK_LOSS < PK_LNG and PK_ROWS % 8 == 0
LATE_GAIN = 2 * VEC_ROWS
LATE_GQ = 3 * VEC_ROWS
LATE_GKV = LATE_GQ + Q_LORA // 128
LATE_ROWS = 32


def _cb_chunks():
    return [(k, k * 128, min(128, FF_BLK - k * 128)) for k in range(CB_ROWS)]


def _put_rows(out_ref, row0, ref, width):
    for k in range(width // 128):
        out_ref[pl.ds(row0 + k, 1), :] = ref[:, pl.ds(k * 128, 128)]


def _pack_small(ada_rows, gains, loss_part, d_ln_g, d_ln_b, d_bs, d_cb, d_ws):
    half = N_DEV // 2

    def body(*refs):
        vec_refs = refs[:7]
        loss_ref, lng_ref, lnb_ref, bs_ref, cb_ref, ws_ref, out_ref = refs[7:]
        out_ref[pl.ds(0, PK_WS), :] = jnp.zeros((PK_WS, 128), F32)
        for n, ref in enumerate(vec_refs[:4]):
            _put_rows(out_ref, PK_ADA + (2 + n) * VEC_ROWS, ref, D_MODEL)
        for n, ref in enumerate(vec_refs[4:]):
            _put_rows(out_ref, PK_GAIN + (1 + n) * VEC_ROWS, ref, D_MODEL)
        _put_rows(out_ref, PK_LOSS, loss_ref, 128)
        out_ref[pl.ds(PK_LNG, HEADS), pl.ds(0, GM_DIM)] = lng_ref[...]
        out_ref[pl.ds(PK_LNB, HEADS), pl.ds(0, GM_DIM)] = lnb_ref[...]
        out_ref[pl.ds(PK_BS, HEADS), :] = bs_ref[...]
        for j in range(N_DEV):
            for k, lane, width in _cb_chunks():
                out_ref[pl.ds(PK_CB + j * CB_ROWS + k, 1), pl.ds(0, width)] = cb_ref[j % half, j // half, :, pl.ds(lane, width)]
        for h in range(HEADS):
            out_ref[pl.ds(PK_WS + h * GM_CHUNK, GM_CHUNK), :] = ws_ref[h]

    ins = list(ada_rows) + list(gains) + [loss_part, d_ln_g, d_ln_b, d_bs, d_cb, d_ws]
    return _call(body, name="pack_small", grid=(1,), in_specs=[_full(a.shape) for a in ins],
                 out_specs=_full((PK_ROWS, 128)), out_shape=_sds((PK_ROWS, 128), F32))(*ins)


def _pack_late(d_sh1, d_sc1, d_g_pre_mix, d_g_q, d_g_kv):
    def body(sh_ref, sc_ref, g_ref, gq_ref, gkv_ref, out_ref):
        out_ref[...] = jnp.zeros((LATE_ROWS, 128), F32)
        _put_rows(out_ref, 0, sh_ref, D_MODEL)
        _put_rows(out_ref, VEC_ROWS, sc_ref, D_MODEL)
        _put_rows(out_ref, LATE_GAIN, g_ref, D_MODEL)
        _put_rows(out_ref, LATE_GQ, gq_ref, Q_LORA)
        _put_rows(out_ref, LATE_GKV, gkv_ref, KV_LORA)

    ins = [d_sh1, d_sc1, d_g_pre_mix, d_g_q, d_g_kv]
    return _call(body, name="pack_late", grid=(1,), in_specs=[_full(a.shape) for a in ins],
                 out_specs=_full((LATE_ROWS, 128)), out_shape=_sds((LATE_ROWS, 128), F32))(*ins)


def _adam_small(gathered, late, params):
    n_par = len(params)

    def body(p_ref, late_ref, *refs):
        ins = [refs[3 * n:3 * n + 3] for n in range(n_par)]
        outs = [refs[3 * n_par + 4 * n:3 * n_par + 4 * n + 4] for n in range(n_par)]
        loss_ref, dada_ref = refs[7 * n_par:]

        def total(rows, lanes=slice(None), src=p_ref):
            g = src[0, rows, lanes]
            for j in range(1, N_DEV):
                g = g + src[j, rows, lanes]
            return g

        def apply(n, g, idx):
            w_ref, m_ref, v_ref = ins[n]
            d, m2, v2 = _adamw(w_ref[idx], g, m_ref[idx], v_ref[idx])
            for ref, val in zip(outs[n], (g, d, m2, v2)):
                ref[idx] = val

        def vector(n, src, row0, width, lane0=0):
            for k in range(width // 128):
                apply(n, total(pl.ds(row0 + k, 1), src=src), (slice(None), pl.ds(lane0 + k * 128, 128)))

        vector(0, late_ref, 0, 2 * D_MODEL)
        vector(0, p_ref, PK_ADA + 2 * VEC_ROWS, 4 * D_MODEL, lane0=2 * D_MODEL)
        vector(1, late_ref, LATE_GAIN, D_MODEL)
        for n in range(1, 4):
            vector(1 + n, p_ref, PK_GAIN + n * VEC_ROWS, D_MODEL)
        vector(5, late_ref, LATE_GQ, Q_LORA)
        vector(6, late_ref, LATE_GKV, KV_LORA)
        apply(7, total(pl.ds(PK_LNG, HEADS), pl.ds(0, GM_DIM)), (0,))
        apply(8, total(pl.ds(PK_LNB, HEADS), pl.ds(0, GM_DIM)), (0,))
        for h in range(HEADS):
            apply(9, total(pl.ds(PK_WS + h * GM_CHUNK, GM_CHUNK)), (0, h))
        apply(10, total(pl.ds(PK_BS, HEADS)), (0,))
        for j in range(N_DEV):
            for k, lane, width in _cb_chunks():
                apply(11, total(pl.ds(PK_CB + j * CB_ROWS + k, 1), pl.ds(0, width)), (pl.ds(j, 1), pl.ds(lane, width)))
        loss_ref[...] = total(pl.ds(PK_LOSS, 1))
        dada_ref[:, pl.ds(0, 2 * VEC_ROWS), :] = late_ref[:, pl.ds(0, 2 * VEC_ROWS), :]
        dada_ref[:, pl.ds(2 * VEC_ROWS, 4 * VEC_ROWS), :] = p_ref[:, pl.ds(PK_ADA + 2 * VEC_ROWS, 4 * VEC_ROWS), :]

    flat = [a for triple in params for a in triple]
    out_shape = [_sds(w.shape, F32) for w, _, _ in params for _ in range(4)]
    out_shape += [_sds((1, 128), F32), _sds((N_DEV, 6 * VEC_ROWS, 128), F32)]
    outs = _call(body, name="adam_small", grid=(1,),
                 in_specs=[_full(gathered.shape), _full(late.shape)] + [_full(a.shape) for a in flat],
                 out_specs=[_full(o.shape) for o in out_shape], out_shape=out_shape)(gathered, late, *flat)
    return [tuple(outs[4 * n:4 * n + 4]) for n in range(n_par)], outs[-2], outs[-1]


def _rope_tables(s):
    pos = jnp.arange(s, dtype=F32)
    inv = ROPE_THETA ** (-jnp.arange(0, ROPE, 2, dtype=F32) / ROPE)
    lane_inv = jnp.concatenate([jnp.zeros((NOPE,), F32), inv, inv, jnp.zeros((HEAD_PAD - NOPE - ROPE,), F32)])
    ang = pos[:, None] * lane_inv[None, :]
    return jnp.cos(ang), jnp.sin(ang)


def kernel(x, c, w_ada, b_ada, g_pre_mix, g_post_mix, w_in, g_q, w_uq, g_kv, w_ukv, gm_ln_g, gm_ln_b, w_spatial, b_spatial, w_out, g_pre_ffn, g_post_ffn, w_up, conv_w, conv_b, w_down, loss_target, m_w_ada, m_b_ada, m_g_pre_mix, m_g_post_mix, m_w_in, m_g_q, m_w_uq, m_g_kv, m_w_ukv, m_gm_ln_g, m_gm_ln_b, m_w_spatial, m_b_spatial, m_w_out, m_g_pre_ffn, m_g_post_ffn, m_w_up, m_conv_w, m_conv_b, m_w_down, v_w_ada, v_b_ada, v_g_pre_mix, v_g_post_mix, v_w_in, v_g_q, v_w_uq, v_g_kv, v_w_ukv, v_gm_ln_g, v_gm_ln_b, v_w_spatial, v_b_spatial, v_w_out, v_g_pre_ffn, v_g_post_ffn, v_w_up, v_conv_w, v_conv_b, v_w_down):
    s = x.shape[1]
    tm = min(512, s)
    tf = min(2 * ROW_SUB, s)
    tq = min(512, s)
    ts = min(2048, s)
    hp = HEADS * HEAD_PAD
    half = N_DEV // 2
    my_slot = 4 * lax.axis_index("x") + 2 * lax.axis_index("y") + lax.axis_index("c")
    x2d, target = x[0], loss_target[0]

    def t_(a):
        return jnp.swapaxes(a[0], 0, 1)

    w_in_t, m_in_t, v_in_t = t_(w_in), t_(m_w_in), t_(v_w_in)
    w_uq_t, m_uq_t, v_uq_t = t_(w_uq), t_(m_w_uq), t_(v_w_uq)
    w_up_t, m_up_t, v_up_t = t_(w_up), t_(m_w_up), t_(v_w_up)
    (g_c, g_in_t, g_uq_t, g_ukv, g_cw), _ = _exchange(
        [c, w_in_t.astype(MXU), w_uq_t.astype(MXU), w_ukv[0].astype(MXU), conv_w[0]], [], "gather_mixer_weights")

    w_in_f = g_in_t.reshape(-1, D_MODEL)
    o1, o2, o3 = Q_LORA, Q_LORA + KV_LORA, Q_LORA + KV_LORA + ROPE
    w1t = jnp.concatenate([w_in_f[:o2], jnp.zeros((NOPE, D_MODEL), MXU), w_in_f[o2:o3],
                           jnp.zeros((HEAD_PAD - NOPE - ROPE, D_MODEL), MXU), w_in_f[o3:]], axis=0)
    wqt = jnp.pad(g_uq_t, ((0, 0), (0, HEAD_PAD - NOPE - ROPE), (0, 0))).reshape(hp, Q_LORA)
    w_ukv_f = jnp.transpose(g_ukv, (1, 0, 2)).reshape(KV_LORA, HEADS, 2 * NOPE)
    pad_head = ((0, 0), (0, 0), (0, HEAD_PAD - NOPE))
    wkv = jnp.concatenate([jnp.pad(w_ukv_f[:, :, :NOPE], pad_head).reshape(KV_LORA, hp),
                           jnp.pad(w_ukv_f[:, :, NOPE:], pad_head).reshape(KV_LORA, hp)], axis=1)
    cb8 = conv_b.reshape(N_DEV, 1, FF_BLK)
    bias_exp = jnp.repeat(b_spatial[0].T, GM_DIM, axis=1)
    ln_g, ln_b = gm_ln_g.reshape(1, GM_WIDTH), gm_ln_b.reshape(1, GM_WIDTH)
    w_sp = w_spatial[0]
    cos_t, sin_t = _rope_tables(s)

    ada_part, c_act = _ada_fwd(g_c.reshape(N_DEV, D_MODEL), w_ada[0])
    _, (ada_recv,) = _exchange([], [(ada_part.reshape(N_DEV, 1, -1), _plain_slot)], "ada_rows")
    ada_raw = ada_recv.reshape(6, D_MODEL)
    ada_b = b_ada.reshape(6, D_MODEL)

    h1, z, qp, kp, vp, cqn, ckvn = _mix_in_fwd(x2d, ada_raw, ada_b, g_pre_mix, w1t, g_q, g_kv, wqt, wkv, cos_t, sin_t, tm)
    sgu = _gmlp_fwd(z, ln_g, ln_b, w_sp, bias_exp, tm)
    o_pad, lse, (g_out, g_up, g_down) = _attn_fwd(
        qp, kp, vp, tq, [w_out[0].astype(MXU), w_up_t.astype(MXU), w_down[0].astype(MXU)])
    w_out_f = g_out.reshape(2 * GM_WIDTH, D_MODEL)
    wo_attn = jnp.pad(w_out_f[:GM_WIDTH].reshape(HEADS, NOPE, D_MODEL), ((0, 0), (0, HEAD_PAD - NOPE), (0, 0)))
    wo = jnp.concatenate([wo_attn.reshape(hp, D_MODEL), w_out_f[GM_WIDTH:]], axis=0)
    wd = g_down.reshape(half, FF_BLK, D_MODEL)
    m_mix, x2, h2 = _out_proj_fwd(o_pad, sgu, wo, x2d, ada_raw, ada_b, g_post_mix, g_pre_ffn, tm)
    up_a, up_b, y_a, y_b, act = _ffn_up_fwd(h2, g_up, g_cw, cb8, tf)
    d_out, d_f, loss_part, d_gt2, d_g_post_ffn = _ffn_down_fwd(act, wd, x2, target, ada_raw, ada_b, g_post_ffn, tf)

    d_up, d_cw, d_cb, p_down, p_up = _ffn_down_bwd(d_f, wd, up_a, up_b, y_a, y_b, g_cw, act, h2, tf)
    p_down = p_down.reshape(N_DEV, -1, D_MODEL)
    d_x2, d_m, d_sh2, d_sc2, d_g_pre_ffn, d_gt1, d_g_post_mix = _ffn_up_bwd(
        d_up, g_up, x2, m_mix, d_out, ada_raw, ada_b, g_pre_ffn, g_post_mix, tm)
    dwo_attn, dwo_sgu = _tn_matmuls([o_pad, sgu, d_m], [(0, 2), (1, 2)], "dw_out", ts)
    dwo_attn = dwo_attn.reshape(HEADS, HEAD_PAD, D_MODEL)[:, :NOPE]
    p_out = jnp.concatenate([dwo_attn.reshape(GM_WIDTH, D_MODEL), dwo_sgu], axis=0).reshape(N_DEV, -1, D_MODEL)
    d_o, d_sgu, delta = _out_proj_bwd(d_m, wo, o_pad, tm)
    d_guv, d_ws, d_bs, d_ln_g, d_ln_b = _gmlp_bwd(z, d_sgu, ln_g, ln_b, w_sp, bias_exp, tm)
    packed = _pack_small([d_gt1, d_sh2, d_sc2, d_gt2], [d_g_post_mix, d_g_pre_ffn, d_g_post_ffn], loss_part,
                         d_ln_g, d_ln_b, d_bs, d_cb, d_ws)

    def ffn_slot(j):
        return (j % half, j // half)

    dq, dk, dv, (r_out, r_up, r_down, r_cw), (g_small,) = _attn_bwd(
        qp, kp, vp, d_o, lse, delta, tq,
        [(p_out, _plain_slot), (p_up, ffn_slot), (p_down, _plain_slot), (d_cw, ffn_slot)], [packed])
    grad_x, d_za, d_qp, d_kvp, d_sh1, d_sc1, d_g_pre_mix, d_g_q, d_g_kv = _mix_in_bwd(
        dq, dk, dv, z, d_guv, x2d, d_x2, ada_raw, ada_b, g_pre_mix, g_q, g_kv, w1t, wqt, wkv, cos_t, sin_t,
        min(256, s))
    dw1a, dw1b, dwq, dwkv = _tn_matmuls([d_za, d_guv, h1, d_qp, cqn, ckvn, d_kvp],
                                        [(0, 2), (1, 2), (3, 4), (5, 6)], "dw_mixer", ts // 2)
    d_w_in_t = jnp.concatenate([dw1a[:o2], dw1a[o2 + NOPE:o2 + NOPE + ROPE], dw1b], axis=0)
    p_in = d_w_in_t.reshape(N_DEV, -1, D_MODEL)
    p_uq = dwq.reshape(HEADS, HEAD_PAD, Q_LORA)[:, :NOPE + ROPE]
    dwk = dwkv[:, :hp].reshape(KV_LORA, HEADS, HEAD_PAD)[:, :, :NOPE]
    dwv = dwkv[:, hp:].reshape(KV_LORA, HEADS, HEAD_PAD)[:, :, :NOPE]
    p_ukv = jnp.transpose(jnp.concatenate([dwk, dwv], axis=2), (1, 0, 2))

    (g_late,), (r_in, r_uq, r_ukv) = _exchange(
        [_pack_late(d_sh1, d_sc1, d_g_pre_mix, d_g_q, d_g_kv)],
        [(p_in, _plain_slot), (p_uq, _plain_slot), (p_ukv, _plain_slot)], "final_exchange")
    small_params = [(b_ada, m_b_ada, v_b_ada), (g_pre_mix, m_g_pre_mix, v_g_pre_mix),
                    (g_post_mix, m_g_post_mix, v_g_post_mix), (g_pre_ffn, m_g_pre_ffn, v_g_pre_ffn),
                    (g_post_ffn, m_g_post_ffn, v_g_post_ffn), (g_q, m_g_q, v_g_q), (g_kv, m_g_kv, v_g_kv),
                    (gm_ln_g, m_gm_ln_g, v_gm_ln_g), (gm_ln_b, m_gm_ln_b, v_gm_ln_b),
                    (w_spatial, m_w_spatial, v_w_spatial), (b_spatial, m_b_spatial, v_b_spatial),
                    tuple(a.reshape(N_DEV, FF_BLK) for a in (conv_b, m_conv_b, v_conv_b))]
    small_out, loss_row, d_ada_all = _adam_small(g_small, g_late, small_params)
    small_out[11] = tuple(o.reshape(conv_b.shape) for o in small_out[11])
    loss = loss_row[0, 0]

    def big(recv, w, m, v, name):
        g, d, m2, v2 = _adam_reduce(recv, w[0], m[0], v[0], name)
        return g[None], d[None], m2[None], v2[None]

    def big_t(recv, w_t, m_t, v_t, name):
        return tuple(jnp.swapaxes(o, 0, 1)[None] for o in _adam_reduce(recv, w_t, m_t, v_t, name))

    a_in = big_t(r_in, w_in_t, m_in_t, v_in_t, "adam_w_in")
    a_uq = big_t(r_uq, w_uq_t, m_uq_t, v_uq_t, "adam_w_uq")
    a_ukv = big(r_ukv, w_ukv, m_w_ukv, v_w_ukv, "adam_w_ukv")
    a_out = big(r_out, w_out, m_w_out, v_w_out, "adam_w_out")
    a_up = big_t(r_up, w_up_t, m_up_t, v_up_t, "adam_w_up")
    a_down = big(r_down, w_down, m_w_down, v_w_down, "adam_w_down")
    ada_cols = w_ada.shape[2]
    d_ada_cols = lax.dynamic_slice(d_ada_all.reshape(N_DEV, 6 * D_MODEL), (0, my_slot * ada_cols), (N_DEV, ada_cols))
    pad_seq = 128 - N_DEV
    a_ada = tuple(t[None] for t in _adam_w_ada(jnp.pad(c_act.T, ((0, 0), (0, pad_seq))).astype(MXU),
                                               jnp.pad(d_ada_cols, ((0, pad_seq), (0, 0))).astype(MXU),
                                               w_ada[0], m_w_ada[0], v_w_ada[0]))
    a_cw = big(r_cw, conv_w, m_conv_w, v_conv_w, "adam_conv_w")

    def small(k):
        return small_out[k]

    per_weight = [a_ada, small(0), small(1), small(2), a_in, small(5), a_uq, small(6), a_ukv, small(7), small(8),
                  small(9), small(10), a_out, small(3), small(4), a_up, a_cw, small(11), a_down]
    outs = [loss, grad_x[None]]
    for k in range(4):
        outs += [t[k] for t in per_weight]
    return tuple(outs)
```

```python
import functools

import jax
import jax.numpy as jnp
from jax import lax
from jax.experimental import pallas as pl
from jax.experimental.pallas import tpu as pltpu

F32 = jnp.float32
MXU = jnp.bfloat16

N_DEV = 8
D_MODEL = 1024
HEADS = 8
HEAD_PAD = 128
NOPE = 64
ROPE = 32
Q_LORA = 256
KV_LORA = 128
GM_WIDTH = 512
GM_DIM = 64
GM_CHUNK = 128
CHUNK_SHIFT = 6
ROPE_THETA = 10000.0
ATTN_SCALE = (NOPE + ROPE) ** -0.5
LOG2E = 1.4426950408889634
SCALE_LOG2E = ATTN_SCALE * LOG2E
Z_COLS = 1536
FF_BLK = 704
EPS = 1e-6
ADAM_LR = 0.001
ADAM_B1 = 0.9
ADAM_B2 = 0.999
ADAM_EPS = 1e-08
ADAM_WD = 0.01
ADAM_STEP = 10
VMEM_LIMIT = 56 * 1024 * 1024
MESH = pl.DeviceIdType.MESH


def _dot(a, b):
    return jnp.dot(a, b, preferred_element_type=F32)


def _dot_nt(a, b):
    return lax.dot_general(a, b, (((1,), (1,)), ((), ())), preferred_element_type=F32)


def _dot_tn(a, b):
    return lax.dot_general(a, b, (((0,), (0,)), ((), ())), preferred_element_type=F32)


def _call(body, *, name, grid, in_specs, out_specs, out_shape, scratch=(), sem=None):
    params = pltpu.CompilerParams(dimension_semantics=sem, vmem_limit_bytes=VMEM_LIMIT)
    return pl.pallas_call(body, name=name, grid=grid, in_specs=in_specs, out_specs=out_specs,
                          out_shape=out_shape, scratch_shapes=list(scratch), compiler_params=params)


def _full(shape):
    n = len(shape)
    return pl.BlockSpec(shape, lambda *_: (0,) * n)


def _rows(tm, cols, col_block=0):
    return pl.BlockSpec((tm, cols), lambda i: (i, col_block))


def _sds(shape, dtype):
    return jax.ShapeDtypeStruct(shape, dtype)


def _row(ref, k):
    return ref[pl.ds(k, 1), :]


def _rms(x):
    r = lax.rsqrt(jnp.mean(x * x, axis=-1, keepdims=True) + EPS)
    return x * r, r


def _rms_bwd(d_hat, hat, r):
    return r * (d_hat - hat * jnp.mean(d_hat * hat, axis=-1, keepdims=True))


def _rope_partner(t):
    lane = lax.broadcasted_iota(jnp.int32, t.shape, 1)
    swapped = jnp.where(lane < NOPE + ROPE // 2, -pltpu.roll(t, HEAD_PAD - ROPE // 2, 1), pltpu.roll(t, ROPE // 2, 1))
    return jnp.where((lane >= NOPE) & (lane < NOPE + ROPE), swapped, 0.0)


def _rope(t, cos, sin):
    return t * cos + _rope_partner(t) * sin


def _rope_transposed(g, cos, sin):
    return g * cos - _rope_partner(g * sin)


def _gelu(x):
    return x * (0.5 * (1.0 + jnp.tanh(0.7978845608028654 * (x + 0.044715 * (x * x * x)))))


def _gelu_grad(x):
    t = jnp.tanh(0.7978845608028654 * (x + 0.044715 * (x * x * x)))
    return 0.5 * (1.0 + t) + 0.5 * x * (1.0 - t * t) * (0.7978845608028654 * (1.0 + 3.0 * 0.044715 * (x * x)))


def _split_dot(x, mat):
    hi = x.astype(MXU)
    lo = (x - hi.astype(F32)).astype(MXU)
    return _dot(hi, mat) + _dot(lo, mat)


def _split_dot3(x, mat):
    hi = x.astype(MXU)
    r1 = x - hi.astype(F32)
    mid = r1.astype(MXU)
    lo = (r1 - mid.astype(F32)).astype(MXU)
    return (_dot(hi, mat) + _dot(mid, mat)) + _dot(lo, mat)


def _seg_matrix():
    r = lax.broadcasted_iota(jnp.int32, (GM_WIDTH, GM_WIDTH), 0) >> 6
    c = lax.broadcasted_iota(jnp.int32, (GM_WIDTH, GM_WIDTH), 1) >> 6
    return jnp.where(r == c, 1.0 / GM_DIM, 0.0).astype(MXU)


def _spatial_mask():
    i = lax.broadcasted_iota(jnp.int32, (GM_CHUNK, GM_CHUNK), 0) >> CHUNK_SHIFT
    j = lax.broadcasted_iota(jnp.int32, (GM_CHUNK, GM_CHUNK), 1) >> CHUNK_SHIFT
    return (j <= i).astype(F32)


def _my_place():
    return lax.axis_index("x"), lax.axis_index("y"), lax.axis_index("c")


def _flat(p):
    return 4 * p[0] + 2 * p[1] + p[2]


def _comm_sems(n):
    return [pltpu.SemaphoreType.DMA((7 * n,)), pltpu.SemaphoreType.DMA((7 * n,)), pltpu.SemaphoreType.DMA((n,))]


def _gather_steps(ins, outs, sems):
    send_sems, recv_sems, local_sems = sems
    n = len(ins)
    x, y, c = _my_place()
    me, sibling = (x, y, c), (x, y, 1 - c)
    chips = [(1 - x, y), (x, 1 - y), (1 - x, 1 - y)]

    def copy(a, k, block, to, src=None):
        slot = outs[a].at[_flat(block)]
        return pltpu.make_async_remote_copy(
            src_ref=slot if src is None else src, dst_ref=slot,
            send_sem=send_sems.at[7 * a + k], recv_sem=recv_sems.at[7 * a + k],
            device_id=to, device_id_type=MESH)

    def mine():
        return [pltpu.make_async_copy(ins[a], outs[a].at[_flat(me)], local_sems.at[a]) for a in range(n)]

    def first():
        cps = []
        for a in range(n):
            cps.append(copy(a, 0, me, sibling, src=ins[a]))
            cps += [copy(a, 1 + j, me, (*chip, c), src=ins[a]) for j, chip in enumerate(chips)]
        return cps

    def passed():
        return [copy(a, 4 + j, (*chip, c), sibling) for a in range(n) for j, chip in enumerate(chips)]

    def start():
        for cp in mine() + first():
            cp.start()

    def forward():
        for a in range(n):
            for j, chip in enumerate(chips):
                copy(a, 1 + j, (*chip, c), me).wait_recv()
                copy(a, 4 + j, (*chip, c), sibling).start()

    def finish():
        for a in range(n):
            copy(a, 0, sibling, me).wait_recv()
            for j, chip in enumerate(chips):
                copy(a, 4 + j, (*chip, 1 - c), me).wait_recv()
        for cp in first() + passed():
            cp.wait_send()
        for cp in mine():
            cp.wait()

    return start, forward, finish


def _scatter_steps(ins, outs, sems, slots):
    send_sems, recv_sems, local_sems = sems
    n = len(ins)
    flips = [(fx, fy, fc) for fx in (0, 1) for fy in (0, 1) for fc in (0, 1)][1:]
    me = _my_place()

    def peer(f):
        return tuple(1 - v if b else v for v, b in zip(me, f))

    def copy(a, k, arriving=False):
        p = peer(flips[k])
        return pltpu.make_async_remote_copy(
            src_ref=ins[a].at[slots[a](_flat(p))], dst_ref=outs[a].at[_flat(p if arriving else me)],
            send_sem=send_sems.at[7 * a + k], recv_sem=recv_sems.at[7 * a + k],
            device_id=p, device_id_type=MESH)

    def mine():
        return [pltpu.make_async_copy(ins[a].at[slots[a](_flat(me))], outs[a].at[_flat(me)], local_sems.at[a])
                for a in range(n)]

    def start():
        for cp in mine() + [copy(a, k) for a in range(n) for k in range(7)]:
            cp.start()

    def finish():
        for a in range(n):
            for k in range(7):
                copy(a, k, arriving=True).wait_recv()
        for a in range(n):
            for k in range(7):
                copy(a, k).wait_send()
        for cp in mine():
            cp.wait()

    return start, finish


def _plain_slot(j):
    return (j,)


def _scatter_out_shape(arr, slot):
    return _sds((N_DEV,) + arr.shape[len(slot(0)):], arr.dtype)


def _exchange(gathered, scattered, name):
    ng, ns = len(gathered), len(scattered)
    slots = [slot for _, slot in scattered]

    def body(*refs):
        g_in, s_in = refs[:ng], refs[ng:ng + ns]
        g_out, s_out = refs[ng + ns:2 * ng + ns], refs[2 * ng + ns:2 * (ng + ns)]
        sems = refs[2 * (ng + ns):]
        g_start, g_forward, g_finish = _gather_steps(g_in, g_out, sems[:3])
        s_start, s_finish = _scatter_steps(s_in, s_out, sems[3:], slots)
        g_start()
        s_start()
        g_forward()
        g_finish()
        s_finish()

    any_spec = pl.BlockSpec(memory_space=pl.ANY)
    outs = pl.pallas_call(
        body, name=name,
        in_specs=[any_spec] * (ng + ns), out_specs=[any_spec] * (ng + ns),
        out_shape=[_sds((N_DEV,) + a.shape, a.dtype) for a in gathered]
        + [_scatter_out_shape(a, slot) for a, slot in scattered],
        scratch_shapes=_comm_sems(max(ng, 1)) + _comm_sems(max(ns, 1)),
    )(*gathered, *[a for a, _ in scattered])
    return outs[:ng], outs[ng:]


def _ada_fwd(c_all, w_ada):
    def body(c_ref, w_ref, part_ref, act_ref):
        cv = c_ref[...]
        act = cv * jax.nn.sigmoid(cv)
        act_ref[...] = act
        part_ref[...] = _dot(act.astype(MXU), w_ref[...].astype(MXU))

    cols = w_ada.shape[1]
    return _call(body, name="ada_fwd", grid=(1,),
                 in_specs=[_full(c_all.shape), _full(w_ada.shape)],
                 out_specs=[_full((N_DEV, cols)), _full(c_all.shape)],
                 out_shape=[_sds((N_DEV, cols), F32), _sds(c_all.shape, F32)])(c_all, w_ada)


def _mix_in_fwd(x, ada_raw, ada_b, g_pre, w1, g_q, g_kv, wq, wkv, cos_t, sin_t, tm):
    s = x.shape[0]

    def body(x_ref, ar_ref, ab_ref, g_ref, w1_ref, gq_ref, gkv_ref, wq_ref, wkv_ref, cos_ref, sin_ref,
             h1_ref, z_ref, qp_ref, kp_ref, vp_ref, cqn_ref, ckvn_ref, w1_n, wq_n):
        @pl.when(pl.program_id(0) == 0)
        def _():
            w1_n[...] = w1_ref[...].T
            wq_n[...] = wq_ref[...].T

        sh = _row(ar_ref, 0) + _row(ab_ref, 0)
        sc = _row(ar_ref, 1) + _row(ab_ref, 1)
        xn, _ = _rms(x_ref[...])
        hb = ((xn * g_ref[...]) * (1.0 + sc) + sh).astype(MXU)
        h1_ref[...] = hb
        z = _dot(hb, w1_n[...])
        z_ref[...] = z
        cos, sin = cos_ref[...], sin_ref[...]
        cqn = (_rms(z[:, :Q_LORA])[0] * gq_ref[...]).astype(MXU)
        ckvn = (_rms(z[:, Q_LORA:Q_LORA + KV_LORA])[0] * gkv_ref[...]).astype(MXU)
        cqn_ref[...] = cqn
        ckvn_ref[...] = ckvn
        q = _dot(cqn, wq_n[...])
        kv = _dot(ckvn, wkv_ref[...])
        k_rope = _rope(z[:, Q_LORA + KV_LORA:Q_LORA + KV_LORA + HEAD_PAD], cos, sin)
        for h in range(HEADS):
            blk = slice(h * HEAD_PAD, (h + 1) * HEAD_PAD)
            qp_ref[:, blk] = _rope(q[:, blk], cos, sin).astype(MXU)
            kp_ref[:, blk] = (kv[:, blk] + k_rope).astype(MXU)
        v_lane = lax.broadcasted_iota(jnp.int32, (tm, HEADS * HEAD_PAD), 1) & (HEAD_PAD - 1)
        vp_ref[...] = jnp.where(v_lane == NOPE, 1.0, kv[:, HEADS * HEAD_PAD:]).astype(MXU)

    hp = HEADS * HEAD_PAD
    return _call(
        body, name="mix_in_fwd", grid=(s // tm,), sem=("arbitrary",),
        in_specs=[_rows(tm, D_MODEL), _full(ada_raw.shape), _full(ada_b.shape), _full(g_pre.shape), _full(w1.shape),
                  _full(g_q.shape), _full(g_kv.shape), _full(wq.shape), _full(wkv.shape),
                  _rows(tm, HEAD_PAD), _rows(tm, HEAD_PAD)],
        out_specs=[_rows(tm, D_MODEL), _rows(tm, Z_COLS), _rows(tm, hp), _rows(tm, hp), _rows(tm, hp),
                   _rows(tm, Q_LORA), _rows(tm, KV_LORA)],
        out_shape=[_sds((s, D_MODEL), MXU), _sds((s, Z_COLS), F32), _sds((s, hp), MXU), _sds((s, hp), MXU),
                   _sds((s, hp), MXU), _sds((s, Q_LORA), MXU), _sds((s, KV_LORA), MXU)],
        scratch=[pltpu.VMEM(w1.shape[::-1], MXU), pltpu.VMEM(wq.shape[::-1], MXU)],
    )(x, ada_raw, ada_b, g_pre, w1, g_q, g_kv, wq, wkv, cos_t, sin_t)


def _gm_norm(zv, seg):
    gv = _gelu(zv)
    cen = gv - _split_dot(gv, seg)
    rstd = lax.rsqrt(_split_dot(cen * cen, seg) + EPS)
    return gv, cen * rstd, rstd


def _gm_pairs(rows):
    first = lax.broadcasted_iota(jnp.int32, (rows, 2 * GM_DIM), 1) < GM_DIM
    return [(slice(p * 2 * GM_DIM, (p + 1) * 2 * GM_DIM), first) for p in range(HEADS // 2)]


def _gm_mix(wm, vb, rows, transposed=False):
    dot = _dot_tn if transposed else _dot
    return jnp.concatenate([jnp.where(first, dot(wm[2 * p], vb[:, lanes]), dot(wm[2 * p + 1], vb[:, lanes]))
                            for p, (lanes, first) in enumerate(_gm_pairs(rows))], axis=1)


def _gmlp_fwd(z, ln_g, ln_b, w_sp, bias_exp, tm):
    s = z.shape[0]
    nblk = tm // GM_CHUNK

    def body(zu_ref, zv_ref, lg_ref, lb_ref, w_ref, be_ref, sgu_ref):
        seg = _seg_matrix()
        mask = _spatial_mask()
        wm = [(w_ref[h] * mask).astype(MXU) for h in range(HEADS)]
        gu = _gelu(zu_ref[...])
        _, vhat, _ = _gm_norm(zv_ref[...], seg)
        vln = (vhat * lg_ref[...] + lb_ref[...]).astype(MXU)
        for n in range(nblk):
            rows = slice(n * GM_CHUNK, (n + 1) * GM_CHUNK)
            mixed = _gm_mix(wm, vln[rows], GM_CHUNK) + be_ref[...]
            sgu_ref[rows, :] = (gu[rows] * mixed).astype(MXU)

    return _call(
        body, name="gmlp_fwd", grid=(s // tm,), sem=("parallel",),
        in_specs=[_rows(tm, GM_WIDTH, 1), _rows(tm, GM_WIDTH, 2), _full(ln_g.shape), _full(ln_b.shape),
                  _full(w_sp.shape), _full(bias_exp.shape)],
        out_specs=_rows(tm, GM_WIDTH), out_shape=_sds((s, GM_WIDTH), MXU),
    )(z, z, ln_g, ln_b, w_sp, bias_exp)


def _chunk_mask(n_q, n_k, q_off):
    qc = (q_off + lax.broadcasted_iota(jnp.int32, (n_q, n_k), 0)) >> CHUNK_SHIFT
    kc = lax.broadcasted_iota(jnp.int32, (n_q, n_k), 1) >> CHUNK_SHIFT
    return kc <= qc


NEG_BIG = -1e30
ATTN_HEADS_PER_STEP = 2


def _attn_fwd(qp, kp, vp, tq, gathered):
    s = qp.shape[0]
    nq = s // tq
    hb = ATTN_HEADS_PER_STEP
    groups = HEADS // hb
    width = hb * HEAD_PAD
    ng = len(gathered)

    def body(q_ref, k_ref, v_ref, *rest):
        g_in, (o_ref, lse_ref), g_out = rest[:ng], rest[ng:ng + 2], rest[ng + 2:2 * ng + 2]
        m_sc, acc_sc = rest[2 * ng + 2:2 * ng + 4]
        g_start, g_forward, g_finish = _gather_steps(g_in, g_out, rest[2 * ng + 4:])
        g, i = pl.program_id(0), pl.program_id(1)
        pl.when((g == 0) & (i == 0))(g_start)
        pl.when((g == groups - 1) & (i == 0))(g_forward)
        m_sc[...] = jnp.full(m_sc.shape, NEG_BIG, F32)
        acc_sc[...] = jnp.zeros(acc_sc.shape, F32)

        def tile(j, masked, n_tiles=1):
            n_k = n_tiles * tq
            rows = pl.ds(pl.multiple_of(j * tq, tq), n_k)
            for hh in range(hb):
                lanes = slice(hh * HEAD_PAD, (hh + 1) * HEAD_PAD)
                sc = _dot_nt(q_ref[:, lanes], k_ref[rows, lanes])
                if masked:
                    sc = jnp.where(_chunk_mask(tq, n_k, n_k - tq), sc, NEG_BIG)
                blocks = [sc[:, b * 128:(b + 1) * 128] for b in range(n_k // 128)]
                m_prev = m_sc[hh]
                m_tile = jnp.max(functools.reduce(jnp.maximum, blocks), axis=-1, keepdims=True)
                m_new = jnp.maximum(m_prev, m_tile)
                alpha = jnp.exp2((m_prev - m_new) * SCALE_LOG2E)
                p = jnp.concatenate([jnp.exp2((b - m_new) * SCALE_LOG2E) for b in blocks], axis=1).astype(MXU)
                acc_sc[hh] = alpha * acc_sc[hh] + _dot(p, v_ref[rows, lanes])
                m_sc[hh] = m_new

        def off_diagonal_pair(p, carry):
            tile(2 * p, False, n_tiles=2)
            return carry

        lax.fori_loop(0, i // 2, off_diagonal_pair, 0)

        @pl.when(i % 2 == 1)
        def _():
            tile(i - 1, True, n_tiles=2)

        @pl.when(i % 2 == 0)
        def _():
            tile(i, True)
        for hh in range(hb):
            lanes = slice(hh * HEAD_PAD, (hh + 1) * HEAD_PAD)
            acc = acc_sc[hh]
            denom = acc[:, NOPE:NOPE + 1]
            o_ref[:, lanes] = (acc / denom).astype(MXU)
            lse_ref[hh] = m_sc[hh][:, :1] * SCALE_LOG2E + jnp.log(denom) * LOG2E
        pl.when((g == groups - 1) & (i == nq - 1))(g_finish)

    q_spec = pl.BlockSpec((tq, width), lambda g, i: (i, g))
    kv_spec = pl.BlockSpec((s, width), lambda g, i: (0, g))
    any_spec = pl.BlockSpec(memory_space=pl.ANY)
    outs = _call(
        body, name="attn_fwd", grid=(groups, nq), sem=("arbitrary", "arbitrary"),
        in_specs=[q_spec, kv_spec, kv_spec] + [any_spec] * ng,
        out_specs=[q_spec, pl.BlockSpec((hb, tq, 1), lambda g, i: (g, i, 0))] + [any_spec] * ng,
        out_shape=[_sds(qp.shape, MXU), _sds((HEADS, s, 1), F32)]
        + [_sds((N_DEV,) + a.shape, a.dtype) for a in gathered],
        scratch=[pltpu.VMEM((hb, tq, HEAD_PAD), F32), pltpu.VMEM((hb, tq, HEAD_PAD), F32)] + _comm_sems(ng),
    )(qp, kp, vp, *gathered)
    return outs[0], outs[1], outs[2:]


def _out_proj_fwd(o_pad, sgu, wo, x, ada_raw, ada_b, g_post_mix, g_pre_ffn, tm):
    s = x.shape[0]
    hp = HEADS * HEAD_PAD

    def body(o_ref, sgu_ref, wo_ref, x_ref, ar_ref, ab_ref, gpm_ref, gpf_ref, m_ref, x2_ref, h2_ref):
        gt1 = _row(ar_ref, 2) + _row(ab_ref, 2)
        sh2 = _row(ar_ref, 3) + _row(ab_ref, 3)
        sc2 = _row(ar_ref, 4) + _row(ab_ref, 4)
        m = _dot(o_ref[...], wo_ref[pl.ds(0, hp), :]) + _dot(sgu_ref[...], wo_ref[pl.ds(hp, GM_WIDTH), :])
        m_ref[...] = m
        x2 = x_ref[...] + gt1 * (_rms(m)[0] * gpm_ref[...])
        x2_ref[...] = x2
        h2_ref[...] = ((_rms(x2)[0] * gpf_ref[...]) * (1.0 + sc2) + sh2).astype(MXU)

    return _call(
        body, name="out_proj_fwd", grid=(s // tm,), sem=("parallel",),
        in_specs=[_rows(tm, hp), _rows(tm, GM_WIDTH), _full(wo.shape), _rows(tm, D_MODEL), _full(ada_raw.shape),
                  _full(ada_b.shape), _full(g_post_mix.shape), _full(g_pre_ffn.shape)],
        out_specs=[_rows(tm, D_MODEL)] * 3,
        out_shape=[_sds((s, D_MODEL), F32), _sds((s, D_MODEL), F32), _sds((s, D_MODEL), MXU)],
    )(o_pad, sgu, wo, x, ada_raw, ada_b, g_post_mix, g_pre_ffn)


def _conv(u, halo, cw_ref, cb_ref):
    ext = jnp.concatenate([halo, u], axis=0)
    m1, m2 = pltpu.roll(ext, 1, 0)[8:], pltpu.roll(ext, 2, 0)[8:]
    return cb_ref[0] + ((m2 * cw_ref[0, pl.ds(0, 1), :] + m1 * cw_ref[0, pl.ds(1, 1), :]) + u * cw_ref[0, pl.ds(2, 1), :])


ROW_SUB = 256


def _sub_blocks(tm):
    return [slice(r, r + ROW_SUB) for r in range(0, tm, ROW_SUB)]


def _ffn_up_fwd(h2, w_up, conv_w, conv_b, tm):
    s = h2.shape[0]
    half = N_DEV // 2

    def body(h_ref, wa_ref, wb_ref, cwa_ref, cwb_ref, cba_ref, cbb_ref,
             ua_ref, ub_ref, ya_ref, yb_ref, act_ref, halo_a, halo_b, wa_t, wb_t):
        i = pl.program_id(1)

        @pl.when(i == 0)
        def _():
            halo_a[...] = jnp.zeros(halo_a.shape, F32)
            halo_b[...] = jnp.zeros(halo_b.shape, F32)
            wa_t[...] = wa_ref[0].T
            wb_t[...] = wb_ref[0].T

        ha, hb = halo_a[...], halo_b[...]
        for rows in _sub_blocks(tm):
            h = h_ref[rows, :]
            ua = _dot(h, wa_t[...])
            ub = _dot(h, wb_t[...])
            ua_ref[0, rows, :] = ua
            ub_ref[0, rows, :] = ub
            ya = _conv(ua, ha, cwa_ref, cba_ref)
            yb = _conv(ub, hb, cwb_ref, cbb_ref)
            ya_ref[0, rows, :] = ya
            yb_ref[0, rows, :] = yb
            ha, hb = ua[ROW_SUB - 8:], ub[ROW_SUB - 8:]
            act_ref[0, rows, :] = ((ya * jax.nn.sigmoid(ya)) * yb).astype(MXU)
        halo_a[...] = ha
        halo_b[...] = hb

    def blk(shape, off):
        return pl.BlockSpec(shape, lambda j, i: (j + off, 0, 0))

    def tok(off=0):
        return pl.BlockSpec((1, tm, FF_BLK), lambda j, i: (j + off, i, 0))

    return _call(
        body, name="ffn_up_fwd", grid=(half, s // tm), sem=("parallel", "arbitrary"),
        in_specs=[pl.BlockSpec((tm, D_MODEL), lambda j, i: (i, 0)),
                  blk((1, FF_BLK, D_MODEL), 0), blk((1, FF_BLK, D_MODEL), half),
                  blk((1, 3, FF_BLK), 0), blk((1, 3, FF_BLK), half), blk((1, 1, FF_BLK), 0), blk((1, 1, FF_BLK), half)],
        out_specs=[tok()] * 5,
        out_shape=[_sds((half, s, FF_BLK), F32)] * 4 + [_sds((half, s, FF_BLK), MXU)],
        scratch=[pltpu.VMEM((8, FF_BLK), F32), pltpu.VMEM((8, FF_BLK), F32),
                 pltpu.VMEM((D_MODEL, FF_BLK), MXU), pltpu.VMEM((D_MODEL, FF_BLK), MXU)],
    )(h2, w_up, w_up, conv_w, conv_w, conv_b, conv_b)


def _ffn_down_fwd(act, wd, x2, target, ada_raw, ada_b, g_post_ffn, tm):
    s = x2.shape[0]
    half = N_DEV // 2

    def body(act_ref, wd_ref, x2_ref, t_ref, ar_ref, ab_ref, g_ref, dout_ref, df_ref, loss_ref, dgt_ref, dg_ref):
        i = pl.program_id(0)

        @pl.when(i == 0)
        def _():
            loss_ref[...] = jnp.zeros(loss_ref.shape, F32)
            dgt_ref[...] = jnp.zeros(dgt_ref.shape, F32)
            dg_ref[...] = jnp.zeros(dg_ref.shape, F32)

        gt2 = _row(ar_ref, 5) + _row(ab_ref, 5)
        g = g_ref[...]
        for rows in _sub_blocks(tm):
            f = _dot(act_ref[0, rows, :], wd_ref[0])
            for j in range(1, half):
                f = f + _dot(act_ref[j, rows, :], wd_ref[j])
            fhat, rf = _rms(f)
            fn = fhat * g
            err = (x2_ref[rows, :] + gt2 * fn) - t_ref[rows, :]
            loss_ref[...] += 0.5 * jnp.sum(jnp.mean(err * err, axis=-1, keepdims=True))
            d_out = err * (1.0 / D_MODEL)
            dout_ref[rows, :] = d_out
            dgt_ref[...] += jnp.sum(d_out * fn, axis=0, keepdims=True)
            d_fn = d_out * gt2
            dg_ref[...] += jnp.sum(d_fn * fhat, axis=0, keepdims=True)
            df_ref[rows, :] = _rms_bwd(d_fn * g, fhat, rf).astype(MXU)

    vec = pl.BlockSpec((1, D_MODEL), lambda i: (0, 0))
    return _call(
        body, name="ffn_down_fwd", grid=(s // tm,), sem=("arbitrary",),
        in_specs=[pl.BlockSpec((half, tm, FF_BLK), lambda i: (0, i, 0)), _full(wd.shape), _rows(tm, D_MODEL),
                  _rows(tm, D_MODEL), _full(ada_raw.shape), _full(ada_b.shape), _full(g_post_ffn.shape)],
        out_specs=[_rows(tm, D_MODEL), _rows(tm, D_MODEL), pl.BlockSpec((1, 128), lambda i: (0, 0)), vec, vec],
        out_shape=[_sds((s, D_MODEL), F32), _sds((s, D_MODEL), MXU), _sds((1, 128), F32),
                   _sds((1, D_MODEL), F32), _sds((1, D_MODEL), F32)],
    )(act, wd, x2, target, ada_raw, ada_b, g_post_ffn)


def _ffn_down_bwd(d_f, wd, up_a, up_b, y_a, y_b, conv_w, act, h2, tm):
    s = d_f.shape[0]
    half = N_DEV // 2
    nt = s // tm

    def body(df_ref, wd_ref, ua_ref, ub_ref, ya_ref, yb_ref, cwa_ref, cwb_ref, act_ref, h2_ref,
             dup_ref, dcw_ref, dcb_ref, pd_ref, pu_ref, next_a, next_b, acc_d, acc_a, acc_b, wd_t):
        i = pl.program_id(1)

        @pl.when(i == 0)
        def _():
            next_a[...] = jnp.zeros(next_a.shape, F32)
            next_b[...] = jnp.zeros(next_b.shape, F32)
            dcw_ref[...] = jnp.zeros(dcw_ref.shape, F32)
            dcb_ref[...] = jnp.zeros(dcb_ref.shape, F32)
            for acc in (acc_d, acc_a, acc_b):
                acc[...] = jnp.zeros(acc.shape, F32)
            wd_t[...] = wd_ref[0].T

        def conv_bwd(d_y, u, nxt, cw_ref, part, rows):
            ext = jnp.concatenate([d_y, nxt], axis=0)
            p1 = pltpu.roll(ext, ROW_SUB + 7, 0)[:ROW_SUB]
            p2 = pltpu.roll(ext, ROW_SUB + 6, 0)[:ROW_SUB]
            d_u = (d_y * cw_ref[0, pl.ds(2, 1), :] + p1 * cw_ref[0, pl.ds(1, 1), :]) + p2 * cw_ref[0, pl.ds(0, 1), :]
            dup_ref[0, part, rows, :] = d_u.astype(MXU)
            dcb_ref[0, part] += jnp.sum(d_y, axis=0, keepdims=True)
            dcw_ref[0, part, pl.ds(0, 1), :] += jnp.sum(p2 * u, axis=0, keepdims=True)
            dcw_ref[0, part, pl.ds(1, 1), :] += jnp.sum(p1 * u, axis=0, keepdims=True)
            dcw_ref[0, part, pl.ds(2, 1), :] += jnp.sum(d_y * u, axis=0, keepdims=True)
            return d_y[:8]

        nxa, nxb = next_a[...], next_b[...]
        for rows in reversed(_sub_blocks(tm)):
            d_act = _dot(df_ref[rows, :], wd_t[...])
            ya, yb = ya_ref[0, rows, :], yb_ref[0, rows, :]
            sig = jax.nn.sigmoid(ya)
            d_ya = d_act * yb * (sig * (1.0 + ya * (1.0 - sig)))
            d_yb = d_act * (ya * sig)
            nxa = conv_bwd(d_ya, ua_ref[0, rows, :], nxa, cwa_ref, 0, rows)
            nxb = conv_bwd(d_yb, ub_ref[0, rows, :], nxb, cwb_ref, 1, rows)
        next_a[...] = nxa
        next_b[...] = nxb
        acc_d[...] += _dot_tn(act_ref[0], df_ref[...])
        acc_a[...] += _dot_tn(dup_ref[0, 0], h2_ref[...])
        acc_b[...] += _dot_tn(dup_ref[0, 1], h2_ref[...])

        @pl.when(i == nt - 1)
        def _():
            pd_ref[0] = acc_d[...].astype(MXU)
            pu_ref[0, 0] = acc_a[...].astype(MXU)
            pu_ref[0, 1] = acc_b[...].astype(MXU)

    def rev(i):
        return nt - 1 - i

    def blk(shape, off):
        return pl.BlockSpec(shape, lambda j, i: (j + off, 0, 0))

    tok = pl.BlockSpec((1, tm, FF_BLK), lambda j, i: (j, rev(i), 0))
    acc3 = pl.BlockSpec((1, 2, 3, FF_BLK), lambda j, i: (j, 0, 0, 0))
    acc1 = pl.BlockSpec((1, 2, 1, FF_BLK), lambda j, i: (j, 0, 0, 0))
    return _call(
        body, name="ffn_down_bwd", grid=(half, nt), sem=("parallel", "arbitrary"),
        in_specs=[pl.BlockSpec((tm, D_MODEL), lambda j, i: (rev(i), 0)), blk((1, FF_BLK, D_MODEL), 0),
                  tok, tok, tok, tok, blk((1, 3, FF_BLK), 0), blk((1, 3, FF_BLK), half),
                  tok, pl.BlockSpec((tm, D_MODEL), lambda j, i: (rev(i), 0))],
        out_specs=[pl.BlockSpec((1, 2, tm, FF_BLK), lambda j, i: (j, 0, rev(i), 0)), acc3, acc1,
                   pl.BlockSpec((1, FF_BLK, D_MODEL), lambda j, i: (j, 0, 0)),
                   pl.BlockSpec((1, 2, FF_BLK, D_MODEL), lambda j, i: (j, 0, 0, 0))],
        out_shape=[_sds((half, 2, s, FF_BLK), MXU), _sds((half, 2, 3, FF_BLK), F32), _sds((half, 2, 1, FF_BLK), F32),
                   _sds((half, FF_BLK, D_MODEL), MXU), _sds((half, 2, FF_BLK, D_MODEL), MXU)],
        scratch=[pltpu.VMEM((8, FF_BLK), F32), pltpu.VMEM((8, FF_BLK), F32)]
        + [pltpu.VMEM((FF_BLK, D_MODEL), F32)] * 3 + [pltpu.VMEM((D_MODEL, FF_BLK), MXU)],
    )(d_f, wd, up_a, up_b, y_a, y_b, conv_w, conv_w, act, h2)


def _ffn_up_bwd(d_up, w_up, x2, m, d_out, ada_raw, ada_b, g_pre_ffn, g_post_mix, tm):
    s = x2.shape[0]
    half = N_DEV // 2

    def body(dup_ref, w_ref, x2_ref, m_ref, dout_ref, ar_ref, ab_ref, gpf_ref, gpm_ref,
             dx_ref, dm_ref, dsh_ref, dsc_ref, dgpf_ref, dgt1_ref, dgpm_ref):
        i = pl.program_id(0)

        @pl.when(i == 0)
        def _():
            for r in (dsh_ref, dsc_ref, dgpf_ref, dgt1_ref, dgpm_ref):
                r[...] = jnp.zeros(r.shape, F32)

        gt1 = _row(ar_ref, 2) + _row(ab_ref, 2)
        sc2 = _row(ar_ref, 4) + _row(ab_ref, 4)
        gpf, gpm = gpf_ref[...], gpm_ref[...]
        d_h2 = _dot(dup_ref[0, 0], w_ref[0])
        for j in range(1, half):
            d_h2 = d_h2 + _dot(dup_ref[j, 0], w_ref[j])
        for j in range(half):
            d_h2 = d_h2 + _dot(dup_ref[j, 1], w_ref[half + j])
        x2n, r2 = _rms(x2_ref[...])
        dsh_ref[...] += jnp.sum(d_h2, axis=0, keepdims=True)
        dsc_ref[...] += jnp.sum(d_h2 * (x2n * gpf), axis=0, keepdims=True)
        d_mod = d_h2 * (1.0 + sc2)
        dgpf_ref[...] += jnp.sum(d_mod * x2n, axis=0, keepdims=True)
        d_x2 = dout_ref[...] + _rms_bwd(d_mod * gpf, x2n, r2)
        dx_ref[...] = d_x2
        mhat, rm = _rms(m_ref[...])
        dgt1_ref[...] += jnp.sum(d_x2 * (mhat * gpm), axis=0, keepdims=True)
        d_mn = d_x2 * gt1
        dgpm_ref[...] += jnp.sum(d_mn * mhat, axis=0, keepdims=True)
        dm_ref[...] = _rms_bwd(d_mn * gpm, mhat, rm).astype(MXU)

    vec = pl.BlockSpec((1, D_MODEL), lambda i: (0, 0))
    tok = pl.BlockSpec((half, 2, tm, FF_BLK), lambda i: (0, 0, i, 0))
    return _call(
        body, name="ffn_up_bwd", grid=(s // tm,), sem=("arbitrary",),
        in_specs=[tok, _full(w_up.shape), _rows(tm, D_MODEL), _rows(tm, D_MODEL), _rows(tm, D_MODEL),
                  _full(ada_raw.shape), _full(ada_b.shape), _full(g_pre_ffn.shape), _full(g_post_mix.shape)],
        out_specs=[_rows(tm, D_MODEL), _rows(tm, D_MODEL), vec, vec, vec, vec, vec],
        out_shape=[_sds((s, D_MODEL), F32), _sds((s, D_MODEL), MXU)] + [_sds((1, D_MODEL), F32)] * 5,
    )(d_up, w_up, x2, m, d_out, ada_raw, ada_b, g_pre_ffn, g_post_mix)


def _out_proj_bwd(d_m, wo, o_pad, tm):
    s = d_m.shape[0]
    hp = HEADS * HEAD_PAD

    def body(dm_ref, wo_ref, o_ref, do_ref, dsgu_ref, delta_ref, wo_t):
        @pl.when(pl.program_id(0) == 0)
        def _():
            wo_t[...] = wo_ref[...].T

        d_cat = _dot(dm_ref[...], wo_t[...])
        d_o = d_cat[:, :hp]
        do_ref[...] = d_o.astype(MXU)
        dsgu_ref[...] = d_cat[:, hp:]
        prod = d_o * o_ref[...].astype(F32)
        for h in range(HEADS):
            delta_ref[h] = jnp.sum(prod[:, h * HEAD_PAD:(h + 1) * HEAD_PAD], axis=-1, keepdims=True)

    return _call(
        body, name="out_proj_bwd", grid=(s // tm,), sem=("arbitrary",),
        in_specs=[_rows(tm, D_MODEL), _full(wo.shape), _rows(tm, hp)],
        out_specs=[_rows(tm, hp), _rows(tm, GM_WIDTH), pl.BlockSpec((HEADS, tm, 1), lambda i: (0, i, 0))],
        out_shape=[_sds((s, hp), MXU), _sds((s, GM_WIDTH), F32), _sds((HEADS, s, 1), F32)],
        scratch=[pltpu.VMEM(wo.shape[::-1], MXU)],
    )(d_m, wo, o_pad)


def _attn_bwd(qp, kp, vp, d_o, lse, delta, tq, scattered, gathered):
    s = qp.shape[0]
    nq = s // tq
    hb = ATTN_HEADS_PER_STEP
    groups = HEADS // hb
    width = hb * HEAD_PAD
    ns, ng = len(scattered), len(gathered)
    nc = ns + ng
    slots = [slot for _, slot in scattered]

    def body(q_ref, k_ref, v_ref, do_ref, lse_ref, dl_ref, *rest):
        c_in, (dq_ref, dk_ref, dv_ref), c_out = rest[:nc], rest[nc:nc + 3], rest[nc + 3:2 * nc + 3]
        dk_sc, dv_sc = rest[2 * nc + 3:2 * nc + 5]
        sems = rest[2 * nc + 5:]
        s_start, s_finish = _scatter_steps(c_in[:ns], c_out[:ns], sems[:3], slots)
        g_start, g_forward, g_finish = _gather_steps(c_in[ns:], c_out[ns:], sems[3:])
        g, j = pl.program_id(0), pl.program_id(1)

        @pl.when((g == 0) & (j == 0))
        def _():
            s_start()
            g_start()

        pl.when((g == groups - 1) & (j == 0))(g_forward)

        @pl.when(j == 0)
        def _():
            dq_ref[...] = jnp.zeros(dq_ref.shape, F32)

        dk_sc[...] = jnp.zeros(dk_sc.shape, F32)
        dv_sc[...] = jnp.zeros(dv_sc.shape, F32)

        def tile(i, masked):
            rows = pl.ds(pl.multiple_of(i * tq, tq), tq)
            for hh in range(hb):
                lanes = slice(hh * HEAD_PAD, (hh + 1) * HEAD_PAD)
                q, do, k = q_ref[rows, lanes], do_ref[rows, lanes], k_ref[:, lanes]
                sc = _dot_nt(q, k)
                if masked:
                    sc = jnp.where(_chunk_mask(tq, tq, 0), sc, NEG_BIG)
                p = jnp.exp2(sc * SCALE_LOG2E - lse_ref[hh, rows, :])
                dv_sc[hh] += _dot_tn(p.astype(MXU), do)
                dp = _dot_nt(do, v_ref[:, lanes])
                ds = (p * (dp - dl_ref[hh, rows, :])).astype(MXU)
                dk_sc[hh] += _dot_tn(ds, q)
                dq_ref[rows, lanes] += _dot(ds, k) * ATTN_SCALE

        def tile_pair(i0, first_masked):
            both = pl.ds(pl.multiple_of(i0 * tq, tq), 2 * tq)
            for hh in range(hb):
                lanes = slice(hh * HEAD_PAD, (hh + 1) * HEAD_PAD)
                k, v = k_ref[:, lanes], v_ref[:, lanes]
                ps, dss = [], []
                for t in range(2):
                    rows = pl.ds(pl.multiple_of((i0 + t) * tq, tq), tq)
                    q, do = q_ref[rows, lanes], do_ref[rows, lanes]
                    sc = _dot_nt(q, k)
                    if first_masked and t == 0:
                        sc = jnp.where(_chunk_mask(tq, tq, 0), sc, NEG_BIG)
                    p = jnp.exp2(sc * SCALE_LOG2E - lse_ref[hh, rows, :])
                    ds = (p * (_dot_nt(do, v) - dl_ref[hh, rows, :])).astype(MXU)
                    dq_ref[rows, lanes] += _dot(ds, k) * ATTN_SCALE
                    ps.append(p.astype(MXU))
                    dss.append(ds)
                dv_sc[hh] += _dot_tn(jnp.concatenate(ps, axis=0), do_ref[both, lanes])
                dk_sc[hh] += _dot_tn(jnp.concatenate(dss, axis=0), q_ref[both, lanes])

        odd = (nq - j) % 2

        @pl.when(odd == 1)
        def _():
            tile(j, True)

        @pl.when(odd == 0)
        def _():
            tile_pair(j, True)

        first = j + 2 - odd

        def later_pair(pair, carry):
            tile_pair(first + 2 * pair, False)
            return carry

        lax.fori_loop(0, (nq - first) // 2, later_pair, 0)
        for hh in range(hb):
            lanes = slice(hh * HEAD_PAD, (hh + 1) * HEAD_PAD)
            dk_ref[:, lanes] = dk_sc[hh] * ATTN_SCALE
            dv_ref[:, lanes] = dv_sc[hh]
        @pl.when((g == groups - 1) & (j == nq - 1))
        def _():
            g_finish()
            s_finish()

    seq_spec = pl.BlockSpec((s, width), lambda g, j: (0, g))
    kv_spec = pl.BlockSpec((tq, width), lambda g, j: (j, g))
    col_spec = pl.BlockSpec((hb, s, 1), lambda g, j: (g, 0, 0))
    any_spec = pl.BlockSpec(memory_space=pl.ANY)
    outs = _call(
        body, name="attn_bwd", grid=(groups, nq), sem=("arbitrary", "arbitrary"),
        in_specs=[seq_spec, kv_spec, kv_spec, seq_spec, col_spec, col_spec] + [any_spec] * nc,
        out_specs=[seq_spec, kv_spec, kv_spec] + [any_spec] * nc,
        out_shape=[_sds(qp.shape, F32), _sds(qp.shape, F32), _sds(qp.shape, F32)]
        + [_scatter_out_shape(a, slot) for a, slot in scattered]
        + [_sds((N_DEV,) + a.shape, a.dtype) for a in gathered],
        scratch=[pltpu.VMEM((hb, tq, HEAD_PAD), F32), pltpu.VMEM((hb, tq, HEAD_PAD), F32)]
        + _comm_sems(ns) + _comm_sems(ng),
    )(qp, kp, vp, d_o, lse, delta, *[a for a, _ in scattered], *gathered)
    return outs[0], outs[1], outs[2], outs[3:3 + ns], outs[3 + ns:]


def _gmlp_bwd(z, d_sgu, ln_g, ln_b, w_sp, bias_exp, tm):
    s = z.shape[0]
    nblk = tm // GM_CHUNK

    def body(zu_ref, zv_ref, dsgu_ref, lg_ref, lb_ref, w_ref, be_ref,
             dguv_ref, dws_ref, dbs_ref, dlg_ref, dlb_ref, dbe_sc, dvln_sc, dlg_sc, dlb_sc):
        i = pl.program_id(0)

        @pl.when(i == 0)
        def _():
            for r in (dws_ref, dlg_sc, dlb_sc, dbe_sc):
                r[...] = jnp.zeros(r.shape, F32)

        seg = _seg_matrix()
        mask = _spatial_mask()
        wm = [(w_ref[h] * mask).astype(MXU) for h in range(HEADS)]
        zu, zv = zu_ref[...], zv_ref[...]
        gu = _gelu(zu)
        _, vhat, rstd = _gm_norm(zv, seg)
        lg = lg_ref[...]
        vln = (vhat * lg + lb_ref[...]).astype(MXU)
        d_sgu = dsgu_ref[...]
        for n in range(nblk):
            rows = slice(n * GM_CHUNK, (n + 1) * GM_CHUNK)
            vb = vln[rows]
            mixed = _gm_mix(wm, vb, GM_CHUNK) + be_ref[...]
            d_mixed = d_sgu[rows] * gu[rows]
            dguv_ref[rows, pl.ds(0, GM_WIDTH)] = ((d_sgu[rows] * mixed) * _gelu_grad(zu[rows])).astype(MXU)
            dbe_sc[...] += d_mixed
            dmb = d_mixed.astype(MXU)
            for p, (lanes, first) in enumerate(_gm_pairs(GM_CHUNK)):
                dm_pair, v_pair = dmb[:, lanes], vb[:, lanes]
                zero = jnp.zeros_like(dm_pair)
                dws_ref[2 * p] += _dot_nt(jnp.where(first, dm_pair, zero), v_pair)
                dws_ref[2 * p + 1] += _dot_nt(jnp.where(first, zero, dm_pair), v_pair)
            dvln_sc[rows, :] = _gm_mix(wm, dmb, GM_CHUNK, transposed=True)
        d_vln = dvln_sc[...]
        dlg_sc[...] += jnp.sum(d_vln * vhat, axis=0, keepdims=True)
        dlb_sc[...] += jnp.sum(d_vln, axis=0, keepdims=True)
        d_vhat = d_vln * lg
        d_gv = rstd * ((d_vhat - _split_dot(d_vhat, seg)) - vhat * _split_dot(d_vhat * vhat, seg))
        dguv_ref[:, pl.ds(GM_WIDTH, GM_WIDTH)] = (d_gv * _gelu_grad(zv)).astype(MXU)

        @pl.when(i == pl.num_programs(0) - 1)
        def _():
            for h in range(HEADS):
                dws_ref[h] = dws_ref[h] * mask
            hrow = lax.broadcasted_iota(jnp.int32, (HEADS, GM_WIDTH), 0)
            hlane = lax.broadcasted_iota(jnp.int32, (HEADS, GM_WIDTH), 1) >> 6
            ind = jnp.where(hrow == hlane, 1.0, 0.0).astype(MXU)
            acc = dbe_sc[...]
            hi = acc.astype(MXU)
            lo = (acc - hi.astype(F32)).astype(MXU)
            dbs_ref[...] = _dot_nt(ind, hi) + _dot_nt(ind, lo)
            pick = (lax.broadcasted_iota(jnp.int32, (GM_WIDTH, GM_DIM), 0) & (GM_DIM - 1)
                    == lax.broadcasted_iota(jnp.int32, (GM_WIDTH, GM_DIM), 1))
            pick = jnp.where(pick, 1.0, 0.0).astype(MXU)
            for src, dst in ((dlg_sc, dlg_ref), (dlb_sc, dlb_ref)):
                spread = jnp.where(hrow == hlane, jnp.broadcast_to(src[...], (HEADS, GM_WIDTH)), 0.0)
                dst[...] = _split_dot3(spread, pick)

    return _call(
        body, name="gmlp_bwd", grid=(s // tm,), sem=("arbitrary",),
        in_specs=[_rows(tm, GM_WIDTH, 1), _rows(tm, GM_WIDTH, 2), _rows(tm, GM_WIDTH), _full(ln_g.shape),
                  _full(ln_b.shape), _full(w_sp.shape), _full(bias_exp.shape)],
        out_specs=[_rows(tm, 2 * GM_WIDTH), _full(w_sp.shape), _full((HEADS, GM_CHUNK)), _full((HEADS, GM_DIM)),
                   _full((HEADS, GM_DIM))],
        out_shape=[_sds((s, 2 * GM_WIDTH), MXU), _sds(w_sp.shape, F32), _sds((HEADS, GM_CHUNK), F32),
                   _sds((HEADS, GM_DIM), F32), _sds((HEADS, GM_DIM), F32)],
        scratch=[pltpu.VMEM((GM_CHUNK, GM_WIDTH), F32), pltpu.VMEM((tm, GM_WIDTH), F32),
                 pltpu.VMEM((1, GM_WIDTH), F32), pltpu.VMEM((1, GM_WIDTH), F32)],
    )(z, z, d_sgu, ln_g, ln_b, w_sp, bias_exp)


def _mix_in_bwd(dq, dk, dv, z, d_guv, x, d_x_part, ada_raw, ada_b, g_pre, g_q, g_kv, w1t, wqt, wkv,
                cos_t, sin_t, tm):
    s = x.shape[0]
    hp = HEADS * HEAD_PAD
    za = Q_LORA + KV_LORA + HEAD_PAD

    def body(dq_ref, dk_ref, dv_ref, z_ref, dguv_ref, x_ref, dxp_ref, ar_ref, ab_ref, g_ref, gq_ref, gkv_ref,
             w1_ref, wq_ref, wkv_ref, cos_ref, sin_ref,
             gx_ref, dza_ref, dqp_ref, dkvp_ref, dsh_ref, dsc_ref, dg_ref, dgq_ref, dgkv_ref):
        i = pl.program_id(0)

        @pl.when(i == 0)
        def _():
            for r in (dsh_ref, dsc_ref, dg_ref, dgq_ref, dgkv_ref):
                r[...] = jnp.zeros(r.shape, F32)

        cos, sin = cos_ref[...], sin_ref[...]
        d_krot = jnp.zeros((tm, HEAD_PAD), F32)
        for h in range(HEADS):
            blk = slice(h * HEAD_PAD, (h + 1) * HEAD_PAD)
            dqp_ref[:, blk] = _rope_transposed(dq_ref[:, blk], cos, sin).astype(MXU)
            dk_h = dk_ref[:, blk]
            d_krot = d_krot + dk_h
            dkvp_ref[:, blk] = dk_h.astype(MXU)
        dkvp_ref[:, pl.ds(hp, hp)] = dv_ref[...].astype(MXU)
        lane = lax.broadcasted_iota(jnp.int32, (tm, HEAD_PAD), 1)
        d_kr = jnp.where((lane >= NOPE) & (lane < NOPE + ROPE), _rope_transposed(d_krot, cos, sin), 0.0)
        d_cqn = _dot(dqp_ref[...], wq_ref[...])
        d_ckvn = _dot_nt(dkvp_ref[...], wkv_ref[...])
        zt = z_ref[...]
        gq, gkv = gq_ref[...], gkv_ref[...]
        cq_hat, rq = _rms(zt[:, :Q_LORA])
        ckv_hat, rkv = _rms(zt[:, Q_LORA:Q_LORA + KV_LORA])
        dgq_ref[...] += jnp.sum(d_cqn * cq_hat, axis=0, keepdims=True)
        dgkv_ref[...] += jnp.sum(d_ckvn * ckv_hat, axis=0, keepdims=True)
        d_cq = _rms_bwd(d_cqn * gq, cq_hat, rq)
        d_ckv = _rms_bwd(d_ckvn * gkv, ckv_hat, rkv)
        d_za = jnp.concatenate([d_cq, d_ckv, d_kr], axis=1).astype(MXU)
        dza_ref[...] = d_za
        d_h1 = _dot(d_za, w1_ref[pl.ds(0, za), :]) + _dot(dguv_ref[...], w1_ref[pl.ds(za, 2 * GM_WIDTH), :])
        sc1 = _row(ar_ref, 1) + _row(ab_ref, 1)
        g = g_ref[...]
        xn, r1 = _rms(x_ref[...])
        dsh_ref[...] += jnp.sum(d_h1, axis=0, keepdims=True)
        dsc_ref[...] += jnp.sum(d_h1 * (xn * g), axis=0, keepdims=True)
        d_mod = d_h1 * (1.0 + sc1)
        dg_ref[...] += jnp.sum(d_mod * xn, axis=0, keepdims=True)
        gx_ref[...] = dxp_ref[...] + _rms_bwd(d_mod * g, xn, r1)

    vec = pl.BlockSpec((1, D_MODEL), lambda i: (0, 0))
    return _call(
        body, name="mix_in_bwd", grid=(s // tm,), sem=("arbitrary",),
        in_specs=[_rows(tm, hp), _rows(tm, hp), _rows(tm, hp), _rows(tm, za), _rows(tm, 2 * GM_WIDTH),
                  _rows(tm, D_MODEL), _rows(tm, D_MODEL), _full(ada_raw.shape), _full(ada_b.shape), _full(g_pre.shape),
                  _full(g_q.shape), _full(g_kv.shape), _full(w1t.shape), _full(wqt.shape),
                  _full(wkv.shape), _rows(tm, HEAD_PAD), _rows(tm, HEAD_PAD)],
        out_specs=[_rows(tm, D_MODEL), _rows(tm, za), _rows(tm, hp), _rows(tm, 2 * hp), vec, vec, vec,
                   _full(g_q.shape), _full(g_kv.shape)],
        out_shape=[_sds((s, D_MODEL), F32), _sds((s, za), MXU), _sds((s, hp), MXU), _sds((s, 2 * hp), MXU),
                   _sds((1, D_MODEL), F32), _sds((1, D_MODEL), F32), _sds((1, D_MODEL), F32),
                   _sds(g_q.shape, F32), _sds(g_kv.shape, F32)],
    )(dq, dk, dv, z, d_guv, x, d_x_part, ada_raw, ada_b, g_pre, g_q, g_kv, w1t, wqt, wkv, cos_t, sin_t)


def _tn_matmuls(arrays, pairs, name, ts):
    s = arrays[0].shape[0]
    steps = s // ts
    n_in, n_out = len(arrays), len(pairs)
    shapes = [(arrays[ia].shape[1], arrays[ib].shape[1]) for ia, ib in pairs]

    def body(*refs):
        ins, outs, accs = refs[:n_in], refs[n_in:n_in + n_out], refs[n_in + n_out:]
        k = pl.program_id(0)

        @pl.when(k == 0)
        def _():
            for acc in accs:
                acc[...] = jnp.zeros(acc.shape, F32)

        for (ia, ib), acc in zip(pairs, accs):
            acc[...] += _dot_tn(ins[ia][...], ins[ib][...])

        @pl.when(k == steps - 1)
        def _():
            for out, acc in zip(outs, accs):
                out[...] = acc[...].astype(MXU)

    return _call(
        body, name=name, grid=(steps,), sem=("arbitrary",),
        in_specs=[_rows(ts, a.shape[1]) for a in arrays],
        out_specs=[_full(shape) for shape in shapes],
        out_shape=[_sds(shape, MXU) for shape in shapes],
        scratch=[pltpu.VMEM(shape, F32) for shape in shapes],
    )(*arrays)


def _adamw(w, g, m, v):
    m2 = ADAM_B1 * m + (1.0 - ADAM_B1) * g
    v2 = ADAM_B2 * v + (1.0 - ADAM_B2) * (g * g)
    m_hat = m2 / (1.0 - ADAM_B1 ** ADAM_STEP)
    v_hat = v2 / (1.0 - ADAM_B2 ** ADAM_STEP)
    delta = -ADAM_LR * (m_hat / (jnp.sqrt(v_hat) + ADAM_EPS) + ADAM_WD * w)
    return delta, m2, v2


def _adam_reduce(recv, w, m, v, name):
    r, c = w.shape
    tr = r if r <= 512 else max(t for t in range(16, 513, 16) if r % t == 0)

    def body(p_ref, w_ref, m_ref, v_ref, g_ref, d_ref, mo_ref, vo_ref):
        g = p_ref[0].astype(F32)
        for j in range(1, N_DEV):
            g = g + p_ref[j].astype(F32)
        g_ref[...] = g
        d_ref[...], mo_ref[...], vo_ref[...] = _adamw(w_ref[...], g, m_ref[...], v_ref[...])

    blk = pl.BlockSpec((tr, c), lambda i: (i, 0))
    return _call(
        body, name=name, grid=(r // tr,), sem=("parallel",),
        in_specs=[pl.BlockSpec((N_DEV, tr, c), lambda i: (0, i, 0)), blk, blk, blk],
        out_specs=[blk] * 4, out_shape=[_sds((r, c), F32)] * 4,
    )(recv, w, m, v)


def _adam_w_ada(c_act_t, d_ada_cols, w, m, v):
    r, c = w.shape
    tr = 256

    def body(ct_ref, da_ref, w_ref, m_ref, v_ref, g_ref, d_ref, mo_ref, vo_ref):
        g = _dot(ct_ref[...], da_ref[...])
        g_ref[...] = g
        d_ref[...], mo_ref[...], vo_ref[...] = _adamw(w_ref[...], g, m_ref[...], v_ref[...])

    blk = pl.BlockSpec((tr, c), lambda i: (i, 0))
    return _call(
        body, name="adam_w_ada", grid=(r // tr,), sem=("parallel",),
        in_specs=[pl.BlockSpec((tr, c_act_t.shape[1]), lambda i: (i, 0)), _full(d_ada_cols.shape), blk, blk, blk],
        out_specs=[blk] * 4, out_shape=[_sds((r, c), F32)] * 4,
    )(c_act_t, d_ada_cols, w, m, v)


VEC_ROWS = D_MODEL // 128
PK_ADA = 0
PK_GAIN = PK_ADA + 6 * VEC_ROWS
PK_GQ = PK_GAIN + 4 * VEC_ROWS
PK_GKV = PK_GQ + Q_LORA // 128
PK_LOSS = PK_GKV + KV_LORA // 128
PK_LNG = 88
PK_LNB = PK_LNG + HEADS
PK_BS = PK_LNB + HEADS
PK_CB = PK_BS + HEADS
CB_ROWS = 6
PK_WS = PK_CB + N_DEV * CB_ROWS
PK_ROWS = PK_WS + HEADS * GM_CHUNK
assert PK_LOSS < PK_LNG and PK_ROWS % 8 == 0
LATE_GAIN = 2 * VEC_ROWS
LATE_GQ = 3 * VEC_ROWS
LATE_GKV = LATE_GQ + Q_LORA // 128
LATE_ROWS = 32


def _cb_chunks():
    return [(k, k * 128, min(128, FF_BLK - k * 128)) for k in range(CB_ROWS)]


def _put_rows(out_ref, row0, ref, width):
    for k in range(width // 128):
        out_ref[pl.ds(row0 + k, 1), :] = ref[:, pl.ds(k * 128, 128)]


def _pack_small(ada_rows, gains, loss_part, d_ln_g, d_ln_b, d_bs, d_cb, d_ws):
    half = N_DEV // 2

    def body(*refs):
        vec_refs = refs[:7]
        loss_ref, lng_ref, lnb_ref, bs_ref, cb_ref, ws_ref, out_ref = refs[7:]
        out_ref[pl.ds(0, PK_WS), :] = jnp.zeros((PK_WS, 128), F32)
        for n, ref in enumerate(vec_refs[:4]):
            _put_rows(out_ref, PK_ADA + (2 + n) * VEC_ROWS, ref, D_MODEL)
        for n, ref in enumerate(vec_refs[4:]):
            _put_rows(out_ref, PK_GAIN + (1 + n) * VEC_ROWS, ref, D_MODEL)
        _put_rows(out_ref, PK_LOSS, loss_ref, 128)
        out_ref[pl.ds(PK_LNG, HEADS), pl.ds(0, GM_DIM)] = lng_ref[...]
        out_ref[pl.ds(PK_LNB, HEADS), pl.ds(0, GM_DIM)] = lnb_ref[...]
        out_ref[pl.ds(PK_BS, HEADS), :] = bs_ref[...]
        for j in range(N_DEV):
            for k, lane, width in _cb_chunks():
                out_ref[pl.ds(PK_CB + j * CB_ROWS + k, 1), pl.ds(0, width)] = cb_ref[j % half, j // half, :, pl.ds(lane, width)]
        for h in range(HEADS):
            out_ref[pl.ds(PK_WS + h * GM_CHUNK, GM_CHUNK), :] = ws_ref[h]

    ins = list(ada_rows) + list(gains) + [loss_part, d_ln_g, d_ln_b, d_bs, d_cb, d_ws]
    return _call(body, name="pack_small", grid=(1,), in_specs=[_full(a.shape) for a in ins],
                 out_specs=_full((PK_ROWS, 128)), out_shape=_sds((PK_ROWS, 128), F32))(*ins)


def _pack_late(d_sh1, d_sc1, d_g_pre_mix, d_g_q, d_g_kv):
    def body(sh_ref, sc_ref, g_ref, gq_ref, gkv_ref, out_ref):
        out_ref[...] = jnp.zeros((LATE_ROWS, 128), F32)
        _put_rows(out_ref, 0, sh_ref, D_MODEL)
        _put_rows(out_ref, VEC_ROWS, sc_ref, D_MODEL)
        _put_rows(out_ref, LATE_GAIN, g_ref, D_MODEL)
        _put_rows(out_ref, LATE_GQ, gq_ref, Q_LORA)
        _put_rows(out_ref, LATE_GKV, gkv_ref, KV_LORA)

    ins = [d_sh1, d_sc1, d_g_pre_mix, d_g_q, d_g_kv]
    return _call(body, name="pack_late", grid=(1,), in_specs=[_full(a.shape) for a in ins],
                 out_specs=_full((LATE_ROWS, 128)), out_shape=_sds((LATE_ROWS, 128), F32))(*ins)


def _adam_small(gathered, late, params):
    n_par = len(params)

    def body(p_ref, late_ref, *refs):
        ins = [refs[3 * n:3 * n + 3] for n in range(n_par)]
        outs = [refs[3 * n_par + 4 * n:3 * n_par + 4 * n + 4] for n in range(n_par)]
        loss_ref, dada_ref = refs[7 * n_par:]

        def total(rows, lanes=slice(None), src=p_ref):
            g = src[0, rows, lanes]
            for j in range(1, N_DEV):
                g = g + src[j, rows, lanes]
            return g

        def apply(n, g, idx):
            w_ref, m_ref, v_ref = ins[n]
            d, m2, v2 = _adamw(w_ref[idx], g, m_ref[idx], v_ref[idx])
            for ref, val in zip(outs[n], (g, d, m2, v2)):
                ref[idx] = val

        def vector(n, src, row0, width, lane0=0):
            for k in range(width // 128):
                apply(n, total(pl.ds(row0 + k, 1), src=src), (slice(None), pl.ds(lane0 + k * 128, 128)))

        vector(0, late_ref, 0, 2 * D_MODEL)
        vector(0, p_ref, PK_ADA + 2 * VEC_ROWS, 4 * D_MODEL, lane0=2 * D_MODEL)
        vector(1, late_ref, LATE_GAIN, D_MODEL)
        for n in range(1, 4):
            vector(1 + n, p_ref, PK_GAIN + n * VEC_ROWS, D_MODEL)
        vector(5, late_ref, LATE_GQ, Q_LORA)
        vector(6, late_ref, LATE_GKV, KV_LORA)
        apply(7, total(pl.ds(PK_LNG, HEADS), pl.ds(0, GM_DIM)), (0,))
        apply(8, total(pl.ds(PK_LNB, HEADS), pl.ds(0, GM_DIM)), (0,))
        for h in range(HEADS):
            apply(9, total(pl.ds(PK_WS + h * GM_CHUNK, GM_CHUNK)), (0, h))
        apply(10, total(pl.ds(PK_BS, HEADS)), (0,))
        for j in range(N_DEV):
            for k, lane, width in _cb_chunks():
                apply(11, total(pl.ds(PK_CB + j * CB_ROWS + k, 1), pl.ds(0, width)), (pl.ds(j, 1), pl.ds(lane, width)))
        loss_ref[...] = total(pl.ds(PK_LOSS, 1))
        dada_ref[:, pl.ds(0, 2 * VEC_ROWS), :] = late_ref[:, pl.ds(0, 2 * VEC_ROWS), :]
        dada_ref[:, pl.ds(2 * VEC_ROWS, 4 * VEC_ROWS), :] = p_ref[:, pl.ds(PK_ADA + 2 * VEC_ROWS, 4 * VEC_ROWS), :]

    flat = [a for triple in params for a in triple]
    out_shape = [_sds(w.shape, F32) for w, _, _ in params for _ in range(4)]
    out_shape += [_sds((1, 128), F32), _sds((N_DEV, 6 * VEC_ROWS, 128), F32)]
    outs = _call(body, name="adam_small", grid=(1,),
                 in_specs=[_full(gathered.shape), _full(late.shape)] + [_full(a.shape) for a in flat],
                 out_specs=[_full(o.shape) for o in out_shape], out_shape=out_shape)(gathered, late, *flat)
    return [tuple(outs[4 * n:4 * n + 4]) for n in range(n_par)], outs[-2], outs[-1]


def _rope_tables(s):
    pos = jnp.arange(s, dtype=F32)
    inv = ROPE_THETA ** (-jnp.arange(0, ROPE, 2, dtype=F32) / ROPE)
    lane_inv = jnp.concatenate([jnp.zeros((NOPE,), F32), inv, inv, jnp.zeros((HEAD_PAD - NOPE - ROPE,), F32)])
    ang = pos[:, None] * lane_inv[None, :]
    return jnp.cos(ang), jnp.sin(ang)


def kernel(x, c, w_ada, b_ada, g_pre_mix, g_post_mix, w_in, g_q, w_uq, g_kv, w_ukv, gm_ln_g, gm_ln_b, w_spatial, b_spatial, w_out, g_pre_ffn, g_post_ffn, w_up, conv_w, conv_b, w_down, loss_target, m_w_ada, m_b_ada, m_g_pre_mix, m_g_post_mix, m_w_in, m_g_q, m_w_uq, m_g_kv, m_w_ukv, m_gm_ln_g, m_gm_ln_b, m_w_spatial, m_b_spatial, m_w_out, m_g_pre_ffn, m_g_post_ffn, m_w_up, m_conv_w, m_conv_b, m_w_down, v_w_ada, v_b_ada, v_g_pre_mix, v_g_post_mix, v_w_in, v_g_q, v_w_uq, v_g_kv, v_w_ukv, v_gm_ln_g, v_gm_ln_b, v_w_spatial, v_b_spatial, v_w_out, v_g_pre_ffn, v_g_post_ffn, v_w_up, v_conv_w, v_conv_b, v_w_down):
    s = x.shape[1]
    tm = min(512, s)
    tf = min(2 * ROW_SUB, s)
    tq = min(512, s)
    ts = min(2048, s)
    hp = HEADS * HEAD_PAD
    half = N_DEV // 2
    my_slot = 4 * lax.axis_index("x") + 2 * lax.axis_index("y") + lax.axis_index("c")
    x2d, target = x[0], loss_target[0]

    def t_(a):
        return jnp.swapaxes(a[0], 0, 1)

    w_in_t, m_in_t, v_in_t = t_(w_in), t_(m_w_in), t_(v_w_in)
    w_uq_t, m_uq_t, v_uq_t = t_(w_uq), t_(m_w_uq), t_(v_w_uq)
    w_up_t, m_up_t, v_up_t = t_(w_up), t_(m_w_up), t_(v_w_up)
    (g_c, g_in_t, g_uq_t, g_ukv, g_cw), _ = _exchange(
        [c, w_in_t.astype(MXU), w_uq_t.astype(MXU), w_ukv[0].astype(MXU), conv_w[0]], [], "gather_mixer_weights")

    w_in_f = g_in_t.reshape(-1, D_MODEL)
    o1, o2, o3 = Q_LORA, Q_LORA + KV_LORA, Q_LORA + KV_LORA + ROPE
    w1t = jnp.concatenate([w_in_f[:o2], jnp.zeros((NOPE, D_MODEL), MXU), w_in_f[o2:o3],
                           jnp.zeros((HEAD_PAD - NOPE - ROPE, D_MODEL), MXU), w_in_f[o3:]], axis=0)
    wqt = jnp.pad(g_uq_t, ((0, 0), (0, HEAD_PAD - NOPE - ROPE), (0, 0))).reshape(hp, Q_LORA)
    w_ukv_f = jnp.transpose(g_ukv, (1, 0, 2)).reshape(KV_LORA, HEADS, 2 * NOPE)
    pad_head = ((0, 0), (0, 0), (0, HEAD_PAD - NOPE))
    wkv = jnp.concatenate([jnp.pad(w_ukv_f[:, :, :NOPE], pad_head).reshape(KV_LORA, hp),
                           jnp.pad(w_ukv_f[:, :, NOPE:], pad_head).reshape(KV_LORA, hp)], axis=1)
    cb8 = conv_b.reshape(N_DEV, 1, FF_BLK)
    bias_exp = jnp.repeat(b_spatial[0].T, GM_DIM, axis=1)
    ln_g, ln_b = gm_ln_g.reshape(1, GM_WIDTH), gm_ln_b.reshape(1, GM_WIDTH)
    w_sp = w_spatial[0]
    cos_t, sin_t = _rope_tables(s)

    ada_part, c_act = _ada_fwd(g_c.reshape(N_DEV, D_MODEL), w_ada[0])
    _, (ada_recv,) = _exchange([], [(ada_part.reshape(N_DEV, 1, -1), _plain_slot)], "ada_rows")
    ada_raw = ada_recv.reshape(6, D_MODEL)
    ada_b = b_ada.reshape(6, D_MODEL)

    h1, z, qp, kp, vp, cqn, ckvn = _mix_in_fwd(x2d, ada_raw, ada_b, g_pre_mix, w1t, g_q, g_kv, wqt, wkv, cos_t, sin_t, tm)
    sgu = _gmlp_fwd(z, ln_g, ln_b, w_sp, bias_exp, tm)
    o_pad, lse, (g_out, g_up, g_down) = _attn_fwd(
        qp, kp, vp, tq, [w_out[0].astype(MXU), w_up_t.astype(MXU), w_down[0].astype(MXU)])
    w_out_f = g_out.reshape(2 * GM_WIDTH, D_MODEL)
    wo_attn = jnp.pad(w_out_f[:GM_WIDTH].reshape(HEADS, NOPE, D_MODEL), ((0, 0), (0, HEAD_PAD - NOPE), (0, 0)))
    wo = jnp.concatenate([wo_attn.reshape(hp, D_MODEL), w_out_f[GM_WIDTH:]], axis=0)
    wd = g_down.reshape(half, FF_BLK, D_MODEL)
    m_mix, x2, h2 = _out_proj_fwd(o_pad, sgu, wo, x2d, ada_raw, ada_b, g_post_mix, g_pre_ffn, tm)
    up_a, up_b, y_a, y_b, act = _ffn_up_fwd(h2, g_up, g_cw, cb8, tf)
    d_out, d_f, loss_part, d_gt2, d_g_post_ffn = _ffn_down_fwd(act, wd, x2, target, ada_raw, ada_b, g_post_ffn, tf)

    d_up, d_cw, d_cb, p_down, p_up = _ffn_down_bwd(d_f, wd, up_a, up_b, y_a, y_b, g_cw, act, h2, tf)
    p_down = p_down.reshape(N_DEV, -1, D_MODEL)
    d_x2, d_m, d_sh2, d_sc2, d_g_pre_ffn, d_gt1, d_g_post_mix = _ffn_up_bwd(
        d_up, g_up, x2, m_mix, d_out, ada_raw, ada_b, g_pre_ffn, g_post_mix, tm)
    dwo_attn, dwo_sgu = _tn_matmuls([o_pad, sgu, d_m], [(0, 2), (1, 2)], "dw_out", ts)
    dwo_attn = dwo_attn.reshape(HEADS, HEAD_PAD, D_MODEL)[:, :NOPE]
    p_out = jnp.concatenate([dwo_attn.reshape(GM_WIDTH, D_MODEL), dwo_sgu], axis=0).reshape(N_DEV, -1, D_MODEL)
    d_o, d_sgu, delta = _out_proj_bwd(d_m, wo, o_pad, tm)
    d_guv, d_ws, d_bs, d_ln_g, d_ln_b = _gmlp_bwd(z, d_sgu, ln_g, ln_b, w_sp, bias_exp, tm)
    packed = _pack_small([d_gt1, d_sh2, d_sc2, d_gt2], [d_g_post_mix, d_g_pre_ffn, d_g_post_ffn], loss_part,
                         d_ln_g, d_ln_b, d_bs, d_cb, d_ws)

    def ffn_slot(j):
        return (j % half, j // half)

    dq, dk, dv, (r_out, r_up, r_down, r_cw), (g_small,) = _attn_bwd(
        qp, kp, vp, d_o, lse, delta, tq,
        [(p_out, _plain_slot), (p_up, ffn_slot), (p_down, _plain_slot), (d_cw, ffn_slot)], [packed])
    grad_x, d_za, d_qp, d_kvp, d_sh1, d_sc1, d_g_pre_mix, d_g_q, d_g_kv = _mix_in_bwd(
        dq, dk, dv, z, d_guv, x2d, d_x2, ada_raw, ada_b, g_pre_mix, g_q, g_kv, w1t, wqt, wkv, cos_t, sin_t,
        min(256, s))
    dw1a, dw1b, dwq, dwkv = _tn_matmuls([d_za, d_guv, h1, d_qp, cqn, ckvn, d_kvp],
                                        [(0, 2), (1, 2), (3, 4), (5, 6)], "dw_mixer", ts // 2)
    d_w_in_t = jnp.concatenate([dw1a[:o2], dw1a[o2 + NOPE:o2 + NOPE + ROPE], dw1b], axis=0)
    p_in = d_w_in_t.reshape(N_DEV, -1, D_MODEL)
    p_uq = dwq.reshape(HEADS, HEAD_PAD, Q_LORA)[:, :NOPE + ROPE]
    dwk = dwkv[:, :hp].reshape(KV_LORA, HEADS, HEAD_PAD)[:, :, :NOPE]
    dwv = dwkv[:, hp:].reshape(KV_LORA, HEADS, HEAD_PAD)[:, :, :NOPE]
    p_ukv = jnp.transpose(jnp.concatenate([dwk, dwv], axis=2), (1, 0, 2))

    (g_late,), (r_in, r_uq, r_ukv) = _exchange(
        [_pack_late(d_sh1, d_sc1, d_g_pre_mix, d_g_q, d_g_kv)],
        [(p_in, _plain_slot), (p_uq, _plain_slot), (p_ukv, _plain_slot)], "final_exchange")
    small_params = [(b_ada, m_b_ada, v_b_ada), (g_pre_mix, m_g_pre_mix, v_g_pre_mix),
                    (g_post_mix, m_g_post_mix, v_g_post_mix), (g_pre_ffn, m_g_pre_ffn, v_g_pre_ffn),
                    (g_post_ffn, m_g_post_ffn, v_g_post_ffn), (g_q, m_g_q, v_g_q), (g_kv, m_g_kv, v_g_kv),
                    (gm_ln_g, m_gm_ln_g, v_gm_ln_g), (gm_ln_b, m_gm_ln_b, v_gm_ln_b),
                    (w_spatial, m_w_spatial, v_w_spatial), (b_spatial, m_b_spatial, v_b_spatial),
                    tuple(a.reshape(N_DEV, FF_BLK) for a in (conv_b, m_conv_b, v_conv_b))]
    small_out, loss_row, d_ada_all = _adam_small(g_small, g_late, small_params)
    small_out[11] = tuple(o.reshape(conv_b.shape) for o in small_out[11])
    loss = loss_row[0, 0]

    def big(recv, w, m, v, name):
        g, d, m2, v2 = _adam_reduce(recv, w[0], m[0], v[0], name)
        return g[None], d[None], m2[None], v2[None]

    def big_t(recv, w_t, m_t, v_t, name):
        return tuple(jnp.swapaxes(o, 0, 1)[None] for o in _adam_reduce(recv, w_t, m_t, v_t, name))

    a_in = big_t(r_in, w_in_t, m_in_t, v_in_t, "adam_w_in")
    a_uq = big_t(r_uq, w_uq_t, m_uq_t, v_uq_t, "adam_w_uq")
    a_ukv = big(r_ukv, w_ukv, m_w_ukv, v_w_ukv, "adam_w_ukv")
    a_out = big(r_out, w_out, m_w_out, v_w_out, "adam_w_out")
    a_up = big_t(r_up, w_up_t, m_up_t, v_up_t, "adam_w_up")
    a_down = big(r_down, w_down, m_w_down, v_w_down, "adam_w_down")
    ada_cols = w_ada.shape[2]
    d_ada_cols = lax.dynamic_slice(d_ada_all.reshape(N_DEV, 6 * D_MODEL), (0, my_slot * ada_cols), (N_DEV, ada_cols))
    pad_seq = 128 - N_DEV
    a_ada = tuple(t[None] for t in _adam_w_ada(jnp.pad(c_act.T, ((0, 0), (0, pad_seq))).astype(MXU),
                                               jnp.pad(d_ada_cols, ((0, pad_seq), (0, 0))).astype(MXU),
                                               w_ada[0], m_w_ada[0], v_w_ada[0]))
    a_cw = big(r_cw, conv_w, m_conv_w, v_conv_w, "adam_conv_w")

    def small(k):
        return small_out[k]

    per_weight = [a_ada, small(0), small(1), small(2), a_in, small(5), a_uq, small(6), a_ukv, small(7), small(8),
                  small(9), small(10), a_out, small(3), small(4), a_up, a_cw, small(11), a_down]
    outs = [loss, grad_x[None]]
    for k in range(4):
        outs += [t[k] for t in per_weight]
    return tuple(outs)
```

```python
import functools

import jax
import jax.numpy as jnp
from jax import lax
from jax.experimental import pallas as pl
from jax.experimental.pallas import tpu as pltpu

F32 = jnp.float32
MXU = jnp.bfloat16

N_DEV = 8
D_MODEL = 1024
HEADS = 8
HEAD_PAD = 128
NOPE = 64
ROPE = 32
Q_LORA = 256
KV_LORA = 128
GM_WIDTH = 512
GM_DIM = 64
GM_CHUNK = 128
CHUNK_SHIFT = 6
ROPE_THETA = 10000.0
ATTN_SCALE = (NOPE + ROPE) ** -0.5
LOG2E = 1.4426950408889634
SCALE_LOG2E = ATTN_SCALE * LOG2E
Z_COLS = 1536
FF_BLK = 704
EPS = 1e-6
ADAM_LR = 0.001
ADAM_B1 = 0.9
ADAM_B2 = 0.999
ADAM_EPS = 1e-08
ADAM_WD = 0.01
ADAM_STEP = 10
VMEM_LIMIT = 56 * 1024 * 1024
MESH = pl.DeviceIdType.MESH


def _dot(a, b):
    return jnp.dot(a, b, preferred_element_type=F32)


def _dot_nt(a, b):
    return lax.dot_general(a, b, (((1,), (1,)), ((), ())), preferred_element_type=F32)


def _dot_tn(a, b):
    return lax.dot_general(a, b, (((0,), (0,)), ((), ())), preferred_element_type=F32)


def _call(body, *, name, grid, in_specs, out_specs, out_shape, scratch=(), sem=None):
    params = pltpu.CompilerParams(dimension_semantics=sem, vmem_limit_bytes=VMEM_LIMIT)
    return pl.pallas_call(body, name=name, grid=grid, in_specs=in_specs, out_specs=out_specs,
                          out_shape=out_shape, scratch_shapes=list(scratch), compiler_params=params)


def _full(shape):
    n = len(shape)
    return pl.BlockSpec(shape, lambda *_: (0,) * n)


def _rows(tm, cols, col_block=0):
    return pl.BlockSpec((tm, cols), lambda i: (i, col_block))


def _sds(shape, dtype):
    return jax.ShapeDtypeStruct(shape, dtype)


def _row(ref, k):
    return ref[pl.ds(k, 1), :]


def _rms(x):
    r = lax.rsqrt(jnp.mean(x * x, axis=-1, keepdims=True) + EPS)
    return x * r, r


def _rms_bwd(d_hat, hat, r):
    return r * (d_hat - hat * jnp.mean(d_hat * hat, axis=-1, keepdims=True))


def _rope_partner(t):
    lane = lax.broadcasted_iota(jnp.int32, t.shape, 1)
    swapped = jnp.where(lane < NOPE + ROPE // 2, -pltpu.roll(t, HEAD_PAD - ROPE // 2, 1), pltpu.roll(t, ROPE // 2, 1))
    return jnp.where((lane >= NOPE) & (lane < NOPE + ROPE), swapped, 0.0)


def _rope(t, cos, sin):
    return t * cos + _rope_partner(t) * sin


def _rope_transposed(g, cos, sin):
    return g * cos - _rope_partner(g * sin)


def _gelu(x):
    return x * (0.5 * (1.0 + jnp.tanh(0.7978845608028654 * (x + 0.044715 * (x * x * x)))))


def _gelu_grad(x):
    t = jnp.tanh(0.7978845608028654 * (x + 0.044715 * (x * x * x)))
    return 0.5 * (1.0 + t) + 0.5 * x * (1.0 - t * t) * (0.7978845608028654 * (1.0 + 3.0 * 0.044715 * (x * x)))


def _split_dot(x, mat):
    hi = x.astype(MXU)
    lo = (x - hi.astype(F32)).astype(MXU)
    return _dot(hi, mat) + _dot(lo, mat)


def _split_dot3(x, mat):
    hi = x.astype(MXU)
    r1 = x - hi.astype(F32)
    mid = r1.astype(MXU)
    lo = (r1 - mid.astype(F32)).astype(MXU)
    return (_dot(hi, mat) + _dot(mid, mat)) + _dot(lo, mat)


def _seg_matrix():
    r = lax.broadcasted_iota(jnp.int32, (GM_WIDTH, GM_WIDTH), 0) >> 6
    c = lax.broadcasted_iota(jnp.int32, (GM_WIDTH, GM_WIDTH), 1) >> 6
    return jnp.where(r == c, 1.0 / GM_DIM, 0.0).astype(MXU)


def _spatial_mask():
    i = lax.broadcasted_iota(jnp.int32, (GM_CHUNK, GM_CHUNK), 0) >> CHUNK_SHIFT
    j = lax.broadcasted_iota(jnp.int32, (GM_CHUNK, GM_CHUNK), 1) >> CHUNK_SHIFT
    return (j <= i).astype(F32)


def _my_place():
    return lax.axis_index("x"), lax.axis_index("y"), lax.axis_index("c")


def _flat(p):
    return 4 * p[0] + 2 * p[1] + p[2]


def _comm_sems(n):
    return [pltpu.SemaphoreType.DMA((7 * n,)), pltpu.SemaphoreType.DMA((7 * n,)), pltpu.SemaphoreType.DMA((n,))]


def _gather_steps(ins, outs, sems):
    send_sems, recv_sems, local_sems = sems
    n = len(ins)
    x, y, c = _my_place()
    me, sibling = (x, y, c), (x, y, 1 - c)
    chips = [(1 - x, y), (x, 1 - y), (1 - x, 1 - y)]

    def copy(a, k, block, to, src=None):
        slot = outs[a].at[_flat(block)]
        return pltpu.make_async_remote_copy(
            src_ref=slot if src is None else src, dst_ref=slot,
            send_sem=send_sems.at[7 * a + k], recv_sem=recv_sems.at[7 * a + k],
            device_id=to, device_id_type=MESH)

    def mine():
        return [pltpu.make_async_copy(ins[a], outs[a].at[_flat(me)], local_sems.at[a]) for a in range(n)]

    def first():
        cps = []
        for a in range(n):
            cps.append(copy(a, 0, me, sibling, src=ins[a]))
            cps += [copy(a, 1 + j, me, (*chip, c), src=ins[a]) for j, chip in enumerate(chips)]
        return cps

    def passed():
        return [copy(a, 4 + j, (*chip, c), sibling) for a in range(n) for j, chip in enumerate(chips)]

    def start():
        for cp in mine() + first():
            cp.start()

    def forward():
        for a in range(n):
            for j, chip in enumerate(chips):
                copy(a, 1 + j, (*chip, c), me).wait_recv()
                copy(a, 4 + j, (*chip, c), sibling).start()

    def finish():
        for a in range(n):
            copy(a, 0, sibling, me).wait_recv()
            for j, chip in enumerate(chips):
                copy(a, 4 + j, (*chip, 1 - c), me).wait_recv()
        for cp in first() + passed():
            cp.wait_send()
        for cp in mine():
            cp.wait()

    return start, forward, finish


def _scatter_steps(ins, outs, sems, slots):
    send_sems, recv_sems, local_sems = sems
    n = len(ins)
    flips = [(fx, fy, fc) for fx in (0, 1) for fy in (0, 1) for fc in (0, 1)][1:]
    me = _my_place()

    def peer(f):
        return tuple(1 - v if b else v for v, b in zip(me, f))

    def copy(a, k, arriving=False):
        p = peer(flips[k])
        return pltpu.make_async_remote_copy(
            src_ref=ins[a].at[slots[a](_flat(p))], dst_ref=outs[a].at[_flat(p if arriving else me)],
            send_sem=send_sems.at[7 * a + k], recv_sem=recv_sems.at[7 * a + k],
            device_id=p, device_id_type=MESH)

    def mine():
        return [pltpu.make_async_copy(ins[a].at[slots[a](_flat(me))], outs[a].at[_flat(me)], local_sems.at[a])
                for a in range(n)]

    def start():
        for cp in mine() + [copy(a, k) for a in range(n) for k in range(7)]:
            cp.start()

    def finish():
        for a in range(n):
            for k in range(7):
                copy(a, k, arriving=True).wait_recv()
        for a in range(n):
            for k in range(7):
                copy(a, k).wait_send()
        for cp in mine():
            cp.wait()

    return start, finish


def _plain_slot(j):
    return (j,)


def _scatter_out_shape(arr, slot):
    return _sds((N_DEV,) + arr.shape[len(slot(0)):], arr.dtype)


def _exchange(gathered, scattered, name):
    ng, ns = len(gathered), len(scattered)
    slots = [slot for _, slot in scattered]

    def body(*refs):
        g_in, s_in = refs[:ng], refs[ng:ng + ns]
        g_out, s_out = refs[ng + ns:2 * ng + ns], refs[2 * ng + ns:2 * (ng + ns)]
        sems = refs[2 * (ng + ns):]
        g_start, g_forward, g_finish = _gather_steps(g_in, g_out, sems[:3])
        s_start, s_finish = _scatter_steps(s_in, s_out, sems[3:], slots)
        g_start()
        s_start()
        g_forward()
        g_finish()
        s_finish()

    any_spec = pl.BlockSpec(memory_space=pl.ANY)
    outs = pl.pallas_call(
        body, name=name,
        in_specs=[any_spec] * (ng + ns), out_specs=[any_spec] * (ng + ns),
        out_shape=[_sds((N_DEV,) + a.shape, a.dtype) for a in gathered]
        + [_scatter_out_shape(a, slot) for a, slot in scattered],
        scratch_shapes=_comm_sems(max(ng, 1)) + _comm_sems(max(ns, 1)),
    )(*gathered, *[a for a, _ in scattered])
    return outs[:ng], outs[ng:]


def _ada_fwd(c_all, w_ada):
    def body(c_ref, w_ref, part_ref, act_ref):
        cv = c_ref[...]
        act = cv * jax.nn.sigmoid(cv)
        act_ref[...] = act
        part_ref[...] = _dot(act.astype(MXU), w_ref[...].astype(MXU))

    cols = w_ada.shape[1]
    return _call(body, name="ada_fwd", grid=(1,),
                 in_specs=[_full(c_all.shape), _full(w_ada.shape)],
                 out_specs=[_full((N_DEV, cols)), _full(c_all.shape)],
                 out_shape=[_sds((N_DEV, cols), F32), _sds(c_all.shape, F32)])(c_all, w_ada)


def _mix_in_fwd(x, ada_raw, ada_b, g_pre, w1, g_q, g_kv, wq, wkv, cos_t, sin_t, tm):
    s = x.shape[0]

    def body(x_ref, ar_ref, ab_ref, g_ref, w1_ref, gq_ref, gkv_ref, wq_ref, wkv_ref, cos_ref, sin_ref,
             h1_ref, z_ref, qp_ref, kp_ref, vp_ref, cqn_ref, ckvn_ref, w1_n, wq_n):
        @pl.when(pl.program_id(0) == 0)
        def _():
            w1_n[...] = w1_ref[...].T
            wq_n[...] = wq_ref[...].T

        sh = _row(ar_ref, 0) + _row(ab_ref, 0)
        sc = _row(ar_ref, 1) + _row(ab_ref, 1)
        xn, _ = _rms(x_ref[...])
        hb = ((xn * g_ref[...]) * (1.0 + sc) + sh).astype(MXU)
        h1_ref[...] = hb
        z = _dot(hb, w1_n[...])
        z_ref[...] = z
        cos, sin = cos_ref[...], sin_ref[...]
        cqn = (_rms(z[:, :Q_LORA])[0] * gq_ref[...]).astype(MXU)
        ckvn = (_rms(z[:, Q_LORA:Q_LORA + KV_LORA])[0] * gkv_ref[...]).astype(MXU)
        cqn_ref[...] = cqn
        ckvn_ref[...] = ckvn
        q = _dot(cqn, wq_n[...])
        kv = _dot(ckvn, wkv_ref[...])
        k_rope = _rope(z[:, Q_LORA + KV_LORA:Q_LORA + KV_LORA + HEAD_PAD], cos, sin)
        for h in range(HEADS):
            blk = slice(h * HEAD_PAD, (h + 1) * HEAD_PAD)
            qp_ref[:, blk] = _rope(q[:, blk], cos, sin).astype(MXU)
            kp_ref[:, blk] = (kv[:, blk] + k_rope).astype(MXU)
        v_lane = lax.broadcasted_iota(jnp.int32, (tm, HEADS * HEAD_PAD), 1) & (HEAD_PAD - 1)
        vp_ref[...] = jnp.where(v_lane == NOPE, 1.0, kv[:, HEADS * HEAD_PAD:]).astype(MXU)

    hp = HEADS * HEAD_PAD
    return _call(
        body, name="mix_in_fwd", grid=(s // tm,), sem=("arbitrary",),
        in_specs=[_rows(tm, D_MODEL), _full(ada_raw.shape), _full(ada_b.shape), _full(g_pre.shape), _full(w1.shape),
                  _full(g_q.shape), _full(g_kv.shape), _full(wq.shape), _full(wkv.shape),
                  _rows(tm, HEAD_PAD), _rows(tm, HEAD_PAD)],
        out_specs=[_rows(tm, D_MODEL), _rows(tm, Z_COLS), _rows(tm, hp), _rows(tm, hp), _rows(tm, hp),
                   _rows(tm, Q_LORA), _rows(tm, KV_LORA)],
        out_shape=[_sds((s, D_MODEL), MXU), _sds((s, Z_COLS), F32), _sds((s, hp), MXU), _sds((s, hp), MXU),
                   _sds((s, hp), MXU), _sds((s, Q_LORA), MXU), _sds((s, KV_LORA), MXU)],
        scratch=[pltpu.VMEM(w1.shape[::-1], MXU), pltpu.VMEM(wq.shape[::-1], MXU)],
    )(x, ada_raw, ada_b, g_pre, w1, g_q, g_kv, wq, wkv, cos_t, sin_t)


def _gm_norm(zv, seg):
    gv = _gelu(zv)
    cen = gv - _split_dot(gv, seg)
    rstd = lax.rsqrt(_split_dot(cen * cen, seg) + EPS)
    return gv, cen * rstd, rstd


def _gm_pairs(rows):
    first = lax.broadcasted_iota(jnp.int32, (rows, 2 * GM_DIM), 1) < GM_DIM
    return [(slice(p * 2 * GM_DIM, (p + 1) * 2 * GM_DIM), first) for p in range(HEADS // 2)]


def _gm_mix(wm, vb, rows, transposed=False):
    dot = _dot_tn if transposed else _dot
    return jnp.concatenate([jnp.where(first, dot(wm[2 * p], vb[:, lanes]), dot(wm[2 * p + 1], vb[:, lanes]))
                            for p, (lanes, first) in enumerate(_gm_pairs(rows))], axis=1)


def _gmlp_fwd(z, ln_g, ln_b, w_sp, bias_exp, tm):
    s = z.shape[0]
    nblk = tm // GM_CHUNK

    def body(zu_ref, zv_ref, lg_ref, lb_ref, w_ref, be_ref, sgu_ref):
        seg = _seg_matrix()
        mask = _spatial_mask()
        wm = [(w_ref[h] * mask).astype(MXU) for h in range(HEADS)]
        gu = _gelu(zu_ref[...])
        _, vhat, _ = _gm_norm(zv_ref[...], seg)
        vln = (vhat * lg_ref[...] + lb_ref[...]).astype(MXU)
        for n in range(nblk):
            rows = slice(n * GM_CHUNK, (n + 1) * GM_CHUNK)
            mixed = _gm_mix(wm, vln[rows], GM_CHUNK) + be_ref[...]
            sgu_ref[rows, :] = (gu[rows] * mixed).astype(MXU)

    return _call(
        body, name="gmlp_fwd", grid=(s // tm,), sem=("parallel",),
        in_specs=[_rows(tm, GM_WIDTH, 1), _rows(tm, GM_WIDTH, 2), _full(ln_g.shape), _full(ln_b.shape),
                  _full(w_sp.shape), _full(bias_exp.shape)],
        out_specs=_rows(tm, GM_WIDTH), out_shape=_sds((s, GM_WIDTH), MXU),
    )(z, z, ln_g, ln_b, w_sp, bias_exp)


def _chunk_mask(n_q, n_k, q_off):
    qc = (q_off + lax.broadcasted_iota(jnp.int32, (n_q, n_k), 0)) >> CHUNK_SHIFT
    kc = lax.broadcasted_iota(jnp.int32, (n_q, n_k), 1) >> CHUNK_SHIFT
    return kc <= qc


NEG_BIG = -1e30
ATTN_HEADS_PER_STEP = 2


def _attn_fwd(qp, kp, vp, tq, gathered):
    s = qp.shape[0]
    nq = s // tq
    hb = 2 * ATTN_HEADS_PER_STEP
    groups = HEADS // hb
    width = hb * HEAD_PAD
    ng = len(gathered)

    def body(q_ref, k_ref, v_ref, *rest):
        g_in, (o_ref, lse_ref), g_out = rest[:ng], rest[ng:ng + 2], rest[ng + 2:2 * ng + 2]
        m_sc, acc_sc = rest[2 * ng + 2:2 * ng + 4]
        g_start, g_forward, g_finish = _gather_steps(g_in, g_out, rest[2 * ng + 4:])
        g, i = pl.program_id(0), pl.program_id(1)
        pl.when((g == 0) & (i == 0))(g_start)
        pl.when((g == groups - 1) & (i == 0))(g_forward)
        m_sc[...] = jnp.full(m_sc.shape, NEG_BIG, F32)
        acc_sc[...] = jnp.zeros(acc_sc.shape, F32)

        def tile(j, masked, n_tiles=1):
            n_k = n_tiles * tq
            rows = pl.ds(pl.multiple_of(j * tq, tq), n_k)
            for hh in range(hb):
                lanes = slice(hh * HEAD_PAD, (hh + 1) * HEAD_PAD)
                sc = _dot_nt(q_ref[:, lanes], k_ref[rows, lanes])
                if masked:
                    sc = jnp.where(_chunk_mask(tq, n_k, n_k - tq), sc, NEG_BIG)
                blocks = [sc[:, b * 128:(b + 1) * 128] for b in range(n_k // 128)]
                m_prev = m_sc[hh]
                m_tile = jnp.max(functools.reduce(jnp.maximum, blocks), axis=-1, keepdims=True)
                m_new = jnp.maximum(m_prev, m_tile)
                alpha = jnp.exp2((m_prev - m_new) * SCALE_LOG2E)
                p = jnp.concatenate([jnp.exp2((b - m_new) * SCALE_LOG2E) for b in blocks], axis=1).astype(MXU)
                acc_sc[hh] = alpha * acc_sc[hh] + _dot(p, v_ref[rows, lanes])
                m_sc[hh] = m_new

        def off_diagonal_pair(p, carry):
            tile(2 * p, False, n_tiles=2)
            return carry

        lax.fori_loop(0, i // 2, off_diagonal_pair, 0)

        @pl.when(i % 2 == 1)
        def _():
            tile(i - 1, True, n_tiles=2)

        @pl.when(i % 2 == 0)
        def _():
            tile(i, True)
        for hh in range(hb):
            lanes = slice(hh * HEAD_PAD, (hh + 1) * HEAD_PAD)
            acc = acc_sc[hh]
            denom = acc[:, NOPE:NOPE + 1]
            o_ref[:, lanes] = (acc / denom).astype(MXU)
            lse_ref[hh] = m_sc[hh][:, :1] * SCALE_LOG2E + jnp.log(denom) * LOG2E
        pl.when((g == groups - 1) & (i == nq - 1))(g_finish)

    q_spec = pl.BlockSpec((tq, width), lambda g, i: (i, g))
    kv_spec = pl.BlockSpec((s, width), lambda g, i: (0, g))
    any_spec = pl.BlockSpec(memory_space=pl.ANY)
    outs = _call(
        body, name="attn_fwd", grid=(groups, nq), sem=("arbitrary", "arbitrary"),
        in_specs=[q_spec, kv_spec, kv_spec] + [any_spec] * ng,
        out_specs=[q_spec, pl.BlockSpec((hb, tq, 1), lambda g, i: (g, i, 0))] + [any_spec] * ng,
        out_shape=[_sds(qp.shape, MXU), _sds((HEADS, s, 1), F32)]
        + [_sds((N_DEV,) + a.shape, a.dtype) for a in gathered],
        scratch=[pltpu.VMEM((hb, tq, HEAD_PAD), F32), pltpu.VMEM((hb, tq, HEAD_PAD), F32)] + _comm_sems(ng),
    )(qp, kp, vp, *gathered)
    return outs[0], outs[1], outs[2:]


def _out_proj_fwd(o_pad, sgu, wo, x, ada_raw, ada_b, g_post_mix, g_pre_ffn, tm):
    s = x.shape[0]
    hp = HEADS * HEAD_PAD

    def body(o_ref, sgu_ref, wo_ref, x_ref, ar_ref, ab_ref, gpm_ref, gpf_ref, m_ref, x2_ref, h2_ref):
        gt1 = _row(ar_ref, 2) + _row(ab_ref, 2)
        sh2 = _row(ar_ref, 3) + _row(ab_ref, 3)
        sc2 = _row(ar_ref, 4) + _row(ab_ref, 4)
        m = _dot(o_ref[...], wo_ref[pl.ds(0, hp), :]) + _dot(sgu_ref[...], wo_ref[pl.ds(hp, GM_WIDTH), :])
        m_ref[...] = m
        x2 = x_ref[...] + gt1 * (_rms(m)[0] * gpm_ref[...])
        x2_ref[...] = x2
        h2_ref[...] = ((_rms(x2)[0] * gpf_ref[...]) * (1.0 + sc2) + sh2).astype(MXU)

    return _call(
        body, name="out_proj_fwd", grid=(s // tm,), sem=("parallel",),
        in_specs=[_rows(tm, hp), _rows(tm, GM_WIDTH), _full(wo.shape), _rows(tm, D_MODEL), _full(ada_raw.shape),
                  _full(ada_b.shape), _full(g_post_mix.shape), _full(g_pre_ffn.shape)],
        out_specs=[_rows(tm, D_MODEL)] * 3,
        out_shape=[_sds((s, D_MODEL), F32), _sds((s, D_MODEL), F32), _sds((s, D_MODEL), MXU)],
    )(o_pad, sgu, wo, x, ada_raw, ada_b, g_post_mix, g_pre_ffn)


def _conv(u, halo, cw_ref, cb_ref):
    ext = jnp.concatenate([halo, u], axis=0)
    m1, m2 = pltpu.roll(ext, 1, 0)[8:], pltpu.roll(ext, 2, 0)[8:]
    return cb_ref[0] + ((m2 * cw_ref[0, pl.ds(0, 1), :] + m1 * cw_ref[0, pl.ds(1, 1), :]) + u * cw_ref[0, pl.ds(2, 1), :])


ROW_SUB = 256


def _sub_blocks(tm):
    return [slice(r, r + ROW_SUB) for r in range(0, tm, ROW_SUB)]


def _ffn_up_fwd(h2, w_up, conv_w, conv_b, tm):
    s = h2.shape[0]
    half = N_DEV // 2

    def body(h_ref, wa_ref, wb_ref, cwa_ref, cwb_ref, cba_ref, cbb_ref,
             ua_ref, ub_ref, ya_ref, yb_ref, act_ref, halo_a, halo_b, wa_t, wb_t):
        i = pl.program_id(1)

        @pl.when(i == 0)
        def _():
            halo_a[...] = jnp.zeros(halo_a.shape, F32)
            halo_b[...] = jnp.zeros(halo_b.shape, F32)
            wa_t[...] = wa_ref[0].T
            wb_t[...] = wb_ref[0].T

        ha, hb = halo_a[...], halo_b[...]
        for rows in _sub_blocks(tm):
            h = h_ref[rows, :]
            ua = _dot(h, wa_t[...])
            ub = _dot(h, wb_t[...])
            ua_ref[0, rows, :] = ua
            ub_ref[0, rows, :] = ub
            ya = _conv(ua, ha, cwa_ref, cba_ref)
            yb = _conv(ub, hb, cwb_ref, cbb_ref)
            ya_ref[0, rows, :] = ya
            yb_ref[0, rows, :] = yb
            ha, hb = ua[ROW_SUB - 8:], ub[ROW_SUB - 8:]
            act_ref[0, rows, :] = ((ya * jax.nn.sigmoid(ya)) * yb).astype(MXU)
        halo_a[...] = ha
        halo_b[...] = hb

    def blk(shape, off):
        return pl.BlockSpec(shape, lambda j, i: (j + off, 0, 0))

    def tok(off=0):
        return pl.BlockSpec((1, tm, FF_BLK), lambda j, i: (j + off, i, 0))

    return _call(
        body, name="ffn_up_fwd", grid=(half, s // tm), sem=("parallel", "arbitrary"),
        in_specs=[pl.BlockSpec((tm, D_MODEL), lambda j, i: (i, 0)),
                  blk((1, FF_BLK, D_MODEL), 0), blk((1, FF_BLK, D_MODEL), half),
                  blk((1, 3, FF_BLK), 0), blk((1, 3, FF_BLK), half), blk((1, 1, FF_BLK), 0), blk((1, 1, FF_BLK), half)],
        out_specs=[tok()] * 5,
        out_shape=[_sds((half, s, FF_BLK), F32)] * 4 + [_sds((half, s, FF_BLK), MXU)],
        scratch=[pltpu.VMEM((8, FF_BLK), F32), pltpu.VMEM((8, FF_BLK), F32),
                 pltpu.VMEM((D_MODEL, FF_BLK), MXU), pltpu.VMEM((D_MODEL, FF_BLK), MXU)],
    )(h2, w_up, w_up, conv_w, conv_w, conv_b, conv_b)


def _ffn_down_fwd(act, wd, x2, target, ada_raw, ada_b, g_post_ffn, tm):
    s = x2.shape[0]
    half = N_DEV // 2

    def body(act_ref, wd_ref, x2_ref, t_ref, ar_ref, ab_ref, g_ref, dout_ref, df_ref, loss_ref, dgt_ref, dg_ref):
        i = pl.program_id(0)

        @pl.when(i == 0)
        def _():
            loss_ref[...] = jnp.zeros(loss_ref.shape, F32)
            dgt_ref[...] = jnp.zeros(dgt_ref.shape, F32)
            dg_ref[...] = jnp.zeros(dg_ref.shape, F32)

        gt2 = _row(ar_ref, 5) + _row(ab_ref, 5)
        g = g_ref[...]
        for rows in _sub_blocks(tm):
            f = _dot(act_ref[0, rows, :], wd_ref[0])
            for j in range(1, half):
                f = f + _dot(act_ref[j, rows, :], wd_ref[j])
            fhat, rf = _rms(f)
            fn = fhat * g
            err = (x2_ref[rows, :] + gt2 * fn) - t_ref[rows, :]
            loss_ref[...] += 0.5 * jnp.sum(jnp.mean(err * err, axis=-1, keepdims=True))
            d_out = err * (1.0 / D_MODEL)
            dout_ref[rows, :] = d_out
            dgt_ref[...] += jnp.sum(d_out * fn, axis=0, keepdims=True)
            d_fn = d_out * gt2
            dg_ref[...] += jnp.sum(d_fn * fhat, axis=0, keepdims=True)
            df_ref[rows, :] = _rms_bwd(d_fn * g, fhat, rf).astype(MXU)

    vec = pl.BlockSpec((1, D_MODEL), lambda i: (0, 0))
    return _call(
        body, name="ffn_down_fwd", grid=(s // tm,), sem=("arbitrary",),
        in_specs=[pl.BlockSpec((half, tm, FF_BLK), lambda i: (0, i, 0)), _full(wd.shape), _rows(tm, D_MODEL),
                  _rows(tm, D_MODEL), _full(ada_raw.shape), _full(ada_b.shape), _full(g_post_ffn.shape)],
        out_specs=[_rows(tm, D_MODEL), _rows(tm, D_MODEL), pl.BlockSpec((1, 128), lambda i: (0, 0)), vec, vec],
        out_shape=[_sds((s, D_MODEL), F32), _sds((s, D_MODEL), MXU), _sds((1, 128), F32),
                   _sds((1, D_MODEL), F32), _sds((1, D_MODEL), F32)],
    )(act, wd, x2, target, ada_raw, ada_b, g_post_ffn)


def _ffn_down_bwd(d_f, wd, up_a, up_b, y_a, y_b, conv_w, act, h2, tm):
    s = d_f.shape[0]
    half = N_DEV // 2
    nt = s // tm

    def body(df_ref, wd_ref, ua_ref, ub_ref, ya_ref, yb_ref, cwa_ref, cwb_ref, act_ref, h2_ref,
             dup_ref, dcw_ref, dcb_ref, pd_ref, pu_ref, next_a, next_b, acc_d, acc_a, acc_b, wd_t):
        i = pl.program_id(1)

        @pl.when(i == 0)
        def _():
            next_a[...] = jnp.zeros(next_a.shape, F32)
            next_b[...] = jnp.zeros(next_b.shape, F32)
            dcw_ref[...] = jnp.zeros(dcw_ref.shape, F32)
            dcb_ref[...] = jnp.zeros(dcb_ref.shape, F32)
            for acc in (acc_d, acc_a, acc_b):
                acc[...] = jnp.zeros(acc.shape, F32)
            wd_t[...] = wd_ref[0].T

        def conv_bwd(d_y, u, nxt, cw_ref, part, rows):
            ext = jnp.concatenate([d_y, nxt], axis=0)
            p1 = pltpu.roll(ext, ROW_SUB + 7, 0)[:ROW_SUB]
            p2 = pltpu.roll(ext, ROW_SUB + 6, 0)[:ROW_SUB]
            d_u = (d_y * cw_ref[0, pl.ds(2, 1), :] + p1 * cw_ref[0, pl.ds(1, 1), :]) + p2 * cw_ref[0, pl.ds(0, 1), :]
            dup_ref[0, part, rows, :] = d_u.astype(MXU)
            dcb_ref[0, part] += jnp.sum(d_y, axis=0, keepdims=True)
            dcw_ref[0, part, pl.ds(0, 1), :] += jnp.sum(p2 * u, axis=0, keepdims=True)
            dcw_ref[0, part, pl.ds(1, 1), :] += jnp.sum(p1 * u, axis=0, keepdims=True)
            dcw_ref[0, part, pl.ds(2, 1), :] += jnp.sum(d_y * u, axis=0, keepdims=True)
            return d_y[:8]

        nxa, nxb = next_a[...], next_b[...]
        for rows in reversed(_sub_blocks(tm)):
            d_act = _dot(df_ref[rows, :], wd_t[...])
            ya, yb = ya_ref[0, rows, :], yb_ref[0, rows, :]
            sig = jax.nn.sigmoid(ya)
            d_ya = d_act * yb * (sig * (1.0 + ya * (1.0 - sig)))
            d_yb = d_act * (ya * sig)
            nxa = conv_bwd(d_ya, ua_ref[0, rows, :], nxa, cwa_ref, 0, rows)
            nxb = conv_bwd(d_yb, ub_ref[0, rows, :], nxb, cwb_ref, 1, rows)
        next_a[...] = nxa
        next_b[...] = nxb
        acc_d[...] += _dot_tn(act_ref[0], df_ref[...])
        acc_a[...] += _dot_tn(dup_ref[0, 0], h2_ref[...])
        acc_b[...] += _dot_tn(dup_ref[0, 1], h2_ref[...])

        @pl.when(i == nt - 1)
        def _():
            pd_ref[0] = acc_d[...].astype(MXU)
            pu_ref[0, 0] = acc_a[...].astype(MXU)
            pu_ref[0, 1] = acc_b[...].astype(MXU)

    def rev(i):
        return nt - 1 - i

    def blk(shape, off):
        return pl.BlockSpec(shape, lambda j, i: (j + off, 0, 0))

    tok = pl.BlockSpec((1, tm, FF_BLK), lambda j, i: (j, rev(i), 0))
    acc3 = pl.BlockSpec((1, 2, 3, FF_BLK), lambda j, i: (j, 0, 0, 0))
    acc1 = pl.BlockSpec((1, 2, 1, FF_BLK), lambda j, i: (j, 0, 0, 0))
    return _call(
        body, name="ffn_down_bwd", grid=(half, nt), sem=("parallel", "arbitrary"),
        in_specs=[pl.BlockSpec((tm, D_MODEL), lambda j, i: (rev(i), 0)), blk((1, FF_BLK, D_MODEL), 0),
                  tok, tok, tok, tok, blk((1, 3, FF_BLK), 0), blk((1, 3, FF_BLK), half),
                  tok, pl.BlockSpec((tm, D_MODEL), lambda j, i: (rev(i), 0))],
        out_specs=[pl.BlockSpec((1, 2, tm, FF_BLK), lambda j, i: (j, 0, rev(i), 0)), acc3, acc1,
                   pl.BlockSpec((1, FF_BLK, D_MODEL), lambda j, i: (j, 0, 0)),
                   pl.BlockSpec((1, 2, FF_BLK, D_MODEL), lambda j, i: (j, 0, 0, 0))],
        out_shape=[_sds((half, 2, s, FF_BLK), MXU), _sds((half, 2, 3, FF_BLK), F32), _sds((half, 2, 1, FF_BLK), F32),
                   _sds((half, FF_BLK, D_MODEL), MXU), _sds((half, 2, FF_BLK, D_MODEL), MXU)],
        scratch=[pltpu.VMEM((8, FF_BLK), F32), pltpu.VMEM((8, FF_BLK), F32)]
        + [pltpu.VMEM((FF_BLK, D_MODEL), F32)] * 3 + [pltpu.VMEM((D_MODEL, FF_BLK), MXU)],
    )(d_f, wd, up_a, up_b, y_a, y_b, conv_w, conv_w, act, h2)


def _ffn_up_bwd(d_up, w_up, x2, m, d_out, ada_raw, ada_b, g_pre_ffn, g_post_mix, tm):
    s = x2.shape[0]
    half = N_DEV // 2

    def body(dup_ref, w_ref, x2_ref, m_ref, dout_ref, ar_ref, ab_ref, gpf_ref, gpm_ref,
             dx_ref, dm_ref, dsh_ref, dsc_ref, dgpf_ref, dgt1_ref, dgpm_ref):
        i = pl.program_id(0)

        @pl.when(i == 0)
        def _():
            for r in (dsh_ref, dsc_ref, dgpf_ref, dgt1_ref, dgpm_ref):
                r[...] = jnp.zeros(r.shape, F32)

        gt1 = _row(ar_ref, 2) + _row(ab_ref, 2)
        sc2 = _row(ar_ref, 4) + _row(ab_ref, 4)
        gpf, gpm = gpf_ref[...], gpm_ref[...]
        d_h2 = _dot(dup_ref[0, 0], w_ref[0])
        for j in range(1, half):
            d_h2 = d_h2 + _dot(dup_ref[j, 0], w_ref[j])
        for j in range(half):
            d_h2 = d_h2 + _dot(dup_ref[j, 1], w_ref[half + j])
        x2n, r2 = _rms(x2_ref[...])
        dsh_ref[...] += jnp.sum(d_h2, axis=0, keepdims=True)
        dsc_ref[...] += jnp.sum(d_h2 * (x2n * gpf), axis=0, keepdims=True)
        d_mod = d_h2 * (1.0 + sc2)
        dgpf_ref[...] += jnp.sum(d_mod * x2n, axis=0, keepdims=True)
        d_x2 = dout_ref[...] + _rms_bwd(d_mod * gpf, x2n, r2)
        dx_ref[...] = d_x2
        mhat, rm = _rms(m_ref[...])
        dgt1_ref[...] += jnp.sum(d_x2 * (mhat * gpm), axis=0, keepdims=True)
        d_mn = d_x2 * gt1
        dgpm_ref[...] += jnp.sum(d_mn * mhat, axis=0, keepdims=True)
        dm_ref[...] = _rms_bwd(d_mn * gpm, mhat, rm).astype(MXU)

    vec = pl.BlockSpec((1, D_MODEL), lambda i: (0, 0))
    tok = pl.BlockSpec((half, 2, tm, FF_BLK), lambda i: (0, 0, i, 0))
    return _call(
        body, name="ffn_up_bwd", grid=(s // tm,), sem=("arbitrary",),
        in_specs=[tok, _full(w_up.shape), _rows(tm, D_MODEL), _rows(tm, D_MODEL), _rows(tm, D_MODEL),
                  _full(ada_raw.shape), _full(ada_b.shape), _full(g_pre_ffn.shape), _full(g_post_mix.shape)],
        out_specs=[_rows(tm, D_MODEL), _rows(tm, D_MODEL), vec, vec, vec, vec, vec],
        out_shape=[_sds((s, D_MODEL), F32), _sds((s, D_MODEL), MXU)] + [_sds((1, D_MODEL), F32)] * 5,
    )(d_up, w_up, x2, m, d_out, ada_raw, ada_b, g_pre_ffn, g_post_mix)


def _out_proj_bwd(d_m, wo, o_pad, tm):
    s = d_m.shape[0]
    hp = HEADS * HEAD_PAD

    def body(dm_ref, wo_ref, o_ref, do_ref, dsgu_ref, delta_ref, wo_t):
        @pl.when(pl.program_id(0) == 0)
        def _():
            wo_t[...] = wo_ref[...].T

        d_cat = _dot(dm_ref[...], wo_t[...])
        d_o = d_cat[:, :hp]
        do_ref[...] = d_o.astype(MXU)
        dsgu_ref[...] = d_cat[:, hp:]
        prod = d_o * o_ref[...].astype(F32)
        for h in range(HEADS):
            delta_ref[h] = jnp.sum(prod[:, h * HEAD_PAD:(h + 1) * HEAD_PAD], axis=-1, keepdims=True)

    return _call(
        body, name="out_proj_bwd", grid=(s // tm,), sem=("arbitrary",),
        in_specs=[_rows(tm, D_MODEL), _full(wo.shape), _rows(tm, hp)],
        out_specs=[_rows(tm, hp), _rows(tm, GM_WIDTH), pl.BlockSpec((HEADS, tm, 1), lambda i: (0, i, 0))],
        out_shape=[_sds((s, hp), MXU), _sds((s, GM_WIDTH), F32), _sds((HEADS, s, 1), F32)],
        scratch=[pltpu.VMEM(wo.shape[::-1], MXU)],
    )(d_m, wo, o_pad)


def _attn_bwd(qp, kp, vp, d_o, lse, delta, tq, scattered, gathered):
    s = qp.shape[0]
    nq = s // tq
    hb = ATTN_HEADS_PER_STEP
    groups = HEADS // hb
    width = hb * HEAD_PAD
    ns, ng = len(scattered), len(gathered)
    nc = ns + ng
    slots = [slot for _, slot in scattered]

    def body(q_ref, k_ref, v_ref, do_ref, lse_ref, dl_ref, *rest):
        c_in, (dq_ref, dk_ref, dv_ref), c_out = rest[:nc], rest[nc:nc + 3], rest[nc + 3:2 * nc + 3]
        dk_sc, dv_sc = rest[2 * nc + 3:2 * nc + 5]
        sems = rest[2 * nc + 5:]
        s_start, s_finish = _scatter_steps(c_in[:ns], c_out[:ns], sems[:3], slots)
        g_start, g_forward, g_finish = _gather_steps(c_in[ns:], c_out[ns:], sems[3:])
        g, j = pl.program_id(0), pl.program_id(1)

        @pl.when((g == 0) & (j == 0))
        def _():
            s_start()
            g_start()

        pl.when((g == groups - 1) & (j == 0))(g_forward)

        @pl.when(j == 0)
        def _():
            dq_ref[...] = jnp.zeros(dq_ref.shape, F32)

        dk_sc[...] = jnp.zeros(dk_sc.shape, F32)
        dv_sc[...] = jnp.zeros(dv_sc.shape, F32)

        def tile(i, masked):
            rows = pl.ds(pl.multiple_of(i * tq, tq), tq)
            for hh in range(hb):
                lanes = slice(hh * HEAD_PAD, (hh + 1) * HEAD_PAD)
                q, do, k = q_ref[rows, lanes], do_ref[rows, lanes], k_ref[:, lanes]
                sc = _dot_nt(q, k)
                if masked:
                    sc = jnp.where(_chunk_mask(tq, tq, 0), sc, NEG_BIG)
                p = jnp.exp2(sc * SCALE_LOG2E - lse_ref[hh, rows, :])
                dv_sc[hh] += _dot_tn(p.astype(MXU), do)
                dp = _dot_nt(do, v_ref[:, lanes])
                ds = (p * (dp - dl_ref[hh, rows, :])).astype(MXU)
                dk_sc[hh] += _dot_tn(ds, q)
                dq_ref[rows, lanes] += _dot(ds, k) * ATTN_SCALE

        def tile_pair(i0, first_masked):
            both = pl.ds(pl.multiple_of(i0 * tq, tq), 2 * tq)
            for hh in range(hb):
                lanes = slice(hh * HEAD_PAD, (hh + 1) * HEAD_PAD)
                k, v = k_ref[:, lanes], v_ref[:, lanes]
                ps, dss = [], []
                for t in range(2):
                    rows = pl.ds(pl.multiple_of((i0 + t) * tq, tq), tq)
                    q, do = q_ref[rows, lanes], do_ref[rows, lanes]
                    sc = _dot_nt(q, k)
                    if first_masked and t == 0:
                        sc = jnp.where(_chunk_mask(tq, tq, 0), sc, NEG_BIG)
                    p = jnp.exp2(sc * SCALE_LOG2E - lse_ref[hh, rows, :])
                    ds = (p * (_dot_nt(do, v) - dl_ref[hh, rows, :])).astype(MXU)
                    dq_ref[rows, lanes] += _dot(ds, k) * ATTN_SCALE
                    ps.append(p.astype(MXU))
                    dss.append(ds)
                dv_sc[hh] += _dot_tn(jnp.concatenate(ps, axis=0), do_ref[both, lanes])
                dk_sc[hh] += _dot_tn(jnp.concatenate(dss, axis=0), q_ref[both, lanes])

        odd = (nq - j) % 2

        @pl.when(odd == 1)
        def _():
            tile(j, True)

        @pl.when(odd == 0)
        def _():
            tile_pair(j, True)

        first = j + 2 - odd

        def later_pair(pair, carry):
            tile_pair(first + 2 * pair, False)
            return carry

        lax.fori_loop(0, (nq - first) // 2, later_pair, 0)
        for hh in range(hb):
            lanes = slice(hh * HEAD_PAD, (hh + 1) * HEAD_PAD)
            dk_ref[:, lanes] = dk_sc[hh] * ATTN_SCALE
            dv_ref[:, lanes] = dv_sc[hh]
        @pl.when((g == groups - 1) & (j == nq - 1))
        def _():
            g_finish()
            s_finish()

    seq_spec = pl.BlockSpec((s, width), lambda g, j: (0, g))
    kv_spec = pl.BlockSpec((tq, width), lambda g, j: (j, g))
    col_spec = pl.BlockSpec((hb, s, 1), lambda g, j: (g, 0, 0))
    any_spec = pl.BlockSpec(memory_space=pl.ANY)
    outs = _call(
        body, name="attn_bwd", grid=(groups, nq), sem=("arbitrary", "arbitrary"),
        in_specs=[seq_spec, kv_spec, kv_spec, seq_spec, col_spec, col_spec] + [any_spec] * nc,
        out_specs=[seq_spec, kv_spec, kv_spec] + [any_spec] * nc,
        out_shape=[_sds(qp.shape, F32), _sds(qp.shape, F32), _sds(qp.shape, F32)]
        + [_scatter_out_shape(a, slot) for a, slot in scattered]
        + [_sds((N_DEV,) + a.shape, a.dtype) for a in gathered],
        scratch=[pltpu.VMEM((hb, tq, HEAD_PAD), F32), pltpu.VMEM((hb, tq, HEAD_PAD), F32)]
        + _comm_sems(ns) + _comm_sems(ng),
    )(qp, kp, vp, d_o, lse, delta, *[a for a, _ in scattered], *gathered)
    return outs[0], outs[1], outs[2], outs[3:3 + ns], outs[3 + ns:]


def _gmlp_bwd(z, d_sgu, ln_g, ln_b, w_sp, bias_exp, tm):
    s = z.shape[0]
    nblk = tm // GM_CHUNK

    def body(zu_ref, zv_ref, dsgu_ref, lg_ref, lb_ref, w_ref, be_ref,
             dguv_ref, dws_ref, dbs_ref, dlg_ref, dlb_ref, dbe_sc, dvln_sc, dlg_sc, dlb_sc):
        i = pl.program_id(0)

        @pl.when(i == 0)
        def _():
            for r in (dws_ref, dlg_sc, dlb_sc, dbe_sc):
                r[...] = jnp.zeros(r.shape, F32)

        seg = _seg_matrix()
        mask = _spatial_mask()
        wm = [(w_ref[h] * mask).astype(MXU) for h in range(HEADS)]
        zu, zv = zu_ref[...], zv_ref[...]
        gu = _gelu(zu)
        _, vhat, rstd = _gm_norm(zv, seg)
        lg = lg_ref[...]
        vln = (vhat * lg + lb_ref[...]).astype(MXU)
        d_sgu = dsgu_ref[...]
        for n in range(nblk):
            rows = slice(n * GM_CHUNK, (n + 1) * GM_CHUNK)
            vb = vln[rows]
            mixed = _gm_mix(wm, vb, GM_CHUNK) + be_ref[...]
            d_mixed = d_sgu[rows] * gu[rows]
            dguv_ref[rows, pl.ds(0, GM_WIDTH)] = ((d_sgu[rows] * mixed) * _gelu_grad(zu[rows])).astype(MXU)
            dbe_sc[...] += d_mixed
            dmb = d_mixed.astype(MXU)
            for p, (lanes, first) in enumerate(_gm_pairs(GM_CHUNK)):
                dm_pair, v_pair = dmb[:, lanes], vb[:, lanes]
                zero = jnp.zeros_like(dm_pair)
                dws_ref[2 * p] += _dot_nt(jnp.where(first, dm_pair, zero), v_pair)
                dws_ref[2 * p + 1] += _dot_nt(jnp.where(first, zero, dm_pair), v_pair)
            dvln_sc[rows, :] = _gm_mix(wm, dmb, GM_CHUNK, transposed=True)
        d_vln = dvln_sc[...]
        dlg_sc[...] += jnp.sum(d_vln * vhat, axis=0, keepdims=True)
        dlb_sc[...] += jnp.sum(d_vln, axis=0, keepdims=True)
        d_vhat = d_vln * lg
        d_gv = rstd * ((d_vhat - _split_dot(d_vhat, seg)) - vhat * _split_dot(d_vhat * vhat, seg))
        dguv_ref[:, pl.ds(GM_WIDTH, GM_WIDTH)] = (d_gv * _gelu_grad(zv)).astype(MXU)

        @pl.when(i == pl.num_programs(0) - 1)
        def _():
            for h in range(HEADS):
                dws_ref[h] = dws_ref[h] * mask
            hrow = lax.broadcasted_iota(jnp.int32, (HEADS, GM_WIDTH), 0)
            hlane = lax.broadcasted_iota(jnp.int32, (HEADS, GM_WIDTH), 1) >> 6
            ind = jnp.where(hrow == hlane, 1.0, 0.0).astype(MXU)
            acc = dbe_sc[...]
            hi = acc.astype(MXU)
            lo = (acc - hi.astype(F32)).astype(MXU)
            dbs_ref[...] = _dot_nt(ind, hi) + _dot_nt(ind, lo)
            pick = (lax.broadcasted_iota(jnp.int32, (GM_WIDTH, GM_DIM), 0) & (GM_DIM - 1)
                    == lax.broadcasted_iota(jnp.int32, (GM_WIDTH, GM_DIM), 1))
            pick = jnp.where(pick, 1.0, 0.0).astype(MXU)
            for src, dst in ((dlg_sc, dlg_ref), (dlb_sc, dlb_ref)):
                spread = jnp.where(hrow == hlane, jnp.broadcast_to(src[...], (HEADS, GM_WIDTH)), 0.0)
                dst[...] = _split_dot3(spread, pick)

    return _call(
        body, name="gmlp_bwd", grid=(s // tm,), sem=("arbitrary",),
        in_specs=[_rows(tm, GM_WIDTH, 1), _rows(tm, GM_WIDTH, 2), _rows(tm, GM_WIDTH), _full(ln_g.shape),
                  _full(ln_b.shape), _full(w_sp.shape), _full(bias_exp.shape)],
        out_specs=[_rows(tm, 2 * GM_WIDTH), _full(w_sp.shape), _full((HEADS, GM_CHUNK)), _full((HEADS, GM_DIM)),
                   _full((HEADS, GM_DIM))],
        out_shape=[_sds((s, 2 * GM_WIDTH), MXU), _sds(w_sp.shape, F32), _sds((HEADS, GM_CHUNK), F32),
                   _sds((HEADS, GM_DIM), F32), _sds((HEADS, GM_DIM), F32)],
        scratch=[pltpu.VMEM((GM_CHUNK, GM_WIDTH), F32), pltpu.VMEM((tm, GM_WIDTH), F32),
                 pltpu.VMEM((1, GM_WIDTH), F32), pltpu.VMEM((1, GM_WIDTH), F32)],
    )(z, z, d_sgu, ln_g, ln_b, w_sp, bias_exp)


def _mix_in_bwd(dq, dk, dv, z, d_guv, x, d_x_part, ada_raw, ada_b, g_pre, g_q, g_kv, w1t, wqt, wkv,
                cos_t, sin_t, tm):
    s = x.shape[0]
    hp = HEADS * HEAD_PAD
    za = Q_LORA + KV_LORA + HEAD_PAD

    def body(dq_ref, dk_ref, dv_ref, z_ref, dguv_ref, x_ref, dxp_ref, ar_ref, ab_ref, g_ref, gq_ref, gkv_ref,
             w1_ref, wq_ref, wkv_ref, cos_ref, sin_ref,
             gx_ref, dza_ref, dqp_ref, dkvp_ref, dsh_ref, dsc_ref, dg_ref, dgq_ref, dgkv_ref):
        i = pl.program_id(0)

        @pl.when(i == 0)
        def _():
            for r in (dsh_ref, dsc_ref, dg_ref, dgq_ref, dgkv_ref):
                r[...] = jnp.zeros(r.shape, F32)

        cos, sin = cos_ref[...], sin_ref[...]
        d_krot = jnp.zeros((tm, HEAD_PAD), F32)
        for h in range(HEADS):
            blk = slice(h * HEAD_PAD, (h + 1) * HEAD_PAD)
            dqp_ref[:, blk] = _rope_transposed(dq_ref[:, blk], cos, sin).astype(MXU)
            dk_h = dk_ref[:, blk]
            d_krot = d_krot + dk_h
            dkvp_ref[:, blk] = dk_h.astype(MXU)
        dkvp_ref[:, pl.ds(hp, hp)] = dv_ref[...].astype(MXU)
        lane = lax.broadcasted_iota(jnp.int32, (tm, HEAD_PAD), 1)
        d_kr = jnp.where((lane >= NOPE) & (lane < NOPE + ROPE), _rope_transposed(d_krot, cos, sin), 0.0)
        d_cqn = _dot(dqp_ref[...], wq_ref[...])
        d_ckvn = _dot_nt(dkvp_ref[...], wkv_ref[...])
        zt = z_ref[...]
        gq, gkv = gq_ref[...], gkv_ref[...]
        cq_hat, rq = _rms(zt[:, :Q_LORA])
        ckv_hat, rkv = _rms(zt[:, Q_LORA:Q_LORA + KV_LORA])
        dgq_ref[...] += jnp.sum(d_cqn * cq_hat, axis=0, keepdims=True)
        dgkv_ref[...] += jnp.sum(d_ckvn * ckv_hat, axis=0, keepdims=True)
        d_cq = _rms_bwd(d_cqn * gq, cq_hat, rq)
        d_ckv = _rms_bwd(d_ckvn * gkv, ckv_hat, rkv)
        d_za = jnp.concatenate([d_cq, d_ckv, d_kr], axis=1).astype(MXU)
        dza_ref[...] = d_za
        d_h1 = _dot(d_za, w1_ref[pl.ds(0, za), :]) + _dot(dguv_ref[...], w1_ref[pl.ds(za, 2 * GM_WIDTH), :])
        sc1 = _row(ar_ref, 1) + _row(ab_ref, 1)
        g = g_ref[...]
        xn, r1 = _rms(x_ref[...])
        dsh_ref[...] += jnp.sum(d_h1, axis=0, keepdims=True)
        dsc_ref[...] += jnp.sum(d_h1 * (xn * g), axis=0, keepdims=True)
        d_mod = d_h1 * (1.0 + sc1)
        dg_ref[...] += jnp.sum(d_mod * xn, axis=0, keepdims=True)
        gx_ref[...] = dxp_ref[...] + _rms_bwd(d_mod * g, xn, r1)

    vec = pl.BlockSpec((1, D_MODEL), lambda i: (0, 0))
    return _call(
        body, name="mix_in_bwd", grid=(s // tm,), sem=("arbitrary",),
        in_specs=[_rows(tm, hp), _rows(tm, hp), _rows(tm, hp), _rows(tm, za), _rows(tm, 2 * GM_WIDTH),
                  _rows(tm, D_MODEL), _rows(tm, D_MODEL), _full(ada_raw.shape), _full(ada_b.shape), _full(g_pre.shape),
                  _full(g_q.shape), _full(g_kv.shape), _full(w1t.shape), _full(wqt.shape),
                  _full(wkv.shape), _rows(tm, HEAD_PAD), _rows(tm, HEAD_PAD)],
        out_specs=[_rows(tm, D_MODEL), _rows(tm, za), _rows(tm, hp), _rows(tm, 2 * hp), vec, vec, vec,
                   _full(g_q.shape), _full(g_kv.shape)],
        out_shape=[_sds((s, D_MODEL), F32), _sds((s, za), MXU), _sds((s, hp), MXU), _sds((s, 2 * hp), MXU),
                   _sds((1, D_MODEL), F32), _sds((1, D_MODEL), F32), _sds((1, D_MODEL), F32),
                   _sds(g_q.shape, F32), _sds(g_kv.shape, F32)],
    )(dq, dk, dv, z, d_guv, x, d_x_part, ada_raw, ada_b, g_pre, g_q, g_kv, w1t, wqt, wkv, cos_t, sin_t)


def _tn_matmuls(arrays, pairs, name, ts):
    s = arrays[0].shape[0]
    steps = s // ts
    n_in, n_out = len(arrays), len(pairs)
    shapes = [(arrays[ia].shape[1], arrays[ib].shape[1]) for ia, ib in pairs]

    def body(*refs):
        ins, outs, accs = refs[:n_in], refs[n_in:n_in + n_out], refs[n_in + n_out:]
        k = pl.program_id(0)

        @pl.when(k == 0)
        def _():
            for acc in accs:
                acc[...] = jnp.zeros(acc.shape, F32)

        for (ia, ib), acc in zip(pairs, accs):
            acc[...] += _dot_tn(ins[ia][...], ins[ib][...])

        @pl.when(k == steps - 1)
        def _():
            for out, acc in zip(outs, accs):
                out[...] = acc[...].astype(MXU)

    return _call(
        body, name=name, grid=(steps,), sem=("arbitrary",),
        in_specs=[_rows(ts, a.shape[1]) for a in arrays],
        out_specs=[_full(shape) for shape in shapes],
        out_shape=[_sds(shape, MXU) for shape in shapes],
        scratch=[pltpu.VMEM(shape, F32) for shape in shapes],
    )(*arrays)


def _adamw(w, g, m, v):
    m2 = ADAM_B1 * m + (1.0 - ADAM_B1) * g
    v2 = ADAM_B2 * v + (1.0 - ADAM_B2) * (g * g)
    m_hat = m2 / (1.0 - ADAM_B1 ** ADAM_STEP)
    v_hat = v2 / (1.0 - ADAM_B2 ** ADAM_STEP)
    delta = -ADAM_LR * (m_hat / (jnp.sqrt(v_hat) + ADAM_EPS) + ADAM_WD * w)
    return delta, m2, v2


def _adam_reduce(recv, w, m, v, name):
    r, c = w.shape
    tr = r if r <= 512 else max(t for t in range(16, 513, 16) if r % t == 0)

    def body(p_ref, w_ref, m_ref, v_ref, g_ref, d_ref, mo_ref, vo_ref):
        g = p_ref[0].astype(F32)
        for j in range(1, N_DEV):
            g = g + p_ref[j].astype(F32)
        g_ref[...] = g
        d_ref[...], mo_ref[...], vo_ref[...] = _adamw(w_ref[...], g, m_ref[...], v_ref[...])

    blk = pl.BlockSpec((tr, c), lambda i: (i, 0))
    return _call(
        body, name=name, grid=(r // tr,), sem=("parallel",),
        in_specs=[pl.BlockSpec((N_DEV, tr, c), lambda i: (0, i, 0)), blk, blk, blk],
        out_specs=[blk] * 4, out_shape=[_sds((r, c), F32)] * 4,
    )(recv, w, m, v)


def _adam_w_ada(c_act_t, d_ada_cols, w, m, v):
    r, c = w.shape
    tr = 256

    def body(ct_ref, da_ref, w_ref, m_ref, v_ref, g_ref, d_ref, mo_ref, vo_ref):
        g = _dot(ct_ref[...], da_ref[...])
        g_ref[...] = g
        d_ref[...], mo_ref[...], vo_ref[...] = _adamw(w_ref[...], g, m_ref[...], v_ref[...])

    blk = pl.BlockSpec((tr, c), lambda i: (i, 0))
    return _call(
        body, name="adam_w_ada", grid=(r // tr,), sem=("parallel",),
        in_specs=[pl.BlockSpec((tr, c_act_t.shape[1]), lambda i: (i, 0)), _full(d_ada_cols.shape), blk, blk, blk],
        out_specs=[blk] * 4, out_shape=[_sds((r, c), F32)] * 4,
    )(c_act_t, d_ada_cols, w, m, v)


VEC_ROWS = D_MODEL // 128
PK_ADA = 0
PK_GAIN = PK_ADA + 6 * VEC_ROWS
PK_GQ = PK_GAIN + 4 * VEC_ROWS
PK_GKV = PK_GQ + Q_LORA // 128
PK_LOSS = PK_GKV + KV_LORA // 128
PK_LNG = 88
PK_LNB = PK_LNG + HEADS
PK_BS = PK_LNB + HEADS
PK_CB = PK_BS + HEADS
CB_ROWS = 6
PK_WS = PK_CB + N_DEV * CB_ROWS
PK_ROWS = PK_WS + HEADS * GM_CHUNK
assert PK_LOSS < PK_LNG and PK_ROWS % 8 == 0
LATE_GAIN = 2 * VEC_ROWS
LATE_GQ = 3 * VEC_ROWS
LATE_GKV = LATE_GQ + Q_LORA // 128
LATE_ROWS = 32


def _cb_chunks():
    return [(k, k * 128, min(128, FF_BLK - k * 128)) for k in range(CB_ROWS)]


def _put_rows(out_ref, row0, ref, width):
    for k in range(width // 128):
        out_ref[pl.ds(row0 + k, 1), :] = ref[:, pl.ds(k * 128, 128)]


def _pack_small(ada_rows, gains, loss_part, d_ln_g, d_ln_b, d_bs, d_cb, d_ws):
    half = N_DEV // 2

    def body(*refs):
        vec_refs = refs[:7]
        loss_ref, lng_ref, lnb_ref, bs_ref, cb_ref, ws_ref, out_ref = refs[7:]
        out_ref[pl.ds(0, PK_WS), :] = jnp.zeros((PK_WS, 128), F32)
        for n, ref in enumerate(vec_refs[:4]):
            _put_rows(out_ref, PK_ADA + (2 + n) * VEC_ROWS, ref, D_MODEL)
        for n, ref in enumerate(vec_refs[4:]):
            _put_rows(out_ref, PK_GAIN + (1 + n) * VEC_ROWS, ref, D_MODEL)
        _put_rows(out_ref, PK_LOSS, loss_ref, 128)
        out_ref[pl.ds(PK_LNG, HEADS), pl.ds(0, GM_DIM)] = lng_ref[...]
        out_ref[pl.ds(PK_LNB, HEADS), pl.ds(0, GM_DIM)] = lnb_ref[...]
        out_ref[pl.ds(PK_BS, HEADS), :] = bs_ref[...]
        for j in range(N_DEV):
            for k, lane, width in _cb_chunks():
                out_ref[pl.ds(PK_CB + j * CB_ROWS + k, 1), pl.ds(0, width)] = cb_ref[j % half, j // half, :, pl.ds(lane, width)]
        for h in range(HEADS):
            out_ref[pl.ds(PK_WS + h * GM_CHUNK, GM_CHUNK), :] = ws_ref[h]

    ins = list(ada_rows) + list(gains) + [loss_part, d_ln_g, d_ln_b, d_bs, d_cb, d_ws]
    return _call(body, name="pack_small", grid=(1,), in_specs=[_full(a.shape) for a in ins],
                 out_specs=_full((PK_ROWS, 128)), out_shape=_sds((PK_ROWS, 128), F32))(*ins)


def _pack_late(d_sh1, d_sc1, d_g_pre_mix, d_g_q, d_g_kv):
    def body(sh_ref, sc_ref, g_ref, gq_ref, gkv_ref, out_ref):
        out_ref[...] = jnp.zeros((LATE_ROWS, 128), F32)
        _put_rows(out_ref, 0, sh_ref, D_MODEL)
        _put_rows(out_ref, VEC_ROWS, sc_ref, D_MODEL)
        _put_rows(out_ref, LATE_GAIN, g_ref, D_MODEL)
        _put_rows(out_ref, LATE_GQ, gq_ref, Q_LORA)
        _put_rows(out_ref, LATE_GKV, gkv_ref, KV_LORA)

    ins = [d_sh1, d_sc1, d_g_pre_mix, d_g_q, d_g_kv]
    return _call(body, name="pack_late", grid=(1,), in_specs=[_full(a.shape) for a in ins],
                 out_specs=_full((LATE_ROWS, 128)), out_shape=_sds((LATE_ROWS, 128), F32))(*ins)


def _adam_small(gathered, late, params):
    n_par = len(params)

    def body(p_ref, late_ref, *refs):
        ins = [refs[3 * n:3 * n + 3] for n in range(n_par)]
        outs = [refs[3 * n_par + 4 * n:3 * n_par + 4 * n + 4] for n in range(n_par)]
        loss_ref, dada_ref = refs[7 * n_par:]

        def total(rows, lanes=slice(None), src=p_ref):
            g = src[0, rows, lanes]
            for j in range(1, N_DEV):
                g = g + src[j, rows, lanes]
            return g

        def apply(n, g, idx):
            w_ref, m_ref, v_ref = ins[n]
            d, m2, v2 = _adamw(w_ref[idx], g, m_ref[idx], v_ref[idx])
            for ref, val in zip(outs[n], (g, d, m2, v2)):
                ref[idx] = val

        def vector(n, src, row0, width, lane0=0):
            for k in range(width // 128):
                apply(n, total(pl.ds(row0 + k, 1), src=src), (slice(None), pl.ds(lane0 + k * 128, 128)))

        vector(0, late_ref, 0, 2 * D_MODEL)
        vector(0, p_ref, PK_ADA + 2 * VEC_ROWS, 4 * D_MODEL, lane0=2 * D_MODEL)
        vector(1, late_ref, LATE_GAIN, D_MODEL)
        for n in range(1, 4):
            vector(1 + n, p_ref, PK_GAIN + n * VEC_ROWS, D_MODEL)
        vector(5, late_ref, LATE_GQ, Q_LORA)
        vector(6, late_ref, LATE_GKV, KV_LORA)
        apply(7, total(pl.ds(PK_LNG, HEADS), pl.ds(0, GM_DIM)), (0,))
        apply(8, total(pl.ds(PK_LNB, HEADS), pl.ds(0, GM_DIM)), (0,))
        for h in range(HEADS):
            apply(9, total(pl.ds(PK_WS + h * GM_CHUNK, GM_CHUNK)), (0, h))
        apply(10, total(pl.ds(PK_BS, HEADS)), (0,))
        for j in range(N_DEV):
            for k, lane, width in _cb_chunks():
                apply(11, total(pl.ds(PK_CB + j * CB_ROWS + k, 1), pl.ds(0, width)), (pl.ds(j, 1), pl.ds(lane, width)))
        loss_ref[...] = total(pl.ds(PK_LOSS, 1))
        dada_ref[:, pl.ds(0, 2 * VEC_ROWS), :] = late_ref[:, pl.ds(0, 2 * VEC_ROWS), :]
        dada_ref[:, pl.ds(2 * VEC_ROWS, 4 * VEC_ROWS), :] = p_ref[:, pl.ds(PK_ADA + 2 * VEC_ROWS, 4 * VEC_ROWS), :]

    flat = [a for triple in params for a in triple]
    out_shape = [_sds(w.shape, F32) for w, _, _ in params for _ in range(4)]
    out_shape += [_sds((1, 128), F32), _sds((N_DEV, 6 * VEC_ROWS, 128), F32)]
    outs = _call(body, name="adam_small", grid=(1,),
                 in_specs=[_full(gathered.shape), _full(late.shape)] + [_full(a.shape) for a in flat],
                 out_specs=[_full(o.shape) for o in out_shape], out_shape=out_shape)(gathered, late, *flat)
    return [tuple(outs[4 * n:4 * n + 4]) for n in range(n_par)], outs[-2], outs[-1]


def _rope_tables(s):
    pos = jnp.arange(s, dtype=F32)
    inv = ROPE_THETA ** (-jnp.arange(0, ROPE, 2, dtype=F32) / ROPE)
    lane_inv = jnp.concatenate([jnp.zeros((NOPE,), F32), inv, inv, jnp.zeros((HEAD_PAD - NOPE - ROPE,), F32)])
    ang = pos[:, None] * lane_inv[None, :]
    return jnp.cos(ang), jnp.sin(ang)


def kernel(x, c, w_ada, b_ada, g_pre_mix, g_post_mix, w_in, g_q, w_uq, g_kv, w_ukv, gm_ln_g, gm_ln_b, w_spatial, b_spatial, w_out, g_pre_ffn, g_post_ffn, w_up, conv_w, conv_b, w_down, loss_target, m_w_ada, m_b_ada, m_g_pre_mix, m_g_post_mix, m_w_in, m_g_q, m_w_uq, m_g_kv, m_w_ukv, m_gm_ln_g, m_gm_ln_b, m_w_spatial, m_b_spatial, m_w_out, m_g_pre_ffn, m_g_post_ffn, m_w_up, m_conv_w, m_conv_b, m_w_down, v_w_ada, v_b_ada, v_g_pre_mix, v_g_post_mix, v_w_in, v_g_q, v_w_uq, v_g_kv, v_w_ukv, v_gm_ln_g, v_gm_ln_b, v_w_spatial, v_b_spatial, v_w_out, v_g_pre_ffn, v_g_post_ffn, v_w_up, v_conv_w, v_conv_b, v_w_down):
    s = x.shape[1]
    tm = min(512, s)
    tg = min(1024, s)
    tf = min(2 * ROW_SUB, s)
    tq = min(512, s)
    ts = min(2048, s)
    hp = HEADS * HEAD_PAD
    half = N_DEV // 2
    my_slot = 4 * lax.axis_index("x") + 2 * lax.axis_index("y") + lax.axis_index("c")
    x2d, target = x[0], loss_target[0]

    def t_(a):
        return jnp.swapaxes(a[0], 0, 1)

    w_in_t, m_in_t, v_in_t = t_(w_in), t_(m_w_in), t_(v_w_in)
    w_uq_t, m_uq_t, v_uq_t = t_(w_uq), t_(m_w_uq), t_(v_w_uq)
    w_up_t, m_up_t, v_up_t = t_(w_up), t_(m_w_up), t_(v_w_up)
    (g_c, g_in_t, g_uq_t, g_ukv, g_cw), _ = _exchange(
        [c, w_in_t.astype(MXU), w_uq_t.astype(MXU), w_ukv[0].astype(MXU), conv_w[0]], [], "gather_mixer_weights")

    w_in_f = g_in_t.reshape(-1, D_MODEL)
    o1, o2, o3 = Q_LORA, Q_LORA + KV_LORA, Q_LORA + KV_LORA + ROPE
    w1t = jnp.concatenate([w_in_f[:o2], jnp.zeros((NOPE, D_MODEL), MXU), w_in_f[o2:o3],
                           jnp.zeros((HEAD_PAD - NOPE - ROPE, D_MODEL), MXU), w_in_f[o3:]], axis=0)
    wqt = jnp.pad(g_uq_t, ((0, 0), (0, HEAD_PAD - NOPE - ROPE), (0, 0))).reshape(hp, Q_LORA)
    w_ukv_f = jnp.transpose(g_ukv, (1, 0, 2)).reshape(KV_LORA, HEADS, 2 * NOPE)
    pad_head = ((0, 0), (0, 0), (0, HEAD_PAD - NOPE))
    wkv = jnp.concatenate([jnp.pad(w_ukv_f[:, :, :NOPE], pad_head).reshape(KV_LORA, hp),
                           jnp.pad(w_ukv_f[:, :, NOPE:], pad_head).reshape(KV_LORA, hp)], axis=1)
    cb8 = conv_b.reshape(N_DEV, 1, FF_BLK)
    bias_exp = jnp.repeat(b_spatial[0].T, GM_DIM, axis=1)
    ln_g, ln_b = gm_ln_g.reshape(1, GM_WIDTH), gm_ln_b.reshape(1, GM_WIDTH)
    w_sp = w_spatial[0]
    cos_t, sin_t = _rope_tables(s)

    ada_part, c_act = _ada_fwd(g_c.reshape(N_DEV, D_MODEL), w_ada[0])
    _, (ada_recv,) = _exchange([], [(ada_part.reshape(N_DEV, 1, -1), _plain_slot)], "ada_rows")
    ada_raw = ada_recv.reshape(6, D_MODEL)
    ada_b = b_ada.reshape(6, D_MODEL)

    h1, z, qp, kp, vp, cqn, ckvn = _mix_in_fwd(x2d, ada_raw, ada_b, g_pre_mix, w1t, g_q, g_kv, wqt, wkv, cos_t, sin_t, tm)
    sgu = _gmlp_fwd(z, ln_g, ln_b, w_sp, bias_exp, tg)
    o_pad, lse, (g_out, g_up, g_down) = _attn_fwd(
        qp, kp, vp, tq, [w_out[0].astype(MXU), w_up_t.astype(MXU), w_down[0].astype(MXU)])
    w_out_f = g_out.reshape(2 * GM_WIDTH, D_MODEL)
    wo_attn = jnp.pad(w_out_f[:GM_WIDTH].reshape(HEADS, NOPE, D_MODEL), ((0, 0), (0, HEAD_PAD - NOPE), (0, 0)))
    wo = jnp.concatenate([wo_attn.reshape(hp, D_MODEL), w_out_f[GM_WIDTH:]], axis=0)
    wd = g_down.reshape(half, FF_BLK, D_MODEL)
    m_mix, x2, h2 = _out_proj_fwd(o_pad, sgu, wo, x2d, ada_raw, ada_b, g_post_mix, g_pre_ffn, tg)
    up_a, up_b, y_a, y_b, act = _ffn_up_fwd(h2, g_up, g_cw, cb8, tf)
    d_out, d_f, loss_part, d_gt2, d_g_post_ffn = _ffn_down_fwd(act, wd, x2, target, ada_raw, ada_b, g_post_ffn, tf)

    d_up, d_cw, d_cb, p_down, p_up = _ffn_down_bwd(d_f, wd, up_a, up_b, y_a, y_b, g_cw, act, h2, tf)
    p_down = p_down.reshape(N_DEV, -1, D_MODEL)
    d_x2, d_m, d_sh2, d_sc2, d_g_pre_ffn, d_gt1, d_g_post_mix = _ffn_up_bwd(
        d_up, g_up, x2, m_mix, d_out, ada_raw, ada_b, g_pre_ffn, g_post_mix, tm)
    dwo_attn, dwo_sgu = _tn_matmuls([o_pad, sgu, d_m], [(0, 2), (1, 2)], "dw_out", ts)
    dwo_attn = dwo_attn.reshape(HEADS, HEAD_PAD, D_MODEL)[:, :NOPE]
    p_out = jnp.concatenate([dwo_attn.reshape(GM_WIDTH, D_MODEL), dwo_sgu], axis=0).reshape(N_DEV, -1, D_MODEL)
    d_o, d_sgu, delta = _out_proj_bwd(d_m, wo, o_pad, tg)
    d_guv, d_ws, d_bs, d_ln_g, d_ln_b = _gmlp_bwd(z, d_sgu, ln_g, ln_b, w_sp, bias_exp, tg)
    packed = _pack_small([d_gt1, d_sh2, d_sc2, d_gt2], [d_g_post_mix, d_g_pre_ffn, d_g_post_ffn], loss_part,
                         d_ln_g, d_ln_b, d_bs, d_cb, d_ws)

    def ffn_slot(j):
        return (j % half, j // half)

    dq, dk, dv, (r_out, r_up, r_down, r_cw), (g_small,) = _attn_bwd(
        qp, kp, vp, d_o, lse, delta, tq,
        [(p_out, _plain_slot), (p_up, ffn_slot), (p_down, _plain_slot), (d_cw, ffn_slot)], [packed])
    grad_x, d_za, d_qp, d_kvp, d_sh1, d_sc1, d_g_pre_mix, d_g_q, d_g_kv = _mix_in_bwd(
        dq, dk, dv, z, d_guv, x2d, d_x2, ada_raw, ada_b, g_pre_mix, g_q, g_kv, w1t, wqt, wkv, cos_t, sin_t,
        min(256, s))
    dw1a, dw1b, dwq, dwkv = _tn_matmuls([d_za, d_guv, h1, d_qp, cqn, ckvn, d_kvp],
                                        [(0, 2), (1, 2), (3, 4), (5, 6)], "dw_mixer", ts // 2)
    d_w_in_t = jnp.concatenate([dw1a[:o2], dw1a[o2 + NOPE:o2 + NOPE + ROPE], dw1b], axis=0)
    p_in = d_w_in_t.reshape(N_DEV, -1, D_MODEL)
    p_uq = dwq.reshape(HEADS, HEAD_PAD, Q_LORA)[:, :NOPE + ROPE]
    dwk = dwkv[:, :hp].reshape(KV_LORA, HEADS, HEAD_PAD)[:, :, :NOPE]
    dwv = dwkv[:, hp:].reshape(KV_LORA, HEADS, HEAD_PAD)[:, :, :NOPE]
    p_ukv = jnp.transpose(jnp.concatenate([dwk, dwv], axis=2), (1, 0, 2))

    (g_late,), (r_in, r_uq, r_ukv) = _exchange(
        [_pack_late(d_sh1, d_sc1, d_g_pre_mix, d_g_q, d_g_kv)],
        [(p_in, _plain_slot), (p_uq, _plain_slot), (p_ukv, _plain_slot)], "final_exchange")
    small_params = [(b_ada, m_b_ada, v_b_ada), (g_pre_mix, m_g_pre_mix, v_g_pre_mix),
                    (g_post_mix, m_g_post_mix, v_g_post_mix), (g_pre_ffn, m_g_pre_ffn, v_g_pre_ffn),
                    (g_post_ffn, m_g_post_ffn, v_g_post_ffn), (g_q, m_g_q, v_g_q), (g_kv, m_g_kv, v_g_kv),
                    (gm_ln_g, m_gm_ln_g, v_gm_ln_g), (gm_ln_b, m_gm_ln_b, v_gm_ln_b),
                    (w_spatial, m_w_spatial, v_w_spatial), (b_spatial, m_b_spatial, v_b_spatial),
                    tuple(a.reshape(N_DEV, FF_BLK) for a in (conv_b, m_conv_b, v_conv_b))]
    small_out, loss_row, d_ada_all = _adam_small(g_small, g_late, small_params)
    small_out[11] = tuple(o.reshape(conv_b.shape) for o in small_out[11])
    loss = loss_row[0, 0]

    def big(recv, w, m, v, name):
        g, d, m2, v2 = _adam_reduce(recv, w[0], m[0], v[0], name)
        return g[None], d[None], m2[None], v2[None]

    def big_t(recv, w_t, m_t, v_t, name):
        return tuple(jnp.swapaxes(o, 0, 1)[None] for o in _adam_reduce(recv, w_t, m_t, v_t, name))

    a_in = big_t(r_in, w_in_t, m_in_t, v_in_t, "adam_w_in")
    a_uq = big_t(r_uq, w_uq_t, m_uq_t, v_uq_t, "adam_w_uq")
    a_ukv = big(r_ukv, w_ukv, m_w_ukv, v_w_ukv, "adam_w_ukv")
    a_out = big(r_out, w_out, m_w_out, v_w_out, "adam_w_out")
    a_up = big_t(r_up, w_up_t, m_up_t, v_up_t, "adam_w_up")
    a_down = big(r_down, w_down, m_w_down, v_w_down, "adam_w_down")
    ada_cols = w_ada.shape[2]
    d_ada_cols = lax.dynamic_slice(d_ada_all.reshape(N_DEV, 6 * D_MODEL), (0, my_slot * ada_cols), (N_DEV, ada_cols))
    pad_seq = 128 - N_DEV
    a_ada = tuple(t[None] for t in _adam_w_ada(jnp.pad(c_act.T, ((0, 0), (0, pad_seq))).astype(MXU),
                                               jnp.pad(d_ada_cols, ((0, pad_seq), (0, 0))).astype(MXU),
                                               w_ada[0], m_w_ada[0], v_w_ada[0]))
    a_cw = big(r_cw, conv_w, m_conv_w, v_conv_w, "adam_conv_w")

    def small(k):
        return small_out[k]

    per_weight = [a_ada, small(0), small(1), small(2), a_in, small(5), a_uq, small(6), a_ukv, small(7), small(8),
                  small(9), small(10), a_out, small(3), small(4), a_up, a_cw, small(11), a_down]
    outs = [loss, grad_x[None]]
    for k in range(4):
        outs += [t[k] for t in per_weight]
    return tuple(outs)
```

```python
import functools

import jax
import jax.numpy as jnp
from jax import lax
from jax.experimental import pallas as pl
from jax.experimental.pallas import tpu as pltpu

F32 = jnp.float32
MXU = jnp.bfloat16

N_DEV = 8
D_MODEL = 1024
HEADS = 8
HEAD_PAD = 128
NOPE = 64
ROPE = 32
Q_LORA = 256
KV_LORA = 128
GM_WIDTH = 512
GM_DIM = 64
GM_CHUNK = 128
CHUNK_SHIFT = 6
ROPE_THETA = 10000.0
ATTN_SCALE = (NOPE + ROPE) ** -0.5
LOG2E = 1.4426950408889634
SCALE_LOG2E = ATTN_SCALE * LOG2E
Z_COLS = 1536
FF_BLK = 704
EPS = 1e-6
ADAM_LR = 0.001
ADAM_B1 = 0.9
ADAM_B2 = 0.999
ADAM_EPS = 1e-08
ADAM_WD = 0.01
ADAM_STEP = 10
VMEM_LIMIT = 56 * 1024 * 1024
MESH = pl.DeviceIdType.MESH


def _dot(a, b):
    return jnp.dot(a, b, preferred_element_type=F32)


def _dot_nt(a, b):
    return lax.dot_general(a, b, (((1,), (1,)), ((), ())), preferred_element_type=F32)


def _dot_tn(a, b):
    return lax.dot_general(a, b, (((0,), (0,)), ((), ())), preferred_element_type=F32)


def _call(body, *, name, grid, in_specs, out_specs, out_shape, scratch=(), sem=None):
    params = pltpu.CompilerParams(dimension_semantics=sem, vmem_limit_bytes=VMEM_LIMIT)
    return pl.pallas_call(body, name=name, grid=grid, in_specs=in_specs, out_specs=out_specs,
                          out_shape=out_shape, scratch_shapes=list(scratch), compiler_params=params)


def _full(shape):
    n = len(shape)
    return pl.BlockSpec(shape, lambda *_: (0,) * n)


def _rows(tm, cols, col_block=0):
    return pl.BlockSpec((tm, cols), lambda i: (i, col_block))


def _sds(shape, dtype):
    return jax.ShapeDtypeStruct(shape, dtype)


def _row(ref, k):
    return ref[pl.ds(k, 1), :]


def _rms(x):
    r = lax.rsqrt(jnp.mean(x * x, axis=-1, keepdims=True) + EPS)
    return x * r, r


def _rms_bwd(d_hat, hat, r):
    return r * (d_hat - hat * jnp.mean(d_hat * hat, axis=-1, keepdims=True))


def _rope_partner(t):
    lane = lax.broadcasted_iota(jnp.int32, t.shape, 1)
    swapped = jnp.where(lane < NOPE + ROPE // 2, -pltpu.roll(t, HEAD_PAD - ROPE // 2, 1), pltpu.roll(t, ROPE // 2, 1))
    return jnp.where((lane >= NOPE) & (lane < NOPE + ROPE), swapped, 0.0)


def _rope(t, cos, sin):
    return t * cos + _rope_partner(t) * sin


def _rope_transposed(g, cos, sin):
    return g * cos - _rope_partner(g * sin)


def _gelu(x):
    return x * (0.5 * (1.0 + jnp.tanh(0.7978845608028654 * (x + 0.044715 * (x * x * x)))))


def _gelu_grad(x):
    t = jnp.tanh(0.7978845608028654 * (x + 0.044715 * (x * x * x)))
    return 0.5 * (1.0 + t) + 0.5 * x * (1.0 - t * t) * (0.7978845608028654 * (1.0 + 3.0 * 0.044715 * (x * x)))


def _split_dot(x, mat):
    hi = x.astype(MXU)
    lo = (x - hi.astype(F32)).astype(MXU)
    return _dot(hi, mat) + _dot(lo, mat)


def _split_dot3(x, mat):
    hi = x.astype(MXU)
    r1 = x - hi.astype(F32)
    mid = r1.astype(MXU)
    lo = (r1 - mid.astype(F32)).astype(MXU)
    return (_dot(hi, mat) + _dot(mid, mat)) + _dot(lo, mat)


def _seg_matrix():
    r = lax.broadcasted_iota(jnp.int32, (GM_WIDTH, GM_WIDTH), 0) >> 6
    c = lax.broadcasted_iota(jnp.int32, (GM_WIDTH, GM_WIDTH), 1) >> 6
    return jnp.where(r == c, 1.0 / GM_DIM, 0.0).astype(MXU)


def _spatial_mask():
    i = lax.broadcasted_iota(jnp.int32, (GM_CHUNK, GM_CHUNK), 0) >> CHUNK_SHIFT
    j = lax.broadcasted_iota(jnp.int32, (GM_CHUNK, GM_CHUNK), 1) >> CHUNK_SHIFT
    return (j <= i).astype(F32)


def _my_place():
    return lax.axis_index("x"), lax.axis_index("y"), lax.axis_index("c")


def _flat(p):
    return 4 * p[0] + 2 * p[1] + p[2]


def _comm_sems(n):
    return [pltpu.SemaphoreType.DMA((7 * n,)), pltpu.SemaphoreType.DMA((7 * n,)), pltpu.SemaphoreType.DMA((n,))]


def _gather_steps(ins, outs, sems):
    send_sems, recv_sems, local_sems = sems
    n = len(ins)
    x, y, c = _my_place()
    me, sibling = (x, y, c), (x, y, 1 - c)
    chips = [(1 - x, y), (x, 1 - y), (1 - x, 1 - y)]

    def copy(a, k, block, to, src=None):
        slot = outs[a].at[_flat(block)]
        return pltpu.make_async_remote_copy(
            src_ref=slot if src is None else src, dst_ref=slot,
            send_sem=send_sems.at[7 * a + k], recv_sem=recv_sems.at[7 * a + k],
            device_id=to, device_id_type=MESH)

    def mine():
        return [pltpu.make_async_copy(ins[a], outs[a].at[_flat(me)], local_sems.at[a]) for a in range(n)]

    def first():
        cps = []
        for a in range(n):
            cps.append(copy(a, 0, me, sibling, src=ins[a]))
            cps += [copy(a, 1 + j, me, (*chip, c), src=ins[a]) for j, chip in enumerate(chips)]
        return cps

    def passed():
        return [copy(a, 4 + j, (*chip, c), sibling) for a in range(n) for j, chip in enumerate(chips)]

    def start():
        for cp in mine() + first():
            cp.start()

    def forward():
        for a in range(n):
            for j, chip in enumerate(chips):
                copy(a, 1 + j, (*chip, c), me).wait_recv()
                copy(a, 4 + j, (*chip, c), sibling).start()

    def finish():
        for a in range(n):
            copy(a, 0, sibling, me).wait_recv()
            for j, chip in enumerate(chips):
                copy(a, 4 + j, (*chip, 1 - c), me).wait_recv()
        for cp in first() + passed():
            cp.wait_send()
        for cp in mine():
            cp.wait()

    return start, forward, finish


def _scatter_steps(ins, outs, sems, slots):
    send_sems, recv_sems, local_sems = sems
    n = len(ins)
    flips = [(fx, fy, fc) for fx in (0, 1) for fy in (0, 1) for fc in (0, 1)][1:]
    me = _my_place()

    def peer(f):
        return tuple(1 - v if b else v for v, b in zip(me, f))

    def copy(a, k, arriving=False):
        p = peer(flips[k])
        return pltpu.make_async_remote_copy(
            src_ref=ins[a].at[slots[a](_flat(p))], dst_ref=outs[a].at[_flat(p if arriving else me)],
            send_sem=send_sems.at[7 * a + k], recv_sem=recv_sems.at[7 * a + k],
            device_id=p, device_id_type=MESH)

    def mine():
        return [pltpu.make_async_copy(ins[a].at[slots[a](_flat(me))], outs[a].at[_flat(me)], local_sems.at[a])
                for a in range(n)]

    def start():
        for cp in mine() + [copy(a, k) for a in range(n) for k in range(7)]:
            cp.start()

    def finish():
        for a in range(n):
            for k in range(7):
                copy(a, k, arriving=True).wait_recv()
        for a in range(n):
            for k in range(7):
                copy(a, k).wait_send()
        for cp in mine():
            cp.wait()

    return start, finish


def _plain_slot(j):
    return (j,)


def _scatter_out_shape(arr, slot):
    return _sds((N_DEV,) + arr.shape[len(slot(0)):], arr.dtype)


def _exchange(gathered, scattered, name):
    ng, ns = len(gathered), len(scattered)
    slots = [slot for _, slot in scattered]

    def body(*refs):
        g_in, s_in = refs[:ng], refs[ng:ng + ns]
        g_out, s_out = refs[ng + ns:2 * ng + ns], refs[2 * ng + ns:2 * (ng + ns)]
        sems = refs[2 * (ng + ns):]
        g_start, g_forward, g_finish = _gather_steps(g_in, g_out, sems[:3])
        s_start, s_finish = _scatter_steps(s_in, s_out, sems[3:], slots)
        g_start()
        s_start()
        g_forward()
        g_finish()
        s_finish()

    any_spec = pl.BlockSpec(memory_space=pl.ANY)
    outs = pl.pallas_call(
        body, name=name,
        in_specs=[any_spec] * (ng + ns), out_specs=[any_spec] * (ng + ns),
        out_shape=[_sds((N_DEV,) + a.shape, a.dtype) for a in gathered]
        + [_scatter_out_shape(a, slot) for a, slot in scattered],
        scratch_shapes=_comm_sems(max(ng, 1)) + _comm_sems(max(ns, 1)),
    )(*gathered, *[a for a, _ in scattered])
    return outs[:ng], outs[ng:]


def _ada_fwd(c_all, w_ada):
    def body(c_ref, w_ref, part_ref, act_ref):
        cv = c_ref[...]
        act = cv * jax.nn.sigmoid(cv)
        act_ref[...] = act
        part_ref[...] = _dot(act.astype(MXU), w_ref[...].astype(MXU))

    cols = w_ada.shape[1]
    return _call(body, name="ada_fwd", grid=(1,),
                 in_specs=[_full(c_all.shape), _full(w_ada.shape)],
                 out_specs=[_full((N_DEV, cols)), _full(c_all.shape)],
                 out_shape=[_sds((N_DEV, cols), F32), _sds(c_all.shape, F32)])(c_all, w_ada)


def _mix_in_fwd(x, ada_raw, ada_b, g_pre, w1, g_q, g_kv, wq, wkv, cos_t, sin_t, tm):
    s = x.shape[0]

    def body(x_ref, ar_ref, ab_ref, g_ref, w1_ref, gq_ref, gkv_ref, wq_ref, wkv_ref, cos_ref, sin_ref,
             h1_ref, z_ref, qp_ref, kp_ref, vp_ref, cqn_ref, ckvn_ref, w1_n, wq_n):
        @pl.when(pl.program_id(0) == 0)
        def _():
            w1_n[...] = w1_ref[...].T
            wq_n[...] = wq_ref[...].T

        sh = _row(ar_ref, 0) + _row(ab_ref, 0)
        sc = _row(ar_ref, 1) + _row(ab_ref, 1)
        xn, _ = _rms(x_ref[...])
        hb = ((xn * g_ref[...]) * (1.0 + sc) + sh).astype(MXU)
        h1_ref[...] = hb
        z = _dot(hb, w1_n[...])
        z_ref[...] = z
        cos, sin = cos_ref[...], sin_ref[...]
        cqn = (_rms(z[:, :Q_LORA])[0] * gq_ref[...]).astype(MXU)
        ckvn = (_rms(z[:, Q_LORA:Q_LORA + KV_LORA])[0] * gkv_ref[...]).astype(MXU)
        cqn_ref[...] = cqn
        ckvn_ref[...] = ckvn
        q = _dot(cqn, wq_n[...])
        kv = _dot(ckvn, wkv_ref[...])
        k_rope = _rope(z[:, Q_LORA + KV_LORA:Q_LORA + KV_LORA + HEAD_PAD], cos, sin)
        for h in range(HEADS):
            blk = slice(h * HEAD_PAD, (h + 1) * HEAD_PAD)
            qp_ref[:, blk] = _rope(q[:, blk], cos, sin).astype(MXU)
            kp_ref[:, blk] = (kv[:, blk] + k_rope).astype(MXU)
        v_lane = lax.broadcasted_iota(jnp.int32, (tm, HEADS * HEAD_PAD), 1) & (HEAD_PAD - 1)
        vp_ref[...] = jnp.where(v_lane == NOPE, 1.0, kv[:, HEADS * HEAD_PAD:]).astype(MXU)

    hp = HEADS * HEAD_PAD
    return _call(
        body, name="mix_in_fwd", grid=(s // tm,), sem=("arbitrary",),
        in_specs=[_rows(tm, D_MODEL), _full(ada_raw.shape), _full(ada_b.shape), _full(g_pre.shape), _full(w1.shape),
                  _full(g_q.shape), _full(g_kv.shape), _full(wq.shape), _full(wkv.shape),
                  _rows(tm, HEAD_PAD), _rows(tm, HEAD_PAD)],
        out_specs=[_rows(tm, D_MODEL), _rows(tm, Z_COLS), _rows(tm, hp), _rows(tm, hp), _rows(tm, hp),
                   _rows(tm, Q_LORA), _rows(tm, KV_LORA)],
        out_shape=[_sds((s, D_MODEL), MXU), _sds((s, Z_COLS), F32), _sds((s, hp), MXU), _sds((s, hp), MXU),
                   _sds((s, hp), MXU), _sds((s, Q_LORA), MXU), _sds((s, KV_LORA), MXU)],
        scratch=[pltpu.VMEM(w1.shape[::-1], MXU), pltpu.VMEM(wq.shape[::-1], MXU)],
    )(x, ada_raw, ada_b, g_pre, w1, g_q, g_kv, wq, wkv, cos_t, sin_t)


def _gm_norm(zv, seg):
    gv = _gelu(zv)
    cen = gv - _split_dot(gv, seg)
    rstd = lax.rsqrt(_split_dot(cen * cen, seg) + EPS)
    return gv, cen * rstd, rstd


def _gm_pairs(rows):
    first = lax.broadcasted_iota(jnp.int32, (rows, 2 * GM_DIM), 1) < GM_DIM
    return [(slice(p * 2 * GM_DIM, (p + 1) * 2 * GM_DIM), first) for p in range(HEADS // 2)]


def _gm_mix(wm, vb, rows, transposed=False):
    dot = _dot_tn if transposed else _dot
    return jnp.concatenate([jnp.where(first, dot(wm[2 * p], vb[:, lanes]), dot(wm[2 * p + 1], vb[:, lanes]))
                            for p, (lanes, first) in enumerate(_gm_pairs(rows))], axis=1)


def _gmlp_fwd(z, ln_g, ln_b, w_sp, bias_exp, tm):
    s = z.shape[0]
    nblk = tm // GM_CHUNK

    def body(zu_ref, zv_ref, lg_ref, lb_ref, w_ref, be_ref, sgu_ref):
        seg = _seg_matrix()
        mask = _spatial_mask()
        wm = [(w_ref[h] * mask).astype(MXU) for h in range(HEADS)]
        gu = _gelu(zu_ref[...])
        _, vhat, _ = _gm_norm(zv_ref[...], seg)
        vln = (vhat * lg_ref[...] + lb_ref[...]).astype(MXU)
        for n in range(nblk):
            rows = slice(n * GM_CHUNK, (n + 1) * GM_CHUNK)
            mixed = _gm_mix(wm, vln[rows], GM_CHUNK) + be_ref[...]
            sgu_ref[rows, :] = (gu[rows] * mixed).astype(MXU)

    return _call(
        body, name="gmlp_fwd", grid=(s // tm,), sem=("parallel",),
        in_specs=[_rows(tm, GM_WIDTH, 1), _rows(tm, GM_WIDTH, 2), _full(ln_g.shape), _full(ln_b.shape),
                  _full(w_sp.shape), _full(bias_exp.shape)],
        out_specs=_rows(tm, GM_WIDTH), out_shape=_sds((s, GM_WIDTH), MXU),
    )(z, z, ln_g, ln_b, w_sp, bias_exp)


def _chunk_mask(n_q, n_k, q_off):
    qc = (q_off + lax.broadcasted_iota(jnp.int32, (n_q, n_k), 0)) >> CHUNK_SHIFT
    kc = lax.broadcasted_iota(jnp.int32, (n_q, n_k), 1) >> CHUNK_SHIFT
    return kc <= qc


NEG_BIG = -1e30
ATTN_HEADS_PER_STEP = 2


def _attn_fwd(qp, kp, vp, tq, gathered):
    s = qp.shape[0]
    nq = s // tq
    hb = ATTN_HEADS_PER_STEP
    groups = HEADS // hb
    width = hb * HEAD_PAD
    ng = len(gathered)

    def body(q_ref, k_ref, v_ref, *rest):
        g_in, (o_ref, lse_ref), g_out = rest[:ng], rest[ng:ng + 2], rest[ng + 2:2 * ng + 2]
        m_sc, acc_sc = rest[2 * ng + 2:2 * ng + 4]
        g_start, g_forward, g_finish = _gather_steps(g_in, g_out, rest[2 * ng + 4:])
        g, i = pl.program_id(0), pl.program_id(1)
        pl.when((g == 0) & (i == 0))(g_start)
        pl.when((g == groups - 1) & (i == nq // 2))(g_forward)
        m_sc[...] = jnp.full(m_sc.shape, NEG_BIG, F32)
        acc_sc[...] = jnp.zeros(acc_sc.shape, F32)

        def tile(j, masked, n_tiles=1):
            n_k = n_tiles * tq
            rows = pl.ds(pl.multiple_of(j * tq, tq), n_k)
            for hh in range(hb):
                lanes = slice(hh * HEAD_PAD, (hh + 1) * HEAD_PAD)
                sc = _dot_nt(q_ref[:, lanes], k_ref[rows, lanes])
                if masked:
                    sc = jnp.where(_chunk_mask(tq, n_k, n_k - tq), sc, NEG_BIG)
                blocks = [sc[:, b * 128:(b + 1) * 128] for b in range(n_k // 128)]
                m_prev = m_sc[hh]
                m_tile = jnp.max(functools.reduce(jnp.maximum, blocks), axis=-1, keepdims=True)
                m_new = jnp.maximum(m_prev, m_tile)
                alpha = jnp.exp2((m_prev - m_new) * SCALE_LOG2E)
                p = jnp.concatenate([jnp.exp2((b - m_new) * SCALE_LOG2E) for b in blocks], axis=1).astype(MXU)
                acc_sc[hh] = alpha * acc_sc[hh] + _dot(p, v_ref[rows, lanes])
                m_sc[hh] = m_new

        def off_diagonal_pair(p, carry):
            tile(2 * p, False, n_tiles=2)
            return carry

        lax.fori_loop(0, i // 2, off_diagonal_pair, 0)

        @pl.when(i % 2 == 1)
        def _():
            tile(i - 1, True, n_tiles=2)

        @pl.when(i % 2 == 0)
        def _():
            tile(i, True)
        for hh in range(hb):
            lanes = slice(hh * HEAD_PAD, (hh + 1) * HEAD_PAD)
            acc = acc_sc[hh]
            denom = acc[:, NOPE:NOPE + 1]
            o_ref[:, lanes] = (acc / denom).astype(MXU)
            lse_ref[hh] = m_sc[hh][:, :1] * SCALE_LOG2E + jnp.log(denom) * LOG2E
        pl.when((g == groups - 1) & (i == nq - 1))(g_finish)

    q_spec = pl.BlockSpec((tq, width), lambda g, i: (i, g))
    kv_spec = pl.BlockSpec((s, width), lambda g, i: (0, g))
    any_spec = pl.BlockSpec(memory_space=pl.ANY)
    outs = _call(
        body, name="attn_fwd", grid=(groups, nq), sem=("arbitrary", "arbitrary"),
        in_specs=[q_spec, kv_spec, kv_spec] + [any_spec] * ng,
        out_specs=[q_spec, pl.BlockSpec((hb, tq, 1), lambda g, i: (g, i, 0))] + [any_spec] * ng,
        out_shape=[_sds(qp.shape, MXU), _sds((HEADS, s, 1), F32)]
        + [_sds((N_DEV,) + a.shape, a.dtype) for a in gathered],
        scratch=[pltpu.VMEM((hb, tq, HEAD_PAD), F32), pltpu.VMEM((hb, tq, HEAD_PAD), F32)] + _comm_sems(ng),
    )(qp, kp, vp, *gathered)
    return outs[0], outs[1], outs[2:]


def _out_proj_fwd(o_pad, sgu, wo, x, ada_raw, ada_b, g_post_mix, g_pre_ffn, tm):
    s = x.shape[0]
    hp = HEADS * HEAD_PAD

    def body(o_ref, sgu_ref, wo_ref, x_ref, ar_ref, ab_ref, gpm_ref, gpf_ref, m_ref, x2_ref, h2_ref):
        gt1 = _row(ar_ref, 2) + _row(ab_ref, 2)
        sh2 = _row(ar_ref, 3) + _row(ab_ref, 3)
        sc2 = _row(ar_ref, 4) + _row(ab_ref, 4)
        m = _dot(o_ref[...], wo_ref[pl.ds(0, hp), :]) + _dot(sgu_ref[...], wo_ref[pl.ds(hp, GM_WIDTH), :])
        m_ref[...] = m
        x2 = x_ref[...] + gt1 * (_rms(m)[0] * gpm_ref[...])
        x2_ref[...] = x2
        h2_ref[...] = ((_rms(x2)[0] * gpf_ref[...]) * (1.0 + sc2) + sh2).astype(MXU)

    return _call(
        body, name="out_proj_fwd", grid=(s // tm,), sem=("parallel",),
        in_specs=[_rows(tm, hp), _rows(tm, GM_WIDTH), _full(wo.shape), _rows(tm, D_MODEL), _full(ada_raw.shape),
                  _full(ada_b.shape), _full(g_post_mix.shape), _full(g_pre_ffn.shape)],
        out_specs=[_rows(tm, D_MODEL)] * 3,
        out_shape=[_sds((s, D_MODEL), F32), _sds((s, D_MODEL), F32), _sds((s, D_MODEL), MXU)],
    )(o_pad, sgu, wo, x, ada_raw, ada_b, g_post_mix, g_pre_ffn)


def _conv(u, halo, cw_ref, cb_ref):
    ext = jnp.concatenate([halo, u], axis=0)
    m1, m2 = pltpu.roll(ext, 1, 0)[8:], pltpu.roll(ext, 2, 0)[8:]
    return cb_ref[0] + ((m2 * cw_ref[0, pl.ds(0, 1), :] + m1 * cw_ref[0, pl.ds(1, 1), :]) + u * cw_ref[0, pl.ds(2, 1), :])


ROW_SUB = 256


def _sub_blocks(tm):
    return [slice(r, r + ROW_SUB) for r in range(0, tm, ROW_SUB)]


def _ffn_up_fwd(h2, w_up, conv_w, conv_b, tm):
    s = h2.shape[0]
    half = N_DEV // 2

    def body(h_ref, wa_ref, wb_ref, cwa_ref, cwb_ref, cba_ref, cbb_ref,
             ua_ref, ub_ref, ya_ref, yb_ref, act_ref, halo_a, halo_b, wa_t, wb_t):
        i = pl.program_id(1)

        @pl.when(i == 0)
        def _():
            halo_a[...] = jnp.zeros(halo_a.shape, F32)
            halo_b[...] = jnp.zeros(halo_b.shape, F32)
            wa_t[...] = wa_ref[0].T
            wb_t[...] = wb_ref[0].T

        ha, hb = halo_a[...], halo_b[...]
        for rows in _sub_blocks(tm):
            h = h_ref[rows, :]
            ua = _dot(h, wa_t[...])
            ub = _dot(h, wb_t[...])
            ua_ref[0, rows, :] = ua
            ub_ref[0, rows, :] = ub
            ya = _conv(ua, ha, cwa_ref, cba_ref)
            yb = _conv(ub, hb, cwb_ref, cbb_ref)
            ya_ref[0, rows, :] = ya
            yb_ref[0, rows, :] = yb
            ha, hb = ua[ROW_SUB - 8:], ub[ROW_SUB - 8:]
            act_ref[0, rows, :] = ((ya * jax.nn.sigmoid(ya)) * yb).astype(MXU)
        halo_a[...] = ha
        halo_b[...] = hb

    def blk(shape, off):
        return pl.BlockSpec(shape, lambda j, i: (j + off, 0, 0))

    def tok(off=0):
        return pl.BlockSpec((1, tm, FF_BLK), lambda j, i: (j + off, i, 0))

    return _call(
        body, name="ffn_up_fwd", grid=(half, s // tm), sem=("parallel", "arbitrary"),
        in_specs=[pl.BlockSpec((tm, D_MODEL), lambda j, i: (i, 0)),
                  blk((1, FF_BLK, D_MODEL), 0), blk((1, FF_BLK, D_MODEL), half),
                  blk((1, 3, FF_BLK), 0), blk((1, 3, FF_BLK), half), blk((1, 1, FF_BLK), 0), blk((1, 1, FF_BLK), half)],
        out_specs=[tok()] * 5,
        out_shape=[_sds((half, s, FF_BLK), F32)] * 4 + [_sds((half, s, FF_BLK), MXU)],
        scratch=[pltpu.VMEM((8, FF_BLK), F32), pltpu.VMEM((8, FF_BLK), F32),
                 pltpu.VMEM((D_MODEL, FF_BLK), MXU), pltpu.VMEM((D_MODEL, FF_BLK), MXU)],
    )(h2, w_up, w_up, conv_w, conv_w, conv_b, conv_b)


def _ffn_down_fwd(act, wd, x2, target, ada_raw, ada_b, g_post_ffn, tm):
    s = x2.shape[0]
    half = N_DEV // 2

    def body(act_ref, wd_ref, x2_ref, t_ref, ar_ref, ab_ref, g_ref, dout_ref, df_ref, loss_ref, dgt_ref, dg_ref):
        i = pl.program_id(0)

        @pl.when(i == 0)
        def _():
            loss_ref[...] = jnp.zeros(loss_ref.shape, F32)
            dgt_ref[...] = jnp.zeros(dgt_ref.shape, F32)
            dg_ref[...] = jnp.zeros(dg_ref.shape, F32)

        gt2 = _row(ar_ref, 5) + _row(ab_ref, 5)
        g = g_ref[...]
        for rows in _sub_blocks(tm):
            f = _dot(act_ref[0, rows, :], wd_ref[0])
            for j in range(1, half):
                f = f + _dot(act_ref[j, rows, :], wd_ref[j])
            fhat, rf = _rms(f)
            fn = fhat * g
            err = (x2_ref[rows, :] + gt2 * fn) - t_ref[rows, :]
            loss_ref[...] += 0.5 * jnp.sum(jnp.mean(err * err, axis=-1, keepdims=True))
            d_out = err * (1.0 / D_MODEL)
            dout_ref[rows, :] = d_out
            dgt_ref[...] += jnp.sum(d_out * fn, axis=0, keepdims=True)
            d_fn = d_out * gt2
            dg_ref[...] += jnp.sum(d_fn * fhat, axis=0, keepdims=True)
            df_ref[rows, :] = _rms_bwd(d_fn * g, fhat, rf).astype(MXU)

    vec = pl.BlockSpec((1, D_MODEL), lambda i: (0, 0))
    return _call(
        body, name="ffn_down_fwd", grid=(s // tm,), sem=("arbitrary",),
        in_specs=[pl.BlockSpec((half, tm, FF_BLK), lambda i: (0, i, 0)), _full(wd.shape), _rows(tm, D_MODEL),
                  _rows(tm, D_MODEL), _full(ada_raw.shape), _full(ada_b.shape), _full(g_post_ffn.shape)],
        out_specs=[_rows(tm, D_MODEL), _rows(tm, D_MODEL), pl.BlockSpec((1, 128), lambda i: (0, 0)), vec, vec],
        out_shape=[_sds((s, D_MODEL), F32), _sds((s, D_MODEL), MXU), _sds((1, 128), F32),
                   _sds((1, D_MODEL), F32), _sds((1, D_MODEL), F32)],
    )(act, wd, x2, target, ada_raw, ada_b, g_post_ffn)


def _ffn_down_bwd(d_f, wd, up_a, up_b, y_a, y_b, conv_w, act, h2, tm):
    s = d_f.shape[0]
    half = N_DEV // 2
    nt = s // tm

    def body(df_ref, wd_ref, ua_ref, ub_ref, ya_ref, yb_ref, cwa_ref, cwb_ref, act_ref, h2_ref,
             dup_ref, dcw_ref, dcb_ref, pd_ref, pu_ref, next_a, next_b, acc_d, acc_a, acc_b, wd_t):
        i = pl.program_id(1)

        @pl.when(i == 0)
        def _():
            next_a[...] = jnp.zeros(next_a.shape, F32)
            next_b[...] = jnp.zeros(next_b.shape, F32)
            dcw_ref[...] = jnp.zeros(dcw_ref.shape, F32)
            dcb_ref[...] = jnp.zeros(dcb_ref.shape, F32)
            for acc in (acc_d, acc_a, acc_b):
                acc[...] = jnp.zeros(acc.shape, F32)
            wd_t[...] = wd_ref[0].T

        def conv_bwd(d_y, u, nxt, cw_ref, part, rows):
            ext = jnp.concatenate([d_y, nxt], axis=0)
            p1 = pltpu.roll(ext, ROW_SUB + 7, 0)[:ROW_SUB]
            p2 = pltpu.roll(ext, ROW_SUB + 6, 0)[:ROW_SUB]
            d_u = (d_y * cw_ref[0, pl.ds(2, 1), :] + p1 * cw_ref[0, pl.ds(1, 1), :]) + p2 * cw_ref[0, pl.ds(0, 1), :]
            dup_ref[0, part, rows, :] = d_u.astype(MXU)
            dcb_ref[0, part] += jnp.sum(d_y, axis=0, keepdims=True)
            dcw_ref[0, part, pl.ds(0, 1), :] += jnp.sum(p2 * u, axis=0, keepdims=True)
            dcw_ref[0, part, pl.ds(1, 1), :] += jnp.sum(p1 * u, axis=0, keepdims=True)
            dcw_ref[0, part, pl.ds(2, 1), :] += jnp.sum(d_y * u, axis=0, keepdims=True)
            return d_y[:8]

        nxa, nxb = next_a[...], next_b[...]
        for rows in reversed(_sub_blocks(tm)):
            d_act = _dot(df_ref[rows, :], wd_t[...])
            ya, yb = ya_ref[0, rows, :], yb_ref[0, rows, :]
            sig = jax.nn.sigmoid(ya)
            d_ya = d_act * yb * (sig * (1.0 + ya * (1.0 - sig)))
            d_yb = d_act * (ya * sig)
            nxa = conv_bwd(d_ya, ua_ref[0, rows, :], nxa, cwa_ref, 0, rows)
            nxb = conv_bwd(d_yb, ub_ref[0, rows, :], nxb, cwb_ref, 1, rows)
        next_a[...] = nxa
        next_b[...] = nxb
        acc_d[...] += _dot_tn(act_ref[0], df_ref[...])
        acc_a[...] += _dot_tn(dup_ref[0, 0], h2_ref[...])
        acc_b[...] += _dot_tn(dup_ref[0, 1], h2_ref[...])

        @pl.when(i == nt - 1)
        def _():
            pd_ref[0] = acc_d[...].astype(MXU)
            pu_ref[0, 0] = acc_a[...].astype(MXU)
            pu_ref[0, 1] = acc_b[...].astype(MXU)

    def rev(i):
        return nt - 1 - i

    def blk(shape, off):
        return pl.BlockSpec(shape, lambda j, i: (j + off, 0, 0))

    tok = pl.BlockSpec((1, tm, FF_BLK), lambda j, i: (j, rev(i), 0))
    acc3 = pl.BlockSpec((1, 2, 3, FF_BLK), lambda j, i: (j, 0, 0, 0))
    acc1 = pl.BlockSpec((1, 2, 1, FF_BLK), lambda j, i: (j, 0, 0, 0))
    return _call(
        body, name="ffn_down_bwd", grid=(half, nt), sem=("parallel", "arbitrary"),
        in_specs=[pl.BlockSpec((tm, D_MODEL), lambda j, i: (rev(i), 0)), blk((1, FF_BLK, D_MODEL), 0),
                  tok, tok, tok, tok, blk((1, 3, FF_BLK), 0), blk((1, 3, FF_BLK), half),
                  tok, pl.BlockSpec((tm, D_MODEL), lambda j, i: (rev(i), 0))],
        out_specs=[pl.BlockSpec((1, 2, tm, FF_BLK), lambda j, i: (j, 0, rev(i), 0)), acc3, acc1,
                   pl.BlockSpec((1, FF_BLK, D_MODEL), lambda j, i: (j, 0, 0)),
                   pl.BlockSpec((1, 2, FF_BLK, D_MODEL), lambda j, i: (j, 0, 0, 0))],
        out_shape=[_sds((half, 2, s, FF_BLK), MXU), _sds((half, 2, 3, FF_BLK), F32), _sds((half, 2, 1, FF_BLK), F32),
                   _sds((half, FF_BLK, D_MODEL), MXU), _sds((half, 2, FF_BLK, D_MODEL), MXU)],
        scratch=[pltpu.VMEM((8, FF_BLK), F32), pltpu.VMEM((8, FF_BLK), F32)]
        + [pltpu.VMEM((FF_BLK, D_MODEL), F32)] * 3 + [pltpu.VMEM((D_MODEL, FF_BLK), MXU)],
    )(d_f, wd, up_a, up_b, y_a, y_b, conv_w, conv_w, act, h2)


def _ffn_up_bwd(d_up, w_up, x2, m, d_out, ada_raw, ada_b, g_pre_ffn, g_post_mix, tm):
    s = x2.shape[0]
    half = N_DEV // 2

    def body(dup_ref, w_ref, x2_ref, m_ref, dout_ref, ar_ref, ab_ref, gpf_ref, gpm_ref,
             dx_ref, dm_ref, dsh_ref, dsc_ref, dgpf_ref, dgt1_ref, dgpm_ref):
        i = pl.program_id(0)

        @pl.when(i == 0)
        def _():
            for r in (dsh_ref, dsc_ref, dgpf_ref, dgt1_ref, dgpm_ref):
                r[...] = jnp.zeros(r.shape, F32)

        gt1 = _row(ar_ref, 2) + _row(ab_ref, 2)
        sc2 = _row(ar_ref, 4) + _row(ab_ref, 4)
        gpf, gpm = gpf_ref[...], gpm_ref[...]
        d_h2 = _dot(dup_ref[0, 0], w_ref[0])
        for j in range(1, half):
            d_h2 = d_h2 + _dot(dup_ref[j, 0], w_ref[j])
        for j in range(half):
            d_h2 = d_h2 + _dot(dup_ref[j, 1], w_ref[half + j])
        x2n, r2 = _rms(x2_ref[...])
        dsh_ref[...] += jnp.sum(d_h2, axis=0, keepdims=True)
        dsc_ref[...] += jnp.sum(d_h2 * (x2n * gpf), axis=0, keepdims=True)
        d_mod = d_h2 * (1.0 + sc2)
        dgpf_ref[...] += jnp.sum(d_mod * x2n, axis=0, keepdims=True)
        d_x2 = dout_ref[...] + _rms_bwd(d_mod * gpf, x2n, r2)
        dx_ref[...] = d_x2
        mhat, rm = _rms(m_ref[...])
        dgt1_ref[...] += jnp.sum(d_x2 * (mhat * gpm), axis=0, keepdims=True)
        d_mn = d_x2 * gt1
        dgpm_ref[...] += jnp.sum(d_mn * mhat, axis=0, keepdims=True)
        dm_ref[...] = _rms_bwd(d_mn * gpm, mhat, rm).astype(MXU)

    vec = pl.BlockSpec((1, D_MODEL), lambda i: (0, 0))
    tok = pl.BlockSpec((half, 2, tm, FF_BLK), lambda i: (0, 0, i, 0))
    return _call(
        body, name="ffn_up_bwd", grid=(s // tm,), sem=("arbitrary",),
        in_specs=[tok, _full(w_up.shape), _rows(tm, D_MODEL), _rows(tm, D_MODEL), _rows(tm, D_MODEL),
                  _full(ada_raw.shape), _full(ada_b.shape), _full(g_pre_ffn.shape), _full(g_post_mix.shape)],
        out_specs=[_rows(tm, D_MODEL), _rows(tm, D_MODEL), vec, vec, vec, vec, vec],
        out_shape=[_sds((s, D_MODEL), F32), _sds((s, D_MODEL), MXU)] + [_sds((1, D_MODEL), F32)] * 5,
    )(d_up, w_up, x2, m, d_out, ada_raw, ada_b, g_pre_ffn, g_post_mix)


def _out_proj_bwd(d_m, wo, o_pad, tm):
    s = d_m.shape[0]
    hp = HEADS * HEAD_PAD

    def body(dm_ref, wo_ref, o_ref, do_ref, dsgu_ref, delta_ref, wo_t):
        @pl.when(pl.program_id(0) == 0)
        def _():
            wo_t[...] = wo_ref[...].T

        d_cat = _dot(dm_ref[...], wo_t[...])
        d_o = d_cat[:, :hp]
        do_ref[...] = d_o.astype(MXU)
        dsgu_ref[...] = d_cat[:, hp:]
        prod = d_o * o_ref[...].astype(F32)
        for h in range(HEADS):
            delta_ref[h] = jnp.sum(prod[:, h * HEAD_PAD:(h + 1) * HEAD_PAD], axis=-1, keepdims=True)

    return _call(
        body, name="out_proj_bwd", grid=(s // tm,), sem=("arbitrary",),
        in_specs=[_rows(tm, D_MODEL), _full(wo.shape), _rows(tm, hp)],
        out_specs=[_rows(tm, hp), _rows(tm, GM_WIDTH), pl.BlockSpec((HEADS, tm, 1), lambda i: (0, i, 0))],
        out_shape=[_sds((s, hp), MXU), _sds((s, GM_WIDTH), F32), _sds((HEADS, s, 1), F32)],
        scratch=[pltpu.VMEM(wo.shape[::-1], MXU)],
    )(d_m, wo, o_pad)


def _attn_bwd(qp, kp, vp, d_o, lse, delta, tq, scattered, gathered):
    s = qp.shape[0]
    nq = s // tq
    hb = ATTN_HEADS_PER_STEP
    groups = HEADS // hb
    width = hb * HEAD_PAD
    ns, ng = len(scattered), len(gathered)
    nc = ns + ng
    slots = [slot for _, slot in scattered]

    def body(q_ref, k_ref, v_ref, do_ref, lse_ref, dl_ref, *rest):
        c_in, (dq_ref, dk_ref, dv_ref), c_out = rest[:nc], rest[nc:nc + 3], rest[nc + 3:2 * nc + 3]
        dk_sc, dv_sc = rest[2 * nc + 3:2 * nc + 5]
        sems = rest[2 * nc + 5:]
        s_start, s_finish = _scatter_steps(c_in[:ns], c_out[:ns], sems[:3], slots)
        g_start, g_forward, g_finish = _gather_steps(c_in[ns:], c_out[ns:], sems[3:])
        g, j = pl.program_id(0), pl.program_id(1)

        @pl.when((g == 0) & (j == 0))
        def _():
            s_start()
            g_start()

        pl.when((g == groups - 1) & (j == nq // 2))(g_forward)

        @pl.when(j == 0)
        def _():
            dq_ref[...] = jnp.zeros(dq_ref.shape, F32)

        dk_sc[...] = jnp.zeros(dk_sc.shape, F32)
        dv_sc[...] = jnp.zeros(dv_sc.shape, F32)

        def tile(i, masked):
            rows = pl.ds(pl.multiple_of(i * tq, tq), tq)
            for hh in range(hb):
                lanes = slice(hh * HEAD_PAD, (hh + 1) * HEAD_PAD)
                q, do, k = q_ref[rows, lanes], do_ref[rows, lanes], k_ref[:, lanes]
                sc = _dot_nt(q, k)
                if masked:
                    sc = jnp.where(_chunk_mask(tq, tq, 0), sc, NEG_BIG)
                p = jnp.exp2(sc * SCALE_LOG2E - lse_ref[hh, rows, :])
                dv_sc[hh] += _dot_tn(p.astype(MXU), do)
                dp = _dot_nt(do, v_ref[:, lanes])
                ds = (p * (dp - dl_ref[hh, rows, :])).astype(MXU)
                dk_sc[hh] += _dot_tn(ds, q)
                dq_ref[rows, lanes] += _dot(ds, k) * ATTN_SCALE

        def tile_pair(i0, first_masked):
            both = pl.ds(pl.multiple_of(i0 * tq, tq), 2 * tq)
            for hh in range(hb):
                lanes = slice(hh * HEAD_PAD, (hh + 1) * HEAD_PAD)
                k, v = k_ref[:, lanes], v_ref[:, lanes]
                ps, dss = [], []
                for t in range(2):
                    rows = pl.ds(pl.multiple_of((i0 + t) * tq, tq), tq)
                    q, do = q_ref[rows, lanes], do_ref[rows, lanes]
                    sc = _dot_nt(q, k)
                    if first_masked and t == 0:
                        sc = jnp.where(_chunk_mask(tq, tq, 0), sc, NEG_BIG)
                    p = jnp.exp2(sc * SCALE_LOG2E - lse_ref[hh, rows, :])
                    ds = (p * (_dot_nt(do, v) - dl_ref[hh, rows, :])).astype(MXU)
                    dq_ref[rows, lanes] += _dot(ds, k) * ATTN_SCALE
                    ps.append(p.astype(MXU))
                    dss.append(ds)
                dv_sc[hh] += _dot_tn(jnp.concatenate(ps, axis=0), do_ref[both, lanes])
                dk_sc[hh] += _dot_tn(jnp.concatenate(dss, axis=0), q_ref[both, lanes])

        odd = (nq - j) % 2

        @pl.when(odd == 1)
        def _():
            tile(j, True)

        @pl.when(odd == 0)
        def _():
            tile_pair(j, True)

        first = j + 2 - odd

        def later_pair(pair, carry):
            tile_pair(first + 2 * pair, False)
            return carry

        lax.fori_loop(0, (nq - first) // 2, later_pair, 0)
        for hh in range(hb):
            lanes = slice(hh * HEAD_PAD, (hh + 1) * HEAD_PAD)
            dk_ref[:, lanes] = dk_sc[hh] * ATTN_SCALE
            dv_ref[:, lanes] = dv_sc[hh]
        @pl.when((g == groups - 1) & (j == nq - 1))
        def _():
            g_finish()
            s_finish()

    seq_spec = pl.BlockSpec((s, width), lambda g, j: (0, g))
    kv_spec = pl.BlockSpec((tq, width), lambda g, j: (j, g))
    col_spec = pl.BlockSpec((hb, s, 1), lambda g, j: (g, 0, 0))
    any_spec = pl.BlockSpec(memory_space=pl.ANY)
    outs = _call(
        body, name="attn_bwd", grid=(groups, nq), sem=("arbitrary", "arbitrary"),
        in_specs=[seq_spec, kv_spec, kv_spec, seq_spec, col_spec, col_spec] + [any_spec] * nc,
        out_specs=[seq_spec, kv_spec, kv_spec] + [any_spec] * nc,
        out_shape=[_sds(qp.shape, F32), _sds(qp.shape, F32), _sds(qp.shape, F32)]
        + [_scatter_out_shape(a, slot) for a, slot in scattered]
        + [_sds((N_DEV,) + a.shape, a.dtype) for a in gathered],
        scratch=[pltpu.VMEM((hb, tq, HEAD_PAD), F32), pltpu.VMEM((hb, tq, HEAD_PAD), F32)]
        + _comm_sems(ns) + _comm_sems(ng),
    )(qp, kp, vp, d_o, lse, delta, *[a for a, _ in scattered], *gathered)
    return outs[0], outs[1], outs[2], outs[3:3 + ns], outs[3 + ns:]


def _gmlp_bwd(z, d_sgu, ln_g, ln_b, w_sp, bias_exp, tm):
    s = z.shape[0]
    nblk = tm // GM_CHUNK

    def body(zu_ref, zv_ref, dsgu_ref, lg_ref, lb_ref, w_ref, be_ref,
             dguv_ref, dws_ref, dbs_ref, dlg_ref, dlb_ref, dbe_sc, dvln_sc, dlg_sc, dlb_sc):
        i = pl.program_id(0)

        @pl.when(i == 0)
        def _():
            for r in (dws_ref, dlg_sc, dlb_sc, dbe_sc):
                r[...] = jnp.zeros(r.shape, F32)

        seg = _seg_matrix()
        mask = _spatial_mask()
        wm = [(w_ref[h] * mask).astype(MXU) for h in range(HEADS)]
        zu, zv = zu_ref[...], zv_ref[...]
        gu = _gelu(zu)
        _, vhat, rstd = _gm_norm(zv, seg)
        lg = lg_ref[...]
        vln = (vhat * lg + lb_ref[...]).astype(MXU)
        d_sgu = dsgu_ref[...]
        for n in range(nblk):
            rows = slice(n * GM_CHUNK, (n + 1) * GM_CHUNK)
            vb = vln[rows]
            mixed = _gm_mix(wm, vb, GM_CHUNK) + be_ref[...]
            d_mixed = d_sgu[rows] * gu[rows]
            dguv_ref[rows, pl.ds(0, GM_WIDTH)] = ((d_sgu[rows] * mixed) * _gelu_grad(zu[rows])).astype(MXU)
            dbe_sc[...] += d_mixed
            dmb = d_mixed.astype(MXU)
            for p, (lanes, first) in enumerate(_gm_pairs(GM_CHUNK)):
                dm_pair, v_pair = dmb[:, lanes], vb[:, lanes]
                zero = jnp.zeros_like(dm_pair)
                dws_ref[2 * p] += _dot_nt(jnp.where(first, dm_pair, zero), v_pair)
                dws_ref[2 * p + 1] += _dot_nt(jnp.where(first, zero, dm_pair), v_pair)
            dvln_sc[rows, :] = _gm_mix(wm, dmb, GM_CHUNK, transposed=True)
        d_vln = dvln_sc[...]
        dlg_sc[...] += jnp.sum(d_vln * vhat, axis=0, keepdims=True)
        dlb_sc[...] += jnp.sum(d_vln, axis=0, keepdims=True)
        d_vhat = d_vln * lg
        d_gv = rstd * ((d_vhat - _split_dot(d_vhat, seg)) - vhat * _split_dot(d_vhat * vhat, seg))
        dguv_ref[:, pl.ds(GM_WIDTH, GM_WIDTH)] = (d_gv * _gelu_grad(zv)).astype(MXU)

        @pl.when(i == pl.num_programs(0) - 1)
        def _():
            for h in range(HEADS):
                dws_ref[h] = dws_ref[h] * mask
            hrow = lax.broadcasted_iota(jnp.int32, (HEADS, GM_WIDTH), 0)
            hlane = lax.broadcasted_iota(jnp.int32, (HEADS, GM_WIDTH), 1) >> 6
            ind = jnp.where(hrow == hlane, 1.0, 0.0).astype(MXU)
            acc = dbe_sc[...]
            hi = acc.astype(MXU)
            lo = (acc - hi.astype(F32)).astype(MXU)
            dbs_ref[...] = _dot_nt(ind, hi) + _dot_nt(ind, lo)
            pick = (lax.broadcasted_iota(jnp.int32, (GM_WIDTH, GM_DIM), 0) & (GM_DIM - 1)
                    == lax.broadcasted_iota(jnp.int32, (GM_WIDTH, GM_DIM), 1))
            pick = jnp.where(pick, 1.0, 0.0).astype(MXU)
            for src, dst in ((dlg_sc, dlg_ref), (dlb_sc, dlb_ref)):
                spread = jnp.where(hrow == hlane, jnp.broadcast_to(src[...], (HEADS, GM_WIDTH)), 0.0)
                dst[...] = _split_dot3(spread, pick)

    return _call(
        body, name="gmlp_bwd", grid=(s // tm,), sem=("arbitrary",),
        in_specs=[_rows(tm, GM_WIDTH, 1), _rows(tm, GM_WIDTH, 2), _rows(tm, GM_WIDTH), _full(ln_g.shape),
                  _full(ln_b.shape), _full(w_sp.shape), _full(bias_exp.shape)],
        out_specs=[_rows(tm, 2 * GM_WIDTH), _full(w_sp.shape), _full((HEADS, GM_CHUNK)), _full((HEADS, GM_DIM)),
                   _full((HEADS, GM_DIM))],
        out_shape=[_sds((s, 2 * GM_WIDTH), MXU), _sds(w_sp.shape, F32), _sds((HEADS, GM_CHUNK), F32),
                   _sds((HEADS, GM_DIM), F32), _sds((HEADS, GM_DIM), F32)],
        scratch=[pltpu.VMEM((GM_CHUNK, GM_WIDTH), F32), pltpu.VMEM((tm, GM_WIDTH), F32),
                 pltpu.VMEM((1, GM_WIDTH), F32), pltpu.VMEM((1, GM_WIDTH), F32)],
    )(z, z, d_sgu, ln_g, ln_b, w_sp, bias_exp)


def _mix_in_bwd(dq, dk, dv, z, d_guv, x, d_x_part, ada_raw, ada_b, g_pre, g_q, g_kv, w1t, wqt, wkv,
                cos_t, sin_t, tm):
    s = x.shape[0]
    hp = HEADS * HEAD_PAD
    za = Q_LORA + KV_LORA + HEAD_PAD

    def body(dq_ref, dk_ref, dv_ref, z_ref, dguv_ref, x_ref, dxp_ref, ar_ref, ab_ref, g_ref, gq_ref, gkv_ref,
             w1_ref, wq_ref, wkv_ref, cos_ref, sin_ref,
             gx_ref, dza_ref, dqp_ref, dkvp_ref, dsh_ref, dsc_ref, dg_ref, dgq_ref, dgkv_ref):
        i = pl.program_id(0)

        @pl.when(i == 0)
        def _():
            for r in (dsh_ref, dsc_ref, dg_ref, dgq_ref, dgkv_ref):
                r[...] = jnp.zeros(r.shape, F32)

        cos, sin = cos_ref[...], sin_ref[...]
        d_krot = jnp.zeros((tm, HEAD_PAD), F32)
        for h in range(HEADS):
            blk = slice(h * HEAD_PAD, (h + 1) * HEAD_PAD)
            dqp_ref[:, blk] = _rope_transposed(dq_ref[:, blk], cos, sin).astype(MXU)
            dk_h = dk_ref[:, blk]
            d_krot = d_krot + dk_h
            dkvp_ref[:, blk] = dk_h.astype(MXU)
        dkvp_ref[:, pl.ds(hp, hp)] = dv_ref[...].astype(MXU)
        lane = lax.broadcasted_iota(jnp.int32, (tm, HEAD_PAD), 1)
        d_kr = jnp.where((lane >= NOPE) & (lane < NOPE + ROPE), _rope_transposed(d_krot, cos, sin), 0.0)
        d_cqn = _dot(dqp_ref[...], wq_ref[...])
        d_ckvn = _dot_nt(dkvp_ref[...], wkv_ref[...])
        zt = z_ref[...]
        gq, gkv = gq_ref[...], gkv_ref[...]
        cq_hat, rq = _rms(zt[:, :Q_LORA])
        ckv_hat, rkv = _rms(zt[:, Q_LORA:Q_LORA + KV_LORA])
        dgq_ref[...] += jnp.sum(d_cqn * cq_hat, axis=0, keepdims=True)
        dgkv_ref[...] += jnp.sum(d_ckvn * ckv_hat, axis=0, keepdims=True)
        d_cq = _rms_bwd(d_cqn * gq, cq_hat, rq)
        d_ckv = _rms_bwd(d_ckvn * gkv, ckv_hat, rkv)
        d_za = jnp.concatenate([d_cq, d_ckv, d_kr], axis=1).astype(MXU)
        dza_ref[...] = d_za
        d_h1 = _dot(d_za, w1_ref[pl.ds(0, za), :]) + _dot(dguv_ref[...], w1_ref[pl.ds(za, 2 * GM_WIDTH), :])
        sc1 = _row(ar_ref, 1) + _row(ab_ref, 1)
        g = g_ref[...]
        xn, r1 = _rms(x_ref[...])
        dsh_ref[...] += jnp.sum(d_h1, axis=0, keepdims=True)
        dsc_ref[...] += jnp.sum(d_h1 * (xn * g), axis=0, keepdims=True)
        d_mod = d_h1 * (1.0 + sc1)
        dg_ref[...] += jnp.sum(d_mod * xn, axis=0, keepdims=True)
        gx_ref[...] = dxp_ref[...] + _rms_bwd(d_mod * g, xn, r1)

    vec = pl.BlockSpec((1, D_MODEL), lambda i: (0, 0))
    return _call(
        body, name="mix_in_bwd", grid=(s // tm,), sem=("arbitrary",),
        in_specs=[_rows(tm, hp), _rows(tm, hp), _rows(tm, hp), _rows(tm, za), _rows(tm, 2 * GM_WIDTH),
                  _rows(tm, D_MODEL), _rows(tm, D_MODEL), _full(ada_raw.shape), _full(ada_b.shape), _full(g_pre.shape),
                  _full(g_q.shape), _full(g_kv.shape), _full(w1t.shape), _full(wqt.shape),
                  _full(wkv.shape), _rows(tm, HEAD_PAD), _rows(tm, HEAD_PAD)],
        out_specs=[_rows(tm, D_MODEL), _rows(tm, za), _rows(tm, hp), _rows(tm, 2 * hp), vec, vec, vec,
                   _full(g_q.shape), _full(g_kv.shape)],
        out_shape=[_sds((s, D_MODEL), F32), _sds((s, za), MXU), _sds((s, hp), MXU), _sds((s, 2 * hp), MXU),
                   _sds((1, D_MODEL), F32), _sds((1, D_MODEL), F32), _sds((1, D_MODEL), F32),
                   _sds(g_q.shape, F32), _sds(g_kv.shape, F32)],
    )(dq, dk, dv, z, d_guv, x, d_x_part, ada_raw, ada_b, g_pre, g_q, g_kv, w1t, wqt, wkv, cos_t, sin_t)


def _tn_matmuls(arrays, pairs, name, ts):
    s = arrays[0].shape[0]
    steps = s // ts
    n_in, n_out = len(arrays), len(pairs)
    shapes = [(arrays[ia].shape[1], arrays[ib].shape[1]) for ia, ib in pairs]

    def body(*refs):
        ins, outs, accs = refs[:n_in], refs[n_in:n_in + n_out], refs[n_in + n_out:]
        k = pl.program_id(0)

        @pl.when(k == 0)
        def _():
            for acc in accs:
                acc[...] = jnp.zeros(acc.shape, F32)

        for (ia, ib), acc in zip(pairs, accs):
            acc[...] += _dot_tn(ins[ia][...], ins[ib][...])

        @pl.when(k == steps - 1)
        def _():
            for out, acc in zip(outs, accs):
                out[...] = acc[...].astype(MXU)

    return _call(
        body, name=name, grid=(steps,), sem=("arbitrary",),
        in_specs=[_rows(ts, a.shape[1]) for a in arrays],
        out_specs=[_full(shape) for shape in shapes],
        out_shape=[_sds(shape, MXU) for shape in shapes],
        scratch=[pltpu.VMEM(shape, F32) for shape in shapes],
    )(*arrays)


def _adamw(w, g, m, v):
    m2 = ADAM_B1 * m + (1.0 - ADAM_B1) * g
    v2 = ADAM_B2 * v + (1.0 - ADAM_B2) * (g * g)
    m_hat = m2 / (1.0 - ADAM_B1 ** ADAM_STEP)
    v_hat = v2 / (1.0 - ADAM_B2 ** ADAM_STEP)
    delta = -ADAM_LR * (m_hat / (jnp.sqrt(v_hat) + ADAM_EPS) + ADAM_WD * w)
    return delta, m2, v2


def _adam_reduce(recv, w, m, v, name):
    r, c = w.shape
    tr = r if r <= 512 else max(t for t in range(16, 513, 16) if r % t == 0)

    def body(p_ref, w_ref, m_ref, v_ref, g_ref, d_ref, mo_ref, vo_ref):
        g = p_ref[0].astype(F32)
        for j in range(1, N_DEV):
            g = g + p_ref[j].astype(F32)
        g_ref[...] = g
        d_ref[...], mo_ref[...], vo_ref[...] = _adamw(w_ref[...], g, m_ref[...], v_ref[...])

    blk = pl.BlockSpec((tr, c), lambda i: (i, 0))
    return _call(
        body, name=name, grid=(r // tr,), sem=("parallel",),
        in_specs=[pl.BlockSpec((N_DEV, tr, c), lambda i: (0, i, 0)), blk, blk, blk],
        out_specs=[blk] * 4, out_shape=[_sds((r, c), F32)] * 4,
    )(recv, w, m, v)


def _adam_w_ada(c_act_t, d_ada_cols, w, m, v):
    r, c = w.shape
    tr = 256

    def body(ct_ref, da_ref, w_ref, m_ref, v_ref, g_ref, d_ref, mo_ref, vo_ref):
        g = _dot(ct_ref[...], da_ref[...])
        g_ref[...] = g
        d_ref[...], mo_ref[...], vo_ref[...] = _adamw(w_ref[...], g, m_ref[...], v_ref[...])

    blk = pl.BlockSpec((tr, c), lambda i: (i, 0))
    return _call(
        body, name="adam_w_ada", grid=(r // tr,), sem=("parallel",),
        in_specs=[pl.BlockSpec((tr, c_act_t.shape[1]), lambda i: (i, 0)), _full(d_ada_cols.shape), blk, blk, blk],
        out_specs=[blk] * 4, out_shape=[_sds((r, c), F32)] * 4,
    )(c_act_t, d_ada_cols, w, m, v)


VEC_ROWS = D_MODEL // 128
PK_ADA = 0
PK_GAIN = PK_ADA + 6 * VEC_ROWS
PK_GQ = PK_GAIN + 4 * VEC_ROWS
PK_GKV = PK_GQ + Q_LORA // 128
PK_LOSS = PK_GKV + KV_LORA // 128
PK_LNG = 88
PK_LNB = PK_LNG + HEADS
PK_BS = PK_LNB + HEADS
PK_CB = PK_BS + HEADS
CB_ROWS = 6
PK_WS = PK_CB + N_DEV * CB_ROWS
PK_ROWS = PK_WS + HEADS * GM_CHUNK
assert PK_LOSS < PK_LNG and PK_ROWS % 8 == 0
LATE_GAIN = 2 * VEC_ROWS
LATE_GQ = 3 * VEC_ROWS
LATE_GKV = LATE_GQ + Q_LORA // 128
LATE_ROWS = 32


def _cb_chunks():
    return [(k, k * 128, min(128, FF_BLK - k * 128)) for k in range(CB_ROWS)]


def _put_rows(out_ref, row0, ref, width):
    for k in range(width // 128):
        out_ref[pl.ds(row0 + k, 1), :] = ref[:, pl.ds(k * 128, 128)]


def _pack_small(ada_rows, gains, loss_part, d_ln_g, d_ln_b, d_bs, d_cb, d_ws):
    half = N_DEV // 2

    def body(*refs):
        vec_refs = refs[:7]
        loss_ref, lng_ref, lnb_ref, bs_ref, cb_ref, ws_ref, out_ref = refs[7:]
        out_ref[pl.ds(0, PK_WS), :] = jnp.zeros((PK_WS, 128), F32)
        for n, ref in enumerate(vec_refs[:4]):
            _put_rows(out_ref, PK_ADA + (2 + n) * VEC_ROWS, ref, D_MODEL)
        for n, ref in enumerate(vec_refs[4:]):
            _put_rows(out_ref, PK_GAIN + (1 + n) * VEC_ROWS, ref, D_MODEL)
        _put_rows(out_ref, PK_LOSS, loss_ref, 128)
        out_ref[pl.ds(PK_LNG, HEADS), pl.ds(0, GM_DIM)] = lng_ref[...]
        out_ref[pl.ds(PK_LNB, HEADS), pl.ds(0, GM_DIM)] = lnb_ref[...]
        out_ref[pl.ds(PK_BS, HEADS), :] = bs_ref[...]
        for j in range(N_DEV):
            for k, lane, width in _cb_chunks():
                out_ref[pl.ds(PK_CB + j * CB_ROWS + k, 1), pl.ds(0, width)] = cb_ref[j % half, j // half, :, pl.ds(lane, width)]
        for h in range(HEADS):
            out_ref[pl.ds(PK_WS + h * GM_CHUNK, GM_CHUNK), :] = ws_ref[h]

    ins = list(ada_rows) + list(gains) + [loss_part, d_ln_g, d_ln_b, d_bs, d_cb, d_ws]
    return _call(body, name="pack_small", grid=(1,), in_specs=[_full(a.shape) for a in ins],
                 out_specs=_full((PK_ROWS, 128)), out_shape=_sds((PK_ROWS, 128), F32))(*ins)


def _pack_late(d_sh1, d_sc1, d_g_pre_mix, d_g_q, d_g_kv):
    def body(sh_ref, sc_ref, g_ref, gq_ref, gkv_ref, out_ref):
        out_ref[...] = jnp.zeros((LATE_ROWS, 128), F32)
        _put_rows(out_ref, 0, sh_ref, D_MODEL)
        _put_rows(out_ref, VEC_ROWS, sc_ref, D_MODEL)
        _put_rows(out_ref, LATE_GAIN, g_ref, D_MODEL)
        _put_rows(out_ref, LATE_GQ, gq_ref, Q_LORA)
        _put_rows(out_ref, LATE_GKV, gkv_ref, KV_LORA)

    ins = [d_sh1, d_sc1, d_g_pre_mix, d_g_q, d_g_kv]
    return _call(body, name="pack_late", grid=(1,), in_specs=[_full(a.shape) for a in ins],
                 out_specs=_full((LATE_ROWS, 128)), out_shape=_sds((LATE_ROWS, 128), F32))(*ins)


def _adam_small(gathered, late, params):
    n_par = len(params)

    def body(p_ref, late_ref, *refs):
        ins = [refs[3 * n:3 * n + 3] for n in range(n_par)]
        outs = [refs[3 * n_par + 4 * n:3 * n_par + 4 * n + 4] for n in range(n_par)]
        loss_ref, dada_ref = refs[7 * n_par:]

        def total(rows, lanes=slice(None), src=p_ref):
            g = src[0, rows, lanes]
            for j in range(1, N_DEV):
                g = g + src[j, rows, lanes]
            return g

        def apply(n, g, idx):
            w_ref, m_ref, v_ref = ins[n]
            d, m2, v2 = _adamw(w_ref[idx], g, m_ref[idx], v_ref[idx])
            for ref, val in zip(outs[n], (g, d, m2, v2)):
                ref[idx] = val

        def vector(n, src, row0, width, lane0=0):
            for k in range(width // 128):
                apply(n, total(pl.ds(row0 + k, 1), src=src), (slice(None), pl.ds(lane0 + k * 128, 128)))

        vector(0, late_ref, 0, 2 * D_MODEL)
        vector(0, p_ref, PK_ADA + 2 * VEC_ROWS, 4 * D_MODEL, lane0=2 * D_MODEL)
        vector(1, late_ref, LATE_GAIN, D_MODEL)
        for n in range(1, 4):
            vector(1 + n, p_ref, PK_GAIN + n * VEC_ROWS, D_MODEL)
        vector(5, late_ref, LATE_GQ, Q_LORA)
        vector(6, late_ref, LATE_GKV, KV_LORA)
        apply(7, total(pl.ds(PK_LNG, HEADS), pl.ds(0, GM_DIM)), (0,))
        apply(8, total(pl.ds(PK_LNB, HEADS), pl.ds(0, GM_DIM)), (0,))
        for h in range(HEADS):
            apply(9, total(pl.ds(PK_WS + h * GM_CHUNK, GM_CHUNK)), (0, h))
        apply(10, total(pl.ds(PK_BS, HEADS)), (0,))
        for j in range(N_DEV):
            for k, lane, width in _cb_chunks():
                apply(11, total(pl.ds(PK_CB + j * CB_ROWS + k, 1), pl.ds(0, width)), (pl.ds(j, 1), pl.ds(lane, width)))
        loss_ref[...] = total(pl.ds(PK_LOSS, 1))
        dada_ref[:, pl.ds(0, 2 * VEC_ROWS), :] = late_ref[:, pl.ds(0, 2 * VEC_ROWS), :]
        dada_ref[:, pl.ds(2 * VEC_ROWS, 4 * VEC_ROWS), :] = p_ref[:, pl.ds(PK_ADA + 2 * VEC_ROWS, 4 * VEC_ROWS), :]

    flat = [a for triple in params for a in triple]
    out_shape = [_sds(w.shape, F32) for w, _, _ in params for _ in range(4)]
    out_shape += [_sds((1, 128), F32), _sds((N_DEV, 6 * VEC_ROWS, 128), F32)]
    outs = _call(body, name="adam_small", grid=(1,),
                 in_specs=[_full(gathered.shape), _full(late.shape)] + [_full(a.shape) for a in flat],
                 out_specs=[_full(o.shape) for o in out_shape], out_shape=out_shape)(gathered, late, *flat)
    return [tuple(outs[4 * n:4 * n + 4]) for n in range(n_par)], outs[-2], outs[-1]


def _rope_tables(s):
    pos = jnp.arange(s, dtype=F32)
    inv = ROPE_THETA ** (-jnp.arange(0, ROPE, 2, dtype=F32) / ROPE)
    lane_inv = jnp.concatenate([jnp.zeros((NOPE,), F32), inv, inv, jnp.zeros((HEAD_PAD - NOPE - ROPE,), F32)])
    ang = pos[:, None] * lane_inv[None, :]
    return jnp.cos(ang), jnp.sin(ang)


def kernel(x, c, w_ada, b_ada, g_pre_mix, g_post_mix, w_in, g_q, w_uq, g_kv, w_ukv, gm_ln_g, gm_ln_b, w_spatial, b_spatial, w_out, g_pre_ffn, g_post_ffn, w_up, conv_w, conv_b, w_down, loss_target, m_w_ada, m_b_ada, m_g_pre_mix, m_g_post_mix, m_w_in, m_g_q, m_w_uq, m_g_kv, m_w_ukv, m_gm_ln_g, m_gm_ln_b, m_w_spatial, m_b_spatial, m_w_out, m_g_pre_ffn, m_g_post_ffn, m_w_up, m_conv_w, m_conv_b, m_w_down, v_w_ada, v_b_ada, v_g_pre_mix, v_g_post_mix, v_w_in, v_g_q, v_w_uq, v_g_kv, v_w_ukv, v_gm_ln_g, v_gm_ln_b, v_w_spatial, v_b_spatial, v_w_out, v_g_pre_ffn, v_g_post_ffn, v_w_up, v_conv_w, v_conv_b, v_w_down):
    s = x.shape[1]
    tm = min(512, s)
    tf = min(2 * ROW_SUB, s)
    tq = min(512, s)
    ts = min(2048, s)
    hp = HEADS * HEAD_PAD
    half = N_DEV // 2
    my_slot = 4 * lax.axis_index("x") + 2 * lax.axis_index("y") + lax.axis_index("c")
    x2d, target = x[0], loss_target[0]

    def t_(a):
        return jnp.swapaxes(a[0], 0, 1)

    w_in_t, m_in_t, v_in_t = t_(w_in), t_(m_w_in), t_(v_w_in)
    w_uq_t, m_uq_t, v_uq_t = t_(w_uq), t_(m_w_uq), t_(v_w_uq)
    w_up_t, m_up_t, v_up_t = t_(w_up), t_(m_w_up), t_(v_w_up)
    (g_c, g_in_t, g_uq_t, g_ukv, g_cw), _ = _exchange(
        [c, w_in_t.astype(MXU), w_uq_t.astype(MXU), w_ukv[0].astype(MXU), conv_w[0]], [], "gather_mixer_weights")

    w_in_f = g_in_t.reshape(-1, D_MODEL)
    o1, o2, o3 = Q_LORA, Q_LORA + KV_LORA, Q_LORA + KV_LORA + ROPE
    w1t = jnp.concatenate([w_in_f[:o2], jnp.zeros((NOPE, D_MODEL), MXU), w_in_f[o2:o3],
                           jnp.zeros((HEAD_PAD - NOPE - ROPE, D_MODEL), MXU), w_in_f[o3:]], axis=0)
    wqt = jnp.pad(g_uq_t, ((0, 0), (0, HEAD_PAD - NOPE - ROPE), (0, 0))).reshape(hp, Q_LORA)
    w_ukv_f = jnp.transpose(g_ukv, (1, 0, 2)).reshape(KV_LORA, HEADS, 2 * NOPE)
    pad_head = ((0, 0), (0, 0), (0, HEAD_PAD - NOPE))
    wkv = jnp.concatenate([jnp.pad(w_ukv_f[:, :, :NOPE], pad_head).reshape(KV_LORA, hp),
                           jnp.pad(w_ukv_f[:, :, NOPE:], pad_head).reshape(KV_LORA, hp)], axis=1)
    cb8 = conv_b.reshape(N_DEV, 1, FF_BLK)
    bias_exp = jnp.repeat(b_spatial[0].T, GM_DIM, axis=1)
    ln_g, ln_b = gm_ln_g.reshape(1, GM_WIDTH), gm_ln_b.reshape(1, GM_WIDTH)
    w_sp = w_spatial[0]
    cos_t, sin_t = _rope_tables(s)

    ada_part, c_act = _ada_fwd(g_c.reshape(N_DEV, D_MODEL), w_ada[0])
    _, (ada_recv,) = _exchange([], [(ada_part.reshape(N_DEV, 1, -1), _plain_slot)], "ada_rows")
    ada_raw = ada_recv.reshape(6, D_MODEL)
    ada_b = b_ada.reshape(6, D_MODEL)

    h1, z, qp, kp, vp, cqn, ckvn = _mix_in_fwd(x2d, ada_raw, ada_b, g_pre_mix, w1t, g_q, g_kv, wqt, wkv, cos_t, sin_t, tm)
    sgu = _gmlp_fwd(z, ln_g, ln_b, w_sp, bias_exp, tm)
    o_pad, lse, (g_out, g_up, g_down) = _attn_fwd(
        qp, kp, vp, tq, [w_out[0].astype(MXU), w_up_t.astype(MXU), w_down[0].astype(MXU)])
    w_out_f = g_out.reshape(2 * GM_WIDTH, D_MODEL)
    wo_attn = jnp.pad(w_out_f[:GM_WIDTH].reshape(HEADS, NOPE, D_MODEL), ((0, 0), (0, HEAD_PAD - NOPE), (0, 0)))
    wo = jnp.concatenate([wo_attn.reshape(hp, D_MODEL), w_out_f[GM_WIDTH:]], axis=0)
    wd = g_down.reshape(half, FF_BLK, D_MODEL)
    m_mix, x2, h2 = _out_proj_fwd(o_pad, sgu, wo, x2d, ada_raw, ada_b, g_post_mix, g_pre_ffn, tm)
    up_a, up_b, y_a, y_b, act = _ffn_up_fwd(h2, g_up, g_cw, cb8, tf)
    d_out, d_f, loss_part, d_gt2, d_g_post_ffn = _ffn_down_fwd(act, wd, x2, target, ada_raw, ada_b, g_post_ffn, tf)

    d_up, d_cw, d_cb, p_down, p_up = _ffn_down_bwd(d_f, wd, up_a, up_b, y_a, y_b, g_cw, act, h2, tf)
    p_down = p_down.reshape(N_DEV, -1, D_MODEL)
    d_x2, d_m, d_sh2, d_sc2, d_g_pre_ffn, d_gt1, d_g_post_mix = _ffn_up_bwd(
        d_up, g_up, x2, m_mix, d_out, ada_raw, ada_b, g_pre_ffn, g_post_mix, tm)
    dwo_attn, dwo_sgu = _tn_matmuls([o_pad, sgu, d_m], [(0, 2), (1, 2)], "dw_out", ts)
    dwo_attn = dwo_attn.reshape(HEADS, HEAD_PAD, D_MODEL)[:, :NOPE]
    p_out = jnp.concatenate([dwo_attn.reshape(GM_WIDTH, D_MODEL), dwo_sgu], axis=0).reshape(N_DEV, -1, D_MODEL)
    d_o, d_sgu, delta = _out_proj_bwd(d_m, wo, o_pad, tm)
    d_guv, d_ws, d_bs, d_ln_g, d_ln_b = _gmlp_bwd(z, d_sgu, ln_g, ln_b, w_sp, bias_exp, tm)
    packed = _pack_small([d_gt1, d_sh2, d_sc2, d_gt2], [d_g_post_mix, d_g_pre_ffn, d_g_post_ffn], loss_part,
                         d_ln_g, d_ln_b, d_bs, d_cb, d_ws)

    def ffn_slot(j):
        return (j % half, j // half)

    dq, dk, dv, (r_out, r_up, r_down, r_cw), (g_small,) = _attn_bwd(
        qp, kp, vp, d_o, lse, delta, tq,
        [(p_out, _plain_slot), (p_up, ffn_slot), (p_down, _plain_slot), (d_cw, ffn_slot)], [packed])
    grad_x, d_za, d_qp, d_kvp, d_sh1, d_sc1, d_g_pre_mix, d_g_q, d_g_kv = _mix_in_bwd(
        dq, dk, dv, z, d_guv, x2d, d_x2, ada_raw, ada_b, g_pre_mix, g_q, g_kv, w1t, wqt, wkv, cos_t, sin_t,
        min(256, s))
    dw1a, dw1b, dwq, dwkv = _tn_matmuls([d_za, d_guv, h1, d_qp, cqn, ckvn, d_kvp],
                                        [(0, 2), (1, 2), (3, 4), (5, 6)], "dw_mixer", ts // 2)
    d_w_in_t = jnp.concatenate([dw1a[:o2], dw1a[o2 + NOPE:o2 + NOPE + ROPE], dw1b], axis=0)
    p_in = d_w_in_t.reshape(N_DEV, -1, D_MODEL)
    p_uq = dwq.reshape(HEADS, HEAD_PAD, Q_LORA)[:, :NOPE + ROPE]
    dwk = dwkv[:, :hp].reshape(KV_LORA, HEADS, HEAD_PAD)[:, :, :NOPE]
    dwv = dwkv[:, hp:].reshape(KV_LORA, HEADS, HEAD_PAD)[:, :, :NOPE]
    p_ukv = jnp.transpose(jnp.concatenate([dwk, dwv], axis=2), (1, 0, 2))

    (g_late,), (r_in, r_uq, r_ukv) = _exchange(
        [_pack_late(d_sh1, d_sc1, d_g_pre_mix, d_g_q, d_g_kv)],
        [(p_in, _plain_slot), (p_uq, _plain_slot), (p_ukv, _plain_slot)], "final_exchange")
    small_params = [(b_ada, m_b_ada, v_b_ada), (g_pre_mix, m_g_pre_mix, v_g_pre_mix),
                    (g_post_mix, m_g_post_mix, v_g_post_mix), (g_pre_ffn, m_g_pre_ffn, v_g_pre_ffn),
                    (g_post_ffn, m_g_post_ffn, v_g_post_ffn), (g_q, m_g_q, v_g_q), (g_kv, m_g_kv, v_g_kv),
                    (gm_ln_g, m_gm_ln_g, v_gm_ln_g), (gm_ln_b, m_gm_ln_b, v_gm_ln_b),
                    (w_spatial, m_w_spatial, v_w_spatial), (b_spatial, m_b_spatial, v_b_spatial),
                    tuple(a.reshape(N_DEV, FF_BLK) for a in (conv_b, m_conv_b, v_conv_b))]
    small_out, loss_row, d_ada_all = _adam_small(g_small, g_late, small_params)
    small_out[11] = tuple(o.reshape(conv_b.shape) for o in small_out[11])
    loss = loss_row[0, 0]

    def big(recv, w, m, v, name):
        g, d, m2, v2 = _adam_reduce(recv, w[0], m[0], v[0], name)
        return g[None], d[None], m2[None], v2[None]

    def big_t(recv, w_t, m_t, v_t, name):
        return tuple(jnp.swapaxes(o, 0, 1)[None] for o in _adam_reduce(recv, w_t, m_t, v_t, name))

    a_in = big_t(r_in, w_in_t, m_in_t, v_in_t, "adam_w_in")
    a_uq = big_t(r_uq, w_uq_t, m_uq_t, v_uq_t, "adam_w_uq")
    a_ukv = big(r_ukv, w_ukv, m_w_ukv, v_w_ukv, "adam_w_ukv")
    a_out = big(r_out, w_out, m_w_out, v_w_out, "adam_w_out")
    a_up = big_t(r_up, w_up_t, m_up_t, v_up_t, "adam_w_up")
    a_down = big(r_down, w_down, m_w_down, v_w_down, "adam_w_down")
    ada_cols = w_ada.shape[2]
    d_ada_cols = lax.dynamic_slice(d_ada_all.reshape(N_DEV, 6 * D_MODEL), (0, my_slot * ada_cols), (N_DEV, ada_cols))
    pad_seq = 128 - N_DEV
    a_ada = tuple(t[None] for t in _adam_w_ada(jnp.pad(c_act.T, ((0, 0), (0, pad_seq))).astype(MXU),
                                               jnp.pad(d_ada_cols, ((0, pad_seq), (0, 0))).astype(MXU),
                                               w_ada[0], m_w_ada[0], v_w_ada[0]))
    a_cw = big(r_cw, conv_w, m_conv_w, v_conv_w, "adam_conv_w")

    def small(k):
        return small_out[k]

    per_weight = [a_ada, small(0), small(1), small(2), a_in, small(5), a_uq, small(6), a_ukv, small(7), small(8),
                  small(9), small(10), a_out, small(3), small(4), a_up, a_cw, small(11), a_down]
    outs = [loss, grad_x[None]]
    for k in range(4):
        outs += [t[k] for t in per_weight]
    return tuple(outs)
```

```python
import functools

import jax
import jax.numpy as jnp
from jax import lax
from jax.experimental import pallas as pl
from jax.experimental.pallas import tpu as pltpu

F32 = jnp.float32
MXU = jnp.bfloat16

N_DEV = 8
D_MODEL = 1024
HEADS = 8
HEAD_PAD = 128
NOPE = 64
ROPE = 32
Q_LORA = 256
KV_LORA = 128
GM_WIDTH = 512
GM_DIM = 64
GM_CHUNK = 128
CHUNK_SHIFT = 6
ROPE_THETA = 10000.0
ATTN_SCALE = (NOPE + ROPE) ** -0.5
LOG2E = 1.4426950408889634
SCALE_LOG2E = ATTN_SCALE * LOG2E
Z_COLS = 1536
FF_BLK = 704
EPS = 1e-6
ADAM_LR = 0.001
ADAM_B1 = 0.9
ADAM_B2 = 0.999
ADAM_EPS = 1e-08
ADAM_WD = 0.01
ADAM_STEP = 10
VMEM_LIMIT = 56 * 1024 * 1024
MESH = pl.DeviceIdType.MESH


def _dot(a, b):
    return jnp.dot(a, b, preferred_element_type=F32)


def _dot_nt(a, b):
    return lax.dot_general(a, b, (((1,), (1,)), ((), ())), preferred_element_type=F32)


def _dot_tn(a, b):
    return lax.dot_general(a, b, (((0,), (0,)), ((), ())), preferred_element_type=F32)


def _call(body, *, name, grid, in_specs, out_specs, out_shape, scratch=(), sem=None):
    params = pltpu.CompilerParams(dimension_semantics=sem, vmem_limit_bytes=VMEM_LIMIT)
    return pl.pallas_call(body, name=name, grid=grid, in_specs=in_specs, out_specs=out_specs,
                          out_shape=out_shape, scratch_shapes=list(scratch), compiler_params=params)


def _full(shape):
    n = len(shape)
    return pl.BlockSpec(shape, lambda *_: (0,) * n)


def _rows(tm, cols, col_block=0):
    return pl.BlockSpec((tm, cols), lambda i: (i, col_block))


def _sds(shape, dtype):
    return jax.ShapeDtypeStruct(shape, dtype)


def _row(ref, k):
    return ref[pl.ds(k, 1), :]


def _rms(x):
    r = lax.rsqrt(jnp.mean(x * x, axis=-1, keepdims=True) + EPS)
    return x * r, r


def _rms_bwd(d_hat, hat, r):
    return r * (d_hat - hat * jnp.mean(d_hat * hat, axis=-1, keepdims=True))


def _rope_partner(t):
    lane = lax.broadcasted_iota(jnp.int32, t.shape, 1)
    swapped = jnp.where(lane < NOPE + ROPE // 2, -pltpu.roll(t, HEAD_PAD - ROPE // 2, 1), pltpu.roll(t, ROPE // 2, 1))
    return jnp.where((lane >= NOPE) & (lane < NOPE + ROPE), swapped, 0.0)


def _rope(t, cos, sin):
    return t * cos + _rope_partner(t) * sin


def _rope_transposed(g, cos, sin):
    return g * cos - _rope_partner(g * sin)


def _gelu(x):
    return x * (0.5 * (1.0 + jnp.tanh(0.7978845608028654 * (x + 0.044715 * (x * x * x)))))


def _gelu_grad(x):
    t = jnp.tanh(0.7978845608028654 * (x + 0.044715 * (x * x * x)))
    return 0.5 * (1.0 + t) + 0.5 * x * (1.0 - t * t) * (0.7978845608028654 * (1.0 + 3.0 * 0.044715 * (x * x)))


def _split_dot(x, mat):
    hi = x.astype(MXU)
    lo = (x - hi.astype(F32)).astype(MXU)
    return _dot(hi, mat) + _dot(lo, mat)


def _split_dot3(x, mat):
    hi = x.astype(MXU)
    r1 = x - hi.astype(F32)
    mid = r1.astype(MXU)
    lo = (r1 - mid.astype(F32)).astype(MXU)
    return (_dot(hi, mat) + _dot(mid, mat)) + _dot(lo, mat)


def _seg_matrix():
    r = lax.broadcasted_iota(jnp.int32, (GM_WIDTH, GM_WIDTH), 0) >> 6
    c = lax.broadcasted_iota(jnp.int32, (GM_WIDTH, GM_WIDTH), 1) >> 6
    return jnp.where(r == c, 1.0 / GM_DIM, 0.0).astype(MXU)


def _spatial_mask():
    i = lax.broadcasted_iota(jnp.int32, (GM_CHUNK, GM_CHUNK), 0) >> CHUNK_SHIFT
    j = lax.broadcasted_iota(jnp.int32, (GM_CHUNK, GM_CHUNK), 1) >> CHUNK_SHIFT
    return (j <= i).astype(F32)


def _my_place():
    return lax.axis_index("x"), lax.axis_index("y"), lax.axis_index("c")


def _flat(p):
    return 4 * p[0] + 2 * p[1] + p[2]


def _comm_sems(n):
    return [pltpu.SemaphoreType.DMA((7 * n,)), pltpu.SemaphoreType.DMA((7 * n,)), pltpu.SemaphoreType.DMA((n,))]


def _gather_steps(ins, outs, sems):
    send_sems, recv_sems, local_sems = sems
    n = len(ins)
    x, y, c = _my_place()
    me, sibling = (x, y, c), (x, y, 1 - c)
    chips = [(1 - x, y), (x, 1 - y), (1 - x, 1 - y)]

    def copy(a, k, block, to, src=None):
        slot = outs[a].at[_flat(block)]
        return pltpu.make_async_remote_copy(
            src_ref=slot if src is None else src, dst_ref=slot,
            send_sem=send_sems.at[7 * a + k], recv_sem=recv_sems.at[7 * a + k],
            device_id=to, device_id_type=MESH)

    def mine():
        return [pltpu.make_async_copy(ins[a], outs[a].at[_flat(me)], local_sems.at[a]) for a in range(n)]

    def first():
        cps = []
        for a in range(n):
            cps.append(copy(a, 0, me, sibling, src=ins[a]))
            cps += [copy(a, 1 + j, me, (*chip, c), src=ins[a]) for j, chip in enumerate(chips)]
        return cps

    def passed():
        return [copy(a, 4 + j, (*chip, c), sibling) for a in range(n) for j, chip in enumerate(chips)]

    def start():
        for cp in mine() + first():
            cp.start()

    def forward():
        for a in range(n):
            for j, chip in enumerate(chips):
                copy(a, 1 + j, (*chip, c), me).wait_recv()
                copy(a, 4 + j, (*chip, c), sibling).start()

    def finish():
        for a in range(n):
            copy(a, 0, sibling, me).wait_recv()
            for j, chip in enumerate(chips):
                copy(a, 4 + j, (*chip, 1 - c), me).wait_recv()
        for cp in first() + passed():
            cp.wait_send()
        for cp in mine():
            cp.wait()

    return start, forward, finish


def _scatter_steps(ins, outs, sems, slots):
    send_sems, recv_sems, local_sems = sems
    n = len(ins)
    flips = [(fx, fy, fc) for fx in (0, 1) for fy in (0, 1) for fc in (0, 1)][1:]
    me = _my_place()

    def peer(f):
        return tuple(1 - v if b else v for v, b in zip(me, f))

    def copy(a, k, arriving=False):
        p = peer(flips[k])
        return pltpu.make_async_remote_copy(
            src_ref=ins[a].at[slots[a](_flat(p))], dst_ref=outs[a].at[_flat(p if arriving else me)],
            send_sem=send_sems.at[7 * a + k], recv_sem=recv_sems.at[7 * a + k],
            device_id=p, device_id_type=MESH)

    def mine():
        return [pltpu.make_async_copy(ins[a].at[slots[a](_flat(me))], outs[a].at[_flat(me)], local_sems.at[a])
                for a in range(n)]

    def start():
        for cp in mine() + [copy(a, k) for a in range(n) for k in range(7)]:
            cp.start()

    def finish():
        for a in range(n):
            for k in range(7):
                copy(a, k, arriving=True).wait_recv()
        for a in range(n):
            for k in range(7):
                copy(a, k).wait_send()
        for cp in mine():
            cp.wait()

    return start, finish


def _plain_slot(j):
    return (j,)


def _scatter_out_shape(arr, slot):
    return _sds((N_DEV,) + arr.shape[len(slot(0)):], arr.dtype)


def _exchange(gathered, scattered, name):
    ng, ns = len(gathered), len(scattered)
    slots = [slot for _, slot in scattered]

    def body(*refs):
        g_in, s_in = refs[:ng], refs[ng:ng + ns]
        g_out, s_out = refs[ng + ns:2 * ng + ns], refs[2 * ng + ns:2 * (ng + ns)]
        sems = refs[2 * (ng + ns):]
        g_start, g_forward, g_finish = _gather_steps(g_in, g_out, sems[:3])
        s_start, s_finish = _scatter_steps(s_in, s_out, sems[3:], slots)
        g_start()
        s_start()
        g_forward()
        g_finish()
        s_finish()

    any_spec = pl.BlockSpec(memory_space=pl.ANY)
    outs = pl.pallas_call(
        body, name=name,
        in_specs=[any_spec] * (ng + ns), out_specs=[any_spec] * (ng + ns),
        out_shape=[_sds((N_DEV,) + a.shape, a.dtype) for a in gathered]
        + [_scatter_out_shape(a, slot) for a, slot in scattered],
        scratch_shapes=_comm_sems(max(ng, 1)) + _comm_sems(max(ns, 1)),
    )(*gathered, *[a for a, _ in scattered])
    return outs[:ng], outs[ng:]


def _ada_fwd(c_all, w_ada):
    def body(c_ref, w_ref, part_ref, act_ref):
        cv = c_ref[...]
        act = cv * jax.nn.sigmoid(cv)
        act_ref[...] = act
        part_ref[...] = _dot(act.astype(MXU), w_ref[...].astype(MXU))

    cols = w_ada.shape[1]
    return _call(body, name="ada_fwd", grid=(1,),
                 in_specs=[_full(c_all.shape), _full(w_ada.shape)],
                 out_specs=[_full((N_DEV, cols)), _full(c_all.shape)],
                 out_shape=[_sds((N_DEV, cols), F32), _sds(c_all.shape, F32)])(c_all, w_ada)


def _mix_in_fwd(x, ada_raw, ada_b, g_pre, w1, g_q, g_kv, wq, wkv, cos_t, sin_t, tm):
    s = x.shape[0]

    def body(x_ref, ar_ref, ab_ref, g_ref, w1_ref, gq_ref, gkv_ref, wq_ref, wkv_ref, cos_ref, sin_ref,
             h1_ref, z_ref, qp_ref, kp_ref, vp_ref, cqn_ref, ckvn_ref, w1_n, wq_n):
        @pl.when(pl.program_id(0) == 0)
        def _():
            w1_n[...] = w1_ref[...].T
            wq_n[...] = wq_ref[...].T

        sh = _row(ar_ref, 0) + _row(ab_ref, 0)
        sc = _row(ar_ref, 1) + _row(ab_ref, 1)
        xn, _ = _rms(x_ref[...])
        hb = ((xn * g_ref[...]) * (1.0 + sc) + sh).astype(MXU)
        h1_ref[...] = hb
        z = _dot(hb, w1_n[...])
        z_ref[...] = z
        cos, sin = cos_ref[...], sin_ref[...]
        cqn = (_rms(z[:, :Q_LORA])[0] * gq_ref[...]).astype(MXU)
        ckvn = (_rms(z[:, Q_LORA:Q_LORA + KV_LORA])[0] * gkv_ref[...]).astype(MXU)
        cqn_ref[...] = cqn
        ckvn_ref[...] = ckvn
        q = _dot(cqn, wq_n[...])
        kv = _dot(ckvn, wkv_ref[...])
        k_rope = _rope(z[:, Q_LORA + KV_LORA:Q_LORA + KV_LORA + HEAD_PAD], cos, sin)
        for h in range(HEADS):
            blk = slice(h * HEAD_PAD, (h + 1) * HEAD_PAD)
            qp_ref[:, blk] = _rope(q[:, blk], cos, sin).astype(MXU)
            kp_ref[:, blk] = (kv[:, blk] + k_rope).astype(MXU)
        v_lane = lax.broadcasted_iota(jnp.int32, (tm, HEADS * HEAD_PAD), 1) & (HEAD_PAD - 1)
        vp_ref[...] = jnp.where(v_lane == NOPE, 1.0, kv[:, HEADS * HEAD_PAD:]).astype(MXU)

    hp = HEADS * HEAD_PAD
    return _call(
        body, name="mix_in_fwd", grid=(s // tm,), sem=("arbitrary",),
        in_specs=[_rows(tm, D_MODEL), _full(ada_raw.shape), _full(ada_b.shape), _full(g_pre.shape), _full(w1.shape),
                  _full(g_q.shape), _full(g_kv.shape), _full(wq.shape), _full(wkv.shape),
                  _rows(tm, HEAD_PAD), _rows(tm, HEAD_PAD)],
        out_specs=[_rows(tm, D_MODEL), _rows(tm, Z_COLS), _rows(tm, hp), _rows(tm, hp), _rows(tm, hp),
                   _rows(tm, Q_LORA), _rows(tm, KV_LORA)],
        out_shape=[_sds((s, D_MODEL), MXU), _sds((s, Z_COLS), F32), _sds((s, hp), MXU), _sds((s, hp), MXU),
                   _sds((s, hp), MXU), _sds((s, Q_LORA), MXU), _sds((s, KV_LORA), MXU)],
        scratch=[pltpu.VMEM(w1.shape[::-1], MXU), pltpu.VMEM(wq.shape[::-1], MXU)],
    )(x, ada_raw, ada_b, g_pre, w1, g_q, g_kv, wq, wkv, cos_t, sin_t)


def _gm_norm(zv, seg):
    gv = _gelu(zv)
    cen = gv - _split_dot(gv, seg)
    rstd = lax.rsqrt(_split_dot(cen * cen, seg) + EPS)
    return gv, cen * rstd, rstd


def _gm_pairs(rows):
    first = lax.broadcasted_iota(jnp.int32, (rows, 2 * GM_DIM), 1) < GM_DIM
    return [(slice(p * 2 * GM_DIM, (p + 1) * 2 * GM_DIM), first) for p in range(HEADS // 2)]


def _gm_mix(wm, vb, rows, transposed=False):
    dot = _dot_tn if transposed else _dot
    return jnp.concatenate([jnp.where(first, dot(wm[2 * p], vb[:, lanes]), dot(wm[2 * p + 1], vb[:, lanes]))
                            for p, (lanes, first) in enumerate(_gm_pairs(rows))], axis=1)


def _gmlp_fwd(z, ln_g, ln_b, w_sp, bias_exp, tm):
    s = z.shape[0]
    nblk = tm // GM_CHUNK

    def body(zu_ref, zv_ref, lg_ref, lb_ref, w_ref, be_ref, sgu_ref):
        seg = _seg_matrix()
        mask = _spatial_mask()
        wm = [(w_ref[h] * mask).astype(MXU) for h in range(HEADS)]
        gu = _gelu(zu_ref[...])
        _, vhat, _ = _gm_norm(zv_ref[...], seg)
        vln = (vhat * lg_ref[...] + lb_ref[...]).astype(MXU)
        for n in range(nblk):
            rows = slice(n * GM_CHUNK, (n + 1) * GM_CHUNK)
            mixed = _gm_mix(wm, vln[rows], GM_CHUNK) + be_ref[...]
            sgu_ref[rows, :] = (gu[rows] * mixed).astype(MXU)

    return _call(
        body, name="gmlp_fwd", grid=(s // tm,), sem=("parallel",),
        in_specs=[_rows(tm, GM_WIDTH, 1), _rows(tm, GM_WIDTH, 2), _full(ln_g.shape), _full(ln_b.shape),
                  _full(w_sp.shape), _full(bias_exp.shape)],
        out_specs=_rows(tm, GM_WIDTH), out_shape=_sds((s, GM_WIDTH), MXU),
    )(z, z, ln_g, ln_b, w_sp, bias_exp)


def _chunk_mask(n_q, n_k, q_off):
    qc = (q_off + lax.broadcasted_iota(jnp.int32, (n_q, n_k), 0)) >> CHUNK_SHIFT
    kc = lax.broadcasted_iota(jnp.int32, (n_q, n_k), 1) >> CHUNK_SHIFT
    return kc <= qc


NEG_BIG = -1e30
ATTN_HEADS_PER_STEP = 2


def _attn_fwd(qp, kp, vp, tq, gathered):
    s = qp.shape[0]
    nq = s // tq
    hb = ATTN_HEADS_PER_STEP
    groups = HEADS // hb
    width = hb * HEAD_PAD
    ng = len(gathered)

    def body(q_ref, k_ref, v_ref, *rest):
        g_in, (o_ref, lse_ref), g_out = rest[:ng], rest[ng:ng + 2], rest[ng + 2:2 * ng + 2]
        m_sc, acc_sc = rest[2 * ng + 2:2 * ng + 4]
        g_start, g_forward, g_finish = _gather_steps(g_in, g_out, rest[2 * ng + 4:])
        g, i = pl.program_id(0), pl.program_id(1)
        pl.when((g == 0) & (i == 0))(g_start)
        pl.when((g == groups - 1) & (i == nq // 2))(g_forward)
        m_sc[...] = jnp.full(m_sc.shape, NEG_BIG, F32)
        acc_sc[...] = jnp.zeros(acc_sc.shape, F32)

        def tile(j, masked, n_tiles=1):
            n_k = n_tiles * tq
            rows = pl.ds(pl.multiple_of(j * tq, tq), n_k)
            for hh in range(hb):
                lanes = slice(hh * HEAD_PAD, (hh + 1) * HEAD_PAD)
                sc = _dot_nt(q_ref[:, lanes], k_ref[rows, lanes])
                if masked:
                    sc = jnp.where(_chunk_mask(tq, n_k, n_k - tq), sc, NEG_BIG)
                blocks = [sc[:, b * 128:(b + 1) * 128] for b in range(n_k // 128)]
                m_prev = m_sc[hh]
                m_tile = jnp.max(functools.reduce(jnp.maximum, blocks), axis=-1, keepdims=True)
                m_new = jnp.maximum(m_prev, m_tile)
                alpha = jnp.exp2((m_prev - m_new) * SCALE_LOG2E)
                p = jnp.concatenate([jnp.exp2((b - m_new) * SCALE_LOG2E) for b in blocks], axis=1).astype(MXU)
                acc_sc[hh] = alpha * acc_sc[hh] + _dot(p, v_ref[rows, lanes])
                m_sc[hh] = m_new

        def off_diagonal_pair(p, carry):
            tile(2 * p, False, n_tiles=2)
            return carry

        lax.fori_loop(0, i // 2, off_diagonal_pair, 0)

        @pl.when(i % 2 == 1)
        def _():
            tile(i - 1, True, n_tiles=2)

        @pl.when(i % 2 == 0)
        def _():
            tile(i, True)
        for hh in range(hb):
            lanes = slice(hh * HEAD_PAD, (hh + 1) * HEAD_PAD)
            acc = acc_sc[hh]
            denom = acc[:, NOPE:NOPE + 1]
            o_ref[:, lanes] = (acc / denom).astype(MXU)
            lse_ref[hh] = m_sc[hh][:, :1] * SCALE_LOG2E + jnp.log(denom) * LOG2E
        pl.when((g == groups - 1) & (i == nq - 1))(g_finish)

    q_spec = pl.BlockSpec((tq, width), lambda g, i: (i, g))
    kv_spec = pl.BlockSpec((s, width), lambda g, i: (0, g))
    any_spec = pl.BlockSpec(memory_space=pl.ANY)
    outs = _call(
        body, name="attn_fwd", grid=(groups, nq), sem=("arbitrary", "arbitrary"),
        in_specs=[q_spec, kv_spec, kv_spec] + [any_spec] * ng,
        out_specs=[q_spec, pl.BlockSpec((hb, tq, 1), lambda g, i: (g, i, 0))] + [any_spec] * ng,
        out_shape=[_sds(qp.shape, MXU), _sds((HEADS, s, 1), F32)]
        + [_sds((N_DEV,) + a.shape, a.dtype) for a in gathered],
        scratch=[pltpu.VMEM((hb, tq, HEAD_PAD), F32), pltpu.VMEM((hb, tq, HEAD_PAD), F32)] + _comm_sems(ng),
    )(qp, kp, vp, *gathered)
    return outs[0], outs[1], outs[2:]


def _out_proj_fwd(o_pad, sgu, wo, x, ada_raw, ada_b, g_post_mix, g_pre_ffn, tm):
    s = x.shape[0]
    hp = HEADS * HEAD_PAD

    def body(o_ref, sgu_ref, wo_ref, x_ref, ar_ref, ab_ref, gpm_ref, gpf_ref, m_ref, x2_ref, h2_ref):
        gt1 = _row(ar_ref, 2) + _row(ab_ref, 2)
        sh2 = _row(ar_ref, 3) + _row(ab_ref, 3)
        sc2 = _row(ar_ref, 4) + _row(ab_ref, 4)
        m = _dot(o_ref[...], wo_ref[pl.ds(0, hp), :]) + _dot(sgu_ref[...], wo_ref[pl.ds(hp, GM_WIDTH), :])
        m_ref[...] = m
        x2 = x_ref[...] + gt1 * (_rms(m)[0] * gpm_ref[...])
        x2_ref[...] = x2
        h2_ref[...] = ((_rms(x2)[0] * gpf_ref[...]) * (1.0 + sc2) + sh2).astype(MXU)

    return _call(
        body, name="out_proj_fwd", grid=(s // tm,), sem=("parallel",),
        in_specs=[_rows(tm, hp), _rows(tm, GM_WIDTH), _full(wo.shape), _rows(tm, D_MODEL), _full(ada_raw.shape),
                  _full(ada_b.shape), _full(g_post_mix.shape), _full(g_pre_ffn.shape)],
        out_specs=[_rows(tm, D_MODEL)] * 3,
        out_shape=[_sds((s, D_MODEL), F32), _sds((s, D_MODEL), F32), _sds((s, D_MODEL), MXU)],
    )(o_pad, sgu, wo, x, ada_raw, ada_b, g_post_mix, g_pre_ffn)


def _conv(u, halo, cw_ref, cb_ref):
    ext = jnp.concatenate([halo, u], axis=0)
    m1, m2 = pltpu.roll(ext, 1, 0)[8:], pltpu.roll(ext, 2, 0)[8:]
    return cb_ref[0] + ((m2 * cw_ref[0, pl.ds(0, 1), :] + m1 * cw_ref[0, pl.ds(1, 1), :]) + u * cw_ref[0, pl.ds(2, 1), :])


ROW_SUB = 256


def _sub_blocks(tm):
    return [slice(r, r + ROW_SUB) for r in range(0, tm, ROW_SUB)]


def _ffn_up_fwd(h2, w_up, conv_w, conv_b, tm):
    s = h2.shape[0]
    half = N_DEV // 2

    def body(h_ref, wa_ref, wb_ref, cwa_ref, cwb_ref, cba_ref, cbb_ref,
             ua_ref, ub_ref, ya_ref, yb_ref, act_ref, halo_a, halo_b, wa_t, wb_t):
        i = pl.program_id(1)

        @pl.when(i == 0)
        def _():
            halo_a[...] = jnp.zeros(halo_a.shape, F32)
            halo_b[...] = jnp.zeros(halo_b.shape, F32)
            wa_t[...] = wa_ref[0].T
            wb_t[...] = wb_ref[0].T

        ha, hb = halo_a[...], halo_b[...]
        for rows in _sub_blocks(tm):
            h = h_ref[rows, :]
            ua = _dot(h, wa_t[...])
            ub = _dot(h, wb_t[...])
            ua_ref[0, rows, :] = ua
            ub_ref[0, rows, :] = ub
            ya = _conv(ua, ha, cwa_ref, cba_ref)
            yb = _conv(ub, hb, cwb_ref, cbb_ref)
            ya_ref[0, rows, :] = ya
            yb_ref[0, rows, :] = yb
            ha, hb = ua[ROW_SUB - 8:], ub[ROW_SUB - 8:]
            act_ref[0, rows, :] = ((ya * jax.nn.sigmoid(ya)) * yb).astype(MXU)
        halo_a[...] = ha
        halo_b[...] = hb

    def blk(shape, off):
        return pl.BlockSpec(shape, lambda j, i: (j + off, 0, 0))

    def tok(off=0):
        return pl.BlockSpec((1, tm, FF_BLK), lambda j, i: (j + off, i, 0))

    return _call(
        body, name="ffn_up_fwd", grid=(half, s // tm), sem=("parallel", "arbitrary"),
        in_specs=[pl.BlockSpec((tm, D_MODEL), lambda j, i: (i, 0)),
                  blk((1, FF_BLK, D_MODEL), 0), blk((1, FF_BLK, D_MODEL), half),
                  blk((1, 3, FF_BLK), 0), blk((1, 3, FF_BLK), half), blk((1, 1, FF_BLK), 0), blk((1, 1, FF_BLK), half)],
        out_specs=[tok()] * 5,
        out_shape=[_sds((half, s, FF_BLK), F32)] * 4 + [_sds((half, s, FF_BLK), MXU)],
        scratch=[pltpu.VMEM((8, FF_BLK), F32), pltpu.VMEM((8, FF_BLK), F32),
                 pltpu.VMEM((D_MODEL, FF_BLK), MXU), pltpu.VMEM((D_MODEL, FF_BLK), MXU)],
    )(h2, w_up, w_up, conv_w, conv_w, conv_b, conv_b)


def _ffn_down_fwd(act, wd, x2, target, ada_raw, ada_b, g_post_ffn, tm):
    s = x2.shape[0]
    half = N_DEV // 2

    def body(act_ref, wd_ref, x2_ref, t_ref, ar_ref, ab_ref, g_ref, dout_ref, df_ref, loss_ref, dgt_ref, dg_ref):
        i = pl.program_id(0)

        @pl.when(i == 0)
        def _():
            loss_ref[...] = jnp.zeros(loss_ref.shape, F32)
            dgt_ref[...] = jnp.zeros(dgt_ref.shape, F32)
            dg_ref[...] = jnp.zeros(dg_ref.shape, F32)

        gt2 = _row(ar_ref, 5) + _row(ab_ref, 5)
        g = g_ref[...]
        for rows in _sub_blocks(tm):
            f = _dot(act_ref[0, rows, :], wd_ref[0])
            for j in range(1, half):
                f = f + _dot(act_ref[j, rows, :], wd_ref[j])
            fhat, rf = _rms(f)
            fn = fhat * g
            err = (x2_ref[rows, :] + gt2 * fn) - t_ref[rows, :]
            loss_ref[...] += 0.5 * jnp.sum(jnp.mean(err * err, axis=-1, keepdims=True))
            d_out = err * (1.0 / D_MODEL)
            dout_ref[rows, :] = d_out
            dgt_ref[...] += jnp.sum(d_out * fn, axis=0, keepdims=True)
            d_fn = d_out * gt2
            dg_ref[...] += jnp.sum(d_fn * fhat, axis=0, keepdims=True)
            df_ref[rows, :] = _rms_bwd(d_fn * g, fhat, rf).astype(MXU)

    vec = pl.BlockSpec((1, D_MODEL), lambda i: (0, 0))
    return _call(
        body, name="ffn_down_fwd", grid=(s // tm,), sem=("arbitrary",),
        in_specs=[pl.BlockSpec((half, tm, FF_BLK), lambda i: (0, i, 0)), _full(wd.shape), _rows(tm, D_MODEL),
                  _rows(tm, D_MODEL), _full(ada_raw.shape), _full(ada_b.shape), _full(g_post_ffn.shape)],
        out_specs=[_rows(tm, D_MODEL), _rows(tm, D_MODEL), pl.BlockSpec((1, 128), lambda i: (0, 0)), vec, vec],
        out_shape=[_sds((s, D_MODEL), F32), _sds((s, D_MODEL), MXU), _sds((1, 128), F32),
                   _sds((1, D_MODEL), F32), _sds((1, D_MODEL), F32)],
    )(act, wd, x2, target, ada_raw, ada_b, g_post_ffn)


def _ffn_down_bwd(d_f, wd, up_a, up_b, y_a, y_b, conv_w, act, h2, tm):
    s = d_f.shape[0]
    half = N_DEV // 2
    nt = s // tm

    def body(df_ref, wd_ref, ua_ref, ub_ref, ya_ref, yb_ref, cwa_ref, cwb_ref, act_ref, h2_ref,
             dup_ref, dcw_ref, dcb_ref, pd_ref, pu_ref, next_a, next_b, acc_d, acc_a, acc_b, wd_t):
        i = pl.program_id(1)

        @pl.when(i == 0)
        def _():
            next_a[...] = jnp.zeros(next_a.shape, F32)
            next_b[...] = jnp.zeros(next_b.shape, F32)
            dcw_ref[...] = jnp.zeros(dcw_ref.shape, F32)
            dcb_ref[...] = jnp.zeros(dcb_ref.shape, F32)
            for acc in (acc_d, acc_a, acc_b):
                acc[...] = jnp.zeros(acc.shape, F32)
            wd_t[...] = wd_ref[0].T

        def conv_bwd(d_y, u, nxt, cw_ref, part, rows):
            ext = jnp.concatenate([d_y, nxt], axis=0)
            p1 = pltpu.roll(ext, ROW_SUB + 7, 0)[:ROW_SUB]
            p2 = pltpu.roll(ext, ROW_SUB + 6, 0)[:ROW_SUB]
            d_u = (d_y * cw_ref[0, pl.ds(2, 1), :] + p1 * cw_ref[0, pl.ds(1, 1), :]) + p2 * cw_ref[0, pl.ds(0, 1), :]
            dup_ref[0, part, rows, :] = d_u.astype(MXU)
            dcb_ref[0, part] += jnp.sum(d_y, axis=0, keepdims=True)
            dcw_ref[0, part, pl.ds(0, 1), :] += jnp.sum(p2 * u, axis=0, keepdims=True)
            dcw_ref[0, part, pl.ds(1, 1), :] += jnp.sum(p1 * u, axis=0, keepdims=True)
            dcw_ref[0, part, pl.ds(2, 1), :] += jnp.sum(d_y * u, axis=0, keepdims=True)
            return d_y[:8]

        nxa, nxb = next_a[...], next_b[...]
        for rows in reversed(_sub_blocks(tm)):
            d_act = _dot(df_ref[rows, :], wd_t[...])
            ya, yb = ya_ref[0, rows, :], yb_ref[0, rows, :]
            sig = jax.nn.sigmoid(ya)
            d_ya = d_act * yb * (sig * (1.0 + ya * (1.0 - sig)))
            d_yb = d_act * (ya * sig)
            nxa = conv_bwd(d_ya, ua_ref[0, rows, :], nxa, cwa_ref, 0, rows)
            nxb = conv_bwd(d_yb, ub_ref[0, rows, :], nxb, cwb_ref, 1, rows)
        next_a[...] = nxa
        next_b[...] = nxb
        acc_d[...] += _dot_tn(act_ref[0], df_ref[...])
        acc_a[...] += _dot_tn(dup_ref[0, 0], h2_ref[...])
        acc_b[...] += _dot_tn(dup_ref[0, 1], h2_ref[...])

        @pl.when(i == nt - 1)
        def _():
            pd_ref[0] = acc_d[...].astype(MXU)
            pu_ref[0, 0] = acc_a[...].astype(MXU)
            pu_ref[0, 1] = acc_b[...].astype(MXU)

    def rev(i):
        return nt - 1 - i

    def blk(shape, off):
        return pl.BlockSpec(shape, lambda j, i: (j + off, 0, 0))

    tok = pl.BlockSpec((1, tm, FF_BLK), lambda j, i: (j, rev(i), 0))
    acc3 = pl.BlockSpec((1, 2, 3, FF_BLK), lambda j, i: (j, 0, 0, 0))
    acc1 = pl.BlockSpec((1, 2, 1, FF_BLK), lambda j, i: (j, 0, 0, 0))
    return _call(
        body, name="ffn_down_bwd", grid=(half, nt), sem=("parallel", "arbitrary"),
        in_specs=[pl.BlockSpec((tm, D_MODEL), lambda j, i: (rev(i), 0)), blk((1, FF_BLK, D_MODEL), 0),
                  tok, tok, tok, tok, blk((1, 3, FF_BLK), 0), blk((1, 3, FF_BLK), half),
                  tok, pl.BlockSpec((tm, D_MODEL), lambda j, i: (rev(i), 0))],
        out_specs=[pl.BlockSpec((1, 2, tm, FF_BLK), lambda j, i: (j, 0, rev(i), 0)), acc3, acc1,
                   pl.BlockSpec((1, FF_BLK, D_MODEL), lambda j, i: (j, 0, 0)),
                   pl.BlockSpec((1, 2, FF_BLK, D_MODEL), lambda j, i: (j, 0, 0, 0))],
        out_shape=[_sds((half, 2, s, FF_BLK), MXU), _sds((half, 2, 3, FF_BLK), F32), _sds((half, 2, 1, FF_BLK), F32),
                   _sds((half, FF_BLK, D_MODEL), MXU), _sds((half, 2, FF_BLK, D_MODEL), MXU)],
        scratch=[pltpu.VMEM((8, FF_BLK), F32), pltpu.VMEM((8, FF_BLK), F32)]
        + [pltpu.VMEM((FF_BLK, D_MODEL), F32)] * 3 + [pltpu.VMEM((D_MODEL, FF_BLK), MXU)],
    )(d_f, wd, up_a, up_b, y_a, y_b, conv_w, conv_w, act, h2)


def _ffn_up_bwd(d_up, w_up, x2, m, d_out, ada_raw, ada_b, g_pre_ffn, g_post_mix, tm):
    s = x2.shape[0]
    half = N_DEV // 2

    def body(dup_ref, w_ref, x2_ref, m_ref, dout_ref, ar_ref, ab_ref, gpf_ref, gpm_ref,
             dx_ref, dm_ref, dsh_ref, dsc_ref, dgpf_ref, dgt1_ref, dgpm_ref):
        i = pl.program_id(0)

        @pl.when(i == 0)
        def _():
            for r in (dsh_ref, dsc_ref, dgpf_ref, dgt1_ref, dgpm_ref):
                r[...] = jnp.zeros(r.shape, F32)

        gt1 = _row(ar_ref, 2) + _row(ab_ref, 2)
        sc2 = _row(ar_ref, 4) + _row(ab_ref, 4)
        gpf, gpm = gpf_ref[...], gpm_ref[...]
        d_h2 = _dot(dup_ref[0, 0], w_ref[0])
        for j in range(1, half):
            d_h2 = d_h2 + _dot(dup_ref[j, 0], w_ref[j])
        for j in range(half):
            d_h2 = d_h2 + _dot(dup_ref[j, 1], w_ref[half + j])
        x2n, r2 = _rms(x2_ref[...])
        dsh_ref[...] += jnp.sum(d_h2, axis=0, keepdims=True)
        dsc_ref[...] += jnp.sum(d_h2 * (x2n * gpf), axis=0, keepdims=True)
        d_mod = d_h2 * (1.0 + sc2)
        dgpf_ref[...] += jnp.sum(d_mod * x2n, axis=0, keepdims=True)
        d_x2 = dout_ref[...] + _rms_bwd(d_mod * gpf, x2n, r2)
        dx_ref[...] = d_x2
        mhat, rm = _rms(m_ref[...])
        dgt1_ref[...] += jnp.sum(d_x2 * (mhat * gpm), axis=0, keepdims=True)
        d_mn = d_x2 * gt1
        dgpm_ref[...] += jnp.sum(d_mn * mhat, axis=0, keepdims=True)
        dm_ref[...] = _rms_bwd(d_mn * gpm, mhat, rm).astype(MXU)

    vec = pl.BlockSpec((1, D_MODEL), lambda i: (0, 0))
    tok = pl.BlockSpec((half, 2, tm, FF_BLK), lambda i: (0, 0, i, 0))
    return _call(
        body, name="ffn_up_bwd", grid=(s // tm,), sem=("arbitrary",),
        in_specs=[tok, _full(w_up.shape), _rows(tm, D_MODEL), _rows(tm, D_MODEL), _rows(tm, D_MODEL),
                  _full(ada_raw.shape), _full(ada_b.shape), _full(g_pre_ffn.shape), _full(g_post_mix.shape)],
        out_specs=[_rows(tm, D_MODEL), _rows(tm, D_MODEL), vec, vec, vec, vec, vec],
        out_shape=[_sds((s, D_MODEL), F32), _sds((s, D_MODEL), MXU)] + [_sds((1, D_MODEL), F32)] * 5,
    )(d_up, w_up, x2, m, d_out, ada_raw, ada_b, g_pre_ffn, g_post_mix)


def _out_proj_bwd(d_m, wo, o_pad, tm):
    s = d_m.shape[0]
    hp = HEADS * HEAD_PAD

    def body(dm_ref, wo_ref, o_ref, do_ref, dsgu_ref, delta_ref, wo_t):
        @pl.when(pl.program_id(0) == 0)
        def _():
            wo_t[...] = wo_ref[...].T

        d_cat = _dot(dm_ref[...], wo_t[...])
        d_o = d_cat[:, :hp]
        do_ref[...] = d_o.astype(MXU)
        dsgu_ref[...] = d_cat[:, hp:]
        prod = d_o * o_ref[...].astype(F32)
        for h in range(HEADS):
            delta_ref[h] = jnp.sum(prod[:, h * HEAD_PAD:(h + 1) * HEAD_PAD], axis=-1, keepdims=True)

    return _call(
        body, name="out_proj_bwd", grid=(s // tm,), sem=("arbitrary",),
        in_specs=[_rows(tm, D_MODEL), _full(wo.shape), _rows(tm, hp)],
        out_specs=[_rows(tm, hp), _rows(tm, GM_WIDTH), pl.BlockSpec((HEADS, tm, 1), lambda i: (0, i, 0))],
        out_shape=[_sds((s, hp), MXU), _sds((s, GM_WIDTH), F32), _sds((HEADS, s, 1), F32)],
        scratch=[pltpu.VMEM(wo.shape[::-1], MXU)],
    )(d_m, wo, o_pad)


def _attn_bwd(qp, kp, vp, d_o, lse, delta, tq, scattered, gathered):
    s = qp.shape[0]
    nq = s // tq
    hb = ATTN_HEADS_PER_STEP
    groups = HEADS // hb
    width = hb * HEAD_PAD
    ns, ng = len(scattered), len(gathered)
    nc = ns + ng
    slots = [slot for _, slot in scattered]

    def body(q_ref, k_ref, v_ref, do_ref, lse_ref, dl_ref, *rest):
        c_in, (dq_ref, dk_ref, dv_ref), c_out = rest[:nc], rest[nc:nc + 3], rest[nc + 3:2 * nc + 3]
        dk_sc, dv_sc = rest[2 * nc + 3:2 * nc + 5]
        sems = rest[2 * nc + 5:]
        s_start, s_finish = _scatter_steps(c_in[:ns], c_out[:ns], sems[:3], slots)
        g_start, g_forward, g_finish = _gather_steps(c_in[ns:], c_out[ns:], sems[3:])
        g, j = pl.program_id(0), pl.program_id(1)

        @pl.when((g == 0) & (j == 0))
        def _():
            s_start()
            g_start()

        pl.when((g == groups - 1) & (j == max(nq - 2, 0)))(g_forward)

        @pl.when(j == 0)
        def _():
            dq_ref[...] = jnp.zeros(dq_ref.shape, F32)

        dk_sc[...] = jnp.zeros(dk_sc.shape, F32)
        dv_sc[...] = jnp.zeros(dv_sc.shape, F32)

        def tile(i, masked):
            rows = pl.ds(pl.multiple_of(i * tq, tq), tq)
            for hh in range(hb):
                lanes = slice(hh * HEAD_PAD, (hh + 1) * HEAD_PAD)
                q, do, k = q_ref[rows, lanes], do_ref[rows, lanes], k_ref[:, lanes]
                sc = _dot_nt(q, k)
                if masked:
                    sc = jnp.where(_chunk_mask(tq, tq, 0), sc, NEG_BIG)
                p = jnp.exp2(sc * SCALE_LOG2E - lse_ref[hh, rows, :])
                dv_sc[hh] += _dot_tn(p.astype(MXU), do)
                dp = _dot_nt(do, v_ref[:, lanes])
                ds = (p * (dp - dl_ref[hh, rows, :])).astype(MXU)
                dk_sc[hh] += _dot_tn(ds, q)
                dq_ref[rows, lanes] += _dot(ds, k) * ATTN_SCALE

        def tile_pair(i0, first_masked):
            both = pl.ds(pl.multiple_of(i0 * tq, tq), 2 * tq)
            for hh in range(hb):
                lanes = slice(hh * HEAD_PAD, (hh + 1) * HEAD_PAD)
                k, v = k_ref[:, lanes], v_ref[:, lanes]
                ps, dss = [], []
                for t in range(2):
                    rows = pl.ds(pl.multiple_of((i0 + t) * tq, tq), tq)
                    q, do = q_ref[rows, lanes], do_ref[rows, lanes]
                    sc = _dot_nt(q, k)
                    if first_masked and t == 0:
                        sc = jnp.where(_chunk_mask(tq, tq, 0), sc, NEG_BIG)
                    p = jnp.exp2(sc * SCALE_LOG2E - lse_ref[hh, rows, :])
                    ds = (p * (_dot_nt(do, v) - dl_ref[hh, rows, :])).astype(MXU)
                    dq_ref[rows, lanes] += _dot(ds, k) * ATTN_SCALE
                    ps.append(p.astype(MXU))
                    dss.append(ds)
                dv_sc[hh] += _dot_tn(jnp.concatenate(ps, axis=0), do_ref[both, lanes])
                dk_sc[hh] += _dot_tn(jnp.concatenate(dss, axis=0), q_ref[both, lanes])

        odd = (nq - j) % 2

        @pl.when(odd == 1)
        def _():
            tile(j, True)

        @pl.when(odd == 0)
        def _():
            tile_pair(j, True)

        first = j + 2 - odd

        def later_pair(pair, carry):
            tile_pair(first + 2 * pair, False)
            return carry

        lax.fori_loop(0, (nq - first) // 2, later_pair, 0)
        for hh in range(hb):
            lanes = slice(hh * HEAD_PAD, (hh + 1) * HEAD_PAD)
            dk_ref[:, lanes] = dk_sc[hh] * ATTN_SCALE
            dv_ref[:, lanes] = dv_sc[hh]
        @pl.when((g == groups - 1) & (j == nq - 1))
        def _():
            g_finish()
            s_finish()

    seq_spec = pl.BlockSpec((s, width), lambda g, j: (0, g))
    kv_spec = pl.BlockSpec((tq, width), lambda g, j: (j, g))
    col_spec = pl.BlockSpec((hb, s, 1), lambda g, j: (g, 0, 0))
    any_spec = pl.BlockSpec(memory_space=pl.ANY)
    outs = _call(
        body, name="attn_bwd", grid=(groups, nq), sem=("arbitrary", "arbitrary"),
        in_specs=[seq_spec, kv_spec, kv_spec, seq_spec, col_spec, col_spec] + [any_spec] * nc,
        out_specs=[seq_spec, kv_spec, kv_spec] + [any_spec] * nc,
        out_shape=[_sds(qp.shape, F32), _sds(qp.shape, F32), _sds(qp.shape, F32)]
        + [_scatter_out_shape(a, slot) for a, slot in scattered]
        + [_sds((N_DEV,) + a.shape, a.dtype) for a in gathered],
        scratch=[pltpu.VMEM((hb, tq, HEAD_PAD), F32), pltpu.VMEM((hb, tq, HEAD_PAD), F32)]
        + _comm_sems(ns) + _comm_sems(ng),
    )(qp, kp, vp, d_o, lse, delta, *[a for a, _ in scattered], *gathered)
    return outs[0], outs[1], outs[2], outs[3:3 + ns], outs[3 + ns:]


def _gmlp_bwd(z, d_sgu, ln_g, ln_b, w_sp, bias_exp, tm):
    s = z.shape[0]
    nblk = tm // GM_CHUNK

    def body(zu_ref, zv_ref, dsgu_ref, lg_ref, lb_ref, w_ref, be_ref,
             dguv_ref, dws_ref, dbs_ref, dlg_ref, dlb_ref, dbe_sc, dvln_sc, dlg_sc, dlb_sc):
        i = pl.program_id(0)

        @pl.when(i == 0)
        def _():
            for r in (dws_ref, dlg_sc, dlb_sc, dbe_sc):
                r[...] = jnp.zeros(r.shape, F32)

        seg = _seg_matrix()
        mask = _spatial_mask()
        wm = [(w_ref[h] * mask).astype(MXU) for h in range(HEADS)]
        zu, zv = zu_ref[...], zv_ref[...]
        gu = _gelu(zu)
        _, vhat, rstd = _gm_norm(zv, seg)
        lg = lg_ref[...]
        vln = (vhat * lg + lb_ref[...]).astype(MXU)
        d_sgu = dsgu_ref[...]
        for n in range(nblk):
            rows = slice(n * GM_CHUNK, (n + 1) * GM_CHUNK)
            vb = vln[rows]
            mixed = _gm_mix(wm, vb, GM_CHUNK) + be_ref[...]
            d_mixed = d_sgu[rows] * gu[rows]
            dguv_ref[rows, pl.ds(0, GM_WIDTH)] = ((d_sgu[rows] * mixed) * _gelu_grad(zu[rows])).astype(MXU)
            dbe_sc[...] += d_mixed
            dmb = d_mixed.astype(MXU)
            for p, (lanes, first) in enumerate(_gm_pairs(GM_CHUNK)):
                dm_pair, v_pair = dmb[:, lanes], vb[:, lanes]
                zero = jnp.zeros_like(dm_pair)
                dws_ref[2 * p] += _dot_nt(jnp.where(first, dm_pair, zero), v_pair)
                dws_ref[2 * p + 1] += _dot_nt(jnp.where(first, zero, dm_pair), v_pair)
            dvln_sc[rows, :] = _gm_mix(wm, dmb, GM_CHUNK, transposed=True)
        d_vln = dvln_sc[...]
        dlg_sc[...] += jnp.sum(d_vln * vhat, axis=0, keepdims=True)
        dlb_sc[...] += jnp.sum(d_vln, axis=0, keepdims=True)
        d_vhat = d_vln * lg
        d_gv = rstd * ((d_vhat - _split_dot(d_vhat, seg)) - vhat * _split_dot(d_vhat * vhat, seg))
        dguv_ref[:, pl.ds(GM_WIDTH, GM_WIDTH)] = (d_gv * _gelu_grad(zv)).astype(MXU)

        @pl.when(i == pl.num_programs(0) - 1)
        def _():
            for h in range(HEADS):
                dws_ref[h] = dws_ref[h] * mask
            hrow = lax.broadcasted_iota(jnp.int32, (HEADS, GM_WIDTH), 0)
            hlane = lax.broadcasted_iota(jnp.int32, (HEADS, GM_WIDTH), 1) >> 6
            ind = jnp.where(hrow == hlane, 1.0, 0.0).astype(MXU)
            acc = dbe_sc[...]
            hi = acc.astype(MXU)
            lo = (acc - hi.astype(F32)).astype(MXU)
            dbs_ref[...] = _dot_nt(ind, hi) + _dot_nt(ind, lo)
            pick = (lax.broadcasted_iota(jnp.int32, (GM_WIDTH, GM_DIM), 0) & (GM_DIM - 1)
                    == lax.broadcasted_iota(jnp.int32, (GM_WIDTH, GM_DIM), 1))
            pick = jnp.where(pick, 1.0, 0.0).astype(MXU)
            for src, dst in ((dlg_sc, dlg_ref), (dlb_sc, dlb_ref)):
                spread = jnp.where(hrow == hlane, jnp.broadcast_to(src[...], (HEADS, GM_WIDTH)), 0.0)
                dst[...] = _split_dot3(spread, pick)

    return _call(
        body, name="gmlp_bwd", grid=(s // tm,), sem=("arbitrary",),
        in_specs=[_rows(tm, GM_WIDTH, 1), _rows(tm, GM_WIDTH, 2), _rows(tm, GM_WIDTH), _full(ln_g.shape),
                  _full(ln_b.shape), _full(w_sp.shape), _full(bias_exp.shape)],
        out_specs=[_rows(tm, 2 * GM_WIDTH), _full(w_sp.shape), _full((HEADS, GM_CHUNK)), _full((HEADS, GM_DIM)),
                   _full((HEADS, GM_DIM))],
        out_shape=[_sds((s, 2 * GM_WIDTH), MXU), _sds(w_sp.shape, F32), _sds((HEADS, GM_CHUNK), F32),
                   _sds((HEADS, GM_DIM), F32), _sds((HEADS, GM_DIM), F32)],
        scratch=[pltpu.VMEM((GM_CHUNK, GM_WIDTH), F32), pltpu.VMEM((tm, GM_WIDTH), F32),
                 pltpu.VMEM((1, GM_WIDTH), F32), pltpu.VMEM((1, GM_WIDTH), F32)],
    )(z, z, d_sgu, ln_g, ln_b, w_sp, bias_exp)


def _mix_in_bwd(dq, dk, dv, z, d_guv, x, d_x_part, ada_raw, ada_b, g_pre, g_q, g_kv, w1t, wqt, wkv,
                cos_t, sin_t, tm):
    s = x.shape[0]
    hp = HEADS * HEAD_PAD
    za = Q_LORA + KV_LORA + HEAD_PAD

    def body(dq_ref, dk_ref, dv_ref, z_ref, dguv_ref, x_ref, dxp_ref, ar_ref, ab_ref, g_ref, gq_ref, gkv_ref,
             w1_ref, wq_ref, wkv_ref, cos_ref, sin_ref,
             gx_ref, dza_ref, dqp_ref, dkvp_ref, dsh_ref, dsc_ref, dg_ref, dgq_ref, dgkv_ref):
        i = pl.program_id(0)

        @pl.when(i == 0)
        def _():
            for r in (dsh_ref, dsc_ref, dg_ref, dgq_ref, dgkv_ref):
                r[...] = jnp.zeros(r.shape, F32)

        cos, sin = cos_ref[...], sin_ref[...]
        d_krot = jnp.zeros((tm, HEAD_PAD), F32)
        for h in range(HEADS):
            blk = slice(h * HEAD_PAD, (h + 1) * HEAD_PAD)
            dqp_ref[:, blk] = _rope_transposed(dq_ref[:, blk], cos, sin).astype(MXU)
            dk_h = dk_ref[:, blk]
            d_krot = d_krot + dk_h
            dkvp_ref[:, blk] = dk_h.astype(MXU)
        dkvp_ref[:, pl.ds(hp, hp)] = dv_ref[...].astype(MXU)
        lane = lax.broadcasted_iota(jnp.int32, (tm, HEAD_PAD), 1)
        d_kr = jnp.where((lane >= NOPE) & (lane < NOPE + ROPE), _rope_transposed(d_krot, cos, sin), 0.0)
        d_cqn = _dot(dqp_ref[...], wq_ref[...])
        d_ckvn = _dot_nt(dkvp_ref[...], wkv_ref[...])
        zt = z_ref[...]
        gq, gkv = gq_ref[...], gkv_ref[...]
        cq_hat, rq = _rms(zt[:, :Q_LORA])
        ckv_hat, rkv = _rms(zt[:, Q_LORA:Q_LORA + KV_LORA])
        dgq_ref[...] += jnp.sum(d_cqn * cq_hat, axis=0, keepdims=True)
        dgkv_ref[...] += jnp.sum(d_ckvn * ckv_hat, axis=0, keepdims=True)
        d_cq = _rms_bwd(d_cqn * gq, cq_hat, rq)
        d_ckv = _rms_bwd(d_ckvn * gkv, ckv_hat, rkv)
        d_za = jnp.concatenate([d_cq, d_ckv, d_kr], axis=1).astype(MXU)
        dza_ref[...] = d_za
        d_h1 = _dot(d_za, w1_ref[pl.ds(0, za), :]) + _dot(dguv_ref[...], w1_ref[pl.ds(za, 2 * GM_WIDTH), :])
        sc1 = _row(ar_ref, 1) + _row(ab_ref, 1)
        g = g_ref[...]
        xn, r1 = _rms(x_ref[...])
        dsh_ref[...] += jnp.sum(d_h1, axis=0, keepdims=True)
        dsc_ref[...] += jnp.sum(d_h1 * (xn * g), axis=0, keepdims=True)
        d_mod = d_h1 * (1.0 + sc1)
        dg_ref[...] += jnp.sum(d_mod * xn, axis=0, keepdims=True)
        gx_ref[...] = dxp_ref[...] + _rms_bwd(d_mod * g, xn, r1)

    vec = pl.BlockSpec((1, D_MODEL), lambda i: (0, 0))
    return _call(
        body, name="mix_in_bwd", grid=(s // tm,), sem=("arbitrary",),
        in_specs=[_rows(tm, hp), _rows(tm, hp), _rows(tm, hp), _rows(tm, za), _rows(tm, 2 * GM_WIDTH),
                  _rows(tm, D_MODEL), _rows(tm, D_MODEL), _full(ada_raw.shape), _full(ada_b.shape), _full(g_pre.shape),
                  _full(g_q.shape), _full(g_kv.shape), _full(w1t.shape), _full(wqt.shape),
                  _full(wkv.shape), _rows(tm, HEAD_PAD), _rows(tm, HEAD_PAD)],
        out_specs=[_rows(tm, D_MODEL), _rows(tm, za), _rows(tm, hp), _rows(tm, 2 * hp), vec, vec, vec,
                   _full(g_q.shape), _full(g_kv.shape)],
        out_shape=[_sds((s, D_MODEL), F32), _sds((s, za), MXU), _sds((s, hp), MXU), _sds((s, 2 * hp), MXU),
                   _sds((1, D_MODEL), F32), _sds((1, D_MODEL), F32), _sds((1, D_MODEL), F32),
                   _sds(g_q.shape, F32), _sds(g_kv.shape, F32)],
    )(dq, dk, dv, z, d_guv, x, d_x_part, ada_raw, ada_b, g_pre, g_q, g_kv, w1t, wqt, wkv, cos_t, sin_t)


def _tn_matmuls(arrays, pairs, name, ts):
    s = arrays[0].shape[0]
    steps = s // ts
    n_in, n_out = len(arrays), len(pairs)
    shapes = [(arrays[ia].shape[1], arrays[ib].shape[1]) for ia, ib in pairs]

    def body(*refs):
        ins, outs, accs = refs[:n_in], refs[n_in:n_in + n_out], refs[n_in + n_out:]
        k = pl.program_id(0)

        @pl.when(k == 0)
        def _():
            for acc in accs:
                acc[...] = jnp.zeros(acc.shape, F32)

        for (ia, ib), acc in zip(pairs, accs):
            acc[...] += _dot_tn(ins[ia][...], ins[ib][...])

        @pl.when(k == steps - 1)
        def _():
            for out, acc in zip(outs, accs):
                out[...] = acc[...].astype(MXU)

    return _call(
        body, name=name, grid=(steps,), sem=("arbitrary",),
        in_specs=[_rows(ts, a.shape[1]) for a in arrays],
        out_specs=[_full(shape) for shape in shapes],
        out_shape=[_sds(shape, MXU) for shape in shapes],
        scratch=[pltpu.VMEM(shape, F32) for shape in shapes],
    )(*arrays)


def _adamw(w, g, m, v):
    m2 = ADAM_B1 * m + (1.0 - ADAM_B1) * g
    v2 = ADAM_B2 * v + (1.0 - ADAM_B2) * (g * g)
    m_hat = m2 / (1.0 - ADAM_B1 ** ADAM_STEP)
    v_hat = v2 / (1.0 - ADAM_B2 ** ADAM_STEP)
    delta = -ADAM_LR * (m_hat / (jnp.sqrt(v_hat) + ADAM_EPS) + ADAM_WD * w)
    return delta, m2, v2


def _adam_reduce(recv, w, m, v, name):
    r, c = w.shape
    tr = r if r <= 512 else max(t for t in range(16, 513, 16) if r % t == 0)

    def body(p_ref, w_ref, m_ref, v_ref, g_ref, d_ref, mo_ref, vo_ref):
        g = p_ref[0].astype(F32)
        for j in range(1, N_DEV):
            g = g + p_ref[j].astype(F32)
        g_ref[...] = g
        d_ref[...], mo_ref[...], vo_ref[...] = _adamw(w_ref[...], g, m_ref[...], v_ref[...])

    blk = pl.BlockSpec((tr, c), lambda i: (i, 0))
    return _call(
        body, name=name, grid=(r // tr,), sem=("parallel",),
        in_specs=[pl.BlockSpec((N_DEV, tr, c), lambda i: (0, i, 0)), blk, blk, blk],
        out_specs=[blk] * 4, out_shape=[_sds((r, c), F32)] * 4,
    )(recv, w, m, v)


def _adam_w_ada(c_act_t, d_ada_cols, w, m, v):
    r, c = w.shape
    tr = 256

    def body(ct_ref, da_ref, w_ref, m_ref, v_ref, g_ref, d_ref, mo_ref, vo_ref):
        g = _dot(ct_ref[...], da_ref[...])
        g_ref[...] = g
        d_ref[...], mo_ref[...], vo_ref[...] = _adamw(w_ref[...], g, m_ref[...], v_ref[...])

    blk = pl.BlockSpec((tr, c), lambda i: (i, 0))
    return _call(
        body, name="adam_w_ada", grid=(r // tr,), sem=("parallel",),
        in_specs=[pl.BlockSpec((tr, c_act_t.shape[1]), lambda i: (i, 0)), _full(d_ada_cols.shape), blk, blk, blk],
        out_specs=[blk] * 4, out_shape=[_sds((r, c), F32)] * 4,
    )(c_act_t, d_ada_cols, w, m, v)


VEC_ROWS = D_MODEL // 128
PK_ADA = 0
PK_GAIN = PK_ADA + 6 * VEC_ROWS
PK_GQ = PK_GAIN + 4 * VEC_ROWS
PK_GKV = PK_GQ + Q_LORA // 128
PK_LOSS = PK_GKV + KV_LORA // 128
PK_LNG = 88
PK_LNB = PK_LNG + HEADS
PK_BS = PK_LNB + HEADS
PK_CB = PK_BS + HEADS
CB_ROWS = 6
PK_WS = PK_CB + N_DEV * CB_ROWS
PK_ROWS = PK_WS + HEADS * GM_CHUNK
assert PK_LOSS < PK_LNG and PK_ROWS % 8 == 0
LATE_GAIN = 2 * VEC_ROWS
LATE_GQ = 3 * VEC_ROWS
LATE_GKV = LATE_GQ + Q_LORA // 128
LATE_ROWS = 32


def _cb_chunks():
    return [(k, k * 128, min(128, FF_BLK - k * 128)) for k in range(CB_ROWS)]


def _put_rows(out_ref, row0, ref, width):
    for k in range(width // 128):
        out_ref[pl.ds(row0 + k, 1), :] = ref[:, pl.ds(k * 128, 128)]


def _pack_small(ada_rows, gains, loss_part, d_ln_g, d_ln_b, d_bs, d_cb, d_ws):
    half = N_DEV // 2

    def body(*refs):
        vec_refs = refs[:7]
        loss_ref, lng_ref, lnb_ref, bs_ref, cb_ref, ws_ref, out_ref = refs[7:]
        out_ref[pl.ds(0, PK_WS), :] = jnp.zeros((PK_WS, 128), F32)
        for n, ref in enumerate(vec_refs[:4]):
            _put_rows(out_ref, PK_ADA + (2 + n) * VEC_ROWS, ref, D_MODEL)
        for n, ref in enumerate(vec_refs[4:]):
            _put_rows(out_ref, PK_GAIN + (1 + n) * VEC_ROWS, ref, D_MODEL)
        _put_rows(out_ref, PK_LOSS, loss_ref, 128)
        out_ref[pl.ds(PK_LNG, HEADS), pl.ds(0, GM_DIM)] = lng_ref[...]
        out_ref[pl.ds(PK_LNB, HEADS), pl.ds(0, GM_DIM)] = lnb_ref[...]
        out_ref[pl.ds(PK_BS, HEADS), :] = bs_ref[...]
        for j in range(N_DEV):
            for k, lane, width in _cb_chunks():
                out_ref[pl.ds(PK_CB + j * CB_ROWS + k, 1), pl.ds(0, width)] = cb_ref[j % half, j // half, :, pl.ds(lane, width)]
        for h in range(HEADS):
            out_ref[pl.ds(PK_WS + h * GM_CHUNK, GM_CHUNK), :] = ws_ref[h]

    ins = list(ada_rows) + list(gains) + [loss_part, d_ln_g, d_ln_b, d_bs, d_cb, d_ws]
    return _call(body, name="pack_small", grid=(1,), in_specs=[_full(a.shape) for a in ins],
                 out_specs=_full((PK_ROWS, 128)), out_shape=_sds((PK_ROWS, 128), F32))(*ins)


def _pack_late(d_sh1, d_sc1, d_g_pre_mix, d_g_q, d_g_kv):
    def body(sh_ref, sc_ref, g_ref, gq_ref, gkv_ref, out_ref):
        out_ref[...] = jnp.zeros((LATE_ROWS, 128), F32)
        _put_rows(out_ref, 0, sh_ref, D_MODEL)
        _put_rows(out_ref, VEC_ROWS, sc_ref, D_MODEL)
        _put_rows(out_ref, LATE_GAIN, g_ref, D_MODEL)
        _put_rows(out_ref, LATE_GQ, gq_ref, Q_LORA)
        _put_rows(out_ref, LATE_GKV, gkv_ref, KV_LORA)

    ins = [d_sh1, d_sc1, d_g_pre_mix, d_g_q, d_g_kv]
    return _call(body, name="pack_late", grid=(1,), in_specs=[_full(a.shape) for a in ins],
                 out_specs=_full((LATE_ROWS, 128)), out_shape=_sds((LATE_ROWS, 128), F32))(*ins)


def _adam_small(gathered, late, params):
    n_par = len(params)

    def body(p_ref, late_ref, *refs):
        ins = [refs[3 * n:3 * n + 3] for n in range(n_par)]
        outs = [refs[3 * n_par + 4 * n:3 * n_par + 4 * n + 4] for n in range(n_par)]
        loss_ref, dada_ref = refs[7 * n_par:]

        def total(rows, lanes=slice(None), src=p_ref):
            g = src[0, rows, lanes]
            for j in range(1, N_DEV):
                g = g + src[j, rows, lanes]
            return g

        def apply(n, g, idx):
            w_ref, m_ref, v_ref = ins[n]
            d, m2, v2 = _adamw(w_ref[idx], g, m_ref[idx], v_ref[idx])
            for ref, val in zip(outs[n], (g, d, m2, v2)):
                ref[idx] = val

        def vector(n, src, row0, width, lane0=0):
            for k in range(width // 128):
                apply(n, total(pl.ds(row0 + k, 1), src=src), (slice(None), pl.ds(lane0 + k * 128, 128)))

        vector(0, late_ref, 0, 2 * D_MODEL)
        vector(0, p_ref, PK_ADA + 2 * VEC_ROWS, 4 * D_MODEL, lane0=2 * D_MODEL)
        vector(1, late_ref, LATE_GAIN, D_MODEL)
        for n in range(1, 4):
            vector(1 + n, p_ref, PK_GAIN + n * VEC_ROWS, D_MODEL)
        vector(5, late_ref, LATE_GQ, Q_LORA)
        vector(6, late_ref, LATE_GKV, KV_LORA)
        apply(7, total(pl.ds(PK_LNG, HEADS), pl.ds(0, GM_DIM)), (0,))
        apply(8, total(pl.ds(PK_LNB, HEADS), pl.ds(0, GM_DIM)), (0,))
        for h in range(HEADS):
            apply(9, total(pl.ds(PK_WS + h * GM_CHUNK, GM_CHUNK)), (0, h))
        apply(10, total(pl.ds(PK_BS, HEADS)), (0,))
        for j in range(N_DEV):
            for k, lane, width in _cb_chunks():
                apply(11, total(pl.ds(PK_CB + j * CB_ROWS + k, 1), pl.ds(0, width)), (pl.ds(j, 1), pl.ds(lane, width)))
        loss_ref[...] = total(pl.ds(PK_LOSS, 1))
        dada_ref[:, pl.ds(0, 2 * VEC_ROWS), :] = late_ref[:, pl.ds(0, 2 * VEC_ROWS), :]
        dada_ref[:, pl.ds(2 * VEC_ROWS, 4 * VEC_ROWS), :] = p_ref[:, pl.ds(PK_ADA + 2 * VEC_ROWS, 4 * VEC_ROWS), :]

    flat = [a for triple in params for a in triple]
    out_shape = [_sds(w.shape, F32) for w, _, _ in params for _ in range(4)]
    out_shape += [_sds((1, 128), F32), _sds((N_DEV, 6 * VEC_ROWS, 128), F32)]
    outs = _call(body, name="adam_small", grid=(1,),
                 in_specs=[_full(gathered.shape), _full(late.shape)] + [_full(a.shape) for a in flat],
                 out_specs=[_full(o.shape) for o in out_shape], out_shape=out_shape)(gathered, late, *flat)
    return [tuple(outs[4 * n:4 * n + 4]) for n in range(n_par)], outs[-2], outs[-1]


def _rope_tables(s):
    pos = jnp.arange(s, dtype=F32)
    inv = ROPE_THETA ** (-jnp.arange(0, ROPE, 2, dtype=F32) / ROPE)
    lane_inv = jnp.concatenate([jnp.zeros((NOPE,), F32), inv, inv, jnp.zeros((HEAD_PAD - NOPE - ROPE,), F32)])
    ang = pos[:, None] * lane_inv[None, :]
    return jnp.cos(ang), jnp.sin(ang)


def kernel(x, c, w_ada, b_ada, g_pre_mix, g_post_mix, w_in, g_q, w_uq, g_kv, w_ukv, gm_ln_g, gm_ln_b, w_spatial, b_spatial, w_out, g_pre_ffn, g_post_ffn, w_up, conv_w, conv_b, w_down, loss_target, m_w_ada, m_b_ada, m_g_pre_mix, m_g_post_mix, m_w_in, m_g_q, m_w_uq, m_g_kv, m_w_ukv, m_gm_ln_g, m_gm_ln_b, m_w_spatial, m_b_spatial, m_w_out, m_g_pre_ffn, m_g_post_ffn, m_w_up, m_conv_w, m_conv_b, m_w_down, v_w_ada, v_b_ada, v_g_pre_mix, v_g_post_mix, v_w_in, v_g_q, v_w_uq, v_g_kv, v_w_ukv, v_gm_ln_g, v_gm_ln_b, v_w_spatial, v_b_spatial, v_w_out, v_g_pre_ffn, v_g_post_ffn, v_w_up, v_conv_w, v_conv_b, v_w_down):
    s = x.shape[1]
    tm = min(512, s)
    tf = min(2 * ROW_SUB, s)
    tq = min(512, s)
    ts = min(2048, s)
    hp = HEADS * HEAD_PAD
    half = N_DEV // 2
    my_slot = 4 * lax.axis_index("x") + 2 * lax.axis_index("y") + lax.axis_index("c")
    x2d, target = x[0], loss_target[0]

    def t_(a):
        return jnp.swapaxes(a[0], 0, 1)

    w_in_t, m_in_t, v_in_t = t_(w_in), t_(m_w_in), t_(v_w_in)
    w_uq_t, m_uq_t, v_uq_t = t_(w_uq), t_(m_w_uq), t_(v_w_uq)
    w_up_t, m_up_t, v_up_t = t_(w_up), t_(m_w_up), t_(v_w_up)
    (g_c, g_in_t, g_uq_t, g_ukv, g_cw), _ = _exchange(
        [c, w_in_t.astype(MXU), w_uq_t.astype(MXU), w_ukv[0].astype(MXU), conv_w[0]], [], "gather_mixer_weights")

    w_in_f = g_in_t.reshape(-1, D_MODEL)
    o1, o2, o3 = Q_LORA, Q_LORA + KV_LORA, Q_LORA + KV_LORA + ROPE
    w1t = jnp.concatenate([w_in_f[:o2], jnp.zeros((NOPE, D_MODEL), MXU), w_in_f[o2:o3],
                           jnp.zeros((HEAD_PAD - NOPE - ROPE, D_MODEL), MXU), w_in_f[o3:]], axis=0)
    wqt = jnp.pad(g_uq_t, ((0, 0), (0, HEAD_PAD - NOPE - ROPE), (0, 0))).reshape(hp, Q_LORA)
    w_ukv_f = jnp.transpose(g_ukv, (1, 0, 2)).reshape(KV_LORA, HEADS, 2 * NOPE)
    pad_head = ((0, 0), (0, 0), (0, HEAD_PAD - NOPE))
    wkv = jnp.concatenate([jnp.pad(w_ukv_f[:, :, :NOPE], pad_head).reshape(KV_LORA, hp),
                           jnp.pad(w_ukv_f[:, :, NOPE:], pad_head).reshape(KV_LORA, hp)], axis=1)
    cb8 = conv_b.reshape(N_DEV, 1, FF_BLK)
    bias_exp = jnp.repeat(b_spatial[0].T, GM_DIM, axis=1)
    ln_g, ln_b = gm_ln_g.reshape(1, GM_WIDTH), gm_ln_b.reshape(1, GM_WIDTH)
    w_sp = w_spatial[0]
    cos_t, sin_t = _rope_tables(s)

    ada_part, c_act = _ada_fwd(g_c.reshape(N_DEV, D_MODEL), w_ada[0])
    _, (ada_recv,) = _exchange([], [(ada_part.reshape(N_DEV, 1, -1), _plain_slot)], "ada_rows")
    ada_raw = ada_recv.reshape(6, D_MODEL)
    ada_b = b_ada.reshape(6, D_MODEL)

    h1, z, qp, kp, vp, cqn, ckvn = _mix_in_fwd(x2d, ada_raw, ada_b, g_pre_mix, w1t, g_q, g_kv, wqt, wkv, cos_t, sin_t, tm)
    sgu = _gmlp_fwd(z, ln_g, ln_b, w_sp, bias_exp, tm)
    o_pad, lse, (g_out, g_up, g_down) = _attn_fwd(
        qp, kp, vp, tq, [w_out[0].astype(MXU), w_up_t.astype(MXU), w_down[0].astype(MXU)])
    w_out_f = g_out.reshape(2 * GM_WIDTH, D_MODEL)
    wo_attn = jnp.pad(w_out_f[:GM_WIDTH].reshape(HEADS, NOPE, D_MODEL), ((0, 0), (0, HEAD_PAD - NOPE), (0, 0)))
    wo = jnp.concatenate([wo_attn.reshape(hp, D_MODEL), w_out_f[GM_WIDTH:]], axis=0)
    wd = g_down.reshape(half, FF_BLK, D_MODEL)
    m_mix, x2, h2 = _out_proj_fwd(o_pad, sgu, wo, x2d, ada_raw, ada_b, g_post_mix, g_pre_ffn, tm)
    up_a, up_b, y_a, y_b, act = _ffn_up_fwd(h2, g_up, g_cw, cb8, tf)
    d_out, d_f, loss_part, d_gt2, d_g_post_ffn = _ffn_down_fwd(act, wd, x2, target, ada_raw, ada_b, g_post_ffn, tf)

    d_up, d_cw, d_cb, p_down, p_up = _ffn_down_bwd(d_f, wd, up_a, up_b, y_a, y_b, g_cw, act, h2, tf)
    p_down = p_down.reshape(N_DEV, -1, D_MODEL)
    d_x2, d_m, d_sh2, d_sc2, d_g_pre_ffn, d_gt1, d_g_post_mix = _ffn_up_bwd(
        d_up, g_up, x2, m_mix, d_out, ada_raw, ada_b, g_pre_ffn, g_post_mix, tm)
    dwo_attn, dwo_sgu = _tn_matmuls([o_pad, sgu, d_m], [(0, 2), (1, 2)], "dw_out", ts)
    dwo_attn = dwo_attn.reshape(HEADS, HEAD_PAD, D_MODEL)[:, :NOPE]
    p_out = jnp.concatenate([dwo_attn.reshape(GM_WIDTH, D_MODEL), dwo_sgu], axis=0).reshape(N_DEV, -1, D_MODEL)
    d_o, d_sgu, delta = _out_proj_bwd(d_m, wo, o_pad, tm)
    d_guv, d_ws, d_bs, d_ln_g, d_ln_b = _gmlp_bwd(z, d_sgu, ln_g, ln_b, w_sp, bias_exp, tm)
    packed = _pack_small([d_gt1, d_sh2, d_sc2, d_gt2], [d_g_post_mix, d_g_pre_ffn, d_g_post_ffn], loss_part,
                         d_ln_g, d_ln_b, d_bs, d_cb, d_ws)

    def ffn_slot(j):
        return (j % half, j // half)

    dq, dk, dv, (r_out, r_up, r_down, r_cw), (g_small,) = _attn_bwd(
        qp, kp, vp, d_o, lse, delta, tq,
        [(p_out, _plain_slot), (p_up, ffn_slot), (p_down, _plain_slot), (d_cw, ffn_slot)], [packed])
    grad_x, d_za, d_qp, d_kvp, d_sh1, d_sc1, d_g_pre_mix, d_g_q, d_g_kv = _mix_in_bwd(
        dq, dk, dv, z, d_guv, x2d, d_x2, ada_raw, ada_b, g_pre_mix, g_q, g_kv, w1t, wqt, wkv, cos_t, sin_t,
        min(256, s))
    dw1a, dw1b, dwq, dwkv = _tn_matmuls([d_za, d_guv, h1, d_qp, cqn, ckvn, d_kvp],
                                        [(0, 2), (1, 2), (3, 4), (5, 6)], "dw_mixer", ts // 2)
    d_w_in_t = jnp.concatenate([dw1a[:o2], dw1a[o2 + NOPE:o2 + NOPE + ROPE], dw1b], axis=0)
    p_in = d_w_in_t.reshape(N_DEV, -1, D_MODEL)
    p_uq = dwq.reshape(HEADS, HEAD_PAD, Q_LORA)[:, :NOPE + ROPE]
    dwk = dwkv[:, :hp].reshape(KV_LORA, HEADS, HEAD_PAD)[:, :, :NOPE]
    dwv = dwkv[:, hp:].reshape(KV_LORA, HEADS, HEAD_PAD)[:, :, :NOPE]
    p_ukv = jnp.transpose(jnp.concatenate([dwk, dwv], axis=2), (1, 0, 2))

    (g_late,), (r_in, r_uq, r_ukv) = _exchange(
        [_pack_late(d_sh1, d_sc1, d_g_pre_mix, d_g_q, d_g_kv)],
        [(p_in, _plain_slot), (p_uq, _plain_slot), (p_ukv, _plain_slot)], "final_exchange")
    small_params = [(b_ada, m_b_ada, v_b_ada), (g_pre_mix, m_g_pre_mix, v_g_pre_mix),
                    (g_post_mix, m_g_post_mix, v_g_post_mix), (g_pre_ffn, m_g_pre_ffn, v_g_pre_ffn),
                    (g_post_ffn, m_g_post_ffn, v_g_post_ffn), (g_q, m_g_q, v_g_q), (g_kv, m_g_kv, v_g_kv),
                    (gm_ln_g, m_gm_ln_g, v_gm_ln_g), (gm_ln_b, m_gm_ln_b, v_gm_ln_b),
                    (w_spatial, m_w_spatial, v_w_spatial), (b_spatial, m_b_spatial, v_b_spatial),
                    tuple(a.reshape(N_DEV, FF_BLK) for a in (conv_b, m_conv_b, v_conv_b))]
    small_out, loss_row, d_ada_all = _adam_small(g_small, g_late, small_params)
    small_out[11] = tuple(o.reshape(conv_b.shape) for o in small_out[11])
    loss = loss_row[0, 0]

    def big(recv, w, m, v, name):
        g, d, m2, v2 = _adam_reduce(recv, w[0], m[0], v[0], name)
        return g[None], d[None], m2[None], v2[None]

    def big_t(recv, w_t, m_t, v_t, name):
        return tuple(jnp.swapaxes(o, 0, 1)[None] for o in _adam_reduce(recv, w_t, m_t, v_t, name))

    a_in = big_t(r_in, w_in_t, m_in_t, v_in_t, "adam_w_in")
    a_uq = big_t(r_uq, w_uq_t, m_uq_t, v_uq_t, "adam_w_uq")
    a_ukv = big(r_ukv, w_ukv, m_w_ukv, v_w_ukv, "adam_w_ukv")
    a_out = big(r_out, w_out, m_w_out, v_w_out, "adam_w_out")
    a_up = big_t(r_up, w_up_t, m_up_t, v_up_t, "adam_w_up")
    a_down = big(r_down, w_down, m_w_down, v_w_down, "adam_w_down")
    ada_cols = w_ada.shape[2]
    d_ada_cols = lax.dynamic_slice(d_ada_all.reshape(N_DEV, 6 * D_MODEL), (0, my_slot * ada_cols), (N_DEV, ada_cols))
    pad_seq = 128 - N_DEV
    a_ada = tuple(t[None] for t in _adam_w_ada(jnp.pad(c_act.T, ((0, 0), (0, pad_seq))).astype(MXU),
                                               jnp.pad(d_ada_cols, ((0, pad_seq), (0, 0))).astype(MXU),
                                               w_ada[0], m_w_ada[0], v_w_ada[0]))
    a_cw = big(r_cw, conv_w, m_conv_w, v_conv_w, "adam_conv_w")

    def small(k):
        return small_out[k]

    per_weight = [a_ada, small(0), small(1), small(2), a_in, small(5), a_uq, small(6), a_ukv, small(7), small(8),
                  small(9), small(10), a_out, small(3), small(4), a_up, a_cw, small(11), a_down]
    outs = [loss, grad_x[None]]
    for k in range(4):
        outs += [t[k] for t in per_weight]
    return tuple(outs)
```
